```python
import math
import jax, jax.numpy as jnp
from jax import lax
import numpy as np

D_MODEL = 1024
BATCH = 8
SEQ = 8192
DEPTH = 2

CHUNK = 64
N_LEFT_CHUNKS = 8
BAND = (N_LEFT_CHUNKS + 1) * CHUNK
HEAD_DIM = 64
N_HEADS_A = 8
N_HEADS_B = 8
WIDTH_A = N_HEADS_A * HEAD_DIM
WIDTH_B = N_HEADS_B * HEAD_DIM
POOL_WINDOWS = (2, 4, 8, 16)
N_POOL_GROUPS = len(POOL_WINDOWS)
POOL_GROUP_DIM = D_MODEL // 8
WIDTH_C = N_POOL_GROUPS * POOL_GROUP_DIM
N_BRANCHES = 3
MAX_REL_DIST = 2 * CHUNK
REL_TABLE = MAX_REL_DIST + CHUNK
D_FF = 2816
CONV_WIDTH = 3
Q_BLOCK = 128
EPS = 1e-6

IN_SIZES = (WIDTH_A, WIDTH_A, WIDTH_A, WIDTH_B, WIDTH_B, WIDTH_B, WIDTH_C,
            N_BRANCHES * D_MODEL)
IN_COLS = sum(IN_SIZES)
IN_SPLITS = [int(v) for v in np.cumsum(IN_SIZES)[:-1]]

kernel_name = "hybrid_chunk_causal_gated_branches"


def rms_norm(x, gain):
    xf = x.astype(jnp.float32)
    y = xf * lax.rsqrt(jnp.mean(xf * xf, axis=-1, keepdims=True) + EPS)
    return (y * gain.astype(jnp.float32)).astype(x.dtype)


def chunked_rel_attention(q, k, v, g_q, g_k, rel_bias):
    B, S, H, Dh = q.shape
    nc = S // CHUNK
    q = rms_norm(q, g_q)
    k = rms_norm(k, g_k)
    qc = q.reshape(B, nc, CHUNK, H, Dh)
    pad = ((0, 0), (N_LEFT_CHUNKS, 0), (0, 0), (0, 0), (0, 0))
    kp = jnp.pad(k.reshape(B, nc, CHUNK, H, Dh), pad)
    vp = jnp.pad(v.reshape(B, nc, CHUNK, H, Dh), pad)
    k_band = jnp.concatenate([kp[:, j:j + nc] for j in range(N_LEFT_CHUNKS + 1)], axis=2)
    v_band = jnp.concatenate([vp[:, j:j + nc] for j in range(N_LEFT_CHUNKS + 1)], axis=2)
    logits = jnp.einsum('bcqhd,bckhd->bhcqk', qc, k_band).astype(jnp.float32) / math.sqrt(Dh)
    q_off = jnp.arange(CHUNK)[:, None] + N_LEFT_CHUNKS * CHUNK
    k_off = jnp.arange(BAND)[None, :]
    rel = jnp.clip(q_off - k_off, -(CHUNK - 1), MAX_REL_DIST) + (CHUNK - 1)
    bias = rel_bias.astype(jnp.float32)[:, rel]
    valid = (jnp.arange(nc)[:, None] + k_off // CHUNK - N_LEFT_CHUNKS) >= 0
    logits = jnp.where(valid[None, None, :, None, :], logits + bias[None, :, None], -1e30)
    p = jax.nn.softmax(logits, axis=-1)
    out = jnp.einsum('bhcqk,bckhd->bcqhd', p.astype(v.dtype), v_band)
    return out.reshape(B, S, H * Dh)


def stick_breaking_attention(q, k, v):
    B, S, H, Dh = q.shape
    nb = S // Q_BLOCK
    scale = 1.0 / math.sqrt(Dh)
    qb = q.reshape(B, nb, Q_BLOCK, H, Dh).transpose(1, 0, 2, 3, 4)
    k_pos = jnp.arange(S)
    starts = jnp.arange(nb, dtype=jnp.int32) * Q_BLOCK

    def block(args):
        q_blk, start = args
        z = jnp.einsum('bqhd,bkhd->bhqk', q_blk, k).astype(jnp.float32) * scale
        q_pos = start + jnp.arange(Q_BLOCK)
        before = k_pos[None, :] < q_pos[:, None]
        log_keep = jnp.where(before, jax.nn.log_sigmoid(-z), 0.0)
        tail = lax.cumsum(log_keep, axis=3, reverse=True) - log_keep
        w = jnp.where(before, jnp.exp(jax.nn.log_sigmoid(z) + tail), 0.0)
        return jnp.einsum('bhqk,bkhd->bqhd', w.astype(v.dtype), v)

    out = lax.map(block, (qb, starts))
    return out.transpose(1, 0, 2, 3, 4).reshape(B, S, H * Dh)


def multiscale_pool(u, w_group, scale):
    B, S, _ = u.shape
    uf = u.astype(jnp.float32).reshape(B, S, N_POOL_GROUPS, POOL_GROUP_DIM)
    cs = jnp.pad(jnp.cumsum(uf, axis=1), ((0, 0), (1, 0), (0, 0), (0, 0)))
    t = jnp.arange(S)
    pooled = []
    for g, win in enumerate(POOL_WINDOWS):
        lo = jnp.maximum(t + 1 - win, 0)
        win_sum = cs[:, 1:, g] - cs[:, lo, g]
        count = (t + 1 - lo).astype(jnp.float32)
        pooled.append(win_sum / count[None, :, None])
    pooled = jnp.stack(pooled, axis=2) - uf
    mixed = jnp.einsum('bsgc,gce->bsge', pooled.astype(u.dtype), w_group)
    return mixed.reshape(B, S, WIDTH_C) * scale


def conv_gated_mlp(h, w_up, conv_w, conv_b, w_down):
    S = h.shape[1]
    u = h @ w_up
    up = jnp.pad(u, ((0, 0), (CONV_WIDTH - 1, 0), (0, 0)))
    c = conv_b + conv_w[0] * up[:, 0:S]
    for j in range(1, CONV_WIDTH):
        c = c + conv_w[j] * up[:, j:j + S]
    gate, val = jnp.split(c, 2, axis=-1)
    return (jax.nn.silu(gate) * val) @ w_down


def _fwd_setup_inputs(seed: int = 0) -> dict:
    key = jax.random.key(seed)
    ks = jax.random.split(key, 20)
    f32 = jnp.float32
    n = lambda k, shape, s: jax.random.normal(k, shape, f32) * s
    L = DEPTH
    return {
        "x": n(ks[0], (BATCH, SEQ, D_MODEL), 1.0),
        "norm_mix": 1.0 + n(ks[1], (L, D_MODEL), 0.05),
        "w_in": n(ks[2], (L, D_MODEL, IN_COLS), D_MODEL ** -0.5),
        "b_gate": n(ks[3], (L, N_BRANCHES * D_MODEL), 0.1),
        "q_norm_a": 1.0 + n(ks[4], (L, HEAD_DIM), 0.05),
        "k_norm_a": 1.0 + n(ks[5], (L, HEAD_DIM), 0.05),
        "rel_bias_a": n(ks[6], (L, N_HEADS_A, REL_TABLE), 0.5),
        "w_pool": n(ks[7], (L, N_POOL_GROUPS, POOL_GROUP_DIM, POOL_GROUP_DIM), POOL_GROUP_DIM ** -0.5),
        "pool_scale": 1.0 + n(ks[8], (L, WIDTH_C), 0.1),
        "w_branch_a": n(ks[9], (L, WIDTH_A, D_MODEL), WIDTH_A ** -0.5),
        "w_branch_b": n(ks[10], (L, WIDTH_B, D_MODEL), WIDTH_B ** -0.5),
        "w_branch_c": n(ks[11], (L, WIDTH_C, D_MODEL), WIDTH_C ** -0.5),
        "w_out": n(ks[12], (L, D_MODEL, D_MODEL), D_MODEL ** -0.5),
        "norm_ffn": 1.0 + n(ks[13], (L, D_MODEL), 0.05),
        "w_up": n(ks[14], (L, D_MODEL, 2 * D_FF), D_MODEL ** -0.5),
        "conv_w": n(ks[15], (L, CONV_WIDTH, 2 * D_FF), CONV_WIDTH ** -0.5),
        "conv_b": n(ks[16], (L, 2 * D_FF), 0.02),
        "w_down": n(ks[17], (L, D_FF, D_MODEL), D_FF ** -0.5),
    }


def _fwd_reference(x, norm_mix, w_in, b_gate, q_norm_a, k_norm_a, rel_bias_a, w_pool,
              pool_scale, w_branch_a, w_branch_b, w_branch_c, w_out, norm_ffn,
              w_up, conv_w, conv_b, w_down):
    B, S, D = x.shape
    for l in range(DEPTH):
        h = rms_norm(x, norm_mix[l])
        proj = h @ w_in[l]
        q_a, k_a, v_a, q_b, k_b, v_b, u_c, g_logits = jnp.split(proj, IN_SPLITS, axis=-1)
        heads_a = lambda t: t.reshape(B, S, N_HEADS_A, HEAD_DIM)
        heads_b = lambda t: t.reshape(B, S, N_HEADS_B, HEAD_DIM)
        o_a = chunked_rel_attention(heads_a(q_a), heads_a(k_a), heads_a(v_a),
                                    q_norm_a[l], k_norm_a[l], rel_bias_a[l])
        o_b = stick_breaking_attention(heads_b(q_b), heads_b(k_b), heads_b(v_b))
        o_c = multiscale_pool(u_c, w_pool[l], pool_scale[l])
        gates = jax.nn.sigmoid((g_logits + b_gate[l]).astype(jnp.float32)).astype(x.dtype)
        gates = gates.reshape(B, S, N_BRANCHES, D)
        merged = (gates[:, :, 0] * (o_a @ w_branch_a[l])
                  + gates[:, :, 1] * (o_b @ w_branch_b[l])
                  + gates[:, :, 2] * (o_c @ w_branch_c[l]))
        x = x + merged @ w_out[l]
        h2 = rms_norm(x, norm_ffn[l])
        x = x + conv_gated_mlp(h2, w_up[l], conv_w[l], conv_b[l], w_down[l])
    return x


import jax as _jax
import jax.numpy as _jnp

TWIN_FORMAT = 'train_step'
FWD_PARAMS = ['x', 'norm_mix', 'w_in', 'b_gate', 'q_norm_a', 'k_norm_a', 'rel_bias_a', 'w_pool', 'pool_scale', 'w_branch_a', 'w_branch_b', 'w_branch_c', 'w_out', 'norm_ffn', 'w_up', 'conv_w', 'conv_b', 'w_down']
TWIN_WEIGHTS = ['norm_mix', 'w_in', 'b_gate', 'q_norm_a', 'k_norm_a', 'rel_bias_a', 'w_pool', 'pool_scale', 'w_branch_a', 'w_branch_b', 'w_branch_c', 'w_out', 'norm_ffn', 'w_up', 'conv_w', 'conv_b', 'w_down']
TWIN_DIFF_INPUT = 'x'
TWIN_INPUTS = ['x', 'norm_mix', 'w_in', 'b_gate', 'q_norm_a', 'k_norm_a', 'rel_bias_a', 'w_pool', 'pool_scale', 'w_branch_a', 'w_branch_b', 'w_branch_c', 'w_out', 'norm_ffn', 'w_up', 'conv_w', 'conv_b', 'w_down', 'loss_target', 'm_norm_mix', 'm_w_in', 'm_b_gate', 'm_q_norm_a', 'm_k_norm_a', 'm_rel_bias_a', 'm_w_pool', 'm_pool_scale', 'm_w_branch_a', 'm_w_branch_b', 'm_w_branch_c', 'm_w_out', 'm_norm_ffn', 'm_w_up', 'm_conv_w', 'm_conv_b', 'm_w_down', 'v_norm_mix', 'v_w_in', 'v_b_gate', 'v_q_norm_a', 'v_k_norm_a', 'v_rel_bias_a', 'v_w_pool', 'v_pool_scale', 'v_w_branch_a', 'v_w_branch_b', 'v_w_branch_c', 'v_w_out', 'v_norm_ffn', 'v_w_up', 'v_conv_w', 'v_conv_b', 'v_w_down']
TWIN_OUTPUTS = ['loss', 'grad_x', 'grad_norm_mix', 'grad_w_in', 'grad_b_gate', 'grad_q_norm_a', 'grad_k_norm_a', 'grad_rel_bias_a', 'grad_w_pool', 'grad_pool_scale', 'grad_w_branch_a', 'grad_w_branch_b', 'grad_w_branch_c', 'grad_w_out', 'grad_norm_ffn', 'grad_w_up', 'grad_conv_w', 'grad_conv_b', 'grad_w_down', 'delta_norm_mix', 'delta_w_in', 'delta_b_gate', 'delta_q_norm_a', 'delta_k_norm_a', 'delta_rel_bias_a', 'delta_w_pool', 'delta_pool_scale', 'delta_w_branch_a', 'delta_w_branch_b', 'delta_w_branch_c', 'delta_w_out', 'delta_norm_ffn', 'delta_w_up', 'delta_conv_w', 'delta_conv_b', 'delta_w_down', 'new_m_norm_mix', 'new_m_w_in', 'new_m_b_gate', 'new_m_q_norm_a', 'new_m_k_norm_a', 'new_m_rel_bias_a', 'new_m_w_pool', 'new_m_pool_scale', 'new_m_w_branch_a', 'new_m_w_branch_b', 'new_m_w_branch_c', 'new_m_w_out', 'new_m_norm_ffn', 'new_m_w_up', 'new_m_conv_w', 'new_m_conv_b', 'new_m_w_down', 'new_v_norm_mix', 'new_v_w_in', 'new_v_b_gate', 'new_v_q_norm_a', 'new_v_k_norm_a', 'new_v_rel_bias_a', 'new_v_w_pool', 'new_v_pool_scale', 'new_v_w_branch_a', 'new_v_w_branch_b', 'new_v_w_branch_c', 'new_v_w_out', 'new_v_norm_ffn', 'new_v_w_up', 'new_v_conv_w', 'new_v_conv_b', 'new_v_w_down']
TWIN_LEAF_KINDS = {'loss': 'loss', 'grad_x': 'grad_x', 'grad_norm_mix': 'grad_w', 'grad_w_in': 'grad_w', 'grad_b_gate': 'grad_w', 'grad_q_norm_a': 'grad_w', 'grad_k_norm_a': 'grad_w', 'grad_rel_bias_a': 'grad_w', 'grad_w_pool': 'grad_w', 'grad_pool_scale': 'grad_w', 'grad_w_branch_a': 'grad_w', 'grad_w_branch_b': 'grad_w', 'grad_w_branch_c': 'grad_w', 'grad_w_out': 'grad_w', 'grad_norm_ffn': 'grad_w', 'grad_w_up': 'grad_w', 'grad_conv_w': 'grad_w', 'grad_conv_b': 'grad_w', 'grad_w_down': 'grad_w', 'delta_norm_mix': 'delta_w', 'delta_w_in': 'delta_w', 'delta_b_gate': 'delta_w', 'delta_q_norm_a': 'delta_w', 'delta_k_norm_a': 'delta_w', 'delta_rel_bias_a': 'delta_w', 'delta_w_pool': 'delta_w', 'delta_pool_scale': 'delta_w', 'delta_w_branch_a': 'delta_w', 'delta_w_branch_b': 'delta_w', 'delta_w_branch_c': 'delta_w', 'delta_w_out': 'delta_w', 'delta_norm_ffn': 'delta_w', 'delta_w_up': 'delta_w', 'delta_conv_w': 'delta_w', 'delta_conv_b': 'delta_w', 'delta_w_down': 'delta_w', 'new_m_norm_mix': 'new_m', 'new_m_w_in': 'new_m', 'new_m_b_gate': 'new_m', 'new_m_q_norm_a': 'new_m', 'new_m_k_norm_a': 'new_m', 'new_m_rel_bias_a': 'new_m', 'new_m_w_pool': 'new_m', 'new_m_pool_scale': 'new_m', 'new_m_w_branch_a': 'new_m', 'new_m_w_branch_b': 'new_m', 'new_m_w_branch_c': 'new_m', 'new_m_w_out': 'new_m', 'new_m_norm_ffn': 'new_m', 'new_m_w_up': 'new_m', 'new_m_conv_w': 'new_m', 'new_m_conv_b': 'new_m', 'new_m_w_down': 'new_m', 'new_v_norm_mix': 'new_v', 'new_v_w_in': 'new_v', 'new_v_b_gate': 'new_v', 'new_v_q_norm_a': 'new_v', 'new_v_k_norm_a': 'new_v', 'new_v_rel_bias_a': 'new_v', 'new_v_w_pool': 'new_v', 'new_v_pool_scale': 'new_v', 'new_v_w_branch_a': 'new_v', 'new_v_w_branch_b': 'new_v', 'new_v_w_branch_c': 'new_v', 'new_v_w_out': 'new_v', 'new_v_norm_ffn': 'new_v', 'new_v_w_up': 'new_v', 'new_v_conv_w': 'new_v', 'new_v_conv_b': 'new_v', 'new_v_w_down': 'new_v'}


def _forward(args):
    return _fwd_reference(*[args[k] for k in FWD_PARAMS])


def _output_shape():
    def fwd():
        inp = _fwd_setup_inputs(0)
        return _fwd_reference(*[inp[k] for k in FWD_PARAMS])
    out = _jax.eval_shape(fwd)
    return out.shape, out.dtype

N_MICROBATCH = 1
ADAM_LR = 0.001
ADAM_B1 = 0.9
ADAM_B2 = 0.999
ADAM_EPS = 1e-08
ADAM_WD = 0.01
ADAM_STEP = 10
PER_EXAMPLE_BATCH_AXIS = {'x': 0, 'loss_target': 0}
SHARED_INPUTS = []
_WEIGHT_DTYPES = {'norm_mix': _jnp.float32, 'w_in': _jnp.float32, 'b_gate': _jnp.float32, 'q_norm_a': _jnp.float32, 'k_norm_a': _jnp.float32, 'rel_bias_a': _jnp.float32, 'w_pool': _jnp.float32, 'pool_scale': _jnp.float32, 'w_branch_a': _jnp.float32, 'w_branch_b': _jnp.float32, 'w_branch_c': _jnp.float32, 'w_out': _jnp.float32, 'norm_ffn': _jnp.float32, 'w_up': _jnp.float32, 'conv_w': _jnp.float32, 'conv_b': _jnp.float32, 'w_down': _jnp.float32}
MOMENT_SCALE = {'norm_mix': 2.703370e+01, 'w_in': 4.977165e-01, 'b_gate': 3.390927e+00, 'q_norm_a': 1.123194e+00, 'k_norm_a': 1.118253e+00, 'rel_bias_a': 6.724279e-02, 'w_pool': 3.064271e+00, 'pool_scale': 3.029549e+01, 'w_branch_a': 9.185430e-02, 'w_branch_b': 4.829421e-01, 'w_branch_c': 1.469787e+00, 'w_out': 1.348828e+00, 'norm_ffn': 5.247878e+01, 'w_up': 4.908284e-01, 'conv_w': 7.247447e+00, 'conv_b': 6.575039e+00, 'w_down': 7.370975e-01}


def _to_microbatches(a, axis):
    t = _jnp.moveaxis(a, axis, 0)
    t = t.reshape((N_MICROBATCH, t.shape[0] // N_MICROBATCH) + t.shape[1:])
    return _jnp.moveaxis(t, 1, axis + 1)


def setup_inputs(seed: int = 0) -> dict:
    inp = _fwd_setup_inputs(seed)
    key = _jax.random.fold_in(_jax.random.key(seed), 7919)
    shape, _ = _output_shape()
    out = dict(inp)
    out["loss_target"] = _jax.random.normal(_jax.random.fold_in(key, 0), shape, _jnp.float32)
    for i, name in enumerate(TWIN_WEIGHTS):
        w = inp[name].astype(_jnp.float32)
        if MOMENT_SCALE is None:
            s = _jnp.sqrt(_jnp.mean(_jnp.square(w)) + 1e-30)
        else:
            s = MOMENT_SCALE[name]
        km, kv = _jax.random.split(_jax.random.fold_in(key, i + 1))
        out[name] = w
        out["m_" + name] = s * _jax.random.normal(km, w.shape, _jnp.float32)
        out["v_" + name] = (s * s) * _jax.random.uniform(kv, w.shape, _jnp.float32, 0.5, 1.5)
    if N_MICROBATCH > 1:
        for name, axis in PER_EXAMPLE_BATCH_AXIS.items():
            out[name] = _to_microbatches(out[name], axis)
    return {'x': out['x'], 'norm_mix': out['norm_mix'], 'w_in': out['w_in'], 'b_gate': out['b_gate'], 'q_norm_a': out['q_norm_a'], 'k_norm_a': out['k_norm_a'], 'rel_bias_a': out['rel_bias_a'], 'w_pool': out['w_pool'], 'pool_scale': out['pool_scale'], 'w_branch_a': out['w_branch_a'], 'w_branch_b': out['w_branch_b'], 'w_branch_c': out['w_branch_c'], 'w_out': out['w_out'], 'norm_ffn': out['norm_ffn'], 'w_up': out['w_up'], 'conv_w': out['conv_w'], 'conv_b': out['conv_b'], 'w_down': out['w_down'], 'loss_target': out['loss_target'], 'm_norm_mix': out['m_norm_mix'], 'm_w_in': out['m_w_in'], 'm_b_gate': out['m_b_gate'], 'm_q_norm_a': out['m_q_norm_a'], 'm_k_norm_a': out['m_k_norm_a'], 'm_rel_bias_a': out['m_rel_bias_a'], 'm_w_pool': out['m_w_pool'], 'm_pool_scale': out['m_pool_scale'], 'm_w_branch_a': out['m_w_branch_a'], 'm_w_branch_b': out['m_w_branch_b'], 'm_w_branch_c': out['m_w_branch_c'], 'm_w_out': out['m_w_out'], 'm_norm_ffn': out['m_norm_ffn'], 'm_w_up': out['m_w_up'], 'm_conv_w': out['m_conv_w'], 'm_conv_b': out['m_conv_b'], 'm_w_down': out['m_w_down'], 'v_norm_mix': out['v_norm_mix'], 'v_w_in': out['v_w_in'], 'v_b_gate': out['v_b_gate'], 'v_q_norm_a': out['v_q_norm_a'], 'v_k_norm_a': out['v_k_norm_a'], 'v_rel_bias_a': out['v_rel_bias_a'], 'v_w_pool': out['v_w_pool'], 'v_pool_scale': out['v_pool_scale'], 'v_w_branch_a': out['v_w_branch_a'], 'v_w_branch_b': out['v_w_branch_b'], 'v_w_branch_c': out['v_w_branch_c'], 'v_w_out': out['v_w_out'], 'v_norm_ffn': out['v_norm_ffn'], 'v_w_up': out['v_w_up'], 'v_conv_w': out['v_conv_w'], 'v_conv_b': out['v_conv_b'], 'v_w_down': out['v_w_down']}


def _loss(weights, diff, rest, loss_target):
    with _jax.named_scope("forward"):
        args = {**rest, TWIN_DIFF_INPUT: diff, **{k: w.astype(_WEIGHT_DTYPES[k]) for k, w in weights.items()}}
        y = _forward(args)
    with _jax.named_scope("loss_head"):
        err = _jnp.square(y.astype(_jnp.float32) - loss_target)
        return 0.5 * _jnp.sum(_jnp.mean(err, axis=-1)) if err.ndim else 0.5 * err


def _adamw(w, g, m, v):
    m = ADAM_B1 * m + (1.0 - ADAM_B1) * g
    v = ADAM_B2 * v + (1.0 - ADAM_B2) * _jnp.square(g)
    m_hat = m / (1.0 - ADAM_B1 ** ADAM_STEP)
    v_hat = v / (1.0 - ADAM_B2 ** ADAM_STEP)
    delta = -ADAM_LR * (m_hat / (_jnp.sqrt(v_hat) + ADAM_EPS) + ADAM_WD * w)
    return delta, m, v


def reference(x, norm_mix, w_in, b_gate, q_norm_a, k_norm_a, rel_bias_a, w_pool, pool_scale, w_branch_a, w_branch_b, w_branch_c, w_out, norm_ffn, w_up, conv_w, conv_b, w_down, loss_target, m_norm_mix, m_w_in, m_b_gate, m_q_norm_a, m_k_norm_a, m_rel_bias_a, m_w_pool, m_pool_scale, m_w_branch_a, m_w_branch_b, m_w_branch_c, m_w_out, m_norm_ffn, m_w_up, m_conv_w, m_conv_b, m_w_down, v_norm_mix, v_w_in, v_b_gate, v_q_norm_a, v_k_norm_a, v_rel_bias_a, v_w_pool, v_pool_scale, v_w_branch_a, v_w_branch_b, v_w_branch_c, v_w_out, v_norm_ffn, v_w_up, v_conv_w, v_conv_b, v_w_down):
    given = dict(x=x, norm_mix=norm_mix, w_in=w_in, b_gate=b_gate, q_norm_a=q_norm_a, k_norm_a=k_norm_a, rel_bias_a=rel_bias_a, w_pool=w_pool, pool_scale=pool_scale, w_branch_a=w_branch_a, w_branch_b=w_branch_b, w_branch_c=w_branch_c, w_out=w_out, norm_ffn=norm_ffn, w_up=w_up, conv_w=conv_w, conv_b=conv_b, w_down=w_down, loss_target=loss_target, m_norm_mix=m_norm_mix, m_w_in=m_w_in, m_b_gate=m_b_gate, m_q_norm_a=m_q_norm_a, m_k_norm_a=m_k_norm_a, m_rel_bias_a=m_rel_bias_a, m_w_pool=m_w_pool, m_pool_scale=m_pool_scale, m_w_branch_a=m_w_branch_a, m_w_branch_b=m_w_branch_b, m_w_branch_c=m_w_branch_c, m_w_out=m_w_out, m_norm_ffn=m_norm_ffn, m_w_up=m_w_up, m_conv_w=m_conv_w, m_conv_b=m_conv_b, m_w_down=m_w_down, v_norm_mix=v_norm_mix, v_w_in=v_w_in, v_b_gate=v_b_gate, v_q_norm_a=v_q_norm_a, v_k_norm_a=v_k_norm_a, v_rel_bias_a=v_rel_bias_a, v_w_pool=v_w_pool, v_pool_scale=v_pool_scale, v_w_branch_a=v_w_branch_a, v_w_branch_b=v_w_branch_b, v_w_branch_c=v_w_branch_c, v_w_out=v_w_out, v_norm_ffn=v_norm_ffn, v_w_up=v_w_up, v_conv_w=v_conv_w, v_conv_b=v_conv_b, v_w_down=v_w_down)
    weights = {n: given[n] for n in TWIN_WEIGHTS}
    shared = {n: given[n] for n in SHARED_INPUTS}
    per_example = {n: given[n] for n in ['x']}
    grad_fn = _jax.value_and_grad(_loss, argnums=(0, 1))

    def one_microbatch(ex, loss_target):
        ex = dict(ex)
        diff = ex.pop(TWIN_DIFF_INPUT)
        return grad_fn(weights, diff, {**shared, **ex}, loss_target)

    if N_MICROBATCH == 1:
        loss, (grad_w, grad_x) = one_microbatch(per_example, given["loss_target"])
    else:
        def body(carry, xs):
            loss_sum, grad_sum = carry
            l_k, (gw_k, gx_k) = one_microbatch(xs[0], xs[1])
            with _jax.named_scope("update"):
                return (loss_sum + l_k, _jax.tree.map(_jnp.add, grad_sum, gw_k)), gx_k

        init = (_jnp.zeros((), _jnp.float32), _jax.tree.map(_jnp.zeros_like, weights))
        (loss, grad_w), grad_x = _jax.lax.scan(body, init, (per_example, given["loss_target"]))
    with _jax.named_scope("update"):
        delta_w, new_m, new_v = {}, {}, {}
        for n in TWIN_WEIGHTS:
            delta_w[n], new_m[n], new_v[n] = _adamw(weights[n], grad_w[n], given["m_" + n], given["v_" + n])
    return (loss, grad_x, *[grad_w[n] for n in TWIN_WEIGHTS], *[delta_w[n] for n in TWIN_WEIGHTS],
            *[new_m[n] for n in TWIN_WEIGHTS], *[new_v[n] for n in TWIN_WEIGHTS])
```

```python
import functools
import math

import jax
import jax.numpy as jnp
from jax import lax
from jax.experimental import pallas as pl
from jax.experimental.pallas import tpu as pltpu

F32 = jnp.float32
BF16 = jnp.bfloat16

N_DEV = 8
D_MODEL = 1024
N_HEADS = 8
HEAD_DIM = 64
CHUNK = 64
N_LEFT = 8
BAND = (N_LEFT + 1) * CHUNK
WIDTH = N_HEADS * HEAD_DIM
POOL_WINDOWS = (2, 4, 8, 16)
POOL_DIM = 128
MAX_REL = 2 * CHUNK
REL_TABLE = MAX_REL + CHUNK
D_FF = 2816
EPS = 1e-6
SB_BLOCK = 128
A_BLOCK = N_LEFT * CHUNK
HALO = 16
LANES = 128
VMEM_LIMIT = 56 * 1024 * 1024

ADAM_LR = 0.001
ADAM_B1 = 0.9
ADAM_B2 = 0.999
ADAM_EPS = 1e-08
ADAM_WD = 0.01
ADAM_STEP = 10

MESH = pl.DeviceIdType.MESH


def _params(sem):
    return pltpu.CompilerParams(dimension_semantics=sem, vmem_limit_bytes=VMEM_LIMIT)


def _pick(n, cap):
    if n <= cap:
        return n
    best = None
    for t in range(LANES, cap + 1, LANES):
        if n % t == 0:
            best = t
    assert best is not None, (n, cap)
    return best


def _mm(a, b, mode, out_dtype, name, tm=512, tn_cap=1024, tk_cap=1024, res=None):
    if mode == "nn":
        (M, K), (K2, N) = a.shape, b.shape
    elif mode == "nt":
        (M, K), (N, K2) = a.shape, b.shape
    else:
        (K, M), (K2, N) = a.shape, b.shape
    assert K == K2, (a.shape, b.shape, mode)
    tm = _pick(M, tm)
    tn = _pick(N, tn_cap)
    tk = _pick(K, tk_cap)
    nk = K // tk
    if mode == "nn":
        dims = (((1,), (0,)), ((), ()))
        a_spec = pl.BlockSpec((tm, tk), lambda i, j, k: (i, k))
        b_spec = pl.BlockSpec((tk, tn), lambda i, j, k: (k, j))
    elif mode == "nt":
        dims = (((1,), (1,)), ((), ()))
        a_spec = pl.BlockSpec((tm, tk), lambda i, j, k: (i, k))
        b_spec = pl.BlockSpec((tn, tk), lambda i, j, k: (j, k))
    else:
        dims = (((0,), (0,)), ((), ()))
        a_spec = pl.BlockSpec((tk, tm), lambda i, j, k: (k, i))
        b_spec = pl.BlockSpec((tk, tn), lambda i, j, k: (k, j))

    o_spec = pl.BlockSpec((tm, tn), lambda i, j, k: (i, j))

    def body(a_ref, b_ref, *rest):
        res_ref = rest[0] if res is not None else None
        o_ref, acc_ref = rest[-2:]
        k = pl.program_id(2)

        @pl.when(k == 0)
        def _():
            acc_ref[...] = jnp.zeros_like(acc_ref)

        acc_ref[...] += lax.dot_general(a_ref[...].astype(BF16), b_ref[...].astype(BF16), dims,
                                        preferred_element_type=F32)

        @pl.when(k == nk - 1)
        def _():
            total = acc_ref[...] if res is None else acc_ref[...] + res_ref[...]
            o_ref[...] = total.astype(out_dtype)

    return pl.pallas_call(
        body, name=name,
        grid=(M // tm, N // tn, nk),
        in_specs=[a_spec, b_spec] + ([o_spec] if res is not None else []),
        out_specs=o_spec,
        out_shape=jax.ShapeDtypeStruct((M, N), out_dtype),
        scratch_shapes=[pltpu.VMEM((tm, tn), F32)],
        compiler_params=_params(("parallel", "parallel", "arbitrary")),
    )(*((a, b) if res is None else (a, b, res)))


def _rmsnorm_fwd(x, gain, name, tm=512):
    T, C = x.shape

    def body(x_ref, g_ref, h_ref):
        xv = x_ref[...]
        r = lax.rsqrt(jnp.mean(xv * xv, axis=-1, keepdims=True) + EPS)
        h_ref[...] = (xv * r * g_ref[...]).astype(BF16)

    return pl.pallas_call(
        body, name=name, grid=(T // tm,),
        in_specs=[pl.BlockSpec((tm, C), lambda i: (i, 0)), pl.BlockSpec((1, C), lambda i: (0, 0))],
        out_specs=pl.BlockSpec((tm, C), lambda i: (i, 0)),
        out_shape=jax.ShapeDtypeStruct((T, C), BF16),
        compiler_params=_params(("parallel",)),
    )(x, gain.reshape(1, C))


def _rmsnorm_bwd(x, gain, dh, dres, name, tm=512):
    T, C = x.shape

    def body(x_ref, g_ref, dh_ref, dres_ref, dx_ref, dg_ref):
        @pl.when(pl.program_id(0) == 0)
        def _():
            dg_ref[...] = jnp.zeros_like(dg_ref)

        xv = x_ref[...]
        dy = dh_ref[...].astype(F32)
        r = lax.rsqrt(jnp.mean(xv * xv, axis=-1, keepdims=True) + EPS)
        gdy = dy * g_ref[...]
        inner = jnp.mean(xv * gdy, axis=-1, keepdims=True)
        dx_ref[...] = dres_ref[...] + r * gdy - xv * (r * r * r * inner)
        dg_ref[...] += jnp.sum(dy * xv * r, axis=0, keepdims=True)

    return pl.pallas_call(
        body, name=name, grid=(T // tm,),
        in_specs=[pl.BlockSpec((tm, C), lambda i: (i, 0)), pl.BlockSpec((1, C), lambda i: (0, 0)),
                  pl.BlockSpec((tm, C), lambda i: (i, 0)), pl.BlockSpec((tm, C), lambda i: (i, 0))],
        out_specs=[pl.BlockSpec((tm, C), lambda i: (i, 0)), pl.BlockSpec((1, C), lambda i: (0, 0))],
        out_shape=[jax.ShapeDtypeStruct((T, C), F32), jax.ShapeDtypeStruct((1, C), F32)],
        compiler_params=_params(("arbitrary",)),
    )(x, gain.reshape(1, C), dh, dres)


def _head_norm(t, g):
    tf = t.astype(F32)
    r = lax.rsqrt(jnp.mean(tf * tf, axis=-1, keepdims=True) + EPS)
    return tf * r * g


def _a_logits(qn, kcat, bias, cc, first):
    qc = qn[cc * CHUNK:(cc + 1) * CHUNK]
    kb = kcat[cc * CHUNK:cc * CHUNK + BAND]
    s = lax.dot_general(qc, kb, (((1,), (1,)), ((), ())), preferred_element_type=F32) * (1.0 / math.sqrt(HEAD_DIM))
    s = s + bias
    col = lax.broadcasted_iota(jnp.int32, (CHUNK, BAND), 1) + cc * CHUNK
    return jnp.where(col >= jnp.where(first, A_BLOCK, 0), s, -1e30)


def _softmax(s):
    m = jnp.max(s, axis=-1, keepdims=True)
    e = jnp.exp(s - m)
    return e / jnp.sum(e, axis=-1, keepdims=True)


def _a_specs(T):
    nb = T // A_BLOCK
    cur = pl.BlockSpec((1, A_BLOCK, HEAD_DIM), lambda h, i: (h, i, 0))
    prev = pl.BlockSpec((1, A_BLOCK, HEAD_DIM), lambda h, i: (h, jnp.maximum(i - 1, 0), 0))
    nxt = pl.BlockSpec((1, A_BLOCK, HEAD_DIM), lambda h, i: (h, jnp.minimum(i + 1, nb - 1), 0))
    bias = pl.BlockSpec((1, CHUNK, BAND), lambda h, i: (h, 0, 0))
    gain = pl.BlockSpec((1, HEAD_DIM), lambda h, i: (0, 0))
    hacc = pl.BlockSpec((1, 1, HEAD_DIM), lambda h, i: (h, 0, 0))
    return nb, cur, prev, nxt, bias, gain, hacc


def _attn_a_fwd(q, k, v, bias, gq, gk):
    Hh, T, _ = q.shape
    nb, cur, prev, _, bspec, gspec, _ = _a_specs(T)

    def body(q_ref, kc_ref, kp_ref, vc_ref, vp_ref, b_ref, gq_ref, gk_ref, o_ref):
        first = pl.program_id(1) == 0
        qn = _head_norm(q_ref[0], gq_ref[...]).astype(BF16)
        kcat = jnp.concatenate([_head_norm(kp_ref[0], gk_ref[...]).astype(BF16),
                                _head_norm(kc_ref[0], gk_ref[...]).astype(BF16)], axis=0)
        vcat = jnp.concatenate([vp_ref[0], vc_ref[0]], axis=0)
        bias_v = b_ref[0]
        for cc in range(N_LEFT):
            p = _softmax(_a_logits(qn, kcat, bias_v, cc, first))
            o = jnp.dot(p.astype(BF16), vcat[cc * CHUNK:cc * CHUNK + BAND], preferred_element_type=F32)
            o_ref[0, cc * CHUNK:(cc + 1) * CHUNK, :] = o.astype(BF16)

    return pl.pallas_call(
        body, name="attn_a_fwd", grid=(Hh, nb),
        in_specs=[cur, cur, prev, cur, prev, bspec, gspec, gspec],
        out_specs=cur,
        out_shape=jax.ShapeDtypeStruct((Hh, T, HEAD_DIM), BF16),
        compiler_params=_params(("parallel", "arbitrary")),
    )(q, k, k, v, v, bias, gq.reshape(1, HEAD_DIM), gk.reshape(1, HEAD_DIM))


def _head_norm_bwd(t, g, dn):
    tf = t.astype(F32)
    r = lax.rsqrt(jnp.mean(tf * tf, axis=-1, keepdims=True) + EPS)
    gd = dn * g
    inner = jnp.mean(tf * gd, axis=-1, keepdims=True)
    dt = r * gd - tf * (r * r * r * inner)
    return dt, jnp.sum(dn * tf * r, axis=0, keepdims=True)


def _attn_a_bwd(q, k, v, do, bias, gq, gk):
    Hh, T, _ = q.shape
    nb, cur, prev, _, bspec, gspec, hacc = _a_specs(T)
    scale = 1.0 / math.sqrt(HEAD_DIM)

    def body(q_ref, kc_ref, kp_ref, vc_ref, vp_ref, do_ref, b_ref, gq_ref, gk_ref,
             dq_ref, dkc_ref, dkp_ref, dvc_ref, dvp_ref, db_ref, dgq_ref, dkcat_ref, dvcat_ref):
        i = pl.program_id(1)
        first = i == 0

        @pl.when(first)
        def _():
            db_ref[...] = jnp.zeros_like(db_ref)
            dgq_ref[...] = jnp.zeros_like(dgq_ref)

        qn = _head_norm(q_ref[0], gq_ref[...]).astype(BF16)
        kcat = jnp.concatenate([_head_norm(kp_ref[0], gk_ref[...]).astype(BF16),
                                _head_norm(kc_ref[0], gk_ref[...]).astype(BF16)], axis=0)
        vcat = jnp.concatenate([vp_ref[0], vc_ref[0]], axis=0)
        bias_v = b_ref[0]
        dov = do_ref[0]
        dkcat_ref[...] = jnp.zeros_like(dkcat_ref)
        dvcat_ref[...] = jnp.zeros_like(dvcat_ref)
        dqn_parts = []
        dbias = jnp.zeros((CHUNK, BAND), F32)
        for cc in range(N_LEFT):
            p = _softmax(_a_logits(qn, kcat, bias_v, cc, first))
            doc = dov[cc * CHUNK:(cc + 1) * CHUNK]
            qc = qn[cc * CHUNK:(cc + 1) * CHUNK]
            dp = lax.dot_general(doc, vcat[cc * CHUNK:cc * CHUNK + BAND], (((1,), (1,)), ((), ())),
                                 preferred_element_type=F32)
            delta = jnp.sum(p * dp, axis=-1, keepdims=True)
            ds = p * (dp - delta)
            dbias = dbias + ds
            dsb = (ds * scale).astype(BF16)
            dqn_parts.append(jnp.dot(dsb, kcat[cc * CHUNK:cc * CHUNK + BAND], preferred_element_type=F32))
            dkcat_ref[cc * CHUNK:cc * CHUNK + BAND, :] += lax.dot_general(
                dsb, qc, (((0,), (0,)), ((), ())), preferred_element_type=F32)
            dvcat_ref[cc * CHUNK:cc * CHUNK + BAND, :] += lax.dot_general(
                p.astype(BF16), doc, (((0,), (0,)), ((), ())), preferred_element_type=F32)
        db_ref[0] += dbias
        dqn = jnp.concatenate(dqn_parts, axis=0)
        dq, dg = _head_norm_bwd(q_ref[0], gq_ref[...], dqn)
        dq_ref[0] = dq.astype(BF16)
        dgq_ref[0] += dg
        dkp_ref[0] = dkcat_ref[0:A_BLOCK, :]
        dkc_ref[0] = dkcat_ref[A_BLOCK:2 * A_BLOCK, :]
        dvp_ref[0] = dvcat_ref[0:A_BLOCK, :]
        dvc_ref[0] = dvcat_ref[A_BLOCK:2 * A_BLOCK, :]

    blk_f32 = jax.ShapeDtypeStruct((Hh, T, HEAD_DIM), F32)
    return pl.pallas_call(
        body, name="attn_a_bwd", grid=(Hh, nb),
        in_specs=[cur, cur, prev, cur, prev, cur, bspec, gspec, gspec],
        out_specs=[cur, cur, cur, cur, cur, bspec, hacc],
        out_shape=[jax.ShapeDtypeStruct((Hh, T, HEAD_DIM), BF16), blk_f32, blk_f32, blk_f32, blk_f32,
                   jax.ShapeDtypeStruct((Hh, CHUNK, BAND), F32), jax.ShapeDtypeStruct((Hh, 1, HEAD_DIM), F32)],
        scratch_shapes=[pltpu.VMEM((2 * A_BLOCK, HEAD_DIM), F32), pltpu.VMEM((2 * A_BLOCK, HEAD_DIM), F32)],
        compiler_params=_params(("parallel", "arbitrary")),
    )(q, k, k, v, v, do, bias, gq.reshape(1, HEAD_DIM), gk.reshape(1, HEAD_DIM))


def _attn_a_bwd_keys(k, dkc, dkp, dvc, dvp, gk):
    Hh, T, _ = k.shape
    nb, cur, _, nxt, _, gspec, hacc = _a_specs(T)

    def body(k_ref, dkc_ref, dkp_ref, dvc_ref, dvp_ref, gk_ref, dk_ref, dv_ref, dgk_ref):
        i = pl.program_id(1)

        @pl.when(i == 0)
        def _():
            dgk_ref[...] = jnp.zeros_like(dgk_ref)

        has_next = (i < nb - 1).astype(F32)
        dkn = dkc_ref[0] + has_next * dkp_ref[0]
        dk, dg = _head_norm_bwd(k_ref[0], gk_ref[...], dkn)
        dk_ref[0] = dk.astype(BF16)
        dv_ref[0] = (dvc_ref[0] + has_next * dvp_ref[0]).astype(BF16)
        dgk_ref[0] += dg

    blk = jax.ShapeDtypeStruct((Hh, T, HEAD_DIM), BF16)
    return pl.pallas_call(
        body, name="attn_a_bwd_keys", grid=(Hh, nb),
        in_specs=[cur, cur, nxt, cur, nxt, gspec],
        out_specs=[cur, cur, hacc],
        out_shape=[blk, blk, jax.ShapeDtypeStruct((Hh, 1, HEAD_DIM), F32)],
        compiler_params=_params(("parallel", "arbitrary")),
    )(k, dkc, dkp, dvc, dvp, gk.reshape(1, HEAD_DIM))


def _split(x):
    hi = x.astype(BF16)
    return hi, (x - hi.astype(F32)).astype(BF16)


def _dot_split(x, m):
    hi, lo = _split(x)
    return jnp.dot(hi, m, preferred_element_type=F32) + jnp.dot(lo, m, preferred_element_type=F32)


def _scan_matrix(after):
    r = lax.broadcasted_iota(jnp.int32, (SB_BLOCK, 2 * SB_BLOCK), 0)
    c = lax.broadcasted_iota(jnp.int32, (SB_BLOCK, 2 * SB_BLOCK), 1)
    tri = (r > c) if after else (r < c)
    return jnp.where(jnp.logical_or(c >= SB_BLOCK, tri), 1.0, 0.0).astype(BF16)


def _sb_block(qv, kv, carry, scan, diag):
    z = lax.dot_general(qv, kv, (((1,), (1,)), ((), ())), preferred_element_type=F32) * (1.0 / math.sqrt(HEAD_DIM))
    e = jnp.exp(-jnp.abs(z))
    sp = jnp.maximum(z, 0.0) + jnp.log(1.0 + e)
    if diag:
        r = lax.broadcasted_iota(jnp.int32, (SB_BLOCK, SB_BLOCK), 0)
        c = lax.broadcasted_iota(jnp.int32, (SB_BLOCK, SB_BLOCK), 1)
        mask = c < r
        logkeep = jnp.where(mask, -sp, 0.0)
    else:
        mask = None
        logkeep = -sp
    sums = _dot_split(logkeep, scan)
    tail = sums[:, :SB_BLOCK]
    total = sums[:, SB_BLOCK:]
    if carry is not None:
        tail = tail + carry
    w = jnp.exp(z - sp + tail)
    if diag:
        w = jnp.where(mask, w, 0.0)
    return z, e, w, total, mask


def _sb_specs(T):
    nq = T // SB_BLOCK
    qblk = pl.BlockSpec((1, SB_BLOCK, HEAD_DIM), lambda h, i: (h, i, 0))
    full = pl.BlockSpec((1, T, HEAD_DIM), lambda h, i: (h, 0, 0))
    return nq, qblk, full


def _attn_b_fwd(q, k, v):
    Hh, T, _ = q.shape
    nq, qblk, full = _sb_specs(T)

    def body(q_ref, k_ref, v_ref, o_ref):
        i = pl.program_id(1)
        scan = _scan_matrix(True)
        qv = q_ref[0]

        def rows(j):
            return pl.ds(pl.multiple_of(j * SB_BLOCK, SB_BLOCK), SB_BLOCK)

        _, _, w, total, _ = _sb_block(qv, k_ref[0, rows(i), :], None, scan, True)
        acc = jnp.dot(w.astype(BF16), v_ref[0, rows(i), :], preferred_element_type=F32)

        def step(jj, c):
            acc, carry = c
            j = i - 1 - jj
            _, _, w, total, _ = _sb_block(qv, k_ref[0, rows(j), :], carry, scan, False)
            acc = acc + jnp.dot(w.astype(BF16), v_ref[0, rows(j), :], preferred_element_type=F32)
            return acc, carry + total

        acc, _ = lax.fori_loop(0, i, step, (acc, total))
        o_ref[0] = acc.astype(BF16)

    return pl.pallas_call(
        body, name="attn_b_fwd", grid=(Hh, nq),
        in_specs=[qblk, full, full],
        out_specs=qblk,
        out_shape=jax.ShapeDtypeStruct((Hh, T, HEAD_DIM), BF16),
        compiler_params=_params(("parallel", "arbitrary")),
    )(q, k, v)


def _attn_b_bwd(q, k, v, do):
    Hh, T, _ = q.shape
    nq, qblk, full = _sb_specs(T)
    scale = 1.0 / math.sqrt(HEAD_DIM)

    def body(q_ref, k_ref, v_ref, do_ref, dq_ref, dk_ref, dv_ref, g_ref, s_ref):
        i = pl.program_id(1)

        @pl.when(i == 0)
        def _():
            dk_ref[...] = jnp.zeros_like(dk_ref)
            dv_ref[...] = jnp.zeros_like(dv_ref)

        scan_after = _scan_matrix(True)
        scan_before = _scan_matrix(False)
        qv = q_ref[0]
        dov = do_ref[0]

        def rows(j):
            return pl.ds(pl.multiple_of(j * SB_BLOCK, SB_BLOCK), SB_BLOCK)

        def weights(j, carry, diag):
            z, e, w, total, mask = _sb_block(qv, k_ref[0, rows(j), :], carry, scan_after, diag)
            dw = lax.dot_general(dov, v_ref[0, rows(j), :], (((1,), (1,)), ((), ())), preferred_element_type=F32)
            g_ref[j] = dw * w
            s_ref[j] = jnp.where(z >= 0.0, 1.0, e) / (1.0 + e)
            dv_ref[0, rows(j), :] += lax.dot_general(w.astype(BF16), dov, (((0,), (0,)), ((), ())),
                                                     preferred_element_type=F32)
            return total

        total = weights(i, None, True)

        def down(jj, carry):
            return carry + weights(i - 1 - jj, carry, False)

        lax.fori_loop(0, i, down, total)

        def grads(j, dq, carry, diag):
            g = g_ref[j]
            sums = _dot_split(g, scan_before)
            before = sums[:, :SB_BLOCK]
            if carry is not None:
                before = before + carry
            dz = g - s_ref[j] * (g + before)
            if diag:
                r = lax.broadcasted_iota(jnp.int32, (SB_BLOCK, SB_BLOCK), 0)
                c = lax.broadcasted_iota(jnp.int32, (SB_BLOCK, SB_BLOCK), 1)
                dz = jnp.where(c < r, dz, 0.0)
            dzb = (dz * scale).astype(BF16)
            dq = dq + jnp.dot(dzb, k_ref[0, rows(j), :], preferred_element_type=F32)
            dk_ref[0, rows(j), :] += lax.dot_general(dzb, qv, (((0,), (0,)), ((), ())), preferred_element_type=F32)
            return dq, sums[:, SB_BLOCK:]

        def up(j, c):
            dq, carry = c
            dq, total = grads(j, dq, carry, False)
            return dq, carry + total

        zero = jnp.zeros((SB_BLOCK, SB_BLOCK), F32)
        dq, carry = lax.fori_loop(0, i, up, (jnp.zeros((SB_BLOCK, HEAD_DIM), F32), zero))
        dq, _ = grads(i, dq, carry, True)
        dq_ref[0] = dq.astype(BF16)

    return pl.pallas_call(
        body, name="attn_b_bwd", grid=(Hh, nq),
        in_specs=[qblk, full, full, qblk],
        out_specs=[qblk, full, full],
        out_shape=[jax.ShapeDtypeStruct((Hh, T, HEAD_DIM), BF16), jax.ShapeDtypeStruct((Hh, T, HEAD_DIM), F32),
                   jax.ShapeDtypeStruct((Hh, T, HEAD_DIM), F32)],
        scratch_shapes=[pltpu.VMEM((nq, SB_BLOCK, SB_BLOCK), F32), pltpu.VMEM((nq, SB_BLOCK, SB_BLOCK), F32)],
        compiler_params=_params(("parallel", "arbitrary")),
    )(q, k, v, do)


def _window_sums(ext, forward):
    n = ext.shape[0]
    out = []
    s = ext
    for step in (1, 2, 4, 8):
        s = s + pltpu.roll(s, (n - step) if forward else step, 0)
        out.append(s)
    return out


def _pool_counts(base, rows, win):
    t = base + lax.broadcasted_iota(jnp.int32, (rows, 1), 0)
    return jnp.minimum(t + 1, win).astype(F32)


def _pooled(u_ref, up_ref, i, tm):
    prev = jnp.where(i > 0, up_ref[...], 0.0)
    ext = jnp.concatenate([prev, u_ref[...]], axis=0)
    sums = _window_sums(ext, False)
    parts = []
    for g, win in enumerate(POOL_WINDOWS):
        cols = slice(g * POOL_DIM, (g + 1) * POOL_DIM)
        cnt = _pool_counts(i * tm, tm, win)
        parts.append(sums[g][HALO:, cols] / cnt - ext[HALO:, cols])
    return parts


def _pool_fwd(ucg, w_pool, scale, tm=512):
    T = ucg.shape[0]
    C = WIDTH

    def body(u_ref, up_ref, w_ref, s_ref, o_ref):
        i = pl.program_id(0)
        parts = _pooled(u_ref, up_ref, i, tm)
        for g in range(len(POOL_WINDOWS)):
            mixed = jnp.dot(parts[g].astype(BF16), w_ref[g], preferred_element_type=F32)
            o_ref[:, g * POOL_DIM:(g + 1) * POOL_DIM] = (mixed * s_ref[:, g * POOL_DIM:(g + 1) * POOL_DIM]).astype(BF16)

    return pl.pallas_call(
        body, name="pool_fwd", grid=(T // tm,),
        in_specs=[pl.BlockSpec((tm, C), lambda i: (i, 0)),
                  pl.BlockSpec((HALO, C), lambda i: (jnp.maximum(i * (tm // HALO) - 1, 0), 0)),
                  pl.BlockSpec((len(POOL_WINDOWS), POOL_DIM, POOL_DIM), lambda i: (0, 0, 0)),
                  pl.BlockSpec((1, C), lambda i: (0, 0))],
        out_specs=pl.BlockSpec((tm, C), lambda i: (i, 0)),
        out_shape=jax.ShapeDtypeStruct((T, C), BF16),
        compiler_params=_params(("parallel",)),
    )(ucg, ucg, w_pool.astype(BF16), scale.reshape(1, C))


def _pool_bwd(ucg, do_c, w_pool, scale, tm=512):
    T = ucg.shape[0]
    C = WIDTH
    nt = T // tm
    G = len(POOL_WINDOWS)

    def body(u_ref, up_ref, do_ref, don_ref, w_ref, s_ref, du_ref, dw_ref, ds_ref):
        i = pl.program_id(0)

        @pl.when(i == 0)
        def _():
            dw_ref[...] = jnp.zeros_like(dw_ref)
            ds_ref[...] = jnp.zeros_like(ds_ref)

        parts = _pooled(u_ref, up_ref, i, tm)
        nxt = jnp.where(i < nt - 1, don_ref[...].astype(F32), 0.0)
        do_ext = jnp.concatenate([do_ref[...].astype(F32), nxt], axis=0) * s_ref[...]
        for g, win in enumerate(POOL_WINDOWS):
            cols = slice(g * POOL_DIM, (g + 1) * POOL_DIM)
            pooled_b = parts[g].astype(BF16)
            dmix = do_ext[:, cols].astype(BF16)
            mixed = jnp.dot(pooled_b, w_ref[g], preferred_element_type=F32)
            ds_ref[:, cols] += jnp.sum(do_ref[:, cols].astype(F32) * mixed, axis=0, keepdims=True)
            dw_ref[g] += lax.dot_general(pooled_b, dmix[:tm], (((0,), (0,)), ((), ())), preferred_element_type=F32)
            dpool = lax.dot_general(dmix, w_ref[g], (((1,), (1,)), ((), ())), preferred_element_type=F32)
            scaled = dpool / _pool_counts(i * tm, tm + HALO, win)
            fwd = _window_sums(scaled, True)[g]
            du_ref[:, cols] = (fwd[:tm] - dpool[:tm]).astype(BF16)

    return pl.pallas_call(
        body, name="pool_bwd", grid=(nt,),
        in_specs=[pl.BlockSpec((tm, C), lambda i: (i, 0)),
                  pl.BlockSpec((HALO, C), lambda i: (jnp.maximum(i * (tm // HALO) - 1, 0), 0)),
                  pl.BlockSpec((tm, C), lambda i: (i, 0)),
                  pl.BlockSpec((HALO, C), lambda i: (jnp.minimum((i + 1) * (tm // HALO), T // HALO - 1), 0)),
                  pl.BlockSpec((G, POOL_DIM, POOL_DIM), lambda i: (0, 0, 0)),
                  pl.BlockSpec((1, C), lambda i: (0, 0))],
        out_specs=[pl.BlockSpec((tm, C), lambda i: (i, 0)),
                   pl.BlockSpec((G, POOL_DIM, POOL_DIM), lambda i: (0, 0, 0)),
                   pl.BlockSpec((1, C), lambda i: (0, 0))],
        out_shape=[jax.ShapeDtypeStruct((T, C), BF16), jax.ShapeDtypeStruct((G, POOL_DIM, POOL_DIM), F32),
                   jax.ShapeDtypeStruct((1, C), F32)],
        compiler_params=_params(("arbitrary",)),
    )(ucg, ucg, do_c, do_c, w_pool.astype(BF16), scale.reshape(1, C))


def _merge_fwd(oa, ob, oc, glog, b_gate, wa, wb, wc, tm=256):
    T = oa.shape[0]
    Dm = D_MODEL
    row = lambda c: pl.BlockSpec((tm, c), lambda i: (i, 0))
    wspec = pl.BlockSpec((WIDTH, Dm), lambda i: (0, 0))

    def body(oa_ref, ob_ref, oc_ref, g_ref, b_ref, wa_ref, wb_ref, wc_ref, m_ref, ya_ref, yb_ref, yc_ref):
        merged = jnp.zeros((tm, Dm), F32)
        for kk, (o_ref, w_ref, y_ref) in enumerate(((oa_ref, wa_ref, ya_ref), (ob_ref, wb_ref, yb_ref),
                                                    (oc_ref, wc_ref, yc_ref))):
            y = jnp.dot(o_ref[...], w_ref[...], preferred_element_type=F32)
            gate = jax.nn.sigmoid(g_ref[:, kk * Dm:(kk + 1) * Dm] + b_ref[:, kk * Dm:(kk + 1) * Dm])
            merged = merged + gate * y
            y_ref[...] = y.astype(BF16)
        m_ref[...] = merged.astype(BF16)

    out = jax.ShapeDtypeStruct((T, Dm), BF16)
    return pl.pallas_call(
        body, name="merge_fwd", grid=(T // tm,),
        in_specs=[row(WIDTH), row(WIDTH), row(WIDTH), row(3 * Dm), pl.BlockSpec((1, 3 * Dm), lambda i: (0, 0)),
                  wspec, wspec, wspec],
        out_specs=[row(Dm)] * 4,
        out_shape=[out] * 4,
        compiler_params=_params(("parallel",)),
    )(oa, ob, oc, glog, b_gate.reshape(1, 3 * Dm), wa, wb, wc)


def _merge_bwd(dmerged, glog, b_gate, ya, yb, yc, tm=256):
    T = dmerged.shape[0]
    Dm = D_MODEL
    row = lambda c: pl.BlockSpec((tm, c), lambda i: (i, 0))

    def body(dm_ref, g_ref, b_ref, ya_ref, yb_ref, yc_ref, dya_ref, dyb_ref, dyc_ref, dg_ref, db_ref):
        @pl.when(pl.program_id(0) == 0)
        def _():
            db_ref[...] = jnp.zeros_like(db_ref)

        dm = dm_ref[...]
        for kk, (y_ref, dy_ref) in enumerate(((ya_ref, dya_ref), (yb_ref, dyb_ref), (yc_ref, dyc_ref))):
            cols = slice(kk * Dm, (kk + 1) * Dm)
            gate = jax.nn.sigmoid(g_ref[:, cols] + b_ref[:, cols])
            dy_ref[...] = (dm * gate).astype(BF16)
            dlog = dm * y_ref[...].astype(F32) * gate * (1.0 - gate)
            dg_ref[:, cols] = dlog.astype(BF16)
            db_ref[:, cols] += jnp.sum(dlog, axis=0, keepdims=True)

    out = jax.ShapeDtypeStruct((T, Dm), BF16)
    return pl.pallas_call(
        body, name="merge_bwd", grid=(T // tm,),
        in_specs=[row(Dm), row(3 * Dm), pl.BlockSpec((1, 3 * Dm), lambda i: (0, 0)), row(Dm), row(Dm), row(Dm)],
        out_specs=[row(Dm), row(Dm), row(Dm), row(3 * Dm), pl.BlockSpec((1, 3 * Dm), lambda i: (0, 0))],
        out_shape=[out, out, out, jax.ShapeDtypeStruct((T, 3 * Dm), BF16), jax.ShapeDtypeStruct((1, 3 * Dm), F32)],
        compiler_params=_params(("arbitrary",)),
    )(dmerged, glog, b_gate.reshape(1, 3 * Dm), ya, yb, yc)


def _residual_add(x, y, name, tm=512):
    T, C = x.shape

    def body(x_ref, y_ref, o_ref):
        o_ref[...] = x_ref[...] + y_ref[...]

    spec = pl.BlockSpec((tm, C), lambda i: (i, 0))
    return pl.pallas_call(body, name=name, grid=(T // tm,), in_specs=[spec, spec], out_specs=spec,
                          out_shape=jax.ShapeDtypeStruct((T, C), F32), compiler_params=_params(("parallel",)))(x, y)


FF_TILE = 256
FF_TILES = D_FF // FF_TILE
CONV_HALO = 8


def _ff_pair_order(w):
    lead = w.shape[:-1]
    n = len(lead)
    w = w.reshape(*lead, 2, FF_TILES, FF_TILE)
    return jnp.swapaxes(w, n, n + 1).reshape(*lead, 2 * D_FF)


def _ff_natural_order(w):
    lead = w.shape[:-1]
    n = len(lead)
    w = w.reshape(*lead, FF_TILES, 2, FF_TILE)
    return jnp.swapaxes(w, n, n + 1).reshape(*lead, 2 * D_FF)


def _conv(ext, w_ref, b_ref):
    c = b_ref[...] + w_ref[2:3, :] * ext
    c = c + w_ref[1:2, :] * pltpu.roll(ext, 1, 0)
    c = c + w_ref[0:1, :] * pltpu.roll(ext, 2, 0)
    return c[CONV_HALO:]


def _ff_specs(T, tm):
    pair = pl.BlockSpec((tm, 2 * FF_TILE), lambda i, j: (i, j))
    prev = pl.BlockSpec((CONV_HALO, 2 * FF_TILE), lambda i, j: (jnp.maximum(i * (tm // CONV_HALO) - 1, 0), j))
    nxt = pl.BlockSpec((CONV_HALO, 2 * FF_TILE),
                       lambda i, j: (jnp.minimum((i + 1) * (tm // CONV_HALO), T // CONV_HALO - 1), j))
    half = pl.BlockSpec((tm, FF_TILE), lambda i, j: (i, j))
    small = lambda r: pl.BlockSpec((r, 2 * FF_TILE), lambda i, j: (0, j))
    return pair, prev, nxt, half, small


def _swap_grid(spec):
    return pl.BlockSpec(spec.block_shape, lambda j, i, f=spec.index_map: f(i, j))


def _ff_act_fwd(u, conv_w, conv_b, tm=512):
    T = u.shape[0]
    pair, prev, _, half, small = _ff_specs(T, tm)

    def body(u_ref, p_ref, w_ref, b_ref, a_ref):
        i = pl.program_id(0)
        c = _conv(jnp.concatenate([jnp.where(i > 0, p_ref[...], 0.0), u_ref[...]], axis=0), w_ref, b_ref)
        cg, cv = c[:, :FF_TILE], c[:, FF_TILE:]
        a_ref[...] = (cg * jax.nn.sigmoid(cg) * cv).astype(BF16)

    return pl.pallas_call(
        body, name="ff_act_fwd", grid=(T // tm, FF_TILES),
        in_specs=[pair, prev, small(3), small(1)],
        out_specs=half,
        out_shape=jax.ShapeDtypeStruct((T, D_FF), BF16),
        compiler_params=_params(("parallel", "parallel")),
    )(u, u, conv_w, conv_b.reshape(1, -1))


def _ff_act_bwd(u, da, conv_w, conv_b, tm=512):
    T = u.shape[0]
    pair, prev, _, half, small = _ff_specs(T, tm)

    def body(u_ref, p_ref, da_ref, w_ref, b_ref, dc_ref, dw_ref, db_ref):
        i = pl.program_id(1)

        @pl.when(i == 0)
        def _():
            dw_ref[...] = jnp.zeros_like(dw_ref)
            db_ref[...] = jnp.zeros_like(db_ref)

        ext = jnp.concatenate([jnp.where(i > 0, p_ref[...], 0.0), u_ref[...]], axis=0)
        c = _conv(ext, w_ref, b_ref)
        cg, cv = c[:, :FF_TILE], c[:, FF_TILE:]
        da = da_ref[...]
        sg = jax.nn.sigmoid(cg)
        dc = jnp.concatenate([da * cv * sg * (1.0 + cg * (1.0 - sg)), da * cg * sg], axis=1)
        dc_ref[...] = dc
        db_ref[...] += jnp.sum(dc, axis=0, keepdims=True)
        dw_ref[2:3, :] += jnp.sum(dc * ext[CONV_HALO:], axis=0, keepdims=True)
        dw_ref[1:2, :] += jnp.sum(dc * pltpu.roll(ext, 1, 0)[CONV_HALO:], axis=0, keepdims=True)
        dw_ref[0:1, :] += jnp.sum(dc * pltpu.roll(ext, 2, 0)[CONV_HALO:], axis=0, keepdims=True)

    return pl.pallas_call(
        body, name="ff_act_bwd", grid=(FF_TILES, T // tm),
        in_specs=[_swap_grid(pair), _swap_grid(prev), _swap_grid(half), _swap_grid(small(3)), _swap_grid(small(1))],
        out_specs=[_swap_grid(pair), _swap_grid(small(3)), _swap_grid(small(1))],
        out_shape=[jax.ShapeDtypeStruct((T, 2 * D_FF), F32), jax.ShapeDtypeStruct((3, 2 * D_FF), F32),
                   jax.ShapeDtypeStruct((1, 2 * D_FF), F32)],
        compiler_params=_params(("parallel", "arbitrary")),
    )(u, u, da, conv_w, conv_b.reshape(1, -1))


def _ff_conv_bwd(dc, conv_w, tm=512):
    T = dc.shape[0]
    nt = T // tm
    pair, _, nxt, _, small = _ff_specs(T, tm)

    def body(dc_ref, n_ref, w_ref, du_ref):
        i = pl.program_id(0)
        ext = jnp.concatenate([dc_ref[...], jnp.where(i < nt - 1, n_ref[...], 0.0)], axis=0)
        n = tm + CONV_HALO
        du = w_ref[2:3, :] * ext + w_ref[1:2, :] * pltpu.roll(ext, n - 1, 0) + w_ref[0:1, :] * pltpu.roll(ext, n - 2, 0)
        du_ref[...] = du[:tm].astype(BF16)

    return pl.pallas_call(
        body, name="ff_conv_bwd", grid=(nt, FF_TILES),
        in_specs=[pair, nxt, small(3)],
        out_specs=pair,
        out_shape=jax.ShapeDtypeStruct((T, 2 * D_FF), BF16),
        compiler_params=_params(("parallel", "parallel")),
    )(dc, dc, conv_w)


def _loss_head(y, target, tm=512):
    T, C = y.shape
    nt = T // tm

    def body(y_ref, t_ref, dy_ref, l_ref):
        err = y_ref[...] - t_ref[...]
        dy_ref[...] = err * (1.0 / C)
        part = jnp.sum(err * err, axis=0, keepdims=True) * (0.5 / C)
        l_ref[0] = jnp.broadcast_to(part, (8, C))

    spec = pl.BlockSpec((tm, C), lambda i: (i, 0))
    dy, parts = pl.pallas_call(
        body, name="loss_head", grid=(nt,),
        in_specs=[spec, spec],
        out_specs=[spec, pl.BlockSpec((1, 8, C), lambda i: (i, 0, 0))],
        out_shape=[jax.ShapeDtypeStruct((T, C), F32), jax.ShapeDtypeStruct((nt, 8, C), F32)],
        compiler_params=_params(("parallel",)),
    )(y, target)
    return dy, jnp.sum(parts[:, 0, :])


def _adamw_math(w, g, m, v):
    m = ADAM_B1 * m + (1.0 - ADAM_B1) * g
    v = ADAM_B2 * v + (1.0 - ADAM_B2) * (g * g)
    m_hat = m / (1.0 - ADAM_B1 ** ADAM_STEP)
    v_hat = v / (1.0 - ADAM_B2 ** ADAM_STEP)
    delta = -ADAM_LR * (m_hat / (jnp.sqrt(v_hat) + ADAM_EPS) + ADAM_WD * w)
    return delta, m, v


def _adamw(parts, w, m, v, name, tm=256):
    R, C = w.shape
    tm = _pick_rows(R, tm)

    def body(p_ref, w_ref, m_ref, v_ref, g_ref, d_ref, nm_ref, nv_ref):
        g = p_ref[0].astype(F32)
        for s in range(1, N_DEV):
            g = g + p_ref[s].astype(F32)
        delta, nm, nv = _adamw_math(w_ref[...], g, m_ref[...], v_ref[...])
        g_ref[...] = g
        d_ref[...] = delta
        nm_ref[...] = nm
        nv_ref[...] = nv

    spec = pl.BlockSpec((tm, C), lambda i: (i, 0))
    out = jax.ShapeDtypeStruct((R, C), F32)
    return pl.pallas_call(
        body, name=name, grid=(R // tm,),
        in_specs=[pl.BlockSpec((N_DEV, tm, C), lambda i: (0, i, 0)), spec, spec, spec],
        out_specs=[spec] * 4,
        out_shape=[out] * 4,
        compiler_params=_params(("parallel",)),
    )(parts, w, m, v)


def _pick_rows(n, cap):
    best = None
    for t in range(16, min(n, cap) + 1, 16):
        if n % t == 0:
            best = t
    assert best is not None, (n, cap)
    return best


def _exchange(src, scatter, name):
    slab = src.shape[1:] if scatter else src.shape

    def body(src_ref, out_ref, send_sems, recv_sems, local_sem):
        x, y, c = lax.axis_index("x"), lax.axis_index("y"), lax.axis_index("c")
        me = 4 * x + 2 * y + c

        def piece(d):
            return src_ref.at[d] if scatter else src_ref

        mine = pltpu.make_async_copy(piece(me), out_ref.at[me], local_sem)
        mine.start()
        copies = []
        for k in range(1, N_DEV):
            px = 1 - x if k & 4 else x
            py = 1 - y if k & 2 else y
            pc = 1 - c if k & 1 else c
            peer = 4 * px + 2 * py + pc
            cp = pltpu.make_async_remote_copy(
                src_ref=piece(peer), dst_ref=out_ref.at[me],
                send_sem=send_sems.at[k], recv_sem=recv_sems.at[k],
                device_id=(px, py, pc), device_id_type=MESH)
            cp.start()
            copies.append((cp, peer))
        for k, (cp, peer) in enumerate(copies, start=1):
            cp.wait_send()
            pltpu.make_async_remote_copy(
                src_ref=piece(peer), dst_ref=out_ref.at[peer],
                send_sem=send_sems.at[k], recv_sem=recv_sems.at[k],
                device_id=(x, y, c), device_id_type=MESH).wait_recv()
        mine.wait()

    return pl.pallas_call(
        body, name=name,
        in_specs=[pl.BlockSpec(memory_space=pl.ANY)],
        out_specs=pl.BlockSpec(memory_space=pl.ANY),
        out_shape=jax.ShapeDtypeStruct((N_DEV,) + tuple(slab), src.dtype),
        scratch_shapes=[pltpu.SemaphoreType.DMA((N_DEV,)), pltpu.SemaphoreType.DMA((N_DEV,)),
                        pltpu.SemaphoreType.DMA],
    )(src)


SHARDED = ("w_in", "w_branch_a", "w_branch_b", "w_branch_c", "w_out", "w_up", "w_down")
REPLICATED = ("norm_mix", "b_gate", "q_norm_a", "k_norm_a", "rel_bias_a", "w_pool", "pool_scale", "norm_ffn", "conv_b")
WEIGHTS = ("norm_mix", "w_in", "b_gate", "q_norm_a", "k_norm_a", "rel_bias_a", "w_pool", "pool_scale",
           "w_branch_a", "w_branch_b", "w_branch_c", "w_out", "norm_ffn", "w_up", "conv_w", "conv_b", "w_down")
PACK_COLS = 1024
SMALL_COLS = 128
QKV_COLS = 6 * WIDTH


def _rel_index():
    q_off = jnp.arange(CHUNK)[:, None] + N_LEFT * CHUNK
    k_off = jnp.arange(BAND)[None, :]
    return jnp.clip(q_off - k_off, -(CHUNK - 1), MAX_REL) + (CHUNK - 1)


def _rel_onehot():
    rel = _rel_index().reshape(1, CHUNK * BAND)
    return (rel == jnp.arange(REL_TABLE)[:, None]).astype(BF16)


def _select_mm(x, onehot, mode, name):
    hi = x.astype(BF16)
    r1 = x - hi.astype(F32)
    mid = r1.astype(BF16)
    lo = (r1 - mid.astype(F32)).astype(BF16)
    y = _mm(jnp.concatenate([hi, mid, lo, jnp.zeros_like(hi)], axis=0), onehot, mode, F32, name)
    n = x.shape[0]
    return y[:n] + y[n:2 * n] + y[2 * n:3 * n]


def _pack_rows(arrays, cols, row_multiple):
    flat = jnp.concatenate([a.reshape(-1) for a in arrays])
    rows = -(-flat.shape[0] // cols)
    rows = -(-rows // row_multiple) * row_multiple
    return jnp.pad(flat, (0, rows * cols - flat.shape[0])).reshape(rows, cols)


def _unpack_rows(packed, like):
    flat = packed.reshape(-1)
    out, off = [], 0
    for a in like:
        out.append(flat[off:off + a.size].reshape(a.shape))
        off += a.size
    return out


def _to_heads(t):
    T = t.shape[0]
    return t.reshape(T, -1, N_HEADS, HEAD_DIM).transpose(1, 2, 0, 3)


def _from_heads(t):
    return t.transpose(1, 0, 2).reshape(t.shape[1], WIDTH)


def _gather_columns(g, shape):
    L, rows, cols = shape
    return g.reshape(N_DEV, L, rows, cols).transpose(1, 2, 0, 3).reshape(L, rows, N_DEV * cols)


def _gather_rows(g, shape):
    L, rows, cols = shape
    return g.reshape(N_DEV, L, rows, cols).transpose(1, 0, 2, 3).reshape(L, N_DEV * rows, cols)


def _split_columns(full):
    L, rows, allc = full.shape
    cols = allc // N_DEV
    return full.reshape(L, rows, N_DEV, cols).transpose(2, 0, 1, 3).reshape(N_DEV, -1, PACK_COLS)


def _split_rows(full):
    L, allr, cols = full.shape
    rows = allr // N_DEV
    return full.reshape(L, N_DEV, rows, cols).transpose(1, 0, 2, 3).reshape(N_DEV, -1, PACK_COLS)


def kernel(x, norm_mix, w_in, b_gate, q_norm_a, k_norm_a, rel_bias_a, w_pool, pool_scale, w_branch_a, w_branch_b, w_branch_c, w_out, norm_ffn, w_up, conv_w, conv_b, w_down, loss_target, m_norm_mix, m_w_in, m_b_gate, m_q_norm_a, m_k_norm_a, m_rel_bias_a, m_w_pool, m_pool_scale, m_w_branch_a, m_w_branch_b, m_w_branch_c, m_w_out, m_norm_ffn, m_w_up, m_conv_w, m_conv_b, m_w_down, v_norm_mix, v_w_in, v_b_gate, v_q_norm_a, v_k_norm_a, v_rel_bias_a, v_w_pool, v_pool_scale, v_w_branch_a, v_w_branch_b, v_w_branch_c, v_w_out, v_norm_ffn, v_w_up, v_conv_w, v_conv_b, v_w_down):
    args = dict(locals())
    w = {n: args[n] for n in WEIGHTS}
    m = {n: args["m_" + n] for n in WEIGHTS}
    v = {n: args["v_" + n] for n in WEIGHTS}
    L = w_in.shape[0]
    T = x.shape[1]
    xs = x.reshape(T, D_MODEL)
    target = loss_target.reshape(T, D_MODEL)

    conv_bits = lax.bitcast_convert_type(conv_w, BF16)
    packed = _pack_rows([w[n].astype(BF16) for n in SHARDED] + [conv_bits], PACK_COLS, 16)
    gathered = _exchange(packed, False, "gather_weights")
    rows = [w[n].size // PACK_COLS for n in SHARDED]
    offs = [sum(rows[:i]) for i in range(len(rows) + 1)]
    part = {n: gathered[:, offs[i]:offs[i + 1]] for i, n in enumerate(SHARDED)}
    w_in_f = _gather_columns(part["w_in"], w_in.shape)
    w_qkv, w_uc, w_g = w_in_f[:, :, :QKV_COLS], w_in_f[:, :, QKV_COLS:QKV_COLS + WIDTH], w_in_f[:, :, QKV_COLS + WIDTH:]
    w_a = _gather_columns(part["w_branch_a"], w_branch_a.shape)
    w_b = _gather_columns(part["w_branch_b"], w_branch_b.shape)
    w_c = _gather_columns(part["w_branch_c"], w_branch_c.shape)
    w_out_f = _gather_rows(part["w_out"], w_out.shape)
    w_up_f = _ff_pair_order(_gather_columns(part["w_up"], w_up.shape))
    w_down_f = _gather_rows(part["w_down"], w_down.shape)
    conv_flat = gathered[:, offs[-1]:].reshape(N_DEV, -1)[:, :conv_bits.size].reshape((N_DEV,) + conv_bits.shape)
    conv_w_f = _ff_pair_order(_gather_columns(lax.bitcast_convert_type(conv_flat, F32).reshape(N_DEV, -1), conv_w.shape))
    conv_b_f = _ff_pair_order(conv_b)
    onehot = _rel_onehot()

    saved = []
    cur = xs
    for l in range(L):
        h = _rmsnorm_fwd(cur, norm_mix[l], "norm_mix_fwd")
        qkv = _mm(h, w_qkv[l], "nn", BF16, "proj_qkv")
        uc = _mm(h, w_uc[l], "nn", F32, "proj_pool")
        glog = _mm(h, w_g[l], "nn", F32, "proj_gate")
        heads = _to_heads(qkv)
        bias = _select_mm(rel_bias_a[l], onehot, "nn", "rel_bias_table").reshape(N_HEADS, CHUNK, BAND)
        oa = _from_heads(_attn_a_fwd(heads[0], heads[1], heads[2], bias, q_norm_a[l], k_norm_a[l]))
        ob = _from_heads(_attn_b_fwd(heads[3], heads[4], heads[5]))
        oc = _pool_fwd(uc, w_pool[l], pool_scale[l])
        merged, ya, yb, yc = _merge_fwd(oa, ob, oc, glog, b_gate[l], w_a[l], w_b[l], w_c[l])
        x1 = _mm(merged, w_out_f[l], "nn", F32, "out_proj", res=cur)
        h2 = _rmsnorm_fwd(x1, norm_ffn[l], "norm_ffn_fwd")
        u = _mm(h2, w_up_f[l], "nn", F32, "ff_up")
        act = _ff_act_fwd(u, conv_w_f[l], conv_b_f[l])
        x2 = _mm(act, w_down_f[l], "nn", F32, "ff_down", res=x1)
        saved.append(dict(x=cur, h=h, heads=heads, uc=uc, glog=glog, bias=bias, oa=oa, ob=ob, oc=oc, ya=ya, yb=yb,
                          yc=yc, merged=merged, x1=x1, h2=h2, u=u, act=act))
        cur = x2

    dcur, loss_local = _loss_head(cur, target)
    loss = lax.psum(loss_local, ("x", "y", "c"))

    gw = {n: [None] * L for n in WEIGHTS}
    for l in reversed(range(L)):
        s = saved[l]
        da = _mm(dcur, w_down_f[l], "nt", F32, "ff_down_dx", tn_cap=1408)
        gw["w_down"][l] = _mm(s["act"], dcur, "tn", BF16, "ff_down_dw")
        dc, dconv_w, dconv_b = _ff_act_bwd(s["u"], da, conv_w_f[l], conv_b_f[l])
        du = _ff_conv_bwd(dc, conv_w_f[l])
        dh2 = _mm(du, w_up_f[l], "nt", F32, "ff_up_dx")
        gw["w_up"][l] = _ff_natural_order(_mm(s["h2"], du, "tn", BF16, "ff_up_dw"))
        gw["conv_w"][l] = _ff_natural_order(dconv_w)
        gw["conv_b"][l] = _ff_natural_order(dconv_b)[0]
        dx1, dg = _rmsnorm_bwd(s["x1"], norm_ffn[l], dh2, dcur, "norm_ffn_bwd")
        gw["norm_ffn"][l] = dg[0]

        dmerged = _mm(dx1, w_out_f[l], "nt", F32, "out_proj_dx")
        gw["w_out"][l] = _mm(s["merged"], dx1, "tn", BF16, "out_proj_dw")
        dya, dyb, dyc, dglog, db_gate = _merge_bwd(dmerged, s["glog"], b_gate[l], s["ya"], s["yb"], s["yc"])
        gw["b_gate"][l] = db_gate[0]
        do = {}
        for tag, dy, wk, ok in (("a", dya, w_a, s["oa"]), ("b", dyb, w_b, s["ob"]), ("c", dyc, w_c, s["oc"])):
            do[tag] = _mm(dy, wk[l], "nt", BF16, "branch_dx_" + tag)
            gw["w_branch_" + tag][l] = _mm(ok, dy, "tn", BF16, "branch_dw_" + tag)
        duc, dw_pool, dscale = _pool_bwd(s["uc"], do["c"], w_pool[l], pool_scale[l])
        gw["w_pool"][l] = dw_pool
        gw["pool_scale"][l] = dscale[0]
        hd = s["heads"]
        dqa, dkc, dkp, dvc, dvp, dbias, dgq = _attn_a_bwd(hd[0], hd[1], hd[2], _to_heads(do["a"])[0], s["bias"],
                                                          q_norm_a[l], k_norm_a[l])
        dka, dva, dgk = _attn_a_bwd_keys(hd[1], dkc, dkp, dvc, dvp, k_norm_a[l])
        gw["q_norm_a"][l] = jnp.sum(dgq, axis=(0, 1))
        gw["k_norm_a"][l] = jnp.sum(dgk, axis=(0, 1))
        gw["rel_bias_a"][l] = _select_mm(dbias.reshape(N_HEADS, CHUNK * BAND), onehot, "nt", "rel_bias_table_dw")
        dqb, dkb, dvb = _attn_b_bwd(hd[3], hd[4], hd[5], _to_heads(do["b"])[0])
        dheads = jnp.stack([dqa, dka, dva, dqb, dkb.astype(BF16), dvb.astype(BF16)])
        dqkv = dheads.transpose(2, 0, 1, 3).reshape(T, QKV_COLS)
        dh = _mm(dqkv, w_qkv[l], "nt", F32, "proj_qkv_dx")
        dh = _mm(duc, w_uc[l], "nt", F32, "proj_pool_dx", res=dh)
        dh = _mm(dglog, w_g[l], "nt", F32, "proj_gate_dx", res=dh)
        gw["w_in"][l] = jnp.concatenate([_mm(s["h"], dqkv, "tn", BF16, "proj_qkv_dw"),
                                         _mm(s["h"], duc, "tn", BF16, "proj_pool_dw"),
                                         _mm(s["h"], dglog, "tn", BF16, "proj_gate_dw")], axis=1)
        dcur, dg = _rmsnorm_bwd(s["x"], norm_mix[l], dh, dx1, "norm_mix_bwd")
        gw["norm_mix"][l] = dg[0]
    gw = {n: jnp.stack(g) for n, g in gw.items()}

    cw_pieces = gw["conv_w"].reshape(L, 3, N_DEV, -1).transpose(2, 0, 1, 3)
    cw_bits = lax.bitcast_convert_type(cw_pieces, BF16).reshape(N_DEV, -1)
    cw_rows = -(-cw_bits.shape[1] // (16 * PACK_COLS)) * 16
    cw_bits = jnp.pad(cw_bits, ((0, 0), (0, cw_rows * PACK_COLS - cw_bits.shape[1]))).reshape(N_DEV, cw_rows, PACK_COLS)
    pieces = jnp.concatenate([
        _split_columns(gw["w_in"]), _split_columns(gw["w_branch_a"]), _split_columns(gw["w_branch_b"]),
        _split_columns(gw["w_branch_c"]), _split_rows(gw["w_out"]), _split_columns(gw["w_up"]),
        _split_rows(gw["w_down"]), cw_bits], axis=1)
    parts = _exchange(pieces, True, "exchange_grads")
    small = _pack_rows([gw[n] for n in REPLICATED], SMALL_COLS, 16)
    small_parts = _exchange(small, False, "gather_small_grads")

    shard_like = [w[n] for n in SHARDED]
    res = _adamw(parts, *[_pack_rows([d[n] for n in SHARDED], PACK_COLS, 16) for d in (w, m, v)], "adamw_sharded")
    out = {n: r for n, r in zip(SHARDED, zip(*[_unpack_rows(r, shard_like) for r in res]))}
    rep_like = [w[n] for n in REPLICATED]
    res = _adamw(small_parts, *[_pack_rows([d[n] for n in REPLICATED], SMALL_COLS, 16) for d in (w, m, v)],
                 "adamw_replicated")
    out.update({n: r for n, r in zip(REPLICATED, zip(*[_unpack_rows(r, rep_like) for r in res]))})
    cw_parts = parts[:, parts.shape[1] - cw_rows:].reshape(N_DEV, -1)[:, :2 * conv_w.size].reshape(N_DEV, conv_w.size, 2)
    cw_parts = lax.bitcast_convert_type(cw_parts, F32)
    cw_parts = jnp.stack([_pack_rows([cw_parts[d]], SMALL_COLS, 16) for d in range(N_DEV)])
    res = _adamw(cw_parts, *[_pack_rows([d["conv_w"]], SMALL_COLS, 16) for d in (w, m, v)], "adamw_conv_w")
    out["conv_w"] = tuple(_unpack_rows(r, [conv_w])[0] for r in res)

    grads, deltas, new_m, new_v = ([out[n][i] for n in WEIGHTS] for i in range(4))
    return (loss, dcur.reshape(x.shape), *grads, *deltas, *new_m, *new_v)
```

```python
import functools
import math

import jax
import jax.numpy as jnp
from jax import lax
from jax.experimental import pallas as pl
from jax.experimental.pallas import tpu as pltpu

F32 = jnp.float32
BF16 = jnp.bfloat16

N_DEV = 8
D_MODEL = 1024
N_HEADS = 8
HEAD_DIM = 64
CHUNK = 64
N_LEFT = 8
BAND = (N_LEFT + 1) * CHUNK
WIDTH = N_HEADS * HEAD_DIM
POOL_WINDOWS = (2, 4, 8, 16)
POOL_DIM = 128
MAX_REL = 2 * CHUNK
REL_TABLE = MAX_REL + CHUNK
D_FF = 2816
EPS = 1e-6
SB_BLOCK = 128
SB_ROWS = 256
SB_KEYS = 256
A_BLOCK = N_LEFT * CHUNK
HALO = 16
LANES = 128
VMEM_LIMIT = 56 * 1024 * 1024

ADAM_LR = 0.001
ADAM_B1 = 0.9
ADAM_B2 = 0.999
ADAM_EPS = 1e-08
ADAM_WD = 0.01
ADAM_STEP = 10

MESH = pl.DeviceIdType.MESH


def _params(sem):
    return pltpu.CompilerParams(dimension_semantics=sem, vmem_limit_bytes=VMEM_LIMIT)


def _pick(n, cap):
    if n <= cap:
        return n
    best = None
    for t in range(LANES, cap + 1, LANES):
        if n % t == 0:
            best = t
    assert best is not None, (n, cap)
    return best


def _mm(a, b, mode, out_dtype, name, tm=512, tn_cap=1024, tk_cap=1024, res=None):
    if mode == "nn":
        (M, K), (K2, N) = a.shape, b.shape
    elif mode == "nt":
        (M, K), (N, K2) = a.shape, b.shape
    else:
        (K, M), (K2, N) = a.shape, b.shape
    assert K == K2, (a.shape, b.shape, mode)
    tm = _pick(M, tm)
    tn = _pick(N, tn_cap)
    tk = _pick(K, tk_cap)
    nk = K // tk
    if mode == "nn":
        dims = (((1,), (0,)), ((), ()))
        a_spec = pl.BlockSpec((tm, tk), lambda i, j, k: (i, k))
        b_spec = pl.BlockSpec((tk, tn), lambda i, j, k: (k, j))
    elif mode == "nt":
        dims = (((1,), (1,)), ((), ()))
        a_spec = pl.BlockSpec((tm, tk), lambda i, j, k: (i, k))
        b_spec = pl.BlockSpec((tn, tk), lambda i, j, k: (j, k))
    else:
        dims = (((0,), (0,)), ((), ()))
        a_spec = pl.BlockSpec((tk, tm), lambda i, j, k: (k, i))
        b_spec = pl.BlockSpec((tk, tn), lambda i, j, k: (k, j))

    o_spec = pl.BlockSpec((tm, tn), lambda i, j, k: (i, j))

    def body(a_ref, b_ref, *rest):
        res_ref = rest[0] if res is not None else None
        o_ref, acc_ref = rest[-2:]
        k = pl.program_id(2)

        @pl.when(k == 0)
        def _():
            acc_ref[...] = jnp.zeros_like(acc_ref)

        acc_ref[...] += lax.dot_general(a_ref[...].astype(BF16), b_ref[...].astype(BF16), dims,
                                        preferred_element_type=F32)

        @pl.when(k == nk - 1)
        def _():
            total = acc_ref[...] if res is None else acc_ref[...] + res_ref[...]
            o_ref[...] = total.astype(out_dtype)

    return pl.pallas_call(
        body, name=name,
        grid=(M // tm, N // tn, nk),
        in_specs=[a_spec, b_spec] + ([o_spec] if res is not None else []),
        out_specs=o_spec,
        out_shape=jax.ShapeDtypeStruct((M, N), out_dtype),
        scratch_shapes=[pltpu.VMEM((tm, tn), F32)],
        compiler_params=_params(("parallel", "parallel", "arbitrary")),
    )(*((a, b) if res is None else (a, b, res)))


def _rmsnorm_fwd(x, gain, name, tm=512):
    T, C = x.shape

    def body(x_ref, g_ref, h_ref):
        xv = x_ref[...]
        r = lax.rsqrt(jnp.mean(xv * xv, axis=-1, keepdims=True) + EPS)
        h_ref[...] = (xv * r * g_ref[...]).astype(BF16)

    return pl.pallas_call(
        body, name=name, grid=(T // tm,),
        in_specs=[pl.BlockSpec((tm, C), lambda i: (i, 0)), pl.BlockSpec((1, C), lambda i: (0, 0))],
        out_specs=pl.BlockSpec((tm, C), lambda i: (i, 0)),
        out_shape=jax.ShapeDtypeStruct((T, C), BF16),
        compiler_params=_params(("parallel",)),
    )(x, gain.reshape(1, C))


def _rmsnorm_bwd(x, gain, dh, dres, name, tm=512):
    T, C = x.shape

    def body(x_ref, g_ref, dh_ref, dres_ref, dx_ref, dg_ref):
        @pl.when(pl.program_id(0) == 0)
        def _():
            dg_ref[...] = jnp.zeros_like(dg_ref)

        xv = x_ref[...]
        dy = dh_ref[...].astype(F32)
        r = lax.rsqrt(jnp.mean(xv * xv, axis=-1, keepdims=True) + EPS)
        gdy = dy * g_ref[...]
        inner = jnp.mean(xv * gdy, axis=-1, keepdims=True)
        dx_ref[...] = dres_ref[...] + r * gdy - xv * (r * r * r * inner)
        dg_ref[...] += jnp.sum(dy * xv * r, axis=0, keepdims=True)

    return pl.pallas_call(
        body, name=name, grid=(T // tm,),
        in_specs=[pl.BlockSpec((tm, C), lambda i: (i, 0)), pl.BlockSpec((1, C), lambda i: (0, 0)),
                  pl.BlockSpec((tm, C), lambda i: (i, 0)), pl.BlockSpec((tm, C), lambda i: (i, 0))],
        out_specs=[pl.BlockSpec((tm, C), lambda i: (i, 0)), pl.BlockSpec((1, C), lambda i: (0, 0))],
        out_shape=[jax.ShapeDtypeStruct((T, C), F32), jax.ShapeDtypeStruct((1, C), F32)],
        compiler_params=_params(("arbitrary",)),
    )(x, gain.reshape(1, C), dh, dres)


def _head_norm(t, g):
    tf = t.astype(F32)
    r = lax.rsqrt(jnp.mean(tf * tf, axis=-1, keepdims=True) + EPS)
    return tf * r * g


def _a_logits(qn, kcat, bias, cc, first):
    qc = qn[cc * CHUNK:(cc + 1) * CHUNK]
    kb = kcat[cc * CHUNK:cc * CHUNK + BAND]
    s = lax.dot_general(qc, kb, (((1,), (1,)), ((), ())), preferred_element_type=F32) * (1.0 / math.sqrt(HEAD_DIM))
    s = s + bias
    col = lax.broadcasted_iota(jnp.int32, (CHUNK, BAND), 1) + cc * CHUNK
    return jnp.where(col >= jnp.where(first, A_BLOCK, 0), s, -1e30)


def _softmax(s):
    m = jnp.max(s, axis=-1, keepdims=True)
    e = jnp.exp(s - m)
    return e / jnp.sum(e, axis=-1, keepdims=True)


def _a_specs(T):
    nb = T // A_BLOCK
    cur = pl.BlockSpec((1, A_BLOCK, HEAD_DIM), lambda h, i: (h, i, 0))
    prev = pl.BlockSpec((1, A_BLOCK, HEAD_DIM), lambda h, i: (h, jnp.maximum(i - 1, 0), 0))
    nxt = pl.BlockSpec((1, A_BLOCK, HEAD_DIM), lambda h, i: (h, jnp.minimum(i + 1, nb - 1), 0))
    bias = pl.BlockSpec((1, CHUNK, BAND), lambda h, i: (h, 0, 0))
    gain = pl.BlockSpec((1, HEAD_DIM), lambda h, i: (0, 0))
    hacc = pl.BlockSpec((1, 1, HEAD_DIM), lambda h, i: (h, 0, 0))
    return nb, cur, prev, nxt, bias, gain, hacc


def _attn_a_fwd(q, k, v, bias, gq, gk):
    Hh, T, _ = q.shape
    nb, cur, prev, _, bspec, gspec, _ = _a_specs(T)

    def body(q_ref, kc_ref, kp_ref, vc_ref, vp_ref, b_ref, gq_ref, gk_ref, o_ref):
        first = pl.program_id(1) == 0
        qn = _head_norm(q_ref[0], gq_ref[...]).astype(BF16)
        kcat = jnp.concatenate([_head_norm(kp_ref[0], gk_ref[...]).astype(BF16),
                                _head_norm(kc_ref[0], gk_ref[...]).astype(BF16)], axis=0)
        vcat = jnp.concatenate([vp_ref[0], vc_ref[0]], axis=0)
        bias_v = b_ref[0]
        for cc in range(N_LEFT):
            p = _softmax(_a_logits(qn, kcat, bias_v, cc, first))
            o = jnp.dot(p.astype(BF16), vcat[cc * CHUNK:cc * CHUNK + BAND], preferred_element_type=F32)
            o_ref[0, cc * CHUNK:(cc + 1) * CHUNK, :] = o.astype(BF16)

    return pl.pallas_call(
        body, name="attn_a_fwd", grid=(Hh, nb),
        in_specs=[cur, cur, prev, cur, prev, bspec, gspec, gspec],
        out_specs=cur,
        out_shape=jax.ShapeDtypeStruct((Hh, T, HEAD_DIM), BF16),
        compiler_params=_params(("parallel", "arbitrary")),
    )(q, k, k, v, v, bias, gq.reshape(1, HEAD_DIM), gk.reshape(1, HEAD_DIM))


def _head_norm_bwd(t, g, dn):
    tf = t.astype(F32)
    r = lax.rsqrt(jnp.mean(tf * tf, axis=-1, keepdims=True) + EPS)
    gd = dn * g
    inner = jnp.mean(tf * gd, axis=-1, keepdims=True)
    dt = r * gd - tf * (r * r * r * inner)
    return dt, jnp.sum(dn * tf * r, axis=0, keepdims=True)


def _attn_a_bwd(q, k, v, do, bias, gq, gk):
    Hh, T, _ = q.shape
    nb, cur, prev, _, bspec, gspec, hacc = _a_specs(T)
    scale = 1.0 / math.sqrt(HEAD_DIM)

    def body(q_ref, kc_ref, kp_ref, vc_ref, vp_ref, do_ref, b_ref, gq_ref, gk_ref,
             dq_ref, dkc_ref, dkp_ref, dvc_ref, dvp_ref, db_ref, dgq_ref, dkcat_ref, dvcat_ref):
        i = pl.program_id(1)
        first = i == 0

        @pl.when(first)
        def _():
            db_ref[...] = jnp.zeros_like(db_ref)
            dgq_ref[...] = jnp.zeros_like(dgq_ref)

        qn = _head_norm(q_ref[0], gq_ref[...]).astype(BF16)
        kcat = jnp.concatenate([_head_norm(kp_ref[0], gk_ref[...]).astype(BF16),
                                _head_norm(kc_ref[0], gk_ref[...]).astype(BF16)], axis=0)
        vcat = jnp.concatenate([vp_ref[0], vc_ref[0]], axis=0)
        bias_v = b_ref[0]
        dov = do_ref[0]
        dkcat_ref[...] = jnp.zeros_like(dkcat_ref)
        dvcat_ref[...] = jnp.zeros_like(dvcat_ref)
        dqn_parts = []
        dbias = jnp.zeros((CHUNK, BAND), F32)
        for cc in range(N_LEFT):
            p = _softmax(_a_logits(qn, kcat, bias_v, cc, first))
            doc = dov[cc * CHUNK:(cc + 1) * CHUNK]
            qc = qn[cc * CHUNK:(cc + 1) * CHUNK]
            dp = lax.dot_general(doc, vcat[cc * CHUNK:cc * CHUNK + BAND], (((1,), (1,)), ((), ())),
                                 preferred_element_type=F32)
            delta = jnp.sum(p * dp, axis=-1, keepdims=True)
            ds = p * (dp - delta)
            dbias = dbias + ds
            dsb = (ds * scale).astype(BF16)
            dqn_parts.append(jnp.dot(dsb, kcat[cc * CHUNK:cc * CHUNK + BAND], preferred_element_type=F32))
            dkcat_ref[cc * CHUNK:cc * CHUNK + BAND, :] += lax.dot_general(
                dsb, qc, (((0,), (0,)), ((), ())), preferred_element_type=F32)
            dvcat_ref[cc * CHUNK:cc * CHUNK + BAND, :] += lax.dot_general(
                p.astype(BF16), doc, (((0,), (0,)), ((), ())), preferred_element_type=F32)
        db_ref[0] += dbias
        dqn = jnp.concatenate(dqn_parts, axis=0)
        dq, dg = _head_norm_bwd(q_ref[0], gq_ref[...], dqn)
        dq_ref[0] = dq.astype(BF16)
        dgq_ref[0] += dg
        dkp_ref[0] = dkcat_ref[0:A_BLOCK, :]
        dkc_ref[0] = dkcat_ref[A_BLOCK:2 * A_BLOCK, :]
        dvp_ref[0] = dvcat_ref[0:A_BLOCK, :]
        dvc_ref[0] = dvcat_ref[A_BLOCK:2 * A_BLOCK, :]

    blk_f32 = jax.ShapeDtypeStruct((Hh, T, HEAD_DIM), F32)
    return pl.pallas_call(
        body, name="attn_a_bwd", grid=(Hh, nb),
        in_specs=[cur, cur, prev, cur, prev, cur, bspec, gspec, gspec],
        out_specs=[cur, cur, cur, cur, cur, bspec, hacc],
        out_shape=[jax.ShapeDtypeStruct((Hh, T, HEAD_DIM), BF16), blk_f32, blk_f32, blk_f32, blk_f32,
                   jax.ShapeDtypeStruct((Hh, CHUNK, BAND), F32), jax.ShapeDtypeStruct((Hh, 1, HEAD_DIM), F32)],
        scratch_shapes=[pltpu.VMEM((2 * A_BLOCK, HEAD_DIM), F32), pltpu.VMEM((2 * A_BLOCK, HEAD_DIM), F32)],
        compiler_params=_params(("parallel", "arbitrary")),
    )(q, k, k, v, v, do, bias, gq.reshape(1, HEAD_DIM), gk.reshape(1, HEAD_DIM))


def _attn_a_bwd_keys(k, dkc, dkp, dvc, dvp, gk):
    Hh, T, _ = k.shape
    nb, cur, _, nxt, _, gspec, hacc = _a_specs(T)

    def body(k_ref, dkc_ref, dkp_ref, dvc_ref, dvp_ref, gk_ref, dk_ref, dv_ref, dgk_ref):
        i = pl.program_id(1)

        @pl.when(i == 0)
        def _():
            dgk_ref[...] = jnp.zeros_like(dgk_ref)

        has_next = (i < nb - 1).astype(F32)
        dkn = dkc_ref[0] + has_next * dkp_ref[0]
        dk, dg = _head_norm_bwd(k_ref[0], gk_ref[...], dkn)
        dk_ref[0] = dk.astype(BF16)
        dv_ref[0] = (dvc_ref[0] + has_next * dvp_ref[0]).astype(BF16)
        dgk_ref[0] += dg

    blk = jax.ShapeDtypeStruct((Hh, T, HEAD_DIM), BF16)
    return pl.pallas_call(
        body, name="attn_a_bwd_keys", grid=(Hh, nb),
        in_specs=[cur, cur, nxt, cur, nxt, gspec],
        out_specs=[cur, cur, hacc],
        out_shape=[blk, blk, jax.ShapeDtypeStruct((Hh, 1, HEAD_DIM), F32)],
        compiler_params=_params(("parallel", "arbitrary")),
    )(k, dkc, dkp, dvc, dvp, gk.reshape(1, HEAD_DIM))


def _split(x):
    hi = x.astype(BF16)
    return hi, (x - hi.astype(F32)).astype(BF16)


def _dot_split(x, m):
    hi, lo = _split(x)
    return jnp.dot(hi, m, preferred_element_type=F32) + jnp.dot(lo, m, preferred_element_type=F32)


def _scan_matrix(later):
    r = lax.broadcasted_iota(jnp.int32, (SB_BLOCK, 2 * SB_BLOCK), 0)
    c = lax.broadcasted_iota(jnp.int32, (SB_BLOCK, 2 * SB_BLOCK), 1)
    tri = (r > c) if later else (r < c)
    return jnp.where(jnp.logical_or(c >= SB_BLOCK, tri), 1.0, 0.0).astype(BF16)


def _running_sums(x, carry, scan, later):
    n = SB_KEYS // SB_BLOCK
    parts = [None] * n
    total = None
    for sb in (reversed(range(n)) if later else range(n)):
        sums = _dot_split(x[:, sb * SB_BLOCK:(sb + 1) * SB_BLOCK], scan)
        local = sums[:, :SB_BLOCK]
        if total is not None:
            local = local + total
        if carry is not None:
            local = local + carry
        parts[sb] = local
        total = sums[:, SB_BLOCK:] if total is None else total + sums[:, SB_BLOCK:]
    return jnp.concatenate(parts, axis=1), total


def _sb_log_keep(z):
    a = jnp.abs(z)
    return -0.5 * (z + a) - jnp.log(1.0 + jnp.exp(-a))


def _sb_mask():
    r = lax.broadcasted_iota(jnp.int32, (2 * SB_ROWS, SB_KEYS), 0)
    c = lax.broadcasted_iota(jnp.int32, (2 * SB_ROWS, SB_KEYS), 1)
    return c < jnp.where(r >= SB_ROWS, r - SB_ROWS, r)


def _stack_heads(t):
    lane = lax.broadcasted_iota(jnp.int32, t.shape, 1)
    zero = jnp.zeros_like(t)
    return jnp.concatenate([jnp.where(lane < HEAD_DIM, t, zero), jnp.where(lane >= HEAD_DIM, t, zero)], axis=0)


def _unstack_heads(t):
    lane = lax.broadcasted_iota(jnp.int32, (SB_ROWS, 2 * HEAD_DIM), 1)
    return jnp.where(lane < HEAD_DIM, t[:SB_ROWS], t[SB_ROWS:])


def _sb_specs(T):
    nq = T // SB_ROWS
    blk = lambda col: pl.BlockSpec((SB_ROWS, 2 * HEAD_DIM), lambda p, i: (i, col + p))
    full = lambda col: pl.BlockSpec((T, 2 * HEAD_DIM), lambda p, i: (0, col + p))
    return nq, blk, full


def _key_rows(j):
    return pl.ds(pl.multiple_of(j * SB_KEYS, SB_KEYS), SB_KEYS)


def _attn_b_fwd(qkv):
    T = qkv.shape[0]
    pairs = N_HEADS // 2
    nq, blk, full = _sb_specs(T)
    scale = 1.0 / math.sqrt(HEAD_DIM)

    assert nq <= LANES

    def body(q_ref, k_ref, v_ref, o_ref, c_ref, acc_ref, carry_ref):
        i = pl.program_id(1)
        scan = _scan_matrix(True)
        qst = _stack_heads((q_ref[...].astype(F32) * scale).astype(BF16))
        lane = lax.broadcasted_iota(jnp.int32, (2 * SB_ROWS, LANES), 1)

        def span(j, carry, mask):
            z = lax.dot_general(qst, k_ref[_key_rows(j), :], (((1,), (1,)), ((), ())), preferred_element_type=F32)
            keep = _sb_log_keep(z)
            if mask is not None:
                keep = jnp.where(mask, keep, 0.0)
            tail, total = _running_sums(keep, carry, scan, True)
            w = jnp.exp(z + keep + tail)
            if mask is not None:
                w = jnp.where(mask, w, 0.0)
            return jnp.dot(w.astype(BF16), v_ref[_key_rows(j), :], preferred_element_type=F32), total

        acc_ref[...], carry_ref[...] = span(i, None, _sb_mask())
        c_ref[0, 0] = jnp.zeros((2 * SB_ROWS, LANES), F32)

        @pl.loop(0, i)
        def _(jj):
            j = i - 1 - jj
            carry = carry_ref[...]
            c_ref[0, 0] = jnp.where(lane == j, carry, c_ref[0, 0])
            out, total = span(j, carry, None)
            acc_ref[...] += out
            carry_ref[...] = carry + total

        o_ref[...] = _unstack_heads(acc_ref[...]).astype(BF16)

    return pl.pallas_call(
        body, name="attn_b_fwd", grid=(pairs, nq),
        in_specs=[blk(3 * pairs), full(4 * pairs), full(5 * pairs)],
        out_specs=[pl.BlockSpec((SB_ROWS, 2 * HEAD_DIM), lambda p, i: (i, p)),
                   pl.BlockSpec((1, 1, 2 * SB_ROWS, LANES), lambda p, i: (p, i, 0, 0))],
        out_shape=[jax.ShapeDtypeStruct((T, WIDTH), BF16), jax.ShapeDtypeStruct((pairs, nq, 2 * SB_ROWS, LANES), F32)],
        scratch_shapes=[pltpu.VMEM((2 * SB_ROWS, 2 * HEAD_DIM), F32), pltpu.VMEM((2 * SB_ROWS, SB_BLOCK), F32)],
        compiler_params=_params(("parallel", "arbitrary")),
    )(qkv, qkv, qkv)


def _attn_b_bwd(qkv, carries, do):
    T = qkv.shape[0]
    pairs = N_HEADS // 2
    nq, blk, full = _sb_specs(T)
    scale = 1.0 / math.sqrt(HEAD_DIM)
    oblk = pl.BlockSpec((SB_ROWS, 2 * HEAD_DIM), lambda p, i: (i, p))
    ofull = pl.BlockSpec((T, 2 * HEAD_DIM), lambda p, i: (0, p))

    def body(q_ref, k_ref, v_ref, c_ref, do_ref, dq_ref, dk_ref, dv_ref, dqacc_ref, before_ref):
        i = pl.program_id(1)

        @pl.when(i == 0)
        def _():
            dk_ref[...] = jnp.zeros_like(dk_ref)
            dv_ref[...] = jnp.zeros_like(dv_ref)

        scan_later = _scan_matrix(True)
        scan_earlier = _scan_matrix(False)
        qst = _stack_heads((q_ref[...].astype(F32) * scale).astype(BF16))
        dost = _stack_heads(do_ref[...].astype(BF16))
        lane = lax.broadcasted_iota(jnp.int32, (2 * SB_ROWS, LANES), 1)

        def span(j, later, mask):
            kv = k_ref[_key_rows(j), :]
            vv = v_ref[_key_rows(j), :]
            z = lax.dot_general(qst, kv, (((1,), (1,)), ((), ())), preferred_element_type=F32)
            keep = _sb_log_keep(z)
            sig = jnp.exp(z + keep)
            if mask is not None:
                keep = jnp.where(mask, keep, 0.0)
            tail, _ = _running_sums(keep, later, scan_later, True)
            w = jnp.exp(z + keep + tail)
            if mask is not None:
                w = jnp.where(mask, w, 0.0)
            g = w * lax.dot_general(dost, vv, (((1,), (1,)), ((), ())), preferred_element_type=F32)
            before, total_g = _running_sums(g, before_ref[...], scan_earlier, False)
            dz = g - sig * (g + before)
            if mask is not None:
                dz = jnp.where(mask, dz, 0.0)
            dzb = dz.astype(BF16)
            dqacc_ref[...] += jnp.dot(dzb, kv, preferred_element_type=F32)
            dk_ref[_key_rows(j), :] += lax.dot_general(dzb, qst, (((0,), (0,)), ((), ())), preferred_element_type=F32)
            dv_ref[_key_rows(j), :] += lax.dot_general(w.astype(BF16), dost, (((0,), (0,)), ((), ())),
                                                       preferred_element_type=F32)
            before_ref[...] += total_g

        dqacc_ref[...] = jnp.zeros_like(dqacc_ref)
        before_ref[...] = jnp.zeros_like(before_ref)

        @pl.loop(0, i)
        def _(j):
            later = jnp.sum(jnp.where(lane == j, c_ref[0, 0], 0.0), axis=-1, keepdims=True)
            span(j, later, None)

        span(i, None, _sb_mask())
        dq_ref[...] = (_unstack_heads(dqacc_ref[...]) * scale).astype(BF16)

    wide = jax.ShapeDtypeStruct((T, WIDTH), F32)
    return pl.pallas_call(
        body, name="attn_b_bwd", grid=(pairs, nq),
        in_specs=[blk(3 * pairs), full(4 * pairs), full(5 * pairs),
                  pl.BlockSpec((1, 1, 2 * SB_ROWS, LANES), lambda p, i: (p, i, 0, 0)), oblk],
        out_specs=[oblk, ofull, ofull],
        out_shape=[jax.ShapeDtypeStruct((T, WIDTH), BF16), wide, wide],
        scratch_shapes=[pltpu.VMEM((2 * SB_ROWS, 2 * HEAD_DIM), F32), pltpu.VMEM((2 * SB_ROWS, SB_BLOCK), F32)],
        compiler_params=_params(("parallel", "arbitrary")),
    )(qkv, qkv, qkv, carries, do)


def _window_sums(ext, forward):
    n = ext.shape[0]
    out = []
    s = ext
    for step in (1, 2, 4, 8):
        s = s + pltpu.roll(s, (n - step) if forward else step, 0)
        out.append(s)
    return out


def _pool_counts(base, rows, win):
    t = base + lax.broadcasted_iota(jnp.int32, (rows, 1), 0)
    return jnp.minimum(t + 1, win).astype(F32)


def _pooled(u_ref, up_ref, i, tm):
    prev = jnp.where(i > 0, up_ref[...], 0.0)
    ext = jnp.concatenate([prev, u_ref[...]], axis=0)
    sums = _window_sums(ext, False)
    parts = []
    for g, win in enumerate(POOL_WINDOWS):
        cols = slice(g * POOL_DIM, (g + 1) * POOL_DIM)
        cnt = _pool_counts(i * tm, tm, win)
        parts.append(sums[g][HALO:, cols] / cnt - ext[HALO:, cols])
    return parts


def _pool_fwd(ucg, w_pool, scale, tm=512):
    T = ucg.shape[0]
    C = WIDTH

    def body(u_ref, up_ref, w_ref, s_ref, o_ref):
        i = pl.program_id(0)
        parts = _pooled(u_ref, up_ref, i, tm)
        for g in range(len(POOL_WINDOWS)):
            mixed = jnp.dot(parts[g].astype(BF16), w_ref[g], preferred_element_type=F32)
            o_ref[:, g * POOL_DIM:(g + 1) * POOL_DIM] = (mixed * s_ref[:, g * POOL_DIM:(g + 1) * POOL_DIM]).astype(BF16)

    return pl.pallas_call(
        body, name="pool_fwd", grid=(T // tm,),
        in_specs=[pl.BlockSpec((tm, C), lambda i: (i, 0)),
                  pl.BlockSpec((HALO, C), lambda i: (jnp.maximum(i * (tm // HALO) - 1, 0), 0)),
                  pl.BlockSpec((len(POOL_WINDOWS), POOL_DIM, POOL_DIM), lambda i: (0, 0, 0)),
                  pl.BlockSpec((1, C), lambda i: (0, 0))],
        out_specs=pl.BlockSpec((tm, C), lambda i: (i, 0)),
        out_shape=jax.ShapeDtypeStruct((T, C), BF16),
        compiler_params=_params(("parallel",)),
    )(ucg, ucg, w_pool.astype(BF16), scale.reshape(1, C))


def _pool_bwd(ucg, do_c, w_pool, scale, tm=512):
    T = ucg.shape[0]
    C = WIDTH
    nt = T // tm
    G = len(POOL_WINDOWS)

    def body(u_ref, up_ref, do_ref, don_ref, w_ref, s_ref, du_ref, dw_ref, ds_ref):
        i = pl.program_id(0)

        @pl.when(i == 0)
        def _():
            dw_ref[...] = jnp.zeros_like(dw_ref)
            ds_ref[...] = jnp.zeros_like(ds_ref)

        parts = _pooled(u_ref, up_ref, i, tm)
        nxt = jnp.where(i < nt - 1, don_ref[...].astype(F32), 0.0)
        do_ext = jnp.concatenate([do_ref[...].astype(F32), nxt], axis=0) * s_ref[...]
        for g, win in enumerate(POOL_WINDOWS):
            cols = slice(g * POOL_DIM, (g + 1) * POOL_DIM)
            pooled_b = parts[g].astype(BF16)
            dmix = do_ext[:, cols].astype(BF16)
            mixed = jnp.dot(pooled_b, w_ref[g], preferred_element_type=F32)
            ds_ref[:, cols] += jnp.sum(do_ref[:, cols].astype(F32) * mixed, axis=0, keepdims=True)
            dw_ref[g] += lax.dot_general(pooled_b, dmix[:tm], (((0,), (0,)), ((), ())), preferred_element_type=F32)
            dpool = lax.dot_general(dmix, w_ref[g], (((1,), (1,)), ((), ())), preferred_element_type=F32)
            scaled = dpool / _pool_counts(i * tm, tm + HALO, win)
            fwd = _window_sums(scaled, True)[g]
            du_ref[:, cols] = (fwd[:tm] - dpool[:tm]).astype(BF16)

    return pl.pallas_call(
        body, name="pool_bwd", grid=(nt,),
        in_specs=[pl.BlockSpec((tm, C), lambda i: (i, 0)),
                  pl.BlockSpec((HALO, C), lambda i: (jnp.maximum(i * (tm // HALO) - 1, 0), 0)),
                  pl.BlockSpec((tm, C), lambda i: (i, 0)),
                  pl.BlockSpec((HALO, C), lambda i: (jnp.minimum((i + 1) * (tm // HALO), T // HALO - 1), 0)),
                  pl.BlockSpec((G, POOL_DIM, POOL_DIM), lambda i: (0, 0, 0)),
                  pl.BlockSpec((1, C), lambda i: (0, 0))],
        out_specs=[pl.BlockSpec((tm, C), lambda i: (i, 0)),
                   pl.BlockSpec((G, POOL_DIM, POOL_DIM), lambda i: (0, 0, 0)),
                   pl.BlockSpec((1, C), lambda i: (0, 0))],
        out_shape=[jax.ShapeDtypeStruct((T, C), BF16), jax.ShapeDtypeStruct((G, POOL_DIM, POOL_DIM), F32),
                   jax.ShapeDtypeStruct((1, C), F32)],
        compiler_params=_params(("arbitrary",)),
    )(ucg, ucg, do_c, do_c, w_pool.astype(BF16), scale.reshape(1, C))


def _merge_fwd(oa, ob, oc, glog, b_gate, wa, wb, wc, tm=256):
    T = oa.shape[0]
    Dm = D_MODEL
    row = lambda c: pl.BlockSpec((tm, c), lambda i: (i, 0))
    wspec = pl.BlockSpec((WIDTH, Dm), lambda i: (0, 0))

    def body(oa_ref, ob_ref, oc_ref, g_ref, b_ref, wa_ref, wb_ref, wc_ref, m_ref, ya_ref, yb_ref, yc_ref):
        merged = jnp.zeros((tm, Dm), F32)
        for kk, (o_ref, w_ref, y_ref) in enumerate(((oa_ref, wa_ref, ya_ref), (ob_ref, wb_ref, yb_ref),
                                                    (oc_ref, wc_ref, yc_ref))):
            y = jnp.dot(o_ref[...].astype(BF16), w_ref[...], preferred_element_type=F32)
            gate = jax.nn.sigmoid(g_ref[:, kk * Dm:(kk + 1) * Dm] + b_ref[:, kk * Dm:(kk + 1) * Dm])
            merged = merged + gate * y
            y_ref[...] = y.astype(BF16)
        m_ref[...] = merged.astype(BF16)

    out = jax.ShapeDtypeStruct((T, Dm), BF16)
    return pl.pallas_call(
        body, name="merge_fwd", grid=(T // tm,),
        in_specs=[row(WIDTH), row(WIDTH), row(WIDTH), row(3 * Dm), pl.BlockSpec((1, 3 * Dm), lambda i: (0, 0)),
                  wspec, wspec, wspec],
        out_specs=[row(Dm)] * 4,
        out_shape=[out] * 4,
        compiler_params=_params(("parallel",)),
    )(oa, ob, oc, glog, b_gate.reshape(1, 3 * Dm), wa, wb, wc)


def _merge_bwd(dmerged, glog, b_gate, ya, yb, yc, tm=256):
    T = dmerged.shape[0]
    Dm = D_MODEL
    row = lambda c: pl.BlockSpec((tm, c), lambda i: (i, 0))

    def body(dm_ref, g_ref, b_ref, ya_ref, yb_ref, yc_ref, dya_ref, dyb_ref, dyc_ref, dg_ref, db_ref):
        @pl.when(pl.program_id(0) == 0)
        def _():
            db_ref[...] = jnp.zeros_like(db_ref)

        dm = dm_ref[...]
        for kk, (y_ref, dy_ref) in enumerate(((ya_ref, dya_ref), (yb_ref, dyb_ref), (yc_ref, dyc_ref))):
            cols = slice(kk * Dm, (kk + 1) * Dm)
            gate = jax.nn.sigmoid(g_ref[:, cols] + b_ref[:, cols])
            dy_ref[...] = (dm * gate).astype(BF16)
            dlog = dm * y_ref[...].astype(F32) * gate * (1.0 - gate)
            dg_ref[:, cols] = dlog.astype(BF16)
            db_ref[:, cols] += jnp.sum(dlog, axis=0, keepdims=True)

    out = jax.ShapeDtypeStruct((T, Dm), BF16)
    return pl.pallas_call(
        body, name="merge_bwd", grid=(T // tm,),
        in_specs=[row(Dm), row(3 * Dm), pl.BlockSpec((1, 3 * Dm), lambda i: (0, 0)), row(Dm), row(Dm), row(Dm)],
        out_specs=[row(Dm), row(Dm), row(Dm), row(3 * Dm), pl.BlockSpec((1, 3 * Dm), lambda i: (0, 0))],
        out_shape=[out, out, out, jax.ShapeDtypeStruct((T, 3 * Dm), BF16), jax.ShapeDtypeStruct((1, 3 * Dm), F32)],
        compiler_params=_params(("arbitrary",)),
    )(dmerged, glog, b_gate.reshape(1, 3 * Dm), ya, yb, yc)


def _residual_add(x, y, name, tm=512):
    T, C = x.shape

    def body(x_ref, y_ref, o_ref):
        o_ref[...] = x_ref[...] + y_ref[...]

    spec = pl.BlockSpec((tm, C), lambda i: (i, 0))
    return pl.pallas_call(body, name=name, grid=(T // tm,), in_specs=[spec, spec], out_specs=spec,
                          out_shape=jax.ShapeDtypeStruct((T, C), F32), compiler_params=_params(("parallel",)))(x, y)


FF_TILE = 256
FF_TILES = D_FF // FF_TILE
CONV_HALO = 8


def _ff_pair_order(w):
    lead = w.shape[:-1]
    n = len(lead)
    w = w.reshape(*lead, 2, FF_TILES, FF_TILE)
    return jnp.swapaxes(w, n, n + 1).reshape(*lead, 2 * D_FF)


def _ff_natural_order(w):
    lead = w.shape[:-1]
    n = len(lead)
    w = w.reshape(*lead, FF_TILES, 2, FF_TILE)
    return jnp.swapaxes(w, n, n + 1).reshape(*lead, 2 * D_FF)


def _conv(ext, w_ref, b_ref):
    c = b_ref[...] + w_ref[2:3, :] * ext
    c = c + w_ref[1:2, :] * pltpu.roll(ext, 1, 0)
    c = c + w_ref[0:1, :] * pltpu.roll(ext, 2, 0)
    return c[CONV_HALO:]


def _ff_specs(T, tm):
    pair = pl.BlockSpec((tm, 2 * FF_TILE), lambda i, j: (i, j))
    prev = pl.BlockSpec((CONV_HALO, 2 * FF_TILE), lambda i, j: (jnp.maximum(i * (tm // CONV_HALO) - 1, 0), j))
    nxt = pl.BlockSpec((CONV_HALO, 2 * FF_TILE),
                       lambda i, j: (jnp.minimum((i + 1) * (tm // CONV_HALO), T // CONV_HALO - 1), j))
    half = pl.BlockSpec((tm, FF_TILE), lambda i, j: (i, j))
    small = lambda r: pl.BlockSpec((r, 2 * FF_TILE), lambda i, j: (0, j))
    return pair, prev, nxt, half, small


def _swap_grid(spec):
    return pl.BlockSpec(spec.block_shape, lambda j, i, f=spec.index_map: f(i, j))


def _ff_act_fwd(u, conv_w, conv_b, tm=512):
    T = u.shape[0]
    pair, prev, _, half, small = _ff_specs(T, tm)

    def body(u_ref, p_ref, w_ref, b_ref, a_ref):
        i = pl.program_id(0)
        c = _conv(jnp.concatenate([jnp.where(i > 0, p_ref[...], 0.0), u_ref[...]], axis=0), w_ref, b_ref)
        cg, cv = c[:, :FF_TILE], c[:, FF_TILE:]
        a_ref[...] = (cg * jax.nn.sigmoid(cg) * cv).astype(BF16)

    return pl.pallas_call(
        body, name="ff_act_fwd", grid=(T // tm, FF_TILES),
        in_specs=[pair, prev, small(3), small(1)],
        out_specs=half,
        out_shape=jax.ShapeDtypeStruct((T, D_FF), BF16),
        compiler_params=_params(("parallel", "parallel")),
    )(u, u, conv_w, conv_b.reshape(1, -1))


def _ff_act_bwd(u, da, conv_w, conv_b, tm=512):
    T = u.shape[0]
    pair, prev, _, half, small = _ff_specs(T, tm)

    def body(u_ref, p_ref, da_ref, w_ref, b_ref, dc_ref, dw_ref, db_ref):
        i = pl.program_id(1)

        @pl.when(i == 0)
        def _():
            dw_ref[...] = jnp.zeros_like(dw_ref)
            db_ref[...] = jnp.zeros_like(db_ref)

        ext = jnp.concatenate([jnp.where(i > 0, p_ref[...], 0.0), u_ref[...]], axis=0)
        c = _conv(ext, w_ref, b_ref)
        cg, cv = c[:, :FF_TILE], c[:, FF_TILE:]
        da = da_ref[...]
        sg = jax.nn.sigmoid(cg)
        dc = jnp.concatenate([da * cv * sg * (1.0 + cg * (1.0 - sg)), da * cg * sg], axis=1)
        dc_ref[...] = dc
        db_ref[...] += jnp.sum(dc, axis=0, keepdims=True)
        dw_ref[2:3, :] += jnp.sum(dc * ext[CONV_HALO:], axis=0, keepdims=True)
        dw_ref[1:2, :] += jnp.sum(dc * pltpu.roll(ext, 1, 0)[CONV_HALO:], axis=0, keepdims=True)
        dw_ref[0:1, :] += jnp.sum(dc * pltpu.roll(ext, 2, 0)[CONV_HALO:], axis=0, keepdims=True)

    return pl.pallas_call(
        body, name="ff_act_bwd", grid=(FF_TILES, T // tm),
        in_specs=[_swap_grid(pair), _swap_grid(prev), _swap_grid(half), _swap_grid(small(3)), _swap_grid(small(1))],
        out_specs=[_swap_grid(pair), _swap_grid(small(3)), _swap_grid(small(1))],
        out_shape=[jax.ShapeDtypeStruct((T, 2 * D_FF), F32), jax.ShapeDtypeStruct((3, 2 * D_FF), F32),
                   jax.ShapeDtypeStruct((1, 2 * D_FF), F32)],
        compiler_params=_params(("parallel", "arbitrary")),
    )(u, u, da, conv_w, conv_b.reshape(1, -1))


def _ff_conv_bwd(dc, conv_w, tm=512):
    T = dc.shape[0]
    nt = T // tm
    pair, _, nxt, _, small = _ff_specs(T, tm)

    def body(dc_ref, n_ref, w_ref, du_ref):
        i = pl.program_id(0)
        ext = jnp.concatenate([dc_ref[...], jnp.where(i < nt - 1, n_ref[...], 0.0)], axis=0)
        n = tm + CONV_HALO
        du = w_ref[2:3, :] * ext + w_ref[1:2, :] * pltpu.roll(ext, n - 1, 0) + w_ref[0:1, :] * pltpu.roll(ext, n - 2, 0)
        du_ref[...] = du[:tm].astype(BF16)

    return pl.pallas_call(
        body, name="ff_conv_bwd", grid=(nt, FF_TILES),
        in_specs=[pair, nxt, small(3)],
        out_specs=pair,
        out_shape=jax.ShapeDtypeStruct((T, 2 * D_FF), BF16),
        compiler_params=_params(("parallel", "parallel")),
    )(dc, dc, conv_w)


def _loss_head(y, target, tm=512):
    T, C = y.shape
    nt = T // tm

    def body(y_ref, t_ref, dy_ref, l_ref):
        err = y_ref[...] - t_ref[...]
        dy_ref[...] = err * (1.0 / C)
        part = jnp.sum(err * err, axis=0, keepdims=True) * (0.5 / C)
        l_ref[0] = jnp.broadcast_to(part, (8, C))

    spec = pl.BlockSpec((tm, C), lambda i: (i, 0))
    dy, parts = pl.pallas_call(
        body, name="loss_head", grid=(nt,),
        in_specs=[spec, spec],
        out_specs=[spec, pl.BlockSpec((1, 8, C), lambda i: (i, 0, 0))],
        out_shape=[jax.ShapeDtypeStruct((T, C), F32), jax.ShapeDtypeStruct((nt, 8, C), F32)],
        compiler_params=_params(("parallel",)),
    )(y, target)
    return dy, jnp.sum(parts[:, 0, :])


def _adamw_math(w, g, m, v):
    m = ADAM_B1 * m + (1.0 - ADAM_B1) * g
    v = ADAM_B2 * v + (1.0 - ADAM_B2) * (g * g)
    m_hat = m / (1.0 - ADAM_B1 ** ADAM_STEP)
    v_hat = v / (1.0 - ADAM_B2 ** ADAM_STEP)
    delta = -ADAM_LR * (m_hat / (jnp.sqrt(v_hat) + ADAM_EPS) + ADAM_WD * w)
    return delta, m, v


def _adamw(parts, w, m, v, name, tm=256):
    R, C = w.shape
    tm = _pick_rows(R, tm)

    def body(p_ref, w_ref, m_ref, v_ref, g_ref, d_ref, nm_ref, nv_ref):
        g = p_ref[0].astype(F32)
        for s in range(1, N_DEV):
            g = g + p_ref[s].astype(F32)
        delta, nm, nv = _adamw_math(w_ref[...], g, m_ref[...], v_ref[...])
        g_ref[...] = g
        d_ref[...] = delta
        nm_ref[...] = nm
        nv_ref[...] = nv

    spec = pl.BlockSpec((tm, C), lambda i: (i, 0))
    out = jax.ShapeDtypeStruct((R, C), F32)
    return pl.pallas_call(
        body, name=name, grid=(R // tm,),
        in_specs=[pl.BlockSpec((N_DEV, tm, C), lambda i: (0, i, 0)), spec, spec, spec],
        out_specs=[spec] * 4,
        out_shape=[out] * 4,
        compiler_params=_params(("parallel",)),
    )(parts, w, m, v)


def _pick_rows(n, cap):
    best = None
    for t in range(16, min(n, cap) + 1, 16):
        if n % t == 0:
            best = t
    assert best is not None, (n, cap)
    return best


def _exchange(src, scatter, name):
    slab = src.shape[1:] if scatter else src.shape

    def body(src_ref, out_ref, send_sems, recv_sems, local_sem):
        x, y, c = lax.axis_index("x"), lax.axis_index("y"), lax.axis_index("c")
        me = 4 * x + 2 * y + c

        def piece(d):
            return src_ref.at[d] if scatter else src_ref

        mine = pltpu.make_async_copy(piece(me), out_ref.at[me], local_sem)
        mine.start()
        copies = []
        for k in range(1, N_DEV):
            px = 1 - x if k & 4 else x
            py = 1 - y if k & 2 else y
            pc = 1 - c if k & 1 else c
            peer = 4 * px + 2 * py + pc
            cp = pltpu.make_async_remote_copy(
                src_ref=piece(peer), dst_ref=out_ref.at[me],
                send_sem=send_sems.at[k], recv_sem=recv_sems.at[k],
                device_id=(px, py, pc), device_id_type=MESH)
            cp.start()
            copies.append((cp, peer))
        for k, (cp, peer) in enumerate(copies, start=1):
            cp.wait_send()
            pltpu.make_async_remote_copy(
                src_ref=piece(peer), dst_ref=out_ref.at[peer],
                send_sem=send_sems.at[k], recv_sem=recv_sems.at[k],
                device_id=(x, y, c), device_id_type=MESH).wait_recv()
        mine.wait()

    return pl.pallas_call(
        body, name=name,
        in_specs=[pl.BlockSpec(memory_space=pl.ANY)],
        out_specs=pl.BlockSpec(memory_space=pl.ANY),
        out_shape=jax.ShapeDtypeStruct((N_DEV,) + tuple(slab), src.dtype),
        scratch_shapes=[pltpu.SemaphoreType.DMA((N_DEV,)), pltpu.SemaphoreType.DMA((N_DEV,)),
                        pltpu.SemaphoreType.DMA],
    )(src)


SHARDED = ("w_in", "w_branch_a", "w_branch_b", "w_branch_c", "w_out", "w_up", "w_down")
REPLICATED = ("norm_mix", "b_gate", "q_norm_a", "k_norm_a", "rel_bias_a", "w_pool", "pool_scale", "norm_ffn", "conv_b")
WEIGHTS = ("norm_mix", "w_in", "b_gate", "q_norm_a", "k_norm_a", "rel_bias_a", "w_pool", "pool_scale",
           "w_branch_a", "w_branch_b", "w_branch_c", "w_out", "norm_ffn", "w_up", "conv_w", "conv_b", "w_down")
PACK_COLS = 1024
SMALL_COLS = 128
QKV_COLS = 6 * WIDTH


def _rel_index():
    q_off = jnp.arange(CHUNK)[:, None] + N_LEFT * CHUNK
    k_off = jnp.arange(BAND)[None, :]
    return jnp.clip(q_off - k_off, -(CHUNK - 1), MAX_REL) + (CHUNK - 1)


def _rel_onehot():
    rel = _rel_index().reshape(1, CHUNK * BAND)
    return (rel == jnp.arange(REL_TABLE)[:, None]).astype(BF16)


def _select_mm(x, onehot, mode, name):
    hi = x.astype(BF16)
    r1 = x - hi.astype(F32)
    mid = r1.astype(BF16)
    lo = (r1 - mid.astype(F32)).astype(BF16)
    y = _mm(jnp.concatenate([hi, mid, lo, jnp.zeros_like(hi)], axis=0), onehot, mode, F32, name)
    n = x.shape[0]
    return y[:n] + y[n:2 * n] + y[2 * n:3 * n]


def _pack_rows(arrays, cols, row_multiple):
    flat = jnp.concatenate([a.reshape(-1) for a in arrays])
    rows = -(-flat.shape[0] // cols)
    rows = -(-rows // row_multiple) * row_multiple
    return jnp.pad(flat, (0, rows * cols - flat.shape[0])).reshape(rows, cols)


def _unpack_rows(packed, like):
    flat = packed.reshape(-1)
    out, off = [], 0
    for a in like:
        out.append(flat[off:off + a.size].reshape(a.shape))
        off += a.size
    return out


def _to_heads(t):
    T = t.shape[0]
    return t.reshape(T, -1, N_HEADS, HEAD_DIM).transpose(1, 2, 0, 3)


def _from_heads(t):
    return t.transpose(1, 0, 2).reshape(t.shape[1], WIDTH)


def _gather_columns(g, shape):
    L, rows, cols = shape
    return g.reshape(N_DEV, L, rows, cols).transpose(1, 2, 0, 3).reshape(L, rows, N_DEV * cols)


def _gather_rows(g, shape):
    L, rows, cols = shape
    return g.reshape(N_DEV, L, rows, cols).transpose(1, 0, 2, 3).reshape(L, N_DEV * rows, cols)


def _split_columns(full):
    L, rows, allc = full.shape
    cols = allc // N_DEV
    return full.reshape(L, rows, N_DEV, cols).transpose(2, 0, 1, 3).reshape(N_DEV, -1, PACK_COLS)


def _split_rows(full):
    L, allr, cols = full.shape
    rows = allr // N_DEV
    return full.reshape(L, N_DEV, rows, cols).transpose(1, 0, 2, 3).reshape(N_DEV, -1, PACK_COLS)


def kernel(x, norm_mix, w_in, b_gate, q_norm_a, k_norm_a, rel_bias_a, w_pool, pool_scale, w_branch_a, w_branch_b, w_branch_c, w_out, norm_ffn, w_up, conv_w, conv_b, w_down, loss_target, m_norm_mix, m_w_in, m_b_gate, m_q_norm_a, m_k_norm_a, m_rel_bias_a, m_w_pool, m_pool_scale, m_w_branch_a, m_w_branch_b, m_w_branch_c, m_w_out, m_norm_ffn, m_w_up, m_conv_w, m_conv_b, m_w_down, v_norm_mix, v_w_in, v_b_gate, v_q_norm_a, v_k_norm_a, v_rel_bias_a, v_w_pool, v_pool_scale, v_w_branch_a, v_w_branch_b, v_w_branch_c, v_w_out, v_norm_ffn, v_w_up, v_conv_w, v_conv_b, v_w_down):
    args = dict(locals())
    w = {n: args[n] for n in WEIGHTS}
    m = {n: args["m_" + n] for n in WEIGHTS}
    v = {n: args["v_" + n] for n in WEIGHTS}
    L = w_in.shape[0]
    T = x.shape[1]
    xs = x.reshape(T, D_MODEL)
    target = loss_target.reshape(T, D_MODEL)

    conv_bits = lax.bitcast_convert_type(conv_w, BF16)
    packed = _pack_rows([w[n].astype(BF16) for n in SHARDED] + [conv_bits], PACK_COLS, 16)
    gathered = _exchange(packed, False, "gather_weights")
    rows = [w[n].size // PACK_COLS for n in SHARDED]
    offs = [sum(rows[:i]) for i in range(len(rows) + 1)]
    part = {n: gathered[:, offs[i]:offs[i + 1]] for i, n in enumerate(SHARDED)}
    w_in_f = _gather_columns(part["w_in"], w_in.shape)
    w_qkv, w_uc, w_g = w_in_f[:, :, :QKV_COLS], w_in_f[:, :, QKV_COLS:QKV_COLS + WIDTH], w_in_f[:, :, QKV_COLS + WIDTH:]
    w_a = _gather_columns(part["w_branch_a"], w_branch_a.shape)
    w_b = _gather_columns(part["w_branch_b"], w_branch_b.shape)
    w_c = _gather_columns(part["w_branch_c"], w_branch_c.shape)
    w_out_f = _gather_rows(part["w_out"], w_out.shape)
    w_up_f = _ff_pair_order(_gather_columns(part["w_up"], w_up.shape))
    w_down_f = _gather_rows(part["w_down"], w_down.shape)
    conv_flat = gathered[:, offs[-1]:].reshape(N_DEV, -1)[:, :conv_bits.size].reshape((N_DEV,) + conv_bits.shape)
    conv_w_f = _ff_pair_order(_gather_columns(lax.bitcast_convert_type(conv_flat, F32).reshape(N_DEV, -1), conv_w.shape))
    conv_b_f = _ff_pair_order(conv_b)
    onehot = _rel_onehot()

    saved = []
    cur = xs
    for l in range(L):
        h = _rmsnorm_fwd(cur, norm_mix[l], "norm_mix_fwd")
        qkv = _mm(h, w_qkv[l], "nn", BF16, "proj_qkv")
        uc = _mm(h, w_uc[l], "nn", F32, "proj_pool")
        glog = _mm(h, w_g[l], "nn", F32, "proj_gate")
        heads = _to_heads(qkv[:, :3 * WIDTH])
        bias = _select_mm(rel_bias_a[l], onehot, "nn", "rel_bias_table").reshape(N_HEADS, CHUNK, BAND)
        oa = _from_heads(_attn_a_fwd(heads[0], heads[1], heads[2], bias, q_norm_a[l], k_norm_a[l]))
        ob, carries = _attn_b_fwd(qkv)
        oc = _pool_fwd(uc, w_pool[l], pool_scale[l])
        merged, ya, yb, yc = _merge_fwd(oa, ob, oc, glog, b_gate[l], w_a[l], w_b[l], w_c[l])
        x1 = _mm(merged, w_out_f[l], "nn", F32, "out_proj", res=cur)
        h2 = _rmsnorm_fwd(x1, norm_ffn[l], "norm_ffn_fwd")
        u = _mm(h2, w_up_f[l], "nn", F32, "ff_up")
        act = _ff_act_fwd(u, conv_w_f[l], conv_b_f[l])
        x2 = _mm(act, w_down_f[l], "nn", F32, "ff_down", res=x1)
        saved.append(dict(x=cur, h=h, qkv=qkv, carries=carries, heads=heads, uc=uc, glog=glog, bias=bias, oa=oa, ob=ob, oc=oc, ya=ya, yb=yb,
                          yc=yc, merged=merged, x1=x1, h2=h2, u=u, act=act))
        cur = x2

    dcur, loss_local = _loss_head(cur, target)
    loss = lax.psum(loss_local, ("x", "y", "c"))

    gw = {n: [None] * L for n in WEIGHTS}
    for l in reversed(range(L)):
        s = saved[l]
        da = _mm(dcur, w_down_f[l], "nt", F32, "ff_down_dx", tn_cap=1408)
        gw["w_down"][l] = _mm(s["act"], dcur, "tn", BF16, "ff_down_dw")
        dc, dconv_w, dconv_b = _ff_act_bwd(s["u"], da, conv_w_f[l], conv_b_f[l])
        du = _ff_conv_bwd(dc, conv_w_f[l])
        dh2 = _mm(du, w_up_f[l], "nt", F32, "ff_up_dx")
        gw["w_up"][l] = _ff_natural_order(_mm(s["h2"], du, "tn", BF16, "ff_up_dw"))
        gw["conv_w"][l] = _ff_natural_order(dconv_w)
        gw["conv_b"][l] = _ff_natural_order(dconv_b)[0]
        dx1, dg = _rmsnorm_bwd(s["x1"], norm_ffn[l], dh2, dcur, "norm_ffn_bwd")
        gw["norm_ffn"][l] = dg[0]

        dmerged = _mm(dx1, w_out_f[l], "nt", F32, "out_proj_dx")
        gw["w_out"][l] = _mm(s["merged"], dx1, "tn", BF16, "out_proj_dw")
        dya, dyb, dyc, dglog, db_gate = _merge_bwd(dmerged, s["glog"], b_gate[l], s["ya"], s["yb"], s["yc"])
        gw["b_gate"][l] = db_gate[0]
        do = {}
        for tag, dy, wk, ok in (("a", dya, w_a, s["oa"]), ("b", dyb, w_b, s["ob"]), ("c", dyc, w_c, s["oc"])):
            do[tag] = _mm(dy, wk[l], "nt", BF16, "branch_dx_" + tag)
            gw["w_branch_" + tag][l] = _mm(ok, dy, "tn", BF16, "branch_dw_" + tag)
        duc, dw_pool, dscale = _pool_bwd(s["uc"], do["c"], w_pool[l], pool_scale[l])
        gw["w_pool"][l] = dw_pool
        gw["pool_scale"][l] = dscale[0]
        hd = s["heads"]
        dqa, dkc, dkp, dvc, dvp, dbias, dgq = _attn_a_bwd(hd[0], hd[1], hd[2], _to_heads(do["a"])[0], s["bias"],
                                                          q_norm_a[l], k_norm_a[l])
        dka, dva, dgk = _attn_a_bwd_keys(hd[1], dkc, dkp, dvc, dvp, k_norm_a[l])
        gw["q_norm_a"][l] = jnp.sum(dgq, axis=(0, 1))
        gw["k_norm_a"][l] = jnp.sum(dgk, axis=(0, 1))
        gw["rel_bias_a"][l] = _select_mm(dbias.reshape(N_HEADS, CHUNK * BAND), onehot, "nt", "rel_bias_table_dw")
        dqb, dkb, dvb = _attn_b_bwd(s["qkv"], s["carries"], do["b"])
        dqkv = jnp.concatenate([jnp.stack([dqa, dka, dva]).transpose(2, 0, 1, 3).reshape(T, 3 * WIDTH),
                                dqb, dkb.astype(BF16), dvb.astype(BF16)], axis=1)
        dh = _mm(dqkv, w_qkv[l], "nt", F32, "proj_qkv_dx")
        dh = _mm(duc, w_uc[l], "nt", F32, "proj_pool_dx", res=dh)
        dh = _mm(dglog, w_g[l], "nt", F32, "proj_gate_dx", res=dh)
        gw["w_in"][l] = jnp.concatenate([_mm(s["h"], dqkv, "tn", BF16, "proj_qkv_dw"),
                                         _mm(s["h"], duc, "tn", BF16, "proj_pool_dw"),
                                         _mm(s["h"], dglog, "tn", BF16, "proj_gate_dw")], axis=1)
        dcur, dg = _rmsnorm_bwd(s["x"], norm_mix[l], dh, dx1, "norm_mix_bwd")
        gw["norm_mix"][l] = dg[0]
    gw = {n: jnp.stack(g) for n, g in gw.items()}

    cw_pieces = gw["conv_w"].reshape(L, 3, N_DEV, -1).transpose(2, 0, 1, 3)
    cw_bits = lax.bitcast_convert_type(cw_pieces, BF16).reshape(N_DEV, -1)
    cw_rows = -(-cw_bits.shape[1] // (16 * PACK_COLS)) * 16
    cw_bits = jnp.pad(cw_bits, ((0, 0), (0, cw_rows * PACK_COLS - cw_bits.shape[1]))).reshape(N_DEV, cw_rows, PACK_COLS)
    pieces = jnp.concatenate([
        _split_columns(gw["w_in"]), _split_columns(gw["w_branch_a"]), _split_columns(gw["w_branch_b"]),
        _split_columns(gw["w_branch_c"]), _split_rows(gw["w_out"]), _split_columns(gw["w_up"]),
        _split_rows(gw["w_down"]), cw_bits], axis=1)
    parts = _exchange(pieces, True, "exchange_grads")
    small = _pack_rows([gw[n] for n in REPLICATED], SMALL_COLS, 16)
    small_parts = _exchange(small, False, "gather_small_grads")

    shard_like = [w[n] for n in SHARDED]
    res = _adamw(parts, *[_pack_rows([d[n] for n in SHARDED], PACK_COLS, 16) for d in (w, m, v)], "adamw_sharded")
    out = {n: r for n, r in zip(SHARDED, zip(*[_unpack_rows(r, shard_like) for r in res]))}
    rep_like = [w[n] for n in REPLICATED]
    res = _adamw(small_parts, *[_pack_rows([d[n] for n in REPLICATED], SMALL_COLS, 16) for d in (w, m, v)],
                 "adamw_replicated")
    out.update({n: r for n, r in zip(REPLICATED, zip(*[_unpack_rows(r, rep_like) for r in res]))})
    cw_parts = parts[:, parts.shape[1] - cw_rows:].reshape(N_DEV, -1)[:, :2 * conv_w.size].reshape(N_DEV, conv_w.size, 2)
    cw_parts = lax.bitcast_convert_type(cw_parts, F32)
    cw_parts = jnp.stack([_pack_rows([cw_parts[d]], SMALL_COLS, 16) for d in range(N_DEV)])
    res = _adamw(cw_parts, *[_pack_rows([d["conv_w"]], SMALL_COLS, 16) for d in (w, m, v)], "adamw_conv_w")
    out["conv_w"] = tuple(_unpack_rows(r, [conv_w])[0] for r in res)

    grads, deltas, new_m, new_v = ([out[n][i] for n in WEIGHTS] for i in range(4))
    return (loss, dcur.reshape(x.shape), *grads, *deltas, *new_m, *new_v)
```

```python
import functools
import math

import jax
import jax.numpy as jnp
from jax import lax
from jax.experimental import pallas as pl
from jax.experimental.pallas import tpu as pltpu

F32 = jnp.float32
BF16 = jnp.bfloat16

N_DEV = 8
D_MODEL = 1024
N_HEADS = 8
HEAD_DIM = 64
CHUNK = 64
N_LEFT = 8
BAND = (N_LEFT + 1) * CHUNK
WIDTH = N_HEADS * HEAD_DIM
POOL_WINDOWS = (2, 4, 8, 16)
POOL_DIM = 128
MAX_REL = 2 * CHUNK
REL_TABLE = MAX_REL + CHUNK
D_FF = 2816
EPS = 1e-6
SB_SCAN = 256
SB_ROWS = 512
SB_KEYS = 512
A_BLOCK = N_LEFT * CHUNK
HALO = 16
LANES = 128
VMEM_LIMIT = 56 * 1024 * 1024

ADAM_LR = 0.001
ADAM_B1 = 0.9
ADAM_B2 = 0.999
ADAM_EPS = 1e-08
ADAM_WD = 0.01
ADAM_STEP = 10

MESH = pl.DeviceIdType.MESH


def _params(sem):
    return pltpu.CompilerParams(dimension_semantics=sem, vmem_limit_bytes=VMEM_LIMIT)


def _pick(n, cap):
    if n <= cap:
        return n
    best = None
    for t in range(LANES, cap + 1, LANES):
        if n % t == 0:
            best = t
    assert best is not None, (n, cap)
    return best


def _mm(a, b, mode, out_dtype, name, tm=512, tn_cap=1024, tk_cap=1024, res=None):
    if mode == "nn":
        (M, K), (K2, N) = a.shape, b.shape
    elif mode == "nt":
        (M, K), (N, K2) = a.shape, b.shape
    else:
        (K, M), (K2, N) = a.shape, b.shape
    assert K == K2, (a.shape, b.shape, mode)
    tm = _pick(M, tm)
    tn = _pick(N, tn_cap)
    tk = _pick(K, tk_cap)
    nk = K // tk
    if mode == "nn":
        dims = (((1,), (0,)), ((), ()))
        a_spec = pl.BlockSpec((tm, tk), lambda i, j, k: (i, k))
        b_spec = pl.BlockSpec((tk, tn), lambda i, j, k: (k, j))
    elif mode == "nt":
        dims = (((1,), (1,)), ((), ()))
        a_spec = pl.BlockSpec((tm, tk), lambda i, j, k: (i, k))
        b_spec = pl.BlockSpec((tn, tk), lambda i, j, k: (j, k))
    else:
        dims = (((0,), (0,)), ((), ()))
        a_spec = pl.BlockSpec((tk, tm), lambda i, j, k: (k, i))
        b_spec = pl.BlockSpec((tk, tn), lambda i, j, k: (k, j))

    o_spec = pl.BlockSpec((tm, tn), lambda i, j, k: (i, j))

    def body(a_ref, b_ref, *rest):
        res_ref = rest[0] if res is not None else None
        o_ref, acc_ref = rest[-2:]
        k = pl.program_id(2)

        @pl.when(k == 0)
        def _():
            acc_ref[...] = jnp.zeros_like(acc_ref)

        acc_ref[...] += lax.dot_general(a_ref[...].astype(BF16), b_ref[...].astype(BF16), dims,
                                        preferred_element_type=F32)

        @pl.when(k == nk - 1)
        def _():
            total = acc_ref[...] if res is None else acc_ref[...] + res_ref[...]
            o_ref[...] = total.astype(out_dtype)

    return pl.pallas_call(
        body, name=name,
        grid=(M // tm, N // tn, nk),
        in_specs=[a_spec, b_spec] + ([o_spec] if res is not None else []),
        out_specs=o_spec,
        out_shape=jax.ShapeDtypeStruct((M, N), out_dtype),
        scratch_shapes=[pltpu.VMEM((tm, tn), F32)],
        compiler_params=_params(("parallel", "parallel", "arbitrary")),
    )(*((a, b) if res is None else (a, b, res)))


def _rmsnorm_fwd(x, gain, name, tm=512):
    T, C = x.shape

    def body(x_ref, g_ref, h_ref):
        xv = x_ref[...]
        r = lax.rsqrt(jnp.mean(xv * xv, axis=-1, keepdims=True) + EPS)
        h_ref[...] = (xv * r * g_ref[...]).astype(BF16)

    return pl.pallas_call(
        body, name=name, grid=(T // tm,),
        in_specs=[pl.BlockSpec((tm, C), lambda i: (i, 0)), pl.BlockSpec((1, C), lambda i: (0, 0))],
        out_specs=pl.BlockSpec((tm, C), lambda i: (i, 0)),
        out_shape=jax.ShapeDtypeStruct((T, C), BF16),
        compiler_params=_params(("parallel",)),
    )(x, gain.reshape(1, C))


def _rmsnorm_bwd(x, gain, dh, dres, name, tm=512):
    T, C = x.shape

    def body(x_ref, g_ref, dh_ref, dres_ref, dx_ref, dg_ref):
        @pl.when(pl.program_id(0) == 0)
        def _():
            dg_ref[...] = jnp.zeros_like(dg_ref)

        xv = x_ref[...]
        dy = dh_ref[...].astype(F32)
        r = lax.rsqrt(jnp.mean(xv * xv, axis=-1, keepdims=True) + EPS)
        gdy = dy * g_ref[...]
        inner = jnp.mean(xv * gdy, axis=-1, keepdims=True)
        dx_ref[...] = dres_ref[...] + r * gdy - xv * (r * r * r * inner)
        dg_ref[...] += jnp.sum(dy * xv * r, axis=0, keepdims=True)

    return pl.pallas_call(
        body, name=name, grid=(T // tm,),
        in_specs=[pl.BlockSpec((tm, C), lambda i: (i, 0)), pl.BlockSpec((1, C), lambda i: (0, 0)),
                  pl.BlockSpec((tm, C), lambda i: (i, 0)), pl.BlockSpec((tm, C), lambda i: (i, 0))],
        out_specs=[pl.BlockSpec((tm, C), lambda i: (i, 0)), pl.BlockSpec((1, C), lambda i: (0, 0))],
        out_shape=[jax.ShapeDtypeStruct((T, C), F32), jax.ShapeDtypeStruct((1, C), F32)],
        compiler_params=_params(("arbitrary",)),
    )(x, gain.reshape(1, C), dh, dres)


def _head_norm(t, g):
    tf = t.astype(F32)
    r = lax.rsqrt(jnp.mean(tf * tf, axis=-1, keepdims=True) + EPS)
    return tf * r * g


def _a_logits(qn, kcat, bias, cc, first):
    qc = qn[cc * CHUNK:(cc + 1) * CHUNK]
    kb = kcat[cc * CHUNK:cc * CHUNK + BAND]
    s = lax.dot_general(qc, kb, (((1,), (1,)), ((), ())), preferred_element_type=F32) * (1.0 / math.sqrt(HEAD_DIM))
    s = s + bias
    col = lax.broadcasted_iota(jnp.int32, (CHUNK, BAND), 1) + cc * CHUNK
    return jnp.where(col >= jnp.where(first, A_BLOCK, 0), s, -1e30)


def _softmax(s):
    m = jnp.max(s, axis=-1, keepdims=True)
    e = jnp.exp(s - m)
    return e / jnp.sum(e, axis=-1, keepdims=True)


def _a_specs(T):
    nb = T // A_BLOCK
    cur = pl.BlockSpec((1, A_BLOCK, HEAD_DIM), lambda h, i: (h, i, 0))
    prev = pl.BlockSpec((1, A_BLOCK, HEAD_DIM), lambda h, i: (h, jnp.maximum(i - 1, 0), 0))
    nxt = pl.BlockSpec((1, A_BLOCK, HEAD_DIM), lambda h, i: (h, jnp.minimum(i + 1, nb - 1), 0))
    bias = pl.BlockSpec((1, CHUNK, BAND), lambda h, i: (h, 0, 0))
    gain = pl.BlockSpec((1, HEAD_DIM), lambda h, i: (0, 0))
    hacc = pl.BlockSpec((1, 1, HEAD_DIM), lambda h, i: (h, 0, 0))
    return nb, cur, prev, nxt, bias, gain, hacc


def _attn_a_fwd(q, k, v, bias, gq, gk):
    Hh, T, _ = q.shape
    nb, cur, prev, _, bspec, gspec, _ = _a_specs(T)

    def body(q_ref, kc_ref, kp_ref, vc_ref, vp_ref, b_ref, gq_ref, gk_ref, o_ref):
        first = pl.program_id(1) == 0
        qn = _head_norm(q_ref[0], gq_ref[...]).astype(BF16)
        kcat = jnp.concatenate([_head_norm(kp_ref[0], gk_ref[...]).astype(BF16),
                                _head_norm(kc_ref[0], gk_ref[...]).astype(BF16)], axis=0)
        vcat = jnp.concatenate([vp_ref[0], vc_ref[0]], axis=0)
        bias_v = b_ref[0]
        for cc in range(N_LEFT):
            p = _softmax(_a_logits(qn, kcat, bias_v, cc, first))
            o = jnp.dot(p.astype(BF16), vcat[cc * CHUNK:cc * CHUNK + BAND], preferred_element_type=F32)
            o_ref[0, cc * CHUNK:(cc + 1) * CHUNK, :] = o.astype(BF16)

    return pl.pallas_call(
        body, name="attn_a_fwd", grid=(Hh, nb),
        in_specs=[cur, cur, prev, cur, prev, bspec, gspec, gspec],
        out_specs=cur,
        out_shape=jax.ShapeDtypeStruct((Hh, T, HEAD_DIM), BF16),
        compiler_params=_params(("parallel", "arbitrary")),
    )(q, k, k, v, v, bias, gq.reshape(1, HEAD_DIM), gk.reshape(1, HEAD_DIM))


def _head_norm_bwd(t, g, dn):
    tf = t.astype(F32)
    r = lax.rsqrt(jnp.mean(tf * tf, axis=-1, keepdims=True) + EPS)
    gd = dn * g
    inner = jnp.mean(tf * gd, axis=-1, keepdims=True)
    dt = r * gd - tf * (r * r * r * inner)
    return dt, jnp.sum(dn * tf * r, axis=0, keepdims=True)


def _attn_a_bwd(q, k, v, do, bias, gq, gk):
    Hh, T, _ = q.shape
    nb, cur, prev, _, bspec, gspec, hacc = _a_specs(T)
    scale = 1.0 / math.sqrt(HEAD_DIM)

    def body(q_ref, kc_ref, kp_ref, vc_ref, vp_ref, do_ref, b_ref, gq_ref, gk_ref,
             dq_ref, dkc_ref, dkp_ref, dvc_ref, dvp_ref, db_ref, dgq_ref, dkcat_ref, dvcat_ref):
        i = pl.program_id(1)
        first = i == 0

        @pl.when(first)
        def _():
            db_ref[...] = jnp.zeros_like(db_ref)
            dgq_ref[...] = jnp.zeros_like(dgq_ref)

        qn = _head_norm(q_ref[0], gq_ref[...]).astype(BF16)
        kcat = jnp.concatenate([_head_norm(kp_ref[0], gk_ref[...]).astype(BF16),
                                _head_norm(kc_ref[0], gk_ref[...]).astype(BF16)], axis=0)
        vcat = jnp.concatenate([vp_ref[0], vc_ref[0]], axis=0)
        bias_v = b_ref[0]
        dov = do_ref[0]
        dkcat_ref[...] = jnp.zeros_like(dkcat_ref)
        dvcat_ref[...] = jnp.zeros_like(dvcat_ref)
        dqn_parts = []
        dbias = jnp.zeros((CHUNK, BAND), F32)
        for cc in range(N_LEFT):
            p = _softmax(_a_logits(qn, kcat, bias_v, cc, first))
            doc = dov[cc * CHUNK:(cc + 1) * CHUNK]
            qc = qn[cc * CHUNK:(cc + 1) * CHUNK]
            dp = lax.dot_general(doc, vcat[cc * CHUNK:cc * CHUNK + BAND], (((1,), (1,)), ((), ())),
                                 preferred_element_type=F32)
            delta = jnp.sum(p * dp, axis=-1, keepdims=True)
            ds = p * (dp - delta)
            dbias = dbias + ds
            dsb = (ds * scale).astype(BF16)
            dqn_parts.append(jnp.dot(dsb, kcat[cc * CHUNK:cc * CHUNK + BAND], preferred_element_type=F32))
            dkcat_ref[cc * CHUNK:cc * CHUNK + BAND, :] += lax.dot_general(
                dsb, qc, (((0,), (0,)), ((), ())), preferred_element_type=F32)
            dvcat_ref[cc * CHUNK:cc * CHUNK + BAND, :] += lax.dot_general(
                p.astype(BF16), doc, (((0,), (0,)), ((), ())), preferred_element_type=F32)
        db_ref[0] += dbias
        dqn = jnp.concatenate(dqn_parts, axis=0)
        dq, dg = _head_norm_bwd(q_ref[0], gq_ref[...], dqn)
        dq_ref[0] = dq.astype(BF16)
        dgq_ref[0] += dg
        dkp_ref[0] = dkcat_ref[0:A_BLOCK, :]
        dkc_ref[0] = dkcat_ref[A_BLOCK:2 * A_BLOCK, :]
        dvp_ref[0] = dvcat_ref[0:A_BLOCK, :]
        dvc_ref[0] = dvcat_ref[A_BLOCK:2 * A_BLOCK, :]

    blk_f32 = jax.ShapeDtypeStruct((Hh, T, HEAD_DIM), F32)
    return pl.pallas_call(
        body, name="attn_a_bwd", grid=(Hh, nb),
        in_specs=[cur, cur, prev, cur, prev, cur, bspec, gspec, gspec],
        out_specs=[cur, cur, cur, cur, cur, bspec, hacc],
        out_shape=[jax.ShapeDtypeStruct((Hh, T, HEAD_DIM), BF16), blk_f32, blk_f32, blk_f32, blk_f32,
                   jax.ShapeDtypeStruct((Hh, CHUNK, BAND), F32), jax.ShapeDtypeStruct((Hh, 1, HEAD_DIM), F32)],
        scratch_shapes=[pltpu.VMEM((2 * A_BLOCK, HEAD_DIM), F32), pltpu.VMEM((2 * A_BLOCK, HEAD_DIM), F32)],
        compiler_params=_params(("parallel", "arbitrary")),
    )(q, k, k, v, v, do, bias, gq.reshape(1, HEAD_DIM), gk.reshape(1, HEAD_DIM))


def _attn_a_bwd_keys(k, dkc, dkp, dvc, dvp, gk):
    Hh, T, _ = k.shape
    nb, cur, _, nxt, _, gspec, hacc = _a_specs(T)

    def body(k_ref, dkc_ref, dkp_ref, dvc_ref, dvp_ref, gk_ref, dk_ref, dv_ref, dgk_ref):
        i = pl.program_id(1)

        @pl.when(i == 0)
        def _():
            dgk_ref[...] = jnp.zeros_like(dgk_ref)

        has_next = (i < nb - 1).astype(F32)
        dkn = dkc_ref[0] + has_next * dkp_ref[0]
        dk, dg = _head_norm_bwd(k_ref[0], gk_ref[...], dkn)
        dk_ref[0] = dk.astype(BF16)
        dv_ref[0] = (dvc_ref[0] + has_next * dvp_ref[0]).astype(BF16)
        dgk_ref[0] += dg

    blk = jax.ShapeDtypeStruct((Hh, T, HEAD_DIM), BF16)
    return pl.pallas_call(
        body, name="attn_a_bwd_keys", grid=(Hh, nb),
        in_specs=[cur, cur, nxt, cur, nxt, gspec],
        out_specs=[cur, cur, hacc],
        out_shape=[blk, blk, jax.ShapeDtypeStruct((Hh, 1, HEAD_DIM), F32)],
        compiler_params=_params(("parallel", "arbitrary")),
    )(k, dkc, dkp, dvc, dvp, gk.reshape(1, HEAD_DIM))


def _split(x):
    hi = x.astype(BF16)
    return hi, (x - hi.astype(F32)).astype(BF16)


def _scan_matrix(later):
    r = lax.broadcasted_iota(jnp.int32, (2 * SB_SCAN, SB_SCAN), 0)
    c = lax.broadcasted_iota(jnp.int32, (2 * SB_SCAN, SB_SCAN), 1)
    r = jnp.where(r >= SB_SCAN, r - SB_SCAN, r)
    return jnp.where((r > c) if later else (r < c), 1.0, 0.0).astype(BF16)


def _running_sums(x, carry, scan, later):
    n = SB_KEYS // SB_SCAN
    parts = [None] * n
    total = carry
    for sb in (reversed(range(n)) if later else range(n)):
        xs = x[:, sb * SB_SCAN:(sb + 1) * SB_SCAN]
        hi, lo = _split(xs)
        local = jnp.dot(jnp.concatenate([hi, lo], axis=1), scan, preferred_element_type=F32)
        parts[sb] = local if total is None else local + total
        rowsum = jnp.sum(xs, axis=-1, keepdims=True)
        total = rowsum if total is None else total + rowsum
    return (parts[0] if n == 1 else jnp.concatenate(parts, axis=1)), total


def _sb_log_sigmoids(z):
    neg_abs = pltpu.bitcast(pltpu.bitcast(z, jnp.uint32) | jnp.uint32(0x80000000), F32)
    take = jnp.minimum(z, 0.0) - jnp.log(1.0 + jnp.exp(neg_abs))
    return take, take - z


def _sb_mask():
    r = lax.broadcasted_iota(jnp.int32, (2 * SB_ROWS, SB_KEYS), 0)
    c = lax.broadcasted_iota(jnp.int32, (2 * SB_ROWS, SB_KEYS), 1)
    return c < jnp.where(r >= SB_ROWS, r - SB_ROWS, r)


def _stack_heads(t):
    lane = lax.broadcasted_iota(jnp.int32, t.shape, 1)
    zero = jnp.zeros_like(t)
    return jnp.concatenate([jnp.where(lane < HEAD_DIM, t, zero), jnp.where(lane >= HEAD_DIM, t, zero)], axis=0)


def _unstack_heads(t):
    lane = lax.broadcasted_iota(jnp.int32, (SB_ROWS, 2 * HEAD_DIM), 1)
    return jnp.where(lane < HEAD_DIM, t[:SB_ROWS], t[SB_ROWS:])


def _sb_specs(T):
    nq = T // SB_ROWS
    blk = lambda col: pl.BlockSpec((SB_ROWS, 2 * HEAD_DIM), lambda p, i: (i, col + p))
    full = lambda col: pl.BlockSpec((T, 2 * HEAD_DIM), lambda p, i: (0, col + p))
    return nq, blk, full


def _key_rows(j):
    return pl.ds(pl.multiple_of(j * SB_KEYS, SB_KEYS), SB_KEYS)


def _attn_b_fwd(qkv):
    T = qkv.shape[0]
    pairs = N_HEADS // 2
    nq, blk, full = _sb_specs(T)
    scale = 1.0 / math.sqrt(HEAD_DIM)

    assert nq <= LANES

    def body(q_ref, k_ref, v_ref, o_ref, c_ref, acc_ref, carry_ref):
        i = pl.program_id(1)
        scan = _scan_matrix(True)
        qst = _stack_heads((q_ref[...].astype(F32) * scale).astype(BF16))
        lane = lax.broadcasted_iota(jnp.int32, (2 * SB_ROWS, LANES), 1)

        def span(j, carry, mask):
            z = lax.dot_general(qst, k_ref[_key_rows(j), :], (((1,), (1,)), ((), ())), preferred_element_type=F32)
            take, keep = _sb_log_sigmoids(z)
            if mask is not None:
                keep = jnp.where(mask, keep, 0.0)
            tail, total = _running_sums(keep, carry, scan, True)
            w = jnp.exp(take + tail)
            if mask is not None:
                w = jnp.where(mask, w, 0.0)
            return jnp.dot(w.astype(BF16), v_ref[_key_rows(j), :], preferred_element_type=F32), total

        acc_ref[...], carry_ref[...] = span(i, None, _sb_mask())
        c_ref[0, 0] = jnp.zeros((2 * SB_ROWS, LANES), F32)

        @pl.loop(0, i)
        def _(jj):
            j = i - 1 - jj
            carry = carry_ref[...]
            c_ref[0, 0] = jnp.where(lane == j, carry, c_ref[0, 0])
            out, carry_ref[...] = span(j, carry, None)
            acc_ref[...] += out

        o_ref[...] = _unstack_heads(acc_ref[...]).astype(BF16)

    return pl.pallas_call(
        body, name="attn_b_fwd", grid=(pairs, nq),
        in_specs=[blk(3 * pairs), full(4 * pairs), full(5 * pairs)],
        out_specs=[pl.BlockSpec((SB_ROWS, 2 * HEAD_DIM), lambda p, i: (i, p)),
                   pl.BlockSpec((1, 1, 2 * SB_ROWS, LANES), lambda p, i: (p, i, 0, 0))],
        out_shape=[jax.ShapeDtypeStruct((T, WIDTH), BF16), jax.ShapeDtypeStruct((pairs, nq, 2 * SB_ROWS, LANES), F32)],
        scratch_shapes=[pltpu.VMEM((2 * SB_ROWS, 2 * HEAD_DIM), F32), pltpu.VMEM((2 * SB_ROWS, 1), F32)],
        compiler_params=_params(("parallel", "arbitrary")),
    )(qkv, qkv, qkv)


def _attn_b_bwd(qkv, carries, do):
    T = qkv.shape[0]
    pairs = N_HEADS // 2
    nq, blk, full = _sb_specs(T)
    scale = 1.0 / math.sqrt(HEAD_DIM)
    oblk = pl.BlockSpec((SB_ROWS, 2 * HEAD_DIM), lambda p, i: (i, p))
    ofull = pl.BlockSpec((T, 2 * HEAD_DIM), lambda p, i: (0, p))

    def body(q_ref, k_ref, v_ref, c_ref, do_ref, dq_ref, dk_ref, dv_ref, dqacc_ref, before_ref):
        i = pl.program_id(1)

        @pl.when(i == 0)
        def _():
            dk_ref[...] = jnp.zeros_like(dk_ref)
            dv_ref[...] = jnp.zeros_like(dv_ref)

        scan_later = _scan_matrix(True)
        scan_earlier = _scan_matrix(False)
        qst = _stack_heads((q_ref[...].astype(F32) * scale).astype(BF16))
        dost = _stack_heads(do_ref[...].astype(BF16))
        lane = lax.broadcasted_iota(jnp.int32, (2 * SB_ROWS, LANES), 1)

        def span(j, later, mask):
            kv = k_ref[_key_rows(j), :]
            vv = v_ref[_key_rows(j), :]
            z = lax.dot_general(qst, kv, (((1,), (1,)), ((), ())), preferred_element_type=F32)
            take, keep = _sb_log_sigmoids(z)
            sig = jnp.exp(take)
            if mask is not None:
                keep = jnp.where(mask, keep, 0.0)
            tail, _ = _running_sums(keep, later, scan_later, True)
            w = jnp.exp(take + tail)
            if mask is not None:
                w = jnp.where(mask, w, 0.0)
            g = w * lax.dot_general(dost, vv, (((1,), (1,)), ((), ())), preferred_element_type=F32)
            before, before_ref[...] = _running_sums(g, before_ref[...], scan_earlier, False)
            dz = g - sig * (g + before)
            if mask is not None:
                dz = jnp.where(mask, dz, 0.0)
            dzb = dz.astype(BF16)
            dqacc_ref[...] += jnp.dot(dzb, kv, preferred_element_type=F32)
            dk_ref[_key_rows(j), :] += lax.dot_general(dzb, qst, (((0,), (0,)), ((), ())), preferred_element_type=F32)
            dv_ref[_key_rows(j), :] += lax.dot_general(w.astype(BF16), dost, (((0,), (0,)), ((), ())),
                                                       preferred_element_type=F32)

        dqacc_ref[...] = jnp.zeros_like(dqacc_ref)
        before_ref[...] = jnp.zeros_like(before_ref)

        @pl.loop(0, i)
        def _(j):
            later = jnp.sum(jnp.where(lane == j, c_ref[0, 0], 0.0), axis=-1, keepdims=True)
            span(j, later, None)

        span(i, None, _sb_mask())
        dq_ref[...] = (_unstack_heads(dqacc_ref[...]) * scale).astype(BF16)

    wide = jax.ShapeDtypeStruct((T, WIDTH), F32)
    return pl.pallas_call(
        body, name="attn_b_bwd", grid=(pairs, nq),
        in_specs=[blk(3 * pairs), full(4 * pairs), full(5 * pairs),
                  pl.BlockSpec((1, 1, 2 * SB_ROWS, LANES), lambda p, i: (p, i, 0, 0)), oblk],
        out_specs=[oblk, ofull, ofull],
        out_shape=[jax.ShapeDtypeStruct((T, WIDTH), BF16), wide, wide],
        scratch_shapes=[pltpu.VMEM((2 * SB_ROWS, 2 * HEAD_DIM), F32), pltpu.VMEM((2 * SB_ROWS, 1), F32)],
        compiler_params=_params(("parallel", "arbitrary")),
    )(qkv, qkv, qkv, carries, do)


def _window_sums(ext, forward):
    n = ext.shape[0]
    out = []
    s = ext
    for step in (1, 2, 4, 8):
        s = s + pltpu.roll(s, (n - step) if forward else step, 0)
        out.append(s)
    return out


def _pool_counts(base, rows, win):
    t = base + lax.broadcasted_iota(jnp.int32, (rows, 1), 0)
    return jnp.minimum(t + 1, win).astype(F32)


def _pooled(u_ref, up_ref, i, tm):
    prev = jnp.where(i > 0, up_ref[...], 0.0)
    ext = jnp.concatenate([prev, u_ref[...]], axis=0)
    sums = _window_sums(ext, False)
    parts = []
    for g, win in enumerate(POOL_WINDOWS):
        cols = slice(g * POOL_DIM, (g + 1) * POOL_DIM)
        cnt = _pool_counts(i * tm, tm, win)
        parts.append(sums[g][HALO:, cols] / cnt - ext[HALO:, cols])
    return parts


def _pool_fwd(ucg, w_pool, scale, tm=512):
    T = ucg.shape[0]
    C = WIDTH

    def body(u_ref, up_ref, w_ref, s_ref, o_ref):
        i = pl.program_id(0)
        parts = _pooled(u_ref, up_ref, i, tm)
        for g in range(len(POOL_WINDOWS)):
            mixed = jnp.dot(parts[g].astype(BF16), w_ref[g], preferred_element_type=F32)
            o_ref[:, g * POOL_DIM:(g + 1) * POOL_DIM] = (mixed * s_ref[:, g * POOL_DIM:(g + 1) * POOL_DIM]).astype(BF16)

    return pl.pallas_call(
        body, name="pool_fwd", grid=(T // tm,),
        in_specs=[pl.BlockSpec((tm, C), lambda i: (i, 0)),
                  pl.BlockSpec((HALO, C), lambda i: (jnp.maximum(i * (tm // HALO) - 1, 0), 0)),
                  pl.BlockSpec((len(POOL_WINDOWS), POOL_DIM, POOL_DIM), lambda i: (0, 0, 0)),
                  pl.BlockSpec((1, C), lambda i: (0, 0))],
        out_specs=pl.BlockSpec((tm, C), lambda i: (i, 0)),
        out_shape=jax.ShapeDtypeStruct((T, C), BF16),
        compiler_params=_params(("parallel",)),
    )(ucg, ucg, w_pool.astype(BF16), scale.reshape(1, C))


def _pool_bwd(ucg, do_c, w_pool, scale, tm=512):
    T = ucg.shape[0]
    C = WIDTH
    nt = T // tm
    G = len(POOL_WINDOWS)

    def body(u_ref, up_ref, do_ref, don_ref, w_ref, s_ref, du_ref, dw_ref, ds_ref):
        i = pl.program_id(0)

        @pl.when(i == 0)
        def _():
            dw_ref[...] = jnp.zeros_like(dw_ref)
            ds_ref[...] = jnp.zeros_like(ds_ref)

        parts = _pooled(u_ref, up_ref, i, tm)
        nxt = jnp.where(i < nt - 1, don_ref[...].astype(F32), 0.0)
        do_ext = jnp.concatenate([do_ref[...].astype(F32), nxt], axis=0) * s_ref[...]
        for g, win in enumerate(POOL_WINDOWS):
            cols = slice(g * POOL_DIM, (g + 1) * POOL_DIM)
            pooled_b = parts[g].astype(BF16)
            dmix = do_ext[:, cols].astype(BF16)
            mixed = jnp.dot(pooled_b, w_ref[g], preferred_element_type=F32)
            ds_ref[:, cols] += jnp.sum(do_ref[:, cols].astype(F32) * mixed, axis=0, keepdims=True)
            dw_ref[g] += lax.dot_general(pooled_b, dmix[:tm], (((0,), (0,)), ((), ())), preferred_element_type=F32)
            dpool = lax.dot_general(dmix, w_ref[g], (((1,), (1,)), ((), ())), preferred_element_type=F32)
            scaled = dpool / _pool_counts(i * tm, tm + HALO, win)
            fwd = _window_sums(scaled, True)[g]
            du_ref[:, cols] = (fwd[:tm] - dpool[:tm]).astype(BF16)

    return pl.pallas_call(
        body, name="pool_bwd", grid=(nt,),
        in_specs=[pl.BlockSpec((tm, C), lambda i: (i, 0)),
                  pl.BlockSpec((HALO, C), lambda i: (jnp.maximum(i * (tm // HALO) - 1, 0), 0)),
                  pl.BlockSpec((tm, C), lambda i: (i, 0)),
                  pl.BlockSpec((HALO, C), lambda i: (jnp.minimum((i + 1) * (tm // HALO), T // HALO - 1), 0)),
                  pl.BlockSpec((G, POOL_DIM, POOL_DIM), lambda i: (0, 0, 0)),
                  pl.BlockSpec((1, C), lambda i: (0, 0))],
        out_specs=[pl.BlockSpec((tm, C), lambda i: (i, 0)),
                   pl.BlockSpec((G, POOL_DIM, POOL_DIM), lambda i: (0, 0, 0)),
                   pl.BlockSpec((1, C), lambda i: (0, 0))],
        out_shape=[jax.ShapeDtypeStruct((T, C), BF16), jax.ShapeDtypeStruct((G, POOL_DIM, POOL_DIM), F32),
                   jax.ShapeDtypeStruct((1, C), F32)],
        compiler_params=_params(("arbitrary",)),
    )(ucg, ucg, do_c, do_c, w_pool.astype(BF16), scale.reshape(1, C))


def _merge_fwd(oa, ob, oc, glog, b_gate, wa, wb, wc, tm=256):
    T = oa.shape[0]
    Dm = D_MODEL
    row = lambda c: pl.BlockSpec((tm, c), lambda i: (i, 0))
    wspec = pl.BlockSpec((WIDTH, Dm), lambda i: (0, 0))

    def body(oa_ref, ob_ref, oc_ref, g_ref, b_ref, wa_ref, wb_ref, wc_ref, m_ref, ya_ref, yb_ref, yc_ref):
        merged = jnp.zeros((tm, Dm), F32)
        for kk, (o_ref, w_ref, y_ref) in enumerate(((oa_ref, wa_ref, ya_ref), (ob_ref, wb_ref, yb_ref),
                                                    (oc_ref, wc_ref, yc_ref))):
            y = jnp.dot(o_ref[...].astype(BF16), w_ref[...], preferred_element_type=F32)
            gate = jax.nn.sigmoid(g_ref[:, kk * Dm:(kk + 1) * Dm] + b_ref[:, kk * Dm:(kk + 1) * Dm])
            merged = merged + gate * y
            y_ref[...] = y.astype(BF16)
        m_ref[...] = merged.astype(BF16)

    out = jax.ShapeDtypeStruct((T, Dm), BF16)
    return pl.pallas_call(
        body, name="merge_fwd", grid=(T // tm,),
        in_specs=[row(WIDTH), row(WIDTH), row(WIDTH), row(3 * Dm), pl.BlockSpec((1, 3 * Dm), lambda i: (0, 0)),
                  wspec, wspec, wspec],
        out_specs=[row(Dm)] * 4,
        out_shape=[out] * 4,
        compiler_params=_params(("parallel",)),
    )(oa, ob, oc, glog, b_gate.reshape(1, 3 * Dm), wa, wb, wc)


def _merge_bwd(dmerged, glog, b_gate, ya, yb, yc, tm=256):
    T = dmerged.shape[0]
    Dm = D_MODEL
    row = lambda c: pl.BlockSpec((tm, c), lambda i: (i, 0))

    def body(dm_ref, g_ref, b_ref, ya_ref, yb_ref, yc_ref, dya_ref, dyb_ref, dyc_ref, dg_ref, db_ref):
        @pl.when(pl.program_id(0) == 0)
        def _():
            db_ref[...] = jnp.zeros_like(db_ref)

        dm = dm_ref[...]
        for kk, (y_ref, dy_ref) in enumerate(((ya_ref, dya_ref), (yb_ref, dyb_ref), (yc_ref, dyc_ref))):
            cols = slice(kk * Dm, (kk + 1) * Dm)
            gate = jax.nn.sigmoid(g_ref[:, cols] + b_ref[:, cols])
            dy_ref[...] = (dm * gate).astype(BF16)
            dlog = dm * y_ref[...].astype(F32) * gate * (1.0 - gate)
            dg_ref[:, cols] = dlog.astype(BF16)
            db_ref[:, cols] += jnp.sum(dlog, axis=0, keepdims=True)

    out = jax.ShapeDtypeStruct((T, Dm), BF16)
    return pl.pallas_call(
        body, name="merge_bwd", grid=(T // tm,),
        in_specs=[row(Dm), row(3 * Dm), pl.BlockSpec((1, 3 * Dm), lambda i: (0, 0)), row(Dm), row(Dm), row(Dm)],
        out_specs=[row(Dm), row(Dm), row(Dm), row(3 * Dm), pl.BlockSpec((1, 3 * Dm), lambda i: (0, 0))],
        out_shape=[out, out, out, jax.ShapeDtypeStruct((T, 3 * Dm), BF16), jax.ShapeDtypeStruct((1, 3 * Dm), F32)],
        compiler_params=_params(("arbitrary",)),
    )(dmerged, glog, b_gate.reshape(1, 3 * Dm), ya, yb, yc)


def _residual_add(x, y, name, tm=512):
    T, C = x.shape

    def body(x_ref, y_ref, o_ref):
        o_ref[...] = x_ref[...] + y_ref[...]

    spec = pl.BlockSpec((tm, C), lambda i: (i, 0))
    return pl.pallas_call(body, name=name, grid=(T // tm,), in_specs=[spec, spec], out_specs=spec,
                          out_shape=jax.ShapeDtypeStruct((T, C), F32), compiler_params=_params(("parallel",)))(x, y)


FF_TILE = 256
FF_TILES = D_FF // FF_TILE
CONV_HALO = 8


def _ff_pair_order(w):
    lead = w.shape[:-1]
    n = len(lead)
    w = w.reshape(*lead, 2, FF_TILES, FF_TILE)
    return jnp.swapaxes(w, n, n + 1).reshape(*lead, 2 * D_FF)


def _ff_natural_order(w):
    lead = w.shape[:-1]
    n = len(lead)
    w = w.reshape(*lead, FF_TILES, 2, FF_TILE)
    return jnp.swapaxes(w, n, n + 1).reshape(*lead, 2 * D_FF)


def _conv(ext, w_ref, b_ref):
    c = b_ref[...] + w_ref[2:3, :] * ext
    c = c + w_ref[1:2, :] * pltpu.roll(ext, 1, 0)
    c = c + w_ref[0:1, :] * pltpu.roll(ext, 2, 0)
    return c[CONV_HALO:]


def _ff_specs(T, tm):
    pair = pl.BlockSpec((tm, 2 * FF_TILE), lambda i, j: (i, j))
    prev = pl.BlockSpec((CONV_HALO, 2 * FF_TILE), lambda i, j: (jnp.maximum(i * (tm // CONV_HALO) - 1, 0), j))
    nxt = pl.BlockSpec((CONV_HALO, 2 * FF_TILE),
                       lambda i, j: (jnp.minimum((i + 1) * (tm // CONV_HALO), T // CONV_HALO - 1), j))
    half = pl.BlockSpec((tm, FF_TILE), lambda i, j: (i, j))
    small = lambda r: pl.BlockSpec((r, 2 * FF_TILE), lambda i, j: (0, j))
    return pair, prev, nxt, half, small


def _swap_grid(spec):
    return pl.BlockSpec(spec.block_shape, lambda j, i, f=spec.index_map: f(i, j))


def _ff_act_fwd(u, conv_w, conv_b, tm=512):
    T = u.shape[0]
    pair, prev, _, half, small = _ff_specs(T, tm)

    def body(u_ref, p_ref, w_ref, b_ref, a_ref):
        i = pl.program_id(0)
        c = _conv(jnp.concatenate([jnp.where(i > 0, p_ref[...], 0.0), u_ref[...]], axis=0), w_ref, b_ref)
        cg, cv = c[:, :FF_TILE], c[:, FF_TILE:]
        a_ref[...] = (cg * jax.nn.sigmoid(cg) * cv).astype(BF16)

    return pl.pallas_call(
        body, name="ff_act_fwd", grid=(T // tm, FF_TILES),
        in_specs=[pair, prev, small(3), small(1)],
        out_specs=half,
        out_shape=jax.ShapeDtypeStruct((T, D_FF), BF16),
        compiler_params=_params(("parallel", "parallel")),
    )(u, u, conv_w, conv_b.reshape(1, -1))


def _ff_act_bwd(u, da, conv_w, conv_b, tm=512):
    T = u.shape[0]
    pair, prev, _, half, small = _ff_specs(T, tm)

    def body(u_ref, p_ref, da_ref, w_ref, b_ref, dc_ref, dw_ref, db_ref):
        i = pl.program_id(1)

        @pl.when(i == 0)
        def _():
            dw_ref[...] = jnp.zeros_like(dw_ref)
            db_ref[...] = jnp.zeros_like(db_ref)

        ext = jnp.concatenate([jnp.where(i > 0, p_ref[...], 0.0), u_ref[...]], axis=0)
        c = _conv(ext, w_ref, b_ref)
        cg, cv = c[:, :FF_TILE], c[:, FF_TILE:]
        da = da_ref[...]
        sg = jax.nn.sigmoid(cg)
        dc = jnp.concatenate([da * cv * sg * (1.0 + cg * (1.0 - sg)), da * cg * sg], axis=1)
        dc_ref[...] = dc
        db_ref[...] += jnp.sum(dc, axis=0, keepdims=True)
        dw_ref[2:3, :] += jnp.sum(dc * ext[CONV_HALO:], axis=0, keepdims=True)
        dw_ref[1:2, :] += jnp.sum(dc * pltpu.roll(ext, 1, 0)[CONV_HALO:], axis=0, keepdims=True)
        dw_ref[0:1, :] += jnp.sum(dc * pltpu.roll(ext, 2, 0)[CONV_HALO:], axis=0, keepdims=True)

    return pl.pallas_call(
        body, name="ff_act_bwd", grid=(FF_TILES, T // tm),
        in_specs=[_swap_grid(pair), _swap_grid(prev), _swap_grid(half), _swap_grid(small(3)), _swap_grid(small(1))],
        out_specs=[_swap_grid(pair), _swap_grid(small(3)), _swap_grid(small(1))],
        out_shape=[jax.ShapeDtypeStruct((T, 2 * D_FF), F32), jax.ShapeDtypeStruct((3, 2 * D_FF), F32),
                   jax.ShapeDtypeStruct((1, 2 * D_FF), F32)],
        compiler_params=_params(("parallel", "arbitrary")),
    )(u, u, da, conv_w, conv_b.reshape(1, -1))


def _ff_conv_bwd(dc, conv_w, tm=512):
    T = dc.shape[0]
    nt = T // tm
    pair, _, nxt, _, small = _ff_specs(T, tm)

    def body(dc_ref, n_ref, w_ref, du_ref):
        i = pl.program_id(0)
        ext = jnp.concatenate([dc_ref[...], jnp.where(i < nt - 1, n_ref[...], 0.0)], axis=0)
        n = tm + CONV_HALO
        du = w_ref[2:3, :] * ext + w_ref[1:2, :] * pltpu.roll(ext, n - 1, 0) + w_ref[0:1, :] * pltpu.roll(ext, n - 2, 0)
        du_ref[...] = du[:tm].astype(BF16)

    return pl.pallas_call(
        body, name="ff_conv_bwd", grid=(nt, FF_TILES),
        in_specs=[pair, nxt, small(3)],
        out_specs=pair,
        out_shape=jax.ShapeDtypeStruct((T, 2 * D_FF), BF16),
        compiler_params=_params(("parallel", "parallel")),
    )(dc, dc, conv_w)


def _loss_head(y, target, tm=512):
    T, C = y.shape
    nt = T // tm

    def body(y_ref, t_ref, dy_ref, l_ref):
        err = y_ref[...] - t_ref[...]
        dy_ref[...] = err * (1.0 / C)
        part = jnp.sum(err * err, axis=0, keepdims=True) * (0.5 / C)
        l_ref[0] = jnp.broadcast_to(part, (8, C))

    spec = pl.BlockSpec((tm, C), lambda i: (i, 0))
    dy, parts = pl.pallas_call(
        body, name="loss_head", grid=(nt,),
        in_specs=[spec, spec],
        out_specs=[spec, pl.BlockSpec((1, 8, C), lambda i: (i, 0, 0))],
        out_shape=[jax.ShapeDtypeStruct((T, C), F32), jax.ShapeDtypeStruct((nt, 8, C), F32)],
        compiler_params=_params(("parallel",)),
    )(y, target)
    return dy, jnp.sum(parts[:, 0, :])


def _adamw_math(w, g, m, v):
    m = ADAM_B1 * m + (1.0 - ADAM_B1) * g
    v = ADAM_B2 * v + (1.0 - ADAM_B2) * (g * g)
    m_hat = m / (1.0 - ADAM_B1 ** ADAM_STEP)
    v_hat = v / (1.0 - ADAM_B2 ** ADAM_STEP)
    delta = -ADAM_LR * (m_hat / (jnp.sqrt(v_hat) + ADAM_EPS) + ADAM_WD * w)
    return delta, m, v


def _adamw(parts, w, m, v, name, tm=256):
    R, C = w.shape
    tm = _pick_rows(R, tm)

    def body(p_ref, w_ref, m_ref, v_ref, g_ref, d_ref, nm_ref, nv_ref):
        g = p_ref[0].astype(F32)
        for s in range(1, N_DEV):
            g = g + p_ref[s].astype(F32)
        delta, nm, nv = _adamw_math(w_ref[...], g, m_ref[...], v_ref[...])
        g_ref[...] = g
        d_ref[...] = delta
        nm_ref[...] = nm
        nv_ref[...] = nv

    spec = pl.BlockSpec((tm, C), lambda i: (i, 0))
    out = jax.ShapeDtypeStruct((R, C), F32)
    return pl.pallas_call(
        body, name=name, grid=(R // tm,),
        in_specs=[pl.BlockSpec((N_DEV, tm, C), lambda i: (0, i, 0)), spec, spec, spec],
        out_specs=[spec] * 4,
        out_shape=[out] * 4,
        compiler_params=_params(("parallel",)),
    )(parts, w, m, v)


def _pick_rows(n, cap):
    best = None
    for t in range(16, min(n, cap) + 1, 16):
        if n % t == 0:
            best = t
    assert best is not None, (n, cap)
    return best


def _exchange(src, scatter, name):
    slab = src.shape[1:] if scatter else src.shape

    def body(src_ref, out_ref, send_sems, recv_sems, local_sem):
        x, y, c = lax.axis_index("x"), lax.axis_index("y"), lax.axis_index("c")
        me = 4 * x + 2 * y + c

        def piece(d):
            return src_ref.at[d] if scatter else src_ref

        mine = pltpu.make_async_copy(piece(me), out_ref.at[me], local_sem)
        mine.start()
        copies = []
        for k in range(1, N_DEV):
            px = 1 - x if k & 4 else x
            py = 1 - y if k & 2 else y
            pc = 1 - c if k & 1 else c
            peer = 4 * px + 2 * py + pc
            cp = pltpu.make_async_remote_copy(
                src_ref=piece(peer), dst_ref=out_ref.at[me],
                send_sem=send_sems.at[k], recv_sem=recv_sems.at[k],
                device_id=(px, py, pc), device_id_type=MESH)
            cp.start()
            copies.append((cp, peer))
        for k, (cp, peer) in enumerate(copies, start=1):
            cp.wait_send()
            pltpu.make_async_remote_copy(
                src_ref=piece(peer), dst_ref=out_ref.at[peer],
                send_sem=send_sems.at[k], recv_sem=recv_sems.at[k],
                device_id=(x, y, c), device_id_type=MESH).wait_recv()
        mine.wait()

    return pl.pallas_call(
        body, name=name,
        in_specs=[pl.BlockSpec(memory_space=pl.ANY)],
        out_specs=pl.BlockSpec(memory_space=pl.ANY),
        out_shape=jax.ShapeDtypeStruct((N_DEV,) + tuple(slab), src.dtype),
        scratch_shapes=[pltpu.SemaphoreType.DMA((N_DEV,)), pltpu.SemaphoreType.DMA((N_DEV,)),
                        pltpu.SemaphoreType.DMA],
    )(src)


SHARDED = ("w_in", "w_branch_a", "w_branch_b", "w_branch_c", "w_out", "w_up", "w_down")
REPLICATED = ("norm_mix", "b_gate", "q_norm_a", "k_norm_a", "rel_bias_a", "w_pool", "pool_scale", "norm_ffn", "conv_b")
WEIGHTS = ("norm_mix", "w_in", "b_gate", "q_norm_a", "k_norm_a", "rel_bias_a", "w_pool", "pool_scale",
           "w_branch_a", "w_branch_b", "w_branch_c", "w_out", "norm_ffn", "w_up", "conv_w", "conv_b", "w_down")
PACK_COLS = 1024
SMALL_COLS = 128
QKV_COLS = 6 * WIDTH


def _rel_index():
    q_off = jnp.arange(CHUNK)[:, None] + N_LEFT * CHUNK
    k_off = jnp.arange(BAND)[None, :]
    return jnp.clip(q_off - k_off, -(CHUNK - 1), MAX_REL) + (CHUNK - 1)


def _rel_onehot():
    rel = _rel_index().reshape(1, CHUNK * BAND)
    return (rel == jnp.arange(REL_TABLE)[:, None]).astype(BF16)


def _select_mm(x, onehot, mode, name):
    hi = x.astype(BF16)
    r1 = x - hi.astype(F32)
    mid = r1.astype(BF16)
    lo = (r1 - mid.astype(F32)).astype(BF16)
    y = _mm(jnp.concatenate([hi, mid, lo, jnp.zeros_like(hi)], axis=0), onehot, mode, F32, name)
    n = x.shape[0]
    return y[:n] + y[n:2 * n] + y[2 * n:3 * n]


def _pack_rows(arrays, cols, row_multiple):
    flat = jnp.concatenate([a.reshape(-1) for a in arrays])
    rows = -(-flat.shape[0] // cols)
    rows = -(-rows // row_multiple) * row_multiple
    return jnp.pad(flat, (0, rows * cols - flat.shape[0])).reshape(rows, cols)


def _unpack_rows(packed, like):
    flat = packed.reshape(-1)
    out, off = [], 0
    for a in like:
        out.append(flat[off:off + a.size].reshape(a.shape))
        off += a.size
    return out


def _to_heads(t):
    T = t.shape[0]
    return t.reshape(T, -1, N_HEADS, HEAD_DIM).transpose(1, 2, 0, 3)


def _from_heads(t):
    return t.transpose(1, 0, 2).reshape(t.shape[1], WIDTH)


def _gather_columns(g, shape):
    L, rows, cols = shape
    return g.reshape(N_DEV, L, rows, cols).transpose(1, 2, 0, 3).reshape(L, rows, N_DEV * cols)


def _gather_rows(g, shape):
    L, rows, cols = shape
    return g.reshape(N_DEV, L, rows, cols).transpose(1, 0, 2, 3).reshape(L, N_DEV * rows, cols)


def _split_columns(full):
    L, rows, allc = full.shape
    cols = allc // N_DEV
    return full.reshape(L, rows, N_DEV, cols).transpose(2, 0, 1, 3).reshape(N_DEV, -1, PACK_COLS)


def _split_rows(full):
    L, allr, cols = full.shape
    rows = allr // N_DEV
    return full.reshape(L, N_DEV, rows, cols).transpose(1, 0, 2, 3).reshape(N_DEV, -1, PACK_COLS)


def kernel(x, norm_mix, w_in, b_gate, q_norm_a, k_norm_a, rel_bias_a, w_pool, pool_scale, w_branch_a, w_branch_b, w_branch_c, w_out, norm_ffn, w_up, conv_w, conv_b, w_down, loss_target, m_norm_mix, m_w_in, m_b_gate, m_q_norm_a, m_k_norm_a, m_rel_bias_a, m_w_pool, m_pool_scale, m_w_branch_a, m_w_branch_b, m_w_branch_c, m_w_out, m_norm_ffn, m_w_up, m_conv_w, m_conv_b, m_w_down, v_norm_mix, v_w_in, v_b_gate, v_q_norm_a, v_k_norm_a, v_rel_bias_a, v_w_pool, v_pool_scale, v_w_branch_a, v_w_branch_b, v_w_branch_c, v_w_out, v_norm_ffn, v_w_up, v_conv_w, v_conv_b, v_w_down):
    args = dict(locals())
    w = {n: args[n] for n in WEIGHTS}
    m = {n: args["m_" + n] for n in WEIGHTS}
    v = {n: args["v_" + n] for n in WEIGHTS}
    L = w_in.shape[0]
    T = x.shape[1]
    xs = x.reshape(T, D_MODEL)
    target = loss_target.reshape(T, D_MODEL)

    conv_bits = lax.bitcast_convert_type(conv_w, BF16)
    packed = _pack_rows([w[n].astype(BF16) for n in SHARDED] + [conv_bits], PACK_COLS, 16)
    gathered = _exchange(packed, False, "gather_weights")
    rows = [w[n].size // PACK_COLS for n in SHARDED]
    offs = [sum(rows[:i]) for i in range(len(rows) + 1)]
    part = {n: gathered[:, offs[i]:offs[i + 1]] for i, n in enumerate(SHARDED)}
    w_in_f = _gather_columns(part["w_in"], w_in.shape)
    w_qkv, w_uc, w_g = w_in_f[:, :, :QKV_COLS], w_in_f[:, :, QKV_COLS:QKV_COLS + WIDTH], w_in_f[:, :, QKV_COLS + WIDTH:]
    w_a = _gather_columns(part["w_branch_a"], w_branch_a.shape)
    w_b = _gather_columns(part["w_branch_b"], w_branch_b.shape)
    w_c = _gather_columns(part["w_branch_c"], w_branch_c.shape)
    w_out_f = _gather_rows(part["w_out"], w_out.shape)
    w_up_f = _ff_pair_order(_gather_columns(part["w_up"], w_up.shape))
    w_down_f = _gather_rows(part["w_down"], w_down.shape)
    conv_flat = gathered[:, offs[-1]:].reshape(N_DEV, -1)[:, :conv_bits.size].reshape((N_DEV,) + conv_bits.shape)
    conv_w_f = _ff_pair_order(_gather_columns(lax.bitcast_convert_type(conv_flat, F32).reshape(N_DEV, -1), conv_w.shape))
    conv_b_f = _ff_pair_order(conv_b)
    onehot = _rel_onehot()

    saved = []
    cur = xs
    for l in range(L):
        h = _rmsnorm_fwd(cur, norm_mix[l], "norm_mix_fwd")
        qkv = _mm(h, w_qkv[l], "nn", BF16, "proj_qkv")
        uc = _mm(h, w_uc[l], "nn", F32, "proj_pool")
        glog = _mm(h, w_g[l], "nn", F32, "proj_gate")
        heads = _to_heads(qkv[:, :3 * WIDTH])
        bias = _select_mm(rel_bias_a[l], onehot, "nn", "rel_bias_table").reshape(N_HEADS, CHUNK, BAND)
        oa = _from_heads(_attn_a_fwd(heads[0], heads[1], heads[2], bias, q_norm_a[l], k_norm_a[l]))
        ob, carries = _attn_b_fwd(qkv)
        oc = _pool_fwd(uc, w_pool[l], pool_scale[l])
        merged, ya, yb, yc = _merge_fwd(oa, ob, oc, glog, b_gate[l], w_a[l], w_b[l], w_c[l])
        x1 = _mm(merged, w_out_f[l], "nn", F32, "out_proj", res=cur)
        h2 = _rmsnorm_fwd(x1, norm_ffn[l], "norm_ffn_fwd")
        u = _mm(h2, w_up_f[l], "nn", F32, "ff_up")
        act = _ff_act_fwd(u, conv_w_f[l], conv_b_f[l])
        x2 = _mm(act, w_down_f[l], "nn", F32, "ff_down", res=x1)
        saved.append(dict(x=cur, h=h, qkv=qkv, carries=carries, heads=heads, uc=uc, glog=glog, bias=bias, oa=oa, ob=ob, oc=oc, ya=ya, yb=yb,
                          yc=yc, merged=merged, x1=x1, h2=h2, u=u, act=act))
        cur = x2

    dcur, loss_local = _loss_head(cur, target)
    loss = lax.psum(loss_local, ("x", "y", "c"))

    gw = {n: [None] * L for n in WEIGHTS}
    for l in reversed(range(L)):
        s = saved[l]
        da = _mm(dcur, w_down_f[l], "nt", F32, "ff_down_dx", tn_cap=1408)
        gw["w_down"][l] = _mm(s["act"], dcur, "tn", BF16, "ff_down_dw")
        dc, dconv_w, dconv_b = _ff_act_bwd(s["u"], da, conv_w_f[l], conv_b_f[l])
        du = _ff_conv_bwd(dc, conv_w_f[l])
        dh2 = _mm(du, w_up_f[l], "nt", F32, "ff_up_dx")
        gw["w_up"][l] = _ff_natural_order(_mm(s["h2"], du, "tn", BF16, "ff_up_dw"))
        gw["conv_w"][l] = _ff_natural_order(dconv_w)
        gw["conv_b"][l] = _ff_natural_order(dconv_b)[0]
        dx1, dg = _rmsnorm_bwd(s["x1"], norm_ffn[l], dh2, dcur, "norm_ffn_bwd")
        gw["norm_ffn"][l] = dg[0]

        dmerged = _mm(dx1, w_out_f[l], "nt", F32, "out_proj_dx")
        gw["w_out"][l] = _mm(s["merged"], dx1, "tn", BF16, "out_proj_dw")
        dya, dyb, dyc, dglog, db_gate = _merge_bwd(dmerged, s["glog"], b_gate[l], s["ya"], s["yb"], s["yc"])
        gw["b_gate"][l] = db_gate[0]
        do = {}
        for tag, dy, wk, ok in (("a", dya, w_a, s["oa"]), ("b", dyb, w_b, s["ob"]), ("c", dyc, w_c, s["oc"])):
            do[tag] = _mm(dy, wk[l], "nt", BF16, "branch_dx_" + tag)
            gw["w_branch_" + tag][l] = _mm(ok, dy, "tn", BF16, "branch_dw_" + tag)
        duc, dw_pool, dscale = _pool_bwd(s["uc"], do["c"], w_pool[l], pool_scale[l])
        gw["w_pool"][l] = dw_pool
        gw["pool_scale"][l] = dscale[0]
        hd = s["heads"]
        dqa, dkc, dkp, dvc, dvp, dbias, dgq = _attn_a_bwd(hd[0], hd[1], hd[2], _to_heads(do["a"])[0], s["bias"],
                                                          q_norm_a[l], k_norm_a[l])
        dka, dva, dgk = _attn_a_bwd_keys(hd[1], dkc, dkp, dvc, dvp, k_norm_a[l])
        gw["q_norm_a"][l] = jnp.sum(dgq, axis=(0, 1))
        gw["k_norm_a"][l] = jnp.sum(dgk, axis=(0, 1))
        gw["rel_bias_a"][l] = _select_mm(dbias.reshape(N_HEADS, CHUNK * BAND), onehot, "nt", "rel_bias_table_dw")
        dqb, dkb, dvb = _attn_b_bwd(s["qkv"], s["carries"], do["b"])
        dqkv = jnp.concatenate([jnp.stack([dqa, dka, dva]).transpose(2, 0, 1, 3).reshape(T, 3 * WIDTH),
                                dqb, dkb.astype(BF16), dvb.astype(BF16)], axis=1)
        dh = _mm(dqkv, w_qkv[l], "nt", F32, "proj_qkv_dx")
        dh = _mm(duc, w_uc[l], "nt", F32, "proj_pool_dx", res=dh)
        dh = _mm(dglog, w_g[l], "nt", F32, "proj_gate_dx", res=dh)
        gw["w_in"][l] = jnp.concatenate([_mm(s["h"], dqkv, "tn", BF16, "proj_qkv_dw"),
                                         _mm(s["h"], duc, "tn", BF16, "proj_pool_dw"),
                                         _mm(s["h"], dglog, "tn", BF16, "proj_gate_dw")], axis=1)
        dcur, dg = _rmsnorm_bwd(s["x"], norm_mix[l], dh, dx1, "norm_mix_bwd")
        gw["norm_mix"][l] = dg[0]
    gw = {n: jnp.stack(g) for n, g in gw.items()}

    cw_pieces = gw["conv_w"].reshape(L, 3, N_DEV, -1).transpose(2, 0, 1, 3)
    cw_bits = lax.bitcast_convert_type(cw_pieces, BF16).reshape(N_DEV, -1)
    cw_rows = -(-cw_bits.shape[1] // (16 * PACK_COLS)) * 16
    cw_bits = jnp.pad(cw_bits, ((0, 0), (0, cw_rows * PACK_COLS - cw_bits.shape[1]))).reshape(N_DEV, cw_rows, PACK_COLS)
    pieces = jnp.concatenate([
        _split_columns(gw["w_in"]), _split_columns(gw["w_branch_a"]), _split_columns(gw["w_branch_b"]),
        _split_columns(gw["w_branch_c"]), _split_rows(gw["w_out"]), _split_columns(gw["w_up"]),
        _split_rows(gw["w_down"]), cw_bits], axis=1)
    parts = _exchange(pieces, True, "exchange_grads")
    small = _pack_rows([gw[n] for n in REPLICATED], SMALL_COLS, 16)
    small_parts = _exchange(small, False, "gather_small_grads")

    shard_like = [w[n] for n in SHARDED]
    res = _adamw(parts, *[_pack_rows([d[n] for n in SHARDED], PACK_COLS, 16) for d in (w, m, v)], "adamw_sharded")
    out = {n: r for n, r in zip(SHARDED, zip(*[_unpack_rows(r, shard_like) for r in res]))}
    rep_like = [w[n] for n in REPLICATED]
    res = _adamw(small_parts, *[_pack_rows([d[n] for n in REPLICATED], SMALL_COLS, 16) for d in (w, m, v)],
                 "adamw_replicated")
    out.update({n: r for n, r in zip(REPLICATED, zip(*[_unpack_rows(r, rep_like) for r in res]))})
    cw_parts = parts[:, parts.shape[1] - cw_rows:].reshape(N_DEV, -1)[:, :2 * conv_w.size].reshape(N_DEV, conv_w.size, 2)
    cw_parts = lax.bitcast_convert_type(cw_parts, F32)
    cw_parts = jnp.stack([_pack_rows([cw_parts[d]], SMALL_COLS, 16) for d in range(N_DEV)])
    res = _adamw(cw_parts, *[_pack_rows([d["conv_w"]], SMALL_COLS, 16) for d in (w, m, v)], "adamw_conv_w")
    out["conv_w"] = tuple(_unpack_rows(r, [conv_w])[0] for r in res)

    grads, deltas, new_m, new_v = ([out[n][i] for n in WEIGHTS] for i in range(4))
    return (loss, dcur.reshape(x.shape), *grads, *deltas, *new_m, *new_v)
```

```python
import functools
import math

import jax
import jax.numpy as jnp
from jax import lax
from jax.experimental import pallas as pl
from jax.experimental.pallas import tpu as pltpu

F32 = jnp.float32
BF16 = jnp.bfloat16

N_DEV = 8
D_MODEL = 1024
N_HEADS = 8
HEAD_DIM = 64
CHUNK = 64
N_LEFT = 8
BAND = (N_LEFT + 1) * CHUNK
WIDTH = N_HEADS * HEAD_DIM
POOL_WINDOWS = (2, 4, 8, 16)
POOL_DIM = 128
MAX_REL = 2 * CHUNK
REL_TABLE = MAX_REL + CHUNK
D_FF = 2816
EPS = 1e-6
SB_SCAN = 256
SB_ROWS = 512
SB_KEYS = 512
A_BLOCK = N_LEFT * CHUNK
HALO = 16
LANES = 128
VMEM_LIMIT = 56 * 1024 * 1024

ADAM_LR = 0.001
ADAM_B1 = 0.9
ADAM_B2 = 0.999
ADAM_EPS = 1e-08
ADAM_WD = 0.01
ADAM_STEP = 10

MESH = pl.DeviceIdType.MESH


def _params(sem):
    return pltpu.CompilerParams(dimension_semantics=sem, vmem_limit_bytes=VMEM_LIMIT)


def _pick(n, cap):
    if n <= cap:
        return n
    best = None
    for t in range(LANES, cap + 1, LANES):
        if n % t == 0:
            best = t
    assert best is not None, (n, cap)
    return best


MM_TILE_CAP = 1408


def _mm(a, b, mode, out_dtype, name, tm=MM_TILE_CAP, tn_cap=MM_TILE_CAP, tk_cap=MM_TILE_CAP, res=None):
    if mode == "nn":
        (M, K), (K2, N) = a.shape, b.shape
    elif mode == "nt":
        (M, K), (N, K2) = a.shape, b.shape
    else:
        (K, M), (K2, N) = a.shape, b.shape
    assert K == K2, (a.shape, b.shape, mode)
    tm = _pick(M, tm)
    tn = _pick(N, tn_cap)
    tk = _pick(K, tk_cap)
    nk = K // tk
    if mode == "nn":
        dims = (((1,), (0,)), ((), ()))
        a_spec = pl.BlockSpec((tm, tk), lambda i, j, k: (i, k))
        b_spec = pl.BlockSpec((tk, tn), lambda i, j, k: (k, j))
    elif mode == "nt":
        dims = (((1,), (1,)), ((), ()))
        a_spec = pl.BlockSpec((tm, tk), lambda i, j, k: (i, k))
        b_spec = pl.BlockSpec((tn, tk), lambda i, j, k: (j, k))
    else:
        dims = (((0,), (0,)), ((), ()))
        a_spec = pl.BlockSpec((tk, tm), lambda i, j, k: (k, i))
        b_spec = pl.BlockSpec((tk, tn), lambda i, j, k: (k, j))

    o_spec = pl.BlockSpec((tm, tn), lambda i, j, k: (i, j))

    def body(a_ref, b_ref, *rest):
        res_ref = rest[0] if res is not None else None
        o_ref = rest[1] if res is not None else rest[0]
        part = lax.dot_general(a_ref[...].astype(BF16), b_ref[...].astype(BF16), dims, preferred_element_type=F32)
        if nk == 1:
            o_ref[...] = (part if res is None else part + res_ref[...]).astype(out_dtype)
            return
        acc_ref = rest[-1]
        k = pl.program_id(2)

        @pl.when(k == 0)
        def _():
            acc_ref[...] = part

        @pl.when(k > 0)
        def _():
            acc_ref[...] += part

        @pl.when(k == nk - 1)
        def _():
            total = acc_ref[...] if res is None else acc_ref[...] + res_ref[...]
            o_ref[...] = total.astype(out_dtype)

    return pl.pallas_call(
        body, name=name,
        grid=(M // tm, N // tn, nk),
        in_specs=[a_spec, b_spec] + ([o_spec] if res is not None else []),
        out_specs=o_spec,
        out_shape=jax.ShapeDtypeStruct((M, N), out_dtype),
        scratch_shapes=[pltpu.VMEM((tm, tn), F32)] if nk > 1 else [],
        compiler_params=_params(("parallel", "parallel", "arbitrary")),
    )(*((a, b) if res is None else (a, b, res)))


def _rmsnorm_fwd(x, gain, name, tm=512):
    T, C = x.shape

    def body(x_ref, g_ref, h_ref):
        xv = x_ref[...]
        r = lax.rsqrt(jnp.mean(xv * xv, axis=-1, keepdims=True) + EPS)
        h_ref[...] = (xv * r * g_ref[...]).astype(BF16)

    return pl.pallas_call(
        body, name=name, grid=(T // tm,),
        in_specs=[pl.BlockSpec((tm, C), lambda i: (i, 0)), pl.BlockSpec((1, C), lambda i: (0, 0))],
        out_specs=pl.BlockSpec((tm, C), lambda i: (i, 0)),
        out_shape=jax.ShapeDtypeStruct((T, C), BF16),
        compiler_params=_params(("parallel",)),
    )(x, gain.reshape(1, C))


def _rmsnorm_bwd(x, gain, dh, dres, name, tm=512):
    T, C = x.shape

    def body(x_ref, g_ref, dh_ref, dres_ref, dx_ref, dg_ref):
        @pl.when(pl.program_id(0) == 0)
        def _():
            dg_ref[...] = jnp.zeros_like(dg_ref)

        xv = x_ref[...]
        dy = dh_ref[...].astype(F32)
        r = lax.rsqrt(jnp.mean(xv * xv, axis=-1, keepdims=True) + EPS)
        gdy = dy * g_ref[...]
        inner = jnp.mean(xv * gdy, axis=-1, keepdims=True)
        dx_ref[...] = dres_ref[...] + r * gdy - xv * (r * r * r * inner)
        dg_ref[...] += jnp.sum(dy * xv * r, axis=0, keepdims=True)

    return pl.pallas_call(
        body, name=name, grid=(T // tm,),
        in_specs=[pl.BlockSpec((tm, C), lambda i: (i, 0)), pl.BlockSpec((1, C), lambda i: (0, 0)),
                  pl.BlockSpec((tm, C), lambda i: (i, 0)), pl.BlockSpec((tm, C), lambda i: (i, 0))],
        out_specs=[pl.BlockSpec((tm, C), lambda i: (i, 0)), pl.BlockSpec((1, C), lambda i: (0, 0))],
        out_shape=[jax.ShapeDtypeStruct((T, C), F32), jax.ShapeDtypeStruct((1, C), F32)],
        compiler_params=_params(("arbitrary",)),
    )(x, gain.reshape(1, C), dh, dres)


def _head_norm(t, g):
    tf = t.astype(F32)
    r = lax.rsqrt(jnp.mean(tf * tf, axis=-1, keepdims=True) + EPS)
    return tf * r * g


def _a_logits(qn, kcat, bias, cc, first):
    qc = qn[cc * CHUNK:(cc + 1) * CHUNK]
    kb = kcat[cc * CHUNK:cc * CHUNK + BAND]
    s = lax.dot_general(qc, kb, (((1,), (1,)), ((), ())), preferred_element_type=F32) * (1.0 / math.sqrt(HEAD_DIM))
    s = s + bias
    col = lax.broadcasted_iota(jnp.int32, (CHUNK, BAND), 1) + cc * CHUNK
    return jnp.where(col >= jnp.where(first, A_BLOCK, 0), s, -1e30)


def _softmax(s):
    m = jnp.max(s, axis=-1, keepdims=True)
    e = jnp.exp(s - m)
    return e / jnp.sum(e, axis=-1, keepdims=True)


def _a_specs(T):
    nb = T // A_BLOCK
    cur = pl.BlockSpec((1, A_BLOCK, HEAD_DIM), lambda h, i: (h, i, 0))
    prev = pl.BlockSpec((1, A_BLOCK, HEAD_DIM), lambda h, i: (h, jnp.maximum(i - 1, 0), 0))
    nxt = pl.BlockSpec((1, A_BLOCK, HEAD_DIM), lambda h, i: (h, jnp.minimum(i + 1, nb - 1), 0))
    bias = pl.BlockSpec((1, CHUNK, BAND), lambda h, i: (h, 0, 0))
    gain = pl.BlockSpec((1, HEAD_DIM), lambda h, i: (0, 0))
    hacc = pl.BlockSpec((1, 1, HEAD_DIM), lambda h, i: (h, 0, 0))
    return nb, cur, prev, nxt, bias, gain, hacc


def _attn_a_fwd(q, k, v, bias, gq, gk):
    Hh, T, _ = q.shape
    nb, cur, prev, _, bspec, gspec, _ = _a_specs(T)

    def body(q_ref, kc_ref, kp_ref, vc_ref, vp_ref, b_ref, gq_ref, gk_ref, o_ref):
        first = pl.program_id(1) == 0
        qn = _head_norm(q_ref[0], gq_ref[...]).astype(BF16)
        kcat = jnp.concatenate([_head_norm(kp_ref[0], gk_ref[...]).astype(BF16),
                                _head_norm(kc_ref[0], gk_ref[...]).astype(BF16)], axis=0)
        vcat = jnp.concatenate([vp_ref[0], vc_ref[0]], axis=0)
        bias_v = b_ref[0]
        for cc in range(N_LEFT):
            p = _softmax(_a_logits(qn, kcat, bias_v, cc, first))
            o = jnp.dot(p.astype(BF16), vcat[cc * CHUNK:cc * CHUNK + BAND], preferred_element_type=F32)
            o_ref[0, cc * CHUNK:(cc + 1) * CHUNK, :] = o.astype(BF16)

    return pl.pallas_call(
        body, name="attn_a_fwd", grid=(Hh, nb),
        in_specs=[cur, cur, prev, cur, prev, bspec, gspec, gspec],
        out_specs=cur,
        out_shape=jax.ShapeDtypeStruct((Hh, T, HEAD_DIM), BF16),
        compiler_params=_params(("parallel", "arbitrary")),
    )(q, k, k, v, v, bias, gq.reshape(1, HEAD_DIM), gk.reshape(1, HEAD_DIM))


def _head_norm_bwd(t, g, dn):
    tf = t.astype(F32)
    r = lax.rsqrt(jnp.mean(tf * tf, axis=-1, keepdims=True) + EPS)
    gd = dn * g
    inner = jnp.mean(tf * gd, axis=-1, keepdims=True)
    dt = r * gd - tf * (r * r * r * inner)
    return dt, jnp.sum(dn * tf * r, axis=0, keepdims=True)


def _attn_a_bwd(q, k, v, do, bias, gq, gk):
    Hh, T, _ = q.shape
    nb, cur, prev, _, bspec, gspec, hacc = _a_specs(T)
    scale = 1.0 / math.sqrt(HEAD_DIM)

    def body(q_ref, kc_ref, kp_ref, vc_ref, vp_ref, do_ref, b_ref, gq_ref, gk_ref,
             dq_ref, dkc_ref, dkp_ref, dvc_ref, dvp_ref, db_ref, dgq_ref, dkcat_ref, dvcat_ref):
        i = pl.program_id(1)
        first = i == 0

        @pl.when(first)
        def _():
            db_ref[...] = jnp.zeros_like(db_ref)
            dgq_ref[...] = jnp.zeros_like(dgq_ref)

        qn = _head_norm(q_ref[0], gq_ref[...]).astype(BF16)
        kcat = jnp.concatenate([_head_norm(kp_ref[0], gk_ref[...]).astype(BF16),
                                _head_norm(kc_ref[0], gk_ref[...]).astype(BF16)], axis=0)
        vcat = jnp.concatenate([vp_ref[0], vc_ref[0]], axis=0)
        bias_v = b_ref[0]
        dov = do_ref[0]
        dkcat_ref[...] = jnp.zeros_like(dkcat_ref)
        dvcat_ref[...] = jnp.zeros_like(dvcat_ref)
        dqn_parts = []
        dbias = jnp.zeros((CHUNK, BAND), F32)
        for cc in range(N_LEFT):
            p = _softmax(_a_logits(qn, kcat, bias_v, cc, first))
            doc = dov[cc * CHUNK:(cc + 1) * CHUNK]
            qc = qn[cc * CHUNK:(cc + 1) * CHUNK]
            dp = lax.dot_general(doc, vcat[cc * CHUNK:cc * CHUNK + BAND], (((1,), (1,)), ((), ())),
                                 preferred_element_type=F32)
            delta = jnp.sum(p * dp, axis=-1, keepdims=True)
            ds = p * (dp - delta)
            dbias = dbias + ds
            dsb = (ds * scale).astype(BF16)
            dqn_parts.append(jnp.dot(dsb, kcat[cc * CHUNK:cc * CHUNK + BAND], preferred_element_type=F32))
            dkcat_ref[cc * CHUNK:cc * CHUNK + BAND, :] += lax.dot_general(
                dsb, qc, (((0,), (0,)), ((), ())), preferred_element_type=F32)
            dvcat_ref[cc * CHUNK:cc * CHUNK + BAND, :] += lax.dot_general(
                p.astype(BF16), doc, (((0,), (0,)), ((), ())), preferred_element_type=F32)
        db_ref[0] += dbias
        dqn = jnp.concatenate(dqn_parts, axis=0)
        dq, dg = _head_norm_bwd(q_ref[0], gq_ref[...], dqn)
        dq_ref[0] = dq.astype(BF16)
        dgq_ref[0] += dg
        dkp_ref[0] = dkcat_ref[0:A_BLOCK, :]
        dkc_ref[0] = dkcat_ref[A_BLOCK:2 * A_BLOCK, :]
        dvp_ref[0] = dvcat_ref[0:A_BLOCK, :]
        dvc_ref[0] = dvcat_ref[A_BLOCK:2 * A_BLOCK, :]

    blk_f32 = jax.ShapeDtypeStruct((Hh, T, HEAD_DIM), F32)
    return pl.pallas_call(
        body, name="attn_a_bwd", grid=(Hh, nb),
        in_specs=[cur, cur, prev, cur, prev, cur, bspec, gspec, gspec],
        out_specs=[cur, cur, cur, cur, cur, bspec, hacc],
        out_shape=[jax.ShapeDtypeStruct((Hh, T, HEAD_DIM), BF16), blk_f32, blk_f32, blk_f32, blk_f32,
                   jax.ShapeDtypeStruct((Hh, CHUNK, BAND), F32), jax.ShapeDtypeStruct((Hh, 1, HEAD_DIM), F32)],
        scratch_shapes=[pltpu.VMEM((2 * A_BLOCK, HEAD_DIM), F32), pltpu.VMEM((2 * A_BLOCK, HEAD_DIM), F32)],
        compiler_params=_params(("parallel", "arbitrary")),
    )(q, k, k, v, v, do, bias, gq.reshape(1, HEAD_DIM), gk.reshape(1, HEAD_DIM))


def _attn_a_bwd_keys(k, dkc, dkp, dvc, dvp, gk):
    Hh, T, _ = k.shape
    nb, cur, _, nxt, _, gspec, hacc = _a_specs(T)

    def body(k_ref, dkc_ref, dkp_ref, dvc_ref, dvp_ref, gk_ref, dk_ref, dv_ref, dgk_ref):
        i = pl.program_id(1)

        @pl.when(i == 0)
        def _():
            dgk_ref[...] = jnp.zeros_like(dgk_ref)

        has_next = (i < nb - 1).astype(F32)
        dkn = dkc_ref[0] + has_next * dkp_ref[0]
        dk, dg = _head_norm_bwd(k_ref[0], gk_ref[...], dkn)
        dk_ref[0] = dk.astype(BF16)
        dv_ref[0] = (dvc_ref[0] + has_next * dvp_ref[0]).astype(BF16)
        dgk_ref[0] += dg

    blk = jax.ShapeDtypeStruct((Hh, T, HEAD_DIM), BF16)
    return pl.pallas_call(
        body, name="attn_a_bwd_keys", grid=(Hh, nb),
        in_specs=[cur, cur, nxt, cur, nxt, gspec],
        out_specs=[cur, cur, hacc],
        out_shape=[blk, blk, jax.ShapeDtypeStruct((Hh, 1, HEAD_DIM), F32)],
        compiler_params=_params(("parallel", "arbitrary")),
    )(k, dkc, dkp, dvc, dvp, gk.reshape(1, HEAD_DIM))


def _scan_matrix(later):
    r = lax.broadcasted_iota(jnp.int32, (SB_SCAN, SB_SCAN), 0)
    c = lax.broadcasted_iota(jnp.int32, (SB_SCAN, SB_SCAN), 1)
    return jnp.where((r > c) if later else (r < c), 1.0, 0.0).astype(BF16)


def _running_sums(x, carry, scan, later):
    n = SB_KEYS // SB_SCAN
    parts = [None] * n
    total = carry
    for sb in (reversed(range(n)) if later else range(n)):
        xs = x[:, sb * SB_SCAN:(sb + 1) * SB_SCAN]
        local = jnp.dot(xs.astype(BF16), scan, preferred_element_type=F32)
        parts[sb] = local if total is None else local + total
        rowsum = jnp.sum(xs, axis=-1, keepdims=True)
        total = rowsum if total is None else total + rowsum
    return (parts[0] if n == 1 else jnp.concatenate(parts, axis=1)), total


def _sb_log_sigmoids(z):
    neg_abs = pltpu.bitcast(pltpu.bitcast(z, jnp.uint32) | jnp.uint32(0x80000000), F32)
    take = jnp.minimum(z, 0.0) - jnp.log(1.0 + jnp.exp(neg_abs))
    return take, take - z


def _sb_mask():
    r = lax.broadcasted_iota(jnp.int32, (2 * SB_ROWS, SB_KEYS), 0)
    c = lax.broadcasted_iota(jnp.int32, (2 * SB_ROWS, SB_KEYS), 1)
    return c < jnp.where(r >= SB_ROWS, r - SB_ROWS, r)


def _stack_heads(t):
    lane = lax.broadcasted_iota(jnp.int32, t.shape, 1)
    zero = jnp.zeros_like(t)
    return jnp.concatenate([jnp.where(lane < HEAD_DIM, t, zero), jnp.where(lane >= HEAD_DIM, t, zero)], axis=0)


def _unstack_heads(t):
    lane = lax.broadcasted_iota(jnp.int32, (SB_ROWS, 2 * HEAD_DIM), 1)
    return jnp.where(lane < HEAD_DIM, t[:SB_ROWS], t[SB_ROWS:])


def _sb_specs(T):
    nq = T // SB_ROWS
    blk = lambda col: pl.BlockSpec((SB_ROWS, 2 * HEAD_DIM), lambda p, i: (i, col + p))
    full = lambda col: pl.BlockSpec((T, 2 * HEAD_DIM), lambda p, i: (0, col + p))
    return nq, blk, full


def _key_rows(j):
    return pl.ds(pl.multiple_of(j * SB_KEYS, SB_KEYS), SB_KEYS)


def _attn_b_fwd(qkv):
    T = qkv.shape[0]
    pairs = N_HEADS // 2
    nq, blk, full = _sb_specs(T)
    scale = 1.0 / math.sqrt(HEAD_DIM)

    assert nq <= LANES

    def body(q_ref, k_ref, v_ref, o_ref, c_ref, acc_ref, carry_ref):
        i = pl.program_id(1)
        scan = _scan_matrix(True)
        qst = _stack_heads((q_ref[...].astype(F32) * scale).astype(BF16))
        lane = lax.broadcasted_iota(jnp.int32, (2 * SB_ROWS, LANES), 1)

        def span(j, carry, mask):
            z = lax.dot_general(qst, k_ref[_key_rows(j), :], (((1,), (1,)), ((), ())), preferred_element_type=F32)
            take, keep = _sb_log_sigmoids(z)
            if mask is not None:
                keep = jnp.where(mask, keep, 0.0)
            tail, total = _running_sums(keep, carry, scan, True)
            w = jnp.exp(take + tail)
            if mask is not None:
                w = jnp.where(mask, w, 0.0)
            return jnp.dot(w.astype(BF16), v_ref[_key_rows(j), :], preferred_element_type=F32), total

        acc_ref[...], carry_ref[...] = span(i, None, _sb_mask())
        c_ref[0, 0] = jnp.zeros((2 * SB_ROWS, LANES), F32)

        @pl.loop(0, i)
        def _(jj):
            j = i - 1 - jj
            carry = carry_ref[...]
            c_ref[0, 0] = jnp.where(lane == j, carry, c_ref[0, 0])
            out, carry_ref[...] = span(j, carry, None)
            acc_ref[...] += out

        o_ref[...] = _unstack_heads(acc_ref[...]).astype(BF16)

    return pl.pallas_call(
        body, name="attn_b_fwd", grid=(pairs, nq),
        in_specs=[blk(3 * pairs), full(4 * pairs), full(5 * pairs)],
        out_specs=[pl.BlockSpec((SB_ROWS, 2 * HEAD_DIM), lambda p, i: (i, p)),
                   pl.BlockSpec((1, 1, 2 * SB_ROWS, LANES), lambda p, i: (p, i, 0, 0))],
        out_shape=[jax.ShapeDtypeStruct((T, WIDTH), BF16), jax.ShapeDtypeStruct((pairs, nq, 2 * SB_ROWS, LANES), F32)],
        scratch_shapes=[pltpu.VMEM((2 * SB_ROWS, 2 * HEAD_DIM), F32), pltpu.VMEM((2 * SB_ROWS, 1), F32)],
        compiler_params=_params(("parallel", "arbitrary")),
    )(qkv, qkv, qkv)


def _attn_b_bwd(qkv, carries, do):
    T = qkv.shape[0]
    pairs = N_HEADS // 2
    nq, blk, full = _sb_specs(T)
    scale = 1.0 / math.sqrt(HEAD_DIM)
    oblk = pl.BlockSpec((SB_ROWS, 2 * HEAD_DIM), lambda p, i: (i, p))
    ofull = pl.BlockSpec((T, 2 * HEAD_DIM), lambda p, i: (0, p))

    def body(q_ref, k_ref, v_ref, c_ref, do_ref, dq_ref, dk_ref, dv_ref, dqacc_ref, before_ref):
        i = pl.program_id(1)

        @pl.when(i == 0)
        def _():
            dk_ref[...] = jnp.zeros_like(dk_ref)
            dv_ref[...] = jnp.zeros_like(dv_ref)

        scan_later = _scan_matrix(True)
        scan_earlier = _scan_matrix(False)
        qst = _stack_heads((q_ref[...].astype(F32) * scale).astype(BF16))
        dost = _stack_heads(do_ref[...].astype(BF16))
        lane = lax.broadcasted_iota(jnp.int32, (2 * SB_ROWS, LANES), 1)

        def span(j, later, mask):
            kv = k_ref[_key_rows(j), :]
            vv = v_ref[_key_rows(j), :]
            z = lax.dot_general(qst, kv, (((1,), (1,)), ((), ())), preferred_element_type=F32)
            take, keep = _sb_log_sigmoids(z)
            sig = jnp.exp(take)
            if mask is not None:
                keep = jnp.where(mask, keep, 0.0)
            tail, _ = _running_sums(keep, later, scan_later, True)
            w = jnp.exp(take + tail)
            if mask is not None:
                w = jnp.where(mask, w, 0.0)
            g = w * lax.dot_general(dost, vv, (((1,), (1,)), ((), ())), preferred_element_type=F32)
            before, before_ref[...] = _running_sums(g, before_ref[...], scan_earlier, False)
            dz = g - sig * (g + before)
            if mask is not None:
                dz = jnp.where(mask, dz, 0.0)
            dzb = dz.astype(BF16)
            dqacc_ref[...] += jnp.dot(dzb, kv, preferred_element_type=F32)
            dk_ref[_key_rows(j), :] += lax.dot_general(dzb, qst, (((0,), (0,)), ((), ())), preferred_element_type=F32)
            dv_ref[_key_rows(j), :] += lax.dot_general(w.astype(BF16), dost, (((0,), (0,)), ((), ())),
                                                       preferred_element_type=F32)

        dqacc_ref[...] = jnp.zeros_like(dqacc_ref)
        before_ref[...] = jnp.zeros_like(before_ref)

        @pl.loop(0, i)
        def _(j):
            later = jnp.sum(jnp.where(lane == j, c_ref[0, 0], 0.0), axis=-1, keepdims=True)
            span(j, later, None)

        span(i, None, _sb_mask())
        dq_ref[...] = (_unstack_heads(dqacc_ref[...]) * scale).astype(BF16)

    wide = jax.ShapeDtypeStruct((T, WIDTH), F32)
    return pl.pallas_call(
        body, name="attn_b_bwd", grid=(pairs, nq),
        in_specs=[blk(3 * pairs), full(4 * pairs), full(5 * pairs),
                  pl.BlockSpec((1, 1, 2 * SB_ROWS, LANES), lambda p, i: (p, i, 0, 0)), oblk],
        out_specs=[oblk, ofull, ofull],
        out_shape=[jax.ShapeDtypeStruct((T, WIDTH), BF16), wide, wide],
        scratch_shapes=[pltpu.VMEM((2 * SB_ROWS, 2 * HEAD_DIM), F32), pltpu.VMEM((2 * SB_ROWS, 1), F32)],
        compiler_params=_params(("parallel", "arbitrary")),
    )(qkv, qkv, qkv, carries, do)


def _window_sums(ext, forward):
    n = ext.shape[0]
    out = []
    s = ext
    for step in (1, 2, 4, 8):
        s = s + pltpu.roll(s, (n - step) if forward else step, 0)
        out.append(s)
    return out


def _pool_counts(base, rows, win):
    t = base + lax.broadcasted_iota(jnp.int32, (rows, 1), 0)
    return jnp.minimum(t + 1, win).astype(F32)


def _pooled(u_ref, up_ref, i, tm):
    prev = jnp.where(i > 0, up_ref[...], 0.0)
    ext = jnp.concatenate([prev, u_ref[...]], axis=0)
    sums = _window_sums(ext, False)
    parts = []
    for g, win in enumerate(POOL_WINDOWS):
        cols = slice(g * POOL_DIM, (g + 1) * POOL_DIM)
        cnt = _pool_counts(i * tm, tm, win)
        parts.append(sums[g][HALO:, cols] / cnt - ext[HALO:, cols])
    return parts


def _pool_fwd(ucg, w_pool, scale, tm=512):
    T = ucg.shape[0]
    C = WIDTH

    def body(u_ref, up_ref, w_ref, s_ref, o_ref):
        i = pl.program_id(0)
        parts = _pooled(u_ref, up_ref, i, tm)
        for g in range(len(POOL_WINDOWS)):
            mixed = jnp.dot(parts[g].astype(BF16), w_ref[g], preferred_element_type=F32)
            o_ref[:, g * POOL_DIM:(g + 1) * POOL_DIM] = (mixed * s_ref[:, g * POOL_DIM:(g + 1) * POOL_DIM]).astype(BF16)

    return pl.pallas_call(
        body, name="pool_fwd", grid=(T // tm,),
        in_specs=[pl.BlockSpec((tm, C), lambda i: (i, 0)),
                  pl.BlockSpec((HALO, C), lambda i: (jnp.maximum(i * (tm // HALO) - 1, 0), 0)),
                  pl.BlockSpec((len(POOL_WINDOWS), POOL_DIM, POOL_DIM), lambda i: (0, 0, 0)),
                  pl.BlockSpec((1, C), lambda i: (0, 0))],
        out_specs=pl.BlockSpec((tm, C), lambda i: (i, 0)),
        out_shape=jax.ShapeDtypeStruct((T, C), BF16),
        compiler_params=_params(("parallel",)),
    )(ucg, ucg, w_pool.astype(BF16), scale.reshape(1, C))


def _pool_bwd(ucg, do_c, w_pool, scale, tm=512):
    T = ucg.shape[0]
    C = WIDTH
    nt = T // tm
    G = len(POOL_WINDOWS)

    def body(u_ref, up_ref, do_ref, don_ref, w_ref, s_ref, du_ref, dw_ref, ds_ref):
        i = pl.program_id(0)

        @pl.when(i == 0)
        def _():
            dw_ref[...] = jnp.zeros_like(dw_ref)
            ds_ref[...] = jnp.zeros_like(ds_ref)

        parts = _pooled(u_ref, up_ref, i, tm)
        nxt = jnp.where(i < nt - 1, don_ref[...].astype(F32), 0.0)
        do_ext = jnp.concatenate([do_ref[...].astype(F32), nxt], axis=0) * s_ref[...]
        for g, win in enumerate(POOL_WINDOWS):
            cols = slice(g * POOL_DIM, (g + 1) * POOL_DIM)
            pooled_b = parts[g].astype(BF16)
            dmix = do_ext[:, cols].astype(BF16)
            mixed = jnp.dot(pooled_b, w_ref[g], preferred_element_type=F32)
            ds_ref[:, cols] += jnp.sum(do_ref[:, cols].astype(F32) * mixed, axis=0, keepdims=True)
            dw_ref[g] += lax.dot_general(pooled_b, dmix[:tm], (((0,), (0,)), ((), ())), preferred_element_type=F32)
            dpool = lax.dot_general(dmix, w_ref[g], (((1,), (1,)), ((), ())), preferred_element_type=F32)
            scaled = dpool / _pool_counts(i * tm, tm + HALO, win)
            fwd = _window_sums(scaled, True)[g]
            du_ref[:, cols] = (fwd[:tm] - dpool[:tm]).astype(BF16)

    return pl.pallas_call(
        body, name="pool_bwd", grid=(nt,),
        in_specs=[pl.BlockSpec((tm, C), lambda i: (i, 0)),
                  pl.BlockSpec((HALO, C), lambda i: (jnp.maximum(i * (tm // HALO) - 1, 0), 0)),
                  pl.BlockSpec((tm, C), lambda i: (i, 0)),
                  pl.BlockSpec((HALO, C), lambda i: (jnp.minimum((i + 1) * (tm // HALO), T // HALO - 1), 0)),
                  pl.BlockSpec((G, POOL_DIM, POOL_DIM), lambda i: (0, 0, 0)),
                  pl.BlockSpec((1, C), lambda i: (0, 0))],
        out_specs=[pl.BlockSpec((tm, C), lambda i: (i, 0)),
                   pl.BlockSpec((G, POOL_DIM, POOL_DIM), lambda i: (0, 0, 0)),
                   pl.BlockSpec((1, C), lambda i: (0, 0))],
        out_shape=[jax.ShapeDtypeStruct((T, C), BF16), jax.ShapeDtypeStruct((G, POOL_DIM, POOL_DIM), F32),
                   jax.ShapeDtypeStruct((1, C), F32)],
        compiler_params=_params(("arbitrary",)),
    )(ucg, ucg, do_c, do_c, w_pool.astype(BF16), scale.reshape(1, C))


def _merge_fwd(oa, ob, oc, glog, b_gate, wa, wb, wc, tm=256):
    T = oa.shape[0]
    Dm = D_MODEL
    row = lambda c: pl.BlockSpec((tm, c), lambda i: (i, 0))
    wspec = pl.BlockSpec((WIDTH, Dm), lambda i: (0, 0))

    def body(oa_ref, ob_ref, oc_ref, g_ref, b_ref, wa_ref, wb_ref, wc_ref, m_ref, ya_ref, yb_ref, yc_ref):
        merged = jnp.zeros((tm, Dm), F32)
        for kk, (o_ref, w_ref, y_ref) in enumerate(((oa_ref, wa_ref, ya_ref), (ob_ref, wb_ref, yb_ref),
                                                    (oc_ref, wc_ref, yc_ref))):
            y = jnp.dot(o_ref[...].astype(BF16), w_ref[...], preferred_element_type=F32)
            gate = jax.nn.sigmoid(g_ref[:, kk * Dm:(kk + 1) * Dm] + b_ref[:, kk * Dm:(kk + 1) * Dm])
            merged = merged + gate * y
            y_ref[...] = y.astype(BF16)
        m_ref[...] = merged.astype(BF16)

    out = jax.ShapeDtypeStruct((T, Dm), BF16)
    return pl.pallas_call(
        body, name="merge_fwd", grid=(T // tm,),
        in_specs=[row(WIDTH), row(WIDTH), row(WIDTH), row(3 * Dm), pl.BlockSpec((1, 3 * Dm), lambda i: (0, 0)),
                  wspec, wspec, wspec],
        out_specs=[row(Dm)] * 4,
        out_shape=[out] * 4,
        compiler_params=_params(("parallel",)),
    )(oa, ob, oc, glog, b_gate.reshape(1, 3 * Dm), wa, wb, wc)


def _merge_bwd(dmerged, glog, b_gate, ya, yb, yc, tm=256):
    T = dmerged.shape[0]
    Dm = D_MODEL
    row = lambda c: pl.BlockSpec((tm, c), lambda i: (i, 0))

    def body(dm_ref, g_ref, b_ref, ya_ref, yb_ref, yc_ref, dya_ref, dyb_ref, dyc_ref, dg_ref, db_ref):
        @pl.when(pl.program_id(0) == 0)
        def _():
            db_ref[...] = jnp.zeros_like(db_ref)

        dm = dm_ref[...]
        for kk, (y_ref, dy_ref) in enumerate(((ya_ref, dya_ref), (yb_ref, dyb_ref), (yc_ref, dyc_ref))):
            cols = slice(kk * Dm, (kk + 1) * Dm)
            gate = jax.nn.sigmoid(g_ref[:, cols] + b_ref[:, cols])
            dy_ref[...] = (dm * gate).astype(BF16)
            dlog = dm * y_ref[...].astype(F32) * gate * (1.0 - gate)
            dg_ref[:, cols] = dlog.astype(BF16)
            db_ref[:, cols] += jnp.sum(dlog, axis=0, keepdims=True)

    out = jax.ShapeDtypeStruct((T, Dm), BF16)
    return pl.pallas_call(
        body, name="merge_bwd", grid=(T // tm,),
        in_specs=[row(Dm), row(3 * Dm), pl.BlockSpec((1, 3 * Dm), lambda i: (0, 0)), row(Dm), row(Dm), row(Dm)],
        out_specs=[row(Dm), row(Dm), row(Dm), row(3 * Dm), pl.BlockSpec((1, 3 * Dm), lambda i: (0, 0))],
        out_shape=[out, out, out, jax.ShapeDtypeStruct((T, 3 * Dm), BF16), jax.ShapeDtypeStruct((1, 3 * Dm), F32)],
        compiler_params=_params(("arbitrary",)),
    )(dmerged, glog, b_gate.reshape(1, 3 * Dm), ya, yb, yc)


def _residual_add(x, y, name, tm=512):
    T, C = x.shape

    def body(x_ref, y_ref, o_ref):
        o_ref[...] = x_ref[...] + y_ref[...]

    spec = pl.BlockSpec((tm, C), lambda i: (i, 0))
    return pl.pallas_call(body, name=name, grid=(T // tm,), in_specs=[spec, spec], out_specs=spec,
                          out_shape=jax.ShapeDtypeStruct((T, C), F32), compiler_params=_params(("parallel",)))(x, y)


FF_TILE = 256
FF_TILES = D_FF // FF_TILE
CONV_HALO = 8


def _ff_pair_order(w):
    lead = w.shape[:-1]
    n = len(lead)
    w = w.reshape(*lead, 2, FF_TILES, FF_TILE)
    return jnp.swapaxes(w, n, n + 1).reshape(*lead, 2 * D_FF)


def _ff_natural_order(w):
    lead = w.shape[:-1]
    n = len(lead)
    w = w.reshape(*lead, FF_TILES, 2, FF_TILE)
    return jnp.swapaxes(w, n, n + 1).reshape(*lead, 2 * D_FF)


def _conv(ext, w_ref, b_ref):
    c = b_ref[...] + w_ref[2:3, :] * ext
    c = c + w_ref[1:2, :] * pltpu.roll(ext, 1, 0)
    c = c + w_ref[0:1, :] * pltpu.roll(ext, 2, 0)
    return c[CONV_HALO:]


def _ff_specs(T, tm):
    pair = pl.BlockSpec((tm, 2 * FF_TILE), lambda i, j: (i, j))
    prev = pl.BlockSpec((CONV_HALO, 2 * FF_TILE), lambda i, j: (jnp.maximum(i * (tm // CONV_HALO) - 1, 0), j))
    nxt = pl.BlockSpec((CONV_HALO, 2 * FF_TILE),
                       lambda i, j: (jnp.minimum((i + 1) * (tm // CONV_HALO), T // CONV_HALO - 1), j))
    half = pl.BlockSpec((tm, FF_TILE), lambda i, j: (i, j))
    small = lambda r: pl.BlockSpec((r, 2 * FF_TILE), lambda i, j: (0, j))
    return pair, prev, nxt, half, small


def _swap_grid(spec):
    return pl.BlockSpec(spec.block_shape, lambda j, i, f=spec.index_map: f(i, j))


def _ff_act_fwd(u, conv_w, conv_b, tm=512):
    T = u.shape[0]
    pair, prev, _, half, small = _ff_specs(T, tm)

    def body(u_ref, p_ref, w_ref, b_ref, a_ref):
        i = pl.program_id(0)
        c = _conv(jnp.concatenate([jnp.where(i > 0, p_ref[...], 0.0), u_ref[...]], axis=0), w_ref, b_ref)
        cg, cv = c[:, :FF_TILE], c[:, FF_TILE:]
        a_ref[...] = (cg * jax.nn.sigmoid(cg) * cv).astype(BF16)

    return pl.pallas_call(
        body, name="ff_act_fwd", grid=(T // tm, FF_TILES),
        in_specs=[pair, prev, small(3), small(1)],
        out_specs=half,
        out_shape=jax.ShapeDtypeStruct((T, D_FF), BF16),
        compiler_params=_params(("parallel", "parallel")),
    )(u, u, conv_w, conv_b.reshape(1, -1))


def _ff_act_bwd(u, da, conv_w, conv_b, tm=512):
    T = u.shape[0]
    pair, prev, _, half, small = _ff_specs(T, tm)

    def body(u_ref, p_ref, da_ref, w_ref, b_ref, dc_ref, dw_ref, db_ref):
        i = pl.program_id(1)

        @pl.when(i == 0)
        def _():
            dw_ref[...] = jnp.zeros_like(dw_ref)
            db_ref[...] = jnp.zeros_like(db_ref)

        ext = jnp.concatenate([jnp.where(i > 0, p_ref[...], 0.0), u_ref[...]], axis=0)
        c = _conv(ext, w_ref, b_ref)
        cg, cv = c[:, :FF_TILE], c[:, FF_TILE:]
        da = da_ref[...]
        sg = jax.nn.sigmoid(cg)
        dc = jnp.concatenate([da * cv * sg * (1.0 + cg * (1.0 - sg)), da * cg * sg], axis=1)
        dc_ref[...] = dc
        db_ref[...] += jnp.sum(dc, axis=0, keepdims=True)
        dw_ref[2:3, :] += jnp.sum(dc * ext[CONV_HALO:], axis=0, keepdims=True)
        dw_ref[1:2, :] += jnp.sum(dc * pltpu.roll(ext, 1, 0)[CONV_HALO:], axis=0, keepdims=True)
        dw_ref[0:1, :] += jnp.sum(dc * pltpu.roll(ext, 2, 0)[CONV_HALO:], axis=0, keepdims=True)

    return pl.pallas_call(
        body, name="ff_act_bwd", grid=(FF_TILES, T // tm),
        in_specs=[_swap_grid(pair), _swap_grid(prev), _swap_grid(half), _swap_grid(small(3)), _swap_grid(small(1))],
        out_specs=[_swap_grid(pair), _swap_grid(small(3)), _swap_grid(small(1))],
        out_shape=[jax.ShapeDtypeStruct((T, 2 * D_FF), F32), jax.ShapeDtypeStruct((3, 2 * D_FF), F32),
                   jax.ShapeDtypeStruct((1, 2 * D_FF), F32)],
        compiler_params=_params(("parallel", "arbitrary")),
    )(u, u, da, conv_w, conv_b.reshape(1, -1))


def _ff_conv_bwd(dc, conv_w, tm=512):
    T = dc.shape[0]
    nt = T // tm
    pair, _, nxt, _, small = _ff_specs(T, tm)

    def body(dc_ref, n_ref, w_ref, du_ref):
        i = pl.program_id(0)
        ext = jnp.concatenate([dc_ref[...], jnp.where(i < nt - 1, n_ref[...], 0.0)], axis=0)
        n = tm + CONV_HALO
        du = w_ref[2:3, :] * ext + w_ref[1:2, :] * pltpu.roll(ext, n - 1, 0) + w_ref[0:1, :] * pltpu.roll(ext, n - 2, 0)
        du_ref[...] = du[:tm].astype(BF16)

    return pl.pallas_call(
        body, name="ff_conv_bwd", grid=(nt, FF_TILES),
        in_specs=[pair, nxt, small(3)],
        out_specs=pair,
        out_shape=jax.ShapeDtypeStruct((T, 2 * D_FF), BF16),
        compiler_params=_params(("parallel", "parallel")),
    )(dc, dc, conv_w)


def _loss_head(y, target, tm=512):
    T, C = y.shape
    nt = T // tm

    def body(y_ref, t_ref, dy_ref, l_ref):
        err = y_ref[...] - t_ref[...]
        dy_ref[...] = err * (1.0 / C)
        part = jnp.sum(err * err, axis=0, keepdims=True) * (0.5 / C)
        l_ref[0] = jnp.broadcast_to(part, (8, C))

    spec = pl.BlockSpec((tm, C), lambda i: (i, 0))
    dy, parts = pl.pallas_call(
        body, name="loss_head", grid=(nt,),
        in_specs=[spec, spec],
        out_specs=[spec, pl.BlockSpec((1, 8, C), lambda i: (i, 0, 0))],
        out_shape=[jax.ShapeDtypeStruct((T, C), F32), jax.ShapeDtypeStruct((nt, 8, C), F32)],
        compiler_params=_params(("parallel",)),
    )(y, target)
    return dy, jnp.sum(parts[:, 0, :])


def _adamw_math(w, g, m, v):
    m = ADAM_B1 * m + (1.0 - ADAM_B1) * g
    v = ADAM_B2 * v + (1.0 - ADAM_B2) * (g * g)
    m_hat = m / (1.0 - ADAM_B1 ** ADAM_STEP)
    v_hat = v / (1.0 - ADAM_B2 ** ADAM_STEP)
    delta = -ADAM_LR * (m_hat / (jnp.sqrt(v_hat) + ADAM_EPS) + ADAM_WD * w)
    return delta, m, v


def _adamw(parts, w, m, v, name, tm=256):
    R, C = w.shape
    tm = _pick_rows(R, tm)

    def body(p_ref, w_ref, m_ref, v_ref, g_ref, d_ref, nm_ref, nv_ref):
        g = p_ref[0].astype(F32)
        for s in range(1, N_DEV):
            g = g + p_ref[s].astype(F32)
        delta, nm, nv = _adamw_math(w_ref[...], g, m_ref[...], v_ref[...])
        g_ref[...] = g
        d_ref[...] = delta
        nm_ref[...] = nm
        nv_ref[...] = nv

    spec = pl.BlockSpec((tm, C), lambda i: (i, 0))
    out = jax.ShapeDtypeStruct((R, C), F32)
    return pl.pallas_call(
        body, name=name, grid=(R // tm,),
        in_specs=[pl.BlockSpec((N_DEV, tm, C), lambda i: (0, i, 0)), spec, spec, spec],
        out_specs=[spec] * 4,
        out_shape=[out] * 4,
        compiler_params=_params(("parallel",)),
    )(parts, w, m, v)


def _pick_rows(n, cap):
    best = None
    for t in range(16, min(n, cap) + 1, 16):
        if n % t == 0:
            best = t
    assert best is not None, (n, cap)
    return best


def _exchange(src, scatter, name):
    slab = src.shape[1:] if scatter else src.shape

    def body(src_ref, out_ref, send_sems, recv_sems, local_sem):
        x, y, c = lax.axis_index("x"), lax.axis_index("y"), lax.axis_index("c")
        me = 4 * x + 2 * y + c

        def piece(d):
            return src_ref.at[d] if scatter else src_ref

        mine = pltpu.make_async_copy(piece(me), out_ref.at[me], local_sem)
        mine.start()
        copies = []
        for k in range(1, N_DEV):
            px = 1 - x if k & 4 else x
            py = 1 - y if k & 2 else y
            pc = 1 - c if k & 1 else c
            peer = 4 * px + 2 * py + pc
            cp = pltpu.make_async_remote_copy(
                src_ref=piece(peer), dst_ref=out_ref.at[me],
                send_sem=send_sems.at[k], recv_sem=recv_sems.at[k],
                device_id=(px, py, pc), device_id_type=MESH)
            cp.start()
            copies.append((cp, peer))
        for k, (cp, peer) in enumerate(copies, start=1):
            cp.wait_send()
            pltpu.make_async_remote_copy(
                src_ref=piece(peer), dst_ref=out_ref.at[peer],
                send_sem=send_sems.at[k], recv_sem=recv_sems.at[k],
                device_id=(x, y, c), device_id_type=MESH).wait_recv()
        mine.wait()

    return pl.pallas_call(
        body, name=name,
        in_specs=[pl.BlockSpec(memory_space=pl.ANY)],
        out_specs=pl.BlockSpec(memory_space=pl.ANY),
        out_shape=jax.ShapeDtypeStruct((N_DEV,) + tuple(slab), src.dtype),
        scratch_shapes=[pltpu.SemaphoreType.DMA((N_DEV,)), pltpu.SemaphoreType.DMA((N_DEV,)),
                        pltpu.SemaphoreType.DMA],
    )(src)


SHARDED = ("w_in", "w_branch_a", "w_branch_b", "w_branch_c", "w_out", "w_up", "w_down")
REPLICATED = ("norm_mix", "b_gate", "q_norm_a", "k_norm_a", "rel_bias_a", "w_pool", "pool_scale", "norm_ffn", "conv_b")
WEIGHTS = ("norm_mix", "w_in", "b_gate", "q_norm_a", "k_norm_a", "rel_bias_a", "w_pool", "pool_scale",
           "w_branch_a", "w_branch_b", "w_branch_c", "w_out", "norm_ffn", "w_up", "conv_w", "conv_b", "w_down")
PACK_COLS = 1024
SMALL_COLS = 128
QKV_COLS = 6 * WIDTH


def _rel_index():
    q_off = jnp.arange(CHUNK)[:, None] + N_LEFT * CHUNK
    k_off = jnp.arange(BAND)[None, :]
    return jnp.clip(q_off - k_off, -(CHUNK - 1), MAX_REL) + (CHUNK - 1)


def _rel_onehot():
    rel = _rel_index().reshape(1, CHUNK * BAND)
    return (rel == jnp.arange(REL_TABLE)[:, None]).astype(BF16)


def _select_mm(x, onehot, mode, name):
    hi = x.astype(BF16)
    r1 = x - hi.astype(F32)
    mid = r1.astype(BF16)
    lo = (r1 - mid.astype(F32)).astype(BF16)
    y = _mm(jnp.concatenate([hi, mid, lo, jnp.zeros_like(hi)], axis=0), onehot, mode, F32, name)
    n = x.shape[0]
    return y[:n] + y[n:2 * n] + y[2 * n:3 * n]


def _pack_rows(arrays, cols, row_multiple):
    flat = jnp.concatenate([a.reshape(-1) for a in arrays])
    rows = -(-flat.shape[0] // cols)
    rows = -(-rows // row_multiple) * row_multiple
    return jnp.pad(flat, (0, rows * cols - flat.shape[0])).reshape(rows, cols)


def _unpack_rows(packed, like):
    flat = packed.reshape(-1)
    out, off = [], 0
    for a in like:
        out.append(flat[off:off + a.size].reshape(a.shape))
        off += a.size
    return out


def _to_heads(t):
    T = t.shape[0]
    return t.reshape(T, -1, N_HEADS, HEAD_DIM).transpose(1, 2, 0, 3)


def _from_heads(t):
    return t.transpose(1, 0, 2).reshape(t.shape[1], WIDTH)


def _gather_columns(g, shape):
    L, rows, cols = shape
    return g.reshape(N_DEV, L, rows, cols).transpose(1, 2, 0, 3).reshape(L, rows, N_DEV * cols)


def _gather_rows(g, shape):
    L, rows, cols = shape
    return g.reshape(N_DEV, L, rows, cols).transpose(1, 0, 2, 3).reshape(L, N_DEV * rows, cols)


def _split_columns(full):
    L, rows, allc = full.shape
    cols = allc // N_DEV
    return full.reshape(L, rows, N_DEV, cols).transpose(2, 0, 1, 3).reshape(N_DEV, -1, PACK_COLS)


def _split_rows(full):
    L, allr, cols = full.shape
    rows = allr // N_DEV
    return full.reshape(L, N_DEV, rows, cols).transpose(1, 0, 2, 3).reshape(N_DEV, -1, PACK_COLS)


def kernel(x, norm_mix, w_in, b_gate, q_norm_a, k_norm_a, rel_bias_a, w_pool, pool_scale, w_branch_a, w_branch_b, w_branch_c, w_out, norm_ffn, w_up, conv_w, conv_b, w_down, loss_target, m_norm_mix, m_w_in, m_b_gate, m_q_norm_a, m_k_norm_a, m_rel_bias_a, m_w_pool, m_pool_scale, m_w_branch_a, m_w_branch_b, m_w_branch_c, m_w_out, m_norm_ffn, m_w_up, m_conv_w, m_conv_b, m_w_down, v_norm_mix, v_w_in, v_b_gate, v_q_norm_a, v_k_norm_a, v_rel_bias_a, v_w_pool, v_pool_scale, v_w_branch_a, v_w_branch_b, v_w_branch_c, v_w_out, v_norm_ffn, v_w_up, v_conv_w, v_conv_b, v_w_down):
    args = dict(locals())
    w = {n: args[n] for n in WEIGHTS}
    m = {n: args["m_" + n] for n in WEIGHTS}
    v = {n: args["v_" + n] for n in WEIGHTS}
    L = w_in.shape[0]
    T = x.shape[1]
    xs = x.reshape(T, D_MODEL)
    target = loss_target.reshape(T, D_MODEL)

    conv_bits = lax.bitcast_convert_type(conv_w, BF16)
    packed = _pack_rows([w[n].astype(BF16) for n in SHARDED] + [conv_bits], PACK_COLS, 16)
    gathered = _exchange(packed, False, "gather_weights")
    rows = [w[n].size // PACK_COLS for n in SHARDED]
    offs = [sum(rows[:i]) for i in range(len(rows) + 1)]
    part = {n: gathered[:, offs[i]:offs[i + 1]] for i, n in enumerate(SHARDED)}
    w_in_f = _gather_columns(part["w_in"], w_in.shape)
    w_qkv, w_uc, w_g = w_in_f[:, :, :QKV_COLS], w_in_f[:, :, QKV_COLS:QKV_COLS + WIDTH], w_in_f[:, :, QKV_COLS + WIDTH:]
    w_a = _gather_columns(part["w_branch_a"], w_branch_a.shape)
    w_b = _gather_columns(part["w_branch_b"], w_branch_b.shape)
    w_c = _gather_columns(part["w_branch_c"], w_branch_c.shape)
    w_out_f = _gather_rows(part["w_out"], w_out.shape)
    w_up_f = _ff_pair_order(_gather_columns(part["w_up"], w_up.shape))
    w_down_f = _gather_rows(part["w_down"], w_down.shape)
    conv_flat = gathered[:, offs[-1]:].reshape(N_DEV, -1)[:, :conv_bits.size].reshape((N_DEV,) + conv_bits.shape)
    conv_w_f = _ff_pair_order(_gather_columns(lax.bitcast_convert_type(conv_flat, F32).reshape(N_DEV, -1), conv_w.shape))
    conv_b_f = _ff_pair_order(conv_b)
    onehot = _rel_onehot()

    saved = []
    cur = xs
    for l in range(L):
        h = _rmsnorm_fwd(cur, norm_mix[l], "norm_mix_fwd")
        qkv = _mm(h, w_qkv[l], "nn", BF16, "proj_qkv")
        uc = _mm(h, w_uc[l], "nn", F32, "proj_pool")
        glog = _mm(h, w_g[l], "nn", F32, "proj_gate")
        heads = _to_heads(qkv[:, :3 * WIDTH])
        bias = _select_mm(rel_bias_a[l], onehot, "nn", "rel_bias_table").reshape(N_HEADS, CHUNK, BAND)
        oa = _from_heads(_attn_a_fwd(heads[0], heads[1], heads[2], bias, q_norm_a[l], k_norm_a[l]))
        ob, carries = _attn_b_fwd(qkv)
        oc = _pool_fwd(uc, w_pool[l], pool_scale[l])
        merged, ya, yb, yc = _merge_fwd(oa, ob, oc, glog, b_gate[l], w_a[l], w_b[l], w_c[l])
        x1 = _mm(merged, w_out_f[l], "nn", F32, "out_proj", res=cur)
        h2 = _rmsnorm_fwd(x1, norm_ffn[l], "norm_ffn_fwd")
        u = _mm(h2, w_up_f[l], "nn", F32, "ff_up")
        act = _ff_act_fwd(u, conv_w_f[l], conv_b_f[l])
        x2 = _mm(act, w_down_f[l], "nn", F32, "ff_down", res=x1)
        saved.append(dict(x=cur, h=h, qkv=qkv, carries=carries, heads=heads, uc=uc, glog=glog, bias=bias, oa=oa, ob=ob, oc=oc, ya=ya, yb=yb,
                          yc=yc, merged=merged, x1=x1, h2=h2, u=u, act=act))
        cur = x2

    dcur, loss_local = _loss_head(cur, target)
    loss = lax.psum(loss_local, ("x", "y", "c"))

    gw = {n: [None] * L for n in WEIGHTS}
    for l in reversed(range(L)):
        s = saved[l]
        da = _mm(dcur, w_down_f[l], "nt", F32, "ff_down_dx", tn_cap=1408)
        gw["w_down"][l] = _mm(s["act"], dcur, "tn", BF16, "ff_down_dw")
        dc, dconv_w, dconv_b = _ff_act_bwd(s["u"], da, conv_w_f[l], conv_b_f[l])
        du = _ff_conv_bwd(dc, conv_w_f[l])
        dh2 = _mm(du, w_up_f[l], "nt", F32, "ff_up_dx")
        gw["w_up"][l] = _ff_natural_order(_mm(s["h2"], du, "tn", BF16, "ff_up_dw"))
        gw["conv_w"][l] = _ff_natural_order(dconv_w)
        gw["conv_b"][l] = _ff_natural_order(dconv_b)[0]
        dx1, dg = _rmsnorm_bwd(s["x1"], norm_ffn[l], dh2, dcur, "norm_ffn_bwd")
        gw["norm_ffn"][l] = dg[0]

        dmerged = _mm(dx1, w_out_f[l], "nt", F32, "out_proj_dx")
        gw["w_out"][l] = _mm(s["merged"], dx1, "tn", BF16, "out_proj_dw")
        dya, dyb, dyc, dglog, db_gate = _merge_bwd(dmerged, s["glog"], b_gate[l], s["ya"], s["yb"], s["yc"])
        gw["b_gate"][l] = db_gate[0]
        do = {}
        for tag, dy, wk, ok in (("a", dya, w_a, s["oa"]), ("b", dyb, w_b, s["ob"]), ("c", dyc, w_c, s["oc"])):
            do[tag] = _mm(dy, wk[l], "nt", BF16, "branch_dx_" + tag)
            gw["w_branch_" + tag][l] = _mm(ok, dy, "tn", BF16, "branch_dw_" + tag)
        duc, dw_pool, dscale = _pool_bwd(s["uc"], do["c"], w_pool[l], pool_scale[l])
        gw["w_pool"][l] = dw_pool
        gw["pool_scale"][l] = dscale[0]
        hd = s["heads"]
        dqa, dkc, dkp, dvc, dvp, dbias, dgq = _attn_a_bwd(hd[0], hd[1], hd[2], _to_heads(do["a"])[0], s["bias"],
                                                          q_norm_a[l], k_norm_a[l])
        dka, dva, dgk = _attn_a_bwd_keys(hd[1], dkc, dkp, dvc, dvp, k_norm_a[l])
        gw["q_norm_a"][l] = jnp.sum(dgq, axis=(0, 1))
        gw["k_norm_a"][l] = jnp.sum(dgk, axis=(0, 1))
        gw["rel_bias_a"][l] = _select_mm(dbias.reshape(N_HEADS, CHUNK * BAND), onehot, "nt", "rel_bias_table_dw")
        dqb, dkb, dvb = _attn_b_bwd(s["qkv"], s["carries"], do["b"])
        dqkv = jnp.concatenate([jnp.stack([dqa, dka, dva]).transpose(2, 0, 1, 3).reshape(T, 3 * WIDTH),
                                dqb, dkb.astype(BF16), dvb.astype(BF16)], axis=1)
        dh = _mm(dqkv, w_qkv[l], "nt", F32, "proj_qkv_dx")
        dh = _mm(duc, w_uc[l], "nt", F32, "proj_pool_dx", res=dh)
        dh = _mm(dglog, w_g[l], "nt", F32, "proj_gate_dx", res=dh)
        gw["w_in"][l] = jnp.concatenate([_mm(s["h"], dqkv, "tn", BF16, "proj_qkv_dw"),
                                         _mm(s["h"], duc, "tn", BF16, "proj_pool_dw"),
                                         _mm(s["h"], dglog, "tn", BF16, "proj_gate_dw")], axis=1)
        dcur, dg = _rmsnorm_bwd(s["x"], norm_mix[l], dh, dx1, "norm_mix_bwd")
        gw["norm_mix"][l] = dg[0]
    gw = {n: jnp.stack(g) for n, g in gw.items()}

    cw_pieces = gw["conv_w"].reshape(L, 3, N_DEV, -1).transpose(2, 0, 1, 3)
    cw_bits = lax.bitcast_convert_type(cw_pieces, BF16).reshape(N_DEV, -1)
    cw_rows = -(-cw_bits.shape[1] // (16 * PACK_COLS)) * 16
    cw_bits = jnp.pad(cw_bits, ((0, 0), (0, cw_rows * PACK_COLS - cw_bits.shape[1]))).reshape(N_DEV, cw_rows, PACK_COLS)
    pieces = jnp.concatenate([
        _split_columns(gw["w_in"]), _split_columns(gw["w_branch_a"]), _split_columns(gw["w_branch_b"]),
        _split_columns(gw["w_branch_c"]), _split_rows(gw["w_out"]), _split_columns(gw["w_up"]),
        _split_rows(gw["w_down"]), cw_bits], axis=1)
    parts = _exchange(pieces, True, "exchange_grads")
    small = _pack_rows([gw[n] for n in REPLICATED], SMALL_COLS, 16)
    small_parts = _exchange(small, False, "gather_small_grads")

    shard_like = [w[n] for n in SHARDED]
    res = _adamw(parts, *[_pack_rows([d[n] for n in SHARDED], PACK_COLS, 16) for d in (w, m, v)], "adamw_sharded")
    out = {n: r for n, r in zip(SHARDED, zip(*[_unpack_rows(r, shard_like) for r in res]))}
    rep_like = [w[n] for n in REPLICATED]
    res = _adamw(small_parts, *[_pack_rows([d[n] for n in REPLICATED], SMALL_COLS, 16) for d in (w, m, v)],
                 "adamw_replicated")
    out.update({n: r for n, r in zip(REPLICATED, zip(*[_unpack_rows(r, rep_like) for r in res]))})
    cw_parts = parts[:, parts.shape[1] - cw_rows:].reshape(N_DEV, -1)[:, :2 * conv_w.size].reshape(N_DEV, conv_w.size, 2)
    cw_parts = lax.bitcast_convert_type(cw_parts, F32)
    cw_parts = jnp.stack([_pack_rows([cw_parts[d]], SMALL_COLS, 16) for d in range(N_DEV)])
    res = _adamw(cw_parts, *[_pack_rows([d["conv_w"]], SMALL_COLS, 16) for d in (w, m, v)], "adamw_conv_w")
    out["conv_w"] = tuple(_unpack_rows(r, [conv_w])[0] for r in res)

    grads, deltas, new_m, new_v = ([out[n][i] for n in WEIGHTS] for i in range(4))
    return (loss, dcur.reshape(x.shape), *grads, *deltas, *new_m, *new_v)
```

```python
import functools
import math

import jax
import jax.numpy as jnp
from jax import lax
from jax.experimental import pallas as pl
from jax.experimental.pallas import tpu as pltpu

F32 = jnp.float32
BF16 = jnp.bfloat16

N_DEV = 8
D_MODEL = 1024
N_HEADS = 8
HEAD_DIM = 64
CHUNK = 64
N_LEFT = 8
BAND = (N_LEFT + 1) * CHUNK
WIDTH = N_HEADS * HEAD_DIM
POOL_WINDOWS = (2, 4, 8, 16)
POOL_DIM = 128
MAX_REL = 2 * CHUNK
REL_TABLE = MAX_REL + CHUNK
D_FF = 2816
EPS = 1e-6
SB_SCAN = 256
SB_ROWS = 512
SB_KEYS = 512
A_BLOCK = N_LEFT * CHUNK
HALO = 16
LANES = 128
VMEM_LIMIT = 56 * 1024 * 1024

ADAM_LR = 0.001
ADAM_B1 = 0.9
ADAM_B2 = 0.999
ADAM_EPS = 1e-08
ADAM_WD = 0.01
ADAM_STEP = 10

MESH = pl.DeviceIdType.MESH


def _params(sem):
    return pltpu.CompilerParams(dimension_semantics=sem, vmem_limit_bytes=VMEM_LIMIT)


def _pick(n, cap):
    if n <= cap:
        return n
    best = None
    for t in range(LANES, cap + 1, LANES):
        if n % t == 0:
            best = t
    assert best is not None, (n, cap)
    return best


MM_TILE_CAP = 1408


def _mm(a, b, mode, out_dtype, name, tm=MM_TILE_CAP, tn_cap=MM_TILE_CAP, tk_cap=MM_TILE_CAP, res=None):
    if mode == "nn":
        (M, K), (K2, N) = a.shape, b.shape
    elif mode == "nt":
        (M, K), (N, K2) = a.shape, b.shape
    else:
        (K, M), (K2, N) = a.shape, b.shape
    assert K == K2, (a.shape, b.shape, mode)
    tm = _pick(M, tm)
    tn = _pick(N, tn_cap)
    tk = _pick(K, tk_cap)
    nk = K // tk
    if mode == "nn":
        dims = (((1,), (0,)), ((), ()))
        a_spec = pl.BlockSpec((tm, tk), lambda i, j, k: (i, k))
        b_spec = pl.BlockSpec((tk, tn), lambda i, j, k: (k, j))
    elif mode == "nt":
        dims = (((1,), (1,)), ((), ()))
        a_spec = pl.BlockSpec((tm, tk), lambda i, j, k: (i, k))
        b_spec = pl.BlockSpec((tn, tk), lambda i, j, k: (j, k))
    else:
        dims = (((0,), (0,)), ((), ()))
        a_spec = pl.BlockSpec((tk, tm), lambda i, j, k: (k, i))
        b_spec = pl.BlockSpec((tk, tn), lambda i, j, k: (k, j))

    o_spec = pl.BlockSpec((tm, tn), lambda i, j, k: (i, j))

    def body(a_ref, b_ref, *rest):
        res_ref = rest[0] if res is not None else None
        o_ref = rest[1] if res is not None else rest[0]
        part = lax.dot_general(a_ref[...].astype(BF16), b_ref[...].astype(BF16), dims, preferred_element_type=F32)
        if nk == 1:
            o_ref[...] = (part if res is None else part + res_ref[...]).astype(out_dtype)
            return
        acc_ref = rest[-1]
        k = pl.program_id(2)

        @pl.when(k == 0)
        def _():
            acc_ref[...] = part

        @pl.when(k > 0)
        def _():
            acc_ref[...] += part

        @pl.when(k == nk - 1)
        def _():
            total = acc_ref[...] if res is None else acc_ref[...] + res_ref[...]
            o_ref[...] = total.astype(out_dtype)

    return pl.pallas_call(
        body, name=name,
        grid=(M // tm, N // tn, nk),
        in_specs=[a_spec, b_spec] + ([o_spec] if res is not None else []),
        out_specs=o_spec,
        out_shape=jax.ShapeDtypeStruct((M, N), out_dtype),
        scratch_shapes=[pltpu.VMEM((tm, tn), F32)] if nk > 1 else [],
        compiler_params=_params(("parallel", "parallel", "arbitrary")),
    )(*((a, b) if res is None else (a, b, res)))


def _rmsnorm_fwd(x, gain, name, tm=512):
    T, C = x.shape

    def body(x_ref, g_ref, h_ref):
        xv = x_ref[...]
        r = lax.rsqrt(jnp.mean(xv * xv, axis=-1, keepdims=True) + EPS)
        h_ref[...] = (xv * r * g_ref[...]).astype(BF16)

    return pl.pallas_call(
        body, name=name, grid=(T // tm,),
        in_specs=[pl.BlockSpec((tm, C), lambda i: (i, 0)), pl.BlockSpec((1, C), lambda i: (0, 0))],
        out_specs=pl.BlockSpec((tm, C), lambda i: (i, 0)),
        out_shape=jax.ShapeDtypeStruct((T, C), BF16),
        compiler_params=_params(("parallel",)),
    )(x, gain.reshape(1, C))


def _rmsnorm_bwd(x, gain, dh, dres, name, tm=512):
    T, C = x.shape

    def body(x_ref, g_ref, dh_ref, dres_ref, dx_ref, dg_ref):
        @pl.when(pl.program_id(0) == 0)
        def _():
            dg_ref[...] = jnp.zeros_like(dg_ref)

        xv = x_ref[...]
        dy = dh_ref[...].astype(F32)
        r = lax.rsqrt(jnp.mean(xv * xv, axis=-1, keepdims=True) + EPS)
        gdy = dy * g_ref[...]
        inner = jnp.mean(xv * gdy, axis=-1, keepdims=True)
        dx_ref[...] = dres_ref[...] + r * gdy - xv * (r * r * r * inner)
        dg_ref[...] += jnp.sum(dy * xv * r, axis=0, keepdims=True)

    return pl.pallas_call(
        body, name=name, grid=(T // tm,),
        in_specs=[pl.BlockSpec((tm, C), lambda i: (i, 0)), pl.BlockSpec((1, C), lambda i: (0, 0)),
                  pl.BlockSpec((tm, C), lambda i: (i, 0)), pl.BlockSpec((tm, C), lambda i: (i, 0))],
        out_specs=[pl.BlockSpec((tm, C), lambda i: (i, 0)), pl.BlockSpec((1, C), lambda i: (0, 0))],
        out_shape=[jax.ShapeDtypeStruct((T, C), F32), jax.ShapeDtypeStruct((1, C), F32)],
        compiler_params=_params(("arbitrary",)),
    )(x, gain.reshape(1, C), dh, dres)


MASKED = -1e30


def _pair_sum(x, same_head):
    hi = x.astype(BF16)
    lo = (x - hi.astype(F32)).astype(BF16)
    return jnp.dot(hi, same_head, preferred_element_type=F32) + jnp.dot(lo, same_head, preferred_element_type=F32)


def _same_head():
    r = lax.broadcasted_iota(jnp.int32, (2 * HEAD_DIM, 2 * HEAD_DIM), 0)
    c = lax.broadcasted_iota(jnp.int32, (2 * HEAD_DIM, 2 * HEAD_DIM), 1)
    return jnp.where((r < HEAD_DIM) == (c < HEAD_DIM), 1.0, 0.0).astype(BF16)


def _pair_norm(t, g, same_head):
    tf = t.astype(F32)
    r = lax.rsqrt(_pair_sum(tf * tf, same_head) * (1.0 / HEAD_DIM) + EPS)
    return tf * r * g


def _pair_norm_bwd(t, g, dn, same_head):
    tf = t.astype(F32)
    r = lax.rsqrt(_pair_sum(tf * tf, same_head) * (1.0 / HEAD_DIM) + EPS)
    gd = dn * g
    inner = _pair_sum(tf * gd, same_head) * (1.0 / HEAD_DIM)
    return r * gd - tf * (r * r * r * inner), jnp.sum(dn * tf * r, axis=0, keepdims=True)


def _band_table(bias):
    table = jnp.full((N_HEADS, A_BLOCK, 2 * A_BLOCK), MASKED, F32)
    for c in range(N_LEFT):
        table = table.at[:, c * CHUNK:(c + 1) * CHUNK, c * CHUNK:c * CHUNK + BAND].set(bias)
    return table.reshape(N_HEADS // 2, 2 * A_BLOCK, 2 * A_BLOCK)


def _band_table_bwd(dtable):
    dtable = dtable.reshape(N_HEADS, A_BLOCK, 2 * A_BLOCK)
    return sum(dtable[:, c * CHUNK:(c + 1) * CHUNK, c * CHUNK:c * CHUNK + BAND] for c in range(N_LEFT))


def _a_specs(T):
    nb = T // A_BLOCK
    pairs = N_HEADS // 2
    col = lambda which: which * pairs
    cur = lambda which: pl.BlockSpec((A_BLOCK, 2 * HEAD_DIM), lambda p, i: (i, col(which) + p))
    prev = lambda which: pl.BlockSpec((A_BLOCK, 2 * HEAD_DIM), lambda p, i: (jnp.maximum(i - 1, 0), col(which) + p))
    nxt = lambda which: pl.BlockSpec((A_BLOCK, 2 * HEAD_DIM), lambda p, i: (jnp.minimum(i + 1, nb - 1), col(which) + p))
    table = pl.BlockSpec((1, 2 * A_BLOCK, 2 * A_BLOCK), lambda p, i: (p, 0, 0))
    gain = pl.BlockSpec((1, 2 * HEAD_DIM), lambda p, i: (0, 0))
    gacc = pl.BlockSpec((1, 1, 2 * HEAD_DIM), lambda p, i: (p, 0, 0))
    return nb, pairs, cur, prev, nxt, table, gain, gacc


def _a_probs(q_ref, kc_ref, kp_ref, t_ref, gq_ref, gk_ref, same_head, first):
    scale = 1.0 / math.sqrt(HEAD_DIM)
    qst = _stack_heads((_pair_norm(q_ref[...], gq_ref[...], same_head) * scale).astype(BF16))
    kcat = jnp.concatenate([_pair_norm(kp_ref[...], gk_ref[...], same_head).astype(BF16),
                            _pair_norm(kc_ref[...], gk_ref[...], same_head).astype(BF16)], axis=0)
    s = lax.dot_general(qst, kcat, (((1,), (1,)), ((), ())), preferred_element_type=F32) + t_ref[0]
    col = lax.broadcasted_iota(jnp.int32, s.shape, 1)
    s = jnp.where(col >= jnp.where(first, A_BLOCK, 0), s, MASKED)
    e = jnp.exp(s - jnp.max(s, axis=-1, keepdims=True))
    return qst, kcat, e, jnp.sum(e, axis=-1, keepdims=True)


def _attn_a_fwd(qkv, table, gq, gk):
    T = qkv.shape[0]
    nb, pairs, cur, prev, _, tspec, gspec, _ = _a_specs(T)

    def body(q_ref, kc_ref, kp_ref, vc_ref, vp_ref, t_ref, gq_ref, gk_ref, o_ref):
        same_head = _same_head()
        _, _, e, total = _a_probs(q_ref, kc_ref, kp_ref, t_ref, gq_ref, gk_ref, same_head, pl.program_id(1) == 0)
        vcat = jnp.concatenate([vp_ref[...], vc_ref[...]], axis=0)
        p = (e / total).astype(BF16)
        o_ref[...] = _unstack_heads(jnp.dot(p, vcat, preferred_element_type=F32)).astype(BF16)

    return pl.pallas_call(
        body, name="attn_a_fwd", grid=(pairs, nb),
        in_specs=[cur(0), cur(1), prev(1), cur(2), prev(2), tspec, gspec, gspec],
        out_specs=pl.BlockSpec((A_BLOCK, 2 * HEAD_DIM), lambda p, i: (i, p)),
        out_shape=jax.ShapeDtypeStruct((T, WIDTH), BF16),
        compiler_params=_params(("parallel", "arbitrary")),
    )(qkv, qkv, qkv, qkv, qkv, table, jnp.tile(gq.reshape(1, HEAD_DIM), (1, 2)), jnp.tile(gk.reshape(1, HEAD_DIM), (1, 2)))


def _attn_a_bwd(qkv, do, table, gq, gk):
    T = qkv.shape[0]
    nb, pairs, cur, prev, _, tspec, gspec, gacc = _a_specs(T)
    scale = 1.0 / math.sqrt(HEAD_DIM)
    oblk = pl.BlockSpec((A_BLOCK, 2 * HEAD_DIM), lambda p, i: (i, p))

    def body(q_ref, kc_ref, kp_ref, vc_ref, vp_ref, do_ref, t_ref, gq_ref, gk_ref,
             dq_ref, dkc_ref, dkp_ref, dvc_ref, dvp_ref, dt_ref, dgq_ref):
        first = pl.program_id(1) == 0

        @pl.when(first)
        def _():
            dt_ref[...] = jnp.zeros_like(dt_ref)
            dgq_ref[...] = jnp.zeros_like(dgq_ref)

        same_head = _same_head()
        qst, kcat, e, total = _a_probs(q_ref, kc_ref, kp_ref, t_ref, gq_ref, gk_ref, same_head, first)
        vcat = jnp.concatenate([vp_ref[...], vc_ref[...]], axis=0)
        dost = _stack_heads(do_ref[...])
        p = e / total
        dp = lax.dot_general(dost, vcat, (((1,), (1,)), ((), ())), preferred_element_type=F32)
        ds = p * (dp - jnp.sum(p * dp, axis=-1, keepdims=True))
        dt_ref[0] += ds
        dsb = ds.astype(BF16)
        dqn = _unstack_heads(jnp.dot(dsb, kcat, preferred_element_type=F32)) * scale
        dq, dg = _pair_norm_bwd(q_ref[...], gq_ref[...], dqn, same_head)
        dq_ref[...] = dq.astype(BF16)
        dgq_ref[0] += dg
        dk = lax.dot_general(dsb, qst, (((0,), (0,)), ((), ())), preferred_element_type=F32)
        dv = lax.dot_general(p.astype(BF16), dost, (((0,), (0,)), ((), ())), preferred_element_type=F32)
        dkp_ref[...] = dk[:A_BLOCK]
        dkc_ref[...] = dk[A_BLOCK:]
        dvp_ref[...] = dv[:A_BLOCK]
        dvc_ref[...] = dv[A_BLOCK:]

    wide = jax.ShapeDtypeStruct((T, WIDTH), F32)
    return pl.pallas_call(
        body, name="attn_a_bwd", grid=(pairs, nb),
        in_specs=[cur(0), cur(1), prev(1), cur(2), prev(2), oblk, tspec, gspec, gspec],
        out_specs=[oblk, oblk, oblk, oblk, oblk, tspec, gacc],
        out_shape=[jax.ShapeDtypeStruct((T, WIDTH), BF16), wide, wide, wide, wide,
                   jax.ShapeDtypeStruct((pairs, 2 * A_BLOCK, 2 * A_BLOCK), F32),
                   jax.ShapeDtypeStruct((pairs, 1, 2 * HEAD_DIM), F32)],
        compiler_params=_params(("parallel", "arbitrary")),
    )(qkv, qkv, qkv, qkv, qkv, do, table, jnp.tile(gq.reshape(1, HEAD_DIM), (1, 2)),
      jnp.tile(gk.reshape(1, HEAD_DIM), (1, 2)))


def _attn_a_bwd_keys(qkv, dkc, dkp, dvc, dvp, gk):
    T = qkv.shape[0]
    nb, pairs, cur, _, _, _, gspec, gacc = _a_specs(T)
    oblk = pl.BlockSpec((A_BLOCK, 2 * HEAD_DIM), lambda p, i: (i, p))
    onext = pl.BlockSpec((A_BLOCK, 2 * HEAD_DIM), lambda p, i: (jnp.minimum(i + 1, nb - 1), p))

    def body(k_ref, dkc_ref, dkp_ref, dvc_ref, dvp_ref, gk_ref, dk_ref, dv_ref, dgk_ref):
        i = pl.program_id(1)

        @pl.when(i == 0)
        def _():
            dgk_ref[...] = jnp.zeros_like(dgk_ref)

        has_next = (i < nb - 1).astype(F32)
        dkn = dkc_ref[...] + has_next * dkp_ref[...]
        dk, dg = _pair_norm_bwd(k_ref[...], gk_ref[...], dkn, _same_head())
        dk_ref[...] = dk.astype(BF16)
        dv_ref[...] = (dvc_ref[...] + has_next * dvp_ref[...]).astype(BF16)
        dgk_ref[0] += dg

    blk = jax.ShapeDtypeStruct((T, WIDTH), BF16)
    return pl.pallas_call(
        body, name="attn_a_bwd_keys", grid=(pairs, nb),
        in_specs=[cur(1), oblk, onext, oblk, onext, gspec],
        out_specs=[oblk, oblk, gacc],
        out_shape=[blk, blk, jax.ShapeDtypeStruct((pairs, 1, 2 * HEAD_DIM), F32)],
        compiler_params=_params(("parallel", "arbitrary")),
    )(qkv, dkc, dkp, dvc, dvp, jnp.tile(gk.reshape(1, HEAD_DIM), (1, 2)))


def _scan_matrix(later):
    r = lax.broadcasted_iota(jnp.int32, (SB_SCAN, SB_SCAN), 0)
    c = lax.broadcasted_iota(jnp.int32, (SB_SCAN, SB_SCAN), 1)
    return jnp.where((r > c) if later else (r < c), 1.0, 0.0).astype(BF16)


def _running_sums(x, carry, scan, later):
    n = SB_KEYS // SB_SCAN
    parts = [None] * n
    total = carry
    for sb in (reversed(range(n)) if later else range(n)):
        xs = x[:, sb * SB_SCAN:(sb + 1) * SB_SCAN]
        local = jnp.dot(xs.astype(BF16), scan, preferred_element_type=F32)
        parts[sb] = local if total is None else local + total
        rowsum = jnp.sum(xs, axis=-1, keepdims=True)
        total = rowsum if total is None else total + rowsum
    return (parts[0] if n == 1 else jnp.concatenate(parts, axis=1)), total


def _sb_log_sigmoids(z):
    neg_abs = pltpu.bitcast(pltpu.bitcast(z, jnp.uint32) | jnp.uint32(0x80000000), F32)
    take = jnp.minimum(z, 0.0) - jnp.log(1.0 + jnp.exp(neg_abs))
    return take, take - z


def _sb_mask():
    r = lax.broadcasted_iota(jnp.int32, (2 * SB_ROWS, SB_KEYS), 0)
    c = lax.broadcasted_iota(jnp.int32, (2 * SB_ROWS, SB_KEYS), 1)
    return c < jnp.where(r >= SB_ROWS, r - SB_ROWS, r)


def _stack_heads(t):
    lane = lax.broadcasted_iota(jnp.int32, t.shape, 1)
    zero = jnp.zeros_like(t)
    return jnp.concatenate([jnp.where(lane < HEAD_DIM, t, zero), jnp.where(lane >= HEAD_DIM, t, zero)], axis=0)


def _unstack_heads(t):
    rows = t.shape[0] // 2
    lane = lax.broadcasted_iota(jnp.int32, (rows, 2 * HEAD_DIM), 1)
    return jnp.where(lane < HEAD_DIM, t[:rows], t[rows:])


def _sb_specs(T):
    nq = T // SB_ROWS
    blk = lambda col: pl.BlockSpec((SB_ROWS, 2 * HEAD_DIM), lambda p, i: (i, col + p))
    full = lambda col: pl.BlockSpec((T, 2 * HEAD_DIM), lambda p, i: (0, col + p))
    return nq, blk, full


def _key_rows(j):
    return pl.ds(pl.multiple_of(j * SB_KEYS, SB_KEYS), SB_KEYS)


def _attn_b_fwd(qkv):
    T = qkv.shape[0]
    pairs = N_HEADS // 2
    nq, blk, full = _sb_specs(T)
    scale = 1.0 / math.sqrt(HEAD_DIM)

    assert nq <= LANES

    def body(q_ref, k_ref, v_ref, o_ref, c_ref, acc_ref, carry_ref):
        i = pl.program_id(1)
        scan = _scan_matrix(True)
        qst = _stack_heads((q_ref[...].astype(F32) * scale).astype(BF16))
        lane = lax.broadcasted_iota(jnp.int32, (2 * SB_ROWS, LANES), 1)

        def span(j, carry, mask):
            z = lax.dot_general(qst, k_ref[_key_rows(j), :], (((1,), (1,)), ((), ())), preferred_element_type=F32)
            take, keep = _sb_log_sigmoids(z)
            if mask is not None:
                keep = jnp.where(mask, keep, 0.0)
            tail, total = _running_sums(keep, carry, scan, True)
            w = jnp.exp(take + tail)
            if mask is not None:
                w = jnp.where(mask, w, 0.0)
            return jnp.dot(w.astype(BF16), v_ref[_key_rows(j), :], preferred_element_type=F32), total

        acc_ref[...], carry_ref[...] = span(i, None, _sb_mask())
        c_ref[0, 0] = jnp.zeros((2 * SB_ROWS, LANES), F32)

        @pl.loop(0, i)
        def _(jj):
            j = i - 1 - jj
            carry = carry_ref[...]
            c_ref[0, 0] = jnp.where(lane == j, carry, c_ref[0, 0])
            out, carry_ref[...] = span(j, carry, None)
            acc_ref[...] += out

        o_ref[...] = _unstack_heads(acc_ref[...]).astype(BF16)

    return pl.pallas_call(
        body, name="attn_b_fwd", grid=(pairs, nq),
        in_specs=[blk(3 * pairs), full(4 * pairs), full(5 * pairs)],
        out_specs=[pl.BlockSpec((SB_ROWS, 2 * HEAD_DIM), lambda p, i: (i, p)),
                   pl.BlockSpec((1, 1, 2 * SB_ROWS, LANES), lambda p, i: (p, i, 0, 0))],
        out_shape=[jax.ShapeDtypeStruct((T, WIDTH), BF16), jax.ShapeDtypeStruct((pairs, nq, 2 * SB_ROWS, LANES), F32)],
        scratch_shapes=[pltpu.VMEM((2 * SB_ROWS, 2 * HEAD_DIM), F32), pltpu.VMEM((2 * SB_ROWS, 1), F32)],
        compiler_params=_params(("parallel", "arbitrary")),
    )(qkv, qkv, qkv)


def _attn_b_bwd(qkv, carries, do):
    T = qkv.shape[0]
    pairs = N_HEADS // 2
    nq, blk, full = _sb_specs(T)
    scale = 1.0 / math.sqrt(HEAD_DIM)
    oblk = pl.BlockSpec((SB_ROWS, 2 * HEAD_DIM), lambda p, i: (i, p))
    ofull = pl.BlockSpec((T, 2 * HEAD_DIM), lambda p, i: (0, p))

    def body(q_ref, k_ref, v_ref, c_ref, do_ref, dq_ref, dk_ref, dv_ref, dqacc_ref, before_ref):
        i = pl.program_id(1)

        @pl.when(i == 0)
        def _():
            dk_ref[...] = jnp.zeros_like(dk_ref)
            dv_ref[...] = jnp.zeros_like(dv_ref)

        scan_later = _scan_matrix(True)
        scan_earlier = _scan_matrix(False)
        qst = _stack_heads((q_ref[...].astype(F32) * scale).astype(BF16))
        dost = _stack_heads(do_ref[...].astype(BF16))
        lane = lax.broadcasted_iota(jnp.int32, (2 * SB_ROWS, LANES), 1)

        def span(j, later, mask):
            kv = k_ref[_key_rows(j), :]
            vv = v_ref[_key_rows(j), :]
            z = lax.dot_general(qst, kv, (((1,), (1,)), ((), ())), preferred_element_type=F32)
            take, keep = _sb_log_sigmoids(z)
            sig = jnp.exp(take)
            if mask is not None:
                keep = jnp.where(mask, keep, 0.0)
            tail, _ = _running_sums(keep, later, scan_later, True)
            w = jnp.exp(take + tail)
            if mask is not None:
                w = jnp.where(mask, w, 0.0)
            g = w * lax.dot_general(dost, vv, (((1,), (1,)), ((), ())), preferred_element_type=F32)
            before, before_ref[...] = _running_sums(g, before_ref[...], scan_earlier, False)
            dz = g - sig * (g + before)
            if mask is not None:
                dz = jnp.where(mask, dz, 0.0)
            dzb = dz.astype(BF16)
            dqacc_ref[...] += jnp.dot(dzb, kv, preferred_element_type=F32)
            dk_ref[_key_rows(j), :] += lax.dot_general(dzb, qst, (((0,), (0,)), ((), ())), preferred_element_type=F32)
            dv_ref[_key_rows(j), :] += lax.dot_general(w.astype(BF16), dost, (((0,), (0,)), ((), ())),
                                                       preferred_element_type=F32)

        dqacc_ref[...] = jnp.zeros_like(dqacc_ref)
        before_ref[...] = jnp.zeros_like(before_ref)

        @pl.loop(0, i)
        def _(j):
            later = jnp.sum(jnp.where(lane == j, c_ref[0, 0], 0.0), axis=-1, keepdims=True)
            span(j, later, None)

        span(i, None, _sb_mask())
        dq_ref[...] = (_unstack_heads(dqacc_ref[...]) * scale).astype(BF16)

    wide = jax.ShapeDtypeStruct((T, WIDTH), F32)
    return pl.pallas_call(
        body, name="attn_b_bwd", grid=(pairs, nq),
        in_specs=[blk(3 * pairs), full(4 * pairs), full(5 * pairs),
                  pl.BlockSpec((1, 1, 2 * SB_ROWS, LANES), lambda p, i: (p, i, 0, 0)), oblk],
        out_specs=[oblk, ofull, ofull],
        out_shape=[jax.ShapeDtypeStruct((T, WIDTH), BF16), wide, wide],
        scratch_shapes=[pltpu.VMEM((2 * SB_ROWS, 2 * HEAD_DIM), F32), pltpu.VMEM((2 * SB_ROWS, 1), F32)],
        compiler_params=_params(("parallel", "arbitrary")),
    )(qkv, qkv, qkv, carries, do)


def _window_sums(ext, forward):
    n = ext.shape[0]
    out = []
    s = ext
    for step in (1, 2, 4, 8):
        s = s + pltpu.roll(s, (n - step) if forward else step, 0)
        out.append(s)
    return out


def _pool_counts(base, rows, win):
    t = base + lax.broadcasted_iota(jnp.int32, (rows, 1), 0)
    return jnp.minimum(t + 1, win).astype(F32)


def _pooled(u_ref, up_ref, i, tm):
    prev = jnp.where(i > 0, up_ref[...], 0.0)
    ext = jnp.concatenate([prev, u_ref[...]], axis=0)
    sums = _window_sums(ext, False)
    parts = []
    for g, win in enumerate(POOL_WINDOWS):
        cols = slice(g * POOL_DIM, (g + 1) * POOL_DIM)
        cnt = _pool_counts(i * tm, tm, win)
        parts.append(sums[g][HALO:, cols] / cnt - ext[HALO:, cols])
    return parts


def _pool_fwd(ucg, w_pool, scale, tm=512):
    T = ucg.shape[0]
    C = WIDTH

    def body(u_ref, up_ref, w_ref, s_ref, o_ref):
        i = pl.program_id(0)
        parts = _pooled(u_ref, up_ref, i, tm)
        for g in range(len(POOL_WINDOWS)):
            mixed = jnp.dot(parts[g].astype(BF16), w_ref[g], preferred_element_type=F32)
            o_ref[:, g * POOL_DIM:(g + 1) * POOL_DIM] = (mixed * s_ref[:, g * POOL_DIM:(g + 1) * POOL_DIM]).astype(BF16)

    return pl.pallas_call(
        body, name="pool_fwd", grid=(T // tm,),
        in_specs=[pl.BlockSpec((tm, C), lambda i: (i, 0)),
                  pl.BlockSpec((HALO, C), lambda i: (jnp.maximum(i * (tm // HALO) - 1, 0), 0)),
                  pl.BlockSpec((len(POOL_WINDOWS), POOL_DIM, POOL_DIM), lambda i: (0, 0, 0)),
                  pl.BlockSpec((1, C), lambda i: (0, 0))],
        out_specs=pl.BlockSpec((tm, C), lambda i: (i, 0)),
        out_shape=jax.ShapeDtypeStruct((T, C), BF16),
        compiler_params=_params(("parallel",)),
    )(ucg, ucg, w_pool.astype(BF16), scale.reshape(1, C))


def _pool_bwd(ucg, do_c, w_pool, scale, tm=512):
    T = ucg.shape[0]
    C = WIDTH
    nt = T // tm
    G = len(POOL_WINDOWS)

    def body(u_ref, up_ref, do_ref, don_ref, w_ref, s_ref, du_ref, dw_ref, ds_ref):
        i = pl.program_id(0)

        @pl.when(i == 0)
        def _():
            dw_ref[...] = jnp.zeros_like(dw_ref)
            ds_ref[...] = jnp.zeros_like(ds_ref)

        parts = _pooled(u_ref, up_ref, i, tm)
        nxt = jnp.where(i < nt - 1, don_ref[...].astype(F32), 0.0)
        do_ext = jnp.concatenate([do_ref[...].astype(F32), nxt], axis=0) * s_ref[...]
        for g, win in enumerate(POOL_WINDOWS):
            cols = slice(g * POOL_DIM, (g + 1) * POOL_DIM)
            pooled_b = parts[g].astype(BF16)
            dmix = do_ext[:, cols].astype(BF16)
            mixed = jnp.dot(pooled_b, w_ref[g], preferred_element_type=F32)
            ds_ref[:, cols] += jnp.sum(do_ref[:, cols].astype(F32) * mixed, axis=0, keepdims=True)
            dw_ref[g] += lax.dot_general(pooled_b, dmix[:tm], (((0,), (0,)), ((), ())), preferred_element_type=F32)
            dpool = lax.dot_general(dmix, w_ref[g], (((1,), (1,)), ((), ())), preferred_element_type=F32)
            scaled = dpool / _pool_counts(i * tm, tm + HALO, win)
            fwd = _window_sums(scaled, True)[g]
            du_ref[:, cols] = (fwd[:tm] - dpool[:tm]).astype(BF16)

    return pl.pallas_call(
        body, name="pool_bwd", grid=(nt,),
        in_specs=[pl.BlockSpec((tm, C), lambda i: (i, 0)),
                  pl.BlockSpec((HALO, C), lambda i: (jnp.maximum(i * (tm // HALO) - 1, 0), 0)),
                  pl.BlockSpec((tm, C), lambda i: (i, 0)),
                  pl.BlockSpec((HALO, C), lambda i: (jnp.minimum((i + 1) * (tm // HALO), T // HALO - 1), 0)),
                  pl.BlockSpec((G, POOL_DIM, POOL_DIM), lambda i: (0, 0, 0)),
                  pl.BlockSpec((1, C), lambda i: (0, 0))],
        out_specs=[pl.BlockSpec((tm, C), lambda i: (i, 0)),
                   pl.BlockSpec((G, POOL_DIM, POOL_DIM), lambda i: (0, 0, 0)),
                   pl.BlockSpec((1, C), lambda i: (0, 0))],
        out_shape=[jax.ShapeDtypeStruct((T, C), BF16), jax.ShapeDtypeStruct((G, POOL_DIM, POOL_DIM), F32),
                   jax.ShapeDtypeStruct((1, C), F32)],
        compiler_params=_params(("arbitrary",)),
    )(ucg, ucg, do_c, do_c, w_pool.astype(BF16), scale.reshape(1, C))


def _merge_fwd(oa, ob, oc, glog, b_gate, wa, wb, wc, tm=256):
    T = oa.shape[0]
    Dm = D_MODEL
    row = lambda c: pl.BlockSpec((tm, c), lambda i: (i, 0))
    wspec = pl.BlockSpec((WIDTH, Dm), lambda i: (0, 0))

    def body(oa_ref, ob_ref, oc_ref, g_ref, b_ref, wa_ref, wb_ref, wc_ref, m_ref, ya_ref, yb_ref, yc_ref):
        merged = jnp.zeros((tm, Dm), F32)
        for kk, (o_ref, w_ref, y_ref) in enumerate(((oa_ref, wa_ref, ya_ref), (ob_ref, wb_ref, yb_ref),
                                                    (oc_ref, wc_ref, yc_ref))):
            y = jnp.dot(o_ref[...].astype(BF16), w_ref[...], preferred_element_type=F32)
            gate = jax.nn.sigmoid(g_ref[:, kk * Dm:(kk + 1) * Dm] + b_ref[:, kk * Dm:(kk + 1) * Dm])
            merged = merged + gate * y
            y_ref[...] = y.astype(BF16)
        m_ref[...] = merged.astype(BF16)

    out = jax.ShapeDtypeStruct((T, Dm), BF16)
    return pl.pallas_call(
        body, name="merge_fwd", grid=(T // tm,),
        in_specs=[row(WIDTH), row(WIDTH), row(WIDTH), row(3 * Dm), pl.BlockSpec((1, 3 * Dm), lambda i: (0, 0)),
                  wspec, wspec, wspec],
        out_specs=[row(Dm)] * 4,
        out_shape=[out] * 4,
        compiler_params=_params(("parallel",)),
    )(oa, ob, oc, glog, b_gate.reshape(1, 3 * Dm), wa, wb, wc)


def _merge_bwd(dmerged, glog, b_gate, ya, yb, yc, tm=256):
    T = dmerged.shape[0]
    Dm = D_MODEL
    row = lambda c: pl.BlockSpec((tm, c), lambda i: (i, 0))

    def body(dm_ref, g_ref, b_ref, ya_ref, yb_ref, yc_ref, dya_ref, dyb_ref, dyc_ref, dg_ref, db_ref):
        @pl.when(pl.program_id(0) == 0)
        def _():
            db_ref[...] = jnp.zeros_like(db_ref)

        dm = dm_ref[...]
        for kk, (y_ref, dy_ref) in enumerate(((ya_ref, dya_ref), (yb_ref, dyb_ref), (yc_ref, dyc_ref))):
            cols = slice(kk * Dm, (kk + 1) * Dm)
            gate = jax.nn.sigmoid(g_ref[:, cols] + b_ref[:, cols])
            dy_ref[...] = (dm * gate).astype(BF16)
            dlog = dm * y_ref[...].astype(F32) * gate * (1.0 - gate)
            dg_ref[:, cols] = dlog.astype(BF16)
            db_ref[:, cols] += jnp.sum(dlog, axis=0, keepdims=True)

    out = jax.ShapeDtypeStruct((T, Dm), BF16)
    return pl.pallas_call(
        body, name="merge_bwd", grid=(T // tm,),
        in_specs=[row(Dm), row(3 * Dm), pl.BlockSpec((1, 3 * Dm), lambda i: (0, 0)), row(Dm), row(Dm), row(Dm)],
        out_specs=[row(Dm), row(Dm), row(Dm), row(3 * Dm), pl.BlockSpec((1, 3 * Dm), lambda i: (0, 0))],
        out_shape=[out, out, out, jax.ShapeDtypeStruct((T, 3 * Dm), BF16), jax.ShapeDtypeStruct((1, 3 * Dm), F32)],
        compiler_params=_params(("arbitrary",)),
    )(dmerged, glog, b_gate.reshape(1, 3 * Dm), ya, yb, yc)


def _residual_add(x, y, name, tm=512):
    T, C = x.shape

    def body(x_ref, y_ref, o_ref):
        o_ref[...] = x_ref[...] + y_ref[...]

    spec = pl.BlockSpec((tm, C), lambda i: (i, 0))
    return pl.pallas_call(body, name=name, grid=(T // tm,), in_specs=[spec, spec], out_specs=spec,
                          out_shape=jax.ShapeDtypeStruct((T, C), F32), compiler_params=_params(("parallel",)))(x, y)


FF_TILE = 256
FF_TILES = D_FF // FF_TILE
CONV_HALO = 8


def _ff_pair_order(w):
    lead = w.shape[:-1]
    n = len(lead)
    w = w.reshape(*lead, 2, FF_TILES, FF_TILE)
    return jnp.swapaxes(w, n, n + 1).reshape(*lead, 2 * D_FF)


def _ff_natural_order(w):
    lead = w.shape[:-1]
    n = len(lead)
    w = w.reshape(*lead, FF_TILES, 2, FF_TILE)
    return jnp.swapaxes(w, n, n + 1).reshape(*lead, 2 * D_FF)


def _conv(ext, w_ref, b_ref):
    c = b_ref[...] + w_ref[2:3, :] * ext
    c = c + w_ref[1:2, :] * pltpu.roll(ext, 1, 0)
    c = c + w_ref[0:1, :] * pltpu.roll(ext, 2, 0)
    return c[CONV_HALO:]


def _ff_specs(T, tm):
    pair = pl.BlockSpec((tm, 2 * FF_TILE), lambda i, j: (i, j))
    prev = pl.BlockSpec((CONV_HALO, 2 * FF_TILE), lambda i, j: (jnp.maximum(i * (tm // CONV_HALO) - 1, 0), j))
    nxt = pl.BlockSpec((CONV_HALO, 2 * FF_TILE),
                       lambda i, j: (jnp.minimum((i + 1) * (tm // CONV_HALO), T // CONV_HALO - 1), j))
    half = pl.BlockSpec((tm, FF_TILE), lambda i, j: (i, j))
    small = lambda r: pl.BlockSpec((r, 2 * FF_TILE), lambda i, j: (0, j))
    return pair, prev, nxt, half, small


def _swap_grid(spec):
    return pl.BlockSpec(spec.block_shape, lambda j, i, f=spec.index_map: f(i, j))


def _ff_act_fwd(u, conv_w, conv_b, tm=512):
    T = u.shape[0]
    pair, prev, _, half, small = _ff_specs(T, tm)

    def body(u_ref, p_ref, w_ref, b_ref, a_ref):
        i = pl.program_id(0)
        c = _conv(jnp.concatenate([jnp.where(i > 0, p_ref[...], 0.0), u_ref[...]], axis=0), w_ref, b_ref)
        cg, cv = c[:, :FF_TILE], c[:, FF_TILE:]
        a_ref[...] = (cg * jax.nn.sigmoid(cg) * cv).astype(BF16)

    return pl.pallas_call(
        body, name="ff_act_fwd", grid=(T // tm, FF_TILES),
        in_specs=[pair, prev, small(3), small(1)],
        out_specs=half,
        out_shape=jax.ShapeDtypeStruct((T, D_FF), BF16),
        compiler_params=_params(("parallel", "parallel")),
    )(u, u, conv_w, conv_b.reshape(1, -1))


def _ff_act_bwd(u, da, conv_w, conv_b, tm=512):
    T = u.shape[0]
    pair, prev, _, half, small = _ff_specs(T, tm)

    def body(u_ref, p_ref, da_ref, w_ref, b_ref, dc_ref, dw_ref, db_ref):
        i = pl.program_id(1)

        @pl.when(i == 0)
        def _():
            dw_ref[...] = jnp.zeros_like(dw_ref)
            db_ref[...] = jnp.zeros_like(db_ref)

        ext = jnp.concatenate([jnp.where(i > 0, p_ref[...], 0.0), u_ref[...]], axis=0)
        c = _conv(ext, w_ref, b_ref)
        cg, cv = c[:, :FF_TILE], c[:, FF_TILE:]
        da = da_ref[...]
        sg = jax.nn.sigmoid(cg)
        dc = jnp.concatenate([da * cv * sg * (1.0 + cg * (1.0 - sg)), da * cg * sg], axis=1)
        dc_ref[...] = dc
        db_ref[...] += jnp.sum(dc, axis=0, keepdims=True)
        dw_ref[2:3, :] += jnp.sum(dc * ext[CONV_HALO:], axis=0, keepdims=True)
        dw_ref[1:2, :] += jnp.sum(dc * pltpu.roll(ext, 1, 0)[CONV_HALO:], axis=0, keepdims=True)
        dw_ref[0:1, :] += jnp.sum(dc * pltpu.roll(ext, 2, 0)[CONV_HALO:], axis=0, keepdims=True)

    return pl.pallas_call(
        body, name="ff_act_bwd", grid=(FF_TILES, T // tm),
        in_specs=[_swap_grid(pair), _swap_grid(prev), _swap_grid(half), _swap_grid(small(3)), _swap_grid(small(1))],
        out_specs=[_swap_grid(pair), _swap_grid(small(3)), _swap_grid(small(1))],
        out_shape=[jax.ShapeDtypeStruct((T, 2 * D_FF), F32), jax.ShapeDtypeStruct((3, 2 * D_FF), F32),
                   jax.ShapeDtypeStruct((1, 2 * D_FF), F32)],
        compiler_params=_params(("parallel", "arbitrary")),
    )(u, u, da, conv_w, conv_b.reshape(1, -1))


def _ff_conv_bwd(dc, conv_w, tm=512):
    T = dc.shape[0]
    nt = T // tm
    pair, _, nxt, _, small = _ff_specs(T, tm)

    def body(dc_ref, n_ref, w_ref, du_ref):
        i = pl.program_id(0)
        ext = jnp.concatenate([dc_ref[...], jnp.where(i < nt - 1, n_ref[...], 0.0)], axis=0)
        n = tm + CONV_HALO
        du = w_ref[2:3, :] * ext + w_ref[1:2, :] * pltpu.roll(ext, n - 1, 0) + w_ref[0:1, :] * pltpu.roll(ext, n - 2, 0)
        du_ref[...] = du[:tm].astype(BF16)

    return pl.pallas_call(
        body, name="ff_conv_bwd", grid=(nt, FF_TILES),
        in_specs=[pair, nxt, small(3)],
        out_specs=pair,
        out_shape=jax.ShapeDtypeStruct((T, 2 * D_FF), BF16),
        compiler_params=_params(("parallel", "parallel")),
    )(dc, dc, conv_w)


def _loss_head(y, target, tm=512):
    T, C = y.shape
    nt = T // tm

    def body(y_ref, t_ref, dy_ref, l_ref):
        err = y_ref[...] - t_ref[...]
        dy_ref[...] = err * (1.0 / C)
        part = jnp.sum(err * err, axis=0, keepdims=True) * (0.5 / C)
        l_ref[0] = jnp.broadcast_to(part, (8, C))

    spec = pl.BlockSpec((tm, C), lambda i: (i, 0))
    dy, parts = pl.pallas_call(
        body, name="loss_head", grid=(nt,),
        in_specs=[spec, spec],
        out_specs=[spec, pl.BlockSpec((1, 8, C), lambda i: (i, 0, 0))],
        out_shape=[jax.ShapeDtypeStruct((T, C), F32), jax.ShapeDtypeStruct((nt, 8, C), F32)],
        compiler_params=_params(("parallel",)),
    )(y, target)
    return dy, jnp.sum(parts[:, 0, :])


def _adamw_math(w, g, m, v):
    m = ADAM_B1 * m + (1.0 - ADAM_B1) * g
    v = ADAM_B2 * v + (1.0 - ADAM_B2) * (g * g)
    m_hat = m / (1.0 - ADAM_B1 ** ADAM_STEP)
    v_hat = v / (1.0 - ADAM_B2 ** ADAM_STEP)
    delta = -ADAM_LR * (m_hat / (jnp.sqrt(v_hat) + ADAM_EPS) + ADAM_WD * w)
    return delta, m, v


def _adamw(parts, w, m, v, name, tm=256):
    R, C = w.shape
    tm = _pick_rows(R, tm)

    def body(p_ref, w_ref, m_ref, v_ref, g_ref, d_ref, nm_ref, nv_ref):
        g = p_ref[0].astype(F32)
        for s in range(1, N_DEV):
            g = g + p_ref[s].astype(F32)
        delta, nm, nv = _adamw_math(w_ref[...], g, m_ref[...], v_ref[...])
        g_ref[...] = g
        d_ref[...] = delta
        nm_ref[...] = nm
        nv_ref[...] = nv

    spec = pl.BlockSpec((tm, C), lambda i: (i, 0))
    out = jax.ShapeDtypeStruct((R, C), F32)
    return pl.pallas_call(
        body, name=name, grid=(R // tm,),
        in_specs=[pl.BlockSpec((N_DEV, tm, C), lambda i: (0, i, 0)), spec, spec, spec],
        out_specs=[spec] * 4,
        out_shape=[out] * 4,
        compiler_params=_params(("parallel",)),
    )(parts, w, m, v)


def _pick_rows(n, cap):
    best = None
    for t in range(16, min(n, cap) + 1, 16):
        if n % t == 0:
            best = t
    assert best is not None, (n, cap)
    return best


def _exchange(src, scatter, name):
    slab = src.shape[1:] if scatter else src.shape

    def body(src_ref, out_ref, send_sems, recv_sems, local_sem):
        x, y, c = lax.axis_index("x"), lax.axis_index("y"), lax.axis_index("c")
        me = 4 * x + 2 * y + c

        def piece(d):
            return src_ref.at[d] if scatter else src_ref

        mine = pltpu.make_async_copy(piece(me), out_ref.at[me], local_sem)
        mine.start()
        copies = []
        for k in range(1, N_DEV):
            px = 1 - x if k & 4 else x
            py = 1 - y if k & 2 else y
            pc = 1 - c if k & 1 else c
            peer = 4 * px + 2 * py + pc
            cp = pltpu.make_async_remote_copy(
                src_ref=piece(peer), dst_ref=out_ref.at[me],
                send_sem=send_sems.at[k], recv_sem=recv_sems.at[k],
                device_id=(px, py, pc), device_id_type=MESH)
            cp.start()
            copies.append((cp, peer))
        for k, (cp, peer) in enumerate(copies, start=1):
            cp.wait_send()
            pltpu.make_async_remote_copy(
                src_ref=piece(peer), dst_ref=out_ref.at[peer],
                send_sem=send_sems.at[k], recv_sem=recv_sems.at[k],
                device_id=(x, y, c), device_id_type=MESH).wait_recv()
        mine.wait()

    return pl.pallas_call(
        body, name=name,
        in_specs=[pl.BlockSpec(memory_space=pl.ANY)],
        out_specs=pl.BlockSpec(memory_space=pl.ANY),
        out_shape=jax.ShapeDtypeStruct((N_DEV,) + tuple(slab), src.dtype),
        scratch_shapes=[pltpu.SemaphoreType.DMA((N_DEV,)), pltpu.SemaphoreType.DMA((N_DEV,)),
                        pltpu.SemaphoreType.DMA],
    )(src)


SHARDED = ("w_in", "w_branch_a", "w_branch_b", "w_branch_c", "w_out", "w_up", "w_down")
REPLICATED = ("norm_mix", "b_gate", "q_norm_a", "k_norm_a", "rel_bias_a", "w_pool", "pool_scale", "norm_ffn", "conv_b")
WEIGHTS = ("norm_mix", "w_in", "b_gate", "q_norm_a", "k_norm_a", "rel_bias_a", "w_pool", "pool_scale",
           "w_branch_a", "w_branch_b", "w_branch_c", "w_out", "norm_ffn", "w_up", "conv_w", "conv_b", "w_down")
PACK_COLS = 1024
SMALL_COLS = 128
QKV_COLS = 6 * WIDTH


def _rel_index():
    q_off = jnp.arange(CHUNK)[:, None] + N_LEFT * CHUNK
    k_off = jnp.arange(BAND)[None, :]
    return jnp.clip(q_off - k_off, -(CHUNK - 1), MAX_REL) + (CHUNK - 1)


def _rel_onehot():
    rel = _rel_index().reshape(1, CHUNK * BAND)
    return (rel == jnp.arange(REL_TABLE)[:, None]).astype(BF16)


def _select_mm(x, onehot, mode, name):
    hi = x.astype(BF16)
    r1 = x - hi.astype(F32)
    mid = r1.astype(BF16)
    lo = (r1 - mid.astype(F32)).astype(BF16)
    y = _mm(jnp.concatenate([hi, mid, lo, jnp.zeros_like(hi)], axis=0), onehot, mode, F32, name)
    n = x.shape[0]
    return y[:n] + y[n:2 * n] + y[2 * n:3 * n]


def _pack_rows(arrays, cols, row_multiple):
    flat = jnp.concatenate([a.reshape(-1) for a in arrays])
    rows = -(-flat.shape[0] // cols)
    rows = -(-rows // row_multiple) * row_multiple
    return jnp.pad(flat, (0, rows * cols - flat.shape[0])).reshape(rows, cols)


def _unpack_rows(packed, like):
    flat = packed.reshape(-1)
    out, off = [], 0
    for a in like:
        out.append(flat[off:off + a.size].reshape(a.shape))
        off += a.size
    return out


def _gather_columns(g, shape):
    L, rows, cols = shape
    return g.reshape(N_DEV, L, rows, cols).transpose(1, 2, 0, 3).reshape(L, rows, N_DEV * cols)


def _gather_rows(g, shape):
    L, rows, cols = shape
    return g.reshape(N_DEV, L, rows, cols).transpose(1, 0, 2, 3).reshape(L, N_DEV * rows, cols)


def _split_columns(full):
    L, rows, allc = full.shape
    cols = allc // N_DEV
    return full.reshape(L, rows, N_DEV, cols).transpose(2, 0, 1, 3).reshape(N_DEV, -1, PACK_COLS)


def _split_rows(full):
    L, allr, cols = full.shape
    rows = allr // N_DEV
    return full.reshape(L, N_DEV, rows, cols).transpose(1, 0, 2, 3).reshape(N_DEV, -1, PACK_COLS)


def kernel(x, norm_mix, w_in, b_gate, q_norm_a, k_norm_a, rel_bias_a, w_pool, pool_scale, w_branch_a, w_branch_b, w_branch_c, w_out, norm_ffn, w_up, conv_w, conv_b, w_down, loss_target, m_norm_mix, m_w_in, m_b_gate, m_q_norm_a, m_k_norm_a, m_rel_bias_a, m_w_pool, m_pool_scale, m_w_branch_a, m_w_branch_b, m_w_branch_c, m_w_out, m_norm_ffn, m_w_up, m_conv_w, m_conv_b, m_w_down, v_norm_mix, v_w_in, v_b_gate, v_q_norm_a, v_k_norm_a, v_rel_bias_a, v_w_pool, v_pool_scale, v_w_branch_a, v_w_branch_b, v_w_branch_c, v_w_out, v_norm_ffn, v_w_up, v_conv_w, v_conv_b, v_w_down):
    args = dict(locals())
    w = {n: args[n] for n in WEIGHTS}
    m = {n: args["m_" + n] for n in WEIGHTS}
    v = {n: args["v_" + n] for n in WEIGHTS}
    L = w_in.shape[0]
    T = x.shape[1]
    xs = x.reshape(T, D_MODEL)
    target = loss_target.reshape(T, D_MODEL)

    conv_bits = lax.bitcast_convert_type(conv_w, BF16)
    packed = _pack_rows([w[n].astype(BF16) for n in SHARDED] + [conv_bits], PACK_COLS, 16)
    gathered = _exchange(packed, False, "gather_weights")
    rows = [w[n].size // PACK_COLS for n in SHARDED]
    offs = [sum(rows[:i]) for i in range(len(rows) + 1)]
    part = {n: gathered[:, offs[i]:offs[i + 1]] for i, n in enumerate(SHARDED)}
    w_in_f = _gather_columns(part["w_in"], w_in.shape)
    w_qkv, w_uc, w_g = w_in_f[:, :, :QKV_COLS], w_in_f[:, :, QKV_COLS:QKV_COLS + WIDTH], w_in_f[:, :, QKV_COLS + WIDTH:]
    w_a = _gather_columns(part["w_branch_a"], w_branch_a.shape)
    w_b = _gather_columns(part["w_branch_b"], w_branch_b.shape)
    w_c = _gather_columns(part["w_branch_c"], w_branch_c.shape)
    w_out_f = _gather_rows(part["w_out"], w_out.shape)
    w_up_f = _ff_pair_order(_gather_columns(part["w_up"], w_up.shape))
    w_down_f = _gather_rows(part["w_down"], w_down.shape)
    conv_flat = gathered[:, offs[-1]:].reshape(N_DEV, -1)[:, :conv_bits.size].reshape((N_DEV,) + conv_bits.shape)
    conv_w_f = _ff_pair_order(_gather_columns(lax.bitcast_convert_type(conv_flat, F32).reshape(N_DEV, -1), conv_w.shape))
    conv_b_f = _ff_pair_order(conv_b)
    onehot = _rel_onehot()

    saved = []
    cur = xs
    for l in range(L):
        h = _rmsnorm_fwd(cur, norm_mix[l], "norm_mix_fwd")
        qkv = _mm(h, w_qkv[l], "nn", BF16, "proj_qkv")
        uc = _mm(h, w_uc[l], "nn", F32, "proj_pool")
        glog = _mm(h, w_g[l], "nn", F32, "proj_gate")
        table = _band_table(_select_mm(rel_bias_a[l], onehot, "nn", "rel_bias_table").reshape(N_HEADS, CHUNK, BAND))
        oa = _attn_a_fwd(qkv, table, q_norm_a[l], k_norm_a[l])
        ob, carries = _attn_b_fwd(qkv)
        oc = _pool_fwd(uc, w_pool[l], pool_scale[l])
        merged, ya, yb, yc = _merge_fwd(oa, ob, oc, glog, b_gate[l], w_a[l], w_b[l], w_c[l])
        x1 = _mm(merged, w_out_f[l], "nn", F32, "out_proj", res=cur)
        h2 = _rmsnorm_fwd(x1, norm_ffn[l], "norm_ffn_fwd")
        u = _mm(h2, w_up_f[l], "nn", F32, "ff_up")
        act = _ff_act_fwd(u, conv_w_f[l], conv_b_f[l])
        x2 = _mm(act, w_down_f[l], "nn", F32, "ff_down", res=x1)
        saved.append(dict(x=cur, h=h, qkv=qkv, carries=carries, uc=uc, glog=glog, table=table, oa=oa, ob=ob, oc=oc,
                          ya=ya, yb=yb, yc=yc, merged=merged, x1=x1, h2=h2, u=u, act=act))
        cur = x2

    dcur, loss_local = _loss_head(cur, target)
    loss = lax.psum(loss_local, ("x", "y", "c"))

    gw = {n: [None] * L for n in WEIGHTS}
    for l in reversed(range(L)):
        s = saved[l]
        da = _mm(dcur, w_down_f[l], "nt", F32, "ff_down_dx", tn_cap=1408)
        gw["w_down"][l] = _mm(s["act"], dcur, "tn", BF16, "ff_down_dw")
        dc, dconv_w, dconv_b = _ff_act_bwd(s["u"], da, conv_w_f[l], conv_b_f[l])
        du = _ff_conv_bwd(dc, conv_w_f[l])
        dh2 = _mm(du, w_up_f[l], "nt", F32, "ff_up_dx")
        gw["w_up"][l] = _ff_natural_order(_mm(s["h2"], du, "tn", BF16, "ff_up_dw"))
        gw["conv_w"][l] = _ff_natural_order(dconv_w)
        gw["conv_b"][l] = _ff_natural_order(dconv_b)[0]
        dx1, dg = _rmsnorm_bwd(s["x1"], norm_ffn[l], dh2, dcur, "norm_ffn_bwd")
        gw["norm_ffn"][l] = dg[0]

        dmerged = _mm(dx1, w_out_f[l], "nt", F32, "out_proj_dx")
        gw["w_out"][l] = _mm(s["merged"], dx1, "tn", BF16, "out_proj_dw")
        dya, dyb, dyc, dglog, db_gate = _merge_bwd(dmerged, s["glog"], b_gate[l], s["ya"], s["yb"], s["yc"])
        gw["b_gate"][l] = db_gate[0]
        do = {}
        for tag, dy, wk, ok in (("a", dya, w_a, s["oa"]), ("b", dyb, w_b, s["ob"]), ("c", dyc, w_c, s["oc"])):
            do[tag] = _mm(dy, wk[l], "nt", BF16, "branch_dx_" + tag)
            gw["w_branch_" + tag][l] = _mm(ok, dy, "tn", BF16, "branch_dw_" + tag)
        duc, dw_pool, dscale = _pool_bwd(s["uc"], do["c"], w_pool[l], pool_scale[l])
        gw["w_pool"][l] = dw_pool
        gw["pool_scale"][l] = dscale[0]
        dqa, dkc, dkp, dvc, dvp, dtable, dgq = _attn_a_bwd(s["qkv"], do["a"], s["table"], q_norm_a[l], k_norm_a[l])
        dka, dva, dgk = _attn_a_bwd_keys(s["qkv"], dkc, dkp, dvc, dvp, k_norm_a[l])
        gw["q_norm_a"][l] = jnp.sum(dgq.reshape(N_HEADS, HEAD_DIM), axis=0)
        gw["k_norm_a"][l] = jnp.sum(dgk.reshape(N_HEADS, HEAD_DIM), axis=0)
        gw["rel_bias_a"][l] = _select_mm(_band_table_bwd(dtable).reshape(N_HEADS, CHUNK * BAND), onehot, "nt",
                                         "rel_bias_table_dw")
        dqb, dkb, dvb = _attn_b_bwd(s["qkv"], s["carries"], do["b"])
        dqkv = jnp.concatenate([dqa, dka, dva, dqb, dkb.astype(BF16), dvb.astype(BF16)], axis=1)
        dh = _mm(dqkv, w_qkv[l], "nt", F32, "proj_qkv_dx")
        dh = _mm(duc, w_uc[l], "nt", F32, "proj_pool_dx", res=dh)
        dh = _mm(dglog, w_g[l], "nt", F32, "proj_gate_dx", res=dh)
        gw["w_in"][l] = jnp.concatenate([_mm(s["h"], dqkv, "tn", BF16, "proj_qkv_dw"),
                                         _mm(s["h"], duc, "tn", BF16, "proj_pool_dw"),
                                         _mm(s["h"], dglog, "tn", BF16, "proj_gate_dw")], axis=1)
        dcur, dg = _rmsnorm_bwd(s["x"], norm_mix[l], dh, dx1, "norm_mix_bwd")
        gw["norm_mix"][l] = dg[0]
    gw = {n: jnp.stack(g) for n, g in gw.items()}

    cw_pieces = gw["conv_w"].reshape(L, 3, N_DEV, -1).transpose(2, 0, 1, 3)
    cw_bits = lax.bitcast_convert_type(cw_pieces, BF16).reshape(N_DEV, -1)
    cw_rows = -(-cw_bits.shape[1] // (16 * PACK_COLS)) * 16
    cw_bits = jnp.pad(cw_bits, ((0, 0), (0, cw_rows * PACK_COLS - cw_bits.shape[1]))).reshape(N_DEV, cw_rows, PACK_COLS)
    pieces = jnp.concatenate([
        _split_columns(gw["w_in"]), _split_columns(gw["w_branch_a"]), _split_columns(gw["w_branch_b"]),
        _split_columns(gw["w_branch_c"]), _split_rows(gw["w_out"]), _split_columns(gw["w_up"]),
        _split_rows(gw["w_down"]), cw_bits], axis=1)
    parts = _exchange(pieces, True, "exchange_grads")
    small = _pack_rows([gw[n] for n in REPLICATED], SMALL_COLS, 16)
    small_parts = _exchange(small, False, "gather_small_grads")

    shard_like = [w[n] for n in SHARDED]
    res = _adamw(parts, *[_pack_rows([d[n] for n in SHARDED], PACK_COLS, 16) for d in (w, m, v)], "adamw_sharded")
    out = {n: r for n, r in zip(SHARDED, zip(*[_unpack_rows(r, shard_like) for r in res]))}
    rep_like = [w[n] for n in REPLICATED]
    res = _adamw(small_parts, *[_pack_rows([d[n] for n in REPLICATED], SMALL_COLS, 16) for d in (w, m, v)],
                 "adamw_replicated")
    out.update({n: r for n, r in zip(REPLICATED, zip(*[_unpack_rows(r, rep_like) for r in res]))})
    cw_parts = parts[:, parts.shape[1] - cw_rows:].reshape(N_DEV, -1)[:, :2 * conv_w.size].reshape(N_DEV, conv_w.size, 2)
    cw_parts = lax.bitcast_convert_type(cw_parts, F32)
    cw_parts = jnp.stack([_pack_rows([cw_parts[d]], SMALL_COLS, 16) for d in range(N_DEV)])
    res = _adamw(cw_parts, *[_pack_rows([d["conv_w"]], SMALL_COLS, 16) for d in (w, m, v)], "adamw_conv_w")
    out["conv_w"] = tuple(_unpack_rows(r, [conv_w])[0] for r in res)

    grads, deltas, new_m, new_v = ([out[n][i] for n in WEIGHTS] for i in range(4))
    return (loss, dcur.reshape(x.shape), *grads, *deltas, *new_m, *new_v)
```

```python
import functools
import math

import jax
import jax.numpy as jnp
from jax import lax
from jax.experimental import pallas as pl
from jax.experimental.pallas import tpu as pltpu

F32 = jnp.float32
BF16 = jnp.bfloat16

N_DEV = 8
D_MODEL = 1024
N_HEADS = 8
HEAD_DIM = 64
CHUNK = 64
N_LEFT = 8
BAND = (N_LEFT + 1) * CHUNK
WIDTH = N_HEADS * HEAD_DIM
POOL_WINDOWS = (2, 4, 8, 16)
POOL_DIM = 128
MAX_REL = 2 * CHUNK
REL_TABLE = MAX_REL + CHUNK
D_FF = 2816
EPS = 1e-6
SB_SCAN = 256
SB_ROWS = 512
SB_KEYS = 512
A_BLOCK = N_LEFT * CHUNK
HALO = 16
LANES = 128
VMEM_LIMIT = 56 * 1024 * 1024

ADAM_LR = 0.001
ADAM_B1 = 0.9
ADAM_B2 = 0.999
ADAM_EPS = 1e-08
ADAM_WD = 0.01
ADAM_STEP = 10

MESH = pl.DeviceIdType.MESH


def _params(sem):
    return pltpu.CompilerParams(dimension_semantics=sem, vmem_limit_bytes=VMEM_LIMIT)


def _pick(n, cap):
    if n <= cap:
        return n
    best = None
    for t in range(LANES, cap + 1, LANES):
        if n % t == 0:
            best = t
    assert best is not None, (n, cap)
    return best


MM_TILE_CAP = 1408


def _mm(a, b, mode, out_dtype, name, tm=MM_TILE_CAP, tn_cap=MM_TILE_CAP, tk_cap=MM_TILE_CAP, res=None):
    if mode == "nn":
        (M, K), (K2, N) = a.shape, b.shape
    elif mode == "nt":
        (M, K), (N, K2) = a.shape, b.shape
    else:
        (K, M), (K2, N) = a.shape, b.shape
    assert K == K2, (a.shape, b.shape, mode)
    tm = _pick(M, tm)
    tn = _pick(N, tn_cap)
    tk = _pick(K, tk_cap)
    nk = K // tk
    if mode == "nn":
        dims = (((1,), (0,)), ((), ()))
        a_spec = pl.BlockSpec((tm, tk), lambda i, j, k: (i, k))
        b_spec = pl.BlockSpec((tk, tn), lambda i, j, k: (k, j))
    elif mode == "nt":
        dims = (((1,), (1,)), ((), ()))
        a_spec = pl.BlockSpec((tm, tk), lambda i, j, k: (i, k))
        b_spec = pl.BlockSpec((tn, tk), lambda i, j, k: (j, k))
    else:
        dims = (((0,), (0,)), ((), ()))
        a_spec = pl.BlockSpec((tk, tm), lambda i, j, k: (k, i))
        b_spec = pl.BlockSpec((tk, tn), lambda i, j, k: (k, j))

    o_spec = pl.BlockSpec((tm, tn), lambda i, j, k: (i, j))

    def body(a_ref, b_ref, *rest):
        res_ref = rest[0] if res is not None else None
        o_ref = rest[1] if res is not None else rest[0]
        part = lax.dot_general(a_ref[...].astype(BF16), b_ref[...].astype(BF16), dims, preferred_element_type=F32)
        if nk == 1:
            o_ref[...] = (part if res is None else part + res_ref[...]).astype(out_dtype)
            return
        acc_ref = rest[-1]
        k = pl.program_id(2)

        @pl.when(k == 0)
        def _():
            acc_ref[...] = part

        @pl.when(k > 0)
        def _():
            acc_ref[...] += part

        @pl.when(k == nk - 1)
        def _():
            total = acc_ref[...] if res is None else acc_ref[...] + res_ref[...]
            o_ref[...] = total.astype(out_dtype)

    return pl.pallas_call(
        body, name=name,
        grid=(M // tm, N // tn, nk),
        in_specs=[a_spec, b_spec] + ([o_spec] if res is not None else []),
        out_specs=o_spec,
        out_shape=jax.ShapeDtypeStruct((M, N), out_dtype),
        scratch_shapes=[pltpu.VMEM((tm, tn), F32)] if nk > 1 else [],
        compiler_params=_params(("parallel", "parallel", "arbitrary")),
    )(*((a, b) if res is None else (a, b, res)))


def _rmsnorm_fwd(x, gain, name, tm=512):
    T, C = x.shape

    def body(x_ref, g_ref, h_ref):
        xv = x_ref[...]
        r = lax.rsqrt(jnp.mean(xv * xv, axis=-1, keepdims=True) + EPS)
        h_ref[...] = (xv * r * g_ref[...]).astype(BF16)

    return pl.pallas_call(
        body, name=name, grid=(T // tm,),
        in_specs=[pl.BlockSpec((tm, C), lambda i: (i, 0)), pl.BlockSpec((1, C), lambda i: (0, 0))],
        out_specs=pl.BlockSpec((tm, C), lambda i: (i, 0)),
        out_shape=jax.ShapeDtypeStruct((T, C), BF16),
        compiler_params=_params(("parallel",)),
    )(x, gain.reshape(1, C))


def _rmsnorm_bwd(x, gain, dh, dres, name, tm=512):
    T, C = x.shape

    def body(x_ref, g_ref, dh_ref, dres_ref, dx_ref, dg_ref):
        @pl.when(pl.program_id(0) == 0)
        def _():
            dg_ref[...] = jnp.zeros_like(dg_ref)

        xv = x_ref[...]
        dy = dh_ref[...].astype(F32)
        r = lax.rsqrt(jnp.mean(xv * xv, axis=-1, keepdims=True) + EPS)
        gdy = dy * g_ref[...]
        inner = jnp.mean(xv * gdy, axis=-1, keepdims=True)
        dx_ref[...] = dres_ref[...] + r * gdy - xv * (r * r * r * inner)
        dg_ref[...] += jnp.sum(dy * xv * r, axis=0, keepdims=True)

    return pl.pallas_call(
        body, name=name, grid=(T // tm,),
        in_specs=[pl.BlockSpec((tm, C), lambda i: (i, 0)), pl.BlockSpec((1, C), lambda i: (0, 0)),
                  pl.BlockSpec((tm, C), lambda i: (i, 0)), pl.BlockSpec((tm, C), lambda i: (i, 0))],
        out_specs=[pl.BlockSpec((tm, C), lambda i: (i, 0)), pl.BlockSpec((1, C), lambda i: (0, 0))],
        out_shape=[jax.ShapeDtypeStruct((T, C), F32), jax.ShapeDtypeStruct((1, C), F32)],
        compiler_params=_params(("arbitrary",)),
    )(x, gain.reshape(1, C), dh, dres)


MASKED = -1e30


def _pair_sum(x, same_head):
    hi = x.astype(BF16)
    lo = (x - hi.astype(F32)).astype(BF16)
    return jnp.dot(hi, same_head, preferred_element_type=F32) + jnp.dot(lo, same_head, preferred_element_type=F32)


def _same_head():
    r = lax.broadcasted_iota(jnp.int32, (2 * HEAD_DIM, 2 * HEAD_DIM), 0)
    c = lax.broadcasted_iota(jnp.int32, (2 * HEAD_DIM, 2 * HEAD_DIM), 1)
    return jnp.where((r < HEAD_DIM) == (c < HEAD_DIM), 1.0, 0.0).astype(BF16)


def _pair_norm(t, g, same_head):
    tf = t.astype(F32)
    r = lax.rsqrt(_pair_sum(tf * tf, same_head) * (1.0 / HEAD_DIM) + EPS)
    return tf * r * g


def _pair_norm_bwd(t, g, dn, same_head):
    tf = t.astype(F32)
    r = lax.rsqrt(_pair_sum(tf * tf, same_head) * (1.0 / HEAD_DIM) + EPS)
    gd = dn * g
    inner = _pair_sum(tf * gd, same_head) * (1.0 / HEAD_DIM)
    return r * gd - tf * (r * r * r * inner), jnp.sum(dn * tf * r, axis=0, keepdims=True)


def _band_table(bias):
    table = jnp.full((N_HEADS, A_BLOCK, 2 * A_BLOCK), MASKED, F32)
    for c in range(N_LEFT):
        table = table.at[:, c * CHUNK:(c + 1) * CHUNK, c * CHUNK:c * CHUNK + BAND].set(bias)
    return table.reshape(N_HEADS // 2, 2 * A_BLOCK, 2 * A_BLOCK)


def _band_table_bwd(dtable):
    dtable = dtable.reshape(N_HEADS, A_BLOCK, 2 * A_BLOCK)
    return sum(dtable[:, c * CHUNK:(c + 1) * CHUNK, c * CHUNK:c * CHUNK + BAND] for c in range(N_LEFT))


def _a_specs(T):
    nb = T // A_BLOCK
    pairs = N_HEADS // 2
    col = lambda which: which * pairs
    cur = lambda which: pl.BlockSpec((A_BLOCK, 2 * HEAD_DIM), lambda p, i: (i, col(which) + p))
    prev = lambda which: pl.BlockSpec((A_BLOCK, 2 * HEAD_DIM), lambda p, i: (jnp.maximum(i - 1, 0), col(which) + p))
    nxt = lambda which: pl.BlockSpec((A_BLOCK, 2 * HEAD_DIM), lambda p, i: (jnp.minimum(i + 1, nb - 1), col(which) + p))
    table = pl.BlockSpec((1, 2 * A_BLOCK, 2 * A_BLOCK), lambda p, i: (p, 0, 0))
    gain = pl.BlockSpec((1, 2 * HEAD_DIM), lambda p, i: (0, 0))
    gacc = pl.BlockSpec((1, 1, 2 * HEAD_DIM), lambda p, i: (p, 0, 0))
    return nb, pairs, cur, prev, nxt, table, gain, gacc


def _a_probs(q_ref, kc_ref, kp_ref, t_ref, gq_ref, gk_ref, same_head, first):
    scale = 1.0 / math.sqrt(HEAD_DIM)
    qst = _stack_heads((_pair_norm(q_ref[...], gq_ref[...], same_head) * scale).astype(BF16))
    kcat = jnp.concatenate([_pair_norm(kp_ref[...], gk_ref[...], same_head).astype(BF16),
                            _pair_norm(kc_ref[...], gk_ref[...], same_head).astype(BF16)], axis=0)
    s = lax.dot_general(qst, kcat, (((1,), (1,)), ((), ())), preferred_element_type=F32) + t_ref[0]
    col = lax.broadcasted_iota(jnp.int32, s.shape, 1)
    s = jnp.where(col >= jnp.where(first, A_BLOCK, 0), s, MASKED)
    e = jnp.exp(s - jnp.max(s, axis=-1, keepdims=True))
    return qst, kcat, e, jnp.sum(e, axis=-1, keepdims=True)


def _attn_a_fwd(qkv, table, gq, gk):
    T = qkv.shape[0]
    nb, pairs, cur, prev, _, tspec, gspec, _ = _a_specs(T)

    def body(q_ref, kc_ref, kp_ref, vc_ref, vp_ref, t_ref, gq_ref, gk_ref, o_ref):
        same_head = _same_head()
        _, _, e, total = _a_probs(q_ref, kc_ref, kp_ref, t_ref, gq_ref, gk_ref, same_head, pl.program_id(1) == 0)
        vcat = jnp.concatenate([vp_ref[...], vc_ref[...]], axis=0)
        p = (e / total).astype(BF16)
        o_ref[...] = _unstack_heads(jnp.dot(p, vcat, preferred_element_type=F32)).astype(BF16)

    return pl.pallas_call(
        body, name="attn_a_fwd", grid=(pairs, nb),
        in_specs=[cur(0), cur(1), prev(1), cur(2), prev(2), tspec, gspec, gspec],
        out_specs=pl.BlockSpec((A_BLOCK, 2 * HEAD_DIM), lambda p, i: (i, p)),
        out_shape=jax.ShapeDtypeStruct((T, WIDTH), BF16),
        compiler_params=_params(("parallel", "arbitrary")),
    )(qkv, qkv, qkv, qkv, qkv, table, jnp.tile(gq.reshape(1, HEAD_DIM), (1, 2)), jnp.tile(gk.reshape(1, HEAD_DIM), (1, 2)))


def _attn_a_bwd(qkv, do, table, gq, gk):
    T = qkv.shape[0]
    nb, pairs, cur, prev, _, tspec, gspec, gacc = _a_specs(T)
    scale = 1.0 / math.sqrt(HEAD_DIM)
    oblk = pl.BlockSpec((A_BLOCK, 2 * HEAD_DIM), lambda p, i: (i, p))

    def body(q_ref, kc_ref, kp_ref, vc_ref, vp_ref, do_ref, t_ref, gq_ref, gk_ref,
             dq_ref, dkc_ref, dkp_ref, dvc_ref, dvp_ref, dt_ref, dgq_ref):
        first = pl.program_id(1) == 0

        @pl.when(first)
        def _():
            dt_ref[...] = jnp.zeros_like(dt_ref)
            dgq_ref[...] = jnp.zeros_like(dgq_ref)

        same_head = _same_head()
        qst, kcat, e, total = _a_probs(q_ref, kc_ref, kp_ref, t_ref, gq_ref, gk_ref, same_head, first)
        vcat = jnp.concatenate([vp_ref[...], vc_ref[...]], axis=0)
        dost = _stack_heads(do_ref[...])
        p = e / total
        dp = lax.dot_general(dost, vcat, (((1,), (1,)), ((), ())), preferred_element_type=F32)
        ds = p * (dp - jnp.sum(p * dp, axis=-1, keepdims=True))
        dt_ref[0] += ds
        dsb = ds.astype(BF16)
        dqn = _unstack_heads(jnp.dot(dsb, kcat, preferred_element_type=F32)) * scale
        dq, dg = _pair_norm_bwd(q_ref[...], gq_ref[...], dqn, same_head)
        dq_ref[...] = dq.astype(BF16)
        dgq_ref[0] += dg
        dk = lax.dot_general(dsb, qst, (((0,), (0,)), ((), ())), preferred_element_type=F32)
        dv = lax.dot_general(p.astype(BF16), dost, (((0,), (0,)), ((), ())), preferred_element_type=F32)
        dkp_ref[...] = dk[:A_BLOCK]
        dkc_ref[...] = dk[A_BLOCK:]
        dvp_ref[...] = dv[:A_BLOCK]
        dvc_ref[...] = dv[A_BLOCK:]

    wide = jax.ShapeDtypeStruct((T, WIDTH), F32)
    return pl.pallas_call(
        body, name="attn_a_bwd", grid=(pairs, nb),
        in_specs=[cur(0), cur(1), prev(1), cur(2), prev(2), oblk, tspec, gspec, gspec],
        out_specs=[oblk, oblk, oblk, oblk, oblk, tspec, gacc],
        out_shape=[jax.ShapeDtypeStruct((T, WIDTH), BF16), wide, wide, wide, wide,
                   jax.ShapeDtypeStruct((pairs, 2 * A_BLOCK, 2 * A_BLOCK), F32),
                   jax.ShapeDtypeStruct((pairs, 1, 2 * HEAD_DIM), F32)],
        compiler_params=_params(("parallel", "arbitrary")),
    )(qkv, qkv, qkv, qkv, qkv, do, table, jnp.tile(gq.reshape(1, HEAD_DIM), (1, 2)),
      jnp.tile(gk.reshape(1, HEAD_DIM), (1, 2)))


def _attn_a_bwd_keys(qkv, dkc, dkp, dvc, dvp, gk):
    T = qkv.shape[0]
    nb, pairs, cur, _, _, _, gspec, gacc = _a_specs(T)
    oblk = pl.BlockSpec((A_BLOCK, 2 * HEAD_DIM), lambda p, i: (i, p))
    onext = pl.BlockSpec((A_BLOCK, 2 * HEAD_DIM), lambda p, i: (jnp.minimum(i + 1, nb - 1), p))

    def body(k_ref, dkc_ref, dkp_ref, dvc_ref, dvp_ref, gk_ref, dk_ref, dv_ref, dgk_ref):
        i = pl.program_id(1)

        @pl.when(i == 0)
        def _():
            dgk_ref[...] = jnp.zeros_like(dgk_ref)

        has_next = (i < nb - 1).astype(F32)
        dkn = dkc_ref[...] + has_next * dkp_ref[...]
        dk, dg = _pair_norm_bwd(k_ref[...], gk_ref[...], dkn, _same_head())
        dk_ref[...] = dk.astype(BF16)
        dv_ref[...] = (dvc_ref[...] + has_next * dvp_ref[...]).astype(BF16)
        dgk_ref[0] += dg

    blk = jax.ShapeDtypeStruct((T, WIDTH), BF16)
    return pl.pallas_call(
        body, name="attn_a_bwd_keys", grid=(pairs, nb),
        in_specs=[cur(1), oblk, onext, oblk, onext, gspec],
        out_specs=[oblk, oblk, gacc],
        out_shape=[blk, blk, jax.ShapeDtypeStruct((pairs, 1, 2 * HEAD_DIM), F32)],
        compiler_params=_params(("parallel", "arbitrary")),
    )(qkv, dkc, dkp, dvc, dvp, jnp.tile(gk.reshape(1, HEAD_DIM), (1, 2)))


def _scan_matrix(later):
    r = lax.broadcasted_iota(jnp.int32, (SB_SCAN, SB_SCAN), 0)
    c = lax.broadcasted_iota(jnp.int32, (SB_SCAN, SB_SCAN), 1)
    return jnp.where((r > c) if later else (r < c), 1.0, 0.0).astype(BF16)


def _running_sums(x, carry, scan, later):
    n = SB_KEYS // SB_SCAN
    parts = [None] * n
    total = carry
    for sb in (reversed(range(n)) if later else range(n)):
        xs = x[:, sb * SB_SCAN:(sb + 1) * SB_SCAN]
        local = jnp.dot(xs.astype(BF16), scan, preferred_element_type=F32)
        parts[sb] = local if total is None else local + total
        rowsum = jnp.sum(xs, axis=-1, keepdims=True)
        total = rowsum if total is None else total + rowsum
    return (parts[0] if n == 1 else jnp.concatenate(parts, axis=1)), total


def _sb_log_sigmoids(z):
    neg_abs = pltpu.bitcast(pltpu.bitcast(z, jnp.uint32) | jnp.uint32(0x80000000), F32)
    take = jnp.minimum(z, 0.0) - jnp.log(1.0 + jnp.exp(neg_abs))
    return take, take - z


def _sb_mask():
    r = lax.broadcasted_iota(jnp.int32, (2 * SB_ROWS, SB_KEYS), 0)
    c = lax.broadcasted_iota(jnp.int32, (2 * SB_ROWS, SB_KEYS), 1)
    return c < jnp.where(r >= SB_ROWS, r - SB_ROWS, r)


def _stack_heads(t):
    lane = lax.broadcasted_iota(jnp.int32, t.shape, 1)
    zero = jnp.zeros_like(t)
    return jnp.concatenate([jnp.where(lane < HEAD_DIM, t, zero), jnp.where(lane >= HEAD_DIM, t, zero)], axis=0)


def _unstack_heads(t):
    rows = t.shape[0] // 2
    lane = lax.broadcasted_iota(jnp.int32, (rows, 2 * HEAD_DIM), 1)
    return jnp.where(lane < HEAD_DIM, t[:rows], t[rows:])


def _sb_specs(T):
    nq = T // SB_ROWS
    blk = lambda col: pl.BlockSpec((SB_ROWS, 2 * HEAD_DIM), lambda p, i: (i, col + p))
    full = lambda col: pl.BlockSpec((T, 2 * HEAD_DIM), lambda p, i: (0, col + p))
    return nq, blk, full


def _key_rows(j):
    return pl.ds(pl.multiple_of(j * SB_KEYS, SB_KEYS), SB_KEYS)


def _attn_b_fwd(qkv):
    T = qkv.shape[0]
    pairs = N_HEADS // 2
    nq, blk, full = _sb_specs(T)
    scale = 1.0 / math.sqrt(HEAD_DIM)

    assert nq <= LANES

    def body(q_ref, k_ref, v_ref, o_ref, c_ref, acc_ref, carry_ref):
        i = pl.program_id(1)
        scan = _scan_matrix(True)
        qst = _stack_heads((q_ref[...].astype(F32) * scale).astype(BF16))
        lane = lax.broadcasted_iota(jnp.int32, (2 * SB_ROWS, LANES), 1)

        def span(j, carry, mask):
            z = lax.dot_general(qst, k_ref[_key_rows(j), :], (((1,), (1,)), ((), ())), preferred_element_type=F32)
            take, keep = _sb_log_sigmoids(z)
            if mask is not None:
                keep = jnp.where(mask, keep, 0.0)
            tail, total = _running_sums(keep, carry, scan, True)
            w = jnp.exp(take + tail)
            if mask is not None:
                w = jnp.where(mask, w, 0.0)
            return jnp.dot(w.astype(BF16), v_ref[_key_rows(j), :], preferred_element_type=F32), total

        acc_ref[...], carry_ref[...] = span(i, None, _sb_mask())
        c_ref[0, 0] = jnp.zeros((2 * SB_ROWS, LANES), F32)

        @pl.loop(0, i)
        def _(jj):
            j = i - 1 - jj
            carry = carry_ref[...]
            c_ref[0, 0] = jnp.where(lane == j, carry, c_ref[0, 0])
            out, carry_ref[...] = span(j, carry, None)
            acc_ref[...] += out

        o_ref[...] = _unstack_heads(acc_ref[...]).astype(BF16)

    return pl.pallas_call(
        body, name="attn_b_fwd", grid=(pairs, nq),
        in_specs=[blk(3 * pairs), full(4 * pairs), full(5 * pairs)],
        out_specs=[pl.BlockSpec((SB_ROWS, 2 * HEAD_DIM), lambda p, i: (i, p)),
                   pl.BlockSpec((1, 1, 2 * SB_ROWS, LANES), lambda p, i: (p, i, 0, 0))],
        out_shape=[jax.ShapeDtypeStruct((T, WIDTH), BF16), jax.ShapeDtypeStruct((pairs, nq, 2 * SB_ROWS, LANES), F32)],
        scratch_shapes=[pltpu.VMEM((2 * SB_ROWS, 2 * HEAD_DIM), F32), pltpu.VMEM((2 * SB_ROWS, 1), F32)],
        compiler_params=_params(("parallel", "arbitrary")),
    )(qkv, qkv, qkv)


def _attn_b_bwd(qkv, carries, do):
    T = qkv.shape[0]
    pairs = N_HEADS // 2
    nq, blk, full = _sb_specs(T)
    scale = 1.0 / math.sqrt(HEAD_DIM)
    oblk = pl.BlockSpec((SB_ROWS, 2 * HEAD_DIM), lambda p, i: (i, p))
    ofull = pl.BlockSpec((T, 2 * HEAD_DIM), lambda p, i: (0, p))

    def body(q_ref, k_ref, v_ref, c_ref, do_ref, dq_ref, dk_ref, dv_ref, dqacc_ref, before_ref):
        i = pl.program_id(1)

        @pl.when(i == 0)
        def _():
            dk_ref[...] = jnp.zeros_like(dk_ref)
            dv_ref[...] = jnp.zeros_like(dv_ref)

        scan_later = _scan_matrix(True)
        scan_earlier = _scan_matrix(False)
        qst = _stack_heads((q_ref[...].astype(F32) * scale).astype(BF16))
        dost = _stack_heads(do_ref[...].astype(BF16))
        lane = lax.broadcasted_iota(jnp.int32, (2 * SB_ROWS, LANES), 1)

        def span(j, later, mask):
            kv = k_ref[_key_rows(j), :]
            vv = v_ref[_key_rows(j), :]
            z = lax.dot_general(qst, kv, (((1,), (1,)), ((), ())), preferred_element_type=F32)
            take, keep = _sb_log_sigmoids(z)
            sig = jnp.exp(take)
            if mask is not None:
                keep = jnp.where(mask, keep, 0.0)
            tail, _ = _running_sums(keep, later, scan_later, True)
            w = jnp.exp(take + tail)
            if mask is not None:
                w = jnp.where(mask, w, 0.0)
            g = w * lax.dot_general(dost, vv, (((1,), (1,)), ((), ())), preferred_element_type=F32)
            before, before_ref[...] = _running_sums(g, before_ref[...], scan_earlier, False)
            dz = g - sig * (g + before)
            if mask is not None:
                dz = jnp.where(mask, dz, 0.0)
            dzb = dz.astype(BF16)
            dqacc_ref[...] += jnp.dot(dzb, kv, preferred_element_type=F32)
            dk_ref[_key_rows(j), :] += lax.dot_general(dzb, qst, (((0,), (0,)), ((), ())), preferred_element_type=F32)
            dv_ref[_key_rows(j), :] += lax.dot_general(w.astype(BF16), dost, (((0,), (0,)), ((), ())),
                                                       preferred_element_type=F32)

        dqacc_ref[...] = jnp.zeros_like(dqacc_ref)
        before_ref[...] = jnp.zeros_like(before_ref)

        @pl.loop(0, i)
        def _(j):
            later = jnp.sum(jnp.where(lane == j, c_ref[0, 0], 0.0), axis=-1, keepdims=True)
            span(j, later, None)

        span(i, None, _sb_mask())
        dq_ref[...] = (_unstack_heads(dqacc_ref[...]) * scale).astype(BF16)

    wide = jax.ShapeDtypeStruct((T, WIDTH), F32)
    return pl.pallas_call(
        body, name="attn_b_bwd", grid=(pairs, nq),
        in_specs=[blk(3 * pairs), full(4 * pairs), full(5 * pairs),
                  pl.BlockSpec((1, 1, 2 * SB_ROWS, LANES), lambda p, i: (p, i, 0, 0)), oblk],
        out_specs=[oblk, ofull, ofull],
        out_shape=[jax.ShapeDtypeStruct((T, WIDTH), BF16), wide, wide],
        scratch_shapes=[pltpu.VMEM((2 * SB_ROWS, 2 * HEAD_DIM), F32), pltpu.VMEM((2 * SB_ROWS, 1), F32)],
        compiler_params=_params(("parallel", "arbitrary")),
    )(qkv, qkv, qkv, carries, do)


def _window_sums(ext, forward):
    n = ext.shape[0]
    out = []
    s = ext
    for step in (1, 2, 4, 8):
        s = s + pltpu.roll(s, (n - step) if forward else step, 0)
        out.append(s)
    return out


def _pool_counts(base, rows, win):
    t = base + lax.broadcasted_iota(jnp.int32, (rows, 1), 0)
    return jnp.minimum(t + 1, win).astype(F32)


def _pooled(u_ref, up_ref, i, tm):
    prev = jnp.where(i > 0, up_ref[...], 0.0)
    ext = jnp.concatenate([prev, u_ref[...]], axis=0)
    sums = _window_sums(ext, False)
    parts = []
    for g, win in enumerate(POOL_WINDOWS):
        cols = slice(g * POOL_DIM, (g + 1) * POOL_DIM)
        cnt = _pool_counts(i * tm, tm, win)
        parts.append(sums[g][HALO:, cols] / cnt - ext[HALO:, cols])
    return parts


def _pool_fwd(ucg, w_pool, scale, tm=512):
    T = ucg.shape[0]
    C = WIDTH

    def body(u_ref, up_ref, w_ref, s_ref, o_ref):
        i = pl.program_id(0)
        parts = _pooled(u_ref, up_ref, i, tm)
        for g in range(len(POOL_WINDOWS)):
            mixed = jnp.dot(parts[g].astype(BF16), w_ref[g], preferred_element_type=F32)
            o_ref[:, g * POOL_DIM:(g + 1) * POOL_DIM] = (mixed * s_ref[:, g * POOL_DIM:(g + 1) * POOL_DIM]).astype(BF16)

    return pl.pallas_call(
        body, name="pool_fwd", grid=(T // tm,),
        in_specs=[pl.BlockSpec((tm, C), lambda i: (i, 0)),
                  pl.BlockSpec((HALO, C), lambda i: (jnp.maximum(i * (tm // HALO) - 1, 0), 0)),
                  pl.BlockSpec((len(POOL_WINDOWS), POOL_DIM, POOL_DIM), lambda i: (0, 0, 0)),
                  pl.BlockSpec((1, C), lambda i: (0, 0))],
        out_specs=pl.BlockSpec((tm, C), lambda i: (i, 0)),
        out_shape=jax.ShapeDtypeStruct((T, C), BF16),
        compiler_params=_params(("parallel",)),
    )(ucg, ucg, w_pool.astype(BF16), scale.reshape(1, C))


def _pool_bwd(ucg, do_c, w_pool, scale, tm=512):
    T = ucg.shape[0]
    C = WIDTH
    nt = T // tm
    G = len(POOL_WINDOWS)

    def body(u_ref, up_ref, do_ref, don_ref, w_ref, s_ref, du_ref, dw_ref, ds_ref):
        i = pl.program_id(0)

        @pl.when(i == 0)
        def _():
            dw_ref[...] = jnp.zeros_like(dw_ref)
            ds_ref[...] = jnp.zeros_like(ds_ref)

        parts = _pooled(u_ref, up_ref, i, tm)
        nxt = jnp.where(i < nt - 1, don_ref[...].astype(F32), 0.0)
        do_ext = jnp.concatenate([do_ref[...].astype(F32), nxt], axis=0) * s_ref[...]
        for g, win in enumerate(POOL_WINDOWS):
            cols = slice(g * POOL_DIM, (g + 1) * POOL_DIM)
            pooled_b = parts[g].astype(BF16)
            dmix = do_ext[:, cols].astype(BF16)
            mixed = jnp.dot(pooled_b, w_ref[g], preferred_element_type=F32)
            ds_ref[:, cols] += jnp.sum(do_ref[:, cols].astype(F32) * mixed, axis=0, keepdims=True)
            dw_ref[g] += lax.dot_general(pooled_b, dmix[:tm], (((0,), (0,)), ((), ())), preferred_element_type=F32)
            dpool = lax.dot_general(dmix, w_ref[g], (((1,), (1,)), ((), ())), preferred_element_type=F32)
            scaled = dpool / _pool_counts(i * tm, tm + HALO, win)
            fwd = _window_sums(scaled, True)[g]
            du_ref[:, cols] = (fwd[:tm] - dpool[:tm]).astype(BF16)

    return pl.pallas_call(
        body, name="pool_bwd", grid=(nt,),
        in_specs=[pl.BlockSpec((tm, C), lambda i: (i, 0)),
                  pl.BlockSpec((HALO, C), lambda i: (jnp.maximum(i * (tm // HALO) - 1, 0), 0)),
                  pl.BlockSpec((tm, C), lambda i: (i, 0)),
                  pl.BlockSpec((HALO, C), lambda i: (jnp.minimum((i + 1) * (tm // HALO), T // HALO - 1), 0)),
                  pl.BlockSpec((G, POOL_DIM, POOL_DIM), lambda i: (0, 0, 0)),
                  pl.BlockSpec((1, C), lambda i: (0, 0))],
        out_specs=[pl.BlockSpec((tm, C), lambda i: (i, 0)),
                   pl.BlockSpec((G, POOL_DIM, POOL_DIM), lambda i: (0, 0, 0)),
                   pl.BlockSpec((1, C), lambda i: (0, 0))],
        out_shape=[jax.ShapeDtypeStruct((T, C), BF16), jax.ShapeDtypeStruct((G, POOL_DIM, POOL_DIM), F32),
                   jax.ShapeDtypeStruct((1, C), F32)],
        compiler_params=_params(("arbitrary",)),
    )(ucg, ucg, do_c, do_c, w_pool.astype(BF16), scale.reshape(1, C))


def _merge_fwd(oa, ob, oc, glog, b_gate, wa, wb, wc, tm=256):
    T = oa.shape[0]
    Dm = D_MODEL
    row = lambda c: pl.BlockSpec((tm, c), lambda i: (i, 0))
    wspec = pl.BlockSpec((WIDTH, Dm), lambda i: (0, 0))

    def body(oa_ref, ob_ref, oc_ref, g_ref, b_ref, wa_ref, wb_ref, wc_ref, m_ref, ya_ref, yb_ref, yc_ref):
        merged = jnp.zeros((tm, Dm), F32)
        for kk, (o_ref, w_ref, y_ref) in enumerate(((oa_ref, wa_ref, ya_ref), (ob_ref, wb_ref, yb_ref),
                                                    (oc_ref, wc_ref, yc_ref))):
            y = jnp.dot(o_ref[...].astype(BF16), w_ref[...], preferred_element_type=F32)
            gate = jax.nn.sigmoid(g_ref[:, kk * Dm:(kk + 1) * Dm] + b_ref[:, kk * Dm:(kk + 1) * Dm])
            merged = merged + gate * y
            y_ref[...] = y.astype(BF16)
        m_ref[...] = merged.astype(BF16)

    out = jax.ShapeDtypeStruct((T, Dm), BF16)
    return pl.pallas_call(
        body, name="merge_fwd", grid=(T // tm,),
        in_specs=[row(WIDTH), row(WIDTH), row(WIDTH), row(3 * Dm), pl.BlockSpec((1, 3 * Dm), lambda i: (0, 0)),
                  wspec, wspec, wspec],
        out_specs=[row(Dm)] * 4,
        out_shape=[out] * 4,
        compiler_params=_params(("parallel",)),
    )(oa, ob, oc, glog, b_gate.reshape(1, 3 * Dm), wa, wb, wc)


def _merge_bwd(dmerged, glog, b_gate, ya, yb, yc, tm=256):
    T = dmerged.shape[0]
    Dm = D_MODEL
    row = lambda c: pl.BlockSpec((tm, c), lambda i: (i, 0))

    def body(dm_ref, g_ref, b_ref, ya_ref, yb_ref, yc_ref, dya_ref, dyb_ref, dyc_ref, dg_ref, db_ref):
        @pl.when(pl.program_id(0) == 0)
        def _():
            db_ref[...] = jnp.zeros_like(db_ref)

        dm = dm_ref[...]
        for kk, (y_ref, dy_ref) in enumerate(((ya_ref, dya_ref), (yb_ref, dyb_ref), (yc_ref, dyc_ref))):
            cols = slice(kk * Dm, (kk + 1) * Dm)
            gate = jax.nn.sigmoid(g_ref[:, cols] + b_ref[:, cols])
            dy_ref[...] = (dm * gate).astype(BF16)
            dlog = dm * y_ref[...].astype(F32) * gate * (1.0 - gate)
            dg_ref[:, cols] = dlog.astype(BF16)
            db_ref[:, cols] += jnp.sum(dlog, axis=0, keepdims=True)

    out = jax.ShapeDtypeStruct((T, Dm), BF16)
    return pl.pallas_call(
        body, name="merge_bwd", grid=(T // tm,),
        in_specs=[row(Dm), row(3 * Dm), pl.BlockSpec((1, 3 * Dm), lambda i: (0, 0)), row(Dm), row(Dm), row(Dm)],
        out_specs=[row(Dm), row(Dm), row(Dm), row(3 * Dm), pl.BlockSpec((1, 3 * Dm), lambda i: (0, 0))],
        out_shape=[out, out, out, jax.ShapeDtypeStruct((T, 3 * Dm), BF16), jax.ShapeDtypeStruct((1, 3 * Dm), F32)],
        compiler_params=_params(("arbitrary",)),
    )(dmerged, glog, b_gate.reshape(1, 3 * Dm), ya, yb, yc)


def _residual_add(x, y, name, tm=512):
    T, C = x.shape

    def body(x_ref, y_ref, o_ref):
        o_ref[...] = x_ref[...] + y_ref[...]

    spec = pl.BlockSpec((tm, C), lambda i: (i, 0))
    return pl.pallas_call(body, name=name, grid=(T // tm,), in_specs=[spec, spec], out_specs=spec,
                          out_shape=jax.ShapeDtypeStruct((T, C), F32), compiler_params=_params(("parallel",)))(x, y)


FF_TILE = 256
FF_TILES = D_FF // FF_TILE
CONV_HALO = 8


def _ff_pair_order(w):
    lead = w.shape[:-1]
    n = len(lead)
    w = w.reshape(*lead, 2, FF_TILES, FF_TILE)
    return jnp.swapaxes(w, n, n + 1).reshape(*lead, 2 * D_FF)


def _ff_natural_order(w):
    lead = w.shape[:-1]
    n = len(lead)
    w = w.reshape(*lead, FF_TILES, 2, FF_TILE)
    return jnp.swapaxes(w, n, n + 1).reshape(*lead, 2 * D_FF)


def _conv(ext, w_ref, b_ref):
    c = b_ref[...] + w_ref[2:3, :] * ext
    c = c + w_ref[1:2, :] * pltpu.roll(ext, 1, 0)
    c = c + w_ref[0:1, :] * pltpu.roll(ext, 2, 0)
    return c[CONV_HALO:]


def _ff_specs(T, tm):
    pair = pl.BlockSpec((tm, 2 * FF_TILE), lambda i, j: (i, j))
    prev = pl.BlockSpec((CONV_HALO, 2 * FF_TILE), lambda i, j: (jnp.maximum(i * (tm // CONV_HALO) - 1, 0), j))
    nxt = pl.BlockSpec((CONV_HALO, 2 * FF_TILE),
                       lambda i, j: (jnp.minimum((i + 1) * (tm // CONV_HALO), T // CONV_HALO - 1), j))
    half = pl.BlockSpec((tm, FF_TILE), lambda i, j: (i, j))
    small = lambda r: pl.BlockSpec((r, 2 * FF_TILE), lambda i, j: (0, j))
    return pair, prev, nxt, half, small


def _swap_grid(spec):
    return pl.BlockSpec(spec.block_shape, lambda j, i, f=spec.index_map: f(i, j))


def _ff_act_fwd(u, conv_w, conv_b, tm=512):
    T = u.shape[0]
    pair, prev, _, half, small = _ff_specs(T, tm)

    def body(u_ref, p_ref, w_ref, b_ref, a_ref):
        i = pl.program_id(0)
        c = _conv(jnp.concatenate([jnp.where(i > 0, p_ref[...], 0.0), u_ref[...]], axis=0), w_ref, b_ref)
        cg, cv = c[:, :FF_TILE], c[:, FF_TILE:]
        a_ref[...] = (cg * jax.nn.sigmoid(cg) * cv).astype(BF16)

    return pl.pallas_call(
        body, name="ff_act_fwd", grid=(T // tm, FF_TILES),
        in_specs=[pair, prev, small(3), small(1)],
        out_specs=half,
        out_shape=jax.ShapeDtypeStruct((T, D_FF), BF16),
        compiler_params=_params(("parallel", "parallel")),
    )(u, u, conv_w, conv_b.reshape(1, -1))


def _ff_act_bwd(u, da, conv_w, conv_b, tm=512):
    T = u.shape[0]
    pair, prev, _, half, small = _ff_specs(T, tm)

    def body(u_ref, p_ref, da_ref, w_ref, b_ref, dc_ref, dw_ref, db_ref):
        i = pl.program_id(1)

        @pl.when(i == 0)
        def _():
            dw_ref[...] = jnp.zeros_like(dw_ref)
            db_ref[...] = jnp.zeros_like(db_ref)

        ext = jnp.concatenate([jnp.where(i > 0, p_ref[...], 0.0), u_ref[...]], axis=0)
        c = _conv(ext, w_ref, b_ref)
        cg, cv = c[:, :FF_TILE], c[:, FF_TILE:]
        da = da_ref[...]
        sg = jax.nn.sigmoid(cg)
        dc = jnp.concatenate([da * cv * sg * (1.0 + cg * (1.0 - sg)), da * cg * sg], axis=1)
        dc_ref[...] = dc
        db_ref[...] += jnp.sum(dc, axis=0, keepdims=True)
        dw_ref[2:3, :] += jnp.sum(dc * ext[CONV_HALO:], axis=0, keepdims=True)
        dw_ref[1:2, :] += jnp.sum(dc * pltpu.roll(ext, 1, 0)[CONV_HALO:], axis=0, keepdims=True)
        dw_ref[0:1, :] += jnp.sum(dc * pltpu.roll(ext, 2, 0)[CONV_HALO:], axis=0, keepdims=True)

    return pl.pallas_call(
        body, name="ff_act_bwd", grid=(FF_TILES, T // tm),
        in_specs=[_swap_grid(pair), _swap_grid(prev), _swap_grid(half), _swap_grid(small(3)), _swap_grid(small(1))],
        out_specs=[_swap_grid(pair), _swap_grid(small(3)), _swap_grid(small(1))],
        out_shape=[jax.ShapeDtypeStruct((T, 2 * D_FF), F32), jax.ShapeDtypeStruct((3, 2 * D_FF), F32),
                   jax.ShapeDtypeStruct((1, 2 * D_FF), F32)],
        compiler_params=_params(("parallel", "arbitrary")),
    )(u, u, da, conv_w, conv_b.reshape(1, -1))


def _ff_conv_bwd(dc, conv_w, tm=512):
    T = dc.shape[0]
    nt = T // tm
    pair, _, nxt, _, small = _ff_specs(T, tm)

    def body(dc_ref, n_ref, w_ref, du_ref):
        i = pl.program_id(0)
        ext = jnp.concatenate([dc_ref[...], jnp.where(i < nt - 1, n_ref[...], 0.0)], axis=0)
        n = tm + CONV_HALO
        du = w_ref[2:3, :] * ext + w_ref[1:2, :] * pltpu.roll(ext, n - 1, 0) + w_ref[0:1, :] * pltpu.roll(ext, n - 2, 0)
        du_ref[...] = du[:tm].astype(BF16)

    return pl.pallas_call(
        body, name="ff_conv_bwd", grid=(nt, FF_TILES),
        in_specs=[pair, nxt, small(3)],
        out_specs=pair,
        out_shape=jax.ShapeDtypeStruct((T, 2 * D_FF), BF16),
        compiler_params=_params(("parallel", "parallel")),
    )(dc, dc, conv_w)


def _loss_head(y, target, tm=512):
    T, C = y.shape
    nt = T // tm

    def body(y_ref, t_ref, dy_ref, l_ref):
        err = y_ref[...] - t_ref[...]
        dy_ref[...] = err * (1.0 / C)
        part = jnp.sum(err * err, axis=0, keepdims=True) * (0.5 / C)
        l_ref[0] = jnp.broadcast_to(part, (8, C))

    spec = pl.BlockSpec((tm, C), lambda i: (i, 0))
    dy, parts = pl.pallas_call(
        body, name="loss_head", grid=(nt,),
        in_specs=[spec, spec],
        out_specs=[spec, pl.BlockSpec((1, 8, C), lambda i: (i, 0, 0))],
        out_shape=[jax.ShapeDtypeStruct((T, C), F32), jax.ShapeDtypeStruct((nt, 8, C), F32)],
        compiler_params=_params(("parallel",)),
    )(y, target)
    return dy, jnp.sum(parts[:, 0, :])


def _adamw_math(w, g, m, v):
    m = ADAM_B1 * m + (1.0 - ADAM_B1) * g
    v = ADAM_B2 * v + (1.0 - ADAM_B2) * (g * g)
    m_hat = m / (1.0 - ADAM_B1 ** ADAM_STEP)
    v_hat = v / (1.0 - ADAM_B2 ** ADAM_STEP)
    delta = -ADAM_LR * (m_hat / (jnp.sqrt(v_hat) + ADAM_EPS) + ADAM_WD * w)
    return delta, m, v


def _adamw(parts, w, m, v, name, tm=256):
    R, C = w.shape
    tm = _pick_rows(R, tm)

    def body(p_ref, w_ref, m_ref, v_ref, g_ref, d_ref, nm_ref, nv_ref):
        g = p_ref[0].astype(F32)
        for s in range(1, N_DEV):
            g = g + p_ref[s].astype(F32)
        delta, nm, nv = _adamw_math(w_ref[...], g, m_ref[...], v_ref[...])
        g_ref[...] = g
        d_ref[...] = delta
        nm_ref[...] = nm
        nv_ref[...] = nv

    spec = pl.BlockSpec((tm, C), lambda i: (i, 0))
    out = jax.ShapeDtypeStruct((R, C), F32)
    return pl.pallas_call(
        body, name=name, grid=(R // tm,),
        in_specs=[pl.BlockSpec((N_DEV, tm, C), lambda i: (0, i, 0)), spec, spec, spec],
        out_specs=[spec] * 4,
        out_shape=[out] * 4,
        compiler_params=_params(("parallel",)),
    )(parts, w, m, v)


def _pick_rows(n, cap):
    if n < 16:
        return n
    best = None
    for t in range(16, min(n, cap) + 1, 16):
        if n % t == 0:
            best = t
    assert best is not None, (n, cap)
    return best


def _exchange(srcs, scatter, name):
    n = len(srcs)

    def body(*refs):
        src_refs, out_refs = refs[:n], refs[n:2 * n]
        send_sems, recv_sems, local_sems = refs[2 * n:]
        x, y, c = lax.axis_index("x"), lax.axis_index("y"), lax.axis_index("c")
        me = 4 * x + 2 * y + c

        def piece(a, d):
            return src_refs[a].at[d] if scatter else src_refs[a]

        local = [pltpu.make_async_copy(piece(a, me), out_refs[a].at[me], local_sems.at[a]) for a in range(n)]
        for cp in local:
            cp.start()
        copies = []
        for k in range(1, N_DEV):
            px = 1 - x if k & 4 else x
            py = 1 - y if k & 2 else y
            pc = 1 - c if k & 1 else c
            peer = 4 * px + 2 * py + pc
            for a in range(n):
                cp = pltpu.make_async_remote_copy(
                    src_ref=piece(a, peer), dst_ref=out_refs[a].at[me],
                    send_sem=send_sems.at[a, k], recv_sem=recv_sems.at[a, k],
                    device_id=(px, py, pc), device_id_type=MESH)
                cp.start()
                copies.append((cp, a, k, peer))
        for cp, a, k, peer in copies:
            cp.wait_send()
            pltpu.make_async_remote_copy(
                src_ref=piece(a, peer), dst_ref=out_refs[a].at[peer],
                send_sem=send_sems.at[a, k], recv_sem=recv_sems.at[a, k],
                device_id=(x, y, c), device_id_type=MESH).wait_recv()
        for cp in local:
            cp.wait()

    slab = lambda s: tuple(s.shape[1:] if scatter else s.shape)
    return pl.pallas_call(
        body, name=name,
        in_specs=[pl.BlockSpec(memory_space=pl.ANY)] * n,
        out_specs=[pl.BlockSpec(memory_space=pl.ANY)] * n,
        out_shape=[jax.ShapeDtypeStruct((N_DEV,) + slab(s), s.dtype) for s in srcs],
        scratch_shapes=[pltpu.SemaphoreType.DMA((n, N_DEV)), pltpu.SemaphoreType.DMA((n, N_DEV)),
                        pltpu.SemaphoreType.DMA((n,))],
    )(*srcs)


SHARDED = ("w_in", "w_branch_a", "w_branch_b", "w_branch_c", "w_out", "w_up", "w_down")
REPLICATED = ("norm_mix", "b_gate", "q_norm_a", "k_norm_a", "rel_bias_a", "w_pool", "pool_scale", "norm_ffn", "conv_b")
WEIGHTS = ("norm_mix", "w_in", "b_gate", "q_norm_a", "k_norm_a", "rel_bias_a", "w_pool", "pool_scale",
           "w_branch_a", "w_branch_b", "w_branch_c", "w_out", "norm_ffn", "w_up", "conv_w", "conv_b", "w_down")
PACK_COLS = 1024
SMALL_COLS = 128
QKV_COLS = 6 * WIDTH


def _rel_index():
    q_off = jnp.arange(CHUNK)[:, None] + N_LEFT * CHUNK
    k_off = jnp.arange(BAND)[None, :]
    return jnp.clip(q_off - k_off, -(CHUNK - 1), MAX_REL) + (CHUNK - 1)


def _rel_onehot():
    rel = _rel_index().reshape(1, CHUNK * BAND)
    return (rel == jnp.arange(REL_TABLE)[:, None]).astype(BF16)


def _select_mm(x, onehot, mode, name):
    hi = x.astype(BF16)
    r1 = x - hi.astype(F32)
    mid = r1.astype(BF16)
    lo = (r1 - mid.astype(F32)).astype(BF16)
    y = _mm(jnp.concatenate([hi, mid, lo, jnp.zeros_like(hi)], axis=0), onehot, mode, F32, name)
    n = x.shape[0]
    return y[:n] + y[n:2 * n] + y[2 * n:3 * n]


def _pack_rows(arrays, cols, row_multiple):
    flat = jnp.concatenate([a.reshape(-1) for a in arrays])
    rows = -(-flat.shape[0] // cols)
    rows = -(-rows // row_multiple) * row_multiple
    return jnp.pad(flat, (0, rows * cols - flat.shape[0])).reshape(rows, cols)


def _unpack_rows(packed, like):
    flat = packed.reshape(-1)
    out, off = [], 0
    for a in like:
        out.append(flat[off:off + a.size].reshape(a.shape))
        off += a.size
    return out


def _gather_columns(g, shape):
    L, rows, cols = shape
    return g.reshape(N_DEV, L, rows, cols).transpose(1, 2, 0, 3).reshape(L, rows, N_DEV * cols)


def _gather_rows(g, shape):
    L, rows, cols = shape
    return g.reshape(N_DEV, L, rows, cols).transpose(1, 0, 2, 3).reshape(L, N_DEV * rows, cols)


def _split_columns(full):
    L, rows, allc = full.shape
    cols = allc // N_DEV
    return full.reshape(L, rows, N_DEV, cols).transpose(2, 0, 1, 3).reshape(N_DEV, L * rows, cols)


def _split_rows(full):
    L, allr, cols = full.shape
    rows = allr // N_DEV
    return full.reshape(L, N_DEV, rows, cols).transpose(1, 0, 2, 3).reshape(N_DEV, L * rows, cols)


def kernel(x, norm_mix, w_in, b_gate, q_norm_a, k_norm_a, rel_bias_a, w_pool, pool_scale, w_branch_a, w_branch_b, w_branch_c, w_out, norm_ffn, w_up, conv_w, conv_b, w_down, loss_target, m_norm_mix, m_w_in, m_b_gate, m_q_norm_a, m_k_norm_a, m_rel_bias_a, m_w_pool, m_pool_scale, m_w_branch_a, m_w_branch_b, m_w_branch_c, m_w_out, m_norm_ffn, m_w_up, m_conv_w, m_conv_b, m_w_down, v_norm_mix, v_w_in, v_b_gate, v_q_norm_a, v_k_norm_a, v_rel_bias_a, v_w_pool, v_pool_scale, v_w_branch_a, v_w_branch_b, v_w_branch_c, v_w_out, v_norm_ffn, v_w_up, v_conv_w, v_conv_b, v_w_down):
    args = dict(locals())
    w = {n: args[n] for n in WEIGHTS}
    m = {n: args["m_" + n] for n in WEIGHTS}
    v = {n: args["v_" + n] for n in WEIGHTS}
    L = w_in.shape[0]
    T = x.shape[1]
    xs = x.reshape(T, D_MODEL)
    target = loss_target.reshape(T, D_MODEL)

    gathered = _exchange([w[n].astype(BF16) for n in SHARDED] + [conv_w], False, "gather_weights")
    part = dict(zip(SHARDED + ("conv_w",), gathered))
    w_in_f = _gather_columns(part["w_in"], w_in.shape)
    w_qkv, w_uc, w_g = w_in_f[:, :, :QKV_COLS], w_in_f[:, :, QKV_COLS:QKV_COLS + WIDTH], w_in_f[:, :, QKV_COLS + WIDTH:]
    w_a = _gather_columns(part["w_branch_a"], w_branch_a.shape)
    w_b = _gather_columns(part["w_branch_b"], w_branch_b.shape)
    w_c = _gather_columns(part["w_branch_c"], w_branch_c.shape)
    w_out_f = _gather_rows(part["w_out"], w_out.shape)
    w_up_f = _ff_pair_order(_gather_columns(part["w_up"], w_up.shape))
    w_down_f = _gather_rows(part["w_down"], w_down.shape)
    conv_w_f = _ff_pair_order(_gather_columns(part["conv_w"], conv_w.shape))
    conv_b_f = _ff_pair_order(conv_b)
    onehot = _rel_onehot()

    saved = []
    cur = xs
    for l in range(L):
        h = _rmsnorm_fwd(cur, norm_mix[l], "norm_mix_fwd")
        qkv = _mm(h, w_qkv[l], "nn", BF16, "proj_qkv")
        uc = _mm(h, w_uc[l], "nn", F32, "proj_pool")
        glog = _mm(h, w_g[l], "nn", F32, "proj_gate")
        table = _band_table(_select_mm(rel_bias_a[l], onehot, "nn", "rel_bias_table").reshape(N_HEADS, CHUNK, BAND))
        oa = _attn_a_fwd(qkv, table, q_norm_a[l], k_norm_a[l])
        ob, carries = _attn_b_fwd(qkv)
        oc = _pool_fwd(uc, w_pool[l], pool_scale[l])
        merged, ya, yb, yc = _merge_fwd(oa, ob, oc, glog, b_gate[l], w_a[l], w_b[l], w_c[l])
        x1 = _mm(merged, w_out_f[l], "nn", F32, "out_proj", res=cur)
        h2 = _rmsnorm_fwd(x1, norm_ffn[l], "norm_ffn_fwd")
        u = _mm(h2, w_up_f[l], "nn", F32, "ff_up")
        act = _ff_act_fwd(u, conv_w_f[l], conv_b_f[l])
        x2 = _mm(act, w_down_f[l], "nn", F32, "ff_down", res=x1)
        saved.append(dict(x=cur, h=h, qkv=qkv, carries=carries, uc=uc, glog=glog, table=table, oa=oa, ob=ob, oc=oc,
                          ya=ya, yb=yb, yc=yc, merged=merged, x1=x1, h2=h2, u=u, act=act))
        cur = x2

    dcur, loss_local = _loss_head(cur, target)
    loss = lax.psum(loss_local, ("x", "y", "c"))

    gw = {n: [None] * L for n in WEIGHTS}
    for l in reversed(range(L)):
        s = saved[l]
        da = _mm(dcur, w_down_f[l], "nt", F32, "ff_down_dx", tn_cap=1408)
        gw["w_down"][l] = _mm(s["act"], dcur, "tn", BF16, "ff_down_dw")
        dc, dconv_w, dconv_b = _ff_act_bwd(s["u"], da, conv_w_f[l], conv_b_f[l])
        du = _ff_conv_bwd(dc, conv_w_f[l])
        dh2 = _mm(du, w_up_f[l], "nt", F32, "ff_up_dx")
        gw["w_up"][l] = _ff_natural_order(_mm(s["h2"], du, "tn", BF16, "ff_up_dw"))
        gw["conv_w"][l] = _ff_natural_order(dconv_w)
        gw["conv_b"][l] = _ff_natural_order(dconv_b)[0]
        dx1, dg = _rmsnorm_bwd(s["x1"], norm_ffn[l], dh2, dcur, "norm_ffn_bwd")
        gw["norm_ffn"][l] = dg[0]

        dmerged = _mm(dx1, w_out_f[l], "nt", F32, "out_proj_dx")
        gw["w_out"][l] = _mm(s["merged"], dx1, "tn", BF16, "out_proj_dw")
        dya, dyb, dyc, dglog, db_gate = _merge_bwd(dmerged, s["glog"], b_gate[l], s["ya"], s["yb"], s["yc"])
        gw["b_gate"][l] = db_gate[0]
        do = {}
        for tag, dy, wk, ok in (("a", dya, w_a, s["oa"]), ("b", dyb, w_b, s["ob"]), ("c", dyc, w_c, s["oc"])):
            do[tag] = _mm(dy, wk[l], "nt", BF16, "branch_dx_" + tag)
            gw["w_branch_" + tag][l] = _mm(ok, dy, "tn", BF16, "branch_dw_" + tag)
        duc, dw_pool, dscale = _pool_bwd(s["uc"], do["c"], w_pool[l], pool_scale[l])
        gw["w_pool"][l] = dw_pool
        gw["pool_scale"][l] = dscale[0]
        dqa, dkc, dkp, dvc, dvp, dtable, dgq = _attn_a_bwd(s["qkv"], do["a"], s["table"], q_norm_a[l], k_norm_a[l])
        dka, dva, dgk = _attn_a_bwd_keys(s["qkv"], dkc, dkp, dvc, dvp, k_norm_a[l])
        gw["q_norm_a"][l] = jnp.sum(dgq.reshape(N_HEADS, HEAD_DIM), axis=0)
        gw["k_norm_a"][l] = jnp.sum(dgk.reshape(N_HEADS, HEAD_DIM), axis=0)
        gw["rel_bias_a"][l] = _select_mm(_band_table_bwd(dtable).reshape(N_HEADS, CHUNK * BAND), onehot, "nt",
                                         "rel_bias_table_dw")
        dqb, dkb, dvb = _attn_b_bwd(s["qkv"], s["carries"], do["b"])
        dqkv = jnp.concatenate([dqa, dka, dva, dqb, dkb.astype(BF16), dvb.astype(BF16)], axis=1)
        dh = _mm(dqkv, w_qkv[l], "nt", F32, "proj_qkv_dx")
        dh = _mm(duc, w_uc[l], "nt", F32, "proj_pool_dx", res=dh)
        dh = _mm(dglog, w_g[l], "nt", F32, "proj_gate_dx", res=dh)
        gw["w_in"][l] = jnp.concatenate([_mm(s["h"], dqkv, "tn", BF16, "proj_qkv_dw"),
                                         _mm(s["h"], duc, "tn", BF16, "proj_pool_dw"),
                                         _mm(s["h"], dglog, "tn", BF16, "proj_gate_dw")], axis=1)
        dcur, dg = _rmsnorm_bwd(s["x"], norm_mix[l], dh, dx1, "norm_mix_bwd")
        gw["norm_mix"][l] = dg[0]
    gw = {n: jnp.stack(g) for n, g in gw.items()}

    row_sharded = ("w_out", "w_down")
    exchanged = SHARDED + ("conv_w",)
    pieces = [(_split_rows if n in row_sharded else _split_columns)(gw[n]) for n in exchanged]
    parts = dict(zip(exchanged, _exchange(pieces, True, "exchange_grads")))
    small = _pack_rows([gw[n] for n in REPLICATED], SMALL_COLS, 16)
    small_parts = _exchange([small], False, "gather_small_grads")[0]

    out = {}
    for n in exchanged:
        flat = lambda a: a.reshape(-1, a.shape[-1])
        res = _adamw(parts[n], flat(w[n]), flat(m[n]), flat(v[n]), "adamw_" + n)
        out[n] = tuple(r.reshape(w[n].shape) for r in res)
    rep_like = [w[n] for n in REPLICATED]
    res = _adamw(small_parts, *[_pack_rows([d[n] for n in REPLICATED], SMALL_COLS, 16) for d in (w, m, v)],
                 "adamw_replicated")
    out.update({n: r for n, r in zip(REPLICATED, zip(*[_unpack_rows(r, rep_like) for r in res]))})

    grads, deltas, new_m, new_v = ([out[n][i] for n in WEIGHTS] for i in range(4))
    return (loss, dcur.reshape(x.shape), *grads, *deltas, *new_m, *new_v)
```

```python
import functools
import math

import jax
import jax.numpy as jnp
from jax import lax
from jax.experimental import pallas as pl
from jax.experimental.pallas import tpu as pltpu

F32 = jnp.float32
BF16 = jnp.bfloat16

N_DEV = 8
D_MODEL = 1024
N_HEADS = 8
HEAD_DIM = 64
CHUNK = 64
N_LEFT = 8
BAND = (N_LEFT + 1) * CHUNK
WIDTH = N_HEADS * HEAD_DIM
POOL_WINDOWS = (2, 4, 8, 16)
POOL_DIM = 128
MAX_REL = 2 * CHUNK
REL_TABLE = MAX_REL + CHUNK
D_FF = 2816
EPS = 1e-6
SB_SCAN = 256
SB_ROWS = 512
SB_KEYS = 512
A_BLOCK = N_LEFT * CHUNK
HALO = 16
LANES = 128
VMEM_LIMIT = 56 * 1024 * 1024

ADAM_LR = 0.001
ADAM_B1 = 0.9
ADAM_B2 = 0.999
ADAM_EPS = 1e-08
ADAM_WD = 0.01
ADAM_STEP = 10

MESH = pl.DeviceIdType.MESH


def _params(sem):
    return pltpu.CompilerParams(dimension_semantics=sem, vmem_limit_bytes=VMEM_LIMIT)


def _pick(n, cap):
    if n <= cap:
        return n
    best = None
    for t in range(LANES, cap + 1, LANES):
        if n % t == 0:
            best = t
    assert best is not None, (n, cap)
    return best


MM_TILE_CAP = 1408


def _mm(a, b, mode, out_dtype, name, tm=MM_TILE_CAP, tn_cap=MM_TILE_CAP, tk_cap=MM_TILE_CAP, res=None):
    if mode == "nn":
        (M, K), (K2, N) = a.shape, b.shape
    elif mode == "nt":
        (M, K), (N, K2) = a.shape, b.shape
    else:
        (K, M), (K2, N) = a.shape, b.shape
    assert K == K2, (a.shape, b.shape, mode)
    tm = _pick(M, tm)
    tn = _pick(N, tn_cap)
    tk = _pick(K, tk_cap)
    nk = K // tk
    if mode == "nn":
        dims = (((1,), (0,)), ((), ()))
        a_spec = pl.BlockSpec((tm, tk), lambda i, j, k: (i, k))
        b_spec = pl.BlockSpec((tk, tn), lambda i, j, k: (k, j))
    elif mode == "nt":
        dims = (((1,), (1,)), ((), ()))
        a_spec = pl.BlockSpec((tm, tk), lambda i, j, k: (i, k))
        b_spec = pl.BlockSpec((tn, tk), lambda i, j, k: (j, k))
    else:
        dims = (((0,), (0,)), ((), ()))
        a_spec = pl.BlockSpec((tk, tm), lambda i, j, k: (k, i))
        b_spec = pl.BlockSpec((tk, tn), lambda i, j, k: (k, j))

    o_spec = pl.BlockSpec((tm, tn), lambda i, j, k: (i, j))

    def body(a_ref, b_ref, *rest):
        res_ref = rest[0] if res is not None else None
        o_ref = rest[1] if res is not None else rest[0]
        part = lax.dot_general(a_ref[...].astype(BF16), b_ref[...].astype(BF16), dims, preferred_element_type=F32)
        if nk == 1:
            o_ref[...] = (part if res is None else part + res_ref[...]).astype(out_dtype)
            return
        acc_ref = rest[-1]
        k = pl.program_id(2)

        @pl.when(k == 0)
        def _():
            acc_ref[...] = part

        @pl.when(k > 0)
        def _():
            acc_ref[...] += part

        @pl.when(k == nk - 1)
        def _():
            total = acc_ref[...] if res is None else acc_ref[...] + res_ref[...]
            o_ref[...] = total.astype(out_dtype)

    return pl.pallas_call(
        body, name=name,
        grid=(M // tm, N // tn, nk),
        in_specs=[a_spec, b_spec] + ([o_spec] if res is not None else []),
        out_specs=o_spec,
        out_shape=jax.ShapeDtypeStruct((M, N), out_dtype),
        scratch_shapes=[pltpu.VMEM((tm, tn), F32)] if nk > 1 else [],
        compiler_params=_params(("parallel", "parallel", "arbitrary")),
    )(*((a, b) if res is None else (a, b, res)))


def _rmsnorm_fwd(x, gain, name, tm=512):
    T, C = x.shape

    def body(x_ref, g_ref, h_ref):
        xv = x_ref[...]
        r = lax.rsqrt(jnp.mean(xv * xv, axis=-1, keepdims=True) + EPS)
        h_ref[...] = (xv * r * g_ref[...]).astype(BF16)

    return pl.pallas_call(
        body, name=name, grid=(T // tm,),
        in_specs=[pl.BlockSpec((tm, C), lambda i: (i, 0)), pl.BlockSpec((1, C), lambda i: (0, 0))],
        out_specs=pl.BlockSpec((tm, C), lambda i: (i, 0)),
        out_shape=jax.ShapeDtypeStruct((T, C), BF16),
        compiler_params=_params(("parallel",)),
    )(x, gain.reshape(1, C))


def _rmsnorm_bwd(x, gain, dh, dres, name, tm=512):
    T, C = x.shape

    def body(x_ref, g_ref, dh_ref, dres_ref, dx_ref, dg_ref):
        @pl.when(pl.program_id(0) == 0)
        def _():
            dg_ref[...] = jnp.zeros_like(dg_ref)

        xv = x_ref[...]
        dy = dh_ref[...].astype(F32)
        r = lax.rsqrt(jnp.mean(xv * xv, axis=-1, keepdims=True) + EPS)
        gdy = dy * g_ref[...]
        inner = jnp.mean(xv * gdy, axis=-1, keepdims=True)
        dx_ref[...] = dres_ref[...] + r * gdy - xv * (r * r * r * inner)
        dg_ref[...] += jnp.sum(dy * xv * r, axis=0, keepdims=True)

    return pl.pallas_call(
        body, name=name, grid=(T // tm,),
        in_specs=[pl.BlockSpec((tm, C), lambda i: (i, 0)), pl.BlockSpec((1, C), lambda i: (0, 0)),
                  pl.BlockSpec((tm, C), lambda i: (i, 0)), pl.BlockSpec((tm, C), lambda i: (i, 0))],
        out_specs=[pl.BlockSpec((tm, C), lambda i: (i, 0)), pl.BlockSpec((1, C), lambda i: (0, 0))],
        out_shape=[jax.ShapeDtypeStruct((T, C), F32), jax.ShapeDtypeStruct((1, C), F32)],
        compiler_params=_params(("arbitrary",)),
    )(x, gain.reshape(1, C), dh, dres)


MASKED = -1e30


def _pair_sum(x, same_head):
    hi = x.astype(BF16)
    lo = (x - hi.astype(F32)).astype(BF16)
    return jnp.dot(hi, same_head, preferred_element_type=F32) + jnp.dot(lo, same_head, preferred_element_type=F32)


def _same_head():
    r = lax.broadcasted_iota(jnp.int32, (2 * HEAD_DIM, 2 * HEAD_DIM), 0)
    c = lax.broadcasted_iota(jnp.int32, (2 * HEAD_DIM, 2 * HEAD_DIM), 1)
    return jnp.where((r < HEAD_DIM) == (c < HEAD_DIM), 1.0, 0.0).astype(BF16)


def _pair_norm(t, g, same_head):
    tf = t.astype(F32)
    r = lax.rsqrt(_pair_sum(tf * tf, same_head) * (1.0 / HEAD_DIM) + EPS)
    return tf * r * g


def _pair_norm_bwd(t, g, dn, same_head):
    tf = t.astype(F32)
    r = lax.rsqrt(_pair_sum(tf * tf, same_head) * (1.0 / HEAD_DIM) + EPS)
    gd = dn * g
    inner = _pair_sum(tf * gd, same_head) * (1.0 / HEAD_DIM)
    return r * gd - tf * (r * r * r * inner), jnp.sum(dn * tf * r, axis=0, keepdims=True)


def _band_table(bias):
    table = jnp.full((N_HEADS, A_BLOCK, 2 * A_BLOCK), MASKED, F32)
    for c in range(N_LEFT):
        table = table.at[:, c * CHUNK:(c + 1) * CHUNK, c * CHUNK:c * CHUNK + BAND].set(bias)
    return table.reshape(N_HEADS // 2, 2 * A_BLOCK, 2 * A_BLOCK)


def _band_table_bwd(dtable):
    dtable = dtable.reshape(N_HEADS, A_BLOCK, 2 * A_BLOCK)
    return sum(dtable[:, c * CHUNK:(c + 1) * CHUNK, c * CHUNK:c * CHUNK + BAND] for c in range(N_LEFT))


def _a_specs(T):
    nb = T // A_BLOCK
    pairs = N_HEADS // 2
    col = lambda which: which * pairs
    cur = lambda which: pl.BlockSpec((A_BLOCK, 2 * HEAD_DIM), lambda p, i: (i, col(which) + p))
    prev = lambda which: pl.BlockSpec((A_BLOCK, 2 * HEAD_DIM), lambda p, i: (jnp.maximum(i - 1, 0), col(which) + p))
    nxt = lambda which: pl.BlockSpec((A_BLOCK, 2 * HEAD_DIM), lambda p, i: (jnp.minimum(i + 1, nb - 1), col(which) + p))
    table = pl.BlockSpec((1, 2 * A_BLOCK, 2 * A_BLOCK), lambda p, i: (p, 0, 0))
    gain = pl.BlockSpec((1, 2 * HEAD_DIM), lambda p, i: (0, 0))
    gacc = pl.BlockSpec((1, 1, 2 * HEAD_DIM), lambda p, i: (p, 0, 0))
    return nb, pairs, cur, prev, nxt, table, gain, gacc


def _a_probs(q_ref, kc_ref, kp_ref, t_ref, gq_ref, gk_ref, same_head, first):
    scale = 1.0 / math.sqrt(HEAD_DIM)
    qst = _stack_heads((_pair_norm(q_ref[...], gq_ref[...], same_head) * scale).astype(BF16))
    kcat = jnp.concatenate([_pair_norm(kp_ref[...], gk_ref[...], same_head).astype(BF16),
                            _pair_norm(kc_ref[...], gk_ref[...], same_head).astype(BF16)], axis=0)
    s = lax.dot_general(qst, kcat, (((1,), (1,)), ((), ())), preferred_element_type=F32) + t_ref[0]
    col = lax.broadcasted_iota(jnp.int32, s.shape, 1)
    s = jnp.where(col >= jnp.where(first, A_BLOCK, 0), s, MASKED)
    e = jnp.exp(s - jnp.max(s, axis=-1, keepdims=True))
    return qst, kcat, e, jnp.sum(e, axis=-1, keepdims=True)


def _attn_a_fwd(qkv, table, gq, gk):
    T = qkv.shape[0]
    nb, pairs, cur, prev, _, tspec, gspec, _ = _a_specs(T)

    def body(q_ref, kc_ref, kp_ref, vc_ref, vp_ref, t_ref, gq_ref, gk_ref, o_ref):
        same_head = _same_head()
        _, _, e, total = _a_probs(q_ref, kc_ref, kp_ref, t_ref, gq_ref, gk_ref, same_head, pl.program_id(1) == 0)
        vcat = jnp.concatenate([vp_ref[...], vc_ref[...]], axis=0)
        p = (e / total).astype(BF16)
        o_ref[...] = _unstack_heads(jnp.dot(p, vcat, preferred_element_type=F32)).astype(BF16)

    return pl.pallas_call(
        body, name="attn_a_fwd", grid=(pairs, nb),
        in_specs=[cur(0), cur(1), prev(1), cur(2), prev(2), tspec, gspec, gspec],
        out_specs=pl.BlockSpec((A_BLOCK, 2 * HEAD_DIM), lambda p, i: (i, p)),
        out_shape=jax.ShapeDtypeStruct((T, WIDTH), BF16),
        compiler_params=_params(("parallel", "arbitrary")),
    )(qkv, qkv, qkv, qkv, qkv, table, jnp.tile(gq.reshape(1, HEAD_DIM), (1, 2)), jnp.tile(gk.reshape(1, HEAD_DIM), (1, 2)))


def _attn_a_bwd(qkv, do, table, gq, gk):
    T = qkv.shape[0]
    nb, pairs, cur, prev, _, tspec, gspec, gacc = _a_specs(T)
    scale = 1.0 / math.sqrt(HEAD_DIM)
    oblk = pl.BlockSpec((A_BLOCK, 2 * HEAD_DIM), lambda p, i: (i, p))

    def body(q_ref, kc_ref, kp_ref, vc_ref, vp_ref, do_ref, t_ref, gq_ref, gk_ref,
             dq_ref, dkc_ref, dkp_ref, dvc_ref, dvp_ref, dt_ref, dgq_ref):
        first = pl.program_id(1) == 0

        @pl.when(first)
        def _():
            dt_ref[...] = jnp.zeros_like(dt_ref)
            dgq_ref[...] = jnp.zeros_like(dgq_ref)

        same_head = _same_head()
        qst, kcat, e, total = _a_probs(q_ref, kc_ref, kp_ref, t_ref, gq_ref, gk_ref, same_head, first)
        vcat = jnp.concatenate([vp_ref[...], vc_ref[...]], axis=0)
        dost = _stack_heads(do_ref[...])
        p = e / total
        dp = lax.dot_general(dost, vcat, (((1,), (1,)), ((), ())), preferred_element_type=F32)
        ds = p * (dp - jnp.sum(p * dp, axis=-1, keepdims=True))
        dt_ref[0] += ds
        dsb = ds.astype(BF16)
        dqn = _unstack_heads(jnp.dot(dsb, kcat, preferred_element_type=F32)) * scale
        dq, dg = _pair_norm_bwd(q_ref[...], gq_ref[...], dqn, same_head)
        dq_ref[...] = dq.astype(BF16)
        dgq_ref[0] += dg
        dk = lax.dot_general(dsb, qst, (((0,), (0,)), ((), ())), preferred_element_type=F32)
        dv = lax.dot_general(p.astype(BF16), dost, (((0,), (0,)), ((), ())), preferred_element_type=F32)
        dkp_ref[...] = dk[:A_BLOCK]
        dkc_ref[...] = dk[A_BLOCK:]
        dvp_ref[...] = dv[:A_BLOCK]
        dvc_ref[...] = dv[A_BLOCK:]

    wide = jax.ShapeDtypeStruct((T, WIDTH), F32)
    return pl.pallas_call(
        body, name="attn_a_bwd", grid=(pairs, nb),
        in_specs=[cur(0), cur(1), prev(1), cur(2), prev(2), oblk, tspec, gspec, gspec],
        out_specs=[oblk, oblk, oblk, oblk, oblk, tspec, gacc],
        out_shape=[jax.ShapeDtypeStruct((T, WIDTH), BF16), wide, wide, wide, wide,
                   jax.ShapeDtypeStruct((pairs, 2 * A_BLOCK, 2 * A_BLOCK), F32),
                   jax.ShapeDtypeStruct((pairs, 1, 2 * HEAD_DIM), F32)],
        compiler_params=_params(("parallel", "arbitrary")),
    )(qkv, qkv, qkv, qkv, qkv, do, table, jnp.tile(gq.reshape(1, HEAD_DIM), (1, 2)),
      jnp.tile(gk.reshape(1, HEAD_DIM), (1, 2)))


def _attn_a_bwd_keys(qkv, dkc, dkp, dvc, dvp, gk):
    T = qkv.shape[0]
    nb, pairs, cur, _, _, _, gspec, gacc = _a_specs(T)
    oblk = pl.BlockSpec((A_BLOCK, 2 * HEAD_DIM), lambda p, i: (i, p))
    onext = pl.BlockSpec((A_BLOCK, 2 * HEAD_DIM), lambda p, i: (jnp.minimum(i + 1, nb - 1), p))

    def body(k_ref, dkc_ref, dkp_ref, dvc_ref, dvp_ref, gk_ref, dk_ref, dv_ref, dgk_ref):
        i = pl.program_id(1)

        @pl.when(i == 0)
        def _():
            dgk_ref[...] = jnp.zeros_like(dgk_ref)

        has_next = (i < nb - 1).astype(F32)
        dkn = dkc_ref[...] + has_next * dkp_ref[...]
        dk, dg = _pair_norm_bwd(k_ref[...], gk_ref[...], dkn, _same_head())
        dk_ref[...] = dk.astype(BF16)
        dv_ref[...] = (dvc_ref[...] + has_next * dvp_ref[...]).astype(BF16)
        dgk_ref[0] += dg

    blk = jax.ShapeDtypeStruct((T, WIDTH), BF16)
    return pl.pallas_call(
        body, name="attn_a_bwd_keys", grid=(pairs, nb),
        in_specs=[cur(1), oblk, onext, oblk, onext, gspec],
        out_specs=[oblk, oblk, gacc],
        out_shape=[blk, blk, jax.ShapeDtypeStruct((pairs, 1, 2 * HEAD_DIM), F32)],
        compiler_params=_params(("parallel", "arbitrary")),
    )(qkv, dkc, dkp, dvc, dvp, jnp.tile(gk.reshape(1, HEAD_DIM), (1, 2)))


def _scan_matrix(later):
    r = lax.broadcasted_iota(jnp.int32, (SB_SCAN, SB_SCAN), 0)
    c = lax.broadcasted_iota(jnp.int32, (SB_SCAN, SB_SCAN), 1)
    return jnp.where((r > c) if later else (r < c), 1.0, 0.0).astype(BF16)


def _running_sums(x, carry, scan, later):
    n = SB_KEYS // SB_SCAN
    parts = [None] * n
    total = carry
    for sb in (reversed(range(n)) if later else range(n)):
        xs = x[:, sb * SB_SCAN:(sb + 1) * SB_SCAN]
        local = jnp.dot(xs.astype(BF16), scan, preferred_element_type=F32)
        parts[sb] = local if total is None else local + total
        rowsum = jnp.sum(xs, axis=-1, keepdims=True)
        total = rowsum if total is None else total + rowsum
    return (parts[0] if n == 1 else jnp.concatenate(parts, axis=1)), total


def _sb_log_sigmoids(z):
    neg_abs = pltpu.bitcast(pltpu.bitcast(z, jnp.uint32) | jnp.uint32(0x80000000), F32)
    take = jnp.minimum(z, 0.0) - jnp.log(1.0 + jnp.exp(neg_abs))
    return take, take - z


def _sb_mask():
    r = lax.broadcasted_iota(jnp.int32, (2 * SB_ROWS, SB_KEYS), 0)
    c = lax.broadcasted_iota(jnp.int32, (2 * SB_ROWS, SB_KEYS), 1)
    return c < jnp.where(r >= SB_ROWS, r - SB_ROWS, r)


def _stack_heads(t):
    lane = lax.broadcasted_iota(jnp.int32, t.shape, 1)
    zero = jnp.zeros_like(t)
    return jnp.concatenate([jnp.where(lane < HEAD_DIM, t, zero), jnp.where(lane >= HEAD_DIM, t, zero)], axis=0)


def _unstack_heads(t):
    rows = t.shape[0] // 2
    lane = lax.broadcasted_iota(jnp.int32, (rows, 2 * HEAD_DIM), 1)
    return jnp.where(lane < HEAD_DIM, t[:rows], t[rows:])


def _sb_specs(T):
    nq = T // SB_ROWS
    blk = lambda col: pl.BlockSpec((SB_ROWS, 2 * HEAD_DIM), lambda p, i: (i, col + p))
    full = lambda col: pl.BlockSpec((T, 2 * HEAD_DIM), lambda p, i: (0, col + p))
    return nq, blk, full


def _key_rows(j):
    return pl.ds(pl.multiple_of(j * SB_KEYS, SB_KEYS), SB_KEYS)


def _attn_b_fwd(qkv):
    T = qkv.shape[0]
    pairs = N_HEADS // 2
    nq, blk, full = _sb_specs(T)
    scale = 1.0 / math.sqrt(HEAD_DIM)

    assert nq <= LANES

    def body(q_ref, k_ref, v_ref, o_ref, c_ref, acc_ref, carry_ref, z_ref, w_ref):
        i = pl.program_id(1)
        scan = _scan_matrix(True)
        qst = _stack_heads((q_ref[...].astype(F32) * scale).astype(BF16))
        lane = lax.broadcasted_iota(jnp.int32, (2 * SB_ROWS, LANES), 1)

        def scores(j):
            return lax.dot_general(qst, k_ref[_key_rows(j), :], (((1,), (1,)), ((), ())), preferred_element_type=F32)

        def weights(z, carry, mask):
            take, keep = _sb_log_sigmoids(z)
            if mask is not None:
                keep = jnp.where(mask, keep, 0.0)
            tail, total = _running_sums(keep, carry, scan, True)
            w = jnp.exp(take + tail)
            if mask is not None:
                w = jnp.where(mask, w, 0.0)
            return w.astype(BF16), total

        w_ref[...], carry_ref[...] = weights(scores(i), None, _sb_mask())
        z_ref[...] = scores(jnp.maximum(i - 1, 0))
        acc_ref[...] = jnp.zeros_like(acc_ref)
        c_ref[0, 0] = jnp.zeros((2 * SB_ROWS, LANES), F32)

        @pl.loop(0, i)
        def _(jj):
            j = i - 1 - jj
            z = z_ref[...]
            z_ref[...] = scores(jnp.maximum(j - 1, 0))
            acc_ref[...] += jnp.dot(w_ref[...], v_ref[_key_rows(j + 1), :], preferred_element_type=F32)
            carry = carry_ref[...]
            c_ref[0, 0] = jnp.where(lane == j, carry, c_ref[0, 0])
            w_ref[...], carry_ref[...] = weights(z, carry, None)

        acc = acc_ref[...] + jnp.dot(w_ref[...], v_ref[_key_rows(0), :], preferred_element_type=F32)
        o_ref[...] = _unstack_heads(acc).astype(BF16)

    return pl.pallas_call(
        body, name="attn_b_fwd", grid=(pairs, nq),
        in_specs=[blk(3 * pairs), full(4 * pairs), full(5 * pairs)],
        out_specs=[pl.BlockSpec((SB_ROWS, 2 * HEAD_DIM), lambda p, i: (i, p)),
                   pl.BlockSpec((1, 1, 2 * SB_ROWS, LANES), lambda p, i: (p, i, 0, 0))],
        out_shape=[jax.ShapeDtypeStruct((T, WIDTH), BF16), jax.ShapeDtypeStruct((pairs, nq, 2 * SB_ROWS, LANES), F32)],
        scratch_shapes=[pltpu.VMEM((2 * SB_ROWS, 2 * HEAD_DIM), F32), pltpu.VMEM((2 * SB_ROWS, 1), F32),
                        pltpu.VMEM((2 * SB_ROWS, SB_KEYS), F32), pltpu.VMEM((2 * SB_ROWS, SB_KEYS), BF16)],
        compiler_params=_params(("parallel", "arbitrary")),
    )(qkv, qkv, qkv)


def _attn_b_bwd(qkv, carries, do):
    T = qkv.shape[0]
    pairs = N_HEADS // 2
    nq, blk, full = _sb_specs(T)
    scale = 1.0 / math.sqrt(HEAD_DIM)
    oblk = pl.BlockSpec((SB_ROWS, 2 * HEAD_DIM), lambda p, i: (i, p))
    ofull = pl.BlockSpec((T, 2 * HEAD_DIM), lambda p, i: (0, p))

    def body(q_ref, k_ref, v_ref, c_ref, do_ref, dq_ref, dk_ref, dv_ref, dqacc_ref, before_ref,
             z_ref, dw_ref, dz_ref, w_ref):
        i = pl.program_id(1)

        @pl.when(i == 0)
        def _():
            dk_ref[...] = jnp.zeros_like(dk_ref)
            dv_ref[...] = jnp.zeros_like(dv_ref)

        scan_later = _scan_matrix(True)
        scan_earlier = _scan_matrix(False)
        qst = _stack_heads((q_ref[...].astype(F32) * scale).astype(BF16))
        dost = _stack_heads(do_ref[...].astype(BF16))
        lane = lax.broadcasted_iota(jnp.int32, (2 * SB_ROWS, LANES), 1)
        nt = (((1,), (1,)), ((), ()))
        tn = (((0,), (0,)), ((), ()))

        def products(j):
            return (lax.dot_general(qst, k_ref[_key_rows(j), :], nt, preferred_element_type=F32),
                    lax.dot_general(dost, v_ref[_key_rows(j), :], nt, preferred_element_type=F32))

        def score_grads(z, dw, later, mask):
            take, keep = _sb_log_sigmoids(z)
            sig = jnp.exp(take)
            if mask is not None:
                keep = jnp.where(mask, keep, 0.0)
            tail, _ = _running_sums(keep, later, scan_later, True)
            w = jnp.exp(take + tail)
            if mask is not None:
                w = jnp.where(mask, w, 0.0)
            g = w * dw
            before, before_ref[...] = _running_sums(g, before_ref[...], scan_earlier, False)
            dz = g - sig * (g + before)
            if mask is not None:
                dz = jnp.where(mask, dz, 0.0)
            return dz.astype(BF16), w.astype(BF16)

        def accumulate(j, dzb, wb):
            dqacc_ref[...] += jnp.dot(dzb, k_ref[_key_rows(j), :], preferred_element_type=F32)
            dk_ref[_key_rows(j), :] += lax.dot_general(dzb, qst, tn, preferred_element_type=F32)
            dv_ref[_key_rows(j), :] += lax.dot_general(wb, dost, tn, preferred_element_type=F32)

        dqacc_ref[...] = jnp.zeros_like(dqacc_ref)
        before_ref[...] = jnp.zeros_like(before_ref)
        dz_ref[...] = jnp.zeros_like(dz_ref)
        w_ref[...] = jnp.zeros_like(w_ref)
        z_ref[...], dw_ref[...] = products(0)

        @pl.loop(0, i)
        def _(j):
            z, dw = z_ref[...], dw_ref[...]
            z_ref[...], dw_ref[...] = products(j + 1)
            accumulate(jnp.maximum(j - 1, 0), dz_ref[...], w_ref[...])
            later = jnp.sum(jnp.where(lane == j, c_ref[0, 0], 0.0), axis=-1, keepdims=True)
            dz_ref[...], w_ref[...] = score_grads(z, dw, later, None)

        accumulate(jnp.maximum(i - 1, 0), dz_ref[...], w_ref[...])
        accumulate(i, *score_grads(z_ref[...], dw_ref[...], None, _sb_mask()))
        dq_ref[...] = (_unstack_heads(dqacc_ref[...]) * scale).astype(BF16)

    wide = jax.ShapeDtypeStruct((T, WIDTH), F32)
    return pl.pallas_call(
        body, name="attn_b_bwd", grid=(pairs, nq),
        in_specs=[blk(3 * pairs), full(4 * pairs), full(5 * pairs),
                  pl.BlockSpec((1, 1, 2 * SB_ROWS, LANES), lambda p, i: (p, i, 0, 0)), oblk],
        out_specs=[oblk, ofull, ofull],
        out_shape=[jax.ShapeDtypeStruct((T, WIDTH), BF16), wide, wide],
        scratch_shapes=[pltpu.VMEM((2 * SB_ROWS, 2 * HEAD_DIM), F32), pltpu.VMEM((2 * SB_ROWS, 1), F32),
                        pltpu.VMEM((2 * SB_ROWS, SB_KEYS), F32), pltpu.VMEM((2 * SB_ROWS, SB_KEYS), F32),
                        pltpu.VMEM((2 * SB_ROWS, SB_KEYS), BF16), pltpu.VMEM((2 * SB_ROWS, SB_KEYS), BF16)],
        compiler_params=_params(("parallel", "arbitrary")),
    )(qkv, qkv, qkv, carries, do)


def _window_sums(ext, forward):
    n = ext.shape[0]
    out = []
    s = ext
    for step in (1, 2, 4, 8):
        s = s + pltpu.roll(s, (n - step) if forward else step, 0)
        out.append(s)
    return out


def _pool_counts(base, rows, win):
    t = base + lax.broadcasted_iota(jnp.int32, (rows, 1), 0)
    return jnp.minimum(t + 1, win).astype(F32)


def _pooled(u_ref, up_ref, i, tm):
    prev = jnp.where(i > 0, up_ref[...], 0.0)
    ext = jnp.concatenate([prev, u_ref[...]], axis=0)
    sums = _window_sums(ext, False)
    parts = []
    for g, win in enumerate(POOL_WINDOWS):
        cols = slice(g * POOL_DIM, (g + 1) * POOL_DIM)
        cnt = _pool_counts(i * tm, tm, win)
        parts.append(sums[g][HALO:, cols] / cnt - ext[HALO:, cols])
    return parts


def _pool_fwd(ucg, w_pool, scale, tm=512):
    T = ucg.shape[0]
    C = WIDTH

    def body(u_ref, up_ref, w_ref, s_ref, o_ref):
        i = pl.program_id(0)
        parts = _pooled(u_ref, up_ref, i, tm)
        for g in range(len(POOL_WINDOWS)):
            mixed = jnp.dot(parts[g].astype(BF16), w_ref[g], preferred_element_type=F32)
            o_ref[:, g * POOL_DIM:(g + 1) * POOL_DIM] = (mixed * s_ref[:, g * POOL_DIM:(g + 1) * POOL_DIM]).astype(BF16)

    return pl.pallas_call(
        body, name="pool_fwd", grid=(T // tm,),
        in_specs=[pl.BlockSpec((tm, C), lambda i: (i, 0)),
                  pl.BlockSpec((HALO, C), lambda i: (jnp.maximum(i * (tm // HALO) - 1, 0), 0)),
                  pl.BlockSpec((len(POOL_WINDOWS), POOL_DIM, POOL_DIM), lambda i: (0, 0, 0)),
                  pl.BlockSpec((1, C), lambda i: (0, 0))],
        out_specs=pl.BlockSpec((tm, C), lambda i: (i, 0)),
        out_shape=jax.ShapeDtypeStruct((T, C), BF16),
        compiler_params=_params(("parallel",)),
    )(ucg, ucg, w_pool.astype(BF16), scale.reshape(1, C))


def _pool_bwd(ucg, do_c, w_pool, scale, tm=512):
    T = ucg.shape[0]
    C = WIDTH
    nt = T // tm
    G = len(POOL_WINDOWS)

    def body(u_ref, up_ref, do_ref, don_ref, w_ref, s_ref, du_ref, dw_ref, ds_ref):
        i = pl.program_id(0)

        @pl.when(i == 0)
        def _():
            dw_ref[...] = jnp.zeros_like(dw_ref)
            ds_ref[...] = jnp.zeros_like(ds_ref)

        parts = _pooled(u_ref, up_ref, i, tm)
        nxt = jnp.where(i < nt - 1, don_ref[...].astype(F32), 0.0)
        do_ext = jnp.concatenate([do_ref[...].astype(F32), nxt], axis=0) * s_ref[...]
        for g, win in enumerate(POOL_WINDOWS):
            cols = slice(g * POOL_DIM, (g + 1) * POOL_DIM)
            pooled_b = parts[g].astype(BF16)
            dmix = do_ext[:, cols].astype(BF16)
            mixed = jnp.dot(pooled_b, w_ref[g], preferred_element_type=F32)
            ds_ref[:, cols] += jnp.sum(do_ref[:, cols].astype(F32) * mixed, axis=0, keepdims=True)
            dw_ref[g] += lax.dot_general(pooled_b, dmix[:tm], (((0,), (0,)), ((), ())), preferred_element_type=F32)
            dpool = lax.dot_general(dmix, w_ref[g], (((1,), (1,)), ((), ())), preferred_element_type=F32)
            scaled = dpool / _pool_counts(i * tm, tm + HALO, win)
            fwd = _window_sums(scaled, True)[g]
            du_ref[:, cols] = (fwd[:tm] - dpool[:tm]).astype(BF16)

    return pl.pallas_call(
        body, name="pool_bwd", grid=(nt,),
        in_specs=[pl.BlockSpec((tm, C), lambda i: (i, 0)),
                  pl.BlockSpec((HALO, C), lambda i: (jnp.maximum(i * (tm // HALO) - 1, 0), 0)),
                  pl.BlockSpec((tm, C), lambda i: (i, 0)),
                  pl.BlockSpec((HALO, C), lambda i: (jnp.minimum((i + 1) * (tm // HALO), T // HALO - 1), 0)),
                  pl.BlockSpec((G, POOL_DIM, POOL_DIM), lambda i: (0, 0, 0)),
                  pl.BlockSpec((1, C), lambda i: (0, 0))],
        out_specs=[pl.BlockSpec((tm, C), lambda i: (i, 0)),
                   pl.BlockSpec((G, POOL_DIM, POOL_DIM), lambda i: (0, 0, 0)),
                   pl.BlockSpec((1, C), lambda i: (0, 0))],
        out_shape=[jax.ShapeDtypeStruct((T, C), BF16), jax.ShapeDtypeStruct((G, POOL_DIM, POOL_DIM), F32),
                   jax.ShapeDtypeStruct((1, C), F32)],
        compiler_params=_params(("arbitrary",)),
    )(ucg, ucg, do_c, do_c, w_pool.astype(BF16), scale.reshape(1, C))


def _merge_fwd(oa, ob, oc, glog, b_gate, wa, wb, wc, tm=256):
    T = oa.shape[0]
    Dm = D_MODEL
    row = lambda c: pl.BlockSpec((tm, c), lambda i: (i, 0))
    wspec = pl.BlockSpec((WIDTH, Dm), lambda i: (0, 0))

    def body(oa_ref, ob_ref, oc_ref, g_ref, b_ref, wa_ref, wb_ref, wc_ref, m_ref, ya_ref, yb_ref, yc_ref):
        merged = jnp.zeros((tm, Dm), F32)
        for kk, (o_ref, w_ref, y_ref) in enumerate(((oa_ref, wa_ref, ya_ref), (ob_ref, wb_ref, yb_ref),
                                                    (oc_ref, wc_ref, yc_ref))):
            y = jnp.dot(o_ref[...].astype(BF16), w_ref[...], preferred_element_type=F32)
            gate = jax.nn.sigmoid(g_ref[:, kk * Dm:(kk + 1) * Dm] + b_ref[:, kk * Dm:(kk + 1) * Dm])
            merged = merged + gate * y
            y_ref[...] = y.astype(BF16)
        m_ref[...] = merged.astype(BF16)

    out = jax.ShapeDtypeStruct((T, Dm), BF16)
    return pl.pallas_call(
        body, name="merge_fwd", grid=(T // tm,),
        in_specs=[row(WIDTH), row(WIDTH), row(WIDTH), row(3 * Dm), pl.BlockSpec((1, 3 * Dm), lambda i: (0, 0)),
                  wspec, wspec, wspec],
        out_specs=[row(Dm)] * 4,
        out_shape=[out] * 4,
        compiler_params=_params(("parallel",)),
    )(oa, ob, oc, glog, b_gate.reshape(1, 3 * Dm), wa, wb, wc)


def _merge_bwd(dmerged, glog, b_gate, ya, yb, yc, tm=256):
    T = dmerged.shape[0]
    Dm = D_MODEL
    row = lambda c: pl.BlockSpec((tm, c), lambda i: (i, 0))

    def body(dm_ref, g_ref, b_ref, ya_ref, yb_ref, yc_ref, dya_ref, dyb_ref, dyc_ref, dg_ref, db_ref):
        @pl.when(pl.program_id(0) == 0)
        def _():
            db_ref[...] = jnp.zeros_like(db_ref)

        dm = dm_ref[...]
        for kk, (y_ref, dy_ref) in enumerate(((ya_ref, dya_ref), (yb_ref, dyb_ref), (yc_ref, dyc_ref))):
            cols = slice(kk * Dm, (kk + 1) * Dm)
            gate = jax.nn.sigmoid(g_ref[:, cols] + b_ref[:, cols])
            dy_ref[...] = (dm * gate).astype(BF16)
            dlog = dm * y_ref[...].astype(F32) * gate * (1.0 - gate)
            dg_ref[:, cols] = dlog.astype(BF16)
            db_ref[:, cols] += jnp.sum(dlog, axis=0, keepdims=True)

    out = jax.ShapeDtypeStruct((T, Dm), BF16)
    return pl.pallas_call(
        body, name="merge_bwd", grid=(T // tm,),
        in_specs=[row(Dm), row(3 * Dm), pl.BlockSpec((1, 3 * Dm), lambda i: (0, 0)), row(Dm), row(Dm), row(Dm)],
        out_specs=[row(Dm), row(Dm), row(Dm), row(3 * Dm), pl.BlockSpec((1, 3 * Dm), lambda i: (0, 0))],
        out_shape=[out, out, out, jax.ShapeDtypeStruct((T, 3 * Dm), BF16), jax.ShapeDtypeStruct((1, 3 * Dm), F32)],
        compiler_params=_params(("arbitrary",)),
    )(dmerged, glog, b_gate.reshape(1, 3 * Dm), ya, yb, yc)


def _residual_add(x, y, name, tm=512):
    T, C = x.shape

    def body(x_ref, y_ref, o_ref):
        o_ref[...] = x_ref[...] + y_ref[...]

    spec = pl.BlockSpec((tm, C), lambda i: (i, 0))
    return pl.pallas_call(body, name=name, grid=(T // tm,), in_specs=[spec, spec], out_specs=spec,
                          out_shape=jax.ShapeDtypeStruct((T, C), F32), compiler_params=_params(("parallel",)))(x, y)


FF_TILE = 256
FF_TILES = D_FF // FF_TILE
CONV_HALO = 8


def _ff_pair_order(w):
    lead = w.shape[:-1]
    n = len(lead)
    w = w.reshape(*lead, 2, FF_TILES, FF_TILE)
    return jnp.swapaxes(w, n, n + 1).reshape(*lead, 2 * D_FF)


def _ff_natural_order(w):
    lead = w.shape[:-1]
    n = len(lead)
    w = w.reshape(*lead, FF_TILES, 2, FF_TILE)
    return jnp.swapaxes(w, n, n + 1).reshape(*lead, 2 * D_FF)


def _conv(ext, w_ref, b_ref):
    c = b_ref[...] + w_ref[2:3, :] * ext
    c = c + w_ref[1:2, :] * pltpu.roll(ext, 1, 0)
    c = c + w_ref[0:1, :] * pltpu.roll(ext, 2, 0)
    return c[CONV_HALO:]


def _ff_specs(T, tm):
    pair = pl.BlockSpec((tm, 2 * FF_TILE), lambda i, j: (i, j))
    prev = pl.BlockSpec((CONV_HALO, 2 * FF_TILE), lambda i, j: (jnp.maximum(i * (tm // CONV_HALO) - 1, 0), j))
    nxt = pl.BlockSpec((CONV_HALO, 2 * FF_TILE),
                       lambda i, j: (jnp.minimum((i + 1) * (tm // CONV_HALO), T // CONV_HALO - 1), j))
    half = pl.BlockSpec((tm, FF_TILE), lambda i, j: (i, j))
    small = lambda r: pl.BlockSpec((r, 2 * FF_TILE), lambda i, j: (0, j))
    return pair, prev, nxt, half, small


def _swap_grid(spec):
    return pl.BlockSpec(spec.block_shape, lambda j, i, f=spec.index_map: f(i, j))


def _ff_act_fwd(u, conv_w, conv_b, tm=512):
    T = u.shape[0]
    pair, prev, _, half, small = _ff_specs(T, tm)

    def body(u_ref, p_ref, w_ref, b_ref, a_ref):
        i = pl.program_id(0)
        c = _conv(jnp.concatenate([jnp.where(i > 0, p_ref[...], 0.0), u_ref[...]], axis=0), w_ref, b_ref)
        cg, cv = c[:, :FF_TILE], c[:, FF_TILE:]
        a_ref[...] = (cg * jax.nn.sigmoid(cg) * cv).astype(BF16)

    return pl.pallas_call(
        body, name="ff_act_fwd", grid=(T // tm, FF_TILES),
        in_specs=[pair, prev, small(3), small(1)],
        out_specs=half,
        out_shape=jax.ShapeDtypeStruct((T, D_FF), BF16),
        compiler_params=_params(("parallel", "parallel")),
    )(u, u, conv_w, conv_b.reshape(1, -1))


def _ff_act_bwd(u, da, conv_w, conv_b, tm=512):
    T = u.shape[0]
    pair, prev, _, half, small = _ff_specs(T, tm)

    def body(u_ref, p_ref, da_ref, w_ref, b_ref, dc_ref, dw_ref, db_ref):
        i = pl.program_id(1)

        @pl.when(i == 0)
        def _():
            dw_ref[...] = jnp.zeros_like(dw_ref)
            db_ref[...] = jnp.zeros_like(db_ref)

        ext = jnp.concatenate([jnp.where(i > 0, p_ref[...], 0.0), u_ref[...]], axis=0)
        c = _conv(ext, w_ref, b_ref)
        cg, cv = c[:, :FF_TILE], c[:, FF_TILE:]
        da = da_ref[...]
        sg = jax.nn.sigmoid(cg)
        dc = jnp.concatenate([da * cv * sg * (1.0 + cg * (1.0 - sg)), da * cg * sg], axis=1)
        dc_ref[...] = dc
        db_ref[...] += jnp.sum(dc, axis=0, keepdims=True)
        dw_ref[2:3, :] += jnp.sum(dc * ext[CONV_HALO:], axis=0, keepdims=True)
        dw_ref[1:2, :] += jnp.sum(dc * pltpu.roll(ext, 1, 0)[CONV_HALO:], axis=0, keepdims=True)
        dw_ref[0:1, :] += jnp.sum(dc * pltpu.roll(ext, 2, 0)[CONV_HALO:], axis=0, keepdims=True)

    return pl.pallas_call(
        body, name="ff_act_bwd", grid=(FF_TILES, T // tm),
        in_specs=[_swap_grid(pair), _swap_grid(prev), _swap_grid(half), _swap_grid(small(3)), _swap_grid(small(1))],
        out_specs=[_swap_grid(pair), _swap_grid(small(3)), _swap_grid(small(1))],
        out_shape=[jax.ShapeDtypeStruct((T, 2 * D_FF), F32), jax.ShapeDtypeStruct((3, 2 * D_FF), F32),
                   jax.ShapeDtypeStruct((1, 2 * D_FF), F32)],
        compiler_params=_params(("parallel", "arbitrary")),
    )(u, u, da, conv_w, conv_b.reshape(1, -1))


def _ff_conv_bwd(dc, conv_w, tm=512):
    T = dc.shape[0]
    nt = T // tm
    pair, _, nxt, _, small = _ff_specs(T, tm)

    def body(dc_ref, n_ref, w_ref, du_ref):
        i = pl.program_id(0)
        ext = jnp.concatenate([dc_ref[...], jnp.where(i < nt - 1, n_ref[...], 0.0)], axis=0)
        n = tm + CONV_HALO
        du = w_ref[2:3, :] * ext + w_ref[1:2, :] * pltpu.roll(ext, n - 1, 0) + w_ref[0:1, :] * pltpu.roll(ext, n - 2, 0)
        du_ref[...] = du[:tm].astype(BF16)

    return pl.pallas_call(
        body, name="ff_conv_bwd", grid=(nt, FF_TILES),
        in_specs=[pair, nxt, small(3)],
        out_specs=pair,
        out_shape=jax.ShapeDtypeStruct((T, 2 * D_FF), BF16),
        compiler_params=_params(("parallel", "parallel")),
    )(dc, dc, conv_w)


def _loss_head(y, target, tm=512):
    T, C = y.shape
    nt = T // tm

    def body(y_ref, t_ref, dy_ref, l_ref):
        err = y_ref[...] - t_ref[...]
        dy_ref[...] = err * (1.0 / C)
        part = jnp.sum(err * err, axis=0, keepdims=True) * (0.5 / C)
        l_ref[0] = jnp.broadcast_to(part, (8, C))

    spec = pl.BlockSpec((tm, C), lambda i: (i, 0))
    dy, parts = pl.pallas_call(
        body, name="loss_head", grid=(nt,),
        in_specs=[spec, spec],
        out_specs=[spec, pl.BlockSpec((1, 8, C), lambda i: (i, 0, 0))],
        out_shape=[jax.ShapeDtypeStruct((T, C), F32), jax.ShapeDtypeStruct((nt, 8, C), F32)],
        compiler_params=_params(("parallel",)),
    )(y, target)
    return dy, jnp.sum(parts[:, 0, :])


def _adamw_math(w, g, m, v):
    m = ADAM_B1 * m + (1.0 - ADAM_B1) * g
    v = ADAM_B2 * v + (1.0 - ADAM_B2) * (g * g)
    m_hat = m / (1.0 - ADAM_B1 ** ADAM_STEP)
    v_hat = v / (1.0 - ADAM_B2 ** ADAM_STEP)
    delta = -ADAM_LR * (m_hat / (jnp.sqrt(v_hat) + ADAM_EPS) + ADAM_WD * w)
    return delta, m, v


def _adamw(parts, w, m, v, name, tm=256):
    R, C = w.shape
    tm = _pick_rows(R, tm)

    def body(p_ref, w_ref, m_ref, v_ref, g_ref, d_ref, nm_ref, nv_ref):
        g = p_ref[0].astype(F32)
        for s in range(1, N_DEV):
            g = g + p_ref[s].astype(F32)
        delta, nm, nv = _adamw_math(w_ref[...], g, m_ref[...], v_ref[...])
        g_ref[...] = g
        d_ref[...] = delta
        nm_ref[...] = nm
        nv_ref[...] = nv

    spec = pl.BlockSpec((tm, C), lambda i: (i, 0))
    out = jax.ShapeDtypeStruct((R, C), F32)
    return pl.pallas_call(
        body, name=name, grid=(R // tm,),
        in_specs=[pl.BlockSpec((N_DEV, tm, C), lambda i: (0, i, 0)), spec, spec, spec],
        out_specs=[spec] * 4,
        out_shape=[out] * 4,
        compiler_params=_params(("parallel",)),
    )(parts, w, m, v)


def _pick_rows(n, cap):
    if n < 16:
        return n
    best = None
    for t in range(16, min(n, cap) + 1, 16):
        if n % t == 0:
            best = t
    assert best is not None, (n, cap)
    return best


def _exchange(srcs, scatter, name):
    n = len(srcs)

    def body(*refs):
        src_refs, out_refs = refs[:n], refs[n:2 * n]
        send_sems, recv_sems, local_sems = refs[2 * n:]
        x, y, c = lax.axis_index("x"), lax.axis_index("y"), lax.axis_index("c")
        me = 4 * x + 2 * y + c

        def piece(a, d):
            return src_refs[a].at[d] if scatter else src_refs[a]

        local = [pltpu.make_async_copy(piece(a, me), out_refs[a].at[me], local_sems.at[a]) for a in range(n)]
        for cp in local:
            cp.start()
        copies = []
        for k in range(1, N_DEV):
            px = 1 - x if k & 4 else x
            py = 1 - y if k & 2 else y
            pc = 1 - c if k & 1 else c
            peer = 4 * px + 2 * py + pc
            for a in range(n):
                cp = pltpu.make_async_remote_copy(
                    src_ref=piece(a, peer), dst_ref=out_refs[a].at[me],
                    send_sem=send_sems.at[a, k], recv_sem=recv_sems.at[a, k],
                    device_id=(px, py, pc), device_id_type=MESH)
                cp.start()
                copies.append((cp, a, k, peer))
        for cp, a, k, peer in copies:
            cp.wait_send()
            pltpu.make_async_remote_copy(
                src_ref=piece(a, peer), dst_ref=out_refs[a].at[peer],
                send_sem=send_sems.at[a, k], recv_sem=recv_sems.at[a, k],
                device_id=(x, y, c), device_id_type=MESH).wait_recv()
        for cp in local:
            cp.wait()

    slab = lambda s: tuple(s.shape[1:] if scatter else s.shape)
    return pl.pallas_call(
        body, name=name,
        in_specs=[pl.BlockSpec(memory_space=pl.ANY)] * n,
        out_specs=[pl.BlockSpec(memory_space=pl.ANY)] * n,
        out_shape=[jax.ShapeDtypeStruct((N_DEV,) + slab(s), s.dtype) for s in srcs],
        scratch_shapes=[pltpu.SemaphoreType.DMA((n, N_DEV)), pltpu.SemaphoreType.DMA((n, N_DEV)),
                        pltpu.SemaphoreType.DMA((n,))],
    )(*srcs)


SHARDED = ("w_in", "w_branch_a", "w_branch_b", "w_branch_c", "w_out", "w_up", "w_down")
REPLICATED = ("norm_mix", "b_gate", "q_norm_a", "k_norm_a", "rel_bias_a", "w_pool", "pool_scale", "norm_ffn", "conv_b")
WEIGHTS = ("norm_mix", "w_in", "b_gate", "q_norm_a", "k_norm_a", "rel_bias_a", "w_pool", "pool_scale",
           "w_branch_a", "w_branch_b", "w_branch_c", "w_out", "norm_ffn", "w_up", "conv_w", "conv_b", "w_down")
SMALL_COLS = 128
QKV_COLS = 6 * WIDTH


def _rel_index():
    q_off = jnp.arange(CHUNK)[:, None] + N_LEFT * CHUNK
    k_off = jnp.arange(BAND)[None, :]
    return jnp.clip(q_off - k_off, -(CHUNK - 1), MAX_REL) + (CHUNK - 1)


def _rel_onehot():
    rel = _rel_index().reshape(1, CHUNK * BAND)
    return (rel == jnp.arange(REL_TABLE)[:, None]).astype(BF16)


def _select_mm(x, onehot, mode, name):
    hi = x.astype(BF16)
    r1 = x - hi.astype(F32)
    mid = r1.astype(BF16)
    lo = (r1 - mid.astype(F32)).astype(BF16)
    y = _mm(jnp.concatenate([hi, mid, lo, jnp.zeros_like(hi)], axis=0), onehot, mode, F32, name)
    n = x.shape[0]
    return y[:n] + y[n:2 * n] + y[2 * n:3 * n]


def _pack_rows(arrays, cols, row_multiple):
    flat = jnp.concatenate([a.reshape(-1) for a in arrays])
    rows = -(-flat.shape[0] // cols)
    rows = -(-rows // row_multiple) * row_multiple
    return jnp.pad(flat, (0, rows * cols - flat.shape[0])).reshape(rows, cols)


def _unpack_rows(packed, like):
    flat = packed.reshape(-1)
    out, off = [], 0
    for a in like:
        out.append(flat[off:off + a.size].reshape(a.shape))
        off += a.size
    return out


def _gather_columns(g, shape):
    L, rows, cols = shape
    return g.reshape(N_DEV, L, rows, cols).transpose(1, 2, 0, 3).reshape(L, rows, N_DEV * cols)


def _gather_rows(g, shape):
    L, rows, cols = shape
    return g.reshape(N_DEV, L, rows, cols).transpose(1, 0, 2, 3).reshape(L, N_DEV * rows, cols)


def _split_columns(full):
    L, rows, allc = full.shape
    cols = allc // N_DEV
    return full.reshape(L, rows, N_DEV, cols).transpose(2, 0, 1, 3).reshape(N_DEV, L * rows, cols)


def _split_rows(full):
    L, allr, cols = full.shape
    rows = allr // N_DEV
    return full.reshape(L, N_DEV, rows, cols).transpose(1, 0, 2, 3).reshape(N_DEV, L * rows, cols)


def kernel(x, norm_mix, w_in, b_gate, q_norm_a, k_norm_a, rel_bias_a, w_pool, pool_scale, w_branch_a, w_branch_b, w_branch_c, w_out, norm_ffn, w_up, conv_w, conv_b, w_down, loss_target, m_norm_mix, m_w_in, m_b_gate, m_q_norm_a, m_k_norm_a, m_rel_bias_a, m_w_pool, m_pool_scale, m_w_branch_a, m_w_branch_b, m_w_branch_c, m_w_out, m_norm_ffn, m_w_up, m_conv_w, m_conv_b, m_w_down, v_norm_mix, v_w_in, v_b_gate, v_q_norm_a, v_k_norm_a, v_rel_bias_a, v_w_pool, v_pool_scale, v_w_branch_a, v_w_branch_b, v_w_branch_c, v_w_out, v_norm_ffn, v_w_up, v_conv_w, v_conv_b, v_w_down):
    args = dict(locals())
    w = {n: args[n] for n in WEIGHTS}
    m = {n: args["m_" + n] for n in WEIGHTS}
    v = {n: args["v_" + n] for n in WEIGHTS}
    L = w_in.shape[0]
    T = x.shape[1]
    xs = x.reshape(T, D_MODEL)
    target = loss_target.reshape(T, D_MODEL)

    gathered = _exchange([w[n].astype(BF16) for n in SHARDED] + [conv_w], False, "gather_weights")
    part = dict(zip(SHARDED + ("conv_w",), gathered))
    w_in_f = _gather_columns(part["w_in"], w_in.shape)
    w_qkv, w_uc, w_g = w_in_f[:, :, :QKV_COLS], w_in_f[:, :, QKV_COLS:QKV_COLS + WIDTH], w_in_f[:, :, QKV_COLS + WIDTH:]
    w_a = _gather_columns(part["w_branch_a"], w_branch_a.shape)
    w_b = _gather_columns(part["w_branch_b"], w_branch_b.shape)
    w_c = _gather_columns(part["w_branch_c"], w_branch_c.shape)
    w_out_f = _gather_rows(part["w_out"], w_out.shape)
    w_up_f = _ff_pair_order(_gather_columns(part["w_up"], w_up.shape))
    w_down_f = _gather_rows(part["w_down"], w_down.shape)
    conv_w_f = _ff_pair_order(_gather_columns(part["conv_w"], conv_w.shape))
    conv_b_f = _ff_pair_order(conv_b)
    onehot = _rel_onehot()

    saved = []
    cur = xs
    for l in range(L):
        h = _rmsnorm_fwd(cur, norm_mix[l], "norm_mix_fwd")
        qkv = _mm(h, w_qkv[l], "nn", BF16, "proj_qkv")
        uc = _mm(h, w_uc[l], "nn", F32, "proj_pool")
        glog = _mm(h, w_g[l], "nn", F32, "proj_gate")
        table = _band_table(_select_mm(rel_bias_a[l], onehot, "nn", "rel_bias_table").reshape(N_HEADS, CHUNK, BAND))
        oa = _attn_a_fwd(qkv, table, q_norm_a[l], k_norm_a[l])
        ob, carries = _attn_b_fwd(qkv)
        oc = _pool_fwd(uc, w_pool[l], pool_scale[l])
        merged, ya, yb, yc = _merge_fwd(oa, ob, oc, glog, b_gate[l], w_a[l], w_b[l], w_c[l])
        x1 = _mm(merged, w_out_f[l], "nn", F32, "out_proj", res=cur)
        h2 = _rmsnorm_fwd(x1, norm_ffn[l], "norm_ffn_fwd")
        u = _mm(h2, w_up_f[l], "nn", F32, "ff_up")
        act = _ff_act_fwd(u, conv_w_f[l], conv_b_f[l])
        x2 = _mm(act, w_down_f[l], "nn", F32, "ff_down", res=x1)
        saved.append(dict(x=cur, h=h, qkv=qkv, carries=carries, uc=uc, glog=glog, table=table, oa=oa, ob=ob, oc=oc,
                          ya=ya, yb=yb, yc=yc, merged=merged, x1=x1, h2=h2, u=u, act=act))
        cur = x2

    dcur, loss_local = _loss_head(cur, target)
    loss = lax.psum(loss_local, ("x", "y", "c"))

    gw = {n: [None] * L for n in WEIGHTS}
    for l in reversed(range(L)):
        s = saved[l]
        da = _mm(dcur, w_down_f[l], "nt", F32, "ff_down_dx", tn_cap=1408)
        gw["w_down"][l] = _mm(s["act"], dcur, "tn", BF16, "ff_down_dw")
        dc, dconv_w, dconv_b = _ff_act_bwd(s["u"], da, conv_w_f[l], conv_b_f[l])
        du = _ff_conv_bwd(dc, conv_w_f[l])
        dh2 = _mm(du, w_up_f[l], "nt", F32, "ff_up_dx")
        gw["w_up"][l] = _ff_natural_order(_mm(s["h2"], du, "tn", BF16, "ff_up_dw"))
        gw["conv_w"][l] = _ff_natural_order(dconv_w)
        gw["conv_b"][l] = _ff_natural_order(dconv_b)[0]
        dx1, dg = _rmsnorm_bwd(s["x1"], norm_ffn[l], dh2, dcur, "norm_ffn_bwd")
        gw["norm_ffn"][l] = dg[0]

        dmerged = _mm(dx1, w_out_f[l], "nt", F32, "out_proj_dx")
        gw["w_out"][l] = _mm(s["merged"], dx1, "tn", BF16, "out_proj_dw")
        dya, dyb, dyc, dglog, db_gate = _merge_bwd(dmerged, s["glog"], b_gate[l], s["ya"], s["yb"], s["yc"])
        gw["b_gate"][l] = db_gate[0]
        do = {}
        for tag, dy, wk, ok in (("a", dya, w_a, s["oa"]), ("b", dyb, w_b, s["ob"]), ("c", dyc, w_c, s["oc"])):
            do[tag] = _mm(dy, wk[l], "nt", BF16, "branch_dx_" + tag)
            gw["w_branch_" + tag][l] = _mm(ok, dy, "tn", BF16, "branch_dw_" + tag)
        duc, dw_pool, dscale = _pool_bwd(s["uc"], do["c"], w_pool[l], pool_scale[l])
        gw["w_pool"][l] = dw_pool
        gw["pool_scale"][l] = dscale[0]
        dqa, dkc, dkp, dvc, dvp, dtable, dgq = _attn_a_bwd(s["qkv"], do["a"], s["table"], q_norm_a[l], k_norm_a[l])
        dka, dva, dgk = _attn_a_bwd_keys(s["qkv"], dkc, dkp, dvc, dvp, k_norm_a[l])
        gw["q_norm_a"][l] = jnp.sum(dgq.reshape(N_HEADS, HEAD_DIM), axis=0)
        gw["k_norm_a"][l] = jnp.sum(dgk.reshape(N_HEADS, HEAD_DIM), axis=0)
        gw["rel_bias_a"][l] = _select_mm(_band_table_bwd(dtable).reshape(N_HEADS, CHUNK * BAND), onehot, "nt",
                                         "rel_bias_table_dw")
        dqb, dkb, dvb = _attn_b_bwd(s["qkv"], s["carries"], do["b"])
        dqkv = jnp.concatenate([dqa, dka, dva, dqb, dkb.astype(BF16), dvb.astype(BF16)], axis=1)
        dh = _mm(dqkv, w_qkv[l], "nt", F32, "proj_qkv_dx")
        dh = _mm(duc, w_uc[l], "nt", F32, "proj_pool_dx", res=dh)
        dh = _mm(dglog, w_g[l], "nt", F32, "proj_gate_dx", res=dh)
        gw["w_in"][l] = jnp.concatenate([_mm(s["h"], dqkv, "tn", BF16, "proj_qkv_dw"),
                                         _mm(s["h"], duc, "tn", BF16, "proj_pool_dw"),
                                         _mm(s["h"], dglog, "tn", BF16, "proj_gate_dw")], axis=1)
        dcur, dg = _rmsnorm_bwd(s["x"], norm_mix[l], dh, dx1, "norm_mix_bwd")
        gw["norm_mix"][l] = dg[0]
    gw = {n: jnp.stack(g) for n, g in gw.items()}

    row_sharded = ("w_out", "w_down")
    exchanged = SHARDED + ("conv_w",)
    pieces = [(_split_rows if n in row_sharded else _split_columns)(gw[n]) for n in exchanged]
    parts = dict(zip(exchanged, _exchange(pieces, True, "exchange_grads")))
    small = _pack_rows([gw[n] for n in REPLICATED], SMALL_COLS, 16)
    small_parts = _exchange([small], False, "gather_small_grads")[0]

    out = {}
    for n in exchanged:
        flat = lambda a: a.reshape(-1, a.shape[-1])
        res = _adamw(parts[n], flat(w[n]), flat(m[n]), flat(v[n]), "adamw_" + n)
        out[n] = tuple(r.reshape(w[n].shape) for r in res)
    rep_like = [w[n] for n in REPLICATED]
    res = _adamw(small_parts, *[_pack_rows([d[n] for n in REPLICATED], SMALL_COLS, 16) for d in (w, m, v)],
                 "adamw_replicated")
    out.update({n: r for n, r in zip(REPLICATED, zip(*[_unpack_rows(r, rep_like) for r in res]))})

    grads, deltas, new_m, new_v = ([out[n][i] for n in WEIGHTS] for i in range(4))
    return (loss, dcur.reshape(x.shape), *grads, *deltas, *new_m, *new_v)
```

```python
import functools
import math

import jax
import jax.numpy as jnp
from jax import lax
from jax.experimental import pallas as pl
from jax.experimental.pallas import tpu as pltpu

F32 = jnp.float32
BF16 = jnp.bfloat16

N_DEV = 8
D_MODEL = 1024
N_HEADS = 8
HEAD_DIM = 64
CHUNK = 64
N_LEFT = 8
BAND = (N_LEFT + 1) * CHUNK
WIDTH = N_HEADS * HEAD_DIM
POOL_WINDOWS = (2, 4, 8, 16)
POOL_DIM = 128
MAX_REL = 2 * CHUNK
REL_TABLE = MAX_REL + CHUNK
D_FF = 2816
EPS = 1e-6
SB_SCAN = 256
SB_ROWS = 512
SB_KEYS = 512
A_BLOCK = N_LEFT * CHUNK
HALO = 16
LANES = 128
VMEM_LIMIT = 56 * 1024 * 1024

ADAM_LR = 0.001
ADAM_B1 = 0.9
ADAM_B2 = 0.999
ADAM_EPS = 1e-08
ADAM_WD = 0.01
ADAM_STEP = 10

MESH = pl.DeviceIdType.MESH


def _params(sem):
    return pltpu.CompilerParams(dimension_semantics=sem, vmem_limit_bytes=VMEM_LIMIT)


def _pick(n, cap):
    if n <= cap:
        return n
    best = None
    for t in range(LANES, cap + 1, LANES):
        if n % t == 0:
            best = t
    assert best is not None, (n, cap)
    return best


MM_TILE_CAP = 1408


def _mm(a, b, mode, out_dtype, name, tm=MM_TILE_CAP, tn_cap=MM_TILE_CAP, tk_cap=MM_TILE_CAP, res=None):
    if mode == "nn":
        (M, K), (K2, N) = a.shape, b.shape
    elif mode == "nt":
        (M, K), (N, K2) = a.shape, b.shape
    else:
        (K, M), (K2, N) = a.shape, b.shape
    assert K == K2, (a.shape, b.shape, mode)
    tm = _pick(M, tm)
    tn = _pick(N, tn_cap)
    tk = _pick(K, tk_cap)
    nk = K // tk
    if mode == "nn":
        dims = (((1,), (0,)), ((), ()))
        a_spec = pl.BlockSpec((tm, tk), lambda i, j, k: (i, k))
        b_spec = pl.BlockSpec((tk, tn), lambda i, j, k: (k, j))
    elif mode == "nt":
        dims = (((1,), (1,)), ((), ()))
        a_spec = pl.BlockSpec((tm, tk), lambda i, j, k: (i, k))
        b_spec = pl.BlockSpec((tn, tk), lambda i, j, k: (j, k))
    else:
        dims = (((0,), (0,)), ((), ()))
        a_spec = pl.BlockSpec((tk, tm), lambda i, j, k: (k, i))
        b_spec = pl.BlockSpec((tk, tn), lambda i, j, k: (k, j))

    o_spec = pl.BlockSpec((tm, tn), lambda i, j, k: (i, j))

    def body(a_ref, b_ref, *rest):
        res_ref = rest[0] if res is not None else None
        o_ref = rest[1] if res is not None else rest[0]
        part = lax.dot_general(a_ref[...].astype(BF16), b_ref[...].astype(BF16), dims, preferred_element_type=F32)
        if nk == 1:
            o_ref[...] = (part if res is None else part + res_ref[...]).astype(out_dtype)
            return
        acc_ref = rest[-1]
        k = pl.program_id(2)

        @pl.when(k == 0)
        def _():
            acc_ref[...] = part

        @pl.when(k > 0)
        def _():
            acc_ref[...] += part

        @pl.when(k == nk - 1)
        def _():
            total = acc_ref[...] if res is None else acc_ref[...] + res_ref[...]
            o_ref[...] = total.astype(out_dtype)

    return pl.pallas_call(
        body, name=name,
        grid=(M // tm, N // tn, nk),
        in_specs=[a_spec, b_spec] + ([o_spec] if res is not None else []),
        out_specs=o_spec,
        out_shape=jax.ShapeDtypeStruct((M, N), out_dtype),
        scratch_shapes=[pltpu.VMEM((tm, tn), F32)] if nk > 1 else [],
        compiler_params=_params(("parallel", "parallel", "arbitrary")),
    )(*((a, b) if res is None else (a, b, res)))


def _rmsnorm_fwd(x, gain, name, tm=512):
    T, C = x.shape

    def body(x_ref, g_ref, h_ref):
        xv = x_ref[...]
        r = lax.rsqrt(jnp.mean(xv * xv, axis=-1, keepdims=True) + EPS)
        h_ref[...] = (xv * r * g_ref[...]).astype(BF16)

    return pl.pallas_call(
        body, name=name, grid=(T // tm,),
        in_specs=[pl.BlockSpec((tm, C), lambda i: (i, 0)), pl.BlockSpec((1, C), lambda i: (0, 0))],
        out_specs=pl.BlockSpec((tm, C), lambda i: (i, 0)),
        out_shape=jax.ShapeDtypeStruct((T, C), BF16),
        compiler_params=_params(("parallel",)),
    )(x, gain.reshape(1, C))


def _rmsnorm_bwd(x, gain, dh, dres, name, tm=512):
    T, C = x.shape

    def body(x_ref, g_ref, dh_ref, dres_ref, dx_ref, dg_ref):
        @pl.when(pl.program_id(0) == 0)
        def _():
            dg_ref[...] = jnp.zeros_like(dg_ref)

        xv = x_ref[...]
        dy = dh_ref[...].astype(F32)
        r = lax.rsqrt(jnp.mean(xv * xv, axis=-1, keepdims=True) + EPS)
        gdy = dy * g_ref[...]
        inner = jnp.mean(xv * gdy, axis=-1, keepdims=True)
        dx_ref[...] = dres_ref[...] + r * gdy - xv * (r * r * r * inner)
        dg_ref[...] += jnp.sum(dy * xv * r, axis=0, keepdims=True)

    return pl.pallas_call(
        body, name=name, grid=(T // tm,),
        in_specs=[pl.BlockSpec((tm, C), lambda i: (i, 0)), pl.BlockSpec((1, C), lambda i: (0, 0)),
                  pl.BlockSpec((tm, C), lambda i: (i, 0)), pl.BlockSpec((tm, C), lambda i: (i, 0))],
        out_specs=[pl.BlockSpec((tm, C), lambda i: (i, 0)), pl.BlockSpec((1, C), lambda i: (0, 0))],
        out_shape=[jax.ShapeDtypeStruct((T, C), F32), jax.ShapeDtypeStruct((1, C), F32)],
        compiler_params=_params(("arbitrary",)),
    )(x, gain.reshape(1, C), dh, dres)


MASKED = -1e30


def _pair_sum(x, same_head):
    hi = x.astype(BF16)
    lo = (x - hi.astype(F32)).astype(BF16)
    return jnp.dot(hi, same_head, preferred_element_type=F32) + jnp.dot(lo, same_head, preferred_element_type=F32)


def _same_head():
    r = lax.broadcasted_iota(jnp.int32, (2 * HEAD_DIM, 2 * HEAD_DIM), 0)
    c = lax.broadcasted_iota(jnp.int32, (2 * HEAD_DIM, 2 * HEAD_DIM), 1)
    return jnp.where((r < HEAD_DIM) == (c < HEAD_DIM), 1.0, 0.0).astype(BF16)


def _pair_norm(t, g, same_head):
    tf = t.astype(F32)
    r = lax.rsqrt(_pair_sum(tf * tf, same_head) * (1.0 / HEAD_DIM) + EPS)
    return tf * r * g


def _pair_norm_bwd(t, g, dn, same_head):
    tf = t.astype(F32)
    r = lax.rsqrt(_pair_sum(tf * tf, same_head) * (1.0 / HEAD_DIM) + EPS)
    gd = dn * g
    inner = _pair_sum(tf * gd, same_head) * (1.0 / HEAD_DIM)
    return r * gd - tf * (r * r * r * inner), jnp.sum(dn * tf * r, axis=0, keepdims=True)


def _band_table(bias):
    table = jnp.full((N_HEADS, A_BLOCK, 2 * A_BLOCK), MASKED, F32)
    for c in range(N_LEFT):
        table = table.at[:, c * CHUNK:(c + 1) * CHUNK, c * CHUNK:c * CHUNK + BAND].set(bias)
    return table.reshape(N_HEADS // 2, 2 * A_BLOCK, 2 * A_BLOCK)


def _band_table_bwd(dtable):
    dtable = dtable.reshape(N_HEADS, A_BLOCK, 2 * A_BLOCK)
    return sum(dtable[:, c * CHUNK:(c + 1) * CHUNK, c * CHUNK:c * CHUNK + BAND] for c in range(N_LEFT))


def _a_specs(T):
    nb = T // A_BLOCK
    pairs = N_HEADS // 2
    col = lambda which: which * pairs
    cur = lambda which: pl.BlockSpec((A_BLOCK, 2 * HEAD_DIM), lambda p, i: (i, col(which) + p))
    prev = lambda which: pl.BlockSpec((A_BLOCK, 2 * HEAD_DIM), lambda p, i: (jnp.maximum(i - 1, 0), col(which) + p))
    nxt = lambda which: pl.BlockSpec((A_BLOCK, 2 * HEAD_DIM), lambda p, i: (jnp.minimum(i + 1, nb - 1), col(which) + p))
    table = pl.BlockSpec((1, 2 * A_BLOCK, 2 * A_BLOCK), lambda p, i: (p, 0, 0))
    gain = pl.BlockSpec((1, 2 * HEAD_DIM), lambda p, i: (0, 0))
    gacc = pl.BlockSpec((1, 1, 2 * HEAD_DIM), lambda p, i: (p, 0, 0))
    return nb, pairs, cur, prev, nxt, table, gain, gacc


def _a_probs(q_ref, kc_ref, kp_ref, t_ref, gq_ref, gk_ref, same_head, first):
    scale = 1.0 / math.sqrt(HEAD_DIM)
    qst = _stack_heads((_pair_norm(q_ref[...], gq_ref[...], same_head) * scale).astype(BF16))
    kcat = jnp.concatenate([_pair_norm(kp_ref[...], gk_ref[...], same_head).astype(BF16),
                            _pair_norm(kc_ref[...], gk_ref[...], same_head).astype(BF16)], axis=0)
    s = lax.dot_general(qst, kcat, (((1,), (1,)), ((), ())), preferred_element_type=F32) + t_ref[0]
    col = lax.broadcasted_iota(jnp.int32, s.shape, 1)
    s = jnp.where(col >= jnp.where(first, A_BLOCK, 0), s, MASKED)
    e = jnp.exp(s - jnp.max(s, axis=-1, keepdims=True))
    return qst, kcat, e, jnp.sum(e, axis=-1, keepdims=True)


def _attn_a_fwd(qkv, table, gq, gk):
    T = qkv.shape[0]
    nb, pairs, cur, prev, _, tspec, gspec, _ = _a_specs(T)

    def body(q_ref, kc_ref, kp_ref, vc_ref, vp_ref, t_ref, gq_ref, gk_ref, o_ref):
        same_head = _same_head()
        _, _, e, total = _a_probs(q_ref, kc_ref, kp_ref, t_ref, gq_ref, gk_ref, same_head, pl.program_id(1) == 0)
        vcat = jnp.concatenate([vp_ref[...], vc_ref[...]], axis=0)
        p = (e / total).astype(BF16)
        o_ref[...] = _unstack_heads(jnp.dot(p, vcat, preferred_element_type=F32)).astype(BF16)

    return pl.pallas_call(
        body, name="attn_a_fwd", grid=(pairs, nb),
        in_specs=[cur(0), cur(1), prev(1), cur(2), prev(2), tspec, gspec, gspec],
        out_specs=pl.BlockSpec((A_BLOCK, 2 * HEAD_DIM), lambda p, i: (i, p)),
        out_shape=jax.ShapeDtypeStruct((T, WIDTH), BF16),
        compiler_params=_params(("parallel", "arbitrary")),
    )(qkv, qkv, qkv, qkv, qkv, table, jnp.tile(gq.reshape(1, HEAD_DIM), (1, 2)), jnp.tile(gk.reshape(1, HEAD_DIM), (1, 2)))


def _attn_a_bwd(qkv, do, table, gq, gk):
    T = qkv.shape[0]
    nb, pairs, cur, prev, _, tspec, gspec, gacc = _a_specs(T)
    scale = 1.0 / math.sqrt(HEAD_DIM)
    oblk = pl.BlockSpec((A_BLOCK, 2 * HEAD_DIM), lambda p, i: (i, p))

    def body(q_ref, kc_ref, kp_ref, vc_ref, vp_ref, do_ref, t_ref, gq_ref, gk_ref,
             dq_ref, dkc_ref, dkp_ref, dvc_ref, dvp_ref, dt_ref, dgq_ref):
        first = pl.program_id(1) == 0

        @pl.when(first)
        def _():
            dt_ref[...] = jnp.zeros_like(dt_ref)
            dgq_ref[...] = jnp.zeros_like(dgq_ref)

        same_head = _same_head()
        qst, kcat, e, total = _a_probs(q_ref, kc_ref, kp_ref, t_ref, gq_ref, gk_ref, same_head, first)
        vcat = jnp.concatenate([vp_ref[...], vc_ref[...]], axis=0)
        dost = _stack_heads(do_ref[...])
        p = e / total
        dp = lax.dot_general(dost, vcat, (((1,), (1,)), ((), ())), preferred_element_type=F32)
        ds = p * (dp - jnp.sum(p * dp, axis=-1, keepdims=True))
        dt_ref[0] += ds
        dsb = ds.astype(BF16)
        dqn = _unstack_heads(jnp.dot(dsb, kcat, preferred_element_type=F32)) * scale
        dq, dg = _pair_norm_bwd(q_ref[...], gq_ref[...], dqn, same_head)
        dq_ref[...] = dq.astype(BF16)
        dgq_ref[0] += dg
        dk = lax.dot_general(dsb, qst, (((0,), (0,)), ((), ())), preferred_element_type=F32)
        dv = lax.dot_general(p.astype(BF16), dost, (((0,), (0,)), ((), ())), preferred_element_type=F32)
        dkp_ref[...] = dk[:A_BLOCK]
        dkc_ref[...] = dk[A_BLOCK:]
        dvp_ref[...] = dv[:A_BLOCK]
        dvc_ref[...] = dv[A_BLOCK:]

    wide = jax.ShapeDtypeStruct((T, WIDTH), F32)
    return pl.pallas_call(
        body, name="attn_a_bwd", grid=(pairs, nb),
        in_specs=[cur(0), cur(1), prev(1), cur(2), prev(2), oblk, tspec, gspec, gspec],
        out_specs=[oblk, oblk, oblk, oblk, oblk, tspec, gacc],
        out_shape=[jax.ShapeDtypeStruct((T, WIDTH), BF16), wide, wide, wide, wide,
                   jax.ShapeDtypeStruct((pairs, 2 * A_BLOCK, 2 * A_BLOCK), F32),
                   jax.ShapeDtypeStruct((pairs, 1, 2 * HEAD_DIM), F32)],
        compiler_params=_params(("parallel", "arbitrary")),
    )(qkv, qkv, qkv, qkv, qkv, do, table, jnp.tile(gq.reshape(1, HEAD_DIM), (1, 2)),
      jnp.tile(gk.reshape(1, HEAD_DIM), (1, 2)))


def _attn_a_bwd_keys(qkv, dkc, dkp, dvc, dvp, gk):
    T = qkv.shape[0]
    nb, pairs, cur, _, _, _, gspec, gacc = _a_specs(T)
    oblk = pl.BlockSpec((A_BLOCK, 2 * HEAD_DIM), lambda p, i: (i, p))
    onext = pl.BlockSpec((A_BLOCK, 2 * HEAD_DIM), lambda p, i: (jnp.minimum(i + 1, nb - 1), p))

    def body(k_ref, dkc_ref, dkp_ref, dvc_ref, dvp_ref, gk_ref, dk_ref, dv_ref, dgk_ref):
        i = pl.program_id(1)

        @pl.when(i == 0)
        def _():
            dgk_ref[...] = jnp.zeros_like(dgk_ref)

        has_next = (i < nb - 1).astype(F32)
        dkn = dkc_ref[...] + has_next * dkp_ref[...]
        dk, dg = _pair_norm_bwd(k_ref[...], gk_ref[...], dkn, _same_head())
        dk_ref[...] = dk.astype(BF16)
        dv_ref[...] = (dvc_ref[...] + has_next * dvp_ref[...]).astype(BF16)
        dgk_ref[0] += dg

    blk = jax.ShapeDtypeStruct((T, WIDTH), BF16)
    return pl.pallas_call(
        body, name="attn_a_bwd_keys", grid=(pairs, nb),
        in_specs=[cur(1), oblk, onext, oblk, onext, gspec],
        out_specs=[oblk, oblk, gacc],
        out_shape=[blk, blk, jax.ShapeDtypeStruct((pairs, 1, 2 * HEAD_DIM), F32)],
        compiler_params=_params(("parallel", "arbitrary")),
    )(qkv, dkc, dkp, dvc, dvp, jnp.tile(gk.reshape(1, HEAD_DIM), (1, 2)))


def _scan_matrix(later):
    r = lax.broadcasted_iota(jnp.int32, (SB_SCAN, SB_SCAN), 0)
    c = lax.broadcasted_iota(jnp.int32, (SB_SCAN, SB_SCAN), 1)
    return jnp.where((r > c) if later else (r < c), 1.0, 0.0).astype(BF16)


def _running_sums(x, carry, scan, later):
    n = SB_KEYS // SB_SCAN
    parts = [None] * n
    total = carry
    for sb in (reversed(range(n)) if later else range(n)):
        xs = x[:, sb * SB_SCAN:(sb + 1) * SB_SCAN]
        local = jnp.dot(xs.astype(BF16), scan, preferred_element_type=F32)
        parts[sb] = local if total is None else local + total
        rowsum = jnp.sum(xs, axis=-1, keepdims=True)
        total = rowsum if total is None else total + rowsum
    return (parts[0] if n == 1 else jnp.concatenate(parts, axis=1)), total


def _sb_log_sigmoids(z):
    neg_abs = pltpu.bitcast(pltpu.bitcast(z, jnp.uint32) | jnp.uint32(0x80000000), F32)
    take = jnp.minimum(z, 0.0) - jnp.log(1.0 + jnp.exp(neg_abs))
    return take, take - z


def _sb_mask():
    r = lax.broadcasted_iota(jnp.int32, (2 * SB_ROWS, SB_KEYS), 0)
    c = lax.broadcasted_iota(jnp.int32, (2 * SB_ROWS, SB_KEYS), 1)
    return c < jnp.where(r >= SB_ROWS, r - SB_ROWS, r)


def _stack_heads(t):
    lane = lax.broadcasted_iota(jnp.int32, t.shape, 1)
    zero = jnp.zeros_like(t)
    return jnp.concatenate([jnp.where(lane < HEAD_DIM, t, zero), jnp.where(lane >= HEAD_DIM, t, zero)], axis=0)


def _unstack_heads(t):
    rows = t.shape[0] // 2
    lane = lax.broadcasted_iota(jnp.int32, (rows, 2 * HEAD_DIM), 1)
    return jnp.where(lane < HEAD_DIM, t[:rows], t[rows:])


def _sb_specs(T):
    nq = T // SB_ROWS
    blk = lambda col: pl.BlockSpec((SB_ROWS, 2 * HEAD_DIM), lambda p, i: (i, col + p))
    full = lambda col: pl.BlockSpec((T, 2 * HEAD_DIM), lambda p, i: (0, col + p))
    return nq, blk, full


def _key_rows(j):
    return pl.ds(pl.multiple_of(j * SB_KEYS, SB_KEYS), SB_KEYS)


def _attn_b_fwd(qkv):
    T = qkv.shape[0]
    pairs = N_HEADS // 2
    nq, blk, full = _sb_specs(T)
    scale = 1.0 / math.sqrt(HEAD_DIM)

    assert nq <= LANES

    def body(q_ref, k_ref, v_ref, o_ref, c_ref, acc_ref, carry_ref, z_ref, w_ref):
        i = pl.program_id(1)
        scan = _scan_matrix(True)
        qst = _stack_heads((q_ref[...].astype(F32) * scale).astype(BF16))
        lane = lax.broadcasted_iota(jnp.int32, (2 * SB_ROWS, LANES), 1)

        def scores(j):
            return lax.dot_general(qst, k_ref[_key_rows(j), :], (((1,), (1,)), ((), ())), preferred_element_type=F32)

        def weights(z, carry, mask):
            take, keep = _sb_log_sigmoids(z)
            if mask is not None:
                keep = jnp.where(mask, keep, 0.0)
            tail, total = _running_sums(keep, carry, scan, True)
            w = jnp.exp(take + tail)
            if mask is not None:
                w = jnp.where(mask, w, 0.0)
            return w.astype(BF16), total

        w_ref[...], carry_ref[...] = weights(scores(i), None, _sb_mask())
        z_ref[...] = scores(jnp.maximum(i - 1, 0))
        acc_ref[...] = jnp.zeros_like(acc_ref)
        c_ref[0, 0] = jnp.zeros((2 * SB_ROWS, LANES), F32)

        @pl.loop(0, i)
        def _(jj):
            j = i - 1 - jj
            z = z_ref[...]
            z_ref[...] = scores(jnp.maximum(j - 1, 0))
            acc_ref[...] += jnp.dot(w_ref[...], v_ref[_key_rows(j + 1), :], preferred_element_type=F32)
            carry = carry_ref[...]
            c_ref[0, 0] = jnp.where(lane == j, carry, c_ref[0, 0])
            w_ref[...], carry_ref[...] = weights(z, carry, None)

        acc = acc_ref[...] + jnp.dot(w_ref[...], v_ref[_key_rows(0), :], preferred_element_type=F32)
        o_ref[...] = _unstack_heads(acc).astype(BF16)

    return pl.pallas_call(
        body, name="attn_b_fwd", grid=(pairs, nq),
        in_specs=[blk(3 * pairs), full(4 * pairs), full(5 * pairs)],
        out_specs=[pl.BlockSpec((SB_ROWS, 2 * HEAD_DIM), lambda p, i: (i, p)),
                   pl.BlockSpec((1, 1, 2 * SB_ROWS, LANES), lambda p, i: (p, i, 0, 0))],
        out_shape=[jax.ShapeDtypeStruct((T, WIDTH), BF16), jax.ShapeDtypeStruct((pairs, nq, 2 * SB_ROWS, LANES), F32)],
        scratch_shapes=[pltpu.VMEM((2 * SB_ROWS, 2 * HEAD_DIM), F32), pltpu.VMEM((2 * SB_ROWS, 1), F32),
                        pltpu.VMEM((2 * SB_ROWS, SB_KEYS), F32), pltpu.VMEM((2 * SB_ROWS, SB_KEYS), BF16)],
        compiler_params=_params(("parallel", "arbitrary")),
    )(qkv, qkv, qkv)


def _attn_b_bwd(qkv, carries, do):
    T = qkv.shape[0]
    pairs = N_HEADS // 2
    nq, blk, full = _sb_specs(T)
    scale = 1.0 / math.sqrt(HEAD_DIM)
    oblk = pl.BlockSpec((SB_ROWS, 2 * HEAD_DIM), lambda p, i: (i, p))
    ofull = pl.BlockSpec((T, 2 * HEAD_DIM), lambda p, i: (0, p))

    def body(q_ref, k_ref, v_ref, c_ref, do_ref, dq_ref, dk_ref, dv_ref, dqacc_ref, before_ref,
             z_ref, dw_ref, dz_ref, w_ref):
        i = pl.program_id(1)

        @pl.when(i == 0)
        def _():
            dk_ref[...] = jnp.zeros_like(dk_ref)
            dv_ref[...] = jnp.zeros_like(dv_ref)

        scan_later = _scan_matrix(True)
        scan_earlier = _scan_matrix(False)
        qst = _stack_heads((q_ref[...].astype(F32) * scale).astype(BF16))
        dost = _stack_heads(do_ref[...].astype(BF16))
        lane = lax.broadcasted_iota(jnp.int32, (2 * SB_ROWS, LANES), 1)
        nt = (((1,), (1,)), ((), ()))
        tn = (((0,), (0,)), ((), ()))

        def products(j):
            return (lax.dot_general(qst, k_ref[_key_rows(j), :], nt, preferred_element_type=F32),
                    lax.dot_general(dost, v_ref[_key_rows(j), :], nt, preferred_element_type=F32))

        def score_grads(z, dw, later, mask):
            take, keep = _sb_log_sigmoids(z)
            sig = jnp.exp(take)
            if mask is not None:
                keep = jnp.where(mask, keep, 0.0)
            tail, _ = _running_sums(keep, later, scan_later, True)
            w = jnp.exp(take + tail)
            if mask is not None:
                w = jnp.where(mask, w, 0.0)
            g = w * dw
            before, before_ref[...] = _running_sums(g, before_ref[...], scan_earlier, False)
            dz = g - sig * (g + before)
            if mask is not None:
                dz = jnp.where(mask, dz, 0.0)
            return dz.astype(BF16), w.astype(BF16)

        def accumulate(j, dzb, wb):
            dqacc_ref[...] += jnp.dot(dzb, k_ref[_key_rows(j), :], preferred_element_type=F32)
            dk_ref[_key_rows(j), :] += lax.dot_general(dzb, qst, tn, preferred_element_type=F32)
            dv_ref[_key_rows(j), :] += lax.dot_general(wb, dost, tn, preferred_element_type=F32)

        dqacc_ref[...] = jnp.zeros_like(dqacc_ref)
        before_ref[...] = jnp.zeros_like(before_ref)
        dz_ref[...] = jnp.zeros_like(dz_ref)
        w_ref[...] = jnp.zeros_like(w_ref)
        z_ref[...], dw_ref[...] = products(0)

        @pl.loop(0, i)
        def _(j):
            z, dw = z_ref[...], dw_ref[...]
            z_ref[...], dw_ref[...] = products(j + 1)
            accumulate(jnp.maximum(j - 1, 0), dz_ref[...], w_ref[...])
            later = jnp.sum(jnp.where(lane == j, c_ref[0, 0], 0.0), axis=-1, keepdims=True)
            dz_ref[...], w_ref[...] = score_grads(z, dw, later, None)

        accumulate(jnp.maximum(i - 1, 0), dz_ref[...], w_ref[...])
        accumulate(i, *score_grads(z_ref[...], dw_ref[...], None, _sb_mask()))
        dq_ref[...] = (_unstack_heads(dqacc_ref[...]) * scale).astype(BF16)

    wide = jax.ShapeDtypeStruct((T, WIDTH), F32)
    return pl.pallas_call(
        body, name="attn_b_bwd", grid=(pairs, nq),
        in_specs=[blk(3 * pairs), full(4 * pairs), full(5 * pairs),
                  pl.BlockSpec((1, 1, 2 * SB_ROWS, LANES), lambda p, i: (p, i, 0, 0)), oblk],
        out_specs=[oblk, ofull, ofull],
        out_shape=[jax.ShapeDtypeStruct((T, WIDTH), BF16), wide, wide],
        scratch_shapes=[pltpu.VMEM((2 * SB_ROWS, 2 * HEAD_DIM), F32), pltpu.VMEM((2 * SB_ROWS, 1), F32),
                        pltpu.VMEM((2 * SB_ROWS, SB_KEYS), F32), pltpu.VMEM((2 * SB_ROWS, SB_KEYS), F32),
                        pltpu.VMEM((2 * SB_ROWS, SB_KEYS), BF16), pltpu.VMEM((2 * SB_ROWS, SB_KEYS), BF16)],
        compiler_params=_params(("parallel", "arbitrary")),
    )(qkv, qkv, qkv, carries, do)


def _window_sums(ext, forward):
    n = ext.shape[0]
    out = []
    s = ext
    for step in (1, 2, 4, 8):
        s = s + pltpu.roll(s, (n - step) if forward else step, 0)
        out.append(s)
    return out


def _pool_counts(base, rows, win):
    t = base + lax.broadcasted_iota(jnp.int32, (rows, 1), 0)
    return jnp.minimum(t + 1, win).astype(F32)


def _pooled(u_ref, up_ref, i, tm):
    prev = jnp.where(i > 0, up_ref[...], 0.0)
    ext = jnp.concatenate([prev, u_ref[...]], axis=0)
    sums = _window_sums(ext, False)
    parts = []
    for g, win in enumerate(POOL_WINDOWS):
        cols = slice(g * POOL_DIM, (g + 1) * POOL_DIM)
        cnt = _pool_counts(i * tm, tm, win)
        parts.append(sums[g][HALO:, cols] / cnt - ext[HALO:, cols])
    return parts


def _pool_fwd(ucg, w_pool, scale, tm=512):
    T = ucg.shape[0]
    C = WIDTH

    def body(u_ref, up_ref, w_ref, s_ref, o_ref):
        i = pl.program_id(0)
        parts = _pooled(u_ref, up_ref, i, tm)
        for g in range(len(POOL_WINDOWS)):
            mixed = jnp.dot(parts[g].astype(BF16), w_ref[g], preferred_element_type=F32)
            o_ref[:, g * POOL_DIM:(g + 1) * POOL_DIM] = (mixed * s_ref[:, g * POOL_DIM:(g + 1) * POOL_DIM]).astype(BF16)

    return pl.pallas_call(
        body, name="pool_fwd", grid=(T // tm,),
        in_specs=[pl.BlockSpec((tm, C), lambda i: (i, 0)),
                  pl.BlockSpec((HALO, C), lambda i: (jnp.maximum(i * (tm // HALO) - 1, 0), 0)),
                  pl.BlockSpec((len(POOL_WINDOWS), POOL_DIM, POOL_DIM), lambda i: (0, 0, 0)),
                  pl.BlockSpec((1, C), lambda i: (0, 0))],
        out_specs=pl.BlockSpec((tm, C), lambda i: (i, 0)),
        out_shape=jax.ShapeDtypeStruct((T, C), BF16),
        compiler_params=_params(("parallel",)),
    )(ucg, ucg, w_pool.astype(BF16), scale.reshape(1, C))


def _pool_bwd(ucg, do_c, w_pool, scale, tm=512):
    T = ucg.shape[0]
    C = WIDTH
    nt = T // tm
    G = len(POOL_WINDOWS)

    def body(u_ref, up_ref, do_ref, don_ref, w_ref, s_ref, du_ref, dw_ref, ds_ref):
        i = pl.program_id(0)

        @pl.when(i == 0)
        def _():
            dw_ref[...] = jnp.zeros_like(dw_ref)
            ds_ref[...] = jnp.zeros_like(ds_ref)

        parts = _pooled(u_ref, up_ref, i, tm)
        nxt = jnp.where(i < nt - 1, don_ref[...].astype(F32), 0.0)
        do_ext = jnp.concatenate([do_ref[...].astype(F32), nxt], axis=0) * s_ref[...]
        for g, win in enumerate(POOL_WINDOWS):
            cols = slice(g * POOL_DIM, (g + 1) * POOL_DIM)
            pooled_b = parts[g].astype(BF16)
            dmix = do_ext[:, cols].astype(BF16)
            mixed = jnp.dot(pooled_b, w_ref[g], preferred_element_type=F32)
            ds_ref[:, cols] += jnp.sum(do_ref[:, cols].astype(F32) * mixed, axis=0, keepdims=True)
            dw_ref[g] += lax.dot_general(pooled_b, dmix[:tm], (((0,), (0,)), ((), ())), preferred_element_type=F32)
            dpool = lax.dot_general(dmix, w_ref[g], (((1,), (1,)), ((), ())), preferred_element_type=F32)
            scaled = dpool / _pool_counts(i * tm, tm + HALO, win)
            fwd = _window_sums(scaled, True)[g]
            du_ref[:, cols] = (fwd[:tm] - dpool[:tm]).astype(BF16)

    return pl.pallas_call(
        body, name="pool_bwd", grid=(nt,),
        in_specs=[pl.BlockSpec((tm, C), lambda i: (i, 0)),
                  pl.BlockSpec((HALO, C), lambda i: (jnp.maximum(i * (tm // HALO) - 1, 0), 0)),
                  pl.BlockSpec((tm, C), lambda i: (i, 0)),
                  pl.BlockSpec((HALO, C), lambda i: (jnp.minimum((i + 1) * (tm // HALO), T // HALO - 1), 0)),
                  pl.BlockSpec((G, POOL_DIM, POOL_DIM), lambda i: (0, 0, 0)),
                  pl.BlockSpec((1, C), lambda i: (0, 0))],
        out_specs=[pl.BlockSpec((tm, C), lambda i: (i, 0)),
                   pl.BlockSpec((G, POOL_DIM, POOL_DIM), lambda i: (0, 0, 0)),
                   pl.BlockSpec((1, C), lambda i: (0, 0))],
        out_shape=[jax.ShapeDtypeStruct((T, C), BF16), jax.ShapeDtypeStruct((G, POOL_DIM, POOL_DIM), F32),
                   jax.ShapeDtypeStruct((1, C), F32)],
        compiler_params=_params(("arbitrary",)),
    )(ucg, ucg, do_c, do_c, w_pool.astype(BF16), scale.reshape(1, C))


def _merge_fwd(oa, ob, oc, glog, b_gate, wa, wb, wc, tm=256):
    T = oa.shape[0]
    Dm = D_MODEL
    row = lambda c: pl.BlockSpec((tm, c), lambda i: (i, 0))
    wspec = pl.BlockSpec((WIDTH, Dm), lambda i: (0, 0))

    def body(oa_ref, ob_ref, oc_ref, g_ref, b_ref, wa_ref, wb_ref, wc_ref, m_ref, ya_ref, yb_ref, yc_ref):
        merged = jnp.zeros((tm, Dm), F32)
        for kk, (o_ref, w_ref, y_ref) in enumerate(((oa_ref, wa_ref, ya_ref), (ob_ref, wb_ref, yb_ref),
                                                    (oc_ref, wc_ref, yc_ref))):
            y = jnp.dot(o_ref[...].astype(BF16), w_ref[...], preferred_element_type=F32)
            gate = jax.nn.sigmoid(g_ref[:, kk * Dm:(kk + 1) * Dm] + b_ref[:, kk * Dm:(kk + 1) * Dm])
            merged = merged + gate * y
            y_ref[...] = y.astype(BF16)
        m_ref[...] = merged.astype(BF16)

    out = jax.ShapeDtypeStruct((T, Dm), BF16)
    return pl.pallas_call(
        body, name="merge_fwd", grid=(T // tm,),
        in_specs=[row(WIDTH), row(WIDTH), row(WIDTH), row(3 * Dm), pl.BlockSpec((1, 3 * Dm), lambda i: (0, 0)),
                  wspec, wspec, wspec],
        out_specs=[row(Dm)] * 4,
        out_shape=[out] * 4,
        compiler_params=_params(("parallel",)),
    )(oa, ob, oc, glog, b_gate.reshape(1, 3 * Dm), wa, wb, wc)


def _merge_bwd(dmerged, glog, b_gate, ya, yb, yc, tm=256):
    T = dmerged.shape[0]
    Dm = D_MODEL
    row = lambda c: pl.BlockSpec((tm, c), lambda i: (i, 0))

    def body(dm_ref, g_ref, b_ref, ya_ref, yb_ref, yc_ref, dya_ref, dyb_ref, dyc_ref, dg_ref, db_ref):
        @pl.when(pl.program_id(0) == 0)
        def _():
            db_ref[...] = jnp.zeros_like(db_ref)

        dm = dm_ref[...]
        for kk, (y_ref, dy_ref) in enumerate(((ya_ref, dya_ref), (yb_ref, dyb_ref), (yc_ref, dyc_ref))):
            cols = slice(kk * Dm, (kk + 1) * Dm)
            gate = jax.nn.sigmoid(g_ref[:, cols] + b_ref[:, cols])
            dy_ref[...] = (dm * gate).astype(BF16)
            dlog = dm * y_ref[...].astype(F32) * gate * (1.0 - gate)
            dg_ref[:, cols] = dlog.astype(BF16)
            db_ref[:, cols] += jnp.sum(dlog, axis=0, keepdims=True)

    out = jax.ShapeDtypeStruct((T, Dm), BF16)
    return pl.pallas_call(
        body, name="merge_bwd", grid=(T // tm,),
        in_specs=[row(Dm), row(3 * Dm), pl.BlockSpec((1, 3 * Dm), lambda i: (0, 0)), row(Dm), row(Dm), row(Dm)],
        out_specs=[row(Dm), row(Dm), row(Dm), row(3 * Dm), pl.BlockSpec((1, 3 * Dm), lambda i: (0, 0))],
        out_shape=[out, out, out, jax.ShapeDtypeStruct((T, 3 * Dm), BF16), jax.ShapeDtypeStruct((1, 3 * Dm), F32)],
        compiler_params=_params(("arbitrary",)),
    )(dmerged, glog, b_gate.reshape(1, 3 * Dm), ya, yb, yc)


def _residual_add(x, y, name, tm=512):
    T, C = x.shape

    def body(x_ref, y_ref, o_ref):
        o_ref[...] = x_ref[...] + y_ref[...]

    spec = pl.BlockSpec((tm, C), lambda i: (i, 0))
    return pl.pallas_call(body, name=name, grid=(T // tm,), in_specs=[spec, spec], out_specs=spec,
                          out_shape=jax.ShapeDtypeStruct((T, C), F32), compiler_params=_params(("parallel",)))(x, y)


FF_TILE = 256
FF_TILES = D_FF // FF_TILE
CONV_HALO = 8


def _ff_pair_order(w):
    lead = w.shape[:-1]
    n = len(lead)
    w = w.reshape(*lead, 2, FF_TILES, FF_TILE)
    return jnp.swapaxes(w, n, n + 1).reshape(*lead, 2 * D_FF)


def _ff_natural_order(w):
    lead = w.shape[:-1]
    n = len(lead)
    w = w.reshape(*lead, FF_TILES, 2, FF_TILE)
    return jnp.swapaxes(w, n, n + 1).reshape(*lead, 2 * D_FF)


def _conv(ext, w_ref, b_ref):
    c = b_ref[...] + w_ref[2:3, :] * ext
    c = c + w_ref[1:2, :] * pltpu.roll(ext, 1, 0)
    c = c + w_ref[0:1, :] * pltpu.roll(ext, 2, 0)
    return c[CONV_HALO:]


def _ff_specs(T, tm):
    pair = pl.BlockSpec((tm, 2 * FF_TILE), lambda i, j: (i, j))
    prev = pl.BlockSpec((CONV_HALO, 2 * FF_TILE), lambda i, j: (jnp.maximum(i * (tm // CONV_HALO) - 1, 0), j))
    nxt = pl.BlockSpec((CONV_HALO, 2 * FF_TILE),
                       lambda i, j: (jnp.minimum((i + 1) * (tm // CONV_HALO), T // CONV_HALO - 1), j))
    half = pl.BlockSpec((tm, FF_TILE), lambda i, j: (i, j))
    small = lambda r: pl.BlockSpec((r, 2 * FF_TILE), lambda i, j: (0, j))
    return pair, prev, nxt, half, small


def _swap_grid(spec):
    return pl.BlockSpec(spec.block_shape, lambda j, i, f=spec.index_map: f(i, j))


def _ff_act_fwd(u, conv_w, conv_b, tm=1024):
    T = u.shape[0]
    pair, prev, _, half, small = _ff_specs(T, tm)

    def body(u_ref, p_ref, w_ref, b_ref, a_ref):
        i = pl.program_id(0)
        c = _conv(jnp.concatenate([jnp.where(i > 0, p_ref[...], 0.0), u_ref[...]], axis=0), w_ref, b_ref)
        cg, cv = c[:, :FF_TILE], c[:, FF_TILE:]
        a_ref[...] = (cg * jax.nn.sigmoid(cg) * cv).astype(BF16)

    return pl.pallas_call(
        body, name="ff_act_fwd", grid=(T // tm, FF_TILES),
        in_specs=[pair, prev, small(3), small(1)],
        out_specs=half,
        out_shape=jax.ShapeDtypeStruct((T, D_FF), BF16),
        compiler_params=_params(("parallel", "parallel")),
    )(u, u, conv_w, conv_b.reshape(1, -1))


def _ff_act_bwd(u, da, conv_w, conv_b, tm=1024):
    T = u.shape[0]
    pair, prev, _, half, small = _ff_specs(T, tm)

    def body(u_ref, p_ref, da_ref, w_ref, b_ref, dc_ref, dw_ref, db_ref):
        i = pl.program_id(1)

        @pl.when(i == 0)
        def _():
            dw_ref[...] = jnp.zeros_like(dw_ref)
            db_ref[...] = jnp.zeros_like(db_ref)

        ext = jnp.concatenate([jnp.where(i > 0, p_ref[...], 0.0), u_ref[...]], axis=0)
        c = _conv(ext, w_ref, b_ref)
        cg, cv = c[:, :FF_TILE], c[:, FF_TILE:]
        da = da_ref[...]
        sg = jax.nn.sigmoid(cg)
        dc = jnp.concatenate([da * cv * sg * (1.0 + cg * (1.0 - sg)), da * cg * sg], axis=1)
        dc_ref[...] = dc
        db_ref[...] += jnp.sum(dc, axis=0, keepdims=True)
        dw_ref[2:3, :] += jnp.sum(dc * ext[CONV_HALO:], axis=0, keepdims=True)
        dw_ref[1:2, :] += jnp.sum(dc * pltpu.roll(ext, 1, 0)[CONV_HALO:], axis=0, keepdims=True)
        dw_ref[0:1, :] += jnp.sum(dc * pltpu.roll(ext, 2, 0)[CONV_HALO:], axis=0, keepdims=True)

    return pl.pallas_call(
        body, name="ff_act_bwd", grid=(FF_TILES, T // tm),
        in_specs=[_swap_grid(pair), _swap_grid(prev), _swap_grid(half), _swap_grid(small(3)), _swap_grid(small(1))],
        out_specs=[_swap_grid(pair), _swap_grid(small(3)), _swap_grid(small(1))],
        out_shape=[jax.ShapeDtypeStruct((T, 2 * D_FF), F32), jax.ShapeDtypeStruct((3, 2 * D_FF), F32),
                   jax.ShapeDtypeStruct((1, 2 * D_FF), F32)],
        compiler_params=_params(("parallel", "arbitrary")),
    )(u, u, da, conv_w, conv_b.reshape(1, -1))


def _ff_conv_bwd(dc, conv_w, tm=1024):
    T = dc.shape[0]
    nt = T // tm
    pair, _, nxt, _, small = _ff_specs(T, tm)

    def body(dc_ref, n_ref, w_ref, du_ref):
        i = pl.program_id(0)
        ext = jnp.concatenate([dc_ref[...], jnp.where(i < nt - 1, n_ref[...], 0.0)], axis=0)
        n = tm + CONV_HALO
        du = w_ref[2:3, :] * ext + w_ref[1:2, :] * pltpu.roll(ext, n - 1, 0) + w_ref[0:1, :] * pltpu.roll(ext, n - 2, 0)
        du_ref[...] = du[:tm].astype(BF16)

    return pl.pallas_call(
        body, name="ff_conv_bwd", grid=(nt, FF_TILES),
        in_specs=[pair, nxt, small(3)],
        out_specs=pair,
        out_shape=jax.ShapeDtypeStruct((T, 2 * D_FF), BF16),
        compiler_params=_params(("parallel", "parallel")),
    )(dc, dc, conv_w)


def _loss_head(y, target, tm=512):
    T, C = y.shape
    nt = T // tm

    def body(y_ref, t_ref, dy_ref, l_ref):
        err = y_ref[...] - t_ref[...]
        dy_ref[...] = err * (1.0 / C)
        part = jnp.sum(err * err, axis=0, keepdims=True) * (0.5 / C)
        l_ref[0] = jnp.broadcast_to(part, (8, C))

    spec = pl.BlockSpec((tm, C), lambda i: (i, 0))
    dy, parts = pl.pallas_call(
        body, name="loss_head", grid=(nt,),
        in_specs=[spec, spec],
        out_specs=[spec, pl.BlockSpec((1, 8, C), lambda i: (i, 0, 0))],
        out_shape=[jax.ShapeDtypeStruct((T, C), F32), jax.ShapeDtypeStruct((nt, 8, C), F32)],
        compiler_params=_params(("parallel",)),
    )(y, target)
    return dy, jnp.sum(parts[:, 0, :])


def _adamw_math(w, g, m, v):
    m = ADAM_B1 * m + (1.0 - ADAM_B1) * g
    v = ADAM_B2 * v + (1.0 - ADAM_B2) * (g * g)
    m_hat = m / (1.0 - ADAM_B1 ** ADAM_STEP)
    v_hat = v / (1.0 - ADAM_B2 ** ADAM_STEP)
    delta = -ADAM_LR * (m_hat / (jnp.sqrt(v_hat) + ADAM_EPS) + ADAM_WD * w)
    return delta, m, v


def _adamw(parts, w, m, v, name, tm=256):
    R, C = w.shape
    tm = _pick_rows(R, tm)

    def body(p_ref, w_ref, m_ref, v_ref, g_ref, d_ref, nm_ref, nv_ref):
        g = p_ref[0].astype(F32)
        for s in range(1, N_DEV):
            g = g + p_ref[s].astype(F32)
        delta, nm, nv = _adamw_math(w_ref[...], g, m_ref[...], v_ref[...])
        g_ref[...] = g
        d_ref[...] = delta
        nm_ref[...] = nm
        nv_ref[...] = nv

    spec = pl.BlockSpec((tm, C), lambda i: (i, 0))
    out = jax.ShapeDtypeStruct((R, C), F32)
    return pl.pallas_call(
        body, name=name, grid=(R // tm,),
        in_specs=[pl.BlockSpec((N_DEV, tm, C), lambda i: (0, i, 0)), spec, spec, spec],
        out_specs=[spec] * 4,
        out_shape=[out] * 4,
        compiler_params=_params(("parallel",)),
    )(parts, w, m, v)


def _pick_rows(n, cap):
    if n < 16:
        return n
    best = None
    for t in range(16, min(n, cap) + 1, 16):
        if n % t == 0:
            best = t
    assert best is not None, (n, cap)
    return best


def _exchange(srcs, scatter, name):
    n = len(srcs)

    def body(*refs):
        src_refs, out_refs = refs[:n], refs[n:2 * n]
        send_sems, recv_sems, local_sems = refs[2 * n:]
        x, y, c = lax.axis_index("x"), lax.axis_index("y"), lax.axis_index("c")
        me = 4 * x + 2 * y + c

        def piece(a, d):
            return src_refs[a].at[d] if scatter else src_refs[a]

        local = [pltpu.make_async_copy(piece(a, me), out_refs[a].at[me], local_sems.at[a]) for a in range(n)]
        for cp in local:
            cp.start()
        copies = []
        for k in range(1, N_DEV):
            where, peer = _peer(k)
            for a in range(n):
                cp = pltpu.make_async_remote_copy(
                    src_ref=piece(a, peer), dst_ref=out_refs[a].at[me],
                    send_sem=send_sems.at[a * N_DEV + k], recv_sem=recv_sems.at[a * N_DEV + k],
                    device_id=where, device_id_type=MESH)
                cp.start()
                copies.append((cp, a, k, peer))
        for cp, a, k, peer in copies:
            cp.wait_send()
            pltpu.make_async_remote_copy(
                src_ref=piece(a, peer), dst_ref=out_refs[a].at[peer],
                send_sem=send_sems.at[a * N_DEV + k], recv_sem=recv_sems.at[a * N_DEV + k],
                device_id=(x, y, c), device_id_type=MESH).wait_recv()
        for cp in local:
            cp.wait()

    slab = lambda s: tuple(s.shape[1:] if scatter else s.shape)
    return pl.pallas_call(
        body, name=name,
        in_specs=[pl.BlockSpec(memory_space=pl.ANY)] * n,
        out_specs=[pl.BlockSpec(memory_space=pl.ANY)] * n,
        out_shape=[jax.ShapeDtypeStruct((N_DEV,) + slab(s), s.dtype) for s in srcs],
        scratch_shapes=[pltpu.SemaphoreType.DMA((n * N_DEV,)), pltpu.SemaphoreType.DMA((n * N_DEV,)),
                        pltpu.SemaphoreType.DMA((n,))],
    )(*srcs)


def _peer(k):
    x, y, c = lax.axis_index("x"), lax.axis_index("y"), lax.axis_index("c")
    px = 1 - x if k & 4 else x
    py = 1 - y if k & 2 else y
    pc = 1 - c if k & 1 else c
    return (px, py, pc), 4 * px + 2 * py + pc


def _split_copies(src_refs, land_refs, send_sems, recv_sems, scatter):
    x, y, c = lax.axis_index("x"), lax.axis_index("y"), lax.axis_index("c")
    me = 4 * x + 2 * y + c
    sends, arrivals = [], []
    for k in range(1, N_DEV):
        where, peer = _peer(k)
        for a, (src, land) in enumerate(zip(src_refs, land_refs)):
            piece = src.at[peer] if scatter else src
            sends.append(pltpu.make_async_remote_copy(
                src_ref=piece, dst_ref=land.at[me], send_sem=send_sems.at[a * N_DEV + k], recv_sem=recv_sems.at[a * N_DEV + k],
                device_id=where, device_id_type=MESH))
            arrivals.append(pltpu.make_async_remote_copy(
                src_ref=piece, dst_ref=land.at[peer], send_sem=send_sems.at[a * N_DEV + k], recv_sem=recv_sems.at[a * N_DEV + k],
                device_id=(x, y, c), device_id_type=MESH))
    return sends, arrivals


def _exchange_start(srcs, scatter, name):
    n = len(srcs)
    slab = lambda s: tuple(s.shape[1:] if scatter else s.shape)
    lands = [lax.empty((N_DEV,) + slab(s), s.dtype) for s in srcs]

    def body(*refs):
        src_refs, land_refs = refs[:n], refs[n:2 * n]
        send_sems, recv_sems = refs[2 * n], refs[2 * n + 1]
        token = refs[-1]
        sends, _ = _split_copies(src_refs, land_refs, send_sems, recv_sems, scatter)
        for cp in sends:
            cp.start()
        token[...] = jnp.zeros_like(token)

    hbm = pl.BlockSpec(memory_space=pltpu.HBM)
    sem = pl.BlockSpec(memory_space=pltpu.SEMAPHORE)
    out = pl.pallas_call(
        body, name=name,
        in_specs=[hbm] * (2 * n),
        out_specs=[sem, sem] + [hbm] * (2 * n) + [pl.BlockSpec(memory_space=pltpu.VMEM)],
        out_shape=[pltpu.SemaphoreType.DMA((n * N_DEV,)), pltpu.SemaphoreType.DMA((n * N_DEV,))]
        + [pltpu.HBM(s.shape, s.dtype) for s in srcs] + [pltpu.HBM(l.shape, l.dtype) for l in lands]
        + [jax.ShapeDtypeStruct((8, LANES), F32)],
        input_output_aliases={j: 2 + j for j in range(2 * n)},
        compiler_params=pltpu.CompilerParams(has_side_effects=pltpu.SideEffectType.DATAFLOW_SIDE_EFFECTING),
    )(*[pltpu.with_memory_space_constraint(s, pltpu.HBM) for s in srcs],
      *[pltpu.with_memory_space_constraint(l, pltpu.HBM) for l in lands])
    return (out[0], out[1], out[2:2 + n], out[2 + n:2 + 2 * n]), out[-1]


def _exchange_finish(state, scatter, after, name):
    send_sems, recv_sems, srcs, lands = state
    n = len(srcs)

    def body(*refs):
        src_refs, land_refs = refs[:n], refs[n:2 * n]
        sends, arrivals = _split_copies(src_refs, land_refs, refs[2 * n], refs[2 * n + 1], scatter)
        for cp in sends:
            cp.wait_send()
        for cp in arrivals:
            cp.wait_recv()

    hbm = pl.BlockSpec(memory_space=pltpu.HBM)
    sem = pl.BlockSpec(memory_space=pltpu.SEMAPHORE)
    out = pl.pallas_call(
        body, name=name,
        in_specs=[hbm] * (2 * n) + [sem, sem, pl.BlockSpec(memory_space=pl.ANY)],
        out_specs=[hbm] * (2 * n),
        out_shape=[pltpu.HBM(s.shape, s.dtype) for s in srcs] + [pltpu.HBM(l.shape, l.dtype) for l in lands],
        input_output_aliases={j: j for j in range(2 * n)},
        compiler_params=pltpu.CompilerParams(has_side_effects=pltpu.SideEffectType.DATAFLOW_SIDE_EFFECTING),
    )(*srcs, *lands, send_sems, recv_sems, after)
    return list(out[n:])


def _own_slab(landed, src, scatter):
    me = 4 * lax.axis_index("x") + 2 * lax.axis_index("y") + lax.axis_index("c")
    own = lax.dynamic_index_in_dim(src, me, axis=0, keepdims=True) if scatter else src[None]
    return lax.dynamic_update_slice_in_dim(landed, own, me, axis=0)


SHARDED = ("w_in", "w_branch_a", "w_branch_b", "w_branch_c", "w_out", "w_up", "w_down")
REPLICATED = ("norm_mix", "b_gate", "q_norm_a", "k_norm_a", "rel_bias_a", "w_pool", "pool_scale", "norm_ffn", "conv_b")
WEIGHTS = ("norm_mix", "w_in", "b_gate", "q_norm_a", "k_norm_a", "rel_bias_a", "w_pool", "pool_scale",
           "w_branch_a", "w_branch_b", "w_branch_c", "w_out", "norm_ffn", "w_up", "conv_w", "conv_b", "w_down")
SMALL_COLS = 128
QKV_COLS = 6 * WIDTH


def _rel_index():
    q_off = jnp.arange(CHUNK)[:, None] + N_LEFT * CHUNK
    k_off = jnp.arange(BAND)[None, :]
    return jnp.clip(q_off - k_off, -(CHUNK - 1), MAX_REL) + (CHUNK - 1)


def _rel_onehot():
    rel = _rel_index().reshape(1, CHUNK * BAND)
    return (rel == jnp.arange(REL_TABLE)[:, None]).astype(BF16)


def _select_mm(x, onehot, mode, name):
    hi = x.astype(BF16)
    r1 = x - hi.astype(F32)
    mid = r1.astype(BF16)
    lo = (r1 - mid.astype(F32)).astype(BF16)
    y = _mm(jnp.concatenate([hi, mid, lo, jnp.zeros_like(hi)], axis=0), onehot, mode, F32, name)
    n = x.shape[0]
    return y[:n] + y[n:2 * n] + y[2 * n:3 * n]


def _pack_rows(arrays, cols, row_multiple):
    flat = jnp.concatenate([a.reshape(-1) for a in arrays])
    rows = -(-flat.shape[0] // cols)
    rows = -(-rows // row_multiple) * row_multiple
    return jnp.pad(flat, (0, rows * cols - flat.shape[0])).reshape(rows, cols)


def _unpack_rows(packed, like):
    flat = packed.reshape(-1)
    out, off = [], 0
    for a in like:
        out.append(flat[off:off + a.size].reshape(a.shape))
        off += a.size
    return out


def kernel(x, norm_mix, w_in, b_gate, q_norm_a, k_norm_a, rel_bias_a, w_pool, pool_scale, w_branch_a, w_branch_b, w_branch_c, w_out, norm_ffn, w_up, conv_w, conv_b, w_down, loss_target, m_norm_mix, m_w_in, m_b_gate, m_q_norm_a, m_k_norm_a, m_rel_bias_a, m_w_pool, m_pool_scale, m_w_branch_a, m_w_branch_b, m_w_branch_c, m_w_out, m_norm_ffn, m_w_up, m_conv_w, m_conv_b, m_w_down, v_norm_mix, v_w_in, v_b_gate, v_q_norm_a, v_k_norm_a, v_rel_bias_a, v_w_pool, v_pool_scale, v_w_branch_a, v_w_branch_b, v_w_branch_c, v_w_out, v_norm_ffn, v_w_up, v_conv_w, v_conv_b, v_w_down):
    args = dict(locals())
    w = {n: args[n] for n in WEIGHTS}
    m = {n: args["m_" + n] for n in WEIGHTS}
    v = {n: args["v_" + n] for n in WEIGHTS}
    L = w_in.shape[0]
    T = x.shape[1]
    xs = x.reshape(T, D_MODEL)
    target = loss_target.reshape(T, D_MODEL)

    exchanged = SHARDED + ("conv_w",)
    row_sharded = ("w_out", "w_down")
    shard = {(n, l): (w[n][l] if n == "conv_w" else w[n][l].astype(BF16)) for n in exchanged for l in range(L)}
    late = [key for key in shard if key != ("w_in", 0)]
    gathered = {("w_in", 0): _exchange([shard["w_in", 0]], False, "gather_first")[0]}
    gather_state, gather_token = _exchange_start([shard[key] for key in late], False, "gather_rest_start")

    def full_weight(key):
        g = gathered[key]
        return g.reshape(-1, g.shape[-1]) if key[0] in row_sharded else g.transpose(1, 0, 2).reshape(g.shape[1], -1)

    w_in_f = {0: full_weight(("w_in", 0))}
    conv_b_f = _ff_pair_order(conv_b)
    onehot = _rel_onehot()

    saved = []
    cur = xs
    full = {}
    for l in range(L):
        w_qkv, w_uc, w_g = w_in_f[l][:, :QKV_COLS], w_in_f[l][:, QKV_COLS:QKV_COLS + WIDTH], w_in_f[l][:, QKV_COLS + WIDTH:]
        gain = norm_mix[l] + gather_token[0, 0] if l == 0 else norm_mix[l]
        h = _rmsnorm_fwd(cur, gain, "norm_mix_fwd")
        qkv = _mm(h, w_qkv, "nn", BF16, "proj_qkv")
        uc = _mm(h, w_uc, "nn", F32, "proj_pool")
        glog = _mm(h, w_g, "nn", F32, "proj_gate")
        table = _band_table(_select_mm(rel_bias_a[l], onehot, "nn", "rel_bias_table").reshape(N_HEADS, CHUNK, BAND))
        oa = _attn_a_fwd(qkv, table, q_norm_a[l], k_norm_a[l])
        ob, carries = _attn_b_fwd(qkv)
        oc = _pool_fwd(uc, w_pool[l], pool_scale[l])
        if l == 0:
            landed = _exchange_finish(gather_state, False, ob, "gather_rest_finish")
            gathered.update({key: _own_slab(g, shard[key], False) for key, g in zip(late, landed)})
            full = {key: full_weight(key) for key in gathered}
            w_in_f.update({k: full["w_in", k] for k in range(1, L)})
        w_a, w_b, w_c = (full["w_branch_" + tag, l] for tag in "abc")
        w_out_f, w_down_f = full["w_out", l], full["w_down", l]
        w_up_f, conv_w_f = _ff_pair_order(full["w_up", l]), _ff_pair_order(full["conv_w", l])
        merged, ya, yb, yc = _merge_fwd(oa, ob, oc, glog, b_gate[l], w_a, w_b, w_c)
        x1 = _mm(merged, w_out_f, "nn", F32, "out_proj", res=cur)
        h2 = _rmsnorm_fwd(x1, norm_ffn[l], "norm_ffn_fwd")
        u = _mm(h2, w_up_f, "nn", F32, "ff_up")
        act = _ff_act_fwd(u, conv_w_f, conv_b_f[l])
        x2 = _mm(act, w_down_f, "nn", F32, "ff_down", res=x1)
        saved.append(dict(x=cur, h=h, qkv=qkv, carries=carries, uc=uc, glog=glog, table=table, oa=oa, ob=ob, oc=oc,
                          ya=ya, yb=yb, yc=yc, merged=merged, x1=x1, h2=h2, u=u, act=act, w_qkv=w_qkv, w_uc=w_uc,
                          w_g=w_g, w_a=w_a, w_b=w_b, w_c=w_c, w_out=w_out_f, w_up=w_up_f, w_down=w_down_f,
                          conv_w=conv_w_f))
        cur = x2

    dcur, loss_local = _loss_head(cur, target)
    loss = lax.psum(loss_local, ("x", "y", "c"))

    def pieces_of(n, g):
        if n in row_sharded:
            return g.reshape(N_DEV, -1, g.shape[-1])
        return g.reshape(g.shape[0], N_DEV, -1).transpose(1, 0, 2)

    gw = {n: [None] * L for n in WEIGHTS}
    for l in reversed(range(L)):
        s = saved[l]
        da = _mm(dcur, s["w_down"], "nt", F32, "ff_down_dx", tn_cap=1408)
        gw["w_down"][l] = _mm(s["act"], dcur, "tn", BF16, "ff_down_dw")
        dc, dconv_w, dconv_b = _ff_act_bwd(s["u"], da, s["conv_w"], conv_b_f[l])
        du = _ff_conv_bwd(dc, s["conv_w"])
        dh2 = _mm(du, s["w_up"], "nt", F32, "ff_up_dx")
        gw["w_up"][l] = _ff_natural_order(_mm(s["h2"], du, "tn", BF16, "ff_up_dw"))
        gw["conv_w"][l] = _ff_natural_order(dconv_w)
        gw["conv_b"][l] = _ff_natural_order(dconv_b)[0]
        dx1, dg = _rmsnorm_bwd(s["x1"], norm_ffn[l], dh2, dcur, "norm_ffn_bwd")
        gw["norm_ffn"][l] = dg[0]

        dmerged = _mm(dx1, s["w_out"], "nt", F32, "out_proj_dx")
        gw["w_out"][l] = _mm(s["merged"], dx1, "tn", BF16, "out_proj_dw")
        dya, dyb, dyc, dglog, db_gate = _merge_bwd(dmerged, s["glog"], b_gate[l], s["ya"], s["yb"], s["yc"])
        gw["b_gate"][l] = db_gate[0]
        do = {}
        for tag, dy, ok in (("a", dya, s["oa"]), ("b", dyb, s["ob"]), ("c", dyc, s["oc"])):
            do[tag] = _mm(dy, s["w_" + tag], "nt", BF16, "branch_dx_" + tag)
            gw["w_branch_" + tag][l] = _mm(ok, dy, "tn", BF16, "branch_dw_" + tag)
        duc, dw_pool, dscale = _pool_bwd(s["uc"], do["c"], w_pool[l], pool_scale[l])
        gw["w_pool"][l] = dw_pool
        gw["pool_scale"][l] = dscale[0]
        gain_q = q_norm_a[l]
        if l == 0:
            early = [(n, k) for n in exchanged for k in range(L) if (n, k) != ("w_in", 0)]
            early_pieces = [pieces_of(n, gw[n][k]) for n, k in early]
            grads_state, grads_token = _exchange_start(early_pieces, True, "exchange_early_start")
            gain_q = gain_q + grads_token[0, 0]
        dqa, dkc, dkp, dvc, dvp, dtable, dgq = _attn_a_bwd(s["qkv"], do["a"], s["table"], gain_q, k_norm_a[l])
        dka, dva, dgk = _attn_a_bwd_keys(s["qkv"], dkc, dkp, dvc, dvp, k_norm_a[l])
        gw["q_norm_a"][l] = jnp.sum(dgq.reshape(N_HEADS, HEAD_DIM), axis=0)
        gw["k_norm_a"][l] = jnp.sum(dgk.reshape(N_HEADS, HEAD_DIM), axis=0)
        gw["rel_bias_a"][l] = _select_mm(_band_table_bwd(dtable).reshape(N_HEADS, CHUNK * BAND), onehot, "nt",
                                         "rel_bias_table_dw")
        dqb, dkb, dvb = _attn_b_bwd(s["qkv"], s["carries"], do["b"])
        dqkv = jnp.concatenate([dqa, dka, dva, dqb, dkb.astype(BF16), dvb.astype(BF16)], axis=1)
        dh = _mm(dqkv, s["w_qkv"], "nt", F32, "proj_qkv_dx")
        dh = _mm(duc, s["w_uc"], "nt", F32, "proj_pool_dx", res=dh)
        dh = _mm(dglog, s["w_g"], "nt", F32, "proj_gate_dx", res=dh)
        gw["w_in"][l] = jnp.concatenate([_mm(s["h"], dqkv, "tn", BF16, "proj_qkv_dw"),
                                         _mm(s["h"], duc, "tn", BF16, "proj_pool_dw"),
                                         _mm(s["h"], dglog, "tn", BF16, "proj_gate_dw")], axis=1)
        dcur, dg = _rmsnorm_bwd(s["x"], norm_mix[l], dh, dx1, "norm_mix_bwd")
        gw["norm_mix"][l] = dg[0]

    landed = _exchange_finish(grads_state, True, dcur, "exchange_early_finish")
    parts = {key: _own_slab(g, src, True) for key, g, src in zip(early, landed, early_pieces)}
    parts["w_in", 0] = _exchange([pieces_of("w_in", gw["w_in"][0])], True, "exchange_last")[0]
    small = _pack_rows([jnp.stack(gw[n]) for n in REPLICATED], SMALL_COLS, 16)
    small_parts = _exchange([small], False, "gather_small_grads")[0]

    out = {}
    for n in exchanged:
        res = [_adamw(parts[n, l], w[n][l], m[n][l], v[n][l], "adamw_" + n) for l in range(L)]
        out[n] = tuple(jnp.stack(r) for r in zip(*res))
    rep_like = [w[n] for n in REPLICATED]
    res = _adamw(small_parts, *[_pack_rows([d[n] for n in REPLICATED], SMALL_COLS, 16) for d in (w, m, v)],
                 "adamw_replicated")
    out.update({n: r for n, r in zip(REPLICATED, zip(*[_unpack_rows(r, rep_like) for r in res]))})

    grads, deltas, new_m, new_v = ([out[n][i] for n in WEIGHTS] for i in range(4))
    return (loss, dcur.reshape(x.shape), *grads, *deltas, *new_m, *new_v)
```

```python
import functools
import math

import jax
import jax.numpy as jnp
from jax import lax
from jax.experimental import pallas as pl
from jax.experimental.pallas import tpu as pltpu

F32 = jnp.float32
BF16 = jnp.bfloat16

N_DEV = 8
D_MODEL = 1024
N_HEADS = 8
HEAD_DIM = 64
CHUNK = 64
N_LEFT = 8
BAND = (N_LEFT + 1) * CHUNK
WIDTH = N_HEADS * HEAD_DIM
POOL_WINDOWS = (2, 4, 8, 16)
POOL_DIM = 128
MAX_REL = 2 * CHUNK
REL_TABLE = MAX_REL + CHUNK
D_FF = 2816
EPS = 1e-6
SB_SCAN = 256
SB_ROWS = 512
SB_KEYS = 512
A_BLOCK = N_LEFT * CHUNK
HALO = 16
LANES = 128
VMEM_LIMIT = 56 * 1024 * 1024

ADAM_LR = 0.001
ADAM_B1 = 0.9
ADAM_B2 = 0.999
ADAM_EPS = 1e-08
ADAM_WD = 0.01
ADAM_STEP = 10

MESH = pl.DeviceIdType.MESH


def _params(sem):
    return pltpu.CompilerParams(dimension_semantics=sem, vmem_limit_bytes=VMEM_LIMIT)


def _pick(n, cap):
    if n <= cap:
        return n
    best = None
    for t in range(LANES, cap + 1, LANES):
        if n % t == 0:
            best = t
    assert best is not None, (n, cap)
    return best


MM_TILE_CAP = 1408


def _mm(a, b, mode, out_dtype, name, tm=MM_TILE_CAP, tn_cap=MM_TILE_CAP, tk_cap=MM_TILE_CAP, res=None):
    if mode == "nn":
        (M, K), (K2, N) = a.shape, b.shape
    elif mode == "nt":
        (M, K), (N, K2) = a.shape, b.shape
    else:
        (K, M), (K2, N) = a.shape, b.shape
    assert K == K2, (a.shape, b.shape, mode)
    tm = _pick(M, tm)
    tn = _pick(N, tn_cap)
    tk = _pick(K, tk_cap)
    nk = K // tk
    if mode == "nn":
        dims = (((1,), (0,)), ((), ()))
        a_spec = pl.BlockSpec((tm, tk), lambda i, j, k: (i, k))
        b_spec = pl.BlockSpec((tk, tn), lambda i, j, k: (k, j))
    elif mode == "nt":
        dims = (((1,), (1,)), ((), ()))
        a_spec = pl.BlockSpec((tm, tk), lambda i, j, k: (i, k))
        b_spec = pl.BlockSpec((tn, tk), lambda i, j, k: (j, k))
    else:
        dims = (((0,), (0,)), ((), ()))
        a_spec = pl.BlockSpec((tk, tm), lambda i, j, k: (k, i))
        b_spec = pl.BlockSpec((tk, tn), lambda i, j, k: (k, j))

    o_spec = pl.BlockSpec((tm, tn), lambda i, j, k: (i, j))

    def body(a_ref, b_ref, *rest):
        res_ref = rest[0] if res is not None else None
        o_ref = rest[1] if res is not None else rest[0]
        part = lax.dot_general(a_ref[...].astype(BF16), b_ref[...].astype(BF16), dims, preferred_element_type=F32)
        if nk == 1:
            o_ref[...] = (part if res is None else part + res_ref[...]).astype(out_dtype)
            return
        acc_ref = rest[-1]
        k = pl.program_id(2)

        @pl.when(k == 0)
        def _():
            acc_ref[...] = part

        @pl.when(k > 0)
        def _():
            acc_ref[...] += part

        @pl.when(k == nk - 1)
        def _():
            total = acc_ref[...] if res is None else acc_ref[...] + res_ref[...]
            o_ref[...] = total.astype(out_dtype)

    return pl.pallas_call(
        body, name=name,
        grid=(M // tm, N // tn, nk),
        in_specs=[a_spec, b_spec] + ([o_spec] if res is not None else []),
        out_specs=o_spec,
        out_shape=jax.ShapeDtypeStruct((M, N), out_dtype),
        scratch_shapes=[pltpu.VMEM((tm, tn), F32)] if nk > 1 else [],
        compiler_params=_params(("parallel", "parallel", "arbitrary")),
    )(*((a, b) if res is None else (a, b, res)))


def _rmsnorm_fwd(x, gain, name, tm=512):
    T, C = x.shape

    def body(x_ref, g_ref, h_ref):
        xv = x_ref[...]
        r = lax.rsqrt(jnp.mean(xv * xv, axis=-1, keepdims=True) + EPS)
        h_ref[...] = (xv * r * g_ref[...]).astype(BF16)

    return pl.pallas_call(
        body, name=name, grid=(T // tm,),
        in_specs=[pl.BlockSpec((tm, C), lambda i: (i, 0)), pl.BlockSpec((1, C), lambda i: (0, 0))],
        out_specs=pl.BlockSpec((tm, C), lambda i: (i, 0)),
        out_shape=jax.ShapeDtypeStruct((T, C), BF16),
        compiler_params=_params(("parallel",)),
    )(x, gain.reshape(1, C))


def _rmsnorm_bwd(x, gain, dh, dres, name, tm=512):
    T, C = x.shape

    def body(x_ref, g_ref, dh_ref, dres_ref, dx_ref, dg_ref):
        @pl.when(pl.program_id(0) == 0)
        def _():
            dg_ref[...] = jnp.zeros_like(dg_ref)

        xv = x_ref[...]
        dy = dh_ref[...].astype(F32)
        r = lax.rsqrt(jnp.mean(xv * xv, axis=-1, keepdims=True) + EPS)
        gdy = dy * g_ref[...]
        inner = jnp.mean(xv * gdy, axis=-1, keepdims=True)
        dx_ref[...] = dres_ref[...] + r * gdy - xv * (r * r * r * inner)
        dg_ref[...] += jnp.sum(dy * xv * r, axis=0, keepdims=True)

    return pl.pallas_call(
        body, name=name, grid=(T // tm,),
        in_specs=[pl.BlockSpec((tm, C), lambda i: (i, 0)), pl.BlockSpec((1, C), lambda i: (0, 0)),
                  pl.BlockSpec((tm, C), lambda i: (i, 0)), pl.BlockSpec((tm, C), lambda i: (i, 0))],
        out_specs=[pl.BlockSpec((tm, C), lambda i: (i, 0)), pl.BlockSpec((1, C), lambda i: (0, 0))],
        out_shape=[jax.ShapeDtypeStruct((T, C), F32), jax.ShapeDtypeStruct((1, C), F32)],
        compiler_params=_params(("arbitrary",)),
    )(x, gain.reshape(1, C), dh, dres)


MASKED = -1e30


def _pair_sum(x, same_head):
    hi = x.astype(BF16)
    lo = (x - hi.astype(F32)).astype(BF16)
    return jnp.dot(hi, same_head, preferred_element_type=F32) + jnp.dot(lo, same_head, preferred_element_type=F32)


def _same_head():
    r = lax.broadcasted_iota(jnp.int32, (2 * HEAD_DIM, 2 * HEAD_DIM), 0)
    c = lax.broadcasted_iota(jnp.int32, (2 * HEAD_DIM, 2 * HEAD_DIM), 1)
    return jnp.where((r < HEAD_DIM) == (c < HEAD_DIM), 1.0, 0.0).astype(BF16)


def _pair_norm(t, g, same_head):
    tf = t.astype(F32)
    r = lax.rsqrt(_pair_sum(tf * tf, same_head) * (1.0 / HEAD_DIM) + EPS)
    return tf * r * g


def _pair_norm_bwd(t, g, dn, same_head):
    tf = t.astype(F32)
    r = lax.rsqrt(_pair_sum(tf * tf, same_head) * (1.0 / HEAD_DIM) + EPS)
    gd = dn * g
    inner = _pair_sum(tf * gd, same_head) * (1.0 / HEAD_DIM)
    return r * gd - tf * (r * r * r * inner), jnp.sum(dn * tf * r, axis=0, keepdims=True)


def _band_table(bias):
    rows = [jnp.pad(bias, ((0, 0), (0, 0), (c * CHUNK, 2 * A_BLOCK - BAND - c * CHUNK)), constant_values=MASKED)
            for c in range(N_LEFT)]
    return jnp.concatenate(rows, axis=1).reshape(N_HEADS // 2, 2 * A_BLOCK, 2 * A_BLOCK)


def _band_table_bwd(dtable):
    dtable = dtable.reshape(N_HEADS, A_BLOCK, 2 * A_BLOCK)
    return sum(dtable[:, c * CHUNK:(c + 1) * CHUNK, c * CHUNK:c * CHUNK + BAND] for c in range(N_LEFT))


def _a_specs(T):
    nb = T // A_BLOCK
    pairs = N_HEADS // 2
    col = lambda which: which * pairs
    cur = lambda which: pl.BlockSpec((A_BLOCK, 2 * HEAD_DIM), lambda p, i: (i, col(which) + p))
    prev = lambda which: pl.BlockSpec((A_BLOCK, 2 * HEAD_DIM), lambda p, i: (jnp.maximum(i - 1, 0), col(which) + p))
    nxt = lambda which: pl.BlockSpec((A_BLOCK, 2 * HEAD_DIM), lambda p, i: (jnp.minimum(i + 1, nb - 1), col(which) + p))
    table = pl.BlockSpec((1, 2 * A_BLOCK, 2 * A_BLOCK), lambda p, i: (p, 0, 0))
    gain = pl.BlockSpec((1, 2 * HEAD_DIM), lambda p, i: (0, 0))
    gacc = pl.BlockSpec((1, 1, 2 * HEAD_DIM), lambda p, i: (p, 0, 0))
    return nb, pairs, cur, prev, nxt, table, gain, gacc


def _a_probs(q_ref, kc_ref, kp_ref, t_ref, gq_ref, gk_ref, same_head, first):
    scale = 1.0 / math.sqrt(HEAD_DIM)
    qst = _stack_heads((_pair_norm(q_ref[...], gq_ref[...], same_head) * scale).astype(BF16))
    kcat = jnp.concatenate([_pair_norm(kp_ref[...], gk_ref[...], same_head).astype(BF16),
                            _pair_norm(kc_ref[...], gk_ref[...], same_head).astype(BF16)], axis=0)
    s = lax.dot_general(qst, kcat, (((1,), (1,)), ((), ())), preferred_element_type=F32) + t_ref[0]
    col = lax.broadcasted_iota(jnp.int32, s.shape, 1)
    s = jnp.where(col >= jnp.where(first, A_BLOCK, 0), s, MASKED)
    e = jnp.exp(s - jnp.max(s, axis=-1, keepdims=True))
    return qst, kcat, e, jnp.sum(e, axis=-1, keepdims=True)


def _attn_a_fwd(qkv, table, gq, gk):
    T = qkv.shape[0]
    nb, pairs, cur, prev, _, tspec, gspec, _ = _a_specs(T)

    def body(q_ref, kc_ref, kp_ref, vc_ref, vp_ref, t_ref, gq_ref, gk_ref, o_ref):
        same_head = _same_head()
        _, _, e, total = _a_probs(q_ref, kc_ref, kp_ref, t_ref, gq_ref, gk_ref, same_head, pl.program_id(1) == 0)
        vcat = jnp.concatenate([vp_ref[...], vc_ref[...]], axis=0)
        p = (e / total).astype(BF16)
        o_ref[...] = _unstack_heads(jnp.dot(p, vcat, preferred_element_type=F32)).astype(BF16)

    return pl.pallas_call(
        body, name="attn_a_fwd", grid=(pairs, nb),
        in_specs=[cur(0), cur(1), prev(1), cur(2), prev(2), tspec, gspec, gspec],
        out_specs=pl.BlockSpec((A_BLOCK, 2 * HEAD_DIM), lambda p, i: (i, p)),
        out_shape=jax.ShapeDtypeStruct((T, WIDTH), BF16),
        compiler_params=_params(("parallel", "arbitrary")),
    )(qkv, qkv, qkv, qkv, qkv, table, jnp.tile(gq.reshape(1, HEAD_DIM), (1, 2)), jnp.tile(gk.reshape(1, HEAD_DIM), (1, 2)))


def _attn_a_bwd(qkv, do, table, gq, gk):
    T = qkv.shape[0]
    nb, pairs, cur, prev, _, tspec, gspec, gacc = _a_specs(T)
    scale = 1.0 / math.sqrt(HEAD_DIM)
    oblk = pl.BlockSpec((A_BLOCK, 2 * HEAD_DIM), lambda p, i: (i, p))

    def body(q_ref, kc_ref, kp_ref, vc_ref, vp_ref, do_ref, t_ref, gq_ref, gk_ref,
             dq_ref, dkc_ref, dkp_ref, dvc_ref, dvp_ref, dt_ref, dgq_ref):
        first = pl.program_id(1) == 0

        @pl.when(first)
        def _():
            dt_ref[...] = jnp.zeros_like(dt_ref)
            dgq_ref[...] = jnp.zeros_like(dgq_ref)

        same_head = _same_head()
        qst, kcat, e, total = _a_probs(q_ref, kc_ref, kp_ref, t_ref, gq_ref, gk_ref, same_head, first)
        vcat = jnp.concatenate([vp_ref[...], vc_ref[...]], axis=0)
        dost = _stack_heads(do_ref[...])
        p = e / total
        dp = lax.dot_general(dost, vcat, (((1,), (1,)), ((), ())), preferred_element_type=F32)
        ds = p * (dp - jnp.sum(p * dp, axis=-1, keepdims=True))
        dt_ref[0] += ds
        dsb = ds.astype(BF16)
        dqn = _unstack_heads(jnp.dot(dsb, kcat, preferred_element_type=F32)) * scale
        dq, dg = _pair_norm_bwd(q_ref[...], gq_ref[...], dqn, same_head)
        dq_ref[...] = dq.astype(BF16)
        dgq_ref[0] += dg
        dk = lax.dot_general(dsb, qst, (((0,), (0,)), ((), ())), preferred_element_type=F32)
        dv = lax.dot_general(p.astype(BF16), dost, (((0,), (0,)), ((), ())), preferred_element_type=F32)
        dkp_ref[...] = dk[:A_BLOCK]
        dkc_ref[...] = dk[A_BLOCK:]
        dvp_ref[...] = dv[:A_BLOCK]
        dvc_ref[...] = dv[A_BLOCK:]

    wide = jax.ShapeDtypeStruct((T, WIDTH), F32)
    return pl.pallas_call(
        body, name="attn_a_bwd", grid=(pairs, nb),
        in_specs=[cur(0), cur(1), prev(1), cur(2), prev(2), oblk, tspec, gspec, gspec],
        out_specs=[oblk, oblk, oblk, oblk, oblk, tspec, gacc],
        out_shape=[jax.ShapeDtypeStruct((T, WIDTH), BF16), wide, wide, wide, wide,
                   jax.ShapeDtypeStruct((pairs, 2 * A_BLOCK, 2 * A_BLOCK), F32),
                   jax.ShapeDtypeStruct((pairs, 1, 2 * HEAD_DIM), F32)],
        compiler_params=_params(("parallel", "arbitrary")),
    )(qkv, qkv, qkv, qkv, qkv, do, table, jnp.tile(gq.reshape(1, HEAD_DIM), (1, 2)),
      jnp.tile(gk.reshape(1, HEAD_DIM), (1, 2)))


def _attn_a_bwd_keys(qkv, dkc, dkp, dvc, dvp, gk):
    T = qkv.shape[0]
    nb, pairs, cur, _, _, _, gspec, gacc = _a_specs(T)
    oblk = pl.BlockSpec((A_BLOCK, 2 * HEAD_DIM), lambda p, i: (i, p))
    onext = pl.BlockSpec((A_BLOCK, 2 * HEAD_DIM), lambda p, i: (jnp.minimum(i + 1, nb - 1), p))

    def body(k_ref, dkc_ref, dkp_ref, dvc_ref, dvp_ref, gk_ref, dk_ref, dv_ref, dgk_ref):
        i = pl.program_id(1)

        @pl.when(i == 0)
        def _():
            dgk_ref[...] = jnp.zeros_like(dgk_ref)

        has_next = (i < nb - 1).astype(F32)
        dkn = dkc_ref[...] + has_next * dkp_ref[...]
        dk, dg = _pair_norm_bwd(k_ref[...], gk_ref[...], dkn, _same_head())
        dk_ref[...] = dk.astype(BF16)
        dv_ref[...] = (dvc_ref[...] + has_next * dvp_ref[...]).astype(BF16)
        dgk_ref[0] += dg

    blk = jax.ShapeDtypeStruct((T, WIDTH), BF16)
    return pl.pallas_call(
        body, name="attn_a_bwd_keys", grid=(pairs, nb),
        in_specs=[cur(1), oblk, onext, oblk, onext, gspec],
        out_specs=[oblk, oblk, gacc],
        out_shape=[blk, blk, jax.ShapeDtypeStruct((pairs, 1, 2 * HEAD_DIM), F32)],
        compiler_params=_params(("parallel", "arbitrary")),
    )(qkv, dkc, dkp, dvc, dvp, jnp.tile(gk.reshape(1, HEAD_DIM), (1, 2)))


def _scan_matrix(later):
    r = lax.broadcasted_iota(jnp.int32, (SB_SCAN, SB_SCAN), 0)
    c = lax.broadcasted_iota(jnp.int32, (SB_SCAN, SB_SCAN), 1)
    return jnp.where((r > c) if later else (r < c), 1.0, 0.0).astype(BF16)


def _running_sums(x, carry, scan, later):
    n = SB_KEYS // SB_SCAN
    parts = [None] * n
    total = carry
    for sb in (reversed(range(n)) if later else range(n)):
        xs = x[:, sb * SB_SCAN:(sb + 1) * SB_SCAN]
        local = jnp.dot(xs.astype(BF16), scan, preferred_element_type=F32)
        parts[sb] = local if total is None else local + total
        rowsum = jnp.sum(xs, axis=-1, keepdims=True)
        total = rowsum if total is None else total + rowsum
    return (parts[0] if n == 1 else jnp.concatenate(parts, axis=1)), total


def _sb_log_sigmoids(z):
    neg_abs = pltpu.bitcast(pltpu.bitcast(z, jnp.uint32) | jnp.uint32(0x80000000), F32)
    take = jnp.minimum(z, 0.0) - jnp.log(1.0 + jnp.exp(neg_abs))
    return take, take - z


def _sb_mask():
    r = lax.broadcasted_iota(jnp.int32, (2 * SB_ROWS, SB_KEYS), 0)
    c = lax.broadcasted_iota(jnp.int32, (2 * SB_ROWS, SB_KEYS), 1)
    return c < jnp.where(r >= SB_ROWS, r - SB_ROWS, r)


def _stack_heads(t):
    lane = lax.broadcasted_iota(jnp.int32, t.shape, 1)
    zero = jnp.zeros_like(t)
    return jnp.concatenate([jnp.where(lane < HEAD_DIM, t, zero), jnp.where(lane >= HEAD_DIM, t, zero)], axis=0)


def _unstack_heads(t):
    rows = t.shape[0] // 2
    lane = lax.broadcasted_iota(jnp.int32, (rows, 2 * HEAD_DIM), 1)
    return jnp.where(lane < HEAD_DIM, t[:rows], t[rows:])


def _sb_specs(T):
    nq = T // SB_ROWS
    blk = lambda col: pl.BlockSpec((SB_ROWS, 2 * HEAD_DIM), lambda p, i: (i, col + p))
    full = lambda col: pl.BlockSpec((T, 2 * HEAD_DIM), lambda p, i: (0, col + p))
    return nq, blk, full


def _key_rows(j):
    return pl.ds(pl.multiple_of(j * SB_KEYS, SB_KEYS), SB_KEYS)


def _attn_b_fwd(qkv):
    T = qkv.shape[0]
    pairs = N_HEADS // 2
    nq, blk, full = _sb_specs(T)
    scale = 1.0 / math.sqrt(HEAD_DIM)

    assert nq <= LANES

    def body(q_ref, k_ref, v_ref, o_ref, c_ref, acc_ref, carry_ref, z_ref, w_ref):
        i = pl.program_id(1)
        scan = _scan_matrix(True)
        qst = _stack_heads((q_ref[...].astype(F32) * scale).astype(BF16))
        lane = lax.broadcasted_iota(jnp.int32, (2 * SB_ROWS, LANES), 1)

        def scores(j):
            return lax.dot_general(qst, k_ref[_key_rows(j), :], (((1,), (1,)), ((), ())), preferred_element_type=F32)

        def weights(z, carry, mask):
            take, keep = _sb_log_sigmoids(z)
            if mask is not None:
                keep = jnp.where(mask, keep, 0.0)
            tail, total = _running_sums(keep, carry, scan, True)
            w = jnp.exp(take + tail)
            if mask is not None:
                w = jnp.where(mask, w, 0.0)
            return w.astype(BF16), total

        w_ref[...], carry_ref[...] = weights(scores(i), None, _sb_mask())
        z_ref[...] = scores(jnp.maximum(i - 1, 0))
        acc_ref[...] = jnp.zeros_like(acc_ref)
        c_ref[0, 0] = jnp.zeros((2 * SB_ROWS, LANES), F32)

        @pl.loop(0, i)
        def _(jj):
            j = i - 1 - jj
            z = z_ref[...]
            z_ref[...] = scores(jnp.maximum(j - 1, 0))
            acc_ref[...] += jnp.dot(w_ref[...], v_ref[_key_rows(j + 1), :], preferred_element_type=F32)
            carry = carry_ref[...]
            c_ref[0, 0] = jnp.where(lane == j, carry, c_ref[0, 0])
            w_ref[...], carry_ref[...] = weights(z, carry, None)

        acc = acc_ref[...] + jnp.dot(w_ref[...], v_ref[_key_rows(0), :], preferred_element_type=F32)
        o_ref[...] = _unstack_heads(acc).astype(BF16)

    return pl.pallas_call(
        body, name="attn_b_fwd", grid=(pairs, nq),
        in_specs=[blk(3 * pairs), full(4 * pairs), full(5 * pairs)],
        out_specs=[pl.BlockSpec((SB_ROWS, 2 * HEAD_DIM), lambda p, i: (i, p)),
                   pl.BlockSpec((1, 1, 2 * SB_ROWS, LANES), lambda p, i: (p, i, 0, 0))],
        out_shape=[jax.ShapeDtypeStruct((T, WIDTH), BF16), jax.ShapeDtypeStruct((pairs, nq, 2 * SB_ROWS, LANES), F32)],
        scratch_shapes=[pltpu.VMEM((2 * SB_ROWS, 2 * HEAD_DIM), F32), pltpu.VMEM((2 * SB_ROWS, 1), F32),
                        pltpu.VMEM((2 * SB_ROWS, SB_KEYS), F32), pltpu.VMEM((2 * SB_ROWS, SB_KEYS), BF16)],
        compiler_params=_params(("parallel", "arbitrary")),
    )(qkv, qkv, qkv)


def _attn_b_bwd(qkv, carries, do):
    T = qkv.shape[0]
    pairs = N_HEADS // 2
    nq, blk, full = _sb_specs(T)
    scale = 1.0 / math.sqrt(HEAD_DIM)
    oblk = pl.BlockSpec((SB_ROWS, 2 * HEAD_DIM), lambda p, i: (i, p))
    ofull = pl.BlockSpec((2 * HEAD_DIM, T), lambda p, i: (p, 0))

    def body(q_ref, k_ref, v_ref, c_ref, do_ref, dq_ref, dk_ref, dv_ref, dqacc_ref, before_ref,
             z_ref, dw_ref, dz_ref, w_ref):
        i = pl.program_id(1)

        @pl.when(i == 0)
        def _():
            dk_ref[...] = jnp.zeros_like(dk_ref)
            dv_ref[...] = jnp.zeros_like(dv_ref)

        scan_later = _scan_matrix(True)
        scan_earlier = _scan_matrix(False)
        qst = _stack_heads((q_ref[...].astype(F32) * scale).astype(BF16))
        dost = _stack_heads(do_ref[...].astype(BF16))
        lane = lax.broadcasted_iota(jnp.int32, (2 * SB_ROWS, LANES), 1)
        nt = (((1,), (1,)), ((), ()))

        def products(j):
            return (lax.dot_general(qst, k_ref[_key_rows(j), :], nt, preferred_element_type=F32),
                    lax.dot_general(dost, v_ref[_key_rows(j), :], nt, preferred_element_type=F32))

        def score_grads(z, dw, later, mask):
            take, keep = _sb_log_sigmoids(z)
            sig = jnp.exp(take)
            if mask is not None:
                keep = jnp.where(mask, keep, 0.0)
            tail, _ = _running_sums(keep, later, scan_later, True)
            w = jnp.exp(take + tail)
            if mask is not None:
                w = jnp.where(mask, w, 0.0)
            g = w * dw
            before, before_ref[...] = _running_sums(g, before_ref[...], scan_earlier, False)
            dz = g - sig * (g + before)
            if mask is not None:
                dz = jnp.where(mask, dz, 0.0)
            return dz.astype(BF16), w.astype(BF16)

        qst_t = qst.T
        dost_t = dost.T

        def accumulate(j, dzb, wb):
            cols = pl.ds(pl.multiple_of(j * SB_KEYS, SB_KEYS), SB_KEYS)
            dqacc_ref[...] += jnp.dot(dzb, k_ref[_key_rows(j), :], preferred_element_type=F32)
            dk_ref[:, cols] += jnp.dot(qst_t, dzb, preferred_element_type=F32)
            dv_ref[:, cols] += jnp.dot(dost_t, wb, preferred_element_type=F32)

        dqacc_ref[...] = jnp.zeros_like(dqacc_ref)
        before_ref[...] = jnp.zeros_like(before_ref)
        dz_ref[...] = jnp.zeros_like(dz_ref)
        w_ref[...] = jnp.zeros_like(w_ref)
        z_ref[...], dw_ref[...] = products(0)

        @pl.loop(0, i)
        def _(j):
            z, dw = z_ref[...], dw_ref[...]
            z_ref[...], dw_ref[...] = products(j + 1)
            accumulate(jnp.maximum(j - 1, 0), dz_ref[...], w_ref[...])
            later = jnp.sum(jnp.where(lane == j, c_ref[0, 0], 0.0), axis=-1, keepdims=True)
            dz_ref[...], w_ref[...] = score_grads(z, dw, later, None)

        accumulate(jnp.maximum(i - 1, 0), dz_ref[...], w_ref[...])
        accumulate(i, *score_grads(z_ref[...], dw_ref[...], None, _sb_mask()))
        dq_ref[...] = (_unstack_heads(dqacc_ref[...]) * scale).astype(BF16)

    wide = jax.ShapeDtypeStruct((WIDTH, T), F32)
    return pl.pallas_call(
        body, name="attn_b_bwd", grid=(pairs, nq),
        in_specs=[blk(3 * pairs), full(4 * pairs), full(5 * pairs),
                  pl.BlockSpec((1, 1, 2 * SB_ROWS, LANES), lambda p, i: (p, i, 0, 0)), oblk],
        out_specs=[oblk, ofull, ofull],
        out_shape=[jax.ShapeDtypeStruct((T, WIDTH), BF16), wide, wide],
        scratch_shapes=[pltpu.VMEM((2 * SB_ROWS, 2 * HEAD_DIM), F32), pltpu.VMEM((2 * SB_ROWS, 1), F32),
                        pltpu.VMEM((2 * SB_ROWS, SB_KEYS), F32), pltpu.VMEM((2 * SB_ROWS, SB_KEYS), F32),
                        pltpu.VMEM((2 * SB_ROWS, SB_KEYS), BF16), pltpu.VMEM((2 * SB_ROWS, SB_KEYS), BF16)],
        compiler_params=_params(("parallel", "arbitrary")),
    )(qkv, qkv, qkv, carries, do)


def _window_sums(ext, forward):
    n = ext.shape[0]
    out = []
    s = ext
    for step in (1, 2, 4, 8):
        s = s + pltpu.roll(s, (n - step) if forward else step, 0)
        out.append(s)
    return out


def _pool_counts(base, rows, win):
    t = base + lax.broadcasted_iota(jnp.int32, (rows, 1), 0)
    return jnp.minimum(t + 1, win).astype(F32)


def _pooled(u_ref, up_ref, i, tm):
    prev = jnp.where(i > 0, up_ref[...], 0.0)
    ext = jnp.concatenate([prev, u_ref[...]], axis=0)
    sums = _window_sums(ext, False)
    parts = []
    for g, win in enumerate(POOL_WINDOWS):
        cols = slice(g * POOL_DIM, (g + 1) * POOL_DIM)
        cnt = _pool_counts(i * tm, tm, win)
        parts.append(sums[g][HALO:, cols] / cnt - ext[HALO:, cols])
    return parts


def _pool_fwd(ucg, w_pool, scale, tm=512):
    T = ucg.shape[0]
    C = WIDTH

    def body(u_ref, up_ref, w_ref, s_ref, o_ref):
        i = pl.program_id(0)
        parts = _pooled(u_ref, up_ref, i, tm)
        for g in range(len(POOL_WINDOWS)):
            mixed = jnp.dot(parts[g].astype(BF16), w_ref[g], preferred_element_type=F32)
            o_ref[:, g * POOL_DIM:(g + 1) * POOL_DIM] = (mixed * s_ref[:, g * POOL_DIM:(g + 1) * POOL_DIM]).astype(BF16)

    return pl.pallas_call(
        body, name="pool_fwd", grid=(T // tm,),
        in_specs=[pl.BlockSpec((tm, C), lambda i: (i, 0)),
                  pl.BlockSpec((HALO, C), lambda i: (jnp.maximum(i * (tm // HALO) - 1, 0), 0)),
                  pl.BlockSpec((len(POOL_WINDOWS), POOL_DIM, POOL_DIM), lambda i: (0, 0, 0)),
                  pl.BlockSpec((1, C), lambda i: (0, 0))],
        out_specs=pl.BlockSpec((tm, C), lambda i: (i, 0)),
        out_shape=jax.ShapeDtypeStruct((T, C), BF16),
        compiler_params=_params(("parallel",)),
    )(ucg, ucg, w_pool.astype(BF16), scale.reshape(1, C))


def _pool_bwd(ucg, do_c, w_pool, scale, tm=512):
    T = ucg.shape[0]
    C = WIDTH
    nt = T // tm
    G = len(POOL_WINDOWS)

    def body(u_ref, up_ref, do_ref, don_ref, w_ref, s_ref, du_ref, dw_ref, ds_ref):
        i = pl.program_id(0)

        @pl.when(i == 0)
        def _():
            dw_ref[...] = jnp.zeros_like(dw_ref)
            ds_ref[...] = jnp.zeros_like(ds_ref)

        parts = _pooled(u_ref, up_ref, i, tm)
        nxt = jnp.where(i < nt - 1, don_ref[...].astype(F32), 0.0)
        do_ext = jnp.concatenate([do_ref[...].astype(F32), nxt], axis=0) * s_ref[...]
        for g, win in enumerate(POOL_WINDOWS):
            cols = slice(g * POOL_DIM, (g + 1) * POOL_DIM)
            pooled_b = parts[g].astype(BF16)
            dmix = do_ext[:, cols].astype(BF16)
            mixed = jnp.dot(pooled_b, w_ref[g], preferred_element_type=F32)
            ds_ref[:, cols] += jnp.sum(do_ref[:, cols].astype(F32) * mixed, axis=0, keepdims=True)
            dw_ref[g] += lax.dot_general(pooled_b, dmix[:tm], (((0,), (0,)), ((), ())), preferred_element_type=F32)
            dpool = lax.dot_general(dmix, w_ref[g], (((1,), (1,)), ((), ())), preferred_element_type=F32)
            scaled = dpool / _pool_counts(i * tm, tm + HALO, win)
            fwd = _window_sums(scaled, True)[g]
            du_ref[:, cols] = (fwd[:tm] - dpool[:tm]).astype(BF16)

    return pl.pallas_call(
        body, name="pool_bwd", grid=(nt,),
        in_specs=[pl.BlockSpec((tm, C), lambda i: (i, 0)),
                  pl.BlockSpec((HALO, C), lambda i: (jnp.maximum(i * (tm // HALO) - 1, 0), 0)),
                  pl.BlockSpec((tm, C), lambda i: (i, 0)),
                  pl.BlockSpec((HALO, C), lambda i: (jnp.minimum((i + 1) * (tm // HALO), T // HALO - 1), 0)),
                  pl.BlockSpec((G, POOL_DIM, POOL_DIM), lambda i: (0, 0, 0)),
                  pl.BlockSpec((1, C), lambda i: (0, 0))],
        out_specs=[pl.BlockSpec((tm, C), lambda i: (i, 0)),
                   pl.BlockSpec((G, POOL_DIM, POOL_DIM), lambda i: (0, 0, 0)),
                   pl.BlockSpec((1, C), lambda i: (0, 0))],
        out_shape=[jax.ShapeDtypeStruct((T, C), BF16), jax.ShapeDtypeStruct((G, POOL_DIM, POOL_DIM), F32),
                   jax.ShapeDtypeStruct((1, C), F32)],
        compiler_params=_params(("arbitrary",)),
    )(ucg, ucg, do_c, do_c, w_pool.astype(BF16), scale.reshape(1, C))


def _merge_fwd(oa, ob, oc, glog, b_gate, wa, wb, wc, tm=256):
    T = oa.shape[0]
    Dm = D_MODEL
    row = lambda c: pl.BlockSpec((tm, c), lambda i: (i, 0))
    wspec = pl.BlockSpec((WIDTH, Dm), lambda i: (0, 0))

    def body(oa_ref, ob_ref, oc_ref, g_ref, b_ref, wa_ref, wb_ref, wc_ref, m_ref, ya_ref, yb_ref, yc_ref):
        merged = jnp.zeros((tm, Dm), F32)
        for kk, (o_ref, w_ref, y_ref) in enumerate(((oa_ref, wa_ref, ya_ref), (ob_ref, wb_ref, yb_ref),
                                                    (oc_ref, wc_ref, yc_ref))):
            y = jnp.dot(o_ref[...].astype(BF16), w_ref[...], preferred_element_type=F32)
            gate = jax.nn.sigmoid(g_ref[:, kk * Dm:(kk + 1) * Dm] + b_ref[:, kk * Dm:(kk + 1) * Dm])
            merged = merged + gate * y
            y_ref[...] = y.astype(BF16)
        m_ref[...] = merged.astype(BF16)

    out = jax.ShapeDtypeStruct((T, Dm), BF16)
    return pl.pallas_call(
        body, name="merge_fwd", grid=(T // tm,),
        in_specs=[row(WIDTH), row(WIDTH), row(WIDTH), row(3 * Dm), pl.BlockSpec((1, 3 * Dm), lambda i: (0, 0)),
                  wspec, wspec, wspec],
        out_specs=[row(Dm)] * 4,
        out_shape=[out] * 4,
        compiler_params=_params(("parallel",)),
    )(oa, ob, oc, glog, b_gate.reshape(1, 3 * Dm), wa, wb, wc)


def _merge_bwd(dmerged, glog, b_gate, ya, yb, yc, tm=256):
    T = dmerged.shape[0]
    Dm = D_MODEL
    row = lambda c: pl.BlockSpec((tm, c), lambda i: (i, 0))

    def body(dm_ref, g_ref, b_ref, ya_ref, yb_ref, yc_ref, dya_ref, dyb_ref, dyc_ref, dg_ref, db_ref):
        @pl.when(pl.program_id(0) == 0)
        def _():
            db_ref[...] = jnp.zeros_like(db_ref)

        dm = dm_ref[...]
        for kk, (y_ref, dy_ref) in enumerate(((ya_ref, dya_ref), (yb_ref, dyb_ref), (yc_ref, dyc_ref))):
            cols = slice(kk * Dm, (kk + 1) * Dm)
            gate = jax.nn.sigmoid(g_ref[:, cols] + b_ref[:, cols])
            dy_ref[...] = (dm * gate).astype(BF16)
            dlog = dm * y_ref[...].astype(F32) * gate * (1.0 - gate)
            dg_ref[:, cols] = dlog.astype(BF16)
            db_ref[:, cols] += jnp.sum(dlog, axis=0, keepdims=True)

    out = jax.ShapeDtypeStruct((T, Dm), BF16)
    return pl.pallas_call(
        body, name="merge_bwd", grid=(T // tm,),
        in_specs=[row(Dm), row(3 * Dm), pl.BlockSpec((1, 3 * Dm), lambda i: (0, 0)), row(Dm), row(Dm), row(Dm)],
        out_specs=[row(Dm), row(Dm), row(Dm), row(3 * Dm), pl.BlockSpec((1, 3 * Dm), lambda i: (0, 0))],
        out_shape=[out, out, out, jax.ShapeDtypeStruct((T, 3 * Dm), BF16), jax.ShapeDtypeStruct((1, 3 * Dm), F32)],
        compiler_params=_params(("arbitrary",)),
    )(dmerged, glog, b_gate.reshape(1, 3 * Dm), ya, yb, yc)


def _residual_add(x, y, name, tm=512):
    T, C = x.shape

    def body(x_ref, y_ref, o_ref):
        o_ref[...] = x_ref[...] + y_ref[...]

    spec = pl.BlockSpec((tm, C), lambda i: (i, 0))
    return pl.pallas_call(body, name=name, grid=(T // tm,), in_specs=[spec, spec], out_specs=spec,
                          out_shape=jax.ShapeDtypeStruct((T, C), F32), compiler_params=_params(("parallel",)))(x, y)


FF_TILE = 256
FF_TILES = D_FF // FF_TILE
CONV_HALO = 8


def _ff_pair_order(w):
    lead = w.shape[:-1]
    n = len(lead)
    w = w.reshape(*lead, 2, FF_TILES, FF_TILE)
    return jnp.swapaxes(w, n, n + 1).reshape(*lead, 2 * D_FF)


def _ff_natural_order(w):
    lead = w.shape[:-1]
    n = len(lead)
    w = w.reshape(*lead, FF_TILES, 2, FF_TILE)
    return jnp.swapaxes(w, n, n + 1).reshape(*lead, 2 * D_FF)


def _conv(ext, w_ref, b_ref):
    c = b_ref[...] + w_ref[2:3, :] * ext
    c = c + w_ref[1:2, :] * pltpu.roll(ext, 1, 0)
    c = c + w_ref[0:1, :] * pltpu.roll(ext, 2, 0)
    return c[CONV_HALO:]


def _ff_specs(T, tm):
    pair = pl.BlockSpec((tm, 2 * FF_TILE), lambda i, j: (i, j))
    prev = pl.BlockSpec((CONV_HALO, 2 * FF_TILE), lambda i, j: (jnp.maximum(i * (tm // CONV_HALO) - 1, 0), j))
    nxt = pl.BlockSpec((CONV_HALO, 2 * FF_TILE),
                       lambda i, j: (jnp.minimum((i + 1) * (tm // CONV_HALO), T // CONV_HALO - 1), j))
    half = pl.BlockSpec((tm, FF_TILE), lambda i, j: (i, j))
    small = lambda r: pl.BlockSpec((r, 2 * FF_TILE), lambda i, j: (0, j))
    return pair, prev, nxt, half, small


def _swap_grid(spec):
    return pl.BlockSpec(spec.block_shape, lambda j, i, f=spec.index_map: f(i, j))


def _ff_act_fwd(u, conv_w, conv_b, tm=1024):
    T = u.shape[0]
    pair, prev, _, half, small = _ff_specs(T, tm)

    def body(u_ref, p_ref, w_ref, b_ref, a_ref):
        i = pl.program_id(0)
        c = _conv(jnp.concatenate([jnp.where(i > 0, p_ref[...], 0.0), u_ref[...]], axis=0), w_ref, b_ref)
        cg, cv = c[:, :FF_TILE], c[:, FF_TILE:]
        a_ref[...] = (cg * jax.nn.sigmoid(cg) * cv).astype(BF16)

    return pl.pallas_call(
        body, name="ff_act_fwd", grid=(T // tm, FF_TILES),
        in_specs=[pair, prev, small(3), small(1)],
        out_specs=half,
        out_shape=jax.ShapeDtypeStruct((T, D_FF), BF16),
        compiler_params=_params(("parallel", "parallel")),
    )(u, u, conv_w, conv_b.reshape(1, -1))


def _ff_act_bwd(u, da, conv_w, conv_b, tm=1024):
    T = u.shape[0]
    pair, prev, _, half, small = _ff_specs(T, tm)

    def body(u_ref, p_ref, da_ref, w_ref, b_ref, dc_ref, dw_ref, db_ref):
        i = pl.program_id(1)

        @pl.when(i == 0)
        def _():
            dw_ref[...] = jnp.zeros_like(dw_ref)
            db_ref[...] = jnp.zeros_like(db_ref)

        ext = jnp.concatenate([jnp.where(i > 0, p_ref[...], 0.0), u_ref[...]], axis=0)
        c = _conv(ext, w_ref, b_ref)
        cg, cv = c[:, :FF_TILE], c[:, FF_TILE:]
        da = da_ref[...]
        sg = jax.nn.sigmoid(cg)
        dc = jnp.concatenate([da * cv * sg * (1.0 + cg * (1.0 - sg)), da * cg * sg], axis=1)
        dc_ref[...] = dc
        db_ref[...] += jnp.sum(dc, axis=0, keepdims=True)
        dw_ref[2:3, :] += jnp.sum(dc * ext[CONV_HALO:], axis=0, keepdims=True)
        dw_ref[1:2, :] += jnp.sum(dc * pltpu.roll(ext, 1, 0)[CONV_HALO:], axis=0, keepdims=True)
        dw_ref[0:1, :] += jnp.sum(dc * pltpu.roll(ext, 2, 0)[CONV_HALO:], axis=0, keepdims=True)

    return pl.pallas_call(
        body, name="ff_act_bwd", grid=(FF_TILES, T // tm),
        in_specs=[_swap_grid(pair), _swap_grid(prev), _swap_grid(half), _swap_grid(small(3)), _swap_grid(small(1))],
        out_specs=[_swap_grid(pair), _swap_grid(small(3)), _swap_grid(small(1))],
        out_shape=[jax.ShapeDtypeStruct((T, 2 * D_FF), F32), jax.ShapeDtypeStruct((3, 2 * D_FF), F32),
                   jax.ShapeDtypeStruct((1, 2 * D_FF), F32)],
        compiler_params=_params(("parallel", "arbitrary")),
    )(u, u, da, conv_w, conv_b.reshape(1, -1))


def _ff_conv_bwd(dc, conv_w, tm=1024):
    T = dc.shape[0]
    nt = T // tm
    pair, _, nxt, _, small = _ff_specs(T, tm)

    def body(dc_ref, n_ref, w_ref, du_ref):
        i = pl.program_id(0)
        ext = jnp.concatenate([dc_ref[...], jnp.where(i < nt - 1, n_ref[...], 0.0)], axis=0)
        n = tm + CONV_HALO
        du = w_ref[2:3, :] * ext + w_ref[1:2, :] * pltpu.roll(ext, n - 1, 0) + w_ref[0:1, :] * pltpu.roll(ext, n - 2, 0)
        du_ref[...] = du[:tm].astype(BF16)

    return pl.pallas_call(
        body, name="ff_conv_bwd", grid=(nt, FF_TILES),
        in_specs=[pair, nxt, small(3)],
        out_specs=pair,
        out_shape=jax.ShapeDtypeStruct((T, 2 * D_FF), BF16),
        compiler_params=_params(("parallel", "parallel")),
    )(dc, dc, conv_w)


def _loss_head(y, target, tm=512):
    T, C = y.shape
    nt = T // tm

    def body(y_ref, t_ref, dy_ref, l_ref):
        err = y_ref[...] - t_ref[...]
        dy_ref[...] = err * (1.0 / C)
        part = jnp.sum(err * err, axis=0, keepdims=True) * (0.5 / C)
        l_ref[0] = jnp.broadcast_to(part, (8, C))

    spec = pl.BlockSpec((tm, C), lambda i: (i, 0))
    dy, parts = pl.pallas_call(
        body, name="loss_head", grid=(nt,),
        in_specs=[spec, spec],
        out_specs=[spec, pl.BlockSpec((1, 8, C), lambda i: (i, 0, 0))],
        out_shape=[jax.ShapeDtypeStruct((T, C), F32), jax.ShapeDtypeStruct((nt, 8, C), F32)],
        compiler_params=_params(("parallel",)),
    )(y, target)
    return dy, jnp.sum(parts[:, 0, :])


def _adamw_math(w, g, m, v):
    m = ADAM_B1 * m + (1.0 - ADAM_B1) * g
    v = ADAM_B2 * v + (1.0 - ADAM_B2) * (g * g)
    m_hat = m / (1.0 - ADAM_B1 ** ADAM_STEP)
    v_hat = v / (1.0 - ADAM_B2 ** ADAM_STEP)
    delta = -ADAM_LR * (m_hat / (jnp.sqrt(v_hat) + ADAM_EPS) + ADAM_WD * w)
    return delta, m, v


def _adamw(parts, w, m, v, name, tm=256):
    R, C = w.shape
    tm = _pick_rows(R, tm)

    def body(p_ref, w_ref, m_ref, v_ref, g_ref, d_ref, nm_ref, nv_ref):
        g = p_ref[0].astype(F32)
        for s in range(1, N_DEV):
            g = g + p_ref[s].astype(F32)
        delta, nm, nv = _adamw_math(w_ref[...], g, m_ref[...], v_ref[...])
        g_ref[...] = g
        d_ref[...] = delta
        nm_ref[...] = nm
        nv_ref[...] = nv

    spec = pl.BlockSpec((tm, C), lambda i: (i, 0))
    out = jax.ShapeDtypeStruct((R, C), F32)
    return pl.pallas_call(
        body, name=name, grid=(R // tm,),
        in_specs=[pl.BlockSpec((N_DEV, tm, C), lambda i: (0, i, 0)), spec, spec, spec],
        out_specs=[spec] * 4,
        out_shape=[out] * 4,
        compiler_params=_params(("parallel",)),
    )(parts, w, m, v)


def _pick_rows(n, cap):
    if n < 16:
        return n
    best = None
    for t in range(16, min(n, cap) + 1, 16):
        if n % t == 0:
            best = t
    assert best is not None, (n, cap)
    return best


def _exchange(srcs, scatter, name):
    n = len(srcs)

    def body(*refs):
        src_refs, out_refs = refs[:n], refs[n:2 * n]
        send_sems, recv_sems, local_sems = refs[2 * n:]
        x, y, c = lax.axis_index("x"), lax.axis_index("y"), lax.axis_index("c")
        me = 4 * x + 2 * y + c

        def piece(a, d):
            return src_refs[a].at[d] if scatter else src_refs[a]

        local = [pltpu.make_async_copy(piece(a, me), out_refs[a].at[me], local_sems.at[a]) for a in range(n)]
        for cp in local:
            cp.start()
        copies = []
        for k in range(1, N_DEV):
            where, peer = _peer(k)
            for a in range(n):
                cp = pltpu.make_async_remote_copy(
                    src_ref=piece(a, peer), dst_ref=out_refs[a].at[me],
                    send_sem=send_sems.at[a * N_DEV + k], recv_sem=recv_sems.at[a * N_DEV + k],
                    device_id=where, device_id_type=MESH)
                cp.start()
                copies.append((cp, a, k, peer))
        for cp, a, k, peer in copies:
            cp.wait_send()
            pltpu.make_async_remote_copy(
                src_ref=piece(a, peer), dst_ref=out_refs[a].at[peer],
                send_sem=send_sems.at[a * N_DEV + k], recv_sem=recv_sems.at[a * N_DEV + k],
                device_id=(x, y, c), device_id_type=MESH).wait_recv()
        for cp in local:
            cp.wait()

    slab = lambda s: tuple(s.shape[1:] if scatter else s.shape)
    return pl.pallas_call(
        body, name=name,
        in_specs=[pl.BlockSpec(memory_space=pl.ANY)] * n,
        out_specs=[pl.BlockSpec(memory_space=pl.ANY)] * n,
        out_shape=[jax.ShapeDtypeStruct((N_DEV,) + slab(s), s.dtype) for s in srcs],
        scratch_shapes=[pltpu.SemaphoreType.DMA((n * N_DEV,)), pltpu.SemaphoreType.DMA((n * N_DEV,)),
                        pltpu.SemaphoreType.DMA((n,))],
    )(*srcs)


def _peer(k):
    x, y, c = lax.axis_index("x"), lax.axis_index("y"), lax.axis_index("c")
    px = 1 - x if k & 4 else x
    py = 1 - y if k & 2 else y
    pc = 1 - c if k & 1 else c
    return (px, py, pc), 4 * px + 2 * py + pc


def _split_copies(src_refs, land_refs, send_sems, recv_sems, scatter):
    x, y, c = lax.axis_index("x"), lax.axis_index("y"), lax.axis_index("c")
    me = 4 * x + 2 * y + c
    sends, arrivals = [], []
    for k in range(1, N_DEV):
        where, peer = _peer(k)
        for a, (src, land) in enumerate(zip(src_refs, land_refs)):
            piece = src.at[peer] if scatter else src
            sends.append(pltpu.make_async_remote_copy(
                src_ref=piece, dst_ref=land.at[me], send_sem=send_sems.at[a * N_DEV + k], recv_sem=recv_sems.at[a * N_DEV + k],
                device_id=where, device_id_type=MESH))
            arrivals.append(pltpu.make_async_remote_copy(
                src_ref=piece, dst_ref=land.at[peer], send_sem=send_sems.at[a * N_DEV + k], recv_sem=recv_sems.at[a * N_DEV + k],
                device_id=(x, y, c), device_id_type=MESH))
    return sends, arrivals


def _exchange_start(srcs, scatter, name):
    n = len(srcs)
    slab = lambda s: tuple(s.shape[1:] if scatter else s.shape)
    lands = [lax.empty((N_DEV,) + slab(s), s.dtype) for s in srcs]

    def body(*refs):
        src_refs, land_refs = refs[:n], refs[n:2 * n]
        send_sems, recv_sems = refs[2 * n], refs[2 * n + 1]
        token = refs[-1]
        sends, _ = _split_copies(src_refs, land_refs, send_sems, recv_sems, scatter)
        for cp in sends:
            cp.start()
        token[...] = jnp.zeros_like(token)

    hbm = pl.BlockSpec(memory_space=pltpu.HBM)
    sem = pl.BlockSpec(memory_space=pltpu.SEMAPHORE)
    out = pl.pallas_call(
        body, name=name,
        in_specs=[hbm] * (2 * n),
        out_specs=[sem, sem] + [hbm] * (2 * n) + [pl.BlockSpec(memory_space=pltpu.VMEM)],
        out_shape=[pltpu.SemaphoreType.DMA((n * N_DEV,)), pltpu.SemaphoreType.DMA((n * N_DEV,))]
        + [pltpu.HBM(s.shape, s.dtype) for s in srcs] + [pltpu.HBM(l.shape, l.dtype) for l in lands]
        + [jax.ShapeDtypeStruct((8, LANES), F32)],
        input_output_aliases={j: 2 + j for j in range(2 * n)},
        compiler_params=pltpu.CompilerParams(has_side_effects=pltpu.SideEffectType.DATAFLOW_SIDE_EFFECTING),
    )(*[pltpu.with_memory_space_constraint(s, pltpu.HBM) for s in srcs],
      *[pltpu.with_memory_space_constraint(l, pltpu.HBM) for l in lands])
    return (out[0], out[1], out[2:2 + n], out[2 + n:2 + 2 * n]), out[-1]


def _exchange_finish(state, scatter, after, name):
    send_sems, recv_sems, srcs, lands = state
    n = len(srcs)

    def body(*refs):
        src_refs, land_refs = refs[:n], refs[n:2 * n]
        sends, arrivals = _split_copies(src_refs, land_refs, refs[2 * n], refs[2 * n + 1], scatter)
        for cp in sends:
            cp.wait_send()
        for cp in arrivals:
            cp.wait_recv()

    hbm = pl.BlockSpec(memory_space=pltpu.HBM)
    sem = pl.BlockSpec(memory_space=pltpu.SEMAPHORE)
    out = pl.pallas_call(
        body, name=name,
        in_specs=[hbm] * (2 * n) + [sem, sem, pl.BlockSpec(memory_space=pl.ANY)],
        out_specs=[hbm] * (2 * n),
        out_shape=[pltpu.HBM(s.shape, s.dtype) for s in srcs] + [pltpu.HBM(l.shape, l.dtype) for l in lands],
        input_output_aliases={j: j for j in range(2 * n)},
        compiler_params=pltpu.CompilerParams(has_side_effects=pltpu.SideEffectType.DATAFLOW_SIDE_EFFECTING),
    )(*srcs, *lands, send_sems, recv_sems, after)
    return list(out[n:])


def _own_slab(landed, src, scatter):
    me = 4 * lax.axis_index("x") + 2 * lax.axis_index("y") + lax.axis_index("c")
    own = lax.dynamic_index_in_dim(src, me, axis=0, keepdims=True) if scatter else src[None]
    return lax.dynamic_update_slice_in_dim(landed, own, me, axis=0)


SHARDED = ("w_in", "w_branch_a", "w_branch_b", "w_branch_c", "w_out", "w_up", "w_down")
REPLICATED = ("norm_mix", "b_gate", "q_norm_a", "k_norm_a", "rel_bias_a", "w_pool", "pool_scale", "norm_ffn", "conv_b")
WEIGHTS = ("norm_mix", "w_in", "b_gate", "q_norm_a", "k_norm_a", "rel_bias_a", "w_pool", "pool_scale",
           "w_branch_a", "w_branch_b", "w_branch_c", "w_out", "norm_ffn", "w_up", "conv_w", "conv_b", "w_down")
SMALL_COLS = 128
QKV_COLS = 6 * WIDTH


def _rel_index():
    q_off = jnp.arange(CHUNK)[:, None] + N_LEFT * CHUNK
    k_off = jnp.arange(BAND)[None, :]
    return jnp.clip(q_off - k_off, -(CHUNK - 1), MAX_REL) + (CHUNK - 1)


def _rel_onehot():
    rel = _rel_index().reshape(1, CHUNK * BAND)
    return (rel == jnp.arange(REL_TABLE)[:, None]).astype(BF16)


def _select_mm(x, onehot, mode, name):
    hi = x.astype(BF16)
    r1 = x - hi.astype(F32)
    mid = r1.astype(BF16)
    lo = (r1 - mid.astype(F32)).astype(BF16)
    y = _mm(jnp.concatenate([hi, mid, lo, jnp.zeros_like(hi)], axis=0), onehot, mode, F32, name)
    n = x.shape[0]
    return y[:n] + y[n:2 * n] + y[2 * n:3 * n]


def _pack_rows(arrays, cols, row_multiple):
    flat = jnp.concatenate([a.reshape(-1) for a in arrays])
    rows = -(-flat.shape[0] // cols)
    rows = -(-rows // row_multiple) * row_multiple
    return jnp.pad(flat, (0, rows * cols - flat.shape[0])).reshape(rows, cols)


def _unpack_rows(packed, like):
    flat = packed.reshape(-1)
    out, off = [], 0
    for a in like:
        out.append(flat[off:off + a.size].reshape(a.shape))
        off += a.size
    return out


def kernel(x, norm_mix, w_in, b_gate, q_norm_a, k_norm_a, rel_bias_a, w_pool, pool_scale, w_branch_a, w_branch_b, w_branch_c, w_out, norm_ffn, w_up, conv_w, conv_b, w_down, loss_target, m_norm_mix, m_w_in, m_b_gate, m_q_norm_a, m_k_norm_a, m_rel_bias_a, m_w_pool, m_pool_scale, m_w_branch_a, m_w_branch_b, m_w_branch_c, m_w_out, m_norm_ffn, m_w_up, m_conv_w, m_conv_b, m_w_down, v_norm_mix, v_w_in, v_b_gate, v_q_norm_a, v_k_norm_a, v_rel_bias_a, v_w_pool, v_pool_scale, v_w_branch_a, v_w_branch_b, v_w_branch_c, v_w_out, v_norm_ffn, v_w_up, v_conv_w, v_conv_b, v_w_down):
    args = dict(locals())
    w = {n: args[n] for n in WEIGHTS}
    m = {n: args["m_" + n] for n in WEIGHTS}
    v = {n: args["v_" + n] for n in WEIGHTS}
    L = w_in.shape[0]
    T = x.shape[1]
    xs = x.reshape(T, D_MODEL)
    target = loss_target.reshape(T, D_MODEL)

    exchanged = SHARDED + ("conv_w",)
    row_sharded = ("w_out", "w_down")
    shard = {(n, l): (w[n][l] if n == "conv_w" else w[n][l].astype(BF16)) for n in exchanged for l in range(L)}
    late = [key for key in shard if key != ("w_in", 0)]
    gathered = {("w_in", 0): _exchange([shard["w_in", 0]], False, "gather_first")[0]}
    gather_state, gather_token = _exchange_start([shard[key] for key in late], False, "gather_rest_start")

    def full_weight(key):
        g = gathered[key]
        return g.reshape(-1, g.shape[-1]) if key[0] in row_sharded else g.transpose(1, 0, 2).reshape(g.shape[1], -1)

    w_in_f = {0: full_weight(("w_in", 0))}
    conv_b_f = _ff_pair_order(conv_b)
    onehot = _rel_onehot()

    saved = []
    cur = xs
    full = {}
    for l in range(L):
        w_qkv, w_uc, w_g = w_in_f[l][:, :QKV_COLS], w_in_f[l][:, QKV_COLS:QKV_COLS + WIDTH], w_in_f[l][:, QKV_COLS + WIDTH:]
        gain = norm_mix[l] + gather_token[0, 0] if l == 0 else norm_mix[l]
        h = _rmsnorm_fwd(cur, gain, "norm_mix_fwd")
        qkv = _mm(h, w_qkv, "nn", BF16, "proj_qkv")
        uc = _mm(h, w_uc, "nn", F32, "proj_pool")
        glog = _mm(h, w_g, "nn", F32, "proj_gate")
        table = _band_table(_select_mm(rel_bias_a[l], onehot, "nn", "rel_bias_table").reshape(N_HEADS, CHUNK, BAND))
        oa = _attn_a_fwd(qkv, table, q_norm_a[l], k_norm_a[l])
        ob, carries = _attn_b_fwd(qkv)
        oc = _pool_fwd(uc, w_pool[l], pool_scale[l])
        if l == 0:
            landed = _exchange_finish(gather_state, False, ob, "gather_rest_finish")
            gathered.update({key: _own_slab(g, shard[key], False) for key, g in zip(late, landed)})
            full = {key: full_weight(key) for key in gathered}
            w_in_f.update({k: full["w_in", k] for k in range(1, L)})
        w_a, w_b, w_c = (full["w_branch_" + tag, l] for tag in "abc")
        w_out_f, w_down_f = full["w_out", l], full["w_down", l]
        w_up_f, conv_w_f = _ff_pair_order(full["w_up", l]), _ff_pair_order(full["conv_w", l])
        merged, ya, yb, yc = _merge_fwd(oa, ob, oc, glog, b_gate[l], w_a, w_b, w_c)
        x1 = _mm(merged, w_out_f, "nn", F32, "out_proj", res=cur)
        h2 = _rmsnorm_fwd(x1, norm_ffn[l], "norm_ffn_fwd")
        u = _mm(h2, w_up_f, "nn", F32, "ff_up")
        act = _ff_act_fwd(u, conv_w_f, conv_b_f[l])
        x2 = _mm(act, w_down_f, "nn", F32, "ff_down", res=x1)
        saved.append(dict(x=cur, h=h, qkv=qkv, carries=carries, uc=uc, glog=glog, table=table, oa=oa, ob=ob, oc=oc,
                          ya=ya, yb=yb, yc=yc, merged=merged, x1=x1, h2=h2, u=u, act=act, w_qkv=w_qkv, w_uc=w_uc,
                          w_g=w_g, w_a=w_a, w_b=w_b, w_c=w_c, w_out=w_out_f, w_up=w_up_f, w_down=w_down_f,
                          conv_w=conv_w_f))
        cur = x2

    dcur, loss_local = _loss_head(cur, target)
    loss = lax.psum(loss_local, ("x", "y", "c"))

    def pieces_of(n, g):
        if n in row_sharded:
            return g.reshape(N_DEV, -1, g.shape[-1])
        return g.reshape(g.shape[0], N_DEV, -1).transpose(1, 0, 2)

    gw = {n: [None] * L for n in WEIGHTS}
    for l in reversed(range(L)):
        s = saved[l]
        da = _mm(dcur, s["w_down"], "nt", F32, "ff_down_dx", tn_cap=1408)
        gw["w_down"][l] = _mm(s["act"], dcur, "tn", BF16, "ff_down_dw")
        dc, dconv_w, dconv_b = _ff_act_bwd(s["u"], da, s["conv_w"], conv_b_f[l])
        du = _ff_conv_bwd(dc, s["conv_w"])
        dh2 = _mm(du, s["w_up"], "nt", F32, "ff_up_dx")
        gw["w_up"][l] = _ff_natural_order(_mm(s["h2"], du, "tn", BF16, "ff_up_dw"))
        gw["conv_w"][l] = _ff_natural_order(dconv_w)
        gw["conv_b"][l] = _ff_natural_order(dconv_b)[0]
        dx1, dg = _rmsnorm_bwd(s["x1"], norm_ffn[l], dh2, dcur, "norm_ffn_bwd")
        gw["norm_ffn"][l] = dg[0]

        dmerged = _mm(dx1, s["w_out"], "nt", F32, "out_proj_dx")
        gw["w_out"][l] = _mm(s["merged"], dx1, "tn", BF16, "out_proj_dw")
        dya, dyb, dyc, dglog, db_gate = _merge_bwd(dmerged, s["glog"], b_gate[l], s["ya"], s["yb"], s["yc"])
        gw["b_gate"][l] = db_gate[0]
        do = {}
        for tag, dy, ok in (("a", dya, s["oa"]), ("b", dyb, s["ob"]), ("c", dyc, s["oc"])):
            do[tag] = _mm(dy, s["w_" + tag], "nt", BF16, "branch_dx_" + tag)
            gw["w_branch_" + tag][l] = _mm(ok, dy, "tn", BF16, "branch_dw_" + tag)
        duc, dw_pool, dscale = _pool_bwd(s["uc"], do["c"], w_pool[l], pool_scale[l])
        gw["w_pool"][l] = dw_pool
        gw["pool_scale"][l] = dscale[0]
        gain_q = q_norm_a[l]
        if l == 0:
            early = [(n, k) for n in exchanged for k in range(L) if (n, k) != ("w_in", 0)]
            early_pieces = [pieces_of(n, gw[n][k]) for n, k in early]
            grads_state, grads_token = _exchange_start(early_pieces, True, "exchange_early_start")
            gain_q = gain_q + grads_token[0, 0]
        dqa, dkc, dkp, dvc, dvp, dtable, dgq = _attn_a_bwd(s["qkv"], do["a"], s["table"], gain_q, k_norm_a[l])
        dka, dva, dgk = _attn_a_bwd_keys(s["qkv"], dkc, dkp, dvc, dvp, k_norm_a[l])
        gw["q_norm_a"][l] = jnp.sum(dgq.reshape(N_HEADS, HEAD_DIM), axis=0)
        gw["k_norm_a"][l] = jnp.sum(dgk.reshape(N_HEADS, HEAD_DIM), axis=0)
        gw["rel_bias_a"][l] = _select_mm(_band_table_bwd(dtable).reshape(N_HEADS, CHUNK * BAND), onehot, "nt",
                                         "rel_bias_table_dw")
        dqb, dkb, dvb = _attn_b_bwd(s["qkv"], s["carries"], do["b"])
        dqkv = jnp.concatenate([dqa, dka, dva, dqb, dkb.T.astype(BF16), dvb.T.astype(BF16)], axis=1)
        dh = _mm(dqkv, s["w_qkv"], "nt", F32, "proj_qkv_dx")
        dh = _mm(duc, s["w_uc"], "nt", F32, "proj_pool_dx", res=dh)
        dh = _mm(dglog, s["w_g"], "nt", F32, "proj_gate_dx", res=dh)
        gw["w_in"][l] = jnp.concatenate([_mm(s["h"], dqkv, "tn", BF16, "proj_qkv_dw"),
                                         _mm(s["h"], duc, "tn", BF16, "proj_pool_dw"),
                                         _mm(s["h"], dglog, "tn", BF16, "proj_gate_dw")], axis=1)
        dcur, dg = _rmsnorm_bwd(s["x"], norm_mix[l], dh, dx1, "norm_mix_bwd")
        gw["norm_mix"][l] = dg[0]

    landed = _exchange_finish(grads_state, True, dcur, "exchange_early_finish")
    parts = {key: _own_slab(g, src, True) for key, g, src in zip(early, landed, early_pieces)}
    parts["w_in", 0] = _exchange([pieces_of("w_in", gw["w_in"][0])], True, "exchange_last")[0]
    small = _pack_rows([jnp.stack(gw[n]) for n in REPLICATED], SMALL_COLS, 16)
    small_parts = _exchange([small], False, "gather_small_grads")[0]

    out = {}
    for n in exchanged:
        res = [_adamw(parts[n, l], w[n][l], m[n][l], v[n][l], "adamw_" + n) for l in range(L)]
        out[n] = tuple(jnp.stack(r) for r in zip(*res))
    rep_like = [w[n] for n in REPLICATED]
    res = _adamw(small_parts, *[_pack_rows([d[n] for n in REPLICATED], SMALL_COLS, 16) for d in (w, m, v)],
                 "adamw_replicated")
    out.update({n: r for n, r in zip(REPLICATED, zip(*[_unpack_rows(r, rep_like) for r in res]))})

    grads, deltas, new_m, new_v = ([out[n][i] for n in WEIGHTS] for i in range(4))
    return (loss, dcur.reshape(x.shape), *grads, *deltas, *new_m, *new_v)
```

```python
import functools
import math

import jax
import jax.numpy as jnp
from jax import lax
from jax.experimental import pallas as pl
from jax.experimental.pallas import tpu as pltpu

F32 = jnp.float32
BF16 = jnp.bfloat16

N_DEV = 8
D_MODEL = 1024
N_HEADS = 8
HEAD_DIM = 64
CHUNK = 64
N_LEFT = 8
BAND = (N_LEFT + 1) * CHUNK
WIDTH = N_HEADS * HEAD_DIM
POOL_WINDOWS = (2, 4, 8, 16)
POOL_DIM = 128
MAX_REL = 2 * CHUNK
REL_TABLE = MAX_REL + CHUNK
D_FF = 2816
EPS = 1e-6
SB_SCAN = 256
SB_ROWS = 512
SB_KEYS = 512
A_BLOCK = N_LEFT * CHUNK
HALO = 16
LANES = 128
VMEM_LIMIT = 56 * 1024 * 1024

ADAM_LR = 0.001
ADAM_B1 = 0.9
ADAM_B2 = 0.999
ADAM_EPS = 1e-08
ADAM_WD = 0.01
ADAM_STEP = 10

MESH = pl.DeviceIdType.MESH


def _params(sem):
    return pltpu.CompilerParams(dimension_semantics=sem, vmem_limit_bytes=VMEM_LIMIT)


def _pick(n, cap):
    if n <= cap:
        return n
    best = None
    for t in range(LANES, cap + 1, LANES):
        if n % t == 0:
            best = t
    assert best is not None, (n, cap)
    return best


MM_TILE_CAP = 1408


def _mm(a, b, mode, out_dtype, name, tm=MM_TILE_CAP, tn_cap=MM_TILE_CAP, tk_cap=MM_TILE_CAP, res=None):
    if mode == "nn":
        (M, K), (K2, N) = a.shape, b.shape
    elif mode == "nt":
        (M, K), (N, K2) = a.shape, b.shape
    else:
        (K, M), (K2, N) = a.shape, b.shape
    assert K == K2, (a.shape, b.shape, mode)
    tm = _pick(M, tm)
    tn = _pick(N, tn_cap)
    tk = _pick(K, tk_cap)
    nk = K // tk
    if mode == "nn":
        dims = (((1,), (0,)), ((), ()))
        a_spec = pl.BlockSpec((tm, tk), lambda i, j, k: (i, k))
        b_spec = pl.BlockSpec((tk, tn), lambda i, j, k: (k, j))
    elif mode == "nt":
        dims = (((1,), (1,)), ((), ()))
        a_spec = pl.BlockSpec((tm, tk), lambda i, j, k: (i, k))
        b_spec = pl.BlockSpec((tn, tk), lambda i, j, k: (j, k))
    else:
        dims = (((0,), (0,)), ((), ()))
        a_spec = pl.BlockSpec((tk, tm), lambda i, j, k: (k, i))
        b_spec = pl.BlockSpec((tk, tn), lambda i, j, k: (k, j))

    o_spec = pl.BlockSpec((tm, tn), lambda i, j, k: (i, j))

    def body(a_ref, b_ref, *rest):
        res_ref = rest[0] if res is not None else None
        o_ref = rest[1] if res is not None else rest[0]
        part = lax.dot_general(a_ref[...].astype(BF16), b_ref[...].astype(BF16), dims, preferred_element_type=F32)
        if nk == 1:
            o_ref[...] = (part if res is None else part + res_ref[...]).astype(out_dtype)
            return
        acc_ref = rest[-1]
        k = pl.program_id(2)

        @pl.when(k == 0)
        def _():
            acc_ref[...] = part

        @pl.when(k > 0)
        def _():
            acc_ref[...] += part

        @pl.when(k == nk - 1)
        def _():
            total = acc_ref[...] if res is None else acc_ref[...] + res_ref[...]
            o_ref[...] = total.astype(out_dtype)

    return pl.pallas_call(
        body, name=name,
        grid=(M // tm, N // tn, nk),
        in_specs=[a_spec, b_spec] + ([o_spec] if res is not None else []),
        out_specs=o_spec,
        out_shape=jax.ShapeDtypeStruct((M, N), out_dtype),
        scratch_shapes=[pltpu.VMEM((tm, tn), F32)] if nk > 1 else [],
        compiler_params=_params(("parallel", "parallel", "arbitrary")),
    )(*((a, b) if res is None else (a, b, res)))


def _rmsnorm_fwd(x, gain, name, tm=512):
    T, C = x.shape

    def body(x_ref, g_ref, h_ref):
        xv = x_ref[...]
        r = lax.rsqrt(jnp.mean(xv * xv, axis=-1, keepdims=True) + EPS)
        h_ref[...] = (xv * r * g_ref[...]).astype(BF16)

    return pl.pallas_call(
        body, name=name, grid=(T // tm,),
        in_specs=[pl.BlockSpec((tm, C), lambda i: (i, 0)), pl.BlockSpec((1, C), lambda i: (0, 0))],
        out_specs=pl.BlockSpec((tm, C), lambda i: (i, 0)),
        out_shape=jax.ShapeDtypeStruct((T, C), BF16),
        compiler_params=_params(("parallel",)),
    )(x, gain.reshape(1, C))


def _rmsnorm_bwd(x, gain, dh, dres, name, tm=512):
    T, C = x.shape

    def body(x_ref, g_ref, dh_ref, dres_ref, dx_ref, dg_ref):
        @pl.when(pl.program_id(0) == 0)
        def _():
            dg_ref[...] = jnp.zeros_like(dg_ref)

        xv = x_ref[...]
        dy = dh_ref[...].astype(F32)
        r = lax.rsqrt(jnp.mean(xv * xv, axis=-1, keepdims=True) + EPS)
        gdy = dy * g_ref[...]
        inner = jnp.mean(xv * gdy, axis=-1, keepdims=True)
        dx_ref[...] = dres_ref[...] + r * gdy - xv * (r * r * r * inner)
        dg_ref[...] += jnp.sum(dy * xv * r, axis=0, keepdims=True)

    return pl.pallas_call(
        body, name=name, grid=(T // tm,),
        in_specs=[pl.BlockSpec((tm, C), lambda i: (i, 0)), pl.BlockSpec((1, C), lambda i: (0, 0)),
                  pl.BlockSpec((tm, C), lambda i: (i, 0)), pl.BlockSpec((tm, C), lambda i: (i, 0))],
        out_specs=[pl.BlockSpec((tm, C), lambda i: (i, 0)), pl.BlockSpec((1, C), lambda i: (0, 0))],
        out_shape=[jax.ShapeDtypeStruct((T, C), F32), jax.ShapeDtypeStruct((1, C), F32)],
        compiler_params=_params(("arbitrary",)),
    )(x, gain.reshape(1, C), dh, dres)


MASKED = -1e30


def _pair_sum(x, same_head):
    hi = x.astype(BF16)
    lo = (x - hi.astype(F32)).astype(BF16)
    return jnp.dot(hi, same_head, preferred_element_type=F32) + jnp.dot(lo, same_head, preferred_element_type=F32)


def _same_head():
    r = lax.broadcasted_iota(jnp.int32, (2 * HEAD_DIM, 2 * HEAD_DIM), 0)
    c = lax.broadcasted_iota(jnp.int32, (2 * HEAD_DIM, 2 * HEAD_DIM), 1)
    return jnp.where((r < HEAD_DIM) == (c < HEAD_DIM), 1.0, 0.0).astype(BF16)


def _pair_norm(t, g, same_head):
    tf = t.astype(F32)
    r = lax.rsqrt(_pair_sum(tf * tf, same_head) * (1.0 / HEAD_DIM) + EPS)
    return tf * r * g


def _pair_norm_bwd(t, g, dn, same_head):
    tf = t.astype(F32)
    r = lax.rsqrt(_pair_sum(tf * tf, same_head) * (1.0 / HEAD_DIM) + EPS)
    gd = dn * g
    inner = _pair_sum(tf * gd, same_head) * (1.0 / HEAD_DIM)
    return r * gd - tf * (r * r * r * inner), jnp.sum(dn * tf * r, axis=0, keepdims=True)


def _band_table(bias):
    rows = [jnp.pad(bias, ((0, 0), (0, 0), (c * CHUNK, 2 * A_BLOCK - BAND - c * CHUNK)), constant_values=MASKED)
            for c in range(N_LEFT)]
    return jnp.concatenate(rows, axis=1).reshape(N_HEADS // 2, 2 * A_BLOCK, 2 * A_BLOCK)


def _band_table_bwd(dtable):
    dtable = dtable.reshape(N_HEADS, A_BLOCK, 2 * A_BLOCK)
    return sum(dtable[:, c * CHUNK:(c + 1) * CHUNK, c * CHUNK:c * CHUNK + BAND] for c in range(N_LEFT))


def _a_specs(T):
    nb = T // A_BLOCK
    pairs = N_HEADS // 2
    col = lambda which: which * pairs
    cur = lambda which: pl.BlockSpec((A_BLOCK, 2 * HEAD_DIM), lambda p, i: (i, col(which) + p))
    prev = lambda which: pl.BlockSpec((A_BLOCK, 2 * HEAD_DIM), lambda p, i: (jnp.maximum(i - 1, 0), col(which) + p))
    nxt = lambda which: pl.BlockSpec((A_BLOCK, 2 * HEAD_DIM), lambda p, i: (jnp.minimum(i + 1, nb - 1), col(which) + p))
    table = pl.BlockSpec((1, 2 * A_BLOCK, 2 * A_BLOCK), lambda p, i: (p, 0, 0))
    gain = pl.BlockSpec((1, 2 * HEAD_DIM), lambda p, i: (0, 0))
    gacc = pl.BlockSpec((1, 1, 2 * HEAD_DIM), lambda p, i: (p, 0, 0))
    return nb, pairs, cur, prev, nxt, table, gain, gacc


def _a_probs(q_ref, kc_ref, kp_ref, t_ref, gq_ref, gk_ref, same_head, first):
    scale = 1.0 / math.sqrt(HEAD_DIM)
    qst = _stack_heads((_pair_norm(q_ref[...], gq_ref[...], same_head) * scale).astype(BF16))
    kcat = jnp.concatenate([_pair_norm(kp_ref[...], gk_ref[...], same_head).astype(BF16),
                            _pair_norm(kc_ref[...], gk_ref[...], same_head).astype(BF16)], axis=0)
    s = lax.dot_general(qst, kcat, (((1,), (1,)), ((), ())), preferred_element_type=F32) + t_ref[0]
    col = lax.broadcasted_iota(jnp.int32, s.shape, 1)
    s = jnp.where(col >= jnp.where(first, A_BLOCK, 0), s, MASKED)
    e = jnp.exp(s - jnp.max(s, axis=-1, keepdims=True))
    return qst, kcat, e, jnp.sum(e, axis=-1, keepdims=True)


def _attn_a_fwd(qkv, table, gq, gk):
    T = qkv.shape[0]
    nb, pairs, cur, prev, _, tspec, gspec, _ = _a_specs(T)

    def body(q_ref, kc_ref, kp_ref, vc_ref, vp_ref, t_ref, gq_ref, gk_ref, o_ref):
        same_head = _same_head()
        _, _, e, total = _a_probs(q_ref, kc_ref, kp_ref, t_ref, gq_ref, gk_ref, same_head, pl.program_id(1) == 0)
        vcat = jnp.concatenate([vp_ref[...], vc_ref[...]], axis=0)
        p = (e / total).astype(BF16)
        o_ref[...] = _unstack_heads(jnp.dot(p, vcat, preferred_element_type=F32)).astype(BF16)

    return pl.pallas_call(
        body, name="attn_a_fwd", grid=(pairs, nb),
        in_specs=[cur(0), cur(1), prev(1), cur(2), prev(2), tspec, gspec, gspec],
        out_specs=pl.BlockSpec((A_BLOCK, 2 * HEAD_DIM), lambda p, i: (i, p)),
        out_shape=jax.ShapeDtypeStruct((T, WIDTH), BF16),
        compiler_params=_params(("parallel", "arbitrary")),
    )(qkv, qkv, qkv, qkv, qkv, table, jnp.tile(gq.reshape(1, HEAD_DIM), (1, 2)), jnp.tile(gk.reshape(1, HEAD_DIM), (1, 2)))


def _attn_a_bwd(qkv, do, table, gq, gk):
    T = qkv.shape[0]
    nb, pairs, cur, prev, _, tspec, gspec, gacc = _a_specs(T)
    scale = 1.0 / math.sqrt(HEAD_DIM)
    oblk = pl.BlockSpec((A_BLOCK, 2 * HEAD_DIM), lambda p, i: (i, p))

    def body(q_ref, kc_ref, kp_ref, vc_ref, vp_ref, do_ref, t_ref, gq_ref, gk_ref,
             dq_ref, dkc_ref, dkp_ref, dvc_ref, dvp_ref, dt_ref, dgq_ref):
        first = pl.program_id(1) == 0

        @pl.when(first)
        def _():
            dt_ref[...] = jnp.zeros_like(dt_ref)
            dgq_ref[...] = jnp.zeros_like(dgq_ref)

        same_head = _same_head()
        qst, kcat, e, total = _a_probs(q_ref, kc_ref, kp_ref, t_ref, gq_ref, gk_ref, same_head, first)
        vcat = jnp.concatenate([vp_ref[...], vc_ref[...]], axis=0)
        dost = _stack_heads(do_ref[...])
        p = e / total
        dp = lax.dot_general(dost, vcat, (((1,), (1,)), ((), ())), preferred_element_type=F32)
        ds = p * (dp - jnp.sum(p * dp, axis=-1, keepdims=True))
        dt_ref[0] += ds
        dsb = ds.astype(BF16)
        dqn = _unstack_heads(jnp.dot(dsb, kcat, preferred_element_type=F32)) * scale
        dq, dg = _pair_norm_bwd(q_ref[...], gq_ref[...], dqn, same_head)
        dq_ref[...] = dq.astype(BF16)
        dgq_ref[0] += dg
        dk = lax.dot_general(dsb, qst, (((0,), (0,)), ((), ())), preferred_element_type=F32)
        dv = lax.dot_general(p.astype(BF16), dost, (((0,), (0,)), ((), ())), preferred_element_type=F32)
        dkp_ref[...] = dk[:A_BLOCK]
        dkc_ref[...] = dk[A_BLOCK:]
        dvp_ref[...] = dv[:A_BLOCK]
        dvc_ref[...] = dv[A_BLOCK:]

    wide = jax.ShapeDtypeStruct((T, WIDTH), F32)
    return pl.pallas_call(
        body, name="attn_a_bwd", grid=(pairs, nb),
        in_specs=[cur(0), cur(1), prev(1), cur(2), prev(2), oblk, tspec, gspec, gspec],
        out_specs=[oblk, oblk, oblk, oblk, oblk, tspec, gacc],
        out_shape=[jax.ShapeDtypeStruct((T, WIDTH), BF16), wide, wide, wide, wide,
                   jax.ShapeDtypeStruct((pairs, 2 * A_BLOCK, 2 * A_BLOCK), F32),
                   jax.ShapeDtypeStruct((pairs, 1, 2 * HEAD_DIM), F32)],
        compiler_params=_params(("parallel", "arbitrary")),
    )(qkv, qkv, qkv, qkv, qkv, do, table, jnp.tile(gq.reshape(1, HEAD_DIM), (1, 2)),
      jnp.tile(gk.reshape(1, HEAD_DIM), (1, 2)))


def _attn_a_bwd_keys(qkv, dkc, dkp, dvc, dvp, gk):
    T = qkv.shape[0]
    nb, pairs, cur, _, _, _, gspec, gacc = _a_specs(T)
    oblk = pl.BlockSpec((A_BLOCK, 2 * HEAD_DIM), lambda p, i: (i, p))
    onext = pl.BlockSpec((A_BLOCK, 2 * HEAD_DIM), lambda p, i: (jnp.minimum(i + 1, nb - 1), p))

    def body(k_ref, dkc_ref, dkp_ref, dvc_ref, dvp_ref, gk_ref, dk_ref, dv_ref, dgk_ref):
        i = pl.program_id(1)

        @pl.when(i == 0)
        def _():
            dgk_ref[...] = jnp.zeros_like(dgk_ref)

        has_next = (i < nb - 1).astype(F32)
        dkn = dkc_ref[...] + has_next * dkp_ref[...]
        dk, dg = _pair_norm_bwd(k_ref[...], gk_ref[...], dkn, _same_head())
        dk_ref[...] = dk.astype(BF16)
        dv_ref[...] = (dvc_ref[...] + has_next * dvp_ref[...]).astype(BF16)
        dgk_ref[0] += dg

    blk = jax.ShapeDtypeStruct((T, WIDTH), BF16)
    return pl.pallas_call(
        body, name="attn_a_bwd_keys", grid=(pairs, nb),
        in_specs=[cur(1), oblk, onext, oblk, onext, gspec],
        out_specs=[oblk, oblk, gacc],
        out_shape=[blk, blk, jax.ShapeDtypeStruct((pairs, 1, 2 * HEAD_DIM), F32)],
        compiler_params=_params(("parallel", "arbitrary")),
    )(qkv, dkc, dkp, dvc, dvp, jnp.tile(gk.reshape(1, HEAD_DIM), (1, 2)))


def _scan_matrix(later):
    r = lax.broadcasted_iota(jnp.int32, (SB_SCAN, SB_SCAN), 0)
    c = lax.broadcasted_iota(jnp.int32, (SB_SCAN, SB_SCAN), 1)
    return jnp.where((r > c) if later else (r < c), 1.0, 0.0).astype(BF16)


def _running_sums(x, carry, scan, later):
    n = SB_KEYS // SB_SCAN
    parts = [None] * n
    total = carry
    for sb in (reversed(range(n)) if later else range(n)):
        xs = x[:, sb * SB_SCAN:(sb + 1) * SB_SCAN]
        local = jnp.dot(xs.astype(BF16), scan, preferred_element_type=F32)
        parts[sb] = local if total is None else local + total
        rowsum = jnp.sum(xs, axis=-1, keepdims=True)
        total = rowsum if total is None else total + rowsum
    return (parts[0] if n == 1 else jnp.concatenate(parts, axis=1)), total


def _sb_log_sigmoids(z):
    neg_abs = pltpu.bitcast(pltpu.bitcast(z, jnp.uint32) | jnp.uint32(0x80000000), F32)
    take = jnp.minimum(z, 0.0) - jnp.log(1.0 + jnp.exp(neg_abs))
    return take, take - z


def _sb_mask():
    r = lax.broadcasted_iota(jnp.int32, (2 * SB_ROWS, SB_KEYS), 0)
    c = lax.broadcasted_iota(jnp.int32, (2 * SB_ROWS, SB_KEYS), 1)
    return c < jnp.where(r >= SB_ROWS, r - SB_ROWS, r)


def _stack_heads(t):
    lane = lax.broadcasted_iota(jnp.int32, t.shape, 1)
    zero = jnp.zeros_like(t)
    return jnp.concatenate([jnp.where(lane < HEAD_DIM, t, zero), jnp.where(lane >= HEAD_DIM, t, zero)], axis=0)


def _unstack_heads(t):
    rows = t.shape[0] // 2
    lane = lax.broadcasted_iota(jnp.int32, (rows, 2 * HEAD_DIM), 1)
    return jnp.where(lane < HEAD_DIM, t[:rows], t[rows:])


def _sb_specs(T):
    nq = T // SB_ROWS
    blk = lambda col: pl.BlockSpec((SB_ROWS, 2 * HEAD_DIM), lambda p, i: (i, col + p))
    full = lambda col: pl.BlockSpec((T, 2 * HEAD_DIM), lambda p, i: (0, col + p))
    return nq, blk, full


def _key_rows(j):
    return pl.ds(pl.multiple_of(j * SB_KEYS, SB_KEYS), SB_KEYS)


def _attn_b_fwd(qkv):
    T = qkv.shape[0]
    pairs = N_HEADS // 2
    nq, blk, full = _sb_specs(T)
    scale = 1.0 / math.sqrt(HEAD_DIM)

    assert nq <= LANES

    def body(q_ref, k_ref, v_ref, o_ref, c_ref, acc_ref, carry_ref, z_ref, w_ref):
        i = pl.program_id(1)
        scan = _scan_matrix(True)
        qst = _stack_heads((q_ref[...].astype(F32) * scale).astype(BF16))
        lane = lax.broadcasted_iota(jnp.int32, (2 * SB_ROWS, LANES), 1)

        def scores(j):
            return lax.dot_general(qst, k_ref[_key_rows(j), :], (((1,), (1,)), ((), ())), preferred_element_type=F32)

        def weights(z, carry, mask):
            take, keep = _sb_log_sigmoids(z)
            if mask is not None:
                keep = jnp.where(mask, keep, 0.0)
            tail, total = _running_sums(keep, carry, scan, True)
            w = jnp.exp(take + tail)
            if mask is not None:
                w = jnp.where(mask, w, 0.0)
            return w.astype(BF16), total

        w_ref[...], carry_ref[...] = weights(scores(i), None, _sb_mask())
        z_ref[...] = scores(jnp.maximum(i - 1, 0))
        acc_ref[...] = jnp.zeros_like(acc_ref)
        c_ref[0, 0] = jnp.zeros((2 * SB_ROWS, LANES), F32)

        @pl.loop(0, i)
        def _(jj):
            j = i - 1 - jj
            z = z_ref[...]
            z_ref[...] = scores(jnp.maximum(j - 1, 0))
            acc_ref[...] += jnp.dot(w_ref[...], v_ref[_key_rows(j + 1), :], preferred_element_type=F32)
            carry = carry_ref[...]
            c_ref[0, 0] = jnp.where(lane == j, carry, c_ref[0, 0])
            w_ref[...], carry_ref[...] = weights(z, carry, None)

        acc = acc_ref[...] + jnp.dot(w_ref[...], v_ref[_key_rows(0), :], preferred_element_type=F32)
        o_ref[...] = _unstack_heads(acc).astype(BF16)

    return pl.pallas_call(
        body, name="attn_b_fwd", grid=(pairs, nq),
        in_specs=[blk(3 * pairs), full(4 * pairs), full(5 * pairs)],
        out_specs=[pl.BlockSpec((SB_ROWS, 2 * HEAD_DIM), lambda p, i: (i, p)),
                   pl.BlockSpec((1, 1, 2 * SB_ROWS, LANES), lambda p, i: (p, i, 0, 0))],
        out_shape=[jax.ShapeDtypeStruct((T, WIDTH), BF16), jax.ShapeDtypeStruct((pairs, nq, 2 * SB_ROWS, LANES), F32)],
        scratch_shapes=[pltpu.VMEM((2 * SB_ROWS, 2 * HEAD_DIM), F32), pltpu.VMEM((2 * SB_ROWS, 1), F32),
                        pltpu.VMEM((2 * SB_ROWS, SB_KEYS), F32), pltpu.VMEM((2 * SB_ROWS, SB_KEYS), BF16)],
        compiler_params=_params(("parallel", "arbitrary")),
    )(qkv, qkv, qkv)


def _attn_b_bwd(qkv, carries, do):
    T = qkv.shape[0]
    pairs = N_HEADS // 2
    nq, blk, full = _sb_specs(T)
    scale = 1.0 / math.sqrt(HEAD_DIM)
    oblk = pl.BlockSpec((SB_ROWS, 2 * HEAD_DIM), lambda p, i: (i, p))
    ofull = pl.BlockSpec((2 * HEAD_DIM, T), lambda p, i: (p, 0))

    def body(q_ref, k_ref, v_ref, c_ref, do_ref, dq_ref, dk_ref, dv_ref, dqacc_ref, before_ref,
             z_ref, dw_ref, dz_ref, w_ref):
        i = pl.program_id(1)

        @pl.when(i == 0)
        def _():
            dk_ref[...] = jnp.zeros_like(dk_ref)
            dv_ref[...] = jnp.zeros_like(dv_ref)

        scan_later = _scan_matrix(True)
        scan_earlier = _scan_matrix(False)
        qst = _stack_heads((q_ref[...].astype(F32) * scale).astype(BF16))
        dost = _stack_heads(do_ref[...].astype(BF16))
        lane = lax.broadcasted_iota(jnp.int32, (2 * SB_ROWS, LANES), 1)
        nt = (((1,), (1,)), ((), ()))

        def products(j):
            return (lax.dot_general(qst, k_ref[_key_rows(j), :], nt, preferred_element_type=F32),
                    lax.dot_general(dost, v_ref[_key_rows(j), :], nt, preferred_element_type=F32))

        def score_grads(z, dw, later, mask):
            take, keep = _sb_log_sigmoids(z)
            sig = jnp.exp(take)
            if mask is not None:
                keep = jnp.where(mask, keep, 0.0)
            tail, _ = _running_sums(keep, later, scan_later, True)
            w = jnp.exp(take + tail)
            if mask is not None:
                w = jnp.where(mask, w, 0.0)
            g = w * dw
            before, before_ref[...] = _running_sums(g, before_ref[...], scan_earlier, False)
            dz = g - sig * (g + before)
            if mask is not None:
                dz = jnp.where(mask, dz, 0.0)
            return dz.astype(BF16), w.astype(BF16)

        qst_t = qst.T
        dost_t = dost.T

        def accumulate(j, dzb, wb):
            cols = pl.ds(pl.multiple_of(j * SB_KEYS, SB_KEYS), SB_KEYS)
            dqacc_ref[...] += jnp.dot(dzb, k_ref[_key_rows(j), :], preferred_element_type=F32)
            dk_ref[:, cols] += jnp.dot(qst_t, dzb, preferred_element_type=F32)
            dv_ref[:, cols] += jnp.dot(dost_t, wb, preferred_element_type=F32)

        dqacc_ref[...] = jnp.zeros_like(dqacc_ref)
        before_ref[...] = jnp.zeros_like(before_ref)
        dz_ref[...] = jnp.zeros_like(dz_ref)
        w_ref[...] = jnp.zeros_like(w_ref)
        z_ref[...], dw_ref[...] = products(0)

        @pl.loop(0, i)
        def _(j):
            z, dw = z_ref[...], dw_ref[...]
            z_ref[...], dw_ref[...] = products(j + 1)
            accumulate(jnp.maximum(j - 1, 0), dz_ref[...], w_ref[...])
            later = jnp.sum(jnp.where(lane == j, c_ref[0, 0], 0.0), axis=-1, keepdims=True)
            dz_ref[...], w_ref[...] = score_grads(z, dw, later, None)

        accumulate(jnp.maximum(i - 1, 0), dz_ref[...], w_ref[...])
        accumulate(i, *score_grads(z_ref[...], dw_ref[...], None, _sb_mask()))
        dq_ref[...] = (_unstack_heads(dqacc_ref[...]) * scale).astype(BF16)

    wide = jax.ShapeDtypeStruct((WIDTH, T), F32)
    return pl.pallas_call(
        body, name="attn_b_bwd", grid=(pairs, nq),
        in_specs=[blk(3 * pairs), full(4 * pairs), full(5 * pairs),
                  pl.BlockSpec((1, 1, 2 * SB_ROWS, LANES), lambda p, i: (p, i, 0, 0)), oblk],
        out_specs=[oblk, ofull, ofull],
        out_shape=[jax.ShapeDtypeStruct((T, WIDTH), BF16), wide, wide],
        scratch_shapes=[pltpu.VMEM((2 * SB_ROWS, 2 * HEAD_DIM), F32), pltpu.VMEM((2 * SB_ROWS, 1), F32),
                        pltpu.VMEM((2 * SB_ROWS, SB_KEYS), F32), pltpu.VMEM((2 * SB_ROWS, SB_KEYS), F32),
                        pltpu.VMEM((2 * SB_ROWS, SB_KEYS), BF16), pltpu.VMEM((2 * SB_ROWS, SB_KEYS), BF16)],
        compiler_params=_params(("parallel", "arbitrary")),
    )(qkv, qkv, qkv, carries, do)


def _window_sums(ext, forward):
    n = ext.shape[0]
    out = []
    s = ext
    for step in (1, 2, 4, 8):
        s = s + pltpu.roll(s, (n - step) if forward else step, 0)
        out.append(s)
    return out


def _pool_counts(base, rows, win):
    t = base + lax.broadcasted_iota(jnp.int32, (rows, 1), 0)
    return jnp.minimum(t + 1, win).astype(F32)


def _pooled(u_ref, up_ref, i, tm):
    prev = jnp.where(i > 0, up_ref[...], 0.0)
    ext = jnp.concatenate([prev, u_ref[...]], axis=0)
    sums = _window_sums(ext, False)
    parts = []
    for g, win in enumerate(POOL_WINDOWS):
        cols = slice(g * POOL_DIM, (g + 1) * POOL_DIM)
        cnt = _pool_counts(i * tm, tm, win)
        parts.append(sums[g][HALO:, cols] / cnt - ext[HALO:, cols])
    return parts


def _pool_fwd(ucg, w_pool, scale, tm=512):
    T = ucg.shape[0]
    C = WIDTH

    def body(u_ref, up_ref, w_ref, s_ref, o_ref):
        i = pl.program_id(0)
        parts = _pooled(u_ref, up_ref, i, tm)
        for g in range(len(POOL_WINDOWS)):
            mixed = jnp.dot(parts[g].astype(BF16), w_ref[g], preferred_element_type=F32)
            o_ref[:, g * POOL_DIM:(g + 1) * POOL_DIM] = (mixed * s_ref[:, g * POOL_DIM:(g + 1) * POOL_DIM]).astype(BF16)

    return pl.pallas_call(
        body, name="pool_fwd", grid=(T // tm,),
        in_specs=[pl.BlockSpec((tm, C), lambda i: (i, 0)),
                  pl.BlockSpec((HALO, C), lambda i: (jnp.maximum(i * (tm // HALO) - 1, 0), 0)),
                  pl.BlockSpec((len(POOL_WINDOWS), POOL_DIM, POOL_DIM), lambda i: (0, 0, 0)),
                  pl.BlockSpec((1, C), lambda i: (0, 0))],
        out_specs=pl.BlockSpec((tm, C), lambda i: (i, 0)),
        out_shape=jax.ShapeDtypeStruct((T, C), BF16),
        compiler_params=_params(("parallel",)),
    )(ucg, ucg, w_pool.astype(BF16), scale.reshape(1, C))


def _pool_bwd(ucg, do_c, w_pool, scale, tm=512):
    T = ucg.shape[0]
    C = WIDTH
    nt = T // tm
    G = len(POOL_WINDOWS)

    def body(u_ref, up_ref, do_ref, don_ref, w_ref, s_ref, du_ref, dw_ref, ds_ref):
        i = pl.program_id(0)

        @pl.when(i == 0)
        def _():
            dw_ref[...] = jnp.zeros_like(dw_ref)
            ds_ref[...] = jnp.zeros_like(ds_ref)

        parts = _pooled(u_ref, up_ref, i, tm)
        nxt = jnp.where(i < nt - 1, don_ref[...].astype(F32), 0.0)
        do_ext = jnp.concatenate([do_ref[...].astype(F32), nxt], axis=0) * s_ref[...]
        for g, win in enumerate(POOL_WINDOWS):
            cols = slice(g * POOL_DIM, (g + 1) * POOL_DIM)
            pooled_b = parts[g].astype(BF16)
            dmix = do_ext[:, cols].astype(BF16)
            mixed = jnp.dot(pooled_b, w_ref[g], preferred_element_type=F32)
            ds_ref[:, cols] += jnp.sum(do_ref[:, cols].astype(F32) * mixed, axis=0, keepdims=True)
            dw_ref[g] += lax.dot_general(pooled_b, dmix[:tm], (((0,), (0,)), ((), ())), preferred_element_type=F32)
            dpool = lax.dot_general(dmix, w_ref[g], (((1,), (1,)), ((), ())), preferred_element_type=F32)
            scaled = dpool / _pool_counts(i * tm, tm + HALO, win)
            fwd = _window_sums(scaled, True)[g]
            du_ref[:, cols] = (fwd[:tm] - dpool[:tm]).astype(BF16)

    return pl.pallas_call(
        body, name="pool_bwd", grid=(nt,),
        in_specs=[pl.BlockSpec((tm, C), lambda i: (i, 0)),
                  pl.BlockSpec((HALO, C), lambda i: (jnp.maximum(i * (tm // HALO) - 1, 0), 0)),
                  pl.BlockSpec((tm, C), lambda i: (i, 0)),
                  pl.BlockSpec((HALO, C), lambda i: (jnp.minimum((i + 1) * (tm // HALO), T // HALO - 1), 0)),
                  pl.BlockSpec((G, POOL_DIM, POOL_DIM), lambda i: (0, 0, 0)),
                  pl.BlockSpec((1, C), lambda i: (0, 0))],
        out_specs=[pl.BlockSpec((tm, C), lambda i: (i, 0)),
                   pl.BlockSpec((G, POOL_DIM, POOL_DIM), lambda i: (0, 0, 0)),
                   pl.BlockSpec((1, C), lambda i: (0, 0))],
        out_shape=[jax.ShapeDtypeStruct((T, C), BF16), jax.ShapeDtypeStruct((G, POOL_DIM, POOL_DIM), F32),
                   jax.ShapeDtypeStruct((1, C), F32)],
        compiler_params=_params(("arbitrary",)),
    )(ucg, ucg, do_c, do_c, w_pool.astype(BF16), scale.reshape(1, C))


def _merge_fwd(oa, ob, oc, glog, b_gate, wa, wb, wc, tm=256):
    T = oa.shape[0]
    Dm = D_MODEL
    row = lambda c: pl.BlockSpec((tm, c), lambda i: (i, 0))
    wspec = pl.BlockSpec((WIDTH, Dm), lambda i: (0, 0))

    def body(oa_ref, ob_ref, oc_ref, g_ref, b_ref, wa_ref, wb_ref, wc_ref, m_ref, ya_ref, yb_ref, yc_ref):
        merged = jnp.zeros((tm, Dm), F32)
        for kk, (o_ref, w_ref, y_ref) in enumerate(((oa_ref, wa_ref, ya_ref), (ob_ref, wb_ref, yb_ref),
                                                    (oc_ref, wc_ref, yc_ref))):
            y = jnp.dot(o_ref[...].astype(BF16), w_ref[...], preferred_element_type=F32)
            gate = jax.nn.sigmoid(g_ref[:, kk * Dm:(kk + 1) * Dm] + b_ref[:, kk * Dm:(kk + 1) * Dm])
            merged = merged + gate * y
            y_ref[...] = y.astype(BF16)
        m_ref[...] = merged.astype(BF16)

    out = jax.ShapeDtypeStruct((T, Dm), BF16)
    return pl.pallas_call(
        body, name="merge_fwd", grid=(T // tm,),
        in_specs=[row(WIDTH), row(WIDTH), row(WIDTH), row(3 * Dm), pl.BlockSpec((1, 3 * Dm), lambda i: (0, 0)),
                  wspec, wspec, wspec],
        out_specs=[row(Dm)] * 4,
        out_shape=[out] * 4,
        compiler_params=_params(("parallel",)),
    )(oa, ob, oc, glog, b_gate.reshape(1, 3 * Dm), wa, wb, wc)


def _merge_bwd(dmerged, glog, b_gate, ya, yb, yc, tm=256):
    T = dmerged.shape[0]
    Dm = D_MODEL
    row = lambda c: pl.BlockSpec((tm, c), lambda i: (i, 0))

    def body(dm_ref, g_ref, b_ref, ya_ref, yb_ref, yc_ref, dya_ref, dyb_ref, dyc_ref, dg_ref, db_ref):
        @pl.when(pl.program_id(0) == 0)
        def _():
            db_ref[...] = jnp.zeros_like(db_ref)

        dm = dm_ref[...]
        for kk, (y_ref, dy_ref) in enumerate(((ya_ref, dya_ref), (yb_ref, dyb_ref), (yc_ref, dyc_ref))):
            cols = slice(kk * Dm, (kk + 1) * Dm)
            gate = jax.nn.sigmoid(g_ref[:, cols] + b_ref[:, cols])
            dy_ref[...] = (dm * gate).astype(BF16)
            dlog = dm * y_ref[...].astype(F32) * gate * (1.0 - gate)
            dg_ref[:, cols] = dlog.astype(BF16)
            db_ref[:, cols] += jnp.sum(dlog, axis=0, keepdims=True)

    out = jax.ShapeDtypeStruct((T, Dm), BF16)
    return pl.pallas_call(
        body, name="merge_bwd", grid=(T // tm,),
        in_specs=[row(Dm), row(3 * Dm), pl.BlockSpec((1, 3 * Dm), lambda i: (0, 0)), row(Dm), row(Dm), row(Dm)],
        out_specs=[row(Dm), row(Dm), row(Dm), row(3 * Dm), pl.BlockSpec((1, 3 * Dm), lambda i: (0, 0))],
        out_shape=[out, out, out, jax.ShapeDtypeStruct((T, 3 * Dm), BF16), jax.ShapeDtypeStruct((1, 3 * Dm), F32)],
        compiler_params=_params(("arbitrary",)),
    )(dmerged, glog, b_gate.reshape(1, 3 * Dm), ya, yb, yc)


def _residual_add(x, y, name, tm=512):
    T, C = x.shape

    def body(x_ref, y_ref, o_ref):
        o_ref[...] = x_ref[...] + y_ref[...]

    spec = pl.BlockSpec((tm, C), lambda i: (i, 0))
    return pl.pallas_call(body, name=name, grid=(T // tm,), in_specs=[spec, spec], out_specs=spec,
                          out_shape=jax.ShapeDtypeStruct((T, C), F32), compiler_params=_params(("parallel",)))(x, y)


FF_TILE = 256
FF_TILES = D_FF // FF_TILE
CONV_HALO = 8
FF_ROWS = 32


def _ff_pair_order(w):
    lead = w.shape[:-1]
    n = len(lead)
    w = w.reshape(*lead, 2, FF_TILES, FF_TILE)
    return jnp.swapaxes(w, n, n + 1).reshape(*lead, 2 * D_FF)


def _ff_natural_order(w):
    lead = w.shape[:-1]
    n = len(lead)
    w = w.reshape(*lead, FF_TILES, 2, FF_TILE)
    return jnp.swapaxes(w, n, n + 1).reshape(*lead, 2 * D_FF)


def _conv(ext, w_ref, b_ref):
    c = b_ref[...] + w_ref[2:3, :] * ext
    c = c + w_ref[1:2, :] * pltpu.roll(ext, 1, 0)
    c = c + w_ref[0:1, :] * pltpu.roll(ext, 2, 0)
    return c[CONV_HALO:]


def _ff_specs(T, tm):
    pair = pl.BlockSpec((tm, 2 * FF_TILE), lambda i, j: (i, j))
    prev = pl.BlockSpec((CONV_HALO, 2 * FF_TILE), lambda i, j: (jnp.maximum(i * (tm // CONV_HALO) - 1, 0), j))
    nxt = pl.BlockSpec((CONV_HALO, 2 * FF_TILE),
                       lambda i, j: (jnp.minimum((i + 1) * (tm // CONV_HALO), T // CONV_HALO - 1), j))
    half = pl.BlockSpec((tm, FF_TILE), lambda i, j: (i, j))
    small = lambda r: pl.BlockSpec((r, 2 * FF_TILE), lambda i, j: (0, j))
    return pair, prev, nxt, half, small


def _ff_row_chunks(u_ref, halo, tm, chunk):
    chunk(0, jnp.concatenate([halo, u_ref[0:FF_ROWS, :]], axis=0))

    @pl.loop(1, tm // FF_ROWS)
    def _(c):
        r0 = pl.multiple_of(c * FF_ROWS, FF_ROWS)
        chunk(r0, u_ref[pl.ds(r0 - CONV_HALO, FF_ROWS + CONV_HALO), :])


def _swap_grid(spec):
    return pl.BlockSpec(spec.block_shape, lambda j, i, f=spec.index_map: f(i, j))


def _ff_act_fwd(u, conv_w, conv_b, tm=1024):
    T = u.shape[0]
    pair, prev, _, half, small = _ff_specs(T, tm)

    def body(u_ref, p_ref, w_ref, b_ref, a_ref):
        i = pl.program_id(0)

        def chunk(r0, ext):
            c = _conv(ext, w_ref, b_ref)
            cg, cv = c[:, :FF_TILE], c[:, FF_TILE:]
            a_ref[pl.ds(r0, FF_ROWS), :] = (cg * jax.nn.sigmoid(cg) * cv).astype(BF16)

        _ff_row_chunks(u_ref, jnp.where(i > 0, p_ref[...], 0.0), tm, chunk)

    return pl.pallas_call(
        body, name="ff_act_fwd", grid=(T // tm, FF_TILES),
        in_specs=[pair, prev, small(3), small(1)],
        out_specs=half,
        out_shape=jax.ShapeDtypeStruct((T, D_FF), BF16),
        compiler_params=_params(("parallel", "parallel")),
    )(u, u, conv_w, conv_b.reshape(1, -1))


def _ff_act_bwd(u, da, conv_w, conv_b, tm=1024):
    T = u.shape[0]
    pair, prev, _, half, small = _ff_specs(T, tm)

    def body(u_ref, p_ref, da_ref, w_ref, b_ref, dc_ref, dw_ref, db_ref, sums_ref):
        i = pl.program_id(1)

        @pl.when(i == 0)
        def _():
            dw_ref[...] = jnp.zeros_like(dw_ref)
            db_ref[...] = jnp.zeros_like(db_ref)

        sums_ref[...] = jnp.zeros_like(sums_ref)

        def fold(x):
            return jnp.sum(x.reshape(FF_ROWS // 8, 8, x.shape[-1]), axis=0)

        def chunk(r0, ext):
            c = _conv(ext, w_ref, b_ref)
            cg, cv = c[:, :FF_TILE], c[:, FF_TILE:]
            da = da_ref[pl.ds(r0, FF_ROWS), :]
            sg = jax.nn.sigmoid(cg)
            dc = jnp.concatenate([da * cv * sg * (1.0 + cg * (1.0 - sg)), da * cg * sg], axis=1)
            dc_ref[pl.ds(r0, FF_ROWS), :] = dc
            sums_ref[3] += fold(dc)
            sums_ref[2] += fold(dc * ext[CONV_HALO:])
            sums_ref[1] += fold(dc * pltpu.roll(ext, 1, 0)[CONV_HALO:])
            sums_ref[0] += fold(dc * pltpu.roll(ext, 2, 0)[CONV_HALO:])

        _ff_row_chunks(u_ref, jnp.where(i > 0, p_ref[...], 0.0), tm, chunk)
        for tap in range(3):
            dw_ref[tap:tap + 1, :] += jnp.sum(sums_ref[tap], axis=0, keepdims=True)
        db_ref[...] += jnp.sum(sums_ref[3], axis=0, keepdims=True)

    return pl.pallas_call(
        body, name="ff_act_bwd", grid=(FF_TILES, T // tm),
        in_specs=[_swap_grid(pair), _swap_grid(prev), _swap_grid(half), _swap_grid(small(3)), _swap_grid(small(1))],
        out_specs=[_swap_grid(pair), _swap_grid(small(3)), _swap_grid(small(1))],
        out_shape=[jax.ShapeDtypeStruct((T, 2 * D_FF), F32), jax.ShapeDtypeStruct((3, 2 * D_FF), F32),
                   jax.ShapeDtypeStruct((1, 2 * D_FF), F32)],
        scratch_shapes=[pltpu.VMEM((4, 8, 2 * FF_TILE), F32)],
        compiler_params=_params(("parallel", "arbitrary")),
    )(u, u, da, conv_w, conv_b.reshape(1, -1))


def _ff_conv_bwd(dc, conv_w, tm=1024):
    T = dc.shape[0]
    nt = T // tm
    pair, _, nxt, _, small = _ff_specs(T, tm)

    def body(dc_ref, n_ref, w_ref, du_ref):
        i = pl.program_id(0)
        n = FF_ROWS + CONV_HALO

        def chunk(r0, ext):
            du = (w_ref[2:3, :] * ext + w_ref[1:2, :] * pltpu.roll(ext, n - 1, 0)
                  + w_ref[0:1, :] * pltpu.roll(ext, n - 2, 0))
            du_ref[pl.ds(r0, FF_ROWS), :] = du[:FF_ROWS].astype(BF16)

        @pl.loop(0, tm // FF_ROWS - 1)
        def _(c):
            r0 = pl.multiple_of(c * FF_ROWS, FF_ROWS)
            chunk(r0, dc_ref[pl.ds(r0, n), :])

        last = tm - FF_ROWS
        chunk(last, jnp.concatenate([dc_ref[last:tm, :], jnp.where(i < nt - 1, n_ref[...], 0.0)], axis=0))

    return pl.pallas_call(
        body, name="ff_conv_bwd", grid=(nt, FF_TILES),
        in_specs=[pair, nxt, small(3)],
        out_specs=pair,
        out_shape=jax.ShapeDtypeStruct((T, 2 * D_FF), BF16),
        compiler_params=_params(("parallel", "parallel")),
    )(dc, dc, conv_w)


def _loss_head(y, target, tm=512):
    T, C = y.shape
    nt = T // tm

    def body(y_ref, t_ref, dy_ref, l_ref):
        err = y_ref[...] - t_ref[...]
        dy_ref[...] = err * (1.0 / C)
        part = jnp.sum(err * err, axis=0, keepdims=True) * (0.5 / C)
        l_ref[0] = jnp.broadcast_to(part, (8, C))

    spec = pl.BlockSpec((tm, C), lambda i: (i, 0))
    dy, parts = pl.pallas_call(
        body, name="loss_head", grid=(nt,),
        in_specs=[spec, spec],
        out_specs=[spec, pl.BlockSpec((1, 8, C), lambda i: (i, 0, 0))],
        out_shape=[jax.ShapeDtypeStruct((T, C), F32), jax.ShapeDtypeStruct((nt, 8, C), F32)],
        compiler_params=_params(("parallel",)),
    )(y, target)
    return dy, jnp.sum(parts[:, 0, :])


def _adamw_math(w, g, m, v):
    m = ADAM_B1 * m + (1.0 - ADAM_B1) * g
    v = ADAM_B2 * v + (1.0 - ADAM_B2) * (g * g)
    m_hat = m / (1.0 - ADAM_B1 ** ADAM_STEP)
    v_hat = v / (1.0 - ADAM_B2 ** ADAM_STEP)
    delta = -ADAM_LR * (m_hat / (jnp.sqrt(v_hat) + ADAM_EPS) + ADAM_WD * w)
    return delta, m, v


def _adamw(parts, w, m, v, name, tm=256):
    R, C = w.shape
    tm = _pick_rows(R, tm)

    def body(p_ref, w_ref, m_ref, v_ref, g_ref, d_ref, nm_ref, nv_ref):
        g = p_ref[0].astype(F32)
        for s in range(1, N_DEV):
            g = g + p_ref[s].astype(F32)
        delta, nm, nv = _adamw_math(w_ref[...], g, m_ref[...], v_ref[...])
        g_ref[...] = g
        d_ref[...] = delta
        nm_ref[...] = nm
        nv_ref[...] = nv

    spec = pl.BlockSpec((tm, C), lambda i: (i, 0))
    out = jax.ShapeDtypeStruct((R, C), F32)
    return pl.pallas_call(
        body, name=name, grid=(R // tm,),
        in_specs=[pl.BlockSpec((N_DEV, tm, C), lambda i: (0, i, 0)), spec, spec, spec],
        out_specs=[spec] * 4,
        out_shape=[out] * 4,
        compiler_params=_params(("parallel",)),
    )(parts, w, m, v)


def _pick_rows(n, cap):
    if n < 16:
        return n
    best = None
    for t in range(16, min(n, cap) + 1, 16):
        if n % t == 0:
            best = t
    assert best is not None, (n, cap)
    return best


def _exchange(srcs, scatter, name):
    n = len(srcs)

    def body(*refs):
        src_refs, out_refs = refs[:n], refs[n:2 * n]
        send_sems, recv_sems, local_sems = refs[2 * n:]
        x, y, c = lax.axis_index("x"), lax.axis_index("y"), lax.axis_index("c")
        me = 4 * x + 2 * y + c

        def piece(a, d):
            return src_refs[a].at[d] if scatter else src_refs[a]

        local = [pltpu.make_async_copy(piece(a, me), out_refs[a].at[me], local_sems.at[a]) for a in range(n)]
        for cp in local:
            cp.start()
        copies = []
        for k in range(1, N_DEV):
            where, peer = _peer(k)
            for a in range(n):
                cp = pltpu.make_async_remote_copy(
                    src_ref=piece(a, peer), dst_ref=out_refs[a].at[me],
                    send_sem=send_sems.at[a * N_DEV + k], recv_sem=recv_sems.at[a * N_DEV + k],
                    device_id=where, device_id_type=MESH)
                cp.start()
                copies.append((cp, a, k, peer))
        for cp, a, k, peer in copies:
            cp.wait_send()
            pltpu.make_async_remote_copy(
                src_ref=piece(a, peer), dst_ref=out_refs[a].at[peer],
                send_sem=send_sems.at[a * N_DEV + k], recv_sem=recv_sems.at[a * N_DEV + k],
                device_id=(x, y, c), device_id_type=MESH).wait_recv()
        for cp in local:
            cp.wait()

    slab = lambda s: tuple(s.shape[1:] if scatter else s.shape)
    return pl.pallas_call(
        body, name=name,
        in_specs=[pl.BlockSpec(memory_space=pl.ANY)] * n,
        out_specs=[pl.BlockSpec(memory_space=pl.ANY)] * n,
        out_shape=[jax.ShapeDtypeStruct((N_DEV,) + slab(s), s.dtype) for s in srcs],
        scratch_shapes=[pltpu.SemaphoreType.DMA((n * N_DEV,)), pltpu.SemaphoreType.DMA((n * N_DEV,)),
                        pltpu.SemaphoreType.DMA((n,))],
    )(*srcs)


def _peer(k):
    x, y, c = lax.axis_index("x"), lax.axis_index("y"), lax.axis_index("c")
    px = 1 - x if k & 4 else x
    py = 1 - y if k & 2 else y
    pc = 1 - c if k & 1 else c
    return (px, py, pc), 4 * px + 2 * py + pc


def _split_copies(src_refs, land_refs, send_sems, recv_sems, scatter):
    x, y, c = lax.axis_index("x"), lax.axis_index("y"), lax.axis_index("c")
    me = 4 * x + 2 * y + c
    sends, arrivals = [], []
    for k in range(1, N_DEV):
        where, peer = _peer(k)
        for a, (src, land) in enumerate(zip(src_refs, land_refs)):
            piece = src.at[peer] if scatter else src
            sends.append(pltpu.make_async_remote_copy(
                src_ref=piece, dst_ref=land.at[me], send_sem=send_sems.at[a * N_DEV + k], recv_sem=recv_sems.at[a * N_DEV + k],
                device_id=where, device_id_type=MESH))
            arrivals.append(pltpu.make_async_remote_copy(
                src_ref=piece, dst_ref=land.at[peer], send_sem=send_sems.at[a * N_DEV + k], recv_sem=recv_sems.at[a * N_DEV + k],
                device_id=(x, y, c), device_id_type=MESH))
    return sends, arrivals


def _exchange_start(srcs, scatter, name):
    n = len(srcs)
    slab = lambda s: tuple(s.shape[1:] if scatter else s.shape)
    lands = [lax.empty((N_DEV,) + slab(s), s.dtype) for s in srcs]

    def body(*refs):
        src_refs, land_refs = refs[:n], refs[n:2 * n]
        send_sems, recv_sems = refs[2 * n], refs[2 * n + 1]
        token = refs[-1]
        sends, _ = _split_copies(src_refs, land_refs, send_sems, recv_sems, scatter)
        for cp in sends:
            cp.start()
        token[...] = jnp.zeros_like(token)

    hbm = pl.BlockSpec(memory_space=pltpu.HBM)
    sem = pl.BlockSpec(memory_space=pltpu.SEMAPHORE)
    out = pl.pallas_call(
        body, name=name,
        in_specs=[hbm] * (2 * n),
        out_specs=[sem, sem] + [hbm] * (2 * n) + [pl.BlockSpec(memory_space=pltpu.VMEM)],
        out_shape=[pltpu.SemaphoreType.DMA((n * N_DEV,)), pltpu.SemaphoreType.DMA((n * N_DEV,))]
        + [pltpu.HBM(s.shape, s.dtype) for s in srcs] + [pltpu.HBM(l.shape, l.dtype) for l in lands]
        + [jax.ShapeDtypeStruct((8, LANES), F32)],
        input_output_aliases={j: 2 + j for j in range(2 * n)},
        compiler_params=pltpu.CompilerParams(has_side_effects=pltpu.SideEffectType.DATAFLOW_SIDE_EFFECTING),
    )(*[pltpu.with_memory_space_constraint(s, pltpu.HBM) for s in srcs],
      *[pltpu.with_memory_space_constraint(l, pltpu.HBM) for l in lands])
    return (out[0], out[1], out[2:2 + n], out[2 + n:2 + 2 * n]), out[-1]


def _exchange_finish(state, scatter, after, name):
    send_sems, recv_sems, srcs, lands = state
    n = len(srcs)

    def body(*refs):
        src_refs, land_refs = refs[:n], refs[n:2 * n]
        sends, arrivals = _split_copies(src_refs, land_refs, refs[2 * n], refs[2 * n + 1], scatter)
        for cp in sends:
            cp.wait_send()
        for cp in arrivals:
            cp.wait_recv()

    hbm = pl.BlockSpec(memory_space=pltpu.HBM)
    sem = pl.BlockSpec(memory_space=pltpu.SEMAPHORE)
    out = pl.pallas_call(
        body, name=name,
        in_specs=[hbm] * (2 * n) + [sem, sem, pl.BlockSpec(memory_space=pl.ANY)],
        out_specs=[hbm] * (2 * n),
        out_shape=[pltpu.HBM(s.shape, s.dtype) for s in srcs] + [pltpu.HBM(l.shape, l.dtype) for l in lands],
        input_output_aliases={j: j for j in range(2 * n)},
        compiler_params=pltpu.CompilerParams(has_side_effects=pltpu.SideEffectType.DATAFLOW_SIDE_EFFECTING),
    )(*srcs, *lands, send_sems, recv_sems, after)
    return list(out[n:])


def _own_slab(landed, src, scatter):
    me = 4 * lax.axis_index("x") + 2 * lax.axis_index("y") + lax.axis_index("c")
    own = lax.dynamic_index_in_dim(src, me, axis=0, keepdims=True) if scatter else src[None]
    return lax.dynamic_update_slice_in_dim(landed, own, me, axis=0)


SHARDED = ("w_in", "w_branch_a", "w_branch_b", "w_branch_c", "w_out", "w_up", "w_down")
REPLICATED = ("norm_mix", "b_gate", "q_norm_a", "k_norm_a", "rel_bias_a", "w_pool", "pool_scale", "norm_ffn", "conv_b")
WEIGHTS = ("norm_mix", "w_in", "b_gate", "q_norm_a", "k_norm_a", "rel_bias_a", "w_pool", "pool_scale",
           "w_branch_a", "w_branch_b", "w_branch_c", "w_out", "norm_ffn", "w_up", "conv_w", "conv_b", "w_down")
SMALL_COLS = 128
QKV_COLS = 6 * WIDTH


def _rel_index():
    q_off = jnp.arange(CHUNK)[:, None] + N_LEFT * CHUNK
    k_off = jnp.arange(BAND)[None, :]
    return jnp.clip(q_off - k_off, -(CHUNK - 1), MAX_REL) + (CHUNK - 1)


def _rel_onehot():
    rel = _rel_index().reshape(1, CHUNK * BAND)
    return (rel == jnp.arange(REL_TABLE)[:, None]).astype(BF16)


def _select_mm(x, onehot, mode, name):
    hi = x.astype(BF16)
    r1 = x - hi.astype(F32)
    mid = r1.astype(BF16)
    lo = (r1 - mid.astype(F32)).astype(BF16)
    y = _mm(jnp.concatenate([hi, mid, lo, jnp.zeros_like(hi)], axis=0), onehot, mode, F32, name)
    n = x.shape[0]
    return y[:n] + y[n:2 * n] + y[2 * n:3 * n]


def _pack_rows(arrays, cols, row_multiple):
    flat = jnp.concatenate([a.reshape(-1) for a in arrays])
    rows = -(-flat.shape[0] // cols)
    rows = -(-rows // row_multiple) * row_multiple
    return jnp.pad(flat, (0, rows * cols - flat.shape[0])).reshape(rows, cols)


def _unpack_rows(packed, like):
    flat = packed.reshape(-1)
    out, off = [], 0
    for a in like:
        out.append(flat[off:off + a.size].reshape(a.shape))
        off += a.size
    return out


def kernel(x, norm_mix, w_in, b_gate, q_norm_a, k_norm_a, rel_bias_a, w_pool, pool_scale, w_branch_a, w_branch_b, w_branch_c, w_out, norm_ffn, w_up, conv_w, conv_b, w_down, loss_target, m_norm_mix, m_w_in, m_b_gate, m_q_norm_a, m_k_norm_a, m_rel_bias_a, m_w_pool, m_pool_scale, m_w_branch_a, m_w_branch_b, m_w_branch_c, m_w_out, m_norm_ffn, m_w_up, m_conv_w, m_conv_b, m_w_down, v_norm_mix, v_w_in, v_b_gate, v_q_norm_a, v_k_norm_a, v_rel_bias_a, v_w_pool, v_pool_scale, v_w_branch_a, v_w_branch_b, v_w_branch_c, v_w_out, v_norm_ffn, v_w_up, v_conv_w, v_conv_b, v_w_down):
    args = dict(locals())
    w = {n: args[n] for n in WEIGHTS}
    m = {n: args["m_" + n] for n in WEIGHTS}
    v = {n: args["v_" + n] for n in WEIGHTS}
    L = w_in.shape[0]
    T = x.shape[1]
    xs = x.reshape(T, D_MODEL)
    target = loss_target.reshape(T, D_MODEL)

    exchanged = SHARDED + ("conv_w",)
    row_sharded = ("w_out", "w_down")
    shard = {(n, l): (w[n][l] if n == "conv_w" else w[n][l].astype(BF16)) for n in exchanged for l in range(L)}
    late = [key for key in shard if key != ("w_in", 0)]
    gathered = {("w_in", 0): _exchange([shard["w_in", 0]], False, "gather_first")[0]}
    gather_state, gather_token = _exchange_start([shard[key] for key in late], False, "gather_rest_start")

    def full_weight(key):
        g = gathered[key]
        return g.reshape(-1, g.shape[-1]) if key[0] in row_sharded else g.transpose(1, 0, 2).reshape(g.shape[1], -1)

    w_in_f = {0: full_weight(("w_in", 0))}
    conv_b_f = _ff_pair_order(conv_b)
    onehot = _rel_onehot()

    saved = []
    cur = xs
    full = {}
    for l in range(L):
        w_qkv, w_uc, w_g = w_in_f[l][:, :QKV_COLS], w_in_f[l][:, QKV_COLS:QKV_COLS + WIDTH], w_in_f[l][:, QKV_COLS + WIDTH:]
        gain = norm_mix[l] + gather_token[0, 0] if l == 0 else norm_mix[l]
        h = _rmsnorm_fwd(cur, gain, "norm_mix_fwd")
        qkv = _mm(h, w_qkv, "nn", BF16, "proj_qkv")
        uc = _mm(h, w_uc, "nn", F32, "proj_pool")
        glog = _mm(h, w_g, "nn", F32, "proj_gate")
        table = _band_table(_select_mm(rel_bias_a[l], onehot, "nn", "rel_bias_table").reshape(N_HEADS, CHUNK, BAND))
        oa = _attn_a_fwd(qkv, table, q_norm_a[l], k_norm_a[l])
        ob, carries = _attn_b_fwd(qkv)
        oc = _pool_fwd(uc, w_pool[l], pool_scale[l])
        if l == 0:
            landed = _exchange_finish(gather_state, False, ob, "gather_rest_finish")
            gathered.update({key: _own_slab(g, shard[key], False) for key, g in zip(late, landed)})
            full = {key: full_weight(key) for key in gathered}
            w_in_f.update({k: full["w_in", k] for k in range(1, L)})
        w_a, w_b, w_c = (full["w_branch_" + tag, l] for tag in "abc")
        w_out_f, w_down_f = full["w_out", l], full["w_down", l]
        w_up_f, conv_w_f = _ff_pair_order(full["w_up", l]), _ff_pair_order(full["conv_w", l])
        merged, ya, yb, yc = _merge_fwd(oa, ob, oc, glog, b_gate[l], w_a, w_b, w_c)
        x1 = _mm(merged, w_out_f, "nn", F32, "out_proj", res=cur)
        h2 = _rmsnorm_fwd(x1, norm_ffn[l], "norm_ffn_fwd")
        u = _mm(h2, w_up_f, "nn", F32, "ff_up")
        act = _ff_act_fwd(u, conv_w_f, conv_b_f[l])
        x2 = _mm(act, w_down_f, "nn", F32, "ff_down", res=x1)
        saved.append(dict(x=cur, h=h, qkv=qkv, carries=carries, uc=uc, glog=glog, table=table, oa=oa, ob=ob, oc=oc,
                          ya=ya, yb=yb, yc=yc, merged=merged, x1=x1, h2=h2, u=u, act=act, w_qkv=w_qkv, w_uc=w_uc,
                          w_g=w_g, w_a=w_a, w_b=w_b, w_c=w_c, w_out=w_out_f, w_up=w_up_f, w_down=w_down_f,
                          conv_w=conv_w_f))
        cur = x2

    dcur, loss_local = _loss_head(cur, target)
    loss = lax.psum(loss_local, ("x", "y", "c"))

    def pieces_of(n, g):
        if n in row_sharded:
            return g.reshape(N_DEV, -1, g.shape[-1])
        return g.reshape(g.shape[0], N_DEV, -1).transpose(1, 0, 2)

    gw = {n: [None] * L for n in WEIGHTS}
    for l in reversed(range(L)):
        s = saved[l]
        da = _mm(dcur, s["w_down"], "nt", F32, "ff_down_dx", tn_cap=1408)
        gw["w_down"][l] = _mm(s["act"], dcur, "tn", BF16, "ff_down_dw")
        dc, dconv_w, dconv_b = _ff_act_bwd(s["u"], da, s["conv_w"], conv_b_f[l])
        du = _ff_conv_bwd(dc, s["conv_w"])
        dh2 = _mm(du, s["w_up"], "nt", F32, "ff_up_dx")
        gw["w_up"][l] = _ff_natural_order(_mm(s["h2"], du, "tn", BF16, "ff_up_dw"))
        gw["conv_w"][l] = _ff_natural_order(dconv_w)
        gw["conv_b"][l] = _ff_natural_order(dconv_b)[0]
        dx1, dg = _rmsnorm_bwd(s["x1"], norm_ffn[l], dh2, dcur, "norm_ffn_bwd")
        gw["norm_ffn"][l] = dg[0]

        dmerged = _mm(dx1, s["w_out"], "nt", F32, "out_proj_dx")
        gw["w_out"][l] = _mm(s["merged"], dx1, "tn", BF16, "out_proj_dw")
        dya, dyb, dyc, dglog, db_gate = _merge_bwd(dmerged, s["glog"], b_gate[l], s["ya"], s["yb"], s["yc"])
        gw["b_gate"][l] = db_gate[0]
        do = {}
        for tag, dy, ok in (("a", dya, s["oa"]), ("b", dyb, s["ob"]), ("c", dyc, s["oc"])):
            do[tag] = _mm(dy, s["w_" + tag], "nt", BF16, "branch_dx_" + tag)
            gw["w_branch_" + tag][l] = _mm(ok, dy, "tn", BF16, "branch_dw_" + tag)
        duc, dw_pool, dscale = _pool_bwd(s["uc"], do["c"], w_pool[l], pool_scale[l])
        gw["w_pool"][l] = dw_pool
        gw["pool_scale"][l] = dscale[0]
        gain_q = q_norm_a[l]
        if l == 0:
            early = [(n, k) for n in exchanged for k in range(L) if (n, k) != ("w_in", 0)]
            early_pieces = [pieces_of(n, gw[n][k]) for n, k in early]
            grads_state, grads_token = _exchange_start(early_pieces, True, "exchange_early_start")
            gain_q = gain_q + grads_token[0, 0]
        dqa, dkc, dkp, dvc, dvp, dtable, dgq = _attn_a_bwd(s["qkv"], do["a"], s["table"], gain_q, k_norm_a[l])
        dka, dva, dgk = _attn_a_bwd_keys(s["qkv"], dkc, dkp, dvc, dvp, k_norm_a[l])
        gw["q_norm_a"][l] = jnp.sum(dgq.reshape(N_HEADS, HEAD_DIM), axis=0)
        gw["k_norm_a"][l] = jnp.sum(dgk.reshape(N_HEADS, HEAD_DIM), axis=0)
        gw["rel_bias_a"][l] = _select_mm(_band_table_bwd(dtable).reshape(N_HEADS, CHUNK * BAND), onehot, "nt",
                                         "rel_bias_table_dw")
        dqb, dkb, dvb = _attn_b_bwd(s["qkv"], s["carries"], do["b"])
        dqkv = jnp.concatenate([dqa, dka, dva, dqb, dkb.T.astype(BF16), dvb.T.astype(BF16)], axis=1)
        dh = _mm(dqkv, s["w_qkv"], "nt", F32, "proj_qkv_dx")
        dh = _mm(duc, s["w_uc"], "nt", F32, "proj_pool_dx", res=dh)
        dh = _mm(dglog, s["w_g"], "nt", F32, "proj_gate_dx", res=dh)
        gw["w_in"][l] = jnp.concatenate([_mm(s["h"], dqkv, "tn", BF16, "proj_qkv_dw"),
                                         _mm(s["h"], duc, "tn", BF16, "proj_pool_dw"),
                                         _mm(s["h"], dglog, "tn", BF16, "proj_gate_dw")], axis=1)
        dcur, dg = _rmsnorm_bwd(s["x"], norm_mix[l], dh, dx1, "norm_mix_bwd")
        gw["norm_mix"][l] = dg[0]

    landed = _exchange_finish(grads_state, True, dcur, "exchange_early_finish")
    parts = {key: _own_slab(g, src, True) for key, g, src in zip(early, landed, early_pieces)}
    parts["w_in", 0] = _exchange([pieces_of("w_in", gw["w_in"][0])], True, "exchange_last")[0]
    small = _pack_rows([jnp.stack(gw[n]) for n in REPLICATED], SMALL_COLS, 16)
    small_parts = _exchange([small], False, "gather_small_grads")[0]

    out = {}
    for n in exchanged:
        res = [_adamw(parts[n, l], w[n][l], m[n][l], v[n][l], "adamw_" + n) for l in range(L)]
        out[n] = tuple(jnp.stack(r) for r in zip(*res))
    rep_like = [w[n] for n in REPLICATED]
    res = _adamw(small_parts, *[_pack_rows([d[n] for n in REPLICATED], SMALL_COLS, 16) for d in (w, m, v)],
                 "adamw_replicated")
    out.update({n: r for n, r in zip(REPLICATED, zip(*[_unpack_rows(r, rep_like) for r in res]))})

    grads, deltas, new_m, new_v = ([out[n][i] for n in WEIGHTS] for i in range(4))
    return (loss, dcur.reshape(x.shape), *grads, *deltas, *new_m, *new_v)
```

```python
import functools
import math

import jax
import jax.numpy as jnp
from jax import lax
from jax.experimental import pallas as pl
from jax.experimental.pallas import tpu as pltpu

F32 = jnp.float32
BF16 = jnp.bfloat16

N_DEV = 8
D_MODEL = 1024
N_HEADS = 8
HEAD_DIM = 64
CHUNK = 64
N_LEFT = 8
BAND = (N_LEFT + 1) * CHUNK
WIDTH = N_HEADS * HEAD_DIM
POOL_WINDOWS = (2, 4, 8, 16)
POOL_DIM = 128
MAX_REL = 2 * CHUNK
REL_TABLE = MAX_REL + CHUNK
D_FF = 2816
EPS = 1e-6
SB_SCAN = 256
SB_ROWS = 512
SB_KEYS = 512
A_BLOCK = N_LEFT * CHUNK
HALO = 16
LANES = 128
VMEM_LIMIT = 56 * 1024 * 1024

ADAM_LR = 0.001
ADAM_B1 = 0.9
ADAM_B2 = 0.999
ADAM_EPS = 1e-08
ADAM_WD = 0.01
ADAM_STEP = 10

MESH = pl.DeviceIdType.MESH


def _params(sem):
    return pltpu.CompilerParams(dimension_semantics=sem, vmem_limit_bytes=VMEM_LIMIT)


def _pick(n, cap):
    if n <= cap:
        return n
    best = None
    for t in range(LANES, cap + 1, LANES):
        if n % t == 0:
            best = t
    assert best is not None, (n, cap)
    return best


MM_TILE_CAP = 1408


def _mm(a, b, mode, out_dtype, name, tm=MM_TILE_CAP, tn_cap=MM_TILE_CAP, tk_cap=MM_TILE_CAP, res=None):
    if mode == "nn":
        (M, K), (K2, N) = a.shape, b.shape
    elif mode == "nt":
        (M, K), (N, K2) = a.shape, b.shape
    else:
        (K, M), (K2, N) = a.shape, b.shape
    assert K == K2, (a.shape, b.shape, mode)
    tm = _pick(M, tm)
    tn = _pick(N, tn_cap)
    tk = _pick(K, tk_cap)
    nk = K // tk
    if mode == "nn":
        dims = (((1,), (0,)), ((), ()))
        a_spec = pl.BlockSpec((tm, tk), lambda i, j, k: (i, k))
        b_spec = pl.BlockSpec((tk, tn), lambda i, j, k: (k, j))
    elif mode == "nt":
        dims = (((1,), (1,)), ((), ()))
        a_spec = pl.BlockSpec((tm, tk), lambda i, j, k: (i, k))
        b_spec = pl.BlockSpec((tn, tk), lambda i, j, k: (j, k))
    else:
        dims = (((0,), (0,)), ((), ()))
        a_spec = pl.BlockSpec((tk, tm), lambda i, j, k: (k, i))
        b_spec = pl.BlockSpec((tk, tn), lambda i, j, k: (k, j))

    o_spec = pl.BlockSpec((tm, tn), lambda i, j, k: (i, j))

    def body(a_ref, b_ref, *rest):
        res_ref = rest[0] if res is not None else None
        o_ref = rest[1] if res is not None else rest[0]
        part = lax.dot_general(a_ref[...].astype(BF16), b_ref[...].astype(BF16), dims, preferred_element_type=F32)
        if nk == 1:
            o_ref[...] = (part if res is None else part + res_ref[...]).astype(out_dtype)
            return
        acc_ref = rest[-1]
        k = pl.program_id(2)

        @pl.when(k == 0)
        def _():
            acc_ref[...] = part

        @pl.when(k > 0)
        def _():
            acc_ref[...] += part

        @pl.when(k == nk - 1)
        def _():
            total = acc_ref[...] if res is None else acc_ref[...] + res_ref[...]
            o_ref[...] = total.astype(out_dtype)

    return pl.pallas_call(
        body, name=name,
        grid=(M // tm, N // tn, nk),
        in_specs=[a_spec, b_spec] + ([o_spec] if res is not None else []),
        out_specs=o_spec,
        out_shape=jax.ShapeDtypeStruct((M, N), out_dtype),
        scratch_shapes=[pltpu.VMEM((tm, tn), F32)] if nk > 1 else [],
        compiler_params=_params(("parallel", "parallel", "arbitrary")),
    )(*((a, b) if res is None else (a, b, res)))


def _rmsnorm_fwd(x, gain, name, tm=512):
    T, C = x.shape

    def body(x_ref, g_ref, h_ref):
        xv = x_ref[...]
        r = lax.rsqrt(jnp.mean(xv * xv, axis=-1, keepdims=True) + EPS)
        h_ref[...] = (xv * r * g_ref[...]).astype(BF16)

    return pl.pallas_call(
        body, name=name, grid=(T // tm,),
        in_specs=[pl.BlockSpec((tm, C), lambda i: (i, 0)), pl.BlockSpec((1, C), lambda i: (0, 0))],
        out_specs=pl.BlockSpec((tm, C), lambda i: (i, 0)),
        out_shape=jax.ShapeDtypeStruct((T, C), BF16),
        compiler_params=_params(("parallel",)),
    )(x, gain.reshape(1, C))


def _rmsnorm_bwd(x, gain, dh, dres, name, tm=512):
    T, C = x.shape

    def body(x_ref, g_ref, dh_ref, dres_ref, dx_ref, dg_ref):
        @pl.when(pl.program_id(0) == 0)
        def _():
            dg_ref[...] = jnp.zeros_like(dg_ref)

        xv = x_ref[...]
        dy = dh_ref[...].astype(F32)
        r = lax.rsqrt(jnp.mean(xv * xv, axis=-1, keepdims=True) + EPS)
        gdy = dy * g_ref[...]
        inner = jnp.mean(xv * gdy, axis=-1, keepdims=True)
        dx_ref[...] = dres_ref[...] + r * gdy - xv * (r * r * r * inner)
        dg_ref[...] += jnp.sum(dy * xv * r, axis=0, keepdims=True)

    return pl.pallas_call(
        body, name=name, grid=(T // tm,),
        in_specs=[pl.BlockSpec((tm, C), lambda i: (i, 0)), pl.BlockSpec((1, C), lambda i: (0, 0)),
                  pl.BlockSpec((tm, C), lambda i: (i, 0)), pl.BlockSpec((tm, C), lambda i: (i, 0))],
        out_specs=[pl.BlockSpec((tm, C), lambda i: (i, 0)), pl.BlockSpec((1, C), lambda i: (0, 0))],
        out_shape=[jax.ShapeDtypeStruct((T, C), F32), jax.ShapeDtypeStruct((1, C), F32)],
        compiler_params=_params(("arbitrary",)),
    )(x, gain.reshape(1, C), dh, dres)


MASKED = -1e30


def _pair_sum(x, same_head):
    hi = x.astype(BF16)
    lo = (x - hi.astype(F32)).astype(BF16)
    return jnp.dot(hi, same_head, preferred_element_type=F32) + jnp.dot(lo, same_head, preferred_element_type=F32)


def _same_head():
    r = lax.broadcasted_iota(jnp.int32, (2 * HEAD_DIM, 2 * HEAD_DIM), 0)
    c = lax.broadcasted_iota(jnp.int32, (2 * HEAD_DIM, 2 * HEAD_DIM), 1)
    return jnp.where((r < HEAD_DIM) == (c < HEAD_DIM), 1.0, 0.0).astype(BF16)


def _pair_norm(t, g, same_head):
    tf = t.astype(F32)
    r = lax.rsqrt(_pair_sum(tf * tf, same_head) * (1.0 / HEAD_DIM) + EPS)
    return tf * r * g


def _pair_norm_bwd(t, g, dn, same_head):
    tf = t.astype(F32)
    r = lax.rsqrt(_pair_sum(tf * tf, same_head) * (1.0 / HEAD_DIM) + EPS)
    gd = dn * g
    inner = _pair_sum(tf * gd, same_head) * (1.0 / HEAD_DIM)
    return r * gd - tf * (r * r * r * inner), jnp.sum(dn * tf * r, axis=0, keepdims=True)


def _band_table(bias):
    rows = [jnp.pad(bias, ((0, 0), (0, 0), (c * CHUNK, 2 * A_BLOCK - BAND - c * CHUNK)), constant_values=MASKED)
            for c in range(N_LEFT)]
    return jnp.concatenate(rows, axis=1).reshape(N_HEADS // 2, 2 * A_BLOCK, 2 * A_BLOCK)


def _band_table_bwd(dtable):
    dtable = dtable.reshape(N_HEADS, A_BLOCK, 2 * A_BLOCK)
    return sum(dtable[:, c * CHUNK:(c + 1) * CHUNK, c * CHUNK:c * CHUNK + BAND] for c in range(N_LEFT))


def _a_specs(T):
    nb = T // A_BLOCK
    pairs = N_HEADS // 2
    col = lambda which: which * pairs
    cur = lambda which: pl.BlockSpec((A_BLOCK, 2 * HEAD_DIM), lambda p, i: (i, col(which) + p))
    prev = lambda which: pl.BlockSpec((A_BLOCK, 2 * HEAD_DIM), lambda p, i: (jnp.maximum(i - 1, 0), col(which) + p))
    nxt = lambda which: pl.BlockSpec((A_BLOCK, 2 * HEAD_DIM), lambda p, i: (jnp.minimum(i + 1, nb - 1), col(which) + p))
    table = pl.BlockSpec((1, 2 * A_BLOCK, 2 * A_BLOCK), lambda p, i: (p, 0, 0))
    gain = pl.BlockSpec((1, 2 * HEAD_DIM), lambda p, i: (0, 0))
    gacc = pl.BlockSpec((1, 1, 2 * HEAD_DIM), lambda p, i: (p, 0, 0))
    return nb, pairs, cur, prev, nxt, table, gain, gacc


def _a_probs(q_ref, kc_ref, kp_ref, t_ref, gq_ref, gk_ref, same_head, first):
    scale = 1.0 / math.sqrt(HEAD_DIM)
    qst = _stack_heads((_pair_norm(q_ref[...], gq_ref[...], same_head) * scale).astype(BF16))
    kcat = jnp.concatenate([_pair_norm(kp_ref[...], gk_ref[...], same_head).astype(BF16),
                            _pair_norm(kc_ref[...], gk_ref[...], same_head).astype(BF16)], axis=0)
    s = lax.dot_general(qst, kcat, (((1,), (1,)), ((), ())), preferred_element_type=F32) + t_ref[0]
    col = lax.broadcasted_iota(jnp.int32, s.shape, 1)
    s = jnp.where(col >= jnp.where(first, A_BLOCK, 0), s, MASKED)
    e = jnp.exp(s - jnp.max(s, axis=-1, keepdims=True))
    return qst, kcat, e, jnp.sum(e, axis=-1, keepdims=True)


def _attn_a_fwd(qkv, table, gq, gk):
    T = qkv.shape[0]
    nb, pairs, cur, prev, _, tspec, gspec, _ = _a_specs(T)

    def body(q_ref, kc_ref, kp_ref, vc_ref, vp_ref, t_ref, gq_ref, gk_ref, o_ref):
        same_head = _same_head()
        _, _, e, total = _a_probs(q_ref, kc_ref, kp_ref, t_ref, gq_ref, gk_ref, same_head, pl.program_id(1) == 0)
        vcat = jnp.concatenate([vp_ref[...], vc_ref[...]], axis=0)
        p = (e / total).astype(BF16)
        o_ref[...] = _unstack_heads(jnp.dot(p, vcat, preferred_element_type=F32)).astype(BF16)

    return pl.pallas_call(
        body, name="attn_a_fwd", grid=(pairs, nb),
        in_specs=[cur(0), cur(1), prev(1), cur(2), prev(2), tspec, gspec, gspec],
        out_specs=pl.BlockSpec((A_BLOCK, 2 * HEAD_DIM), lambda p, i: (i, p)),
        out_shape=jax.ShapeDtypeStruct((T, WIDTH), BF16),
        compiler_params=_params(("parallel", "arbitrary")),
    )(qkv, qkv, qkv, qkv, qkv, table, jnp.tile(gq.reshape(1, HEAD_DIM), (1, 2)), jnp.tile(gk.reshape(1, HEAD_DIM), (1, 2)))


def _attn_a_bwd(qkv, do, table, gq, gk):
    T = qkv.shape[0]
    nb, pairs, cur, prev, _, tspec, gspec, gacc = _a_specs(T)
    scale = 1.0 / math.sqrt(HEAD_DIM)
    oblk = pl.BlockSpec((A_BLOCK, 2 * HEAD_DIM), lambda p, i: (i, p))

    def body(q_ref, kc_ref, kp_ref, vc_ref, vp_ref, do_ref, t_ref, gq_ref, gk_ref,
             dq_ref, dkc_ref, dkp_ref, dvc_ref, dvp_ref, dt_ref, dgq_ref):
        first = pl.program_id(1) == 0

        @pl.when(first)
        def _():
            dt_ref[...] = jnp.zeros_like(dt_ref)
            dgq_ref[...] = jnp.zeros_like(dgq_ref)

        same_head = _same_head()
        qst, kcat, e, total = _a_probs(q_ref, kc_ref, kp_ref, t_ref, gq_ref, gk_ref, same_head, first)
        vcat = jnp.concatenate([vp_ref[...], vc_ref[...]], axis=0)
        dost = _stack_heads(do_ref[...])
        p = e / total
        dp = lax.dot_general(dost, vcat, (((1,), (1,)), ((), ())), preferred_element_type=F32)
        ds = p * (dp - jnp.sum(p * dp, axis=-1, keepdims=True))
        dt_ref[0] += ds
        dsb = ds.astype(BF16)
        dqn = _unstack_heads(jnp.dot(dsb, kcat, preferred_element_type=F32)) * scale
        dq, dg = _pair_norm_bwd(q_ref[...], gq_ref[...], dqn, same_head)
        dq_ref[...] = dq.astype(BF16)
        dgq_ref[0] += dg
        dk = lax.dot_general(dsb, qst, (((0,), (0,)), ((), ())), preferred_element_type=F32)
        dv = lax.dot_general(p.astype(BF16), dost, (((0,), (0,)), ((), ())), preferred_element_type=F32)
        dkp_ref[...] = dk[:A_BLOCK]
        dkc_ref[...] = dk[A_BLOCK:]
        dvp_ref[...] = dv[:A_BLOCK]
        dvc_ref[...] = dv[A_BLOCK:]

    wide = jax.ShapeDtypeStruct((T, WIDTH), F32)
    return pl.pallas_call(
        body, name="attn_a_bwd", grid=(pairs, nb),
        in_specs=[cur(0), cur(1), prev(1), cur(2), prev(2), oblk, tspec, gspec, gspec],
        out_specs=[oblk, oblk, oblk, oblk, oblk, tspec, gacc],
        out_shape=[jax.ShapeDtypeStruct((T, WIDTH), BF16), wide, wide, wide, wide,
                   jax.ShapeDtypeStruct((pairs, 2 * A_BLOCK, 2 * A_BLOCK), F32),
                   jax.ShapeDtypeStruct((pairs, 1, 2 * HEAD_DIM), F32)],
        compiler_params=_params(("parallel", "arbitrary")),
    )(qkv, qkv, qkv, qkv, qkv, do, table, jnp.tile(gq.reshape(1, HEAD_DIM), (1, 2)),
      jnp.tile(gk.reshape(1, HEAD_DIM), (1, 2)))


def _attn_a_bwd_keys(qkv, dkc, dkp, dvc, dvp, gk):
    T = qkv.shape[0]
    nb, pairs, cur, _, _, _, gspec, gacc = _a_specs(T)
    oblk = pl.BlockSpec((A_BLOCK, 2 * HEAD_DIM), lambda p, i: (i, p))
    onext = pl.BlockSpec((A_BLOCK, 2 * HEAD_DIM), lambda p, i: (jnp.minimum(i + 1, nb - 1), p))

    def body(k_ref, dkc_ref, dkp_ref, dvc_ref, dvp_ref, gk_ref, dk_ref, dv_ref, dgk_ref):
        i = pl.program_id(1)

        @pl.when(i == 0)
        def _():
            dgk_ref[...] = jnp.zeros_like(dgk_ref)

        has_next = (i < nb - 1).astype(F32)
        dkn = dkc_ref[...] + has_next * dkp_ref[...]
        dk, dg = _pair_norm_bwd(k_ref[...], gk_ref[...], dkn, _same_head())
        dk_ref[...] = dk.astype(BF16)
        dv_ref[...] = (dvc_ref[...] + has_next * dvp_ref[...]).astype(BF16)
        dgk_ref[0] += dg

    blk = jax.ShapeDtypeStruct((T, WIDTH), BF16)
    return pl.pallas_call(
        body, name="attn_a_bwd_keys", grid=(pairs, nb),
        in_specs=[cur(1), oblk, onext, oblk, onext, gspec],
        out_specs=[oblk, oblk, gacc],
        out_shape=[blk, blk, jax.ShapeDtypeStruct((pairs, 1, 2 * HEAD_DIM), F32)],
        compiler_params=_params(("parallel", "arbitrary")),
    )(qkv, dkc, dkp, dvc, dvp, jnp.tile(gk.reshape(1, HEAD_DIM), (1, 2)))


def _scan_matrix(later):
    r = lax.broadcasted_iota(jnp.int32, (SB_SCAN, SB_SCAN), 0)
    c = lax.broadcasted_iota(jnp.int32, (SB_SCAN, SB_SCAN), 1)
    return jnp.where((r > c) if later else (r < c), 1.0, 0.0).astype(BF16)


def _running_sums(x, carry, scan, later):
    n = SB_KEYS // SB_SCAN
    parts = [None] * n
    total = carry
    for sb in (reversed(range(n)) if later else range(n)):
        xs = x[:, sb * SB_SCAN:(sb + 1) * SB_SCAN]
        local = jnp.dot(xs.astype(BF16), scan, preferred_element_type=F32)
        parts[sb] = local if total is None else local + total
        rowsum = jnp.sum(xs, axis=-1, keepdims=True)
        total = rowsum if total is None else total + rowsum
    return (parts[0] if n == 1 else jnp.concatenate(parts, axis=1)), total


def _sb_log_sigmoids(z):
    neg_abs = pltpu.bitcast(pltpu.bitcast(z, jnp.uint32) | jnp.uint32(0x80000000), F32)
    take = jnp.minimum(z, 0.0) - jnp.log(1.0 + jnp.exp(neg_abs))
    return take, take - z


def _sb_mask():
    r = lax.broadcasted_iota(jnp.int32, (2 * SB_ROWS, SB_KEYS), 0)
    c = lax.broadcasted_iota(jnp.int32, (2 * SB_ROWS, SB_KEYS), 1)
    return c < jnp.where(r >= SB_ROWS, r - SB_ROWS, r)


def _stack_heads(t):
    lane = lax.broadcasted_iota(jnp.int32, t.shape, 1)
    zero = jnp.zeros_like(t)
    return jnp.concatenate([jnp.where(lane < HEAD_DIM, t, zero), jnp.where(lane >= HEAD_DIM, t, zero)], axis=0)


def _unstack_heads(t):
    rows = t.shape[0] // 2
    lane = lax.broadcasted_iota(jnp.int32, (rows, 2 * HEAD_DIM), 1)
    return jnp.where(lane < HEAD_DIM, t[:rows], t[rows:])


def _sb_specs(T):
    nq = T // SB_ROWS
    blk = lambda col: pl.BlockSpec((SB_ROWS, 2 * HEAD_DIM), lambda p, i: (i, col + p))
    full = lambda col: pl.BlockSpec((T, 2 * HEAD_DIM), lambda p, i: (0, col + p))
    return nq, blk, full


def _key_rows(j):
    return pl.ds(pl.multiple_of(j * SB_KEYS, SB_KEYS), SB_KEYS)


def _attn_b_fwd(qkv):
    T = qkv.shape[0]
    pairs = N_HEADS // 2
    nq, blk, full = _sb_specs(T)
    scale = 1.0 / math.sqrt(HEAD_DIM)

    assert nq <= LANES

    def body(q_ref, k_ref, v_ref, o_ref, c_ref, acc_ref, carry_ref, z_ref, w_ref):
        i = pl.program_id(1)
        scan = _scan_matrix(True)
        qst = _stack_heads((q_ref[...].astype(F32) * scale).astype(BF16))
        lane = lax.broadcasted_iota(jnp.int32, (2 * SB_ROWS, LANES), 1)

        def scores(j):
            return lax.dot_general(qst, k_ref[_key_rows(j), :], (((1,), (1,)), ((), ())), preferred_element_type=F32)

        def weights(z, carry, mask):
            take, keep = _sb_log_sigmoids(z)
            if mask is not None:
                keep = jnp.where(mask, keep, 0.0)
            tail, total = _running_sums(keep, carry, scan, True)
            w = jnp.exp(take + tail)
            if mask is not None:
                w = jnp.where(mask, w, 0.0)
            return w.astype(BF16), total

        w_ref[...], carry_ref[...] = weights(scores(i), None, _sb_mask())
        z_ref[...] = scores(jnp.maximum(i - 1, 0))
        acc_ref[...] = jnp.zeros_like(acc_ref)
        c_ref[0, 0] = jnp.zeros((2 * SB_ROWS, LANES), F32)

        @pl.loop(0, i)
        def _(jj):
            j = i - 1 - jj
            z = z_ref[...]
            z_ref[...] = scores(jnp.maximum(j - 1, 0))
            acc_ref[...] += jnp.dot(w_ref[...], v_ref[_key_rows(j + 1), :], preferred_element_type=F32)
            carry = carry_ref[...]
            c_ref[0, 0] = jnp.where(lane == j, carry, c_ref[0, 0])
            w_ref[...], carry_ref[...] = weights(z, carry, None)

        acc = acc_ref[...] + jnp.dot(w_ref[...], v_ref[_key_rows(0), :], preferred_element_type=F32)
        o_ref[...] = _unstack_heads(acc).astype(BF16)

    return pl.pallas_call(
        body, name="attn_b_fwd", grid=(pairs, nq),
        in_specs=[blk(3 * pairs), full(4 * pairs), full(5 * pairs)],
        out_specs=[pl.BlockSpec((SB_ROWS, 2 * HEAD_DIM), lambda p, i: (i, p)),
                   pl.BlockSpec((1, 1, 2 * SB_ROWS, LANES), lambda p, i: (p, i, 0, 0))],
        out_shape=[jax.ShapeDtypeStruct((T, WIDTH), BF16), jax.ShapeDtypeStruct((pairs, nq, 2 * SB_ROWS, LANES), F32)],
        scratch_shapes=[pltpu.VMEM((2 * SB_ROWS, 2 * HEAD_DIM), F32), pltpu.VMEM((2 * SB_ROWS, 1), F32),
                        pltpu.VMEM((2 * SB_ROWS, SB_KEYS), F32), pltpu.VMEM((2 * SB_ROWS, SB_KEYS), BF16)],
        compiler_params=_params(("parallel", "arbitrary")),
    )(qkv, qkv, qkv)


def _attn_b_bwd(qkv, carries, do):
    T = qkv.shape[0]
    pairs = N_HEADS // 2
    nq, blk, full = _sb_specs(T)
    scale = 1.0 / math.sqrt(HEAD_DIM)
    oblk = pl.BlockSpec((SB_ROWS, 2 * HEAD_DIM), lambda p, i: (i, p))
    ofull = pl.BlockSpec((2 * HEAD_DIM, T), lambda p, i: (p, 0))

    def body(q_ref, k_ref, v_ref, c_ref, do_ref, dq_ref, dk_ref, dv_ref, dqacc_ref, before_ref,
             z_ref, dw_ref, dz_ref, w_ref):
        i = pl.program_id(1)

        @pl.when(i == 0)
        def _():
            dk_ref[...] = jnp.zeros_like(dk_ref)
            dv_ref[...] = jnp.zeros_like(dv_ref)

        scan_later = _scan_matrix(True)
        scan_earlier = _scan_matrix(False)
        qst = _stack_heads((q_ref[...].astype(F32) * scale).astype(BF16))
        dost = _stack_heads(do_ref[...].astype(BF16))
        lane = lax.broadcasted_iota(jnp.int32, (2 * SB_ROWS, LANES), 1)
        nt = (((1,), (1,)), ((), ()))

        def products(j):
            return (lax.dot_general(qst, k_ref[_key_rows(j), :], nt, preferred_element_type=F32),
                    lax.dot_general(dost, v_ref[_key_rows(j), :], nt, preferred_element_type=F32))

        def score_grads(z, dw, later, mask):
            take, keep = _sb_log_sigmoids(z)
            sig = jnp.exp(take)
            if mask is not None:
                keep = jnp.where(mask, keep, 0.0)
            tail, _ = _running_sums(keep, later, scan_later, True)
            w = jnp.exp(take + tail)
            if mask is not None:
                w = jnp.where(mask, w, 0.0)
            g = w * dw
            before, before_ref[...] = _running_sums(g, before_ref[...], scan_earlier, False)
            dz = g - sig * (g + before)
            if mask is not None:
                dz = jnp.where(mask, dz, 0.0)
            return dz.astype(BF16), w.astype(BF16)

        qst_t = qst.T
        dost_t = dost.T

        def accumulate(j, dzb, wb):
            cols = pl.ds(pl.multiple_of(j * SB_KEYS, SB_KEYS), SB_KEYS)
            dqacc_ref[...] += jnp.dot(dzb, k_ref[_key_rows(j), :], preferred_element_type=F32)
            dk_ref[:, cols] += jnp.dot(qst_t, dzb, preferred_element_type=F32)
            dv_ref[:, cols] += jnp.dot(dost_t, wb, preferred_element_type=F32)

        dqacc_ref[...] = jnp.zeros_like(dqacc_ref)
        before_ref[...] = jnp.zeros_like(before_ref)
        dz_ref[...] = jnp.zeros_like(dz_ref)
        w_ref[...] = jnp.zeros_like(w_ref)
        z_ref[...], dw_ref[...] = products(0)

        @pl.loop(0, i)
        def _(j):
            z, dw = z_ref[...], dw_ref[...]
            z_ref[...], dw_ref[...] = products(j + 1)
            accumulate(jnp.maximum(j - 1, 0), dz_ref[...], w_ref[...])
            later = jnp.sum(jnp.where(lane == j, c_ref[0, 0], 0.0), axis=-1, keepdims=True)
            dz_ref[...], w_ref[...] = score_grads(z, dw, later, None)

        accumulate(jnp.maximum(i - 1, 0), dz_ref[...], w_ref[...])
        accumulate(i, *score_grads(z_ref[...], dw_ref[...], None, _sb_mask()))
        dq_ref[...] = (_unstack_heads(dqacc_ref[...]) * scale).astype(BF16)

    wide = jax.ShapeDtypeStruct((WIDTH, T), F32)
    return pl.pallas_call(
        body, name="attn_b_bwd", grid=(pairs, nq),
        in_specs=[blk(3 * pairs), full(4 * pairs), full(5 * pairs),
                  pl.BlockSpec((1, 1, 2 * SB_ROWS, LANES), lambda p, i: (p, i, 0, 0)), oblk],
        out_specs=[oblk, ofull, ofull],
        out_shape=[jax.ShapeDtypeStruct((T, WIDTH), BF16), wide, wide],
        scratch_shapes=[pltpu.VMEM((2 * SB_ROWS, 2 * HEAD_DIM), F32), pltpu.VMEM((2 * SB_ROWS, 1), F32),
                        pltpu.VMEM((2 * SB_ROWS, SB_KEYS), F32), pltpu.VMEM((2 * SB_ROWS, SB_KEYS), F32),
                        pltpu.VMEM((2 * SB_ROWS, SB_KEYS), BF16), pltpu.VMEM((2 * SB_ROWS, SB_KEYS), BF16)],
        compiler_params=_params(("parallel", "arbitrary")),
    )(qkv, qkv, qkv, carries, do)


def _window_sums(ext, forward):
    n = ext.shape[0]
    out = []
    s = ext
    for step in (1, 2, 4, 8):
        s = s + pltpu.roll(s, (n - step) if forward else step, 0)
        out.append(s)
    return out


def _pool_counts(base, rows, win):
    t = base + lax.broadcasted_iota(jnp.int32, (rows, 1), 0)
    return jnp.minimum(t + 1, win).astype(F32)


def _pooled(u_ref, up_ref, i, tm):
    prev = jnp.where(i > 0, up_ref[...], 0.0)
    ext = jnp.concatenate([prev, u_ref[...]], axis=0)
    sums = _window_sums(ext, False)
    parts = []
    for g, win in enumerate(POOL_WINDOWS):
        cols = slice(g * POOL_DIM, (g + 1) * POOL_DIM)
        cnt = _pool_counts(i * tm, tm, win)
        parts.append(sums[g][HALO:, cols] / cnt - ext[HALO:, cols])
    return parts


def _pool_fwd(ucg, w_pool, scale, tm=512):
    T = ucg.shape[0]
    C = WIDTH

    def body(u_ref, up_ref, w_ref, s_ref, o_ref):
        i = pl.program_id(0)
        parts = _pooled(u_ref, up_ref, i, tm)
        for g in range(len(POOL_WINDOWS)):
            mixed = jnp.dot(parts[g].astype(BF16), w_ref[g], preferred_element_type=F32)
            o_ref[:, g * POOL_DIM:(g + 1) * POOL_DIM] = (mixed * s_ref[:, g * POOL_DIM:(g + 1) * POOL_DIM]).astype(BF16)

    return pl.pallas_call(
        body, name="pool_fwd", grid=(T // tm,),
        in_specs=[pl.BlockSpec((tm, C), lambda i: (i, 0)),
                  pl.BlockSpec((HALO, C), lambda i: (jnp.maximum(i * (tm // HALO) - 1, 0), 0)),
                  pl.BlockSpec((len(POOL_WINDOWS), POOL_DIM, POOL_DIM), lambda i: (0, 0, 0)),
                  pl.BlockSpec((1, C), lambda i: (0, 0))],
        out_specs=pl.BlockSpec((tm, C), lambda i: (i, 0)),
        out_shape=jax.ShapeDtypeStruct((T, C), BF16),
        compiler_params=_params(("parallel",)),
    )(ucg, ucg, w_pool.astype(BF16), scale.reshape(1, C))


def _pool_bwd(ucg, do_c, w_pool, scale, tm=512):
    T = ucg.shape[0]
    C = WIDTH
    nt = T // tm
    G = len(POOL_WINDOWS)

    def body(u_ref, up_ref, do_ref, don_ref, w_ref, s_ref, du_ref, dw_ref, ds_ref):
        i = pl.program_id(0)

        @pl.when(i == 0)
        def _():
            dw_ref[...] = jnp.zeros_like(dw_ref)
            ds_ref[...] = jnp.zeros_like(ds_ref)

        parts = _pooled(u_ref, up_ref, i, tm)
        nxt = jnp.where(i < nt - 1, don_ref[...].astype(F32), 0.0)
        do_ext = jnp.concatenate([do_ref[...].astype(F32), nxt], axis=0) * s_ref[...]
        for g, win in enumerate(POOL_WINDOWS):
            cols = slice(g * POOL_DIM, (g + 1) * POOL_DIM)
            pooled_b = parts[g].astype(BF16)
            dmix = do_ext[:, cols].astype(BF16)
            mixed = jnp.dot(pooled_b, w_ref[g], preferred_element_type=F32)
            ds_ref[:, cols] += jnp.sum(do_ref[:, cols].astype(F32) * mixed, axis=0, keepdims=True)
            dw_ref[g] += lax.dot_general(pooled_b, dmix[:tm], (((0,), (0,)), ((), ())), preferred_element_type=F32)
            dpool = lax.dot_general(dmix, w_ref[g], (((1,), (1,)), ((), ())), preferred_element_type=F32)
            scaled = dpool / _pool_counts(i * tm, tm + HALO, win)
            fwd = _window_sums(scaled, True)[g]
            du_ref[:, cols] = (fwd[:tm] - dpool[:tm]).astype(BF16)

    return pl.pallas_call(
        body, name="pool_bwd", grid=(nt,),
        in_specs=[pl.BlockSpec((tm, C), lambda i: (i, 0)),
                  pl.BlockSpec((HALO, C), lambda i: (jnp.maximum(i * (tm // HALO) - 1, 0), 0)),
                  pl.BlockSpec((tm, C), lambda i: (i, 0)),
                  pl.BlockSpec((HALO, C), lambda i: (jnp.minimum((i + 1) * (tm // HALO), T // HALO - 1), 0)),
                  pl.BlockSpec((G, POOL_DIM, POOL_DIM), lambda i: (0, 0, 0)),
                  pl.BlockSpec((1, C), lambda i: (0, 0))],
        out_specs=[pl.BlockSpec((tm, C), lambda i: (i, 0)),
                   pl.BlockSpec((G, POOL_DIM, POOL_DIM), lambda i: (0, 0, 0)),
                   pl.BlockSpec((1, C), lambda i: (0, 0))],
        out_shape=[jax.ShapeDtypeStruct((T, C), BF16), jax.ShapeDtypeStruct((G, POOL_DIM, POOL_DIM), F32),
                   jax.ShapeDtypeStruct((1, C), F32)],
        compiler_params=_params(("arbitrary",)),
    )(ucg, ucg, do_c, do_c, w_pool.astype(BF16), scale.reshape(1, C))


def _merge_fwd(oa, ob, oc, glog, b_gate, wa, wb, wc, tm=256):
    T = oa.shape[0]
    Dm = D_MODEL
    row = lambda c: pl.BlockSpec((tm, c), lambda i: (i, 0))
    wspec = pl.BlockSpec((WIDTH, Dm), lambda i: (0, 0))

    def body(oa_ref, ob_ref, oc_ref, g_ref, b_ref, wa_ref, wb_ref, wc_ref, m_ref, ya_ref, yb_ref, yc_ref):
        merged = jnp.zeros((tm, Dm), F32)
        for kk, (o_ref, w_ref, y_ref) in enumerate(((oa_ref, wa_ref, ya_ref), (ob_ref, wb_ref, yb_ref),
                                                    (oc_ref, wc_ref, yc_ref))):
            y = jnp.dot(o_ref[...].astype(BF16), w_ref[...], preferred_element_type=F32)
            gate = jax.nn.sigmoid(g_ref[:, kk * Dm:(kk + 1) * Dm] + b_ref[:, kk * Dm:(kk + 1) * Dm])
            merged = merged + gate * y
            y_ref[...] = y.astype(BF16)
        m_ref[...] = merged.astype(BF16)

    out = jax.ShapeDtypeStruct((T, Dm), BF16)
    return pl.pallas_call(
        body, name="merge_fwd", grid=(T // tm,),
        in_specs=[row(WIDTH), row(WIDTH), row(WIDTH), row(3 * Dm), pl.BlockSpec((1, 3 * Dm), lambda i: (0, 0)),
                  wspec, wspec, wspec],
        out_specs=[row(Dm)] * 4,
        out_shape=[out] * 4,
        compiler_params=_params(("parallel",)),
    )(oa, ob, oc, glog, b_gate.reshape(1, 3 * Dm), wa, wb, wc)


def _merge_bwd(dmerged, glog, b_gate, ya, yb, yc, tm=256):
    T = dmerged.shape[0]
    Dm = D_MODEL
    row = lambda c: pl.BlockSpec((tm, c), lambda i: (i, 0))

    def body(dm_ref, g_ref, b_ref, ya_ref, yb_ref, yc_ref, dya_ref, dyb_ref, dyc_ref, dg_ref, db_ref):
        @pl.when(pl.program_id(0) == 0)
        def _():
            db_ref[...] = jnp.zeros_like(db_ref)

        dm = dm_ref[...]
        for kk, (y_ref, dy_ref) in enumerate(((ya_ref, dya_ref), (yb_ref, dyb_ref), (yc_ref, dyc_ref))):
            cols = slice(kk * Dm, (kk + 1) * Dm)
            gate = jax.nn.sigmoid(g_ref[:, cols] + b_ref[:, cols])
            dy_ref[...] = (dm * gate).astype(BF16)
            dlog = dm * y_ref[...].astype(F32) * gate * (1.0 - gate)
            dg_ref[:, cols] = dlog.astype(BF16)
            db_ref[:, cols] += jnp.sum(dlog, axis=0, keepdims=True)

    out = jax.ShapeDtypeStruct((T, Dm), BF16)
    return pl.pallas_call(
        body, name="merge_bwd", grid=(T // tm,),
        in_specs=[row(Dm), row(3 * Dm), pl.BlockSpec((1, 3 * Dm), lambda i: (0, 0)), row(Dm), row(Dm), row(Dm)],
        out_specs=[row(Dm), row(Dm), row(Dm), row(3 * Dm), pl.BlockSpec((1, 3 * Dm), lambda i: (0, 0))],
        out_shape=[out, out, out, jax.ShapeDtypeStruct((T, 3 * Dm), BF16), jax.ShapeDtypeStruct((1, 3 * Dm), F32)],
        compiler_params=_params(("arbitrary",)),
    )(dmerged, glog, b_gate.reshape(1, 3 * Dm), ya, yb, yc)


def _residual_add(x, y, name, tm=512):
    T, C = x.shape

    def body(x_ref, y_ref, o_ref):
        o_ref[...] = x_ref[...] + y_ref[...]

    spec = pl.BlockSpec((tm, C), lambda i: (i, 0))
    return pl.pallas_call(body, name=name, grid=(T // tm,), in_specs=[spec, spec], out_specs=spec,
                          out_shape=jax.ShapeDtypeStruct((T, C), F32), compiler_params=_params(("parallel",)))(x, y)


FF_TILE = 256
FF_TILES = D_FF // FF_TILE
CONV_HALO = 8
FF_ROWS = 32


def _ff_pair_order(w):
    lead = w.shape[:-1]
    n = len(lead)
    w = w.reshape(*lead, 2, FF_TILES, FF_TILE)
    return jnp.swapaxes(w, n, n + 1).reshape(*lead, 2 * D_FF)


def _ff_natural_order(w):
    lead = w.shape[:-1]
    n = len(lead)
    w = w.reshape(*lead, FF_TILES, 2, FF_TILE)
    return jnp.swapaxes(w, n, n + 1).reshape(*lead, 2 * D_FF)


def _conv(ext, w_ref, b_ref):
    c = b_ref[...] + w_ref[2:3, :] * ext
    c = c + w_ref[1:2, :] * pltpu.roll(ext, 1, 0)
    c = c + w_ref[0:1, :] * pltpu.roll(ext, 2, 0)
    return c[CONV_HALO:]


def _ff_specs(T, tm):
    pair = pl.BlockSpec((tm, 2 * FF_TILE), lambda i, j: (i, j))
    prev = pl.BlockSpec((CONV_HALO, 2 * FF_TILE), lambda i, j: (jnp.maximum(i * (tm // CONV_HALO) - 1, 0), j))
    nxt = pl.BlockSpec((CONV_HALO, 2 * FF_TILE),
                       lambda i, j: (jnp.minimum((i + 1) * (tm // CONV_HALO), T // CONV_HALO - 1), j))
    half = pl.BlockSpec((tm, FF_TILE), lambda i, j: (i, j))
    small = lambda r: pl.BlockSpec((r, 2 * FF_TILE), lambda i, j: (0, j))
    return pair, prev, nxt, half, small


def _ff_row_chunks(u_ref, halo, tm, chunk):
    chunk(0, jnp.concatenate([halo, u_ref[0:FF_ROWS, :]], axis=0))

    @pl.loop(1, tm // FF_ROWS)
    def _(c):
        r0 = pl.multiple_of(c * FF_ROWS, FF_ROWS)
        chunk(r0, u_ref[pl.ds(r0 - CONV_HALO, FF_ROWS + CONV_HALO), :])


def _swap_grid(spec):
    return pl.BlockSpec(spec.block_shape, lambda j, i, f=spec.index_map: f(i, j))


def _ff_act_fwd(u, conv_w, conv_b, tm=1024):
    T = u.shape[0]
    pair, prev, _, half, small = _ff_specs(T, tm)

    def body(u_ref, p_ref, w_ref, b_ref, a_ref):
        i = pl.program_id(0)
        c = _conv(jnp.concatenate([jnp.where(i > 0, p_ref[...], 0.0), u_ref[...]], axis=0), w_ref, b_ref)
        cg, cv = c[:, :FF_TILE], c[:, FF_TILE:]
        a_ref[...] = (cg * jax.nn.sigmoid(cg) * cv).astype(BF16)

    return pl.pallas_call(
        body, name="ff_act_fwd", grid=(T // tm, FF_TILES),
        in_specs=[pair, prev, small(3), small(1)],
        out_specs=half,
        out_shape=jax.ShapeDtypeStruct((T, D_FF), BF16),
        compiler_params=_params(("parallel", "parallel")),
    )(u, u, conv_w, conv_b.reshape(1, -1))


def _ff_act_bwd(u, da, conv_w, conv_b, tm=1024):
    T = u.shape[0]
    pair, prev, _, half, small = _ff_specs(T, tm)

    def body(u_ref, p_ref, da_ref, w_ref, b_ref, dc_ref, dw_ref, db_ref, sums_ref):
        i = pl.program_id(1)

        @pl.when(i == 0)
        def _():
            dw_ref[...] = jnp.zeros_like(dw_ref)
            db_ref[...] = jnp.zeros_like(db_ref)

        sums_ref[...] = jnp.zeros_like(sums_ref)

        def fold(x):
            return jnp.sum(x.reshape(FF_ROWS // 8, 8, x.shape[-1]), axis=0)

        def chunk(r0, ext):
            c = _conv(ext, w_ref, b_ref)
            cg, cv = c[:, :FF_TILE], c[:, FF_TILE:]
            da = da_ref[pl.ds(r0, FF_ROWS), :].astype(F32)
            sg = jax.nn.sigmoid(cg)
            dc = jnp.concatenate([da * cv * sg * (1.0 + cg * (1.0 - sg)), da * cg * sg], axis=1)
            dc_ref[pl.ds(r0, FF_ROWS), :] = dc.astype(BF16)
            sums_ref[3] += fold(dc)
            sums_ref[2] += fold(dc * ext[CONV_HALO:])
            sums_ref[1] += fold(dc * pltpu.roll(ext, 1, 0)[CONV_HALO:])
            sums_ref[0] += fold(dc * pltpu.roll(ext, 2, 0)[CONV_HALO:])

        _ff_row_chunks(u_ref, jnp.where(i > 0, p_ref[...], 0.0), tm, chunk)
        for tap in range(3):
            dw_ref[tap:tap + 1, :] += jnp.sum(sums_ref[tap], axis=0, keepdims=True)
        db_ref[...] += jnp.sum(sums_ref[3], axis=0, keepdims=True)

    return pl.pallas_call(
        body, name="ff_act_bwd", grid=(FF_TILES, T // tm),
        in_specs=[_swap_grid(pair), _swap_grid(prev), _swap_grid(half), _swap_grid(small(3)), _swap_grid(small(1))],
        out_specs=[_swap_grid(pair), _swap_grid(small(3)), _swap_grid(small(1))],
        out_shape=[jax.ShapeDtypeStruct((T, 2 * D_FF), BF16), jax.ShapeDtypeStruct((3, 2 * D_FF), F32),
                   jax.ShapeDtypeStruct((1, 2 * D_FF), F32)],
        scratch_shapes=[pltpu.VMEM((4, 8, 2 * FF_TILE), F32)],
        compiler_params=_params(("parallel", "arbitrary")),
    )(u, u, da, conv_w, conv_b.reshape(1, -1))


def _ff_conv_bwd(dc, conv_w, tm=1024):
    T = dc.shape[0]
    nt = T // tm
    pair, _, nxt, _, small = _ff_specs(T, tm)

    halo = 16
    nxt = pl.BlockSpec((halo, 2 * FF_TILE), lambda i, j: (jnp.minimum((i + 1) * (tm // halo), T // halo - 1), j))

    def body(dc_ref, n_ref, w_ref, du_ref):
        i = pl.program_id(0)
        following = jnp.where(i < nt - 1, n_ref[...].astype(F32), 0.0)
        ext = jnp.concatenate([dc_ref[...].astype(F32), following], axis=0)
        n = tm + halo
        du = w_ref[2:3, :] * ext + w_ref[1:2, :] * pltpu.roll(ext, n - 1, 0) + w_ref[0:1, :] * pltpu.roll(ext, n - 2, 0)
        du_ref[...] = du[:tm].astype(BF16)

    return pl.pallas_call(
        body, name="ff_conv_bwd", grid=(nt, FF_TILES),
        in_specs=[pair, nxt, small(3)],
        out_specs=pair,
        out_shape=jax.ShapeDtypeStruct((T, 2 * D_FF), BF16),
        compiler_params=_params(("parallel", "parallel")),
    )(dc, dc, conv_w)


def _loss_head(y, target, tm=512):
    T, C = y.shape
    nt = T // tm

    def body(y_ref, t_ref, dy_ref, l_ref):
        err = y_ref[...] - t_ref[...]
        dy_ref[...] = err * (1.0 / C)
        part = jnp.sum(err * err, axis=0, keepdims=True) * (0.5 / C)
        l_ref[0] = jnp.broadcast_to(part, (8, C))

    spec = pl.BlockSpec((tm, C), lambda i: (i, 0))
    dy, parts = pl.pallas_call(
        body, name="loss_head", grid=(nt,),
        in_specs=[spec, spec],
        out_specs=[spec, pl.BlockSpec((1, 8, C), lambda i: (i, 0, 0))],
        out_shape=[jax.ShapeDtypeStruct((T, C), F32), jax.ShapeDtypeStruct((nt, 8, C), F32)],
        compiler_params=_params(("parallel",)),
    )(y, target)
    return dy, jnp.sum(parts[:, 0, :])


def _adamw_math(w, g, m, v):
    m = ADAM_B1 * m + (1.0 - ADAM_B1) * g
    v = ADAM_B2 * v + (1.0 - ADAM_B2) * (g * g)
    m_hat = m / (1.0 - ADAM_B1 ** ADAM_STEP)
    v_hat = v / (1.0 - ADAM_B2 ** ADAM_STEP)
    delta = -ADAM_LR * (m_hat / (jnp.sqrt(v_hat) + ADAM_EPS) + ADAM_WD * w)
    return delta, m, v


def _adamw(parts, w, m, v, name, tm=256):
    R, C = w.shape
    tm = _pick_rows(R, tm)

    def body(p_ref, w_ref, m_ref, v_ref, g_ref, d_ref, nm_ref, nv_ref):
        g = p_ref[0].astype(F32)
        for s in range(1, N_DEV):
            g = g + p_ref[s].astype(F32)
        delta, nm, nv = _adamw_math(w_ref[...], g, m_ref[...], v_ref[...])
        g_ref[...] = g
        d_ref[...] = delta
        nm_ref[...] = nm
        nv_ref[...] = nv

    spec = pl.BlockSpec((tm, C), lambda i: (i, 0))
    out = jax.ShapeDtypeStruct((R, C), F32)
    return pl.pallas_call(
        body, name=name, grid=(R // tm,),
        in_specs=[pl.BlockSpec((N_DEV, tm, C), lambda i: (0, i, 0)), spec, spec, spec],
        out_specs=[spec] * 4,
        out_shape=[out] * 4,
        compiler_params=_params(("parallel",)),
    )(parts, w, m, v)


def _pick_rows(n, cap):
    if n < 16:
        return n
    best = None
    for t in range(16, min(n, cap) + 1, 16):
        if n % t == 0:
            best = t
    assert best is not None, (n, cap)
    return best


def _exchange(srcs, scatter, name):
    n = len(srcs)

    def body(*refs):
        src_refs, out_refs = refs[:n], refs[n:2 * n]
        send_sems, recv_sems, local_sems = refs[2 * n:]
        x, y, c = lax.axis_index("x"), lax.axis_index("y"), lax.axis_index("c")
        me = 4 * x + 2 * y + c

        def piece(a, d):
            return src_refs[a].at[d] if scatter else src_refs[a]

        local = [pltpu.make_async_copy(piece(a, me), out_refs[a].at[me], local_sems.at[a]) for a in range(n)]
        for cp in local:
            cp.start()
        copies = []
        for k in range(1, N_DEV):
            where, peer = _peer(k)
            for a in range(n):
                cp = pltpu.make_async_remote_copy(
                    src_ref=piece(a, peer), dst_ref=out_refs[a].at[me],
                    send_sem=send_sems.at[a * N_DEV + k], recv_sem=recv_sems.at[a * N_DEV + k],
                    device_id=where, device_id_type=MESH)
                cp.start()
                copies.append((cp, a, k, peer))
        for cp, a, k, peer in copies:
            cp.wait_send()
            pltpu.make_async_remote_copy(
                src_ref=piece(a, peer), dst_ref=out_refs[a].at[peer],
                send_sem=send_sems.at[a * N_DEV + k], recv_sem=recv_sems.at[a * N_DEV + k],
                device_id=(x, y, c), device_id_type=MESH).wait_recv()
        for cp in local:
            cp.wait()

    slab = lambda s: tuple(s.shape[1:] if scatter else s.shape)
    return pl.pallas_call(
        body, name=name,
        in_specs=[pl.BlockSpec(memory_space=pl.ANY)] * n,
        out_specs=[pl.BlockSpec(memory_space=pl.ANY)] * n,
        out_shape=[jax.ShapeDtypeStruct((N_DEV,) + slab(s), s.dtype) for s in srcs],
        scratch_shapes=[pltpu.SemaphoreType.DMA((n * N_DEV,)), pltpu.SemaphoreType.DMA((n * N_DEV,)),
                        pltpu.SemaphoreType.DMA((n,))],
    )(*srcs)


def _peer(k):
    x, y, c = lax.axis_index("x"), lax.axis_index("y"), lax.axis_index("c")
    px = 1 - x if k & 4 else x
    py = 1 - y if k & 2 else y
    pc = 1 - c if k & 1 else c
    return (px, py, pc), 4 * px + 2 * py + pc


def _split_copies(src_refs, land_refs, send_sems, recv_sems, scatter):
    x, y, c = lax.axis_index("x"), lax.axis_index("y"), lax.axis_index("c")
    me = 4 * x + 2 * y + c
    sends, arrivals = [], []
    for k in range(1, N_DEV):
        where, peer = _peer(k)
        for a, (src, land) in enumerate(zip(src_refs, land_refs)):
            piece = src.at[peer] if scatter else src
            sends.append(pltpu.make_async_remote_copy(
                src_ref=piece, dst_ref=land.at[me], send_sem=send_sems.at[a * N_DEV + k], recv_sem=recv_sems.at[a * N_DEV + k],
                device_id=where, device_id_type=MESH))
            arrivals.append(pltpu.make_async_remote_copy(
                src_ref=piece, dst_ref=land.at[peer], send_sem=send_sems.at[a * N_DEV + k], recv_sem=recv_sems.at[a * N_DEV + k],
                device_id=(x, y, c), device_id_type=MESH))
    return sends, arrivals


def _exchange_start(srcs, scatter, name):
    n = len(srcs)
    slab = lambda s: tuple(s.shape[1:] if scatter else s.shape)
    lands = [lax.empty((N_DEV,) + slab(s), s.dtype) for s in srcs]

    def body(*refs):
        src_refs, land_refs = refs[:n], refs[n:2 * n]
        send_sems, recv_sems = refs[2 * n], refs[2 * n + 1]
        token = refs[-1]
        sends, _ = _split_copies(src_refs, land_refs, send_sems, recv_sems, scatter)
        for cp in sends:
            cp.start()
        token[...] = jnp.zeros_like(token)

    hbm = pl.BlockSpec(memory_space=pltpu.HBM)
    sem = pl.BlockSpec(memory_space=pltpu.SEMAPHORE)
    out = pl.pallas_call(
        body, name=name,
        in_specs=[hbm] * (2 * n),
        out_specs=[sem, sem] + [hbm] * (2 * n) + [pl.BlockSpec(memory_space=pltpu.VMEM)],
        out_shape=[pltpu.SemaphoreType.DMA((n * N_DEV,)), pltpu.SemaphoreType.DMA((n * N_DEV,))]
        + [pltpu.HBM(s.shape, s.dtype) for s in srcs] + [pltpu.HBM(l.shape, l.dtype) for l in lands]
        + [jax.ShapeDtypeStruct((8, LANES), F32)],
        input_output_aliases={j: 2 + j for j in range(2 * n)},
        compiler_params=pltpu.CompilerParams(has_side_effects=pltpu.SideEffectType.DATAFLOW_SIDE_EFFECTING),
    )(*[pltpu.with_memory_space_constraint(s, pltpu.HBM) for s in srcs],
      *[pltpu.with_memory_space_constraint(l, pltpu.HBM) for l in lands])
    return (out[0], out[1], out[2:2 + n], out[2 + n:2 + 2 * n]), out[-1]


def _exchange_finish(state, scatter, after, name):
    send_sems, recv_sems, srcs, lands = state
    n = len(srcs)

    def body(*refs):
        src_refs, land_refs = refs[:n], refs[n:2 * n]
        sends, arrivals = _split_copies(src_refs, land_refs, refs[2 * n], refs[2 * n + 1], scatter)
        for cp in sends:
            cp.wait_send()
        for cp in arrivals:
            cp.wait_recv()

    hbm = pl.BlockSpec(memory_space=pltpu.HBM)
    sem = pl.BlockSpec(memory_space=pltpu.SEMAPHORE)
    out = pl.pallas_call(
        body, name=name,
        in_specs=[hbm] * (2 * n) + [sem, sem, pl.BlockSpec(memory_space=pl.ANY)],
        out_specs=[hbm] * (2 * n),
        out_shape=[pltpu.HBM(s.shape, s.dtype) for s in srcs] + [pltpu.HBM(l.shape, l.dtype) for l in lands],
        input_output_aliases={j: j for j in range(2 * n)},
        compiler_params=pltpu.CompilerParams(has_side_effects=pltpu.SideEffectType.DATAFLOW_SIDE_EFFECTING),
    )(*srcs, *lands, send_sems, recv_sems, after)
    return list(out[n:])


def _own_slab(landed, src, scatter):
    me = 4 * lax.axis_index("x") + 2 * lax.axis_index("y") + lax.axis_index("c")
    own = lax.dynamic_index_in_dim(src, me, axis=0, keepdims=True) if scatter else src[None]
    return lax.dynamic_update_slice_in_dim(landed, own, me, axis=0)


SHARDED = ("w_in", "w_branch_a", "w_branch_b", "w_branch_c", "w_out", "w_up", "w_down")
REPLICATED = ("norm_mix", "b_gate", "q_norm_a", "k_norm_a", "rel_bias_a", "w_pool", "pool_scale", "norm_ffn", "conv_b")
WEIGHTS = ("norm_mix", "w_in", "b_gate", "q_norm_a", "k_norm_a", "rel_bias_a", "w_pool", "pool_scale",
           "w_branch_a", "w_branch_b", "w_branch_c", "w_out", "norm_ffn", "w_up", "conv_w", "conv_b", "w_down")
SMALL_COLS = 128
QKV_COLS = 6 * WIDTH


def _rel_index():
    q_off = jnp.arange(CHUNK)[:, None] + N_LEFT * CHUNK
    k_off = jnp.arange(BAND)[None, :]
    return jnp.clip(q_off - k_off, -(CHUNK - 1), MAX_REL) + (CHUNK - 1)


def _rel_onehot():
    rel = _rel_index().reshape(1, CHUNK * BAND)
    return (rel == jnp.arange(REL_TABLE)[:, None]).astype(BF16)


def _select_mm(x, onehot, mode, name):
    hi = x.astype(BF16)
    r1 = x - hi.astype(F32)
    mid = r1.astype(BF16)
    lo = (r1 - mid.astype(F32)).astype(BF16)
    y = _mm(jnp.concatenate([hi, mid, lo, jnp.zeros_like(hi)], axis=0), onehot, mode, F32, name)
    n = x.shape[0]
    return y[:n] + y[n:2 * n] + y[2 * n:3 * n]


def _pack_rows(arrays, cols, row_multiple):
    flat = jnp.concatenate([a.reshape(-1) for a in arrays])
    rows = -(-flat.shape[0] // cols)
    rows = -(-rows // row_multiple) * row_multiple
    return jnp.pad(flat, (0, rows * cols - flat.shape[0])).reshape(rows, cols)


def _unpack_rows(packed, like):
    flat = packed.reshape(-1)
    out, off = [], 0
    for a in like:
        out.append(flat[off:off + a.size].reshape(a.shape))
        off += a.size
    return out


def kernel(x, norm_mix, w_in, b_gate, q_norm_a, k_norm_a, rel_bias_a, w_pool, pool_scale, w_branch_a, w_branch_b, w_branch_c, w_out, norm_ffn, w_up, conv_w, conv_b, w_down, loss_target, m_norm_mix, m_w_in, m_b_gate, m_q_norm_a, m_k_norm_a, m_rel_bias_a, m_w_pool, m_pool_scale, m_w_branch_a, m_w_branch_b, m_w_branch_c, m_w_out, m_norm_ffn, m_w_up, m_conv_w, m_conv_b, m_w_down, v_norm_mix, v_w_in, v_b_gate, v_q_norm_a, v_k_norm_a, v_rel_bias_a, v_w_pool, v_pool_scale, v_w_branch_a, v_w_branch_b, v_w_branch_c, v_w_out, v_norm_ffn, v_w_up, v_conv_w, v_conv_b, v_w_down):
    args = dict(locals())
    w = {n: args[n] for n in WEIGHTS}
    m = {n: args["m_" + n] for n in WEIGHTS}
    v = {n: args["v_" + n] for n in WEIGHTS}
    L = w_in.shape[0]
    T = x.shape[1]
    xs = x.reshape(T, D_MODEL)
    target = loss_target.reshape(T, D_MODEL)

    exchanged = SHARDED + ("conv_w",)
    row_sharded = ("w_out", "w_down")
    shard = {(n, l): (w[n][l] if n == "conv_w" else w[n][l].astype(BF16)) for n in exchanged for l in range(L)}
    late = [key for key in shard if key != ("w_in", 0)]
    gathered = {("w_in", 0): _exchange([shard["w_in", 0]], False, "gather_first")[0]}
    gather_state, gather_token = _exchange_start([shard[key] for key in late], False, "gather_rest_start")

    def full_weight(key):
        g = gathered[key]
        return g.reshape(-1, g.shape[-1]) if key[0] in row_sharded else g.transpose(1, 0, 2).reshape(g.shape[1], -1)

    w_in_f = {0: full_weight(("w_in", 0))}
    conv_b_f = _ff_pair_order(conv_b)
    onehot = _rel_onehot()

    saved = []
    cur = xs
    full = {}
    for l in range(L):
        w_qkv, w_uc, w_g = w_in_f[l][:, :QKV_COLS], w_in_f[l][:, QKV_COLS:QKV_COLS + WIDTH], w_in_f[l][:, QKV_COLS + WIDTH:]
        gain = norm_mix[l] + gather_token[0, 0] if l == 0 else norm_mix[l]
        h = _rmsnorm_fwd(cur, gain, "norm_mix_fwd")
        qkv = _mm(h, w_qkv, "nn", BF16, "proj_qkv")
        uc = _mm(h, w_uc, "nn", F32, "proj_pool")
        glog = _mm(h, w_g, "nn", F32, "proj_gate")
        table = _band_table(_select_mm(rel_bias_a[l], onehot, "nn", "rel_bias_table").reshape(N_HEADS, CHUNK, BAND))
        oa = _attn_a_fwd(qkv, table, q_norm_a[l], k_norm_a[l])
        ob, carries = _attn_b_fwd(qkv)
        oc = _pool_fwd(uc, w_pool[l], pool_scale[l])
        if l == 0:
            landed = _exchange_finish(gather_state, False, ob, "gather_rest_finish")
            gathered.update({key: _own_slab(g, shard[key], False) for key, g in zip(late, landed)})
            full = {key: full_weight(key) for key in gathered}
            w_in_f.update({k: full["w_in", k] for k in range(1, L)})
        w_a, w_b, w_c = (full["w_branch_" + tag, l] for tag in "abc")
        w_out_f, w_down_f = full["w_out", l], full["w_down", l]
        w_up_f, conv_w_f = _ff_pair_order(full["w_up", l]), _ff_pair_order(full["conv_w", l])
        merged, ya, yb, yc = _merge_fwd(oa, ob, oc, glog, b_gate[l], w_a, w_b, w_c)
        x1 = _mm(merged, w_out_f, "nn", F32, "out_proj", res=cur)
        h2 = _rmsnorm_fwd(x1, norm_ffn[l], "norm_ffn_fwd")
        u = _mm(h2, w_up_f, "nn", F32, "ff_up")
        act = _ff_act_fwd(u, conv_w_f, conv_b_f[l])
        x2 = _mm(act, w_down_f, "nn", F32, "ff_down", res=x1)
        saved.append(dict(x=cur, h=h, qkv=qkv, carries=carries, uc=uc, glog=glog, table=table, oa=oa, ob=ob, oc=oc,
                          ya=ya, yb=yb, yc=yc, merged=merged, x1=x1, h2=h2, u=u, act=act, w_qkv=w_qkv, w_uc=w_uc,
                          w_g=w_g, w_a=w_a, w_b=w_b, w_c=w_c, w_out=w_out_f, w_up=w_up_f, w_down=w_down_f,
                          conv_w=conv_w_f))
        cur = x2

    dcur, loss_local = _loss_head(cur, target)
    loss = lax.psum(loss_local, ("x", "y", "c"))

    def pieces_of(n, g):
        if n in row_sharded:
            return g.reshape(N_DEV, -1, g.shape[-1])
        return g.reshape(g.shape[0], N_DEV, -1).transpose(1, 0, 2)

    gw = {n: [None] * L for n in WEIGHTS}
    for l in reversed(range(L)):
        s = saved[l]
        da = _mm(dcur, s["w_down"], "nt", BF16, "ff_down_dx", tn_cap=1408)
        gw["w_down"][l] = _mm(s["act"], dcur, "tn", BF16, "ff_down_dw")
        dc, dconv_w, dconv_b = _ff_act_bwd(s["u"], da, s["conv_w"], conv_b_f[l])
        du = _ff_conv_bwd(dc, s["conv_w"])
        dh2 = _mm(du, s["w_up"], "nt", F32, "ff_up_dx")
        gw["w_up"][l] = _ff_natural_order(_mm(s["h2"], du, "tn", BF16, "ff_up_dw"))
        gw["conv_w"][l] = _ff_natural_order(dconv_w)
        gw["conv_b"][l] = _ff_natural_order(dconv_b)[0]
        dx1, dg = _rmsnorm_bwd(s["x1"], norm_ffn[l], dh2, dcur, "norm_ffn_bwd")
        gw["norm_ffn"][l] = dg[0]

        dmerged = _mm(dx1, s["w_out"], "nt", F32, "out_proj_dx")
        gw["w_out"][l] = _mm(s["merged"], dx1, "tn", BF16, "out_proj_dw")
        dya, dyb, dyc, dglog, db_gate = _merge_bwd(dmerged, s["glog"], b_gate[l], s["ya"], s["yb"], s["yc"])
        gw["b_gate"][l] = db_gate[0]
        do = {}
        for tag, dy, ok in (("a", dya, s["oa"]), ("b", dyb, s["ob"]), ("c", dyc, s["oc"])):
            do[tag] = _mm(dy, s["w_" + tag], "nt", BF16, "branch_dx_" + tag)
            gw["w_branch_" + tag][l] = _mm(ok, dy, "tn", BF16, "branch_dw_" + tag)
        duc, dw_pool, dscale = _pool_bwd(s["uc"], do["c"], w_pool[l], pool_scale[l])
        gw["w_pool"][l] = dw_pool
        gw["pool_scale"][l] = dscale[0]
        gain_q = q_norm_a[l]
        if l == 0:
            early = [(n, k) for n in exchanged for k in range(L) if (n, k) != ("w_in", 0)]
            early_pieces = [pieces_of(n, gw[n][k]) for n, k in early]
            grads_state, grads_token = _exchange_start(early_pieces, True, "exchange_early_start")
            gain_q = gain_q + grads_token[0, 0]
        dqa, dkc, dkp, dvc, dvp, dtable, dgq = _attn_a_bwd(s["qkv"], do["a"], s["table"], gain_q, k_norm_a[l])
        dka, dva, dgk = _attn_a_bwd_keys(s["qkv"], dkc, dkp, dvc, dvp, k_norm_a[l])
        gw["q_norm_a"][l] = jnp.sum(dgq.reshape(N_HEADS, HEAD_DIM), axis=0)
        gw["k_norm_a"][l] = jnp.sum(dgk.reshape(N_HEADS, HEAD_DIM), axis=0)
        gw["rel_bias_a"][l] = _select_mm(_band_table_bwd(dtable).reshape(N_HEADS, CHUNK * BAND), onehot, "nt",
                                         "rel_bias_table_dw")
        dqb, dkb, dvb = _attn_b_bwd(s["qkv"], s["carries"], do["b"])
        dqkv = jnp.concatenate([dqa, dka, dva, dqb, dkb.T.astype(BF16), dvb.T.astype(BF16)], axis=1)
        dh = _mm(dqkv, s["w_qkv"], "nt", F32, "proj_qkv_dx")
        dh = _mm(duc, s["w_uc"], "nt", F32, "proj_pool_dx", res=dh)
        dh = _mm(dglog, s["w_g"], "nt", F32, "proj_gate_dx", res=dh)
        gw["w_in"][l] = jnp.concatenate([_mm(s["h"], dqkv, "tn", BF16, "proj_qkv_dw"),
                                         _mm(s["h"], duc, "tn", BF16, "proj_pool_dw"),
                                         _mm(s["h"], dglog, "tn", BF16, "proj_gate_dw")], axis=1)
        dcur, dg = _rmsnorm_bwd(s["x"], norm_mix[l], dh, dx1, "norm_mix_bwd")
        gw["norm_mix"][l] = dg[0]

    landed = _exchange_finish(grads_state, True, dcur, "exchange_early_finish")
    parts = {key: _own_slab(g, src, True) for key, g, src in zip(early, landed, early_pieces)}
    parts["w_in", 0] = _exchange([pieces_of("w_in", gw["w_in"][0])], True, "exchange_last")[0]
    small = _pack_rows([jnp.stack(gw[n]) for n in REPLICATED], SMALL_COLS, 16)
    small_parts = _exchange([small], False, "gather_small_grads")[0]

    out = {}
    for n in exchanged:
        res = [_adamw(parts[n, l], w[n][l], m[n][l], v[n][l], "adamw_" + n) for l in range(L)]
        out[n] = tuple(jnp.stack(r) for r in zip(*res))
    rep_like = [w[n] for n in REPLICATED]
    res = _adamw(small_parts, *[_pack_rows([d[n] for n in REPLICATED], SMALL_COLS, 16) for d in (w, m, v)],
                 "adamw_replicated")
    out.update({n: r for n, r in zip(REPLICATED, zip(*[_unpack_rows(r, rep_like) for r in res]))})

    grads, deltas, new_m, new_v = ([out[n][i] for n in WEIGHTS] for i in range(4))
    return (loss, dcur.reshape(x.shape), *grads, *deltas, *new_m, *new_v)
```

```python
import functools
import math

import jax
import jax.numpy as jnp
from jax import lax
from jax.experimental import pallas as pl
from jax.experimental.pallas import tpu as pltpu

F32 = jnp.float32
BF16 = jnp.bfloat16

N_DEV = 8
D_MODEL = 1024
N_HEADS = 8
HEAD_DIM = 64
CHUNK = 64
N_LEFT = 8
BAND = (N_LEFT + 1) * CHUNK
WIDTH = N_HEADS * HEAD_DIM
POOL_WINDOWS = (2, 4, 8, 16)
POOL_DIM = 128
MAX_REL = 2 * CHUNK
REL_TABLE = MAX_REL + CHUNK
D_FF = 2816
EPS = 1e-6
SB_SCAN = 256
SB_ROWS = 512
SB_KEYS = 512
A_BLOCK = N_LEFT * CHUNK
HALO = 16
LANES = 128
VMEM_LIMIT = 56 * 1024 * 1024

ADAM_LR = 0.001
ADAM_B1 = 0.9
ADAM_B2 = 0.999
ADAM_EPS = 1e-08
ADAM_WD = 0.01
ADAM_STEP = 10

MESH = pl.DeviceIdType.MESH


def _params(sem):
    return pltpu.CompilerParams(dimension_semantics=sem, vmem_limit_bytes=VMEM_LIMIT)


def _pick(n, cap):
    if n <= cap:
        return n
    best = None
    for t in range(LANES, cap + 1, LANES):
        if n % t == 0:
            best = t
    assert best is not None, (n, cap)
    return best


MM_TILE_CAP = 1408


def _mm(a, b, mode, out_dtype, name, tm=MM_TILE_CAP, tn_cap=MM_TILE_CAP, tk_cap=MM_TILE_CAP, res=None):
    if mode == "nn":
        (M, K), (K2, N) = a.shape, b.shape
    elif mode == "nt":
        (M, K), (N, K2) = a.shape, b.shape
    else:
        (K, M), (K2, N) = a.shape, b.shape
    assert K == K2, (a.shape, b.shape, mode)
    tm = _pick(M, tm)
    tn = _pick(N, tn_cap)
    tk = _pick(K, tk_cap)
    nk = K // tk
    if mode == "nn":
        dims = (((1,), (0,)), ((), ()))
        a_spec = pl.BlockSpec((tm, tk), lambda i, j, k: (i, k))
        b_spec = pl.BlockSpec((tk, tn), lambda i, j, k: (k, j))
    elif mode == "nt":
        dims = (((1,), (1,)), ((), ()))
        a_spec = pl.BlockSpec((tm, tk), lambda i, j, k: (i, k))
        b_spec = pl.BlockSpec((tn, tk), lambda i, j, k: (j, k))
    else:
        dims = (((0,), (0,)), ((), ()))
        a_spec = pl.BlockSpec((tk, tm), lambda i, j, k: (k, i))
        b_spec = pl.BlockSpec((tk, tn), lambda i, j, k: (k, j))

    o_spec = pl.BlockSpec((tm, tn), lambda i, j, k: (i, j))

    def body(a_ref, b_ref, *rest):
        res_ref = rest[0] if res is not None else None
        o_ref = rest[1] if res is not None else rest[0]
        part = lax.dot_general(a_ref[...].astype(BF16), b_ref[...].astype(BF16), dims, preferred_element_type=F32)
        if nk == 1:
            o_ref[...] = (part if res is None else part + res_ref[...]).astype(out_dtype)
            return
        acc_ref = rest[-1]
        k = pl.program_id(2)

        @pl.when(k == 0)
        def _():
            acc_ref[...] = part

        @pl.when(k > 0)
        def _():
            acc_ref[...] += part

        @pl.when(k == nk - 1)
        def _():
            total = acc_ref[...] if res is None else acc_ref[...] + res_ref[...]
            o_ref[...] = total.astype(out_dtype)

    return pl.pallas_call(
        body, name=name,
        grid=(M // tm, N // tn, nk),
        in_specs=[a_spec, b_spec] + ([o_spec] if res is not None else []),
        out_specs=o_spec,
        out_shape=jax.ShapeDtypeStruct((M, N), out_dtype),
        scratch_shapes=[pltpu.VMEM((tm, tn), F32)] if nk > 1 else [],
        compiler_params=_params(("parallel", "parallel", "arbitrary")),
    )(*((a, b) if res is None else (a, b, res)))


def _rmsnorm_fwd(x, gain, name, tm=512):
    T, C = x.shape

    def body(x_ref, g_ref, h_ref):
        xv = x_ref[...]
        r = lax.rsqrt(jnp.mean(xv * xv, axis=-1, keepdims=True) + EPS)
        h_ref[...] = (xv * r * g_ref[...]).astype(BF16)

    return pl.pallas_call(
        body, name=name, grid=(T // tm,),
        in_specs=[pl.BlockSpec((tm, C), lambda i: (i, 0)), pl.BlockSpec((1, C), lambda i: (0, 0))],
        out_specs=pl.BlockSpec((tm, C), lambda i: (i, 0)),
        out_shape=jax.ShapeDtypeStruct((T, C), BF16),
        compiler_params=_params(("parallel",)),
    )(x, gain.reshape(1, C))


def _rmsnorm_bwd(x, gain, dh, dres, name, tm=512):
    T, C = x.shape

    def body(x_ref, g_ref, dh_ref, dres_ref, dx_ref, dg_ref):
        @pl.when(pl.program_id(0) == 0)
        def _():
            dg_ref[...] = jnp.zeros_like(dg_ref)

        xv = x_ref[...]
        dy = dh_ref[...].astype(F32)
        r = lax.rsqrt(jnp.mean(xv * xv, axis=-1, keepdims=True) + EPS)
        gdy = dy * g_ref[...]
        inner = jnp.mean(xv * gdy, axis=-1, keepdims=True)
        dx_ref[...] = dres_ref[...] + r * gdy - xv * (r * r * r * inner)
        dg_ref[...] += jnp.sum(dy * xv * r, axis=0, keepdims=True)

    return pl.pallas_call(
        body, name=name, grid=(T // tm,),
        in_specs=[pl.BlockSpec((tm, C), lambda i: (i, 0)), pl.BlockSpec((1, C), lambda i: (0, 0)),
                  pl.BlockSpec((tm, C), lambda i: (i, 0)), pl.BlockSpec((tm, C), lambda i: (i, 0))],
        out_specs=[pl.BlockSpec((tm, C), lambda i: (i, 0)), pl.BlockSpec((1, C), lambda i: (0, 0))],
        out_shape=[jax.ShapeDtypeStruct((T, C), F32), jax.ShapeDtypeStruct((1, C), F32)],
        compiler_params=_params(("arbitrary",)),
    )(x, gain.reshape(1, C), dh, dres)


MASKED = -1e30


def _pair_sum(x, same_head):
    hi = x.astype(BF16)
    lo = (x - hi.astype(F32)).astype(BF16)
    return jnp.dot(hi, same_head, preferred_element_type=F32) + jnp.dot(lo, same_head, preferred_element_type=F32)


def _same_head():
    r = lax.broadcasted_iota(jnp.int32, (2 * HEAD_DIM, 2 * HEAD_DIM), 0)
    c = lax.broadcasted_iota(jnp.int32, (2 * HEAD_DIM, 2 * HEAD_DIM), 1)
    return jnp.where((r < HEAD_DIM) == (c < HEAD_DIM), 1.0, 0.0).astype(BF16)


def _pair_norm(t, g, same_head):
    tf = t.astype(F32)
    r = lax.rsqrt(_pair_sum(tf * tf, same_head) * (1.0 / HEAD_DIM) + EPS)
    return tf * r * g


def _pair_norm_bwd(t, g, dn, same_head):
    tf = t.astype(F32)
    r = lax.rsqrt(_pair_sum(tf * tf, same_head) * (1.0 / HEAD_DIM) + EPS)
    gd = dn * g
    inner = _pair_sum(tf * gd, same_head) * (1.0 / HEAD_DIM)
    return r * gd - tf * (r * r * r * inner), jnp.sum(dn * tf * r, axis=0, keepdims=True)


def _band_table(bias):
    rows = [jnp.pad(bias, ((0, 0), (0, 0), (c * CHUNK, 2 * A_BLOCK - BAND - c * CHUNK)), constant_values=MASKED)
            for c in range(N_LEFT)]
    return jnp.concatenate(rows, axis=1).reshape(N_HEADS // 2, 2 * A_BLOCK, 2 * A_BLOCK)


def _band_table_bwd(dtable):
    dtable = dtable.reshape(N_HEADS, A_BLOCK, 2 * A_BLOCK)
    return sum(dtable[:, c * CHUNK:(c + 1) * CHUNK, c * CHUNK:c * CHUNK + BAND] for c in range(N_LEFT))


def _a_specs(T):
    nb = T // A_BLOCK
    pairs = N_HEADS // 2
    col = lambda which: which * pairs
    cur = lambda which: pl.BlockSpec((A_BLOCK, 2 * HEAD_DIM), lambda p, i: (i, col(which) + p))
    prev = lambda which: pl.BlockSpec((A_BLOCK, 2 * HEAD_DIM), lambda p, i: (jnp.maximum(i - 1, 0), col(which) + p))
    nxt = lambda which: pl.BlockSpec((A_BLOCK, 2 * HEAD_DIM), lambda p, i: (jnp.minimum(i + 1, nb - 1), col(which) + p))
    table = pl.BlockSpec((1, 2 * A_BLOCK, 2 * A_BLOCK), lambda p, i: (p, 0, 0))
    gain = pl.BlockSpec((1, 2 * HEAD_DIM), lambda p, i: (0, 0))
    gacc = pl.BlockSpec((1, 1, 2 * HEAD_DIM), lambda p, i: (p, 0, 0))
    return nb, pairs, cur, prev, nxt, table, gain, gacc


def _a_probs(q_ref, kc_ref, kp_ref, t_ref, gq_ref, gk_ref, same_head, first):
    scale = 1.0 / math.sqrt(HEAD_DIM)
    qst = _stack_heads((_pair_norm(q_ref[...], gq_ref[...], same_head) * scale).astype(BF16))
    kcat = jnp.concatenate([_pair_norm(kp_ref[...], gk_ref[...], same_head).astype(BF16),
                            _pair_norm(kc_ref[...], gk_ref[...], same_head).astype(BF16)], axis=0)
    s = lax.dot_general(qst, kcat, (((1,), (1,)), ((), ())), preferred_element_type=F32) + t_ref[0]
    col = lax.broadcasted_iota(jnp.int32, s.shape, 1)
    s = jnp.where(col >= jnp.where(first, A_BLOCK, 0), s, MASKED)
    e = jnp.exp(s - jnp.max(s, axis=-1, keepdims=True))
    return qst, kcat, e, jnp.sum(e, axis=-1, keepdims=True)


def _attn_a_fwd(qkv, table, gq, gk):
    T = qkv.shape[0]
    nb, pairs, cur, prev, _, tspec, gspec, _ = _a_specs(T)

    def body(q_ref, kc_ref, kp_ref, vc_ref, vp_ref, t_ref, gq_ref, gk_ref, o_ref):
        same_head = _same_head()
        _, _, e, total = _a_probs(q_ref, kc_ref, kp_ref, t_ref, gq_ref, gk_ref, same_head, pl.program_id(1) == 0)
        vcat = jnp.concatenate([vp_ref[...], vc_ref[...]], axis=0)
        p = (e / total).astype(BF16)
        o_ref[...] = _unstack_heads(jnp.dot(p, vcat, preferred_element_type=F32)).astype(BF16)

    return pl.pallas_call(
        body, name="attn_a_fwd", grid=(pairs, nb),
        in_specs=[cur(0), cur(1), prev(1), cur(2), prev(2), tspec, gspec, gspec],
        out_specs=pl.BlockSpec((A_BLOCK, 2 * HEAD_DIM), lambda p, i: (i, p)),
        out_shape=jax.ShapeDtypeStruct((T, WIDTH), BF16),
        compiler_params=_params(("parallel", "arbitrary")),
    )(qkv, qkv, qkv, qkv, qkv, table, jnp.tile(gq.reshape(1, HEAD_DIM), (1, 2)), jnp.tile(gk.reshape(1, HEAD_DIM), (1, 2)))


def _attn_a_bwd(qkv, do, table, gq, gk):
    T = qkv.shape[0]
    nb, pairs, cur, prev, _, tspec, gspec, gacc = _a_specs(T)
    scale = 1.0 / math.sqrt(HEAD_DIM)
    oblk = pl.BlockSpec((A_BLOCK, 2 * HEAD_DIM), lambda p, i: (i, p))

    def body(q_ref, kc_ref, kp_ref, vc_ref, vp_ref, do_ref, t_ref, gq_ref, gk_ref,
             dq_ref, dkc_ref, dkp_ref, dvc_ref, dvp_ref, dt_ref, dgq_ref):
        first = pl.program_id(1) == 0

        @pl.when(first)
        def _():
            dt_ref[...] = jnp.zeros_like(dt_ref)
            dgq_ref[...] = jnp.zeros_like(dgq_ref)

        same_head = _same_head()
        qst, kcat, e, total = _a_probs(q_ref, kc_ref, kp_ref, t_ref, gq_ref, gk_ref, same_head, first)
        vcat = jnp.concatenate([vp_ref[...], vc_ref[...]], axis=0)
        dost = _stack_heads(do_ref[...])
        p = e / total
        dp = lax.dot_general(dost, vcat, (((1,), (1,)), ((), ())), preferred_element_type=F32)
        ds = p * (dp - jnp.sum(p * dp, axis=-1, keepdims=True))
        dt_ref[0] += ds
        dsb = ds.astype(BF16)
        dqn = _unstack_heads(jnp.dot(dsb, kcat, preferred_element_type=F32)) * scale
        dq, dg = _pair_norm_bwd(q_ref[...], gq_ref[...], dqn, same_head)
        dq_ref[...] = dq.astype(BF16)
        dgq_ref[0] += dg
        dk = lax.dot_general(dsb, qst, (((0,), (0,)), ((), ())), preferred_element_type=F32)
        dv = lax.dot_general(p.astype(BF16), dost, (((0,), (0,)), ((), ())), preferred_element_type=F32)
        dkp_ref[...] = dk[:A_BLOCK]
        dkc_ref[...] = dk[A_BLOCK:]
        dvp_ref[...] = dv[:A_BLOCK]
        dvc_ref[...] = dv[A_BLOCK:]

    wide = jax.ShapeDtypeStruct((T, WIDTH), F32)
    return pl.pallas_call(
        body, name="attn_a_bwd", grid=(pairs, nb),
        in_specs=[cur(0), cur(1), prev(1), cur(2), prev(2), oblk, tspec, gspec, gspec],
        out_specs=[oblk, oblk, oblk, oblk, oblk, tspec, gacc],
        out_shape=[jax.ShapeDtypeStruct((T, WIDTH), BF16), wide, wide, wide, wide,
                   jax.ShapeDtypeStruct((pairs, 2 * A_BLOCK, 2 * A_BLOCK), F32),
                   jax.ShapeDtypeStruct((pairs, 1, 2 * HEAD_DIM), F32)],
        compiler_params=_params(("parallel", "arbitrary")),
    )(qkv, qkv, qkv, qkv, qkv, do, table, jnp.tile(gq.reshape(1, HEAD_DIM), (1, 2)),
      jnp.tile(gk.reshape(1, HEAD_DIM), (1, 2)))


def _attn_a_bwd_keys(qkv, dkc, dkp, dvc, dvp, gk):
    T = qkv.shape[0]
    nb, pairs, cur, _, _, _, gspec, gacc = _a_specs(T)
    oblk = pl.BlockSpec((A_BLOCK, 2 * HEAD_DIM), lambda p, i: (i, p))
    onext = pl.BlockSpec((A_BLOCK, 2 * HEAD_DIM), lambda p, i: (jnp.minimum(i + 1, nb - 1), p))

    def body(k_ref, dkc_ref, dkp_ref, dvc_ref, dvp_ref, gk_ref, dk_ref, dv_ref, dgk_ref):
        i = pl.program_id(1)

        @pl.when(i == 0)
        def _():
            dgk_ref[...] = jnp.zeros_like(dgk_ref)

        has_next = (i < nb - 1).astype(F32)
        dkn = dkc_ref[...] + has_next * dkp_ref[...]
        dk, dg = _pair_norm_bwd(k_ref[...], gk_ref[...], dkn, _same_head())
        dk_ref[...] = dk.astype(BF16)
        dv_ref[...] = (dvc_ref[...] + has_next * dvp_ref[...]).astype(BF16)
        dgk_ref[0] += dg

    blk = jax.ShapeDtypeStruct((T, WIDTH), BF16)
    return pl.pallas_call(
        body, name="attn_a_bwd_keys", grid=(pairs, nb),
        in_specs=[cur(1), oblk, onext, oblk, onext, gspec],
        out_specs=[oblk, oblk, gacc],
        out_shape=[blk, blk, jax.ShapeDtypeStruct((pairs, 1, 2 * HEAD_DIM), F32)],
        compiler_params=_params(("parallel", "arbitrary")),
    )(qkv, dkc, dkp, dvc, dvp, jnp.tile(gk.reshape(1, HEAD_DIM), (1, 2)))


def _scan_matrix(later):
    r = lax.broadcasted_iota(jnp.int32, (SB_SCAN, SB_SCAN), 0)
    c = lax.broadcasted_iota(jnp.int32, (SB_SCAN, SB_SCAN), 1)
    return jnp.where((r > c) if later else (r < c), 1.0, 0.0).astype(BF16)


def _running_sums(x, carry, scan, later):
    n = SB_KEYS // SB_SCAN
    parts = [None] * n
    total = carry
    for sb in (reversed(range(n)) if later else range(n)):
        xs = x[:, sb * SB_SCAN:(sb + 1) * SB_SCAN]
        local = jnp.dot(xs.astype(BF16), scan, preferred_element_type=F32)
        parts[sb] = local if total is None else local + total
        rowsum = jnp.sum(xs, axis=-1, keepdims=True)
        total = rowsum if total is None else total + rowsum
    return (parts[0] if n == 1 else jnp.concatenate(parts, axis=1)), total


def _sb_log_sigmoids(z):
    neg_abs = pltpu.bitcast(pltpu.bitcast(z, jnp.uint32) | jnp.uint32(0x80000000), F32)
    take = jnp.minimum(z, 0.0) - jnp.log(1.0 + jnp.exp(neg_abs))
    return take, take - z


def _sb_mask():
    r = lax.broadcasted_iota(jnp.int32, (2 * SB_ROWS, SB_KEYS), 0)
    c = lax.broadcasted_iota(jnp.int32, (2 * SB_ROWS, SB_KEYS), 1)
    return c < jnp.where(r >= SB_ROWS, r - SB_ROWS, r)


def _stack_heads(t):
    lane = lax.broadcasted_iota(jnp.int32, t.shape, 1)
    zero = jnp.zeros_like(t)
    return jnp.concatenate([jnp.where(lane < HEAD_DIM, t, zero), jnp.where(lane >= HEAD_DIM, t, zero)], axis=0)


def _unstack_heads(t):
    rows = t.shape[0] // 2
    lane = lax.broadcasted_iota(jnp.int32, (rows, 2 * HEAD_DIM), 1)
    return jnp.where(lane < HEAD_DIM, t[:rows], t[rows:])


def _sb_specs(T):
    nq = T // SB_ROWS
    blk = lambda col: pl.BlockSpec((SB_ROWS, 2 * HEAD_DIM), lambda p, i: (i, col + p))
    full = lambda col: pl.BlockSpec((T, 2 * HEAD_DIM), lambda p, i: (0, col + p))
    return nq, blk, full


def _key_rows(j):
    return pl.ds(pl.multiple_of(j * SB_KEYS, SB_KEYS), SB_KEYS)


def _attn_b_fwd(qkv):
    T = qkv.shape[0]
    pairs = N_HEADS // 2
    nq, blk, full = _sb_specs(T)
    scale = 1.0 / math.sqrt(HEAD_DIM)

    assert nq <= LANES

    def body(q_ref, k_ref, v_ref, o_ref, c_ref, acc_ref, carry_ref, z_ref, w_ref):
        i = pl.program_id(1)
        scan = _scan_matrix(True)
        qst = _stack_heads((q_ref[...].astype(F32) * scale).astype(BF16))
        lane = lax.broadcasted_iota(jnp.int32, (2 * SB_ROWS, LANES), 1)

        def scores(j):
            return lax.dot_general(qst, k_ref[_key_rows(j), :], (((1,), (1,)), ((), ())), preferred_element_type=F32)

        def weights(z, carry, mask):
            take, keep = _sb_log_sigmoids(z)
            if mask is not None:
                keep = jnp.where(mask, keep, 0.0)
            tail, total = _running_sums(keep, carry, scan, True)
            w = jnp.exp(take + tail)
            if mask is not None:
                w = jnp.where(mask, w, 0.0)
            return w.astype(BF16), total

        w_ref[...], carry_ref[...] = weights(scores(i), None, _sb_mask())
        z_ref[...] = scores(jnp.maximum(i - 1, 0))
        acc_ref[...] = jnp.zeros_like(acc_ref)
        c_ref[0, 0] = jnp.zeros((2 * SB_ROWS, LANES), F32)

        @pl.loop(0, i)
        def _(jj):
            j = i - 1 - jj
            z = z_ref[...]
            z_ref[...] = scores(jnp.maximum(j - 1, 0))
            acc_ref[...] += jnp.dot(w_ref[...], v_ref[_key_rows(j + 1), :], preferred_element_type=F32)
            carry = carry_ref[...]
            c_ref[0, 0] = jnp.where(lane == j, carry, c_ref[0, 0])
            w_ref[...], carry_ref[...] = weights(z, carry, None)

        acc = acc_ref[...] + jnp.dot(w_ref[...], v_ref[_key_rows(0), :], preferred_element_type=F32)
        o_ref[...] = _unstack_heads(acc).astype(BF16)

    return pl.pallas_call(
        body, name="attn_b_fwd", grid=(pairs, nq),
        in_specs=[blk(3 * pairs), full(4 * pairs), full(5 * pairs)],
        out_specs=[pl.BlockSpec((SB_ROWS, 2 * HEAD_DIM), lambda p, i: (i, p)),
                   pl.BlockSpec((1, 1, 2 * SB_ROWS, LANES), lambda p, i: (p, i, 0, 0))],
        out_shape=[jax.ShapeDtypeStruct((T, WIDTH), BF16), jax.ShapeDtypeStruct((pairs, nq, 2 * SB_ROWS, LANES), F32)],
        scratch_shapes=[pltpu.VMEM((2 * SB_ROWS, 2 * HEAD_DIM), F32), pltpu.VMEM((2 * SB_ROWS, 1), F32),
                        pltpu.VMEM((2 * SB_ROWS, SB_KEYS), F32), pltpu.VMEM((2 * SB_ROWS, SB_KEYS), BF16)],
        compiler_params=_params(("parallel", "arbitrary")),
    )(qkv, qkv, qkv)


def _attn_b_bwd(qkv, carries, do):
    T = qkv.shape[0]
    pairs = N_HEADS // 2
    nq, blk, full = _sb_specs(T)
    scale = 1.0 / math.sqrt(HEAD_DIM)
    oblk = pl.BlockSpec((SB_ROWS, 2 * HEAD_DIM), lambda p, i: (i, p))
    ofull = pl.BlockSpec((2 * HEAD_DIM, T), lambda p, i: (p, 0))

    def body(q_ref, k_ref, v_ref, c_ref, do_ref, dq_ref, dk_ref, dv_ref, dqacc_ref, before_ref,
             z_ref, dw_ref, dz_ref, w_ref):
        i = pl.program_id(1)

        @pl.when(i == 0)
        def _():
            dk_ref[...] = jnp.zeros_like(dk_ref)
            dv_ref[...] = jnp.zeros_like(dv_ref)

        scan_later = _scan_matrix(True)
        scan_earlier = _scan_matrix(False)
        qst = _stack_heads((q_ref[...].astype(F32) * scale).astype(BF16))
        dost = _stack_heads(do_ref[...].astype(BF16))
        lane = lax.broadcasted_iota(jnp.int32, (2 * SB_ROWS, LANES), 1)
        nt = (((1,), (1,)), ((), ()))

        def products(j):
            return (lax.dot_general(qst, k_ref[_key_rows(j), :], nt, preferred_element_type=F32),
                    lax.dot_general(dost, v_ref[_key_rows(j), :], nt, preferred_element_type=F32))

        def score_grads(z, dw, later, mask):
            take, keep = _sb_log_sigmoids(z)
            sig = jnp.exp(take)
            if mask is not None:
                keep = jnp.where(mask, keep, 0.0)
            tail, _ = _running_sums(keep, later, scan_later, True)
            w = jnp.exp(take + tail)
            if mask is not None:
                w = jnp.where(mask, w, 0.0)
            g = w * dw
            before, before_ref[...] = _running_sums(g, before_ref[...], scan_earlier, False)
            dz = g - sig * (g + before)
            if mask is not None:
                dz = jnp.where(mask, dz, 0.0)
            return dz.astype(BF16), w.astype(BF16)

        qst_t = qst.T
        dost_t = dost.T

        def accumulate(j, dzb, wb):
            cols = pl.ds(pl.multiple_of(j * SB_KEYS, SB_KEYS), SB_KEYS)
            dqacc_ref[...] += jnp.dot(dzb, k_ref[_key_rows(j), :], preferred_element_type=F32)
            dk_ref[:, cols] += jnp.dot(qst_t, dzb, preferred_element_type=F32)
            dv_ref[:, cols] += jnp.dot(dost_t, wb, preferred_element_type=F32)

        dqacc_ref[...] = jnp.zeros_like(dqacc_ref)
        before_ref[...] = jnp.zeros_like(before_ref)
        dz_ref[...] = jnp.zeros_like(dz_ref)
        w_ref[...] = jnp.zeros_like(w_ref)
        z_ref[...], dw_ref[...] = products(0)

        @pl.loop(0, i)
        def _(j):
            z, dw = z_ref[...], dw_ref[...]
            z_ref[...], dw_ref[...] = products(j + 1)
            accumulate(jnp.maximum(j - 1, 0), dz_ref[...], w_ref[...])
            later = jnp.sum(jnp.where(lane == j, c_ref[0, 0], 0.0), axis=-1, keepdims=True)
            dz_ref[...], w_ref[...] = score_grads(z, dw, later, None)

        accumulate(jnp.maximum(i - 1, 0), dz_ref[...], w_ref[...])
        accumulate(i, *score_grads(z_ref[...], dw_ref[...], None, _sb_mask()))
        dq_ref[...] = (_unstack_heads(dqacc_ref[...]) * scale).astype(BF16)

    wide = jax.ShapeDtypeStruct((WIDTH, T), F32)
    return pl.pallas_call(
        body, name="attn_b_bwd", grid=(pairs, nq),
        in_specs=[blk(3 * pairs), full(4 * pairs), full(5 * pairs),
                  pl.BlockSpec((1, 1, 2 * SB_ROWS, LANES), lambda p, i: (p, i, 0, 0)), oblk],
        out_specs=[oblk, ofull, ofull],
        out_shape=[jax.ShapeDtypeStruct((T, WIDTH), BF16), wide, wide],
        scratch_shapes=[pltpu.VMEM((2 * SB_ROWS, 2 * HEAD_DIM), F32), pltpu.VMEM((2 * SB_ROWS, 1), F32),
                        pltpu.VMEM((2 * SB_ROWS, SB_KEYS), F32), pltpu.VMEM((2 * SB_ROWS, SB_KEYS), F32),
                        pltpu.VMEM((2 * SB_ROWS, SB_KEYS), BF16), pltpu.VMEM((2 * SB_ROWS, SB_KEYS), BF16)],
        compiler_params=_params(("parallel", "arbitrary")),
    )(qkv, qkv, qkv, carries, do)


def _window_sums(ext, forward):
    n = ext.shape[0]
    out = []
    s = ext
    for step in (1, 2, 4, 8):
        s = s + pltpu.roll(s, (n - step) if forward else step, 0)
        out.append(s)
    return out


def _pool_counts(base, rows, win):
    t = base + lax.broadcasted_iota(jnp.int32, (rows, 1), 0)
    return jnp.minimum(t + 1, win).astype(F32)


def _pooled(u_ref, up_ref, i, tm):
    prev = jnp.where(i > 0, up_ref[...], 0.0)
    ext = jnp.concatenate([prev, u_ref[...]], axis=0)
    sums = _window_sums(ext, False)
    parts = []
    for g, win in enumerate(POOL_WINDOWS):
        cols = slice(g * POOL_DIM, (g + 1) * POOL_DIM)
        cnt = _pool_counts(i * tm, tm, win)
        parts.append(sums[g][HALO:, cols] / cnt - ext[HALO:, cols])
    return parts


def _pool_fwd(ucg, w_pool, scale, tm=512):
    T = ucg.shape[0]
    C = WIDTH

    def body(u_ref, up_ref, w_ref, s_ref, o_ref):
        i = pl.program_id(0)
        parts = _pooled(u_ref, up_ref, i, tm)
        for g in range(len(POOL_WINDOWS)):
            mixed = jnp.dot(parts[g].astype(BF16), w_ref[g], preferred_element_type=F32)
            o_ref[:, g * POOL_DIM:(g + 1) * POOL_DIM] = (mixed * s_ref[:, g * POOL_DIM:(g + 1) * POOL_DIM]).astype(BF16)

    return pl.pallas_call(
        body, name="pool_fwd", grid=(T // tm,),
        in_specs=[pl.BlockSpec((tm, C), lambda i: (i, 0)),
                  pl.BlockSpec((HALO, C), lambda i: (jnp.maximum(i * (tm // HALO) - 1, 0), 0)),
                  pl.BlockSpec((len(POOL_WINDOWS), POOL_DIM, POOL_DIM), lambda i: (0, 0, 0)),
                  pl.BlockSpec((1, C), lambda i: (0, 0))],
        out_specs=pl.BlockSpec((tm, C), lambda i: (i, 0)),
        out_shape=jax.ShapeDtypeStruct((T, C), BF16),
        compiler_params=_params(("parallel",)),
    )(ucg, ucg, w_pool.astype(BF16), scale.reshape(1, C))


def _pool_bwd(ucg, do_c, w_pool, scale, tm=512):
    T = ucg.shape[0]
    C = WIDTH
    nt = T // tm
    G = len(POOL_WINDOWS)

    def body(u_ref, up_ref, do_ref, don_ref, w_ref, s_ref, du_ref, dw_ref, ds_ref):
        i = pl.program_id(0)

        @pl.when(i == 0)
        def _():
            dw_ref[...] = jnp.zeros_like(dw_ref)
            ds_ref[...] = jnp.zeros_like(ds_ref)

        parts = _pooled(u_ref, up_ref, i, tm)
        nxt = jnp.where(i < nt - 1, don_ref[...].astype(F32), 0.0)
        do_ext = jnp.concatenate([do_ref[...].astype(F32), nxt], axis=0) * s_ref[...]
        for g, win in enumerate(POOL_WINDOWS):
            cols = slice(g * POOL_DIM, (g + 1) * POOL_DIM)
            pooled_b = parts[g].astype(BF16)
            dmix = do_ext[:, cols].astype(BF16)
            mixed = jnp.dot(pooled_b, w_ref[g], preferred_element_type=F32)
            ds_ref[:, cols] += jnp.sum(do_ref[:, cols].astype(F32) * mixed, axis=0, keepdims=True)
            dw_ref[g] += lax.dot_general(pooled_b, dmix[:tm], (((0,), (0,)), ((), ())), preferred_element_type=F32)
            dpool = lax.dot_general(dmix, w_ref[g], (((1,), (1,)), ((), ())), preferred_element_type=F32)
            scaled = dpool / _pool_counts(i * tm, tm + HALO, win)
            fwd = _window_sums(scaled, True)[g]
            du_ref[:, cols] = (fwd[:tm] - dpool[:tm]).astype(BF16)

    return pl.pallas_call(
        body, name="pool_bwd", grid=(nt,),
        in_specs=[pl.BlockSpec((tm, C), lambda i: (i, 0)),
                  pl.BlockSpec((HALO, C), lambda i: (jnp.maximum(i * (tm // HALO) - 1, 0), 0)),
                  pl.BlockSpec((tm, C), lambda i: (i, 0)),
                  pl.BlockSpec((HALO, C), lambda i: (jnp.minimum((i + 1) * (tm // HALO), T // HALO - 1), 0)),
                  pl.BlockSpec((G, POOL_DIM, POOL_DIM), lambda i: (0, 0, 0)),
                  pl.BlockSpec((1, C), lambda i: (0, 0))],
        out_specs=[pl.BlockSpec((tm, C), lambda i: (i, 0)),
                   pl.BlockSpec((G, POOL_DIM, POOL_DIM), lambda i: (0, 0, 0)),
                   pl.BlockSpec((1, C), lambda i: (0, 0))],
        out_shape=[jax.ShapeDtypeStruct((T, C), BF16), jax.ShapeDtypeStruct((G, POOL_DIM, POOL_DIM), F32),
                   jax.ShapeDtypeStruct((1, C), F32)],
        compiler_params=_params(("arbitrary",)),
    )(ucg, ucg, do_c, do_c, w_pool.astype(BF16), scale.reshape(1, C))


def _merge_fwd(oa, ob, oc, glog, b_gate, wa, wb, wc, tm=256):
    T = oa.shape[0]
    Dm = D_MODEL
    row = lambda c: pl.BlockSpec((tm, c), lambda i: (i, 0))
    wspec = pl.BlockSpec((WIDTH, Dm), lambda i: (0, 0))

    def body(oa_ref, ob_ref, oc_ref, g_ref, b_ref, wa_ref, wb_ref, wc_ref, m_ref, ya_ref, yb_ref, yc_ref):
        merged = jnp.zeros((tm, Dm), F32)
        for kk, (o_ref, w_ref, y_ref) in enumerate(((oa_ref, wa_ref, ya_ref), (ob_ref, wb_ref, yb_ref),
                                                    (oc_ref, wc_ref, yc_ref))):
            y = jnp.dot(o_ref[...].astype(BF16), w_ref[...], preferred_element_type=F32)
            gate = jax.nn.sigmoid(g_ref[:, kk * Dm:(kk + 1) * Dm] + b_ref[:, kk * Dm:(kk + 1) * Dm])
            merged = merged + gate * y
            y_ref[...] = y.astype(BF16)
        m_ref[...] = merged.astype(BF16)

    out = jax.ShapeDtypeStruct((T, Dm), BF16)
    return pl.pallas_call(
        body, name="merge_fwd", grid=(T // tm,),
        in_specs=[row(WIDTH), row(WIDTH), row(WIDTH), row(3 * Dm), pl.BlockSpec((1, 3 * Dm), lambda i: (0, 0)),
                  wspec, wspec, wspec],
        out_specs=[row(Dm)] * 4,
        out_shape=[out] * 4,
        compiler_params=_params(("parallel",)),
    )(oa, ob, oc, glog, b_gate.reshape(1, 3 * Dm), wa, wb, wc)


def _merge_bwd(dmerged, glog, b_gate, ya, yb, yc, tm=256):
    T = dmerged.shape[0]
    Dm = D_MODEL
    row = lambda c: pl.BlockSpec((tm, c), lambda i: (i, 0))

    def body(dm_ref, g_ref, b_ref, ya_ref, yb_ref, yc_ref, dya_ref, dyb_ref, dyc_ref, dg_ref, db_ref):
        @pl.when(pl.program_id(0) == 0)
        def _():
            db_ref[...] = jnp.zeros_like(db_ref)

        dm = dm_ref[...]
        for kk, (y_ref, dy_ref) in enumerate(((ya_ref, dya_ref), (yb_ref, dyb_ref), (yc_ref, dyc_ref))):
            cols = slice(kk * Dm, (kk + 1) * Dm)
            gate = jax.nn.sigmoid(g_ref[:, cols] + b_ref[:, cols])
            dy_ref[...] = (dm * gate).astype(BF16)
            dlog = dm * y_ref[...].astype(F32) * gate * (1.0 - gate)
            dg_ref[:, cols] = dlog.astype(BF16)
            db_ref[:, cols] += jnp.sum(dlog, axis=0, keepdims=True)

    out = jax.ShapeDtypeStruct((T, Dm), BF16)
    return pl.pallas_call(
        body, name="merge_bwd", grid=(T // tm,),
        in_specs=[row(Dm), row(3 * Dm), pl.BlockSpec((1, 3 * Dm), lambda i: (0, 0)), row(Dm), row(Dm), row(Dm)],
        out_specs=[row(Dm), row(Dm), row(Dm), row(3 * Dm), pl.BlockSpec((1, 3 * Dm), lambda i: (0, 0))],
        out_shape=[out, out, out, jax.ShapeDtypeStruct((T, 3 * Dm), BF16), jax.ShapeDtypeStruct((1, 3 * Dm), F32)],
        compiler_params=_params(("arbitrary",)),
    )(dmerged, glog, b_gate.reshape(1, 3 * Dm), ya, yb, yc)


def _residual_add(x, y, name, tm=512):
    T, C = x.shape

    def body(x_ref, y_ref, o_ref):
        o_ref[...] = x_ref[...] + y_ref[...]

    spec = pl.BlockSpec((tm, C), lambda i: (i, 0))
    return pl.pallas_call(body, name=name, grid=(T // tm,), in_specs=[spec, spec], out_specs=spec,
                          out_shape=jax.ShapeDtypeStruct((T, C), F32), compiler_params=_params(("parallel",)))(x, y)


FF_TILE = 256
FF_TILES = D_FF // FF_TILE
CONV_HALO = 16
FF_ROWS = 32


def _ff_pair_order(w):
    lead = w.shape[:-1]
    n = len(lead)
    w = w.reshape(*lead, 2, FF_TILES, FF_TILE)
    return jnp.swapaxes(w, n, n + 1).reshape(*lead, 2 * D_FF)


def _ff_natural_order(w):
    lead = w.shape[:-1]
    n = len(lead)
    w = w.reshape(*lead, FF_TILES, 2, FF_TILE)
    return jnp.swapaxes(w, n, n + 1).reshape(*lead, 2 * D_FF)


def _conv(ext, w_ref, b_ref):
    c = b_ref[...] + w_ref[2:3, :] * ext
    c = c + w_ref[1:2, :] * pltpu.roll(ext, 1, 0)
    c = c + w_ref[0:1, :] * pltpu.roll(ext, 2, 0)
    return c[CONV_HALO:]


def _ff_specs(T, tm):
    pair = pl.BlockSpec((tm, 2 * FF_TILE), lambda i, j: (i, j))
    prev = pl.BlockSpec((CONV_HALO, 2 * FF_TILE), lambda i, j: (jnp.maximum(i * (tm // CONV_HALO) - 1, 0), j))
    nxt = pl.BlockSpec((CONV_HALO, 2 * FF_TILE),
                       lambda i, j: (jnp.minimum((i + 1) * (tm // CONV_HALO), T // CONV_HALO - 1), j))
    half = pl.BlockSpec((tm, FF_TILE), lambda i, j: (i, j))
    small = lambda r: pl.BlockSpec((r, 2 * FF_TILE), lambda i, j: (0, j))
    return pair, prev, nxt, half, small


def _ff_row_chunks(u_ref, halo, tm, chunk):
    chunk(0, jnp.concatenate([halo, u_ref[0:FF_ROWS, :]], axis=0).astype(F32))

    @pl.loop(1, tm // FF_ROWS)
    def _(c):
        r0 = pl.multiple_of(c * FF_ROWS, FF_ROWS)
        start = pl.multiple_of(r0 - CONV_HALO, CONV_HALO)
        chunk(r0, u_ref[pl.ds(start, FF_ROWS + CONV_HALO), :].astype(F32))


def _swap_grid(spec):
    return pl.BlockSpec(spec.block_shape, lambda j, i, f=spec.index_map: f(i, j))


def _ff_act_fwd(u, conv_w, conv_b, tm=1024):
    T = u.shape[0]
    pair, prev, _, half, small = _ff_specs(T, tm)

    def body(u_ref, p_ref, w_ref, b_ref, a_ref):
        i = pl.program_id(0)
        ext = jnp.concatenate([jnp.where(i > 0, p_ref[...], 0.0), u_ref[...]], axis=0).astype(F32)
        c = _conv(ext, w_ref, b_ref)
        cg, cv = c[:, :FF_TILE], c[:, FF_TILE:]
        a_ref[...] = (cg * jax.nn.sigmoid(cg) * cv).astype(BF16)

    return pl.pallas_call(
        body, name="ff_act_fwd", grid=(T // tm, FF_TILES),
        in_specs=[pair, prev, small(3), small(1)],
        out_specs=half,
        out_shape=jax.ShapeDtypeStruct((T, D_FF), BF16),
        compiler_params=_params(("parallel", "parallel")),
    )(u, u, conv_w, conv_b.reshape(1, -1))


def _ff_act_bwd(u, da, conv_w, conv_b, tm=1024):
    T = u.shape[0]
    pair, prev, _, half, small = _ff_specs(T, tm)

    def body(u_ref, p_ref, da_ref, w_ref, b_ref, dc_ref, dw_ref, db_ref, sums_ref):
        i = pl.program_id(1)

        @pl.when(i == 0)
        def _():
            dw_ref[...] = jnp.zeros_like(dw_ref)
            db_ref[...] = jnp.zeros_like(db_ref)

        sums_ref[...] = jnp.zeros_like(sums_ref)

        def fold(x):
            return jnp.sum(x.reshape(FF_ROWS // 8, 8, x.shape[-1]), axis=0)

        def chunk(r0, ext):
            c = _conv(ext, w_ref, b_ref)
            cg, cv = c[:, :FF_TILE], c[:, FF_TILE:]
            da = da_ref[pl.ds(r0, FF_ROWS), :].astype(F32)
            sg = jax.nn.sigmoid(cg)
            dc = jnp.concatenate([da * cv * sg * (1.0 + cg * (1.0 - sg)), da * cg * sg], axis=1)
            dc_ref[pl.ds(r0, FF_ROWS), :] = dc.astype(BF16)
            sums_ref[3] += fold(dc)
            sums_ref[2] += fold(dc * ext[CONV_HALO:])
            sums_ref[1] += fold(dc * pltpu.roll(ext, 1, 0)[CONV_HALO:])
            sums_ref[0] += fold(dc * pltpu.roll(ext, 2, 0)[CONV_HALO:])

        _ff_row_chunks(u_ref, jnp.where(i > 0, p_ref[...], 0.0), tm, chunk)
        for tap in range(3):
            dw_ref[tap:tap + 1, :] += jnp.sum(sums_ref[tap], axis=0, keepdims=True)
        db_ref[...] += jnp.sum(sums_ref[3], axis=0, keepdims=True)

    return pl.pallas_call(
        body, name="ff_act_bwd", grid=(FF_TILES, T // tm),
        in_specs=[_swap_grid(pair), _swap_grid(prev), _swap_grid(half), _swap_grid(small(3)), _swap_grid(small(1))],
        out_specs=[_swap_grid(pair), _swap_grid(small(3)), _swap_grid(small(1))],
        out_shape=[jax.ShapeDtypeStruct((T, 2 * D_FF), BF16), jax.ShapeDtypeStruct((3, 2 * D_FF), F32),
                   jax.ShapeDtypeStruct((1, 2 * D_FF), F32)],
        scratch_shapes=[pltpu.VMEM((4, 8, 2 * FF_TILE), F32)],
        compiler_params=_params(("parallel", "arbitrary")),
    )(u, u, da, conv_w, conv_b.reshape(1, -1))


def _ff_conv_bwd(dc, conv_w, tm=1024):
    T = dc.shape[0]
    nt = T // tm
    pair, _, nxt, _, small = _ff_specs(T, tm)
    halo = CONV_HALO

    def body(dc_ref, n_ref, w_ref, du_ref):
        i = pl.program_id(0)
        following = jnp.where(i < nt - 1, n_ref[...].astype(F32), 0.0)
        ext = jnp.concatenate([dc_ref[...].astype(F32), following], axis=0)
        n = tm + halo
        du = w_ref[2:3, :] * ext + w_ref[1:2, :] * pltpu.roll(ext, n - 1, 0) + w_ref[0:1, :] * pltpu.roll(ext, n - 2, 0)
        du_ref[...] = du[:tm].astype(BF16)

    return pl.pallas_call(
        body, name="ff_conv_bwd", grid=(nt, FF_TILES),
        in_specs=[pair, nxt, small(3)],
        out_specs=pair,
        out_shape=jax.ShapeDtypeStruct((T, 2 * D_FF), BF16),
        compiler_params=_params(("parallel", "parallel")),
    )(dc, dc, conv_w)


def _loss_head(y, target, tm=512):
    T, C = y.shape
    nt = T // tm

    def body(y_ref, t_ref, dy_ref, l_ref):
        err = y_ref[...] - t_ref[...]
        dy_ref[...] = err * (1.0 / C)
        part = jnp.sum(err * err, axis=0, keepdims=True) * (0.5 / C)
        l_ref[0] = jnp.broadcast_to(part, (8, C))

    spec = pl.BlockSpec((tm, C), lambda i: (i, 0))
    dy, parts = pl.pallas_call(
        body, name="loss_head", grid=(nt,),
        in_specs=[spec, spec],
        out_specs=[spec, pl.BlockSpec((1, 8, C), lambda i: (i, 0, 0))],
        out_shape=[jax.ShapeDtypeStruct((T, C), F32), jax.ShapeDtypeStruct((nt, 8, C), F32)],
        compiler_params=_params(("parallel",)),
    )(y, target)
    return dy, jnp.sum(parts[:, 0, :])


def _adamw_math(w, g, m, v):
    m = ADAM_B1 * m + (1.0 - ADAM_B1) * g
    v = ADAM_B2 * v + (1.0 - ADAM_B2) * (g * g)
    m_hat = m / (1.0 - ADAM_B1 ** ADAM_STEP)
    v_hat = v / (1.0 - ADAM_B2 ** ADAM_STEP)
    delta = -ADAM_LR * (m_hat / (jnp.sqrt(v_hat) + ADAM_EPS) + ADAM_WD * w)
    return delta, m, v


def _adamw(parts, w, m, v, name, tm=256):
    R, C = w.shape
    tm = _pick_rows(R, tm)

    def body(p_ref, w_ref, m_ref, v_ref, g_ref, d_ref, nm_ref, nv_ref):
        g = p_ref[0].astype(F32)
        for s in range(1, N_DEV):
            g = g + p_ref[s].astype(F32)
        delta, nm, nv = _adamw_math(w_ref[...], g, m_ref[...], v_ref[...])
        g_ref[...] = g
        d_ref[...] = delta
        nm_ref[...] = nm
        nv_ref[...] = nv

    spec = pl.BlockSpec((tm, C), lambda i: (i, 0))
    out = jax.ShapeDtypeStruct((R, C), F32)
    return pl.pallas_call(
        body, name=name, grid=(R // tm,),
        in_specs=[pl.BlockSpec((N_DEV, tm, C), lambda i: (0, i, 0)), spec, spec, spec],
        out_specs=[spec] * 4,
        out_shape=[out] * 4,
        compiler_params=_params(("parallel",)),
    )(parts, w, m, v)


def _pick_rows(n, cap):
    if n < 16:
        return n
    best = None
    for t in range(16, min(n, cap) + 1, 16):
        if n % t == 0:
            best = t
    assert best is not None, (n, cap)
    return best


def _exchange(srcs, scatter, name):
    n = len(srcs)

    def body(*refs):
        src_refs, out_refs = refs[:n], refs[n:2 * n]
        send_sems, recv_sems, local_sems = refs[2 * n:]
        x, y, c = lax.axis_index("x"), lax.axis_index("y"), lax.axis_index("c")
        me = 4 * x + 2 * y + c

        def piece(a, d):
            return src_refs[a].at[d] if scatter else src_refs[a]

        local = [pltpu.make_async_copy(piece(a, me), out_refs[a].at[me], local_sems.at[a]) for a in range(n)]
        for cp in local:
            cp.start()
        copies = []
        for k in range(1, N_DEV):
            where, peer = _peer(k)
            for a in range(n):
                cp = pltpu.make_async_remote_copy(
                    src_ref=piece(a, peer), dst_ref=out_refs[a].at[me],
                    send_sem=send_sems.at[a * N_DEV + k], recv_sem=recv_sems.at[a * N_DEV + k],
                    device_id=where, device_id_type=MESH)
                cp.start()
                copies.append((cp, a, k, peer))
        for cp, a, k, peer in copies:
            cp.wait_send()
            pltpu.make_async_remote_copy(
                src_ref=piece(a, peer), dst_ref=out_refs[a].at[peer],
                send_sem=send_sems.at[a * N_DEV + k], recv_sem=recv_sems.at[a * N_DEV + k],
                device_id=(x, y, c), device_id_type=MESH).wait_recv()
        for cp in local:
            cp.wait()

    slab = lambda s: tuple(s.shape[1:] if scatter else s.shape)
    return pl.pallas_call(
        body, name=name,
        in_specs=[pl.BlockSpec(memory_space=pl.ANY)] * n,
        out_specs=[pl.BlockSpec(memory_space=pl.ANY)] * n,
        out_shape=[jax.ShapeDtypeStruct((N_DEV,) + slab(s), s.dtype) for s in srcs],
        scratch_shapes=[pltpu.SemaphoreType.DMA((n * N_DEV,)), pltpu.SemaphoreType.DMA((n * N_DEV,)),
                        pltpu.SemaphoreType.DMA((n,))],
    )(*srcs)


def _peer(k):
    x, y, c = lax.axis_index("x"), lax.axis_index("y"), lax.axis_index("c")
    px = 1 - x if k & 4 else x
    py = 1 - y if k & 2 else y
    pc = 1 - c if k & 1 else c
    return (px, py, pc), 4 * px + 2 * py + pc


def _split_copies(src_refs, land_refs, send_sems, recv_sems, scatter):
    x, y, c = lax.axis_index("x"), lax.axis_index("y"), lax.axis_index("c")
    me = 4 * x + 2 * y + c
    sends, arrivals = [], []
    for k in range(1, N_DEV):
        where, peer = _peer(k)
        for a, (src, land) in enumerate(zip(src_refs, land_refs)):
            piece = src.at[peer] if scatter else src
            sends.append(pltpu.make_async_remote_copy(
                src_ref=piece, dst_ref=land.at[me], send_sem=send_sems.at[a * N_DEV + k], recv_sem=recv_sems.at[a * N_DEV + k],
                device_id=where, device_id_type=MESH))
            arrivals.append(pltpu.make_async_remote_copy(
                src_ref=piece, dst_ref=land.at[peer], send_sem=send_sems.at[a * N_DEV + k], recv_sem=recv_sems.at[a * N_DEV + k],
                device_id=(x, y, c), device_id_type=MESH))
    return sends, arrivals


def _exchange_start(srcs, scatter, name):
    n = len(srcs)
    slab = lambda s: tuple(s.shape[1:] if scatter else s.shape)
    lands = [lax.empty((N_DEV,) + slab(s), s.dtype) for s in srcs]

    def body(*refs):
        src_refs, land_refs = refs[:n], refs[n:2 * n]
        send_sems, recv_sems = refs[2 * n], refs[2 * n + 1]
        token = refs[-1]
        sends, _ = _split_copies(src_refs, land_refs, send_sems, recv_sems, scatter)
        for cp in sends:
            cp.start()
        token[...] = jnp.zeros_like(token)

    hbm = pl.BlockSpec(memory_space=pltpu.HBM)
    sem = pl.BlockSpec(memory_space=pltpu.SEMAPHORE)
    out = pl.pallas_call(
        body, name=name,
        in_specs=[hbm] * (2 * n),
        out_specs=[sem, sem] + [hbm] * (2 * n) + [pl.BlockSpec(memory_space=pltpu.VMEM)],
        out_shape=[pltpu.SemaphoreType.DMA((n * N_DEV,)), pltpu.SemaphoreType.DMA((n * N_DEV,))]
        + [pltpu.HBM(s.shape, s.dtype) for s in srcs] + [pltpu.HBM(l.shape, l.dtype) for l in lands]
        + [jax.ShapeDtypeStruct((8, LANES), F32)],
        input_output_aliases={j: 2 + j for j in range(2 * n)},
        compiler_params=pltpu.CompilerParams(has_side_effects=pltpu.SideEffectType.DATAFLOW_SIDE_EFFECTING),
    )(*[pltpu.with_memory_space_constraint(s, pltpu.HBM) for s in srcs],
      *[pltpu.with_memory_space_constraint(l, pltpu.HBM) for l in lands])
    return (out[0], out[1], out[2:2 + n], out[2 + n:2 + 2 * n]), out[-1]


def _exchange_finish(state, scatter, after, name):
    send_sems, recv_sems, srcs, lands = state
    n = len(srcs)

    def body(*refs):
        src_refs, land_refs = refs[:n], refs[n:2 * n]
        sends, arrivals = _split_copies(src_refs, land_refs, refs[2 * n], refs[2 * n + 1], scatter)
        for cp in sends:
            cp.wait_send()
        for cp in arrivals:
            cp.wait_recv()

    hbm = pl.BlockSpec(memory_space=pltpu.HBM)
    sem = pl.BlockSpec(memory_space=pltpu.SEMAPHORE)
    out = pl.pallas_call(
        body, name=name,
        in_specs=[hbm] * (2 * n) + [sem, sem, pl.BlockSpec(memory_space=pl.ANY)],
        out_specs=[hbm] * (2 * n),
        out_shape=[pltpu.HBM(s.shape, s.dtype) for s in srcs] + [pltpu.HBM(l.shape, l.dtype) for l in lands],
        input_output_aliases={j: j for j in range(2 * n)},
        compiler_params=pltpu.CompilerParams(has_side_effects=pltpu.SideEffectType.DATAFLOW_SIDE_EFFECTING),
    )(*srcs, *lands, send_sems, recv_sems, after)
    return list(out[n:])


def _own_slab(landed, src, scatter):
    me = 4 * lax.axis_index("x") + 2 * lax.axis_index("y") + lax.axis_index("c")
    own = lax.dynamic_index_in_dim(src, me, axis=0, keepdims=True) if scatter else src[None]
    return lax.dynamic_update_slice_in_dim(landed, own, me, axis=0)


SHARDED = ("w_in", "w_branch_a", "w_branch_b", "w_branch_c", "w_out", "w_up", "w_down")
REPLICATED = ("norm_mix", "b_gate", "q_norm_a", "k_norm_a", "rel_bias_a", "w_pool", "pool_scale", "norm_ffn", "conv_b")
WEIGHTS = ("norm_mix", "w_in", "b_gate", "q_norm_a", "k_norm_a", "rel_bias_a", "w_pool", "pool_scale",
           "w_branch_a", "w_branch_b", "w_branch_c", "w_out", "norm_ffn", "w_up", "conv_w", "conv_b", "w_down")
SMALL_COLS = 128
QKV_COLS = 6 * WIDTH


def _rel_index():
    q_off = jnp.arange(CHUNK)[:, None] + N_LEFT * CHUNK
    k_off = jnp.arange(BAND)[None, :]
    return jnp.clip(q_off - k_off, -(CHUNK - 1), MAX_REL) + (CHUNK - 1)


def _rel_onehot():
    rel = _rel_index().reshape(1, CHUNK * BAND)
    return (rel == jnp.arange(REL_TABLE)[:, None]).astype(BF16)


def _select_mm(x, onehot, mode, name):
    hi = x.astype(BF16)
    r1 = x - hi.astype(F32)
    mid = r1.astype(BF16)
    lo = (r1 - mid.astype(F32)).astype(BF16)
    y = _mm(jnp.concatenate([hi, mid, lo, jnp.zeros_like(hi)], axis=0), onehot, mode, F32, name)
    n = x.shape[0]
    return y[:n] + y[n:2 * n] + y[2 * n:3 * n]


def _pack_rows(arrays, cols, row_multiple):
    flat = jnp.concatenate([a.reshape(-1) for a in arrays])
    rows = -(-flat.shape[0] // cols)
    rows = -(-rows // row_multiple) * row_multiple
    return jnp.pad(flat, (0, rows * cols - flat.shape[0])).reshape(rows, cols)


def _unpack_rows(packed, like):
    flat = packed.reshape(-1)
    out, off = [], 0
    for a in like:
        out.append(flat[off:off + a.size].reshape(a.shape))
        off += a.size
    return out


def kernel(x, norm_mix, w_in, b_gate, q_norm_a, k_norm_a, rel_bias_a, w_pool, pool_scale, w_branch_a, w_branch_b, w_branch_c, w_out, norm_ffn, w_up, conv_w, conv_b, w_down, loss_target, m_norm_mix, m_w_in, m_b_gate, m_q_norm_a, m_k_norm_a, m_rel_bias_a, m_w_pool, m_pool_scale, m_w_branch_a, m_w_branch_b, m_w_branch_c, m_w_out, m_norm_ffn, m_w_up, m_conv_w, m_conv_b, m_w_down, v_norm_mix, v_w_in, v_b_gate, v_q_norm_a, v_k_norm_a, v_rel_bias_a, v_w_pool, v_pool_scale, v_w_branch_a, v_w_branch_b, v_w_branch_c, v_w_out, v_norm_ffn, v_w_up, v_conv_w, v_conv_b, v_w_down):
    args = dict(locals())
    w = {n: args[n] for n in WEIGHTS}
    m = {n: args["m_" + n] for n in WEIGHTS}
    v = {n: args["v_" + n] for n in WEIGHTS}
    L = w_in.shape[0]
    T = x.shape[1]
    xs = x.reshape(T, D_MODEL)
    target = loss_target.reshape(T, D_MODEL)

    exchanged = SHARDED + ("conv_w",)
    row_sharded = ("w_out", "w_down")
    shard = {(n, l): (w[n][l] if n == "conv_w" else w[n][l].astype(BF16)) for n in exchanged for l in range(L)}
    late = [key for key in shard if key != ("w_in", 0)]
    gathered = {("w_in", 0): _exchange([shard["w_in", 0]], False, "gather_first")[0]}
    gather_state, gather_token = _exchange_start([shard[key] for key in late], False, "gather_rest_start")

    def full_weight(key):
        g = gathered[key]
        return g.reshape(-1, g.shape[-1]) if key[0] in row_sharded else g.transpose(1, 0, 2).reshape(g.shape[1], -1)

    w_in_f = {0: full_weight(("w_in", 0))}
    conv_b_f = _ff_pair_order(conv_b)
    onehot = _rel_onehot()

    saved = []
    cur = xs
    full = {}
    for l in range(L):
        w_qkv, w_uc, w_g = w_in_f[l][:, :QKV_COLS], w_in_f[l][:, QKV_COLS:QKV_COLS + WIDTH], w_in_f[l][:, QKV_COLS + WIDTH:]
        gain = norm_mix[l] + gather_token[0, 0] if l == 0 else norm_mix[l]
        h = _rmsnorm_fwd(cur, gain, "norm_mix_fwd")
        qkv = _mm(h, w_qkv, "nn", BF16, "proj_qkv")
        uc = _mm(h, w_uc, "nn", F32, "proj_pool")
        glog = _mm(h, w_g, "nn", F32, "proj_gate")
        table = _band_table(_select_mm(rel_bias_a[l], onehot, "nn", "rel_bias_table").reshape(N_HEADS, CHUNK, BAND))
        oa = _attn_a_fwd(qkv, table, q_norm_a[l], k_norm_a[l])
        ob, carries = _attn_b_fwd(qkv)
        oc = _pool_fwd(uc, w_pool[l], pool_scale[l])
        if l == 0:
            landed = _exchange_finish(gather_state, False, ob, "gather_rest_finish")
            gathered.update({key: _own_slab(g, shard[key], False) for key, g in zip(late, landed)})
            full = {key: full_weight(key) for key in gathered}
            w_in_f.update({k: full["w_in", k] for k in range(1, L)})
        w_a, w_b, w_c = (full["w_branch_" + tag, l] for tag in "abc")
        w_out_f, w_down_f = full["w_out", l], full["w_down", l]
        w_up_f, conv_w_f = _ff_pair_order(full["w_up", l]), _ff_pair_order(full["conv_w", l])
        merged, ya, yb, yc = _merge_fwd(oa, ob, oc, glog, b_gate[l], w_a, w_b, w_c)
        x1 = _mm(merged, w_out_f, "nn", F32, "out_proj", res=cur)
        h2 = _rmsnorm_fwd(x1, norm_ffn[l], "norm_ffn_fwd")
        u = _mm(h2, w_up_f, "nn", BF16, "ff_up")
        act = _ff_act_fwd(u, conv_w_f, conv_b_f[l])
        x2 = _mm(act, w_down_f, "nn", F32, "ff_down", res=x1)
        saved.append(dict(x=cur, h=h, qkv=qkv, carries=carries, uc=uc, glog=glog, table=table, oa=oa, ob=ob, oc=oc,
                          ya=ya, yb=yb, yc=yc, merged=merged, x1=x1, h2=h2, u=u, act=act, w_qkv=w_qkv, w_uc=w_uc,
                          w_g=w_g, w_a=w_a, w_b=w_b, w_c=w_c, w_out=w_out_f, w_up=w_up_f, w_down=w_down_f,
                          conv_w=conv_w_f))
        cur = x2

    dcur, loss_local = _loss_head(cur, target)
    loss = lax.psum(loss_local, ("x", "y", "c"))

    def pieces_of(n, g):
        if n in row_sharded:
            return g.reshape(N_DEV, -1, g.shape[-1])
        return g.reshape(g.shape[0], N_DEV, -1).transpose(1, 0, 2)

    gw = {n: [None] * L for n in WEIGHTS}
    for l in reversed(range(L)):
        s = saved[l]
        da = _mm(dcur, s["w_down"], "nt", BF16, "ff_down_dx", tn_cap=1408)
        gw["w_down"][l] = _mm(s["act"], dcur, "tn", BF16, "ff_down_dw")
        dc, dconv_w, dconv_b = _ff_act_bwd(s["u"], da, s["conv_w"], conv_b_f[l])
        du = _ff_conv_bwd(dc, s["conv_w"])
        dh2 = _mm(du, s["w_up"], "nt", F32, "ff_up_dx")
        gw["w_up"][l] = _ff_natural_order(_mm(s["h2"], du, "tn", BF16, "ff_up_dw"))
        gw["conv_w"][l] = _ff_natural_order(dconv_w)
        gw["conv_b"][l] = _ff_natural_order(dconv_b)[0]
        dx1, dg = _rmsnorm_bwd(s["x1"], norm_ffn[l], dh2, dcur, "norm_ffn_bwd")
        gw["norm_ffn"][l] = dg[0]

        dmerged = _mm(dx1, s["w_out"], "nt", F32, "out_proj_dx")
        gw["w_out"][l] = _mm(s["merged"], dx1, "tn", BF16, "out_proj_dw")
        dya, dyb, dyc, dglog, db_gate = _merge_bwd(dmerged, s["glog"], b_gate[l], s["ya"], s["yb"], s["yc"])
        gw["b_gate"][l] = db_gate[0]
        do = {}
        for tag, dy, ok in (("a", dya, s["oa"]), ("b", dyb, s["ob"]), ("c", dyc, s["oc"])):
            do[tag] = _mm(dy, s["w_" + tag], "nt", BF16, "branch_dx_" + tag)
            gw["w_branch_" + tag][l] = _mm(ok, dy, "tn", BF16, "branch_dw_" + tag)
        duc, dw_pool, dscale = _pool_bwd(s["uc"], do["c"], w_pool[l], pool_scale[l])
        gw["w_pool"][l] = dw_pool
        gw["pool_scale"][l] = dscale[0]
        gain_q = q_norm_a[l]
        if l == 0:
            early = [(n, k) for n in exchanged for k in range(L) if (n, k) != ("w_in", 0)]
            early_pieces = [pieces_of(n, gw[n][k]) for n, k in early]
            grads_state, grads_token = _exchange_start(early_pieces, True, "exchange_early_start")
            gain_q = gain_q + grads_token[0, 0]
        dqa, dkc, dkp, dvc, dvp, dtable, dgq = _attn_a_bwd(s["qkv"], do["a"], s["table"], gain_q, k_norm_a[l])
        dka, dva, dgk = _attn_a_bwd_keys(s["qkv"], dkc, dkp, dvc, dvp, k_norm_a[l])
        gw["q_norm_a"][l] = jnp.sum(dgq.reshape(N_HEADS, HEAD_DIM), axis=0)
        gw["k_norm_a"][l] = jnp.sum(dgk.reshape(N_HEADS, HEAD_DIM), axis=0)
        gw["rel_bias_a"][l] = _select_mm(_band_table_bwd(dtable).reshape(N_HEADS, CHUNK * BAND), onehot, "nt",
                                         "rel_bias_table_dw")
        dqb, dkb, dvb = _attn_b_bwd(s["qkv"], s["carries"], do["b"])
        dqkv = jnp.concatenate([dqa, dka, dva, dqb, dkb.T.astype(BF16), dvb.T.astype(BF16)], axis=1)
        dh = _mm(dqkv, s["w_qkv"], "nt", F32, "proj_qkv_dx")
        dh = _mm(duc, s["w_uc"], "nt", F32, "proj_pool_dx", res=dh)
        dh = _mm(dglog, s["w_g"], "nt", F32, "proj_gate_dx", res=dh)
        gw["w_in"][l] = jnp.concatenate([_mm(s["h"], dqkv, "tn", BF16, "proj_qkv_dw"),
                                         _mm(s["h"], duc, "tn", BF16, "proj_pool_dw"),
                                         _mm(s["h"], dglog, "tn", BF16, "proj_gate_dw")], axis=1)
        dcur, dg = _rmsnorm_bwd(s["x"], norm_mix[l], dh, dx1, "norm_mix_bwd")
        gw["norm_mix"][l] = dg[0]

    landed = _exchange_finish(grads_state, True, dcur, "exchange_early_finish")
    parts = {key: _own_slab(g, src, True) for key, g, src in zip(early, landed, early_pieces)}
    parts["w_in", 0] = _exchange([pieces_of("w_in", gw["w_in"][0])], True, "exchange_last")[0]
    small = _pack_rows([jnp.stack(gw[n]) for n in REPLICATED], SMALL_COLS, 16)
    small_parts = _exchange([small], False, "gather_small_grads")[0]

    out = {}
    for n in exchanged:
        res = [_adamw(parts[n, l], w[n][l], m[n][l], v[n][l], "adamw_" + n) for l in range(L)]
        out[n] = tuple(jnp.stack(r) for r in zip(*res))
    rep_like = [w[n] for n in REPLICATED]
    res = _adamw(small_parts, *[_pack_rows([d[n] for n in REPLICATED], SMALL_COLS, 16) for d in (w, m, v)],
                 "adamw_replicated")
    out.update({n: r for n, r in zip(REPLICATED, zip(*[_unpack_rows(r, rep_like) for r in res]))})

    grads, deltas, new_m, new_v = ([out[n][i] for n in WEIGHTS] for i in range(4))
    return (loss, dcur.reshape(x.shape), *grads, *deltas, *new_m, *new_v)
```

```python
import functools
import math

import jax
import jax.numpy as jnp
from jax import lax
from jax.experimental import pallas as pl
from jax.experimental.pallas import tpu as pltpu

F32 = jnp.float32
BF16 = jnp.bfloat16

N_DEV = 8
D_MODEL = 1024
N_HEADS = 8
HEAD_DIM = 64
CHUNK = 64
N_LEFT = 8
BAND = (N_LEFT + 1) * CHUNK
WIDTH = N_HEADS * HEAD_DIM
POOL_WINDOWS = (2, 4, 8, 16)
POOL_DIM = 128
MAX_REL = 2 * CHUNK
REL_TABLE = MAX_REL + CHUNK
D_FF = 2816
EPS = 1e-6
SB_SCAN = 256
SB_ROWS = 512
SB_KEYS = 512
A_BLOCK = N_LEFT * CHUNK
HALO = 16
LANES = 128
VMEM_LIMIT = 56 * 1024 * 1024

ADAM_LR = 0.001
ADAM_B1 = 0.9
ADAM_B2 = 0.999
ADAM_EPS = 1e-08
ADAM_WD = 0.01
ADAM_STEP = 10

MESH = pl.DeviceIdType.MESH


def _params(sem):
    return pltpu.CompilerParams(dimension_semantics=sem, vmem_limit_bytes=VMEM_LIMIT)


def _pick(n, cap):
    if n <= cap:
        return n
    best = None
    for t in range(LANES, cap + 1, LANES):
        if n % t == 0:
            best = t
    assert best is not None, (n, cap)
    return best


MM_TILE_CAP = 1408


def _mm(a, b, mode, out_dtype, name, tm=MM_TILE_CAP, tn_cap=MM_TILE_CAP, tk_cap=MM_TILE_CAP, res=None):
    if mode == "nn":
        (M, K), (K2, N) = a.shape, b.shape
    elif mode == "nt":
        (M, K), (N, K2) = a.shape, b.shape
    else:
        (K, M), (K2, N) = a.shape, b.shape
    assert K == K2, (a.shape, b.shape, mode)
    tm = _pick(M, tm)
    tn = _pick(N, tn_cap)
    tk = _pick(K, tk_cap)
    nk = K // tk
    if mode == "nn":
        dims = (((1,), (0,)), ((), ()))
        a_spec = pl.BlockSpec((tm, tk), lambda i, j, k: (i, k))
        b_spec = pl.BlockSpec((tk, tn), lambda i, j, k: (k, j))
    elif mode == "nt":
        dims = (((1,), (1,)), ((), ()))
        a_spec = pl.BlockSpec((tm, tk), lambda i, j, k: (i, k))
        b_spec = pl.BlockSpec((tn, tk), lambda i, j, k: (j, k))
    else:
        dims = (((0,), (0,)), ((), ()))
        a_spec = pl.BlockSpec((tk, tm), lambda i, j, k: (k, i))
        b_spec = pl.BlockSpec((tk, tn), lambda i, j, k: (k, j))

    o_spec = pl.BlockSpec((tm, tn), lambda i, j, k: (i, j))

    def body(a_ref, b_ref, *rest):
        res_ref = rest[0] if res is not None else None
        o_ref = rest[1] if res is not None else rest[0]
        part = lax.dot_general(a_ref[...].astype(BF16), b_ref[...].astype(BF16), dims, preferred_element_type=F32)
        if nk == 1:
            o_ref[...] = (part if res is None else part + res_ref[...]).astype(out_dtype)
            return
        acc_ref = rest[-1]
        k = pl.program_id(2)

        @pl.when(k == 0)
        def _():
            acc_ref[...] = part

        @pl.when(k > 0)
        def _():
            acc_ref[...] += part

        @pl.when(k == nk - 1)
        def _():
            total = acc_ref[...] if res is None else acc_ref[...] + res_ref[...]
            o_ref[...] = total.astype(out_dtype)

    return pl.pallas_call(
        body, name=name,
        grid=(M // tm, N // tn, nk),
        in_specs=[a_spec, b_spec] + ([o_spec] if res is not None else []),
        out_specs=o_spec,
        out_shape=jax.ShapeDtypeStruct((M, N), out_dtype),
        scratch_shapes=[pltpu.VMEM((tm, tn), F32)] if nk > 1 else [],
        compiler_params=_params(("parallel", "parallel", "arbitrary")),
    )(*((a, b) if res is None else (a, b, res)))


def _rmsnorm_fwd(x, gain, name, tm=512):
    T, C = x.shape

    def body(x_ref, g_ref, h_ref):
        xv = x_ref[...]
        r = lax.rsqrt(jnp.mean(xv * xv, axis=-1, keepdims=True) + EPS)
        h_ref[...] = (xv * r * g_ref[...]).astype(BF16)

    return pl.pallas_call(
        body, name=name, grid=(T // tm,),
        in_specs=[pl.BlockSpec((tm, C), lambda i: (i, 0)), pl.BlockSpec((1, C), lambda i: (0, 0))],
        out_specs=pl.BlockSpec((tm, C), lambda i: (i, 0)),
        out_shape=jax.ShapeDtypeStruct((T, C), BF16),
        compiler_params=_params(("parallel",)),
    )(x, gain.reshape(1, C))


def _rmsnorm_bwd(x, gain, dh, dres, name, tm=512):
    T, C = x.shape

    def body(x_ref, g_ref, dh_ref, dres_ref, dx_ref, dg_ref):
        @pl.when(pl.program_id(0) == 0)
        def _():
            dg_ref[...] = jnp.zeros_like(dg_ref)

        xv = x_ref[...]
        dy = dh_ref[...].astype(F32)
        r = lax.rsqrt(jnp.mean(xv * xv, axis=-1, keepdims=True) + EPS)
        gdy = dy * g_ref[...]
        inner = jnp.mean(xv * gdy, axis=-1, keepdims=True)
        dx_ref[...] = dres_ref[...] + r * gdy - xv * (r * r * r * inner)
        dg_ref[...] += jnp.sum(dy * xv * r, axis=0, keepdims=True)

    return pl.pallas_call(
        body, name=name, grid=(T // tm,),
        in_specs=[pl.BlockSpec((tm, C), lambda i: (i, 0)), pl.BlockSpec((1, C), lambda i: (0, 0)),
                  pl.BlockSpec((tm, C), lambda i: (i, 0)), pl.BlockSpec((tm, C), lambda i: (i, 0))],
        out_specs=[pl.BlockSpec((tm, C), lambda i: (i, 0)), pl.BlockSpec((1, C), lambda i: (0, 0))],
        out_shape=[jax.ShapeDtypeStruct((T, C), F32), jax.ShapeDtypeStruct((1, C), F32)],
        compiler_params=_params(("arbitrary",)),
    )(x, gain.reshape(1, C), dh, dres)


MASKED = -1e30


def _pair_sum(x, same_head):
    hi = x.astype(BF16)
    lo = (x - hi.astype(F32)).astype(BF16)
    return jnp.dot(hi, same_head, preferred_element_type=F32) + jnp.dot(lo, same_head, preferred_element_type=F32)


def _same_head():
    r = lax.broadcasted_iota(jnp.int32, (2 * HEAD_DIM, 2 * HEAD_DIM), 0)
    c = lax.broadcasted_iota(jnp.int32, (2 * HEAD_DIM, 2 * HEAD_DIM), 1)
    return jnp.where((r < HEAD_DIM) == (c < HEAD_DIM), 1.0, 0.0).astype(BF16)


def _pair_norm(t, g, same_head):
    tf = t.astype(F32)
    r = lax.rsqrt(_pair_sum(tf * tf, same_head) * (1.0 / HEAD_DIM) + EPS)
    return tf * r * g


def _pair_norm_bwd(t, g, dn, same_head):
    tf = t.astype(F32)
    r = lax.rsqrt(_pair_sum(tf * tf, same_head) * (1.0 / HEAD_DIM) + EPS)
    gd = dn * g
    inner = _pair_sum(tf * gd, same_head) * (1.0 / HEAD_DIM)
    return r * gd - tf * (r * r * r * inner), jnp.sum(dn * tf * r, axis=0, keepdims=True)


def _band_table(bias):
    rows = [jnp.pad(bias, ((0, 0), (0, 0), (c * CHUNK, 2 * A_BLOCK - BAND - c * CHUNK)), constant_values=MASKED)
            for c in range(N_LEFT)]
    return jnp.concatenate(rows, axis=1).reshape(N_HEADS // 2, 2 * A_BLOCK, 2 * A_BLOCK)


def _band_table_bwd(dtable):
    dtable = dtable.reshape(N_HEADS, A_BLOCK, 2 * A_BLOCK)
    return sum(dtable[:, c * CHUNK:(c + 1) * CHUNK, c * CHUNK:c * CHUNK + BAND] for c in range(N_LEFT))


def _a_specs(T):
    nb = T // A_BLOCK
    pairs = N_HEADS // 2
    col = lambda which: which * pairs
    cur = lambda which: pl.BlockSpec((A_BLOCK, 2 * HEAD_DIM), lambda p, i: (i, col(which) + p))
    prev = lambda which: pl.BlockSpec((A_BLOCK, 2 * HEAD_DIM), lambda p, i: (jnp.maximum(i - 1, 0), col(which) + p))
    nxt = lambda which: pl.BlockSpec((A_BLOCK, 2 * HEAD_DIM), lambda p, i: (jnp.minimum(i + 1, nb - 1), col(which) + p))
    table = pl.BlockSpec((1, 2 * A_BLOCK, 2 * A_BLOCK), lambda p, i: (p, 0, 0))
    gain = pl.BlockSpec((1, 2 * HEAD_DIM), lambda p, i: (0, 0))
    gacc = pl.BlockSpec((1, 1, 2 * HEAD_DIM), lambda p, i: (p, 0, 0))
    return nb, pairs, cur, prev, nxt, table, gain, gacc


def _a_probs(q_ref, kc_ref, kp_ref, t_ref, gq_ref, gk_ref, same_head, first):
    scale = 1.0 / math.sqrt(HEAD_DIM)
    qst = _stack_heads((_pair_norm(q_ref[...], gq_ref[...], same_head) * scale).astype(BF16))
    kcat = jnp.concatenate([_pair_norm(kp_ref[...], gk_ref[...], same_head).astype(BF16),
                            _pair_norm(kc_ref[...], gk_ref[...], same_head).astype(BF16)], axis=0)
    s = lax.dot_general(qst, kcat, (((1,), (1,)), ((), ())), preferred_element_type=F32) + t_ref[0]
    col = lax.broadcasted_iota(jnp.int32, s.shape, 1)
    s = jnp.where(col >= jnp.where(first, A_BLOCK, 0), s, MASKED)
    e = jnp.exp(s - jnp.max(s, axis=-1, keepdims=True))
    return qst, kcat, e, jnp.sum(e, axis=-1, keepdims=True)


def _attn_a_fwd(qkv, table, gq, gk):
    T = qkv.shape[0]
    nb, pairs, cur, prev, _, tspec, gspec, _ = _a_specs(T)

    def body(q_ref, kc_ref, kp_ref, vc_ref, vp_ref, t_ref, gq_ref, gk_ref, o_ref):
        same_head = _same_head()
        _, _, e, total = _a_probs(q_ref, kc_ref, kp_ref, t_ref, gq_ref, gk_ref, same_head, pl.program_id(1) == 0)
        vcat = jnp.concatenate([vp_ref[...], vc_ref[...]], axis=0)
        p = (e / total).astype(BF16)
        o_ref[...] = _unstack_heads(jnp.dot(p, vcat, preferred_element_type=F32)).astype(BF16)

    return pl.pallas_call(
        body, name="attn_a_fwd", grid=(pairs, nb),
        in_specs=[cur(0), cur(1), prev(1), cur(2), prev(2), tspec, gspec, gspec],
        out_specs=pl.BlockSpec((A_BLOCK, 2 * HEAD_DIM), lambda p, i: (i, p)),
        out_shape=jax.ShapeDtypeStruct((T, WIDTH), BF16),
        compiler_params=_params(("parallel", "arbitrary")),
    )(qkv, qkv, qkv, qkv, qkv, table, jnp.tile(gq.reshape(1, HEAD_DIM), (1, 2)), jnp.tile(gk.reshape(1, HEAD_DIM), (1, 2)))


def _attn_a_bwd(qkv, do, table, gq, gk):
    T = qkv.shape[0]
    nb, pairs, cur, prev, _, tspec, gspec, gacc = _a_specs(T)
    scale = 1.0 / math.sqrt(HEAD_DIM)
    oblk = pl.BlockSpec((A_BLOCK, 2 * HEAD_DIM), lambda p, i: (i, p))

    def body(q_ref, kc_ref, kp_ref, vc_ref, vp_ref, do_ref, t_ref, gq_ref, gk_ref,
             dq_ref, dkc_ref, dkp_ref, dvc_ref, dvp_ref, dt_ref, dgq_ref):
        first = pl.program_id(1) == 0

        @pl.when(first)
        def _():
            dt_ref[...] = jnp.zeros_like(dt_ref)
            dgq_ref[...] = jnp.zeros_like(dgq_ref)

        same_head = _same_head()
        qst, kcat, e, total = _a_probs(q_ref, kc_ref, kp_ref, t_ref, gq_ref, gk_ref, same_head, first)
        vcat = jnp.concatenate([vp_ref[...], vc_ref[...]], axis=0)
        dost = _stack_heads(do_ref[...])
        p = e / total
        dp = lax.dot_general(dost, vcat, (((1,), (1,)), ((), ())), preferred_element_type=F32)
        ds = p * (dp - jnp.sum(p * dp, axis=-1, keepdims=True))
        dt_ref[0] += ds
        dsb = ds.astype(BF16)
        dqn = _unstack_heads(jnp.dot(dsb, kcat, preferred_element_type=F32)) * scale
        dq, dg = _pair_norm_bwd(q_ref[...], gq_ref[...], dqn, same_head)
        dq_ref[...] = dq.astype(BF16)
        dgq_ref[0] += dg
        dk = lax.dot_general(dsb, qst, (((0,), (0,)), ((), ())), preferred_element_type=F32)
        dv = lax.dot_general(p.astype(BF16), dost, (((0,), (0,)), ((), ())), preferred_element_type=F32)
        dkp_ref[...] = dk[:A_BLOCK]
        dkc_ref[...] = dk[A_BLOCK:]
        dvp_ref[...] = dv[:A_BLOCK]
        dvc_ref[...] = dv[A_BLOCK:]

    wide = jax.ShapeDtypeStruct((T, WIDTH), F32)
    return pl.pallas_call(
        body, name="attn_a_bwd", grid=(pairs, nb),
        in_specs=[cur(0), cur(1), prev(1), cur(2), prev(2), oblk, tspec, gspec, gspec],
        out_specs=[oblk, oblk, oblk, oblk, oblk, tspec, gacc],
        out_shape=[jax.ShapeDtypeStruct((T, WIDTH), BF16), wide, wide, wide, wide,
                   jax.ShapeDtypeStruct((pairs, 2 * A_BLOCK, 2 * A_BLOCK), F32),
                   jax.ShapeDtypeStruct((pairs, 1, 2 * HEAD_DIM), F32)],
        compiler_params=_params(("parallel", "arbitrary")),
    )(qkv, qkv, qkv, qkv, qkv, do, table, jnp.tile(gq.reshape(1, HEAD_DIM), (1, 2)),
      jnp.tile(gk.reshape(1, HEAD_DIM), (1, 2)))


def _attn_a_bwd_keys(qkv, dkc, dkp, dvc, dvp, gk):
    T = qkv.shape[0]
    nb, pairs, cur, _, _, _, gspec, gacc = _a_specs(T)
    oblk = pl.BlockSpec((A_BLOCK, 2 * HEAD_DIM), lambda p, i: (i, p))
    onext = pl.BlockSpec((A_BLOCK, 2 * HEAD_DIM), lambda p, i: (jnp.minimum(i + 1, nb - 1), p))

    def body(k_ref, dkc_ref, dkp_ref, dvc_ref, dvp_ref, gk_ref, dk_ref, dv_ref, dgk_ref):
        i = pl.program_id(1)

        @pl.when(i == 0)
        def _():
            dgk_ref[...] = jnp.zeros_like(dgk_ref)

        has_next = (i < nb - 1).astype(F32)
        dkn = dkc_ref[...] + has_next * dkp_ref[...]
        dk, dg = _pair_norm_bwd(k_ref[...], gk_ref[...], dkn, _same_head())
        dk_ref[...] = dk.astype(BF16)
        dv_ref[...] = (dvc_ref[...] + has_next * dvp_ref[...]).astype(BF16)
        dgk_ref[0] += dg

    blk = jax.ShapeDtypeStruct((T, WIDTH), BF16)
    return pl.pallas_call(
        body, name="attn_a_bwd_keys", grid=(pairs, nb),
        in_specs=[cur(1), oblk, onext, oblk, onext, gspec],
        out_specs=[oblk, oblk, gacc],
        out_shape=[blk, blk, jax.ShapeDtypeStruct((pairs, 1, 2 * HEAD_DIM), F32)],
        compiler_params=_params(("parallel", "arbitrary")),
    )(qkv, dkc, dkp, dvc, dvp, jnp.tile(gk.reshape(1, HEAD_DIM), (1, 2)))


def _scan_matrix(later):
    r = lax.broadcasted_iota(jnp.int32, (SB_SCAN, SB_SCAN), 0)
    c = lax.broadcasted_iota(jnp.int32, (SB_SCAN, SB_SCAN), 1)
    return jnp.where((r > c) if later else (r < c), 1.0, 0.0).astype(BF16)


def _running_sums(x, carry, scan, later):
    n = SB_KEYS // SB_SCAN
    parts = [None] * n
    total = carry
    for sb in (reversed(range(n)) if later else range(n)):
        xs = x[:, sb * SB_SCAN:(sb + 1) * SB_SCAN]
        local = jnp.dot(xs.astype(BF16), scan, preferred_element_type=F32)
        parts[sb] = local if total is None else local + total
        rowsum = jnp.sum(xs, axis=-1, keepdims=True)
        total = rowsum if total is None else total + rowsum
    return (parts[0] if n == 1 else jnp.concatenate(parts, axis=1)), total


def _sb_log_sigmoids(z):
    neg_abs = pltpu.bitcast(pltpu.bitcast(z, jnp.uint32) | jnp.uint32(0x80000000), F32)
    take = jnp.minimum(z, 0.0) - jnp.log(1.0 + jnp.exp(neg_abs))
    return take, take - z


def _sb_mask():
    r = lax.broadcasted_iota(jnp.int32, (2 * SB_ROWS, SB_KEYS), 0)
    c = lax.broadcasted_iota(jnp.int32, (2 * SB_ROWS, SB_KEYS), 1)
    return c < jnp.where(r >= SB_ROWS, r - SB_ROWS, r)


def _stack_heads(t):
    lane = lax.broadcasted_iota(jnp.int32, t.shape, 1)
    zero = jnp.zeros_like(t)
    return jnp.concatenate([jnp.where(lane < HEAD_DIM, t, zero), jnp.where(lane >= HEAD_DIM, t, zero)], axis=0)


def _unstack_heads(t):
    rows = t.shape[0] // 2
    lane = lax.broadcasted_iota(jnp.int32, (rows, 2 * HEAD_DIM), 1)
    return jnp.where(lane < HEAD_DIM, t[:rows], t[rows:])


def _sb_specs(T):
    nq = T // SB_ROWS
    blk = lambda col: pl.BlockSpec((SB_ROWS, 2 * HEAD_DIM), lambda p, i: (i, col + p))
    full = lambda col: pl.BlockSpec((T, 2 * HEAD_DIM), lambda p, i: (0, col + p))
    return nq, blk, full


def _key_rows(j):
    return pl.ds(pl.multiple_of(j * SB_KEYS, SB_KEYS), SB_KEYS)


def _attn_b_fwd(qkv):
    T = qkv.shape[0]
    pairs = N_HEADS // 2
    nq, blk, full = _sb_specs(T)
    scale = 1.0 / math.sqrt(HEAD_DIM)

    assert nq <= LANES

    def body(q_ref, k_ref, v_ref, o_ref, c_ref, acc_ref, carry_ref, z_ref, w_ref):
        i = pl.program_id(1)
        scan = _scan_matrix(True)
        qst = _stack_heads((q_ref[...].astype(F32) * scale).astype(BF16))
        lane = lax.broadcasted_iota(jnp.int32, (2 * SB_ROWS, LANES), 1)

        def scores(j):
            return lax.dot_general(qst, k_ref[_key_rows(j), :], (((1,), (1,)), ((), ())), preferred_element_type=F32)

        def weights(z, carry, mask):
            take, keep = _sb_log_sigmoids(z)
            if mask is not None:
                keep = jnp.where(mask, keep, 0.0)
            tail, total = _running_sums(keep, carry, scan, True)
            w = jnp.exp(take + tail)
            if mask is not None:
                w = jnp.where(mask, w, 0.0)
            return w.astype(BF16), total

        w_ref[...], carry_ref[...] = weights(scores(i), None, _sb_mask())
        z_ref[...] = scores(jnp.maximum(i - 1, 0))
        acc_ref[...] = jnp.zeros_like(acc_ref)
        c_ref[0, 0] = jnp.zeros((2 * SB_ROWS, LANES), F32)

        @pl.loop(0, i)
        def _(jj):
            j = i - 1 - jj
            z = z_ref[...]
            z_ref[...] = scores(jnp.maximum(j - 1, 0))
            acc_ref[...] += jnp.dot(w_ref[...], v_ref[_key_rows(j + 1), :], preferred_element_type=F32)
            carry = carry_ref[...]
            c_ref[0, 0] = jnp.where(lane == j, carry, c_ref[0, 0])
            w_ref[...], carry_ref[...] = weights(z, carry, None)

        acc = acc_ref[...] + jnp.dot(w_ref[...], v_ref[_key_rows(0), :], preferred_element_type=F32)
        o_ref[...] = _unstack_heads(acc).astype(BF16)

    return pl.pallas_call(
        body, name="attn_b_fwd", grid=(pairs, nq),
        in_specs=[blk(3 * pairs), full(4 * pairs), full(5 * pairs)],
        out_specs=[pl.BlockSpec((SB_ROWS, 2 * HEAD_DIM), lambda p, i: (i, p)),
                   pl.BlockSpec((1, 1, 2 * SB_ROWS, LANES), lambda p, i: (p, i, 0, 0))],
        out_shape=[jax.ShapeDtypeStruct((T, WIDTH), BF16), jax.ShapeDtypeStruct((pairs, nq, 2 * SB_ROWS, LANES), F32)],
        scratch_shapes=[pltpu.VMEM((2 * SB_ROWS, 2 * HEAD_DIM), F32), pltpu.VMEM((2 * SB_ROWS, 1), F32),
                        pltpu.VMEM((2 * SB_ROWS, SB_KEYS), F32), pltpu.VMEM((2 * SB_ROWS, SB_KEYS), BF16)],
        compiler_params=_params(("parallel", "arbitrary")),
    )(qkv, qkv, qkv)


def _attn_b_bwd(qkv, carries, do):
    T = qkv.shape[0]
    pairs = N_HEADS // 2
    nq, blk, full = _sb_specs(T)
    scale = 1.0 / math.sqrt(HEAD_DIM)
    oblk = pl.BlockSpec((SB_ROWS, 2 * HEAD_DIM), lambda p, i: (i, p))
    ofull = pl.BlockSpec((2 * HEAD_DIM, T), lambda p, i: (p, 0))

    def body(q_ref, k_ref, v_ref, c_ref, do_ref, dq_ref, dk_ref, dv_ref, dqacc_ref, before_ref,
             z_ref, dw_ref, dz_ref, w_ref):
        i = pl.program_id(1)

        @pl.when(i == 0)
        def _():
            dk_ref[...] = jnp.zeros_like(dk_ref)
            dv_ref[...] = jnp.zeros_like(dv_ref)

        scan_later = _scan_matrix(True)
        scan_earlier = _scan_matrix(False)
        qst = _stack_heads((q_ref[...].astype(F32) * scale).astype(BF16))
        dost = _stack_heads(do_ref[...].astype(BF16))
        lane = lax.broadcasted_iota(jnp.int32, (2 * SB_ROWS, LANES), 1)
        nt = (((1,), (1,)), ((), ()))

        def products(j):
            return (lax.dot_general(qst, k_ref[_key_rows(j), :], nt, preferred_element_type=F32),
                    lax.dot_general(dost, v_ref[_key_rows(j), :], nt, preferred_element_type=F32))

        def score_grads(z, dw, later, mask):
            take, keep = _sb_log_sigmoids(z)
            sig = jnp.exp(take)
            if mask is not None:
                keep = jnp.where(mask, keep, 0.0)
            tail, _ = _running_sums(keep, later, scan_later, True)
            w = jnp.exp(take + tail)
            if mask is not None:
                w = jnp.where(mask, w, 0.0)
            g = w * dw
            before, before_ref[...] = _running_sums(g, before_ref[...], scan_earlier, False)
            dz = g - sig * (g + before)
            if mask is not None:
                dz = jnp.where(mask, dz, 0.0)
            return dz.astype(BF16), w.astype(BF16)

        qst_t = qst.T
        dost_t = dost.T

        def accumulate(j, dzb, wb):
            cols = pl.ds(pl.multiple_of(j * SB_KEYS, SB_KEYS), SB_KEYS)
            dqacc_ref[...] += jnp.dot(dzb, k_ref[_key_rows(j), :], preferred_element_type=F32)
            dk_ref[:, cols] += jnp.dot(qst_t, dzb, preferred_element_type=F32)
            dv_ref[:, cols] += jnp.dot(dost_t, wb, preferred_element_type=F32)

        dqacc_ref[...] = jnp.zeros_like(dqacc_ref)
        before_ref[...] = jnp.zeros_like(before_ref)
        dz_ref[...] = jnp.zeros_like(dz_ref)
        w_ref[...] = jnp.zeros_like(w_ref)
        z_ref[...], dw_ref[...] = products(0)

        @pl.loop(0, i)
        def _(j):
            z, dw = z_ref[...], dw_ref[...]
            accumulate(jnp.maximum(j - 1, 0), dz_ref[...], w_ref[...])
            later = jnp.sum(jnp.where(lane == j, c_ref[0, 0], 0.0), axis=-1, keepdims=True)
            dz_ref[...], w_ref[...] = score_grads(z, dw, later, None)
            z_ref[...], dw_ref[...] = products(j + 1)

        accumulate(jnp.maximum(i - 1, 0), dz_ref[...], w_ref[...])
        accumulate(i, *score_grads(z_ref[...], dw_ref[...], None, _sb_mask()))
        dq_ref[...] = (_unstack_heads(dqacc_ref[...]) * scale).astype(BF16)

    wide = jax.ShapeDtypeStruct((WIDTH, T), F32)
    return pl.pallas_call(
        body, name="attn_b_bwd", grid=(pairs, nq),
        in_specs=[blk(3 * pairs), full(4 * pairs), full(5 * pairs),
                  pl.BlockSpec((1, 1, 2 * SB_ROWS, LANES), lambda p, i: (p, i, 0, 0)), oblk],
        out_specs=[oblk, ofull, ofull],
        out_shape=[jax.ShapeDtypeStruct((T, WIDTH), BF16), wide, wide],
        scratch_shapes=[pltpu.VMEM((2 * SB_ROWS, 2 * HEAD_DIM), F32), pltpu.VMEM((2 * SB_ROWS, 1), F32),
                        pltpu.VMEM((2 * SB_ROWS, SB_KEYS), F32), pltpu.VMEM((2 * SB_ROWS, SB_KEYS), F32),
                        pltpu.VMEM((2 * SB_ROWS, SB_KEYS), BF16), pltpu.VMEM((2 * SB_ROWS, SB_KEYS), BF16)],
        compiler_params=_params(("parallel", "arbitrary")),
    )(qkv, qkv, qkv, carries, do)


def _window_sums(ext, forward):
    n = ext.shape[0]
    out = []
    s = ext
    for step in (1, 2, 4, 8):
        s = s + pltpu.roll(s, (n - step) if forward else step, 0)
        out.append(s)
    return out


def _pool_counts(base, rows, win):
    t = base + lax.broadcasted_iota(jnp.int32, (rows, 1), 0)
    return jnp.minimum(t + 1, win).astype(F32)


def _pooled(u_ref, up_ref, i, tm):
    prev = jnp.where(i > 0, up_ref[...], 0.0)
    ext = jnp.concatenate([prev, u_ref[...]], axis=0)
    sums = _window_sums(ext, False)
    parts = []
    for g, win in enumerate(POOL_WINDOWS):
        cols = slice(g * POOL_DIM, (g + 1) * POOL_DIM)
        cnt = _pool_counts(i * tm, tm, win)
        parts.append(sums[g][HALO:, cols] / cnt - ext[HALO:, cols])
    return parts


def _pool_fwd(ucg, w_pool, scale, tm=512):
    T = ucg.shape[0]
    C = WIDTH

    def body(u_ref, up_ref, w_ref, s_ref, o_ref):
        i = pl.program_id(0)
        parts = _pooled(u_ref, up_ref, i, tm)
        for g in range(len(POOL_WINDOWS)):
            mixed = jnp.dot(parts[g].astype(BF16), w_ref[g], preferred_element_type=F32)
            o_ref[:, g * POOL_DIM:(g + 1) * POOL_DIM] = (mixed * s_ref[:, g * POOL_DIM:(g + 1) * POOL_DIM]).astype(BF16)

    return pl.pallas_call(
        body, name="pool_fwd", grid=(T // tm,),
        in_specs=[pl.BlockSpec((tm, C), lambda i: (i, 0)),
                  pl.BlockSpec((HALO, C), lambda i: (jnp.maximum(i * (tm // HALO) - 1, 0), 0)),
                  pl.BlockSpec((len(POOL_WINDOWS), POOL_DIM, POOL_DIM), lambda i: (0, 0, 0)),
                  pl.BlockSpec((1, C), lambda i: (0, 0))],
        out_specs=pl.BlockSpec((tm, C), lambda i: (i, 0)),
        out_shape=jax.ShapeDtypeStruct((T, C), BF16),
        compiler_params=_params(("parallel",)),
    )(ucg, ucg, w_pool.astype(BF16), scale.reshape(1, C))


def _pool_bwd(ucg, do_c, w_pool, scale, tm=512):
    T = ucg.shape[0]
    C = WIDTH
    nt = T // tm
    G = len(POOL_WINDOWS)

    def body(u_ref, up_ref, do_ref, don_ref, w_ref, s_ref, du_ref, dw_ref, ds_ref):
        i = pl.program_id(0)

        @pl.when(i == 0)
        def _():
            dw_ref[...] = jnp.zeros_like(dw_ref)
            ds_ref[...] = jnp.zeros_like(ds_ref)

        parts = _pooled(u_ref, up_ref, i, tm)
        nxt = jnp.where(i < nt - 1, don_ref[...].astype(F32), 0.0)
        do_ext = jnp.concatenate([do_ref[...].astype(F32), nxt], axis=0) * s_ref[...]
        for g, win in enumerate(POOL_WINDOWS):
            cols = slice(g * POOL_DIM, (g + 1) * POOL_DIM)
            pooled_b = parts[g].astype(BF16)
            dmix = do_ext[:, cols].astype(BF16)
            mixed = jnp.dot(pooled_b, w_ref[g], preferred_element_type=F32)
            ds_ref[:, cols] += jnp.sum(do_ref[:, cols].astype(F32) * mixed, axis=0, keepdims=True)
            dw_ref[g] += lax.dot_general(pooled_b, dmix[:tm], (((0,), (0,)), ((), ())), preferred_element_type=F32)
            dpool = lax.dot_general(dmix, w_ref[g], (((1,), (1,)), ((), ())), preferred_element_type=F32)
            scaled = dpool / _pool_counts(i * tm, tm + HALO, win)
            fwd = _window_sums(scaled, True)[g]
            du_ref[:, cols] = (fwd[:tm] - dpool[:tm]).astype(BF16)

    return pl.pallas_call(
        body, name="pool_bwd", grid=(nt,),
        in_specs=[pl.BlockSpec((tm, C), lambda i: (i, 0)),
                  pl.BlockSpec((HALO, C), lambda i: (jnp.maximum(i * (tm // HALO) - 1, 0), 0)),
                  pl.BlockSpec((tm, C), lambda i: (i, 0)),
                  pl.BlockSpec((HALO, C), lambda i: (jnp.minimum((i + 1) * (tm // HALO), T // HALO - 1), 0)),
                  pl.BlockSpec((G, POOL_DIM, POOL_DIM), lambda i: (0, 0, 0)),
                  pl.BlockSpec((1, C), lambda i: (0, 0))],
        out_specs=[pl.BlockSpec((tm, C), lambda i: (i, 0)),
                   pl.BlockSpec((G, POOL_DIM, POOL_DIM), lambda i: (0, 0, 0)),
                   pl.BlockSpec((1, C), lambda i: (0, 0))],
        out_shape=[jax.ShapeDtypeStruct((T, C), BF16), jax.ShapeDtypeStruct((G, POOL_DIM, POOL_DIM), F32),
                   jax.ShapeDtypeStruct((1, C), F32)],
        compiler_params=_params(("arbitrary",)),
    )(ucg, ucg, do_c, do_c, w_pool.astype(BF16), scale.reshape(1, C))


def _merge_fwd(oa, ob, oc, glog, b_gate, wa, wb, wc, tm=256):
    T = oa.shape[0]
    Dm = D_MODEL
    row = lambda c: pl.BlockSpec((tm, c), lambda i: (i, 0))
    wspec = pl.BlockSpec((WIDTH, Dm), lambda i: (0, 0))

    def body(oa_ref, ob_ref, oc_ref, g_ref, b_ref, wa_ref, wb_ref, wc_ref, m_ref, ya_ref, yb_ref, yc_ref):
        merged = jnp.zeros((tm, Dm), F32)
        for kk, (o_ref, w_ref, y_ref) in enumerate(((oa_ref, wa_ref, ya_ref), (ob_ref, wb_ref, yb_ref),
                                                    (oc_ref, wc_ref, yc_ref))):
            y = jnp.dot(o_ref[...].astype(BF16), w_ref[...], preferred_element_type=F32)
            gate = jax.nn.sigmoid(g_ref[:, kk * Dm:(kk + 1) * Dm] + b_ref[:, kk * Dm:(kk + 1) * Dm])
            merged = merged + gate * y
            y_ref[...] = y.astype(BF16)
        m_ref[...] = merged.astype(BF16)

    out = jax.ShapeDtypeStruct((T, Dm), BF16)
    return pl.pallas_call(
        body, name="merge_fwd", grid=(T // tm,),
        in_specs=[row(WIDTH), row(WIDTH), row(WIDTH), row(3 * Dm), pl.BlockSpec((1, 3 * Dm), lambda i: (0, 0)),
                  wspec, wspec, wspec],
        out_specs=[row(Dm)] * 4,
        out_shape=[out] * 4,
        compiler_params=_params(("parallel",)),
    )(oa, ob, oc, glog, b_gate.reshape(1, 3 * Dm), wa, wb, wc)


def _merge_bwd(dmerged, glog, b_gate, ya, yb, yc, tm=256):
    T = dmerged.shape[0]
    Dm = D_MODEL
    row = lambda c: pl.BlockSpec((tm, c), lambda i: (i, 0))

    def body(dm_ref, g_ref, b_ref, ya_ref, yb_ref, yc_ref, dya_ref, dyb_ref, dyc_ref, dg_ref, db_ref):
        @pl.when(pl.program_id(0) == 0)
        def _():
            db_ref[...] = jnp.zeros_like(db_ref)

        dm = dm_ref[...]
        for kk, (y_ref, dy_ref) in enumerate(((ya_ref, dya_ref), (yb_ref, dyb_ref), (yc_ref, dyc_ref))):
            cols = slice(kk * Dm, (kk + 1) * Dm)
            gate = jax.nn.sigmoid(g_ref[:, cols] + b_ref[:, cols])
            dy_ref[...] = (dm * gate).astype(BF16)
            dlog = dm * y_ref[...].astype(F32) * gate * (1.0 - gate)
            dg_ref[:, cols] = dlog.astype(BF16)
            db_ref[:, cols] += jnp.sum(dlog, axis=0, keepdims=True)

    out = jax.ShapeDtypeStruct((T, Dm), BF16)
    return pl.pallas_call(
        body, name="merge_bwd", grid=(T // tm,),
        in_specs=[row(Dm), row(3 * Dm), pl.BlockSpec((1, 3 * Dm), lambda i: (0, 0)), row(Dm), row(Dm), row(Dm)],
        out_specs=[row(Dm), row(Dm), row(Dm), row(3 * Dm), pl.BlockSpec((1, 3 * Dm), lambda i: (0, 0))],
        out_shape=[out, out, out, jax.ShapeDtypeStruct((T, 3 * Dm), BF16), jax.ShapeDtypeStruct((1, 3 * Dm), F32)],
        compiler_params=_params(("arbitrary",)),
    )(dmerged, glog, b_gate.reshape(1, 3 * Dm), ya, yb, yc)


def _residual_add(x, y, name, tm=512):
    T, C = x.shape

    def body(x_ref, y_ref, o_ref):
        o_ref[...] = x_ref[...] + y_ref[...]

    spec = pl.BlockSpec((tm, C), lambda i: (i, 0))
    return pl.pallas_call(body, name=name, grid=(T // tm,), in_specs=[spec, spec], out_specs=spec,
                          out_shape=jax.ShapeDtypeStruct((T, C), F32), compiler_params=_params(("parallel",)))(x, y)


FF_TILE = 256
FF_TILES = D_FF // FF_TILE
CONV_HALO = 16
FF_ROWS = 32


def _ff_pair_order(w):
    lead = w.shape[:-1]
    n = len(lead)
    w = w.reshape(*lead, 2, FF_TILES, FF_TILE)
    return jnp.swapaxes(w, n, n + 1).reshape(*lead, 2 * D_FF)


def _ff_natural_order(w):
    lead = w.shape[:-1]
    n = len(lead)
    w = w.reshape(*lead, FF_TILES, 2, FF_TILE)
    return jnp.swapaxes(w, n, n + 1).reshape(*lead, 2 * D_FF)


def _conv(ext, w_ref, b_ref):
    c = b_ref[...] + w_ref[2:3, :] * ext
    c = c + w_ref[1:2, :] * pltpu.roll(ext, 1, 0)
    c = c + w_ref[0:1, :] * pltpu.roll(ext, 2, 0)
    return c[CONV_HALO:]


def _ff_specs(T, tm):
    pair = pl.BlockSpec((tm, 2 * FF_TILE), lambda i, j: (i, j))
    prev = pl.BlockSpec((CONV_HALO, 2 * FF_TILE), lambda i, j: (jnp.maximum(i * (tm // CONV_HALO) - 1, 0), j))
    nxt = pl.BlockSpec((CONV_HALO, 2 * FF_TILE),
                       lambda i, j: (jnp.minimum((i + 1) * (tm // CONV_HALO), T // CONV_HALO - 1), j))
    half = pl.BlockSpec((tm, FF_TILE), lambda i, j: (i, j))
    small = lambda r: pl.BlockSpec((r, 2 * FF_TILE), lambda i, j: (0, j))
    return pair, prev, nxt, half, small


def _ff_row_chunks(u_ref, halo, tm, chunk):
    chunk(0, jnp.concatenate([halo, u_ref[0:FF_ROWS, :]], axis=0).astype(F32))

    @pl.loop(1, tm // FF_ROWS)
    def _(c):
        r0 = pl.multiple_of(c * FF_ROWS, FF_ROWS)
        start = pl.multiple_of(r0 - CONV_HALO, CONV_HALO)
        chunk(r0, u_ref[pl.ds(start, FF_ROWS + CONV_HALO), :].astype(F32))


def _swap_grid(spec):
    return pl.BlockSpec(spec.block_shape, lambda j, i, f=spec.index_map: f(i, j))


def _ff_act_fwd(u, conv_w, conv_b, tm=1024):
    T = u.shape[0]
    pair, prev, _, half, small = _ff_specs(T, tm)

    def body(u_ref, p_ref, w_ref, b_ref, a_ref):
        i = pl.program_id(0)
        ext = jnp.concatenate([jnp.where(i > 0, p_ref[...], 0.0), u_ref[...]], axis=0).astype(F32)
        c = _conv(ext, w_ref, b_ref)
        cg, cv = c[:, :FF_TILE], c[:, FF_TILE:]
        a_ref[...] = (cg * jax.nn.sigmoid(cg) * cv).astype(BF16)

    return pl.pallas_call(
        body, name="ff_act_fwd", grid=(T // tm, FF_TILES),
        in_specs=[pair, prev, small(3), small(1)],
        out_specs=half,
        out_shape=jax.ShapeDtypeStruct((T, D_FF), BF16),
        compiler_params=_params(("parallel", "parallel")),
    )(u, u, conv_w, conv_b.reshape(1, -1))


def _ff_act_bwd(u, da, conv_w, conv_b, tm=1024):
    T = u.shape[0]
    pair, prev, _, half, small = _ff_specs(T, tm)

    def body(u_ref, p_ref, da_ref, w_ref, b_ref, dc_ref, dw_ref, db_ref, sums_ref):
        i = pl.program_id(1)

        @pl.when(i == 0)
        def _():
            dw_ref[...] = jnp.zeros_like(dw_ref)
            db_ref[...] = jnp.zeros_like(db_ref)

        sums_ref[...] = jnp.zeros_like(sums_ref)

        def fold(x):
            return jnp.sum(x.reshape(FF_ROWS // 8, 8, x.shape[-1]), axis=0)

        def chunk(r0, ext):
            c = _conv(ext, w_ref, b_ref)
            cg, cv = c[:, :FF_TILE], c[:, FF_TILE:]
            da = da_ref[pl.ds(r0, FF_ROWS), :].astype(F32)
            sg = jax.nn.sigmoid(cg)
            dc = jnp.concatenate([da * cv * sg * (1.0 + cg * (1.0 - sg)), da * cg * sg], axis=1)
            dc_ref[pl.ds(r0, FF_ROWS), :] = dc.astype(BF16)
            sums_ref[3] += fold(dc)
            sums_ref[2] += fold(dc * ext[CONV_HALO:])
            sums_ref[1] += fold(dc * pltpu.roll(ext, 1, 0)[CONV_HALO:])
            sums_ref[0] += fold(dc * pltpu.roll(ext, 2, 0)[CONV_HALO:])

        _ff_row_chunks(u_ref, jnp.where(i > 0, p_ref[...], 0.0), tm, chunk)
        for tap in range(3):
            dw_ref[tap:tap + 1, :] += jnp.sum(sums_ref[tap], axis=0, keepdims=True)
        db_ref[...] += jnp.sum(sums_ref[3], axis=0, keepdims=True)

    return pl.pallas_call(
        body, name="ff_act_bwd", grid=(FF_TILES, T // tm),
        in_specs=[_swap_grid(pair), _swap_grid(prev), _swap_grid(half), _swap_grid(small(3)), _swap_grid(small(1))],
        out_specs=[_swap_grid(pair), _swap_grid(small(3)), _swap_grid(small(1))],
        out_shape=[jax.ShapeDtypeStruct((T, 2 * D_FF), BF16), jax.ShapeDtypeStruct((3, 2 * D_FF), F32),
                   jax.ShapeDtypeStruct((1, 2 * D_FF), F32)],
        scratch_shapes=[pltpu.VMEM((4, 8, 2 * FF_TILE), F32)],
        compiler_params=_params(("parallel", "arbitrary")),
    )(u, u, da, conv_w, conv_b.reshape(1, -1))


def _ff_conv_bwd(dc, conv_w, tm=1024):
    T = dc.shape[0]
    nt = T // tm
    pair, _, nxt, _, small = _ff_specs(T, tm)
    halo = CONV_HALO

    def body(dc_ref, n_ref, w_ref, du_ref):
        i = pl.program_id(0)
        following = jnp.where(i < nt - 1, n_ref[...].astype(F32), 0.0)
        ext = jnp.concatenate([dc_ref[...].astype(F32), following], axis=0)
        n = tm + halo
        du = w_ref[2:3, :] * ext + w_ref[1:2, :] * pltpu.roll(ext, n - 1, 0) + w_ref[0:1, :] * pltpu.roll(ext, n - 2, 0)
        du_ref[...] = du[:tm].astype(BF16)

    return pl.pallas_call(
        body, name="ff_conv_bwd", grid=(nt, FF_TILES),
        in_specs=[pair, nxt, small(3)],
        out_specs=pair,
        out_shape=jax.ShapeDtypeStruct((T, 2 * D_FF), BF16),
        compiler_params=_params(("parallel", "parallel")),
    )(dc, dc, conv_w)


def _loss_head(y, target, tm=512):
    T, C = y.shape
    nt = T // tm

    def body(y_ref, t_ref, dy_ref, l_ref):
        err = y_ref[...] - t_ref[...]
        dy_ref[...] = err * (1.0 / C)
        part = jnp.sum(err * err, axis=0, keepdims=True) * (0.5 / C)
        l_ref[0] = jnp.broadcast_to(part, (8, C))

    spec = pl.BlockSpec((tm, C), lambda i: (i, 0))
    dy, parts = pl.pallas_call(
        body, name="loss_head", grid=(nt,),
        in_specs=[spec, spec],
        out_specs=[spec, pl.BlockSpec((1, 8, C), lambda i: (i, 0, 0))],
        out_shape=[jax.ShapeDtypeStruct((T, C), F32), jax.ShapeDtypeStruct((nt, 8, C), F32)],
        compiler_params=_params(("parallel",)),
    )(y, target)
    return dy, jnp.sum(parts[:, 0, :])


def _adamw_math(w, g, m, v):
    m = ADAM_B1 * m + (1.0 - ADAM_B1) * g
    v = ADAM_B2 * v + (1.0 - ADAM_B2) * (g * g)
    m_hat = m / (1.0 - ADAM_B1 ** ADAM_STEP)
    v_hat = v / (1.0 - ADAM_B2 ** ADAM_STEP)
    delta = -ADAM_LR * (m_hat / (jnp.sqrt(v_hat) + ADAM_EPS) + ADAM_WD * w)
    return delta, m, v


def _adamw(parts, w, m, v, name, tm=256):
    R, C = w.shape
    tm = _pick_rows(R, tm)

    def body(p_ref, w_ref, m_ref, v_ref, g_ref, d_ref, nm_ref, nv_ref):
        g = p_ref[0].astype(F32)
        for s in range(1, N_DEV):
            g = g + p_ref[s].astype(F32)
        delta, nm, nv = _adamw_math(w_ref[...], g, m_ref[...], v_ref[...])
        g_ref[...] = g
        d_ref[...] = delta
        nm_ref[...] = nm
        nv_ref[...] = nv

    spec = pl.BlockSpec((tm, C), lambda i: (i, 0))
    out = jax.ShapeDtypeStruct((R, C), F32)
    return pl.pallas_call(
        body, name=name, grid=(R // tm,),
        in_specs=[pl.BlockSpec((N_DEV, tm, C), lambda i: (0, i, 0)), spec, spec, spec],
        out_specs=[spec] * 4,
        out_shape=[out] * 4,
        compiler_params=_params(("parallel",)),
    )(parts, w, m, v)


def _pick_rows(n, cap):
    if n < 16:
        return n
    best = None
    for t in range(16, min(n, cap) + 1, 16):
        if n % t == 0:
            best = t
    assert best is not None, (n, cap)
    return best


def _exchange(srcs, scatter, name):
    n = len(srcs)

    def body(*refs):
        src_refs, out_refs = refs[:n], refs[n:2 * n]
        send_sems, recv_sems, local_sems = refs[2 * n:]
        x, y, c = lax.axis_index("x"), lax.axis_index("y"), lax.axis_index("c")
        me = 4 * x + 2 * y + c

        def piece(a, d):
            return src_refs[a].at[d] if scatter else src_refs[a]

        local = [pltpu.make_async_copy(piece(a, me), out_refs[a].at[me], local_sems.at[a]) for a in range(n)]
        for cp in local:
            cp.start()
        copies = []
        for k in range(1, N_DEV):
            where, peer = _peer(k)
            for a in range(n):
                cp = pltpu.make_async_remote_copy(
                    src_ref=piece(a, peer), dst_ref=out_refs[a].at[me],
                    send_sem=send_sems.at[a * N_DEV + k], recv_sem=recv_sems.at[a * N_DEV + k],
                    device_id=where, device_id_type=MESH)
                cp.start()
                copies.append((cp, a, k, peer))
        for cp, a, k, peer in copies:
            cp.wait_send()
            pltpu.make_async_remote_copy(
                src_ref=piece(a, peer), dst_ref=out_refs[a].at[peer],
                send_sem=send_sems.at[a * N_DEV + k], recv_sem=recv_sems.at[a * N_DEV + k],
                device_id=(x, y, c), device_id_type=MESH).wait_recv()
        for cp in local:
            cp.wait()

    slab = lambda s: tuple(s.shape[1:] if scatter else s.shape)
    return pl.pallas_call(
        body, name=name,
        in_specs=[pl.BlockSpec(memory_space=pl.ANY)] * n,
        out_specs=[pl.BlockSpec(memory_space=pl.ANY)] * n,
        out_shape=[jax.ShapeDtypeStruct((N_DEV,) + slab(s), s.dtype) for s in srcs],
        scratch_shapes=[pltpu.SemaphoreType.DMA((n * N_DEV,)), pltpu.SemaphoreType.DMA((n * N_DEV,)),
                        pltpu.SemaphoreType.DMA((n,))],
    )(*srcs)


def _peer(k):
    x, y, c = lax.axis_index("x"), lax.axis_index("y"), lax.axis_index("c")
    px = 1 - x if k & 4 else x
    py = 1 - y if k & 2 else y
    pc = 1 - c if k & 1 else c
    return (px, py, pc), 4 * px + 2 * py + pc


def _split_copies(src_refs, land_refs, send_sems, recv_sems, scatter):
    x, y, c = lax.axis_index("x"), lax.axis_index("y"), lax.axis_index("c")
    me = 4 * x + 2 * y + c
    sends, arrivals = [], []
    for k in range(1, N_DEV):
        where, peer = _peer(k)
        for a, (src, land) in enumerate(zip(src_refs, land_refs)):
            piece = src.at[peer] if scatter else src
            sends.append(pltpu.make_async_remote_copy(
                src_ref=piece, dst_ref=land.at[me], send_sem=send_sems.at[a * N_DEV + k], recv_sem=recv_sems.at[a * N_DEV + k],
                device_id=where, device_id_type=MESH))
            arrivals.append(pltpu.make_async_remote_copy(
                src_ref=piece, dst_ref=land.at[peer], send_sem=send_sems.at[a * N_DEV + k], recv_sem=recv_sems.at[a * N_DEV + k],
                device_id=(x, y, c), device_id_type=MESH))
    return sends, arrivals


def _exchange_start(srcs, scatter, name):
    n = len(srcs)
    slab = lambda s: tuple(s.shape[1:] if scatter else s.shape)
    lands = [lax.empty((N_DEV,) + slab(s), s.dtype) for s in srcs]

    def body(*refs):
        src_refs, land_refs = refs[:n], refs[n:2 * n]
        send_sems, recv_sems = refs[2 * n], refs[2 * n + 1]
        token = refs[-1]
        sends, _ = _split_copies(src_refs, land_refs, send_sems, recv_sems, scatter)
        for cp in sends:
            cp.start()
        token[...] = jnp.zeros_like(token)

    hbm = pl.BlockSpec(memory_space=pltpu.HBM)
    sem = pl.BlockSpec(memory_space=pltpu.SEMAPHORE)
    out = pl.pallas_call(
        body, name=name,
        in_specs=[hbm] * (2 * n),
        out_specs=[sem, sem] + [hbm] * (2 * n) + [pl.BlockSpec(memory_space=pltpu.VMEM)],
        out_shape=[pltpu.SemaphoreType.DMA((n * N_DEV,)), pltpu.SemaphoreType.DMA((n * N_DEV,))]
        + [pltpu.HBM(s.shape, s.dtype) for s in srcs] + [pltpu.HBM(l.shape, l.dtype) for l in lands]
        + [jax.ShapeDtypeStruct((8, LANES), F32)],
        input_output_aliases={j: 2 + j for j in range(2 * n)},
        compiler_params=pltpu.CompilerParams(has_side_effects=pltpu.SideEffectType.DATAFLOW_SIDE_EFFECTING),
    )(*[pltpu.with_memory_space_constraint(s, pltpu.HBM) for s in srcs],
      *[pltpu.with_memory_space_constraint(l, pltpu.HBM) for l in lands])
    return (out[0], out[1], out[2:2 + n], out[2 + n:2 + 2 * n]), out[-1]


def _exchange_finish(state, scatter, after, name):
    send_sems, recv_sems, srcs, lands = state
    n = len(srcs)

    def body(*refs):
        src_refs, land_refs = refs[:n], refs[n:2 * n]
        sends, arrivals = _split_copies(src_refs, land_refs, refs[2 * n], refs[2 * n + 1], scatter)
        for cp in sends:
            cp.wait_send()
        for cp in arrivals:
            cp.wait_recv()

    hbm = pl.BlockSpec(memory_space=pltpu.HBM)
    sem = pl.BlockSpec(memory_space=pltpu.SEMAPHORE)
    out = pl.pallas_call(
        body, name=name,
        in_specs=[hbm] * (2 * n) + [sem, sem, pl.BlockSpec(memory_space=pl.ANY)],
        out_specs=[hbm] * (2 * n),
        out_shape=[pltpu.HBM(s.shape, s.dtype) for s in srcs] + [pltpu.HBM(l.shape, l.dtype) for l in lands],
        input_output_aliases={j: j for j in range(2 * n)},
        compiler_params=pltpu.CompilerParams(has_side_effects=pltpu.SideEffectType.DATAFLOW_SIDE_EFFECTING),
    )(*srcs, *lands, send_sems, recv_sems, after)
    return list(out[n:])


def _own_slab(landed, src, scatter):
    me = 4 * lax.axis_index("x") + 2 * lax.axis_index("y") + lax.axis_index("c")
    own = lax.dynamic_index_in_dim(src, me, axis=0, keepdims=True) if scatter else src[None]
    return lax.dynamic_update_slice_in_dim(landed, own, me, axis=0)


SHARDED = ("w_in", "w_branch_a", "w_branch_b", "w_branch_c", "w_out", "w_up", "w_down")
REPLICATED = ("norm_mix", "b_gate", "q_norm_a", "k_norm_a", "rel_bias_a", "w_pool", "pool_scale", "norm_ffn", "conv_b")
WEIGHTS = ("norm_mix", "w_in", "b_gate", "q_norm_a", "k_norm_a", "rel_bias_a", "w_pool", "pool_scale",
           "w_branch_a", "w_branch_b", "w_branch_c", "w_out", "norm_ffn", "w_up", "conv_w", "conv_b", "w_down")
SMALL_COLS = 128
QKV_COLS = 6 * WIDTH


def _rel_index():
    q_off = jnp.arange(CHUNK)[:, None] + N_LEFT * CHUNK
    k_off = jnp.arange(BAND)[None, :]
    return jnp.clip(q_off - k_off, -(CHUNK - 1), MAX_REL) + (CHUNK - 1)


def _rel_onehot():
    rel = _rel_index().reshape(1, CHUNK * BAND)
    return (rel == jnp.arange(REL_TABLE)[:, None]).astype(BF16)


def _select_mm(x, onehot, mode, name):
    hi = x.astype(BF16)
    r1 = x - hi.astype(F32)
    mid = r1.astype(BF16)
    lo = (r1 - mid.astype(F32)).astype(BF16)
    y = _mm(jnp.concatenate([hi, mid, lo, jnp.zeros_like(hi)], axis=0), onehot, mode, F32, name)
    n = x.shape[0]
    return y[:n] + y[n:2 * n] + y[2 * n:3 * n]


def _pack_rows(arrays, cols, row_multiple):
    flat = jnp.concatenate([a.reshape(-1) for a in arrays])
    rows = -(-flat.shape[0] // cols)
    rows = -(-rows // row_multiple) * row_multiple
    return jnp.pad(flat, (0, rows * cols - flat.shape[0])).reshape(rows, cols)


def _unpack_rows(packed, like):
    flat = packed.reshape(-1)
    out, off = [], 0
    for a in like:
        out.append(flat[off:off + a.size].reshape(a.shape))
        off += a.size
    return out


def kernel(x, norm_mix, w_in, b_gate, q_norm_a, k_norm_a, rel_bias_a, w_pool, pool_scale, w_branch_a, w_branch_b, w_branch_c, w_out, norm_ffn, w_up, conv_w, conv_b, w_down, loss_target, m_norm_mix, m_w_in, m_b_gate, m_q_norm_a, m_k_norm_a, m_rel_bias_a, m_w_pool, m_pool_scale, m_w_branch_a, m_w_branch_b, m_w_branch_c, m_w_out, m_norm_ffn, m_w_up, m_conv_w, m_conv_b, m_w_down, v_norm_mix, v_w_in, v_b_gate, v_q_norm_a, v_k_norm_a, v_rel_bias_a, v_w_pool, v_pool_scale, v_w_branch_a, v_w_branch_b, v_w_branch_c, v_w_out, v_norm_ffn, v_w_up, v_conv_w, v_conv_b, v_w_down):
    args = dict(locals())
    w = {n: args[n] for n in WEIGHTS}
    m = {n: args["m_" + n] for n in WEIGHTS}
    v = {n: args["v_" + n] for n in WEIGHTS}
    L = w_in.shape[0]
    T = x.shape[1]
    xs = x.reshape(T, D_MODEL)
    target = loss_target.reshape(T, D_MODEL)

    exchanged = SHARDED + ("conv_w",)
    row_sharded = ("w_out", "w_down")
    shard = {(n, l): (w[n][l] if n == "conv_w" else w[n][l].astype(BF16)) for n in exchanged for l in range(L)}
    late = [key for key in shard if key != ("w_in", 0)]
    gathered = {("w_in", 0): _exchange([shard["w_in", 0]], False, "gather_first")[0]}
    gather_state, gather_token = _exchange_start([shard[key] for key in late], False, "gather_rest_start")

    def full_weight(key):
        g = gathered[key]
        return g.reshape(-1, g.shape[-1]) if key[0] in row_sharded else g.transpose(1, 0, 2).reshape(g.shape[1], -1)

    w_in_f = {0: full_weight(("w_in", 0))}
    conv_b_f = _ff_pair_order(conv_b)
    onehot = _rel_onehot()

    saved = []
    cur = xs
    full = {}
    for l in range(L):
        w_qkv, w_uc, w_g = w_in_f[l][:, :QKV_COLS], w_in_f[l][:, QKV_COLS:QKV_COLS + WIDTH], w_in_f[l][:, QKV_COLS + WIDTH:]
        gain = norm_mix[l] + gather_token[0, 0] if l == 0 else norm_mix[l]
        h = _rmsnorm_fwd(cur, gain, "norm_mix_fwd")
        qkv = _mm(h, w_qkv, "nn", BF16, "proj_qkv")
        uc = _mm(h, w_uc, "nn", F32, "proj_pool")
        glog = _mm(h, w_g, "nn", F32, "proj_gate")
        table = _band_table(_select_mm(rel_bias_a[l], onehot, "nn", "rel_bias_table").reshape(N_HEADS, CHUNK, BAND))
        oa = _attn_a_fwd(qkv, table, q_norm_a[l], k_norm_a[l])
        ob, carries = _attn_b_fwd(qkv)
        oc = _pool_fwd(uc, w_pool[l], pool_scale[l])
        if l == 0:
            landed = _exchange_finish(gather_state, False, ob, "gather_rest_finish")
            gathered.update({key: _own_slab(g, shard[key], False) for key, g in zip(late, landed)})
            full = {key: full_weight(key) for key in gathered}
            w_in_f.update({k: full["w_in", k] for k in range(1, L)})
        w_a, w_b, w_c = (full["w_branch_" + tag, l] for tag in "abc")
        w_out_f, w_down_f = full["w_out", l], full["w_down", l]
        w_up_f, conv_w_f = _ff_pair_order(full["w_up", l]), _ff_pair_order(full["conv_w", l])
        merged, ya, yb, yc = _merge_fwd(oa, ob, oc, glog, b_gate[l], w_a, w_b, w_c)
        x1 = _mm(merged, w_out_f, "nn", F32, "out_proj", res=cur)
        h2 = _rmsnorm_fwd(x1, norm_ffn[l], "norm_ffn_fwd")
        u = _mm(h2, w_up_f, "nn", BF16, "ff_up")
        act = _ff_act_fwd(u, conv_w_f, conv_b_f[l])
        x2 = _mm(act, w_down_f, "nn", F32, "ff_down", res=x1)
        saved.append(dict(x=cur, h=h, qkv=qkv, carries=carries, uc=uc, glog=glog, table=table, oa=oa, ob=ob, oc=oc,
                          ya=ya, yb=yb, yc=yc, merged=merged, x1=x1, h2=h2, u=u, act=act, w_qkv=w_qkv, w_uc=w_uc,
                          w_g=w_g, w_a=w_a, w_b=w_b, w_c=w_c, w_out=w_out_f, w_up=w_up_f, w_down=w_down_f,
                          conv_w=conv_w_f))
        cur = x2

    dcur, loss_local = _loss_head(cur, target)
    loss = lax.psum(loss_local, ("x", "y", "c"))

    def pieces_of(n, g):
        if n in row_sharded:
            return g.reshape(N_DEV, -1, g.shape[-1])
        return g.reshape(g.shape[0], N_DEV, -1).transpose(1, 0, 2)

    gw = {n: [None] * L for n in WEIGHTS}
    for l in reversed(range(L)):
        s = saved[l]
        da = _mm(dcur, s["w_down"], "nt", BF16, "ff_down_dx", tn_cap=1408)
        gw["w_down"][l] = _mm(s["act"], dcur, "tn", BF16, "ff_down_dw")
        dc, dconv_w, dconv_b = _ff_act_bwd(s["u"], da, s["conv_w"], conv_b_f[l])
        du = _ff_conv_bwd(dc, s["conv_w"])
        dh2 = _mm(du, s["w_up"], "nt", F32, "ff_up_dx")
        gw["w_up"][l] = _ff_natural_order(_mm(s["h2"], du, "tn", BF16, "ff_up_dw"))
        gw["conv_w"][l] = _ff_natural_order(dconv_w)
        gw["conv_b"][l] = _ff_natural_order(dconv_b)[0]
        dx1, dg = _rmsnorm_bwd(s["x1"], norm_ffn[l], dh2, dcur, "norm_ffn_bwd")
        gw["norm_ffn"][l] = dg[0]

        dmerged = _mm(dx1, s["w_out"], "nt", F32, "out_proj_dx")
        gw["w_out"][l] = _mm(s["merged"], dx1, "tn", BF16, "out_proj_dw")
        dya, dyb, dyc, dglog, db_gate = _merge_bwd(dmerged, s["glog"], b_gate[l], s["ya"], s["yb"], s["yc"])
        gw["b_gate"][l] = db_gate[0]
        do = {}
        for tag, dy, ok in (("a", dya, s["oa"]), ("b", dyb, s["ob"]), ("c", dyc, s["oc"])):
            do[tag] = _mm(dy, s["w_" + tag], "nt", BF16, "branch_dx_" + tag)
            gw["w_branch_" + tag][l] = _mm(ok, dy, "tn", BF16, "branch_dw_" + tag)
        duc, dw_pool, dscale = _pool_bwd(s["uc"], do["c"], w_pool[l], pool_scale[l])
        gw["w_pool"][l] = dw_pool
        gw["pool_scale"][l] = dscale[0]
        gain_q = q_norm_a[l]
        if l == 0:
            early = [(n, k) for n in exchanged for k in range(L) if (n, k) != ("w_in", 0)]
            early_pieces = [pieces_of(n, gw[n][k]) for n, k in early]
            grads_state, grads_token = _exchange_start(early_pieces, True, "exchange_early_start")
            gain_q = gain_q + grads_token[0, 0]
        dqa, dkc, dkp, dvc, dvp, dtable, dgq = _attn_a_bwd(s["qkv"], do["a"], s["table"], gain_q, k_norm_a[l])
        dka, dva, dgk = _attn_a_bwd_keys(s["qkv"], dkc, dkp, dvc, dvp, k_norm_a[l])
        gw["q_norm_a"][l] = jnp.sum(dgq.reshape(N_HEADS, HEAD_DIM), axis=0)
        gw["k_norm_a"][l] = jnp.sum(dgk.reshape(N_HEADS, HEAD_DIM), axis=0)
        gw["rel_bias_a"][l] = _select_mm(_band_table_bwd(dtable).reshape(N_HEADS, CHUNK * BAND), onehot, "nt",
                                         "rel_bias_table_dw")
        dqb, dkb, dvb = _attn_b_bwd(s["qkv"], s["carries"], do["b"])
        dqkv = jnp.concatenate([dqa, dka, dva, dqb, dkb.T.astype(BF16), dvb.T.astype(BF16)], axis=1)
        dh = _mm(dqkv, s["w_qkv"], "nt", F32, "proj_qkv_dx")
        dh = _mm(duc, s["w_uc"], "nt", F32, "proj_pool_dx", res=dh)
        dh = _mm(dglog, s["w_g"], "nt", F32, "proj_gate_dx", res=dh)
        gw["w_in"][l] = jnp.concatenate([_mm(s["h"], dqkv, "tn", BF16, "proj_qkv_dw"),
                                         _mm(s["h"], duc, "tn", BF16, "proj_pool_dw"),
                                         _mm(s["h"], dglog, "tn", BF16, "proj_gate_dw")], axis=1)
        dcur, dg = _rmsnorm_bwd(s["x"], norm_mix[l], dh, dx1, "norm_mix_bwd")
        gw["norm_mix"][l] = dg[0]

    landed = _exchange_finish(grads_state, True, dcur, "exchange_early_finish")
    parts = {key: _own_slab(g, src, True) for key, g, src in zip(early, landed, early_pieces)}
    parts["w_in", 0] = _exchange([pieces_of("w_in", gw["w_in"][0])], True, "exchange_last")[0]
    small = _pack_rows([jnp.stack(gw[n]) for n in REPLICATED], SMALL_COLS, 16)
    small_parts = _exchange([small], False, "gather_small_grads")[0]

    out = {}
    for n in exchanged:
        res = [_adamw(parts[n, l], w[n][l], m[n][l], v[n][l], "adamw_" + n) for l in range(L)]
        out[n] = tuple(jnp.stack(r) for r in zip(*res))
    rep_like = [w[n] for n in REPLICATED]
    res = _adamw(small_parts, *[_pack_rows([d[n] for n in REPLICATED], SMALL_COLS, 16) for d in (w, m, v)],
                 "adamw_replicated")
    out.update({n: r for n, r in zip(REPLICATED, zip(*[_unpack_rows(r, rep_like) for r in res]))})

    grads, deltas, new_m, new_v = ([out[n][i] for n in WEIGHTS] for i in range(4))
    return (loss, dcur.reshape(x.shape), *grads, *deltas, *new_m, *new_v)
```

```python
import functools
import math

import jax
import jax.numpy as jnp
from jax import lax
from jax.experimental import pallas as pl
from jax.experimental.pallas import tpu as pltpu

F32 = jnp.float32
BF16 = jnp.bfloat16

N_DEV = 8
D_MODEL = 1024
N_HEADS = 8
HEAD_DIM = 64
CHUNK = 64
N_LEFT = 8
BAND = (N_LEFT + 1) * CHUNK
WIDTH = N_HEADS * HEAD_DIM
POOL_WINDOWS = (2, 4, 8, 16)
POOL_DIM = 128
MAX_REL = 2 * CHUNK
REL_TABLE = MAX_REL + CHUNK
D_FF = 2816
EPS = 1e-6
SB_SCAN = 256
SB_ROWS = 512
SB_KEYS = 512
A_BLOCK = N_LEFT * CHUNK
HALO = 16
LANES = 128
VMEM_LIMIT = 56 * 1024 * 1024

ADAM_LR = 0.001
ADAM_B1 = 0.9
ADAM_B2 = 0.999
ADAM_EPS = 1e-08
ADAM_WD = 0.01
ADAM_STEP = 10

MESH = pl.DeviceIdType.MESH


def _params(sem):
    return pltpu.CompilerParams(dimension_semantics=sem, vmem_limit_bytes=VMEM_LIMIT)


def _pick(n, cap):
    if n <= cap:
        return n
    best = None
    for t in range(LANES, cap + 1, LANES):
        if n % t == 0:
            best = t
    assert best is not None, (n, cap)
    return best


MM_TILE_CAP = 1408


def _mm(a, b, mode, out_dtype, name, tm=MM_TILE_CAP, tn_cap=MM_TILE_CAP, tk_cap=MM_TILE_CAP, res=None):
    if mode == "nn":
        (M, K), (K2, N) = a.shape, b.shape
    elif mode == "nt":
        (M, K), (N, K2) = a.shape, b.shape
    else:
        (K, M), (K2, N) = a.shape, b.shape
    assert K == K2, (a.shape, b.shape, mode)
    tm = _pick(M, tm)
    tn = _pick(N, tn_cap)
    tk = _pick(K, tk_cap)
    nk = K // tk
    if mode == "nn":
        dims = (((1,), (0,)), ((), ()))
        a_spec = pl.BlockSpec((tm, tk), lambda i, j, k: (i, k))
        b_spec = pl.BlockSpec((tk, tn), lambda i, j, k: (k, j))
    elif mode == "nt":
        dims = (((1,), (1,)), ((), ()))
        a_spec = pl.BlockSpec((tm, tk), lambda i, j, k: (i, k))
        b_spec = pl.BlockSpec((tn, tk), lambda i, j, k: (j, k))
    else:
        dims = (((0,), (0,)), ((), ()))
        a_spec = pl.BlockSpec((tk, tm), lambda i, j, k: (k, i))
        b_spec = pl.BlockSpec((tk, tn), lambda i, j, k: (k, j))

    o_spec = pl.BlockSpec((tm, tn), lambda i, j, k: (i, j))

    def body(a_ref, b_ref, *rest):
        res_ref = rest[0] if res is not None else None
        o_ref = rest[1] if res is not None else rest[0]
        part = lax.dot_general(a_ref[...].astype(BF16), b_ref[...].astype(BF16), dims, preferred_element_type=F32)
        if nk == 1:
            o_ref[...] = (part if res is None else part + res_ref[...]).astype(out_dtype)
            return
        acc_ref = rest[-1]
        k = pl.program_id(2)

        @pl.when(k == 0)
        def _():
            acc_ref[...] = part

        @pl.when(k > 0)
        def _():
            acc_ref[...] += part

        @pl.when(k == nk - 1)
        def _():
            total = acc_ref[...] if res is None else acc_ref[...] + res_ref[...]
            o_ref[...] = total.astype(out_dtype)

    return pl.pallas_call(
        body, name=name,
        grid=(M // tm, N // tn, nk),
        in_specs=[a_spec, b_spec] + ([o_spec] if res is not None else []),
        out_specs=o_spec,
        out_shape=jax.ShapeDtypeStruct((M, N), out_dtype),
        scratch_shapes=[pltpu.VMEM((tm, tn), F32)] if nk > 1 else [],
        compiler_params=_params(("parallel", "parallel", "arbitrary")),
    )(*((a, b) if res is None else (a, b, res)))


def _rmsnorm_fwd(x, gain, name, tm=512):
    T, C = x.shape

    def body(x_ref, g_ref, h_ref):
        xv = x_ref[...]
        r = lax.rsqrt(jnp.mean(xv * xv, axis=-1, keepdims=True) + EPS)
        h_ref[...] = (xv * r * g_ref[...]).astype(BF16)

    return pl.pallas_call(
        body, name=name, grid=(T // tm,),
        in_specs=[pl.BlockSpec((tm, C), lambda i: (i, 0)), pl.BlockSpec((1, C), lambda i: (0, 0))],
        out_specs=pl.BlockSpec((tm, C), lambda i: (i, 0)),
        out_shape=jax.ShapeDtypeStruct((T, C), BF16),
        compiler_params=_params(("parallel",)),
    )(x, gain.reshape(1, C))


def _rmsnorm_bwd(x, gain, dh, dres, name, tm=512):
    T, C = x.shape

    def body(x_ref, g_ref, dh_ref, dres_ref, dx_ref, dg_ref):
        @pl.when(pl.program_id(0) == 0)
        def _():
            dg_ref[...] = jnp.zeros_like(dg_ref)

        xv = x_ref[...]
        dy = dh_ref[...].astype(F32)
        r = lax.rsqrt(jnp.mean(xv * xv, axis=-1, keepdims=True) + EPS)
        gdy = dy * g_ref[...]
        inner = jnp.mean(xv * gdy, axis=-1, keepdims=True)
        dx_ref[...] = dres_ref[...] + r * gdy - xv * (r * r * r * inner)
        dg_ref[...] += jnp.sum(dy * xv * r, axis=0, keepdims=True)

    return pl.pallas_call(
        body, name=name, grid=(T // tm,),
        in_specs=[pl.BlockSpec((tm, C), lambda i: (i, 0)), pl.BlockSpec((1, C), lambda i: (0, 0)),
                  pl.BlockSpec((tm, C), lambda i: (i, 0)), pl.BlockSpec((tm, C), lambda i: (i, 0))],
        out_specs=[pl.BlockSpec((tm, C), lambda i: (i, 0)), pl.BlockSpec((1, C), lambda i: (0, 0))],
        out_shape=[jax.ShapeDtypeStruct((T, C), F32), jax.ShapeDtypeStruct((1, C), F32)],
        compiler_params=_params(("arbitrary",)),
    )(x, gain.reshape(1, C), dh, dres)


MASKED = -1e30


def _pair_sum(x, same_head):
    hi = x.astype(BF16)
    lo = (x - hi.astype(F32)).astype(BF16)
    return jnp.dot(hi, same_head, preferred_element_type=F32) + jnp.dot(lo, same_head, preferred_element_type=F32)


def _same_head():
    r = lax.broadcasted_iota(jnp.int32, (2 * HEAD_DIM, 2 * HEAD_DIM), 0)
    c = lax.broadcasted_iota(jnp.int32, (2 * HEAD_DIM, 2 * HEAD_DIM), 1)
    return jnp.where((r < HEAD_DIM) == (c < HEAD_DIM), 1.0, 0.0).astype(BF16)


def _pair_norm(t, g, same_head):
    tf = t.astype(F32)
    r = lax.rsqrt(_pair_sum(tf * tf, same_head) * (1.0 / HEAD_DIM) + EPS)
    return tf * r * g


def _pair_norm_bwd(t, g, dn, same_head):
    tf = t.astype(F32)
    r = lax.rsqrt(_pair_sum(tf * tf, same_head) * (1.0 / HEAD_DIM) + EPS)
    gd = dn * g
    inner = _pair_sum(tf * gd, same_head) * (1.0 / HEAD_DIM)
    return r * gd - tf * (r * r * r * inner), jnp.sum(dn * tf * r, axis=0, keepdims=True)


def _band_table(bias):
    rows = [jnp.pad(bias, ((0, 0), (0, 0), (c * CHUNK, 2 * A_BLOCK - BAND - c * CHUNK)), constant_values=MASKED)
            for c in range(N_LEFT)]
    return jnp.concatenate(rows, axis=1).reshape(N_HEADS // 2, 2 * A_BLOCK, 2 * A_BLOCK)


def _band_table_bwd(dtable):
    dtable = dtable.reshape(N_HEADS, A_BLOCK, 2 * A_BLOCK)
    return sum(dtable[:, c * CHUNK:(c + 1) * CHUNK, c * CHUNK:c * CHUNK + BAND] for c in range(N_LEFT))


def _a_specs(T):
    nb = T // A_BLOCK
    pairs = N_HEADS // 2
    col = lambda which: which * pairs
    cur = lambda which: pl.BlockSpec((A_BLOCK, 2 * HEAD_DIM), lambda p, i: (i, col(which) + p))
    prev = lambda which: pl.BlockSpec((A_BLOCK, 2 * HEAD_DIM), lambda p, i: (jnp.maximum(i - 1, 0), col(which) + p))
    nxt = lambda which: pl.BlockSpec((A_BLOCK, 2 * HEAD_DIM), lambda p, i: (jnp.minimum(i + 1, nb - 1), col(which) + p))
    table = pl.BlockSpec((1, 2 * A_BLOCK, 2 * A_BLOCK), lambda p, i: (p, 0, 0))
    gain = pl.BlockSpec((1, 2 * HEAD_DIM), lambda p, i: (0, 0))
    gacc = pl.BlockSpec((1, 1, 2 * HEAD_DIM), lambda p, i: (p, 0, 0))
    return nb, pairs, cur, prev, nxt, table, gain, gacc


A_ROWS = 256


def _a_operands(q_ref, kc_ref, kp_ref, gq_ref, gk_ref, same_head):
    scale = 1.0 / math.sqrt(HEAD_DIM)
    qst = _stack_heads((_pair_norm(q_ref[...], gq_ref[...], same_head) * scale).astype(BF16))
    kcat = jnp.concatenate([_pair_norm(kp_ref[...], gk_ref[...], same_head).astype(BF16),
                            _pair_norm(kc_ref[...], gk_ref[...], same_head).astype(BF16)], axis=0)
    return qst, kcat


def _a_probs(qst, kcat, t_ref, first, r0):
    rows = slice(r0, r0 + A_ROWS)
    s = lax.dot_general(qst[rows], kcat, (((1,), (1,)), ((), ())), preferred_element_type=F32) + t_ref[0, rows, :]
    col = lax.broadcasted_iota(jnp.int32, s.shape, 1)
    s = jnp.where(col >= jnp.where(first, A_BLOCK, 0), s, MASKED)
    e = jnp.exp(s - jnp.max(s, axis=-1, keepdims=True))
    return e / jnp.sum(e, axis=-1, keepdims=True)


def _attn_a_fwd(qkv, table, gq, gk):
    T = qkv.shape[0]
    nb, pairs, cur, prev, _, tspec, gspec, _ = _a_specs(T)

    def body(q_ref, kc_ref, kp_ref, vc_ref, vp_ref, t_ref, gq_ref, gk_ref, o_ref):
        first = pl.program_id(1) == 0
        qst, kcat = _a_operands(q_ref, kc_ref, kp_ref, gq_ref, gk_ref, _same_head())
        vcat = jnp.concatenate([vp_ref[...], vc_ref[...]], axis=0)
        outs = [jnp.dot(_a_probs(qst, kcat, t_ref, first, r0).astype(BF16), vcat, preferred_element_type=F32)
                for r0 in range(0, 2 * A_BLOCK, A_ROWS)]
        o_ref[...] = _unstack_heads(jnp.concatenate(outs, axis=0)).astype(BF16)

    return pl.pallas_call(
        body, name="attn_a_fwd", grid=(pairs, nb),
        in_specs=[cur(0), cur(1), prev(1), cur(2), prev(2), tspec, gspec, gspec],
        out_specs=pl.BlockSpec((A_BLOCK, 2 * HEAD_DIM), lambda p, i: (i, p)),
        out_shape=jax.ShapeDtypeStruct((T, WIDTH), BF16),
        compiler_params=_params(("parallel", "arbitrary")),
    )(qkv, qkv, qkv, qkv, qkv, table, jnp.tile(gq.reshape(1, HEAD_DIM), (1, 2)), jnp.tile(gk.reshape(1, HEAD_DIM), (1, 2)))


def _attn_a_bwd(qkv, do, table, gq, gk):
    T = qkv.shape[0]
    nb, pairs, cur, prev, _, tspec, gspec, gacc = _a_specs(T)
    scale = 1.0 / math.sqrt(HEAD_DIM)
    oblk = pl.BlockSpec((A_BLOCK, 2 * HEAD_DIM), lambda p, i: (i, p))

    def body(q_ref, kc_ref, kp_ref, vc_ref, vp_ref, do_ref, t_ref, gq_ref, gk_ref,
             dq_ref, dkc_ref, dkp_ref, dvc_ref, dvp_ref, dt_ref, dgq_ref):
        first = pl.program_id(1) == 0

        @pl.when(first)
        def _():
            dt_ref[...] = jnp.zeros_like(dt_ref)
            dgq_ref[...] = jnp.zeros_like(dgq_ref)

        same_head = _same_head()
        qst, kcat = _a_operands(q_ref, kc_ref, kp_ref, gq_ref, gk_ref, same_head)
        vcat = jnp.concatenate([vp_ref[...], vc_ref[...]], axis=0)
        dost = _stack_heads(do_ref[...])
        nt = (((1,), (1,)), ((), ()))
        tn = (((0,), (0,)), ((), ()))
        dqn_parts = []
        dk = dv = None
        for r0 in range(0, 2 * A_BLOCK, A_ROWS):
            rows = slice(r0, r0 + A_ROWS)
            p = _a_probs(qst, kcat, t_ref, first, r0)
            dp = lax.dot_general(dost[rows], vcat, nt, preferred_element_type=F32)
            ds = p * (dp - jnp.sum(p * dp, axis=-1, keepdims=True))
            dt_ref[0, rows, :] += ds
            dsb = ds.astype(BF16)
            dqn_parts.append(jnp.dot(dsb, kcat, preferred_element_type=F32))
            dk_part = lax.dot_general(dsb, qst[rows], tn, preferred_element_type=F32)
            dv_part = lax.dot_general(p.astype(BF16), dost[rows], tn, preferred_element_type=F32)
            dk = dk_part if dk is None else dk + dk_part
            dv = dv_part if dv is None else dv + dv_part
        dqn = _unstack_heads(jnp.concatenate(dqn_parts, axis=0)) * scale
        dq, dg = _pair_norm_bwd(q_ref[...], gq_ref[...], dqn, same_head)
        dq_ref[...] = dq.astype(BF16)
        dgq_ref[0] += dg
        dkp_ref[...] = dk[:A_BLOCK]
        dkc_ref[...] = dk[A_BLOCK:]
        dvp_ref[...] = dv[:A_BLOCK]
        dvc_ref[...] = dv[A_BLOCK:]

    wide = jax.ShapeDtypeStruct((T, WIDTH), F32)
    return pl.pallas_call(
        body, name="attn_a_bwd", grid=(pairs, nb),
        in_specs=[cur(0), cur(1), prev(1), cur(2), prev(2), oblk, tspec, gspec, gspec],
        out_specs=[oblk, oblk, oblk, oblk, oblk, tspec, gacc],
        out_shape=[jax.ShapeDtypeStruct((T, WIDTH), BF16), wide, wide, wide, wide,
                   jax.ShapeDtypeStruct((pairs, 2 * A_BLOCK, 2 * A_BLOCK), F32),
                   jax.ShapeDtypeStruct((pairs, 1, 2 * HEAD_DIM), F32)],
        compiler_params=_params(("parallel", "arbitrary")),
    )(qkv, qkv, qkv, qkv, qkv, do, table, jnp.tile(gq.reshape(1, HEAD_DIM), (1, 2)),
      jnp.tile(gk.reshape(1, HEAD_DIM), (1, 2)))


def _attn_a_bwd_keys(qkv, dkc, dkp, dvc, dvp, gk):
    T = qkv.shape[0]
    nb, pairs, cur, _, _, _, gspec, gacc = _a_specs(T)
    oblk = pl.BlockSpec((A_BLOCK, 2 * HEAD_DIM), lambda p, i: (i, p))
    onext = pl.BlockSpec((A_BLOCK, 2 * HEAD_DIM), lambda p, i: (jnp.minimum(i + 1, nb - 1), p))

    def body(k_ref, dkc_ref, dkp_ref, dvc_ref, dvp_ref, gk_ref, dk_ref, dv_ref, dgk_ref):
        i = pl.program_id(1)

        @pl.when(i == 0)
        def _():
            dgk_ref[...] = jnp.zeros_like(dgk_ref)

        has_next = (i < nb - 1).astype(F32)
        dkn = dkc_ref[...] + has_next * dkp_ref[...]
        dk, dg = _pair_norm_bwd(k_ref[...], gk_ref[...], dkn, _same_head())
        dk_ref[...] = dk.astype(BF16)
        dv_ref[...] = (dvc_ref[...] + has_next * dvp_ref[...]).astype(BF16)
        dgk_ref[0] += dg

    blk = jax.ShapeDtypeStruct((T, WIDTH), BF16)
    return pl.pallas_call(
        body, name="attn_a_bwd_keys", grid=(pairs, nb),
        in_specs=[cur(1), oblk, onext, oblk, onext, gspec],
        out_specs=[oblk, oblk, gacc],
        out_shape=[blk, blk, jax.ShapeDtypeStruct((pairs, 1, 2 * HEAD_DIM), F32)],
        compiler_params=_params(("parallel", "arbitrary")),
    )(qkv, dkc, dkp, dvc, dvp, jnp.tile(gk.reshape(1, HEAD_DIM), (1, 2)))


def _scan_matrix(later):
    r = lax.broadcasted_iota(jnp.int32, (SB_SCAN, SB_SCAN), 0)
    c = lax.broadcasted_iota(jnp.int32, (SB_SCAN, SB_SCAN), 1)
    return jnp.where((r > c) if later else (r < c), 1.0, 0.0).astype(BF16)


def _running_sums(x, carry, scan, later):
    n = SB_KEYS // SB_SCAN
    parts = [None] * n
    total = carry
    for sb in (reversed(range(n)) if later else range(n)):
        xs = x[:, sb * SB_SCAN:(sb + 1) * SB_SCAN]
        local = jnp.dot(xs.astype(BF16), scan, preferred_element_type=F32)
        parts[sb] = local if total is None else local + total
        rowsum = jnp.sum(xs, axis=-1, keepdims=True)
        total = rowsum if total is None else total + rowsum
    return (parts[0] if n == 1 else jnp.concatenate(parts, axis=1)), total


def _sb_log_sigmoids(z):
    neg_abs = pltpu.bitcast(pltpu.bitcast(z, jnp.uint32) | jnp.uint32(0x80000000), F32)
    take = jnp.minimum(z, 0.0) - jnp.log(1.0 + jnp.exp(neg_abs))
    return take, take - z


def _sb_mask():
    r = lax.broadcasted_iota(jnp.int32, (2 * SB_ROWS, SB_KEYS), 0)
    c = lax.broadcasted_iota(jnp.int32, (2 * SB_ROWS, SB_KEYS), 1)
    return c < jnp.where(r >= SB_ROWS, r - SB_ROWS, r)


def _stack_heads(t):
    lane = lax.broadcasted_iota(jnp.int32, t.shape, 1)
    zero = jnp.zeros_like(t)
    return jnp.concatenate([jnp.where(lane < HEAD_DIM, t, zero), jnp.where(lane >= HEAD_DIM, t, zero)], axis=0)


def _unstack_heads(t):
    rows = t.shape[0] // 2
    lane = lax.broadcasted_iota(jnp.int32, (rows, 2 * HEAD_DIM), 1)
    return jnp.where(lane < HEAD_DIM, t[:rows], t[rows:])


def _sb_specs(T):
    nq = T // SB_ROWS
    blk = lambda col: pl.BlockSpec((SB_ROWS, 2 * HEAD_DIM), lambda p, i: (i, col + p))
    full = lambda col: pl.BlockSpec((T, 2 * HEAD_DIM), lambda p, i: (0, col + p))
    return nq, blk, full


def _key_rows(j):
    return pl.ds(pl.multiple_of(j * SB_KEYS, SB_KEYS), SB_KEYS)


def _attn_b_fwd(qkv):
    T = qkv.shape[0]
    pairs = N_HEADS // 2
    nq, blk, full = _sb_specs(T)
    scale = 1.0 / math.sqrt(HEAD_DIM)

    assert nq <= LANES

    def body(q_ref, k_ref, v_ref, o_ref, c_ref, acc_ref, carry_ref, z_ref, w_ref):
        i = pl.program_id(1)
        scan = _scan_matrix(True)
        qst = _stack_heads((q_ref[...].astype(F32) * scale).astype(BF16))
        lane = lax.broadcasted_iota(jnp.int32, (2 * SB_ROWS, LANES), 1)

        def scores(j):
            return lax.dot_general(qst, k_ref[_key_rows(j), :], (((1,), (1,)), ((), ())), preferred_element_type=F32)

        def weights(z, carry, mask):
            take, keep = _sb_log_sigmoids(z)
            if mask is not None:
                keep = jnp.where(mask, keep, 0.0)
            tail, total = _running_sums(keep, carry, scan, True)
            w = jnp.exp(take + tail)
            if mask is not None:
                w = jnp.where(mask, w, 0.0)
            return w.astype(BF16), total

        w_ref[...], carry_ref[...] = weights(scores(i), None, _sb_mask())
        z_ref[...] = scores(jnp.maximum(i - 1, 0))
        acc_ref[...] = jnp.zeros_like(acc_ref)
        c_ref[0, 0] = jnp.zeros((2 * SB_ROWS, LANES), F32)

        @pl.loop(0, i)
        def _(jj):
            j = i - 1 - jj
            z = z_ref[...]
            z_ref[...] = scores(jnp.maximum(j - 1, 0))
            acc_ref[...] += jnp.dot(w_ref[...], v_ref[_key_rows(j + 1), :], preferred_element_type=F32)
            carry = carry_ref[...]
            c_ref[0, 0] = jnp.where(lane == j, carry, c_ref[0, 0])
            w_ref[...], carry_ref[...] = weights(z, carry, None)

        acc = acc_ref[...] + jnp.dot(w_ref[...], v_ref[_key_rows(0), :], preferred_element_type=F32)
        o_ref[...] = _unstack_heads(acc).astype(BF16)

    return pl.pallas_call(
        body, name="attn_b_fwd", grid=(pairs, nq),
        in_specs=[blk(3 * pairs), full(4 * pairs), full(5 * pairs)],
        out_specs=[pl.BlockSpec((SB_ROWS, 2 * HEAD_DIM), lambda p, i: (i, p)),
                   pl.BlockSpec((1, 1, 2 * SB_ROWS, LANES), lambda p, i: (p, i, 0, 0))],
        out_shape=[jax.ShapeDtypeStruct((T, WIDTH), BF16), jax.ShapeDtypeStruct((pairs, nq, 2 * SB_ROWS, LANES), F32)],
        scratch_shapes=[pltpu.VMEM((2 * SB_ROWS, 2 * HEAD_DIM), F32), pltpu.VMEM((2 * SB_ROWS, 1), F32),
                        pltpu.VMEM((2 * SB_ROWS, SB_KEYS), F32), pltpu.VMEM((2 * SB_ROWS, SB_KEYS), BF16)],
        compiler_params=_params(("parallel", "arbitrary")),
    )(qkv, qkv, qkv)


def _attn_b_bwd(qkv, carries, do):
    T = qkv.shape[0]
    pairs = N_HEADS // 2
    nq, blk, full = _sb_specs(T)
    scale = 1.0 / math.sqrt(HEAD_DIM)
    oblk = pl.BlockSpec((SB_ROWS, 2 * HEAD_DIM), lambda p, i: (i, p))
    ofull = pl.BlockSpec((2 * HEAD_DIM, T), lambda p, i: (p, 0))

    def body(q_ref, k_ref, v_ref, c_ref, do_ref, dq_ref, dk_ref, dv_ref, dqacc_ref, before_ref,
             z_ref, dw_ref, dz_ref, w_ref):
        i = pl.program_id(1)

        @pl.when(i == 0)
        def _():
            dk_ref[...] = jnp.zeros_like(dk_ref)
            dv_ref[...] = jnp.zeros_like(dv_ref)

        scan_later = _scan_matrix(True)
        scan_earlier = _scan_matrix(False)
        qst = _stack_heads((q_ref[...].astype(F32) * scale).astype(BF16))
        dost = _stack_heads(do_ref[...].astype(BF16))
        lane = lax.broadcasted_iota(jnp.int32, (2 * SB_ROWS, LANES), 1)
        nt = (((1,), (1,)), ((), ()))

        def products(j):
            return (lax.dot_general(qst, k_ref[_key_rows(j), :], nt, preferred_element_type=F32),
                    lax.dot_general(dost, v_ref[_key_rows(j), :], nt, preferred_element_type=F32))

        def score_grads(z, dw, later, mask):
            take, keep = _sb_log_sigmoids(z)
            sig = jnp.exp(take)
            if mask is not None:
                keep = jnp.where(mask, keep, 0.0)
            tail, _ = _running_sums(keep, later, scan_later, True)
            w = jnp.exp(take + tail)
            if mask is not None:
                w = jnp.where(mask, w, 0.0)
            g = w * dw
            before, before_ref[...] = _running_sums(g, before_ref[...], scan_earlier, False)
            dz = g - sig * (g + before)
            if mask is not None:
                dz = jnp.where(mask, dz, 0.0)
            return dz.astype(BF16), w.astype(BF16)

        qst_t = qst.T
        dost_t = dost.T

        def accumulate(j, dzb, wb):
            cols = pl.ds(pl.multiple_of(j * SB_KEYS, SB_KEYS), SB_KEYS)
            dqacc_ref[...] += jnp.dot(dzb, k_ref[_key_rows(j), :], preferred_element_type=F32)
            dk_ref[:, cols] += jnp.dot(qst_t, dzb, preferred_element_type=F32)
            dv_ref[:, cols] += jnp.dot(dost_t, wb, preferred_element_type=F32)

        dqacc_ref[...] = jnp.zeros_like(dqacc_ref)
        before_ref[...] = jnp.zeros_like(before_ref)
        dz_ref[...] = jnp.zeros_like(dz_ref)
        w_ref[...] = jnp.zeros_like(w_ref)
        z_ref[...], dw_ref[...] = products(0)

        @pl.loop(0, i)
        def _(j):
            z, dw = z_ref[...], dw_ref[...]
            accumulate(jnp.maximum(j - 1, 0), dz_ref[...], w_ref[...])
            later = jnp.sum(jnp.where(lane == j, c_ref[0, 0], 0.0), axis=-1, keepdims=True)
            dz_ref[...], w_ref[...] = score_grads(z, dw, later, None)
            z_ref[...], dw_ref[...] = products(j + 1)

        accumulate(jnp.maximum(i - 1, 0), dz_ref[...], w_ref[...])
        accumulate(i, *score_grads(z_ref[...], dw_ref[...], None, _sb_mask()))
        dq_ref[...] = (_unstack_heads(dqacc_ref[...]) * scale).astype(BF16)

    wide = jax.ShapeDtypeStruct((WIDTH, T), F32)
    return pl.pallas_call(
        body, name="attn_b_bwd", grid=(pairs, nq),
        in_specs=[blk(3 * pairs), full(4 * pairs), full(5 * pairs),
                  pl.BlockSpec((1, 1, 2 * SB_ROWS, LANES), lambda p, i: (p, i, 0, 0)), oblk],
        out_specs=[oblk, ofull, ofull],
        out_shape=[jax.ShapeDtypeStruct((T, WIDTH), BF16), wide, wide],
        scratch_shapes=[pltpu.VMEM((2 * SB_ROWS, 2 * HEAD_DIM), F32), pltpu.VMEM((2 * SB_ROWS, 1), F32),
                        pltpu.VMEM((2 * SB_ROWS, SB_KEYS), F32), pltpu.VMEM((2 * SB_ROWS, SB_KEYS), F32),
                        pltpu.VMEM((2 * SB_ROWS, SB_KEYS), BF16), pltpu.VMEM((2 * SB_ROWS, SB_KEYS), BF16)],
        compiler_params=_params(("parallel", "arbitrary")),
    )(qkv, qkv, qkv, carries, do)


def _window_sums(ext, forward):
    n = ext.shape[0]
    out = []
    s = ext
    for step in (1, 2, 4, 8):
        s = s + pltpu.roll(s, (n - step) if forward else step, 0)
        out.append(s)
    return out


def _pool_counts(base, rows, win):
    t = base + lax.broadcasted_iota(jnp.int32, (rows, 1), 0)
    return jnp.minimum(t + 1, win).astype(F32)


def _pooled(u_ref, up_ref, i, tm):
    prev = jnp.where(i > 0, up_ref[...], 0.0)
    ext = jnp.concatenate([prev, u_ref[...]], axis=0)
    sums = _window_sums(ext, False)
    parts = []
    for g, win in enumerate(POOL_WINDOWS):
        cols = slice(g * POOL_DIM, (g + 1) * POOL_DIM)
        cnt = _pool_counts(i * tm, tm, win)
        parts.append(sums[g][HALO:, cols] / cnt - ext[HALO:, cols])
    return parts


def _pool_fwd(ucg, w_pool, scale, tm=512):
    T = ucg.shape[0]
    C = WIDTH

    def body(u_ref, up_ref, w_ref, s_ref, o_ref):
        i = pl.program_id(0)
        parts = _pooled(u_ref, up_ref, i, tm)
        for g in range(len(POOL_WINDOWS)):
            mixed = jnp.dot(parts[g].astype(BF16), w_ref[g], preferred_element_type=F32)
            o_ref[:, g * POOL_DIM:(g + 1) * POOL_DIM] = (mixed * s_ref[:, g * POOL_DIM:(g + 1) * POOL_DIM]).astype(BF16)

    return pl.pallas_call(
        body, name="pool_fwd", grid=(T // tm,),
        in_specs=[pl.BlockSpec((tm, C), lambda i: (i, 0)),
                  pl.BlockSpec((HALO, C), lambda i: (jnp.maximum(i * (tm // HALO) - 1, 0), 0)),
                  pl.BlockSpec((len(POOL_WINDOWS), POOL_DIM, POOL_DIM), lambda i: (0, 0, 0)),
                  pl.BlockSpec((1, C), lambda i: (0, 0))],
        out_specs=pl.BlockSpec((tm, C), lambda i: (i, 0)),
        out_shape=jax.ShapeDtypeStruct((T, C), BF16),
        compiler_params=_params(("parallel",)),
    )(ucg, ucg, w_pool.astype(BF16), scale.reshape(1, C))


def _pool_bwd(ucg, do_c, w_pool, scale, tm=512):
    T = ucg.shape[0]
    C = WIDTH
    nt = T // tm
    G = len(POOL_WINDOWS)

    def body(u_ref, up_ref, do_ref, don_ref, w_ref, s_ref, du_ref, dw_ref, ds_ref):
        i = pl.program_id(0)

        @pl.when(i == 0)
        def _():
            dw_ref[...] = jnp.zeros_like(dw_ref)
            ds_ref[...] = jnp.zeros_like(ds_ref)

        parts = _pooled(u_ref, up_ref, i, tm)
        nxt = jnp.where(i < nt - 1, don_ref[...].astype(F32), 0.0)
        do_ext = jnp.concatenate([do_ref[...].astype(F32), nxt], axis=0) * s_ref[...]
        for g, win in enumerate(POOL_WINDOWS):
            cols = slice(g * POOL_DIM, (g + 1) * POOL_DIM)
            pooled_b = parts[g].astype(BF16)
            dmix = do_ext[:, cols].astype(BF16)
            mixed = jnp.dot(pooled_b, w_ref[g], preferred_element_type=F32)
            ds_ref[:, cols] += jnp.sum(do_ref[:, cols].astype(F32) * mixed, axis=0, keepdims=True)
            dw_ref[g] += lax.dot_general(pooled_b, dmix[:tm], (((0,), (0,)), ((), ())), preferred_element_type=F32)
            dpool = lax.dot_general(dmix, w_ref[g], (((1,), (1,)), ((), ())), preferred_element_type=F32)
            scaled = dpool / _pool_counts(i * tm, tm + HALO, win)
            fwd = _window_sums(scaled, True)[g]
            du_ref[:, cols] = (fwd[:tm] - dpool[:tm]).astype(BF16)

    return pl.pallas_call(
        body, name="pool_bwd", grid=(nt,),
        in_specs=[pl.BlockSpec((tm, C), lambda i: (i, 0)),
                  pl.BlockSpec((HALO, C), lambda i: (jnp.maximum(i * (tm // HALO) - 1, 0), 0)),
                  pl.BlockSpec((tm, C), lambda i: (i, 0)),
                  pl.BlockSpec((HALO, C), lambda i: (jnp.minimum((i + 1) * (tm // HALO), T // HALO - 1), 0)),
                  pl.BlockSpec((G, POOL_DIM, POOL_DIM), lambda i: (0, 0, 0)),
                  pl.BlockSpec((1, C), lambda i: (0, 0))],
        out_specs=[pl.BlockSpec((tm, C), lambda i: (i, 0)),
                   pl.BlockSpec((G, POOL_DIM, POOL_DIM), lambda i: (0, 0, 0)),
                   pl.BlockSpec((1, C), lambda i: (0, 0))],
        out_shape=[jax.ShapeDtypeStruct((T, C), BF16), jax.ShapeDtypeStruct((G, POOL_DIM, POOL_DIM), F32),
                   jax.ShapeDtypeStruct((1, C), F32)],
        compiler_params=_params(("arbitrary",)),
    )(ucg, ucg, do_c, do_c, w_pool.astype(BF16), scale.reshape(1, C))


def _merge_fwd(oa, ob, oc, glog, b_gate, wa, wb, wc, tm=256):
    T = oa.shape[0]
    Dm = D_MODEL
    row = lambda c: pl.BlockSpec((tm, c), lambda i: (i, 0))
    wspec = pl.BlockSpec((WIDTH, Dm), lambda i: (0, 0))

    def body(oa_ref, ob_ref, oc_ref, g_ref, b_ref, wa_ref, wb_ref, wc_ref, m_ref, ya_ref, yb_ref, yc_ref):
        merged = jnp.zeros((tm, Dm), F32)
        for kk, (o_ref, w_ref, y_ref) in enumerate(((oa_ref, wa_ref, ya_ref), (ob_ref, wb_ref, yb_ref),
                                                    (oc_ref, wc_ref, yc_ref))):
            y = jnp.dot(o_ref[...].astype(BF16), w_ref[...], preferred_element_type=F32)
            gate = jax.nn.sigmoid(g_ref[:, kk * Dm:(kk + 1) * Dm] + b_ref[:, kk * Dm:(kk + 1) * Dm])
            merged = merged + gate * y
            y_ref[...] = y.astype(BF16)
        m_ref[...] = merged.astype(BF16)

    out = jax.ShapeDtypeStruct((T, Dm), BF16)
    return pl.pallas_call(
        body, name="merge_fwd", grid=(T // tm,),
        in_specs=[row(WIDTH), row(WIDTH), row(WIDTH), row(3 * Dm), pl.BlockSpec((1, 3 * Dm), lambda i: (0, 0)),
                  wspec, wspec, wspec],
        out_specs=[row(Dm)] * 4,
        out_shape=[out] * 4,
        compiler_params=_params(("parallel",)),
    )(oa, ob, oc, glog, b_gate.reshape(1, 3 * Dm), wa, wb, wc)


def _merge_bwd(dmerged, glog, b_gate, ya, yb, yc, tm=256):
    T = dmerged.shape[0]
    Dm = D_MODEL
    row = lambda c: pl.BlockSpec((tm, c), lambda i: (i, 0))

    def body(dm_ref, g_ref, b_ref, ya_ref, yb_ref, yc_ref, dya_ref, dyb_ref, dyc_ref, dg_ref, db_ref):
        @pl.when(pl.program_id(0) == 0)
        def _():
            db_ref[...] = jnp.zeros_like(db_ref)

        dm = dm_ref[...]
        for kk, (y_ref, dy_ref) in enumerate(((ya_ref, dya_ref), (yb_ref, dyb_ref), (yc_ref, dyc_ref))):
            cols = slice(kk * Dm, (kk + 1) * Dm)
            gate = jax.nn.sigmoid(g_ref[:, cols] + b_ref[:, cols])
            dy_ref[...] = (dm * gate).astype(BF16)
            dlog = dm * y_ref[...].astype(F32) * gate * (1.0 - gate)
            dg_ref[:, cols] = dlog.astype(BF16)
            db_ref[:, cols] += jnp.sum(dlog, axis=0, keepdims=True)

    out = jax.ShapeDtypeStruct((T, Dm), BF16)
    return pl.pallas_call(
        body, name="merge_bwd", grid=(T // tm,),
        in_specs=[row(Dm), row(3 * Dm), pl.BlockSpec((1, 3 * Dm), lambda i: (0, 0)), row(Dm), row(Dm), row(Dm)],
        out_specs=[row(Dm), row(Dm), row(Dm), row(3 * Dm), pl.BlockSpec((1, 3 * Dm), lambda i: (0, 0))],
        out_shape=[out, out, out, jax.ShapeDtypeStruct((T, 3 * Dm), BF16), jax.ShapeDtypeStruct((1, 3 * Dm), F32)],
        compiler_params=_params(("arbitrary",)),
    )(dmerged, glog, b_gate.reshape(1, 3 * Dm), ya, yb, yc)


def _residual_add(x, y, name, tm=512):
    T, C = x.shape

    def body(x_ref, y_ref, o_ref):
        o_ref[...] = x_ref[...] + y_ref[...]

    spec = pl.BlockSpec((tm, C), lambda i: (i, 0))
    return pl.pallas_call(body, name=name, grid=(T // tm,), in_specs=[spec, spec], out_specs=spec,
                          out_shape=jax.ShapeDtypeStruct((T, C), F32), compiler_params=_params(("parallel",)))(x, y)


FF_TILE = 256
FF_TILES = D_FF // FF_TILE
CONV_HALO = 16
FF_ROWS = 32


def _ff_pair_order(w):
    lead = w.shape[:-1]
    n = len(lead)
    w = w.reshape(*lead, 2, FF_TILES, FF_TILE)
    return jnp.swapaxes(w, n, n + 1).reshape(*lead, 2 * D_FF)


def _ff_natural_order(w):
    lead = w.shape[:-1]
    n = len(lead)
    w = w.reshape(*lead, FF_TILES, 2, FF_TILE)
    return jnp.swapaxes(w, n, n + 1).reshape(*lead, 2 * D_FF)


def _conv(ext, w_ref, b_ref):
    c = b_ref[...] + w_ref[2:3, :] * ext
    c = c + w_ref[1:2, :] * pltpu.roll(ext, 1, 0)
    c = c + w_ref[0:1, :] * pltpu.roll(ext, 2, 0)
    return c[CONV_HALO:]


def _ff_specs(T, tm):
    pair = pl.BlockSpec((tm, 2 * FF_TILE), lambda i, j: (i, j))
    prev = pl.BlockSpec((CONV_HALO, 2 * FF_TILE), lambda i, j: (jnp.maximum(i * (tm // CONV_HALO) - 1, 0), j))
    nxt = pl.BlockSpec((CONV_HALO, 2 * FF_TILE),
                       lambda i, j: (jnp.minimum((i + 1) * (tm // CONV_HALO), T // CONV_HALO - 1), j))
    half = pl.BlockSpec((tm, FF_TILE), lambda i, j: (i, j))
    small = lambda r: pl.BlockSpec((r, 2 * FF_TILE), lambda i, j: (0, j))
    return pair, prev, nxt, half, small


def _ff_row_chunks(u_ref, halo, tm, chunk):
    chunk(0, jnp.concatenate([halo, u_ref[0:FF_ROWS, :]], axis=0).astype(F32))

    @pl.loop(1, tm // FF_ROWS)
    def _(c):
        r0 = pl.multiple_of(c * FF_ROWS, FF_ROWS)
        start = pl.multiple_of(r0 - CONV_HALO, CONV_HALO)
        chunk(r0, u_ref[pl.ds(start, FF_ROWS + CONV_HALO), :].astype(F32))


def _swap_grid(spec):
    return pl.BlockSpec(spec.block_shape, lambda j, i, f=spec.index_map: f(i, j))


def _ff_act_fwd(u, conv_w, conv_b, tm=1024):
    T = u.shape[0]
    pair, prev, _, half, small = _ff_specs(T, tm)

    def body(u_ref, p_ref, w_ref, b_ref, a_ref):
        i = pl.program_id(0)
        ext = jnp.concatenate([jnp.where(i > 0, p_ref[...], 0.0), u_ref[...]], axis=0).astype(F32)
        c = _conv(ext, w_ref, b_ref)
        cg, cv = c[:, :FF_TILE], c[:, FF_TILE:]
        a_ref[...] = (cg * jax.nn.sigmoid(cg) * cv).astype(BF16)

    return pl.pallas_call(
        body, name="ff_act_fwd", grid=(T // tm, FF_TILES),
        in_specs=[pair, prev, small(3), small(1)],
        out_specs=half,
        out_shape=jax.ShapeDtypeStruct((T, D_FF), BF16),
        compiler_params=_params(("parallel", "parallel")),
    )(u, u, conv_w, conv_b.reshape(1, -1))


def _ff_act_bwd(u, da, conv_w, conv_b, tm=1024):
    T = u.shape[0]
    pair, prev, _, half, small = _ff_specs(T, tm)

    def body(u_ref, p_ref, da_ref, w_ref, b_ref, dc_ref, dw_ref, db_ref, sums_ref):
        i = pl.program_id(1)

        @pl.when(i == 0)
        def _():
            dw_ref[...] = jnp.zeros_like(dw_ref)
            db_ref[...] = jnp.zeros_like(db_ref)

        sums_ref[...] = jnp.zeros_like(sums_ref)

        def fold(x):
            return jnp.sum(x.reshape(FF_ROWS // 8, 8, x.shape[-1]), axis=0)

        def chunk(r0, ext):
            c = _conv(ext, w_ref, b_ref)
            cg, cv = c[:, :FF_TILE], c[:, FF_TILE:]
            da = da_ref[pl.ds(r0, FF_ROWS), :].astype(F32)
            sg = jax.nn.sigmoid(cg)
            dc = jnp.concatenate([da * cv * sg * (1.0 + cg * (1.0 - sg)), da * cg * sg], axis=1)
            dc_ref[pl.ds(r0, FF_ROWS), :] = dc.astype(BF16)
            sums_ref[3] += fold(dc)
            sums_ref[2] += fold(dc * ext[CONV_HALO:])
            sums_ref[1] += fold(dc * pltpu.roll(ext, 1, 0)[CONV_HALO:])
            sums_ref[0] += fold(dc * pltpu.roll(ext, 2, 0)[CONV_HALO:])

        _ff_row_chunks(u_ref, jnp.where(i > 0, p_ref[...], 0.0), tm, chunk)
        for tap in range(3):
            dw_ref[tap:tap + 1, :] += jnp.sum(sums_ref[tap], axis=0, keepdims=True)
        db_ref[...] += jnp.sum(sums_ref[3], axis=0, keepdims=True)

    return pl.pallas_call(
        body, name="ff_act_bwd", grid=(FF_TILES, T // tm),
        in_specs=[_swap_grid(pair), _swap_grid(prev), _swap_grid(half), _swap_grid(small(3)), _swap_grid(small(1))],
        out_specs=[_swap_grid(pair), _swap_grid(small(3)), _swap_grid(small(1))],
        out_shape=[jax.ShapeDtypeStruct((T, 2 * D_FF), BF16), jax.ShapeDtypeStruct((3, 2 * D_FF), F32),
                   jax.ShapeDtypeStruct((1, 2 * D_FF), F32)],
        scratch_shapes=[pltpu.VMEM((4, 8, 2 * FF_TILE), F32)],
        compiler_params=_params(("parallel", "arbitrary")),
    )(u, u, da, conv_w, conv_b.reshape(1, -1))


def _ff_conv_bwd(dc, conv_w, tm=1024):
    T = dc.shape[0]
    nt = T // tm
    pair, _, nxt, _, small = _ff_specs(T, tm)
    halo = CONV_HALO

    def body(dc_ref, n_ref, w_ref, du_ref):
        i = pl.program_id(0)
        following = jnp.where(i < nt - 1, n_ref[...].astype(F32), 0.0)
        ext = jnp.concatenate([dc_ref[...].astype(F32), following], axis=0)
        n = tm + halo
        du = w_ref[2:3, :] * ext + w_ref[1:2, :] * pltpu.roll(ext, n - 1, 0) + w_ref[0:1, :] * pltpu.roll(ext, n - 2, 0)
        du_ref[...] = du[:tm].astype(BF16)

    return pl.pallas_call(
        body, name="ff_conv_bwd", grid=(nt, FF_TILES),
        in_specs=[pair, nxt, small(3)],
        out_specs=pair,
        out_shape=jax.ShapeDtypeStruct((T, 2 * D_FF), BF16),
        compiler_params=_params(("parallel", "parallel")),
    )(dc, dc, conv_w)


def _loss_head(y, target, tm=512):
    T, C = y.shape
    nt = T // tm

    def body(y_ref, t_ref, dy_ref, l_ref):
        err = y_ref[...] - t_ref[...]
        dy_ref[...] = err * (1.0 / C)
        part = jnp.sum(err * err, axis=0, keepdims=True) * (0.5 / C)
        l_ref[0] = jnp.broadcast_to(part, (8, C))

    spec = pl.BlockSpec((tm, C), lambda i: (i, 0))
    dy, parts = pl.pallas_call(
        body, name="loss_head", grid=(nt,),
        in_specs=[spec, spec],
        out_specs=[spec, pl.BlockSpec((1, 8, C), lambda i: (i, 0, 0))],
        out_shape=[jax.ShapeDtypeStruct((T, C), F32), jax.ShapeDtypeStruct((nt, 8, C), F32)],
        compiler_params=_params(("parallel",)),
    )(y, target)
    return dy, jnp.sum(parts[:, 0, :])


def _adamw_math(w, g, m, v):
    m = ADAM_B1 * m + (1.0 - ADAM_B1) * g
    v = ADAM_B2 * v + (1.0 - ADAM_B2) * (g * g)
    m_hat = m / (1.0 - ADAM_B1 ** ADAM_STEP)
    v_hat = v / (1.0 - ADAM_B2 ** ADAM_STEP)
    delta = -ADAM_LR * (m_hat / (jnp.sqrt(v_hat) + ADAM_EPS) + ADAM_WD * w)
    return delta, m, v


def _adamw(parts, w, m, v, name, tm=256):
    R, C = w.shape
    tm = _pick_rows(R, tm)

    def body(p_ref, w_ref, m_ref, v_ref, g_ref, d_ref, nm_ref, nv_ref):
        g = p_ref[0].astype(F32)
        for s in range(1, N_DEV):
            g = g + p_ref[s].astype(F32)
        delta, nm, nv = _adamw_math(w_ref[...], g, m_ref[...], v_ref[...])
        g_ref[...] = g
        d_ref[...] = delta
        nm_ref[...] = nm
        nv_ref[...] = nv

    spec = pl.BlockSpec((tm, C), lambda i: (i, 0))
    out = jax.ShapeDtypeStruct((R, C), F32)
    return pl.pallas_call(
        body, name=name, grid=(R // tm,),
        in_specs=[pl.BlockSpec((N_DEV, tm, C), lambda i: (0, i, 0)), spec, spec, spec],
        out_specs=[spec] * 4,
        out_shape=[out] * 4,
        compiler_params=_params(("parallel",)),
    )(parts, w, m, v)


def _pick_rows(n, cap):
    if n < 16:
        return n
    best = None
    for t in range(16, min(n, cap) + 1, 16):
        if n % t == 0:
            best = t
    assert best is not None, (n, cap)
    return best


def _exchange(srcs, scatter, name):
    n = len(srcs)

    def body(*refs):
        src_refs, out_refs = refs[:n], refs[n:2 * n]
        send_sems, recv_sems, local_sems = refs[2 * n:]
        x, y, c = lax.axis_index("x"), lax.axis_index("y"), lax.axis_index("c")
        me = 4 * x + 2 * y + c

        def piece(a, d):
            return src_refs[a].at[d] if scatter else src_refs[a]

        local = [pltpu.make_async_copy(piece(a, me), out_refs[a].at[me], local_sems.at[a]) for a in range(n)]
        for cp in local:
            cp.start()
        copies = []
        for k in range(1, N_DEV):
            where, peer = _peer(k)
            for a in range(n):
                cp = pltpu.make_async_remote_copy(
                    src_ref=piece(a, peer), dst_ref=out_refs[a].at[me],
                    send_sem=send_sems.at[a * N_DEV + k], recv_sem=recv_sems.at[a * N_DEV + k],
                    device_id=where, device_id_type=MESH)
                cp.start()
                copies.append((cp, a, k, peer))
        for cp, a, k, peer in copies:
            cp.wait_send()
            pltpu.make_async_remote_copy(
                src_ref=piece(a, peer), dst_ref=out_refs[a].at[peer],
                send_sem=send_sems.at[a * N_DEV + k], recv_sem=recv_sems.at[a * N_DEV + k],
                device_id=(x, y, c), device_id_type=MESH).wait_recv()
        for cp in local:
            cp.wait()

    slab = lambda s: tuple(s.shape[1:] if scatter else s.shape)
    return pl.pallas_call(
        body, name=name,
        in_specs=[pl.BlockSpec(memory_space=pl.ANY)] * n,
        out_specs=[pl.BlockSpec(memory_space=pl.ANY)] * n,
        out_shape=[jax.ShapeDtypeStruct((N_DEV,) + slab(s), s.dtype) for s in srcs],
        scratch_shapes=[pltpu.SemaphoreType.DMA((n * N_DEV,)), pltpu.SemaphoreType.DMA((n * N_DEV,)),
                        pltpu.SemaphoreType.DMA((n,))],
    )(*srcs)


def _peer(k):
    x, y, c = lax.axis_index("x"), lax.axis_index("y"), lax.axis_index("c")
    px = 1 - x if k & 4 else x
    py = 1 - y if k & 2 else y
    pc = 1 - c if k & 1 else c
    return (px, py, pc), 4 * px + 2 * py + pc


def _split_copies(src_refs, land_refs, send_sems, recv_sems, scatter):
    x, y, c = lax.axis_index("x"), lax.axis_index("y"), lax.axis_index("c")
    me = 4 * x + 2 * y + c
    sends, arrivals = [], []
    for k in range(1, N_DEV):
        where, peer = _peer(k)
        for a, (src, land) in enumerate(zip(src_refs, land_refs)):
            piece = src.at[peer] if scatter else src
            sends.append(pltpu.make_async_remote_copy(
                src_ref=piece, dst_ref=land.at[me], send_sem=send_sems.at[a * N_DEV + k], recv_sem=recv_sems.at[a * N_DEV + k],
                device_id=where, device_id_type=MESH))
            arrivals.append(pltpu.make_async_remote_copy(
                src_ref=piece, dst_ref=land.at[peer], send_sem=send_sems.at[a * N_DEV + k], recv_sem=recv_sems.at[a * N_DEV + k],
                device_id=(x, y, c), device_id_type=MESH))
    return sends, arrivals


def _exchange_start(srcs, scatter, name):
    n = len(srcs)
    slab = lambda s: tuple(s.shape[1:] if scatter else s.shape)
    lands = [lax.empty((N_DEV,) + slab(s), s.dtype) for s in srcs]

    def body(*refs):
        src_refs, land_refs = refs[:n], refs[n:2 * n]
        send_sems, recv_sems = refs[2 * n], refs[2 * n + 1]
        token = refs[-1]
        sends, _ = _split_copies(src_refs, land_refs, send_sems, recv_sems, scatter)
        for cp in sends:
            cp.start()
        token[...] = jnp.zeros_like(token)

    hbm = pl.BlockSpec(memory_space=pltpu.HBM)
    sem = pl.BlockSpec(memory_space=pltpu.SEMAPHORE)
    out = pl.pallas_call(
        body, name=name,
        in_specs=[hbm] * (2 * n),
        out_specs=[sem, sem] + [hbm] * (2 * n) + [pl.BlockSpec(memory_space=pltpu.VMEM)],
        out_shape=[pltpu.SemaphoreType.DMA((n * N_DEV,)), pltpu.SemaphoreType.DMA((n * N_DEV,))]
        + [pltpu.HBM(s.shape, s.dtype) for s in srcs] + [pltpu.HBM(l.shape, l.dtype) for l in lands]
        + [jax.ShapeDtypeStruct((8, LANES), F32)],
        input_output_aliases={j: 2 + j for j in range(2 * n)},
        compiler_params=pltpu.CompilerParams(has_side_effects=pltpu.SideEffectType.DATAFLOW_SIDE_EFFECTING),
    )(*[pltpu.with_memory_space_constraint(s, pltpu.HBM) for s in srcs],
      *[pltpu.with_memory_space_constraint(l, pltpu.HBM) for l in lands])
    return (out[0], out[1], out[2:2 + n], out[2 + n:2 + 2 * n]), out[-1]


def _exchange_finish(state, scatter, after, name):
    send_sems, recv_sems, srcs, lands = state
    n = len(srcs)

    def body(*refs):
        src_refs, land_refs = refs[:n], refs[n:2 * n]
        sends, arrivals = _split_copies(src_refs, land_refs, refs[2 * n], refs[2 * n + 1], scatter)
        for cp in sends:
            cp.wait_send()
        for cp in arrivals:
            cp.wait_recv()

    hbm = pl.BlockSpec(memory_space=pltpu.HBM)
    sem = pl.BlockSpec(memory_space=pltpu.SEMAPHORE)
    out = pl.pallas_call(
        body, name=name,
        in_specs=[hbm] * (2 * n) + [sem, sem, pl.BlockSpec(memory_space=pl.ANY)],
        out_specs=[hbm] * (2 * n),
        out_shape=[pltpu.HBM(s.shape, s.dtype) for s in srcs] + [pltpu.HBM(l.shape, l.dtype) for l in lands],
        input_output_aliases={j: j for j in range(2 * n)},
        compiler_params=pltpu.CompilerParams(has_side_effects=pltpu.SideEffectType.DATAFLOW_SIDE_EFFECTING),
    )(*srcs, *lands, send_sems, recv_sems, after)
    return list(out[n:])


def _own_slab(landed, src, scatter):
    me = 4 * lax.axis_index("x") + 2 * lax.axis_index("y") + lax.axis_index("c")
    own = lax.dynamic_index_in_dim(src, me, axis=0, keepdims=True) if scatter else src[None]
    return lax.dynamic_update_slice_in_dim(landed, own, me, axis=0)


SHARDED = ("w_in", "w_branch_a", "w_branch_b", "w_branch_c", "w_out", "w_up", "w_down")
REPLICATED = ("norm_mix", "b_gate", "q_norm_a", "k_norm_a", "rel_bias_a", "w_pool", "pool_scale", "norm_ffn", "conv_b")
WEIGHTS = ("norm_mix", "w_in", "b_gate", "q_norm_a", "k_norm_a", "rel_bias_a", "w_pool", "pool_scale",
           "w_branch_a", "w_branch_b", "w_branch_c", "w_out", "norm_ffn", "w_up", "conv_w", "conv_b", "w_down")
SMALL_COLS = 128
QKV_COLS = 6 * WIDTH


def _rel_index():
    q_off = jnp.arange(CHUNK)[:, None] + N_LEFT * CHUNK
    k_off = jnp.arange(BAND)[None, :]
    return jnp.clip(q_off - k_off, -(CHUNK - 1), MAX_REL) + (CHUNK - 1)


def _rel_onehot():
    rel = _rel_index().reshape(1, CHUNK * BAND)
    return (rel == jnp.arange(REL_TABLE)[:, None]).astype(BF16)


def _select_mm(x, onehot, mode, name):
    hi = x.astype(BF16)
    r1 = x - hi.astype(F32)
    mid = r1.astype(BF16)
    lo = (r1 - mid.astype(F32)).astype(BF16)
    y = _mm(jnp.concatenate([hi, mid, lo, jnp.zeros_like(hi)], axis=0), onehot, mode, F32, name)
    n = x.shape[0]
    return y[:n] + y[n:2 * n] + y[2 * n:3 * n]


def _pack_rows(arrays, cols, row_multiple):
    flat = jnp.concatenate([a.reshape(-1) for a in arrays])
    rows = -(-flat.shape[0] // cols)
    rows = -(-rows // row_multiple) * row_multiple
    return jnp.pad(flat, (0, rows * cols - flat.shape[0])).reshape(rows, cols)


def _unpack_rows(packed, like):
    flat = packed.reshape(-1)
    out, off = [], 0
    for a in like:
        out.append(flat[off:off + a.size].reshape(a.shape))
        off += a.size
    return out


def kernel(x, norm_mix, w_in, b_gate, q_norm_a, k_norm_a, rel_bias_a, w_pool, pool_scale, w_branch_a, w_branch_b, w_branch_c, w_out, norm_ffn, w_up, conv_w, conv_b, w_down, loss_target, m_norm_mix, m_w_in, m_b_gate, m_q_norm_a, m_k_norm_a, m_rel_bias_a, m_w_pool, m_pool_scale, m_w_branch_a, m_w_branch_b, m_w_branch_c, m_w_out, m_norm_ffn, m_w_up, m_conv_w, m_conv_b, m_w_down, v_norm_mix, v_w_in, v_b_gate, v_q_norm_a, v_k_norm_a, v_rel_bias_a, v_w_pool, v_pool_scale, v_w_branch_a, v_w_branch_b, v_w_branch_c, v_w_out, v_norm_ffn, v_w_up, v_conv_w, v_conv_b, v_w_down):
    args = dict(locals())
    w = {n: args[n] for n in WEIGHTS}
    m = {n: args["m_" + n] for n in WEIGHTS}
    v = {n: args["v_" + n] for n in WEIGHTS}
    L = w_in.shape[0]
    T = x.shape[1]
    xs = x.reshape(T, D_MODEL)
    target = loss_target.reshape(T, D_MODEL)

    exchanged = SHARDED + ("conv_w",)
    row_sharded = ("w_out", "w_down")
    shard = {(n, l): (w[n][l] if n == "conv_w" else w[n][l].astype(BF16)) for n in exchanged for l in range(L)}
    late = [key for key in shard if key != ("w_in", 0)]
    gathered = {("w_in", 0): _exchange([shard["w_in", 0]], False, "gather_first")[0]}
    gather_state, gather_token = _exchange_start([shard[key] for key in late], False, "gather_rest_start")

    def full_weight(key):
        g = gathered[key]
        return g.reshape(-1, g.shape[-1]) if key[0] in row_sharded else g.transpose(1, 0, 2).reshape(g.shape[1], -1)

    w_in_f = {0: full_weight(("w_in", 0))}
    conv_b_f = _ff_pair_order(conv_b)
    onehot = _rel_onehot()

    saved = []
    cur = xs
    full = {}
    for l in range(L):
        w_qkv, w_uc, w_g = w_in_f[l][:, :QKV_COLS], w_in_f[l][:, QKV_COLS:QKV_COLS + WIDTH], w_in_f[l][:, QKV_COLS + WIDTH:]
        gain = norm_mix[l] + gather_token[0, 0] if l == 0 else norm_mix[l]
        h = _rmsnorm_fwd(cur, gain, "norm_mix_fwd")
        qkv = _mm(h, w_qkv, "nn", BF16, "proj_qkv")
        uc = _mm(h, w_uc, "nn", F32, "proj_pool")
        glog = _mm(h, w_g, "nn", F32, "proj_gate")
        table = _band_table(_select_mm(rel_bias_a[l], onehot, "nn", "rel_bias_table").reshape(N_HEADS, CHUNK, BAND))
        oa = _attn_a_fwd(qkv, table, q_norm_a[l], k_norm_a[l])
        ob, carries = _attn_b_fwd(qkv)
        oc = _pool_fwd(uc, w_pool[l], pool_scale[l])
        if l == 0:
            landed = _exchange_finish(gather_state, False, ob, "gather_rest_finish")
            gathered.update({key: _own_slab(g, shard[key], False) for key, g in zip(late, landed)})
            full = {key: full_weight(key) for key in gathered}
            w_in_f.update({k: full["w_in", k] for k in range(1, L)})
        w_a, w_b, w_c = (full["w_branch_" + tag, l] for tag in "abc")
        w_out_f, w_down_f = full["w_out", l], full["w_down", l]
        w_up_f, conv_w_f = _ff_pair_order(full["w_up", l]), _ff_pair_order(full["conv_w", l])
        merged, ya, yb, yc = _merge_fwd(oa, ob, oc, glog, b_gate[l], w_a, w_b, w_c)
        x1 = _mm(merged, w_out_f, "nn", F32, "out_proj", res=cur)
        h2 = _rmsnorm_fwd(x1, norm_ffn[l], "norm_ffn_fwd")
        u = _mm(h2, w_up_f, "nn", BF16, "ff_up")
        act = _ff_act_fwd(u, conv_w_f, conv_b_f[l])
        x2 = _mm(act, w_down_f, "nn", F32, "ff_down", res=x1)
        saved.append(dict(x=cur, h=h, qkv=qkv, carries=carries, uc=uc, glog=glog, table=table, oa=oa, ob=ob, oc=oc,
                          ya=ya, yb=yb, yc=yc, merged=merged, x1=x1, h2=h2, u=u, act=act, w_qkv=w_qkv, w_uc=w_uc,
                          w_g=w_g, w_a=w_a, w_b=w_b, w_c=w_c, w_out=w_out_f, w_up=w_up_f, w_down=w_down_f,
                          conv_w=conv_w_f))
        cur = x2

    dcur, loss_local = _loss_head(cur, target)
    loss = lax.psum(loss_local, ("x", "y", "c"))

    def pieces_of(n, g):
        if n in row_sharded:
            return g.reshape(N_DEV, -1, g.shape[-1])
        return g.reshape(g.shape[0], N_DEV, -1).transpose(1, 0, 2)

    gw = {n: [None] * L for n in WEIGHTS}
    for l in reversed(range(L)):
        s = saved[l]
        da = _mm(dcur, s["w_down"], "nt", BF16, "ff_down_dx", tn_cap=1408)
        gw["w_down"][l] = _mm(s["act"], dcur, "tn", BF16, "ff_down_dw")
        dc, dconv_w, dconv_b = _ff_act_bwd(s["u"], da, s["conv_w"], conv_b_f[l])
        du = _ff_conv_bwd(dc, s["conv_w"])
        dh2 = _mm(du, s["w_up"], "nt", F32, "ff_up_dx")
        gw["w_up"][l] = _ff_natural_order(_mm(s["h2"], du, "tn", BF16, "ff_up_dw"))
        gw["conv_w"][l] = _ff_natural_order(dconv_w)
        gw["conv_b"][l] = _ff_natural_order(dconv_b)[0]
        dx1, dg = _rmsnorm_bwd(s["x1"], norm_ffn[l], dh2, dcur, "norm_ffn_bwd")
        gw["norm_ffn"][l] = dg[0]

        dmerged = _mm(dx1, s["w_out"], "nt", F32, "out_proj_dx")
        gw["w_out"][l] = _mm(s["merged"], dx1, "tn", BF16, "out_proj_dw")
        dya, dyb, dyc, dglog, db_gate = _merge_bwd(dmerged, s["glog"], b_gate[l], s["ya"], s["yb"], s["yc"])
        gw["b_gate"][l] = db_gate[0]
        do = {}
        for tag, dy, ok in (("a", dya, s["oa"]), ("b", dyb, s["ob"]), ("c", dyc, s["oc"])):
            do[tag] = _mm(dy, s["w_" + tag], "nt", BF16, "branch_dx_" + tag)
            gw["w_branch_" + tag][l] = _mm(ok, dy, "tn", BF16, "branch_dw_" + tag)
        duc, dw_pool, dscale = _pool_bwd(s["uc"], do["c"], w_pool[l], pool_scale[l])
        gw["w_pool"][l] = dw_pool
        gw["pool_scale"][l] = dscale[0]
        gain_q = q_norm_a[l]
        if l == 0:
            early = [(n, k) for n in exchanged for k in range(L) if (n, k) != ("w_in", 0)]
            early_pieces = [pieces_of(n, gw[n][k]) for n, k in early]
            grads_state, grads_token = _exchange_start(early_pieces, True, "exchange_early_start")
            gain_q = gain_q + grads_token[0, 0]
        dqa, dkc, dkp, dvc, dvp, dtable, dgq = _attn_a_bwd(s["qkv"], do["a"], s["table"], gain_q, k_norm_a[l])
        dka, dva, dgk = _attn_a_bwd_keys(s["qkv"], dkc, dkp, dvc, dvp, k_norm_a[l])
        gw["q_norm_a"][l] = jnp.sum(dgq.reshape(N_HEADS, HEAD_DIM), axis=0)
        gw["k_norm_a"][l] = jnp.sum(dgk.reshape(N_HEADS, HEAD_DIM), axis=0)
        gw["rel_bias_a"][l] = _select_mm(_band_table_bwd(dtable).reshape(N_HEADS, CHUNK * BAND), onehot, "nt",
                                         "rel_bias_table_dw")
        dqb, dkb, dvb = _attn_b_bwd(s["qkv"], s["carries"], do["b"])
        dqkv = jnp.concatenate([dqa, dka, dva, dqb, dkb.T.astype(BF16), dvb.T.astype(BF16)], axis=1)
        dh = _mm(dqkv, s["w_qkv"], "nt", F32, "proj_qkv_dx")
        dh = _mm(duc, s["w_uc"], "nt", F32, "proj_pool_dx", res=dh)
        dh = _mm(dglog, s["w_g"], "nt", F32, "proj_gate_dx", res=dh)
        gw["w_in"][l] = jnp.concatenate([_mm(s["h"], dqkv, "tn", BF16, "proj_qkv_dw"),
                                         _mm(s["h"], duc, "tn", BF16, "proj_pool_dw"),
                                         _mm(s["h"], dglog, "tn", BF16, "proj_gate_dw")], axis=1)
        dcur, dg = _rmsnorm_bwd(s["x"], norm_mix[l], dh, dx1, "norm_mix_bwd")
        gw["norm_mix"][l] = dg[0]

    landed = _exchange_finish(grads_state, True, dcur, "exchange_early_finish")
    parts = {key: _own_slab(g, src, True) for key, g, src in zip(early, landed, early_pieces)}
    parts["w_in", 0] = _exchange([pieces_of("w_in", gw["w_in"][0])], True, "exchange_last")[0]
    small = _pack_rows([jnp.stack(gw[n]) for n in REPLICATED], SMALL_COLS, 16)
    small_parts = _exchange([small], False, "gather_small_grads")[0]

    out = {}
    for n in exchanged:
        res = [_adamw(parts[n, l], w[n][l], m[n][l], v[n][l], "adamw_" + n) for l in range(L)]
        out[n] = tuple(jnp.stack(r) for r in zip(*res))
    rep_like = [w[n] for n in REPLICATED]
    res = _adamw(small_parts, *[_pack_rows([d[n] for n in REPLICATED], SMALL_COLS, 16) for d in (w, m, v)],
                 "adamw_replicated")
    out.update({n: r for n, r in zip(REPLICATED, zip(*[_unpack_rows(r, rep_like) for r in res]))})

    grads, deltas, new_m, new_v = ([out[n][i] for n in WEIGHTS] for i in range(4))
    return (loss, dcur.reshape(x.shape), *grads, *deltas, *new_m, *new_v)
```

```python
import functools
import math

import jax
import jax.numpy as jnp
from jax import lax
from jax.experimental import pallas as pl
from jax.experimental.pallas import tpu as pltpu

F32 = jnp.float32
BF16 = jnp.bfloat16

N_DEV = 8
D_MODEL = 1024
N_HEADS = 8
HEAD_DIM = 64
CHUNK = 64
N_LEFT = 8
BAND = (N_LEFT + 1) * CHUNK
WIDTH = N_HEADS * HEAD_DIM
POOL_WINDOWS = (2, 4, 8, 16)
POOL_DIM = 128
MAX_REL = 2 * CHUNK
REL_TABLE = MAX_REL + CHUNK
D_FF = 2816
EPS = 1e-6
SB_SCAN = 256
SB_ROWS = 512
SB_KEYS = 512
A_BLOCK = N_LEFT * CHUNK
HALO = 16
LANES = 128
VMEM_LIMIT = 56 * 1024 * 1024

ADAM_LR = 0.001
ADAM_B1 = 0.9
ADAM_B2 = 0.999
ADAM_EPS = 1e-08
ADAM_WD = 0.01
ADAM_STEP = 10

MESH = pl.DeviceIdType.MESH


def _params(sem):
    return pltpu.CompilerParams(dimension_semantics=sem, vmem_limit_bytes=VMEM_LIMIT)


def _pick(n, cap):
    if n <= cap:
        return n
    best = None
    for t in range(LANES, cap + 1, LANES):
        if n % t == 0:
            best = t
    assert best is not None, (n, cap)
    return best


MM_TILE_CAP = 1408


def _mm(a, b, mode, out_dtype, name, tm=MM_TILE_CAP, tn_cap=MM_TILE_CAP, tk_cap=MM_TILE_CAP, res=None):
    if mode == "nn":
        (M, K), (K2, N) = a.shape, b.shape
    elif mode == "nt":
        (M, K), (N, K2) = a.shape, b.shape
    else:
        (K, M), (K2, N) = a.shape, b.shape
    assert K == K2, (a.shape, b.shape, mode)
    tm = _pick(M, tm)
    tn = _pick(N, tn_cap)
    tk = _pick(K, tk_cap)
    nk = K // tk
    if mode == "nn":
        dims = (((1,), (0,)), ((), ()))
        a_spec = pl.BlockSpec((tm, tk), lambda i, j, k: (i, k))
        b_spec = pl.BlockSpec((tk, tn), lambda i, j, k: (k, j))
    elif mode == "nt":
        dims = (((1,), (1,)), ((), ()))
        a_spec = pl.BlockSpec((tm, tk), lambda i, j, k: (i, k))
        b_spec = pl.BlockSpec((tn, tk), lambda i, j, k: (j, k))
    else:
        dims = (((0,), (0,)), ((), ()))
        a_spec = pl.BlockSpec((tk, tm), lambda i, j, k: (k, i))
        b_spec = pl.BlockSpec((tk, tn), lambda i, j, k: (k, j))

    o_spec = pl.BlockSpec((tm, tn), lambda i, j, k: (i, j))

    def body(a_ref, b_ref, *rest):
        res_ref = rest[0] if res is not None else None
        o_ref = rest[1] if res is not None else rest[0]
        part = lax.dot_general(a_ref[...].astype(BF16), b_ref[...].astype(BF16), dims, preferred_element_type=F32)
        if nk == 1:
            o_ref[...] = (part if res is None else part + res_ref[...]).astype(out_dtype)
            return
        acc_ref = rest[-1]
        k = pl.program_id(2)

        @pl.when(k == 0)
        def _():
            acc_ref[...] = part

        @pl.when(k > 0)
        def _():
            acc_ref[...] += part

        @pl.when(k == nk - 1)
        def _():
            total = acc_ref[...] if res is None else acc_ref[...] + res_ref[...]
            o_ref[...] = total.astype(out_dtype)

    return pl.pallas_call(
        body, name=name,
        grid=(M // tm, N // tn, nk),
        in_specs=[a_spec, b_spec] + ([o_spec] if res is not None else []),
        out_specs=o_spec,
        out_shape=jax.ShapeDtypeStruct((M, N), out_dtype),
        scratch_shapes=[pltpu.VMEM((tm, tn), F32)] if nk > 1 else [],
        compiler_params=_params(("parallel", "parallel", "arbitrary")),
    )(*((a, b) if res is None else (a, b, res)))


def _rmsnorm_fwd(x, gain, name, tm=512):
    T, C = x.shape

    def body(x_ref, g_ref, h_ref):
        xv = x_ref[...]
        r = lax.rsqrt(jnp.mean(xv * xv, axis=-1, keepdims=True) + EPS)
        h_ref[...] = (xv * r * g_ref[...]).astype(BF16)

    return pl.pallas_call(
        body, name=name, grid=(T // tm,),
        in_specs=[pl.BlockSpec((tm, C), lambda i: (i, 0)), pl.BlockSpec((1, C), lambda i: (0, 0))],
        out_specs=pl.BlockSpec((tm, C), lambda i: (i, 0)),
        out_shape=jax.ShapeDtypeStruct((T, C), BF16),
        compiler_params=_params(("parallel",)),
    )(x, gain.reshape(1, C))


def _rmsnorm_bwd(x, gain, dh, dres, name, tm=512):
    T, C = x.shape

    def body(x_ref, g_ref, dh_ref, dres_ref, dx_ref, dxb_ref, dg_ref):
        @pl.when(pl.program_id(0) == 0)
        def _():
            dg_ref[...] = jnp.zeros_like(dg_ref)

        xv = x_ref[...]
        dy = dh_ref[...].astype(F32)
        r = lax.rsqrt(jnp.mean(xv * xv, axis=-1, keepdims=True) + EPS)
        gdy = dy * g_ref[...]
        inner = jnp.mean(xv * gdy, axis=-1, keepdims=True)
        dx = dres_ref[...] + r * gdy - xv * (r * r * r * inner)
        dx_ref[...] = dx
        dxb_ref[...] = dx.astype(BF16)
        dg_ref[...] += jnp.sum(dy * xv * r, axis=0, keepdims=True)

    row = pl.BlockSpec((tm, C), lambda i: (i, 0))
    return pl.pallas_call(
        body, name=name, grid=(T // tm,),
        in_specs=[row, pl.BlockSpec((1, C), lambda i: (0, 0)), row, row],
        out_specs=[row, row, pl.BlockSpec((1, C), lambda i: (0, 0))],
        out_shape=[jax.ShapeDtypeStruct((T, C), F32), jax.ShapeDtypeStruct((T, C), BF16),
                   jax.ShapeDtypeStruct((1, C), F32)],
        compiler_params=_params(("arbitrary",)),
    )(x, gain.reshape(1, C), dh, dres)


MASKED = -1e30


def _pair_sum(x, same_head):
    hi = x.astype(BF16)
    lo = (x - hi.astype(F32)).astype(BF16)
    return jnp.dot(hi, same_head, preferred_element_type=F32) + jnp.dot(lo, same_head, preferred_element_type=F32)


def _same_head():
    r = lax.broadcasted_iota(jnp.int32, (2 * HEAD_DIM, 2 * HEAD_DIM), 0)
    c = lax.broadcasted_iota(jnp.int32, (2 * HEAD_DIM, 2 * HEAD_DIM), 1)
    return jnp.where((r < HEAD_DIM) == (c < HEAD_DIM), 1.0, 0.0).astype(BF16)


def _pair_norm(t, g, same_head):
    tf = t.astype(F32)
    r = lax.rsqrt(_pair_sum(tf * tf, same_head) * (1.0 / HEAD_DIM) + EPS)
    return tf * r * g


def _pair_norm_bwd(t, g, dn, same_head):
    tf = t.astype(F32)
    r = lax.rsqrt(_pair_sum(tf * tf, same_head) * (1.0 / HEAD_DIM) + EPS)
    gd = dn * g
    inner = _pair_sum(tf * gd, same_head) * (1.0 / HEAD_DIM)
    return r * gd - tf * (r * r * r * inner), jnp.sum(dn * tf * r, axis=0, keepdims=True)


def _band_table(bias):
    rows = [jnp.pad(bias, ((0, 0), (0, 0), (c * CHUNK, 2 * A_BLOCK - BAND - c * CHUNK)), constant_values=MASKED)
            for c in range(N_LEFT)]
    return jnp.concatenate(rows, axis=1).reshape(N_HEADS // 2, 2 * A_BLOCK, 2 * A_BLOCK)


def _band_table_bwd(dtable):
    dtable = dtable.reshape(N_HEADS, A_BLOCK, 2 * A_BLOCK)
    return sum(dtable[:, c * CHUNK:(c + 1) * CHUNK, c * CHUNK:c * CHUNK + BAND] for c in range(N_LEFT))


def _a_specs(T):
    nb = T // A_BLOCK
    pairs = N_HEADS // 2
    col = lambda which: which * pairs
    cur = lambda which: pl.BlockSpec((A_BLOCK, 2 * HEAD_DIM), lambda p, i: (i, col(which) + p))
    prev = lambda which: pl.BlockSpec((A_BLOCK, 2 * HEAD_DIM), lambda p, i: (jnp.maximum(i - 1, 0), col(which) + p))
    nxt = lambda which: pl.BlockSpec((A_BLOCK, 2 * HEAD_DIM), lambda p, i: (jnp.minimum(i + 1, nb - 1), col(which) + p))
    table = pl.BlockSpec((1, 2 * A_BLOCK, 2 * A_BLOCK), lambda p, i: (p, 0, 0))
    gain = pl.BlockSpec((1, 2 * HEAD_DIM), lambda p, i: (0, 0))
    gacc = pl.BlockSpec((1, 1, 2 * HEAD_DIM), lambda p, i: (p, 0, 0))
    return nb, pairs, cur, prev, nxt, table, gain, gacc


def _a_probs(q_ref, kc_ref, kp_ref, t_ref, gq_ref, gk_ref, same_head, first):
    scale = 1.0 / math.sqrt(HEAD_DIM)
    qst = _stack_heads((_pair_norm(q_ref[...], gq_ref[...], same_head) * scale).astype(BF16))
    kcat = jnp.concatenate([_pair_norm(kp_ref[...], gk_ref[...], same_head).astype(BF16),
                            _pair_norm(kc_ref[...], gk_ref[...], same_head).astype(BF16)], axis=0)
    s = lax.dot_general(qst, kcat, (((1,), (1,)), ((), ())), preferred_element_type=F32) + t_ref[0]
    col = lax.broadcasted_iota(jnp.int32, s.shape, 1)
    s = jnp.where(col >= jnp.where(first, A_BLOCK, 0), s, MASKED)
    e = jnp.exp(s - jnp.max(s, axis=-1, keepdims=True))
    return qst, kcat, e, jnp.sum(e, axis=-1, keepdims=True)


def _attn_a_fwd(qkv, table, gq, gk):
    T = qkv.shape[0]
    nb, pairs, cur, prev, _, tspec, gspec, _ = _a_specs(T)

    def body(q_ref, kc_ref, kp_ref, vc_ref, vp_ref, t_ref, gq_ref, gk_ref, o_ref):
        same_head = _same_head()
        _, _, e, total = _a_probs(q_ref, kc_ref, kp_ref, t_ref, gq_ref, gk_ref, same_head, pl.program_id(1) == 0)
        vcat = jnp.concatenate([vp_ref[...], vc_ref[...]], axis=0)
        p = (e / total).astype(BF16)
        o_ref[...] = _unstack_heads(jnp.dot(p, vcat, preferred_element_type=F32)).astype(BF16)

    return pl.pallas_call(
        body, name="attn_a_fwd", grid=(pairs, nb),
        in_specs=[cur(0), cur(1), prev(1), cur(2), prev(2), tspec, gspec, gspec],
        out_specs=pl.BlockSpec((A_BLOCK, 2 * HEAD_DIM), lambda p, i: (i, p)),
        out_shape=jax.ShapeDtypeStruct((T, WIDTH), BF16),
        compiler_params=_params(("parallel", "arbitrary")),
    )(qkv, qkv, qkv, qkv, qkv, table, jnp.tile(gq.reshape(1, HEAD_DIM), (1, 2)), jnp.tile(gk.reshape(1, HEAD_DIM), (1, 2)))


def _attn_a_bwd(qkv, do, table, gq, gk):
    T = qkv.shape[0]
    nb, pairs, cur, prev, _, tspec, gspec, gacc = _a_specs(T)
    scale = 1.0 / math.sqrt(HEAD_DIM)
    oblk = pl.BlockSpec((A_BLOCK, 2 * HEAD_DIM), lambda p, i: (i, p))

    def body(q_ref, kc_ref, kp_ref, vc_ref, vp_ref, do_ref, t_ref, gq_ref, gk_ref,
             dq_ref, dkc_ref, dkp_ref, dvc_ref, dvp_ref, dt_ref, dgq_ref):
        first = pl.program_id(1) == 0

        @pl.when(first)
        def _():
            dt_ref[...] = jnp.zeros_like(dt_ref)
            dgq_ref[...] = jnp.zeros_like(dgq_ref)

        same_head = _same_head()
        qst, kcat, e, total = _a_probs(q_ref, kc_ref, kp_ref, t_ref, gq_ref, gk_ref, same_head, first)
        vcat = jnp.concatenate([vp_ref[...], vc_ref[...]], axis=0)
        dost = _stack_heads(do_ref[...])
        p = e / total
        dp = lax.dot_general(dost, vcat, (((1,), (1,)), ((), ())), preferred_element_type=F32)
        ds = p * (dp - jnp.sum(p * dp, axis=-1, keepdims=True))
        dt_ref[0] += ds
        dsb = ds.astype(BF16)
        dqn = _unstack_heads(jnp.dot(dsb, kcat, preferred_element_type=F32)) * scale
        dq, dg = _pair_norm_bwd(q_ref[...], gq_ref[...], dqn, same_head)
        dq_ref[...] = dq.astype(BF16)
        dgq_ref[0] += dg
        dk = lax.dot_general(dsb, qst, (((0,), (0,)), ((), ())), preferred_element_type=F32)
        dv = lax.dot_general(p.astype(BF16), dost, (((0,), (0,)), ((), ())), preferred_element_type=F32)
        dkp_ref[...] = dk[:A_BLOCK]
        dkc_ref[...] = dk[A_BLOCK:]
        dvp_ref[...] = dv[:A_BLOCK]
        dvc_ref[...] = dv[A_BLOCK:]

    wide = jax.ShapeDtypeStruct((T, WIDTH), F32)
    return pl.pallas_call(
        body, name="attn_a_bwd", grid=(pairs, nb),
        in_specs=[cur(0), cur(1), prev(1), cur(2), prev(2), oblk, tspec, gspec, gspec],
        out_specs=[oblk, oblk, oblk, oblk, oblk, tspec, gacc],
        out_shape=[jax.ShapeDtypeStruct((T, WIDTH), BF16), wide, wide, wide, wide,
                   jax.ShapeDtypeStruct((pairs, 2 * A_BLOCK, 2 * A_BLOCK), F32),
                   jax.ShapeDtypeStruct((pairs, 1, 2 * HEAD_DIM), F32)],
        compiler_params=_params(("parallel", "arbitrary")),
    )(qkv, qkv, qkv, qkv, qkv, do, table, jnp.tile(gq.reshape(1, HEAD_DIM), (1, 2)),
      jnp.tile(gk.reshape(1, HEAD_DIM), (1, 2)))


def _attn_a_bwd_keys(qkv, dkc, dkp, dvc, dvp, gk):
    T = qkv.shape[0]
    nb, pairs, cur, _, _, _, gspec, gacc = _a_specs(T)
    oblk = pl.BlockSpec((A_BLOCK, 2 * HEAD_DIM), lambda p, i: (i, p))
    onext = pl.BlockSpec((A_BLOCK, 2 * HEAD_DIM), lambda p, i: (jnp.minimum(i + 1, nb - 1), p))

    def body(k_ref, dkc_ref, dkp_ref, dvc_ref, dvp_ref, gk_ref, dk_ref, dv_ref, dgk_ref):
        i = pl.program_id(1)

        @pl.when(i == 0)
        def _():
            dgk_ref[...] = jnp.zeros_like(dgk_ref)

        has_next = (i < nb - 1).astype(F32)
        dkn = dkc_ref[...] + has_next * dkp_ref[...]
        dk, dg = _pair_norm_bwd(k_ref[...], gk_ref[...], dkn, _same_head())
        dk_ref[...] = dk.astype(BF16)
        dv_ref[...] = (dvc_ref[...] + has_next * dvp_ref[...]).astype(BF16)
        dgk_ref[0] += dg

    blk = jax.ShapeDtypeStruct((T, WIDTH), BF16)
    return pl.pallas_call(
        body, name="attn_a_bwd_keys", grid=(pairs, nb),
        in_specs=[cur(1), oblk, onext, oblk, onext, gspec],
        out_specs=[oblk, oblk, gacc],
        out_shape=[blk, blk, jax.ShapeDtypeStruct((pairs, 1, 2 * HEAD_DIM), F32)],
        compiler_params=_params(("parallel", "arbitrary")),
    )(qkv, dkc, dkp, dvc, dvp, jnp.tile(gk.reshape(1, HEAD_DIM), (1, 2)))


def _scan_matrix(later):
    r = lax.broadcasted_iota(jnp.int32, (SB_SCAN, SB_SCAN), 0)
    c = lax.broadcasted_iota(jnp.int32, (SB_SCAN, SB_SCAN), 1)
    return jnp.where((r > c) if later else (r < c), 1.0, 0.0).astype(BF16)


def _running_sums(x, carry, scan, later):
    n = SB_KEYS // SB_SCAN
    parts = [None] * n
    total = carry
    for sb in (reversed(range(n)) if later else range(n)):
        xs = x[:, sb * SB_SCAN:(sb + 1) * SB_SCAN]
        local = jnp.dot(xs.astype(BF16), scan, preferred_element_type=F32)
        parts[sb] = local if total is None else local + total
        rowsum = jnp.sum(xs, axis=-1, keepdims=True)
        total = rowsum if total is None else total + rowsum
    return (parts[0] if n == 1 else jnp.concatenate(parts, axis=1)), total


def _sb_log_sigmoids(z):
    neg_abs = pltpu.bitcast(pltpu.bitcast(z, jnp.uint32) | jnp.uint32(0x80000000), F32)
    take = jnp.minimum(z, 0.0) - jnp.log(1.0 + jnp.exp(neg_abs))
    return take, take - z


def _sb_mask():
    r = lax.broadcasted_iota(jnp.int32, (2 * SB_ROWS, SB_KEYS), 0)
    c = lax.broadcasted_iota(jnp.int32, (2 * SB_ROWS, SB_KEYS), 1)
    return c < jnp.where(r >= SB_ROWS, r - SB_ROWS, r)


def _stack_heads(t):
    lane = lax.broadcasted_iota(jnp.int32, t.shape, 1)
    zero = jnp.zeros_like(t)
    return jnp.concatenate([jnp.where(lane < HEAD_DIM, t, zero), jnp.where(lane >= HEAD_DIM, t, zero)], axis=0)


def _unstack_heads(t):
    rows = t.shape[0] // 2
    lane = lax.broadcasted_iota(jnp.int32, (rows, 2 * HEAD_DIM), 1)
    return jnp.where(lane < HEAD_DIM, t[:rows], t[rows:])


def _sb_specs(T):
    nq = T // SB_ROWS
    blk = lambda col: pl.BlockSpec((SB_ROWS, 2 * HEAD_DIM), lambda p, i: (i, col + p))
    full = lambda col: pl.BlockSpec((T, 2 * HEAD_DIM), lambda p, i: (0, col + p))
    return nq, blk, full


def _key_rows(j):
    return pl.ds(pl.multiple_of(j * SB_KEYS, SB_KEYS), SB_KEYS)


def _attn_b_fwd(qkv):
    T = qkv.shape[0]
    pairs = N_HEADS // 2
    nq, blk, full = _sb_specs(T)
    scale = 1.0 / math.sqrt(HEAD_DIM)

    assert nq <= LANES

    def body(q_ref, k_ref, v_ref, o_ref, c_ref, acc_ref, carry_ref, z_ref, w_ref):
        i = pl.program_id(1)
        scan = _scan_matrix(True)
        qst = _stack_heads((q_ref[...].astype(F32) * scale).astype(BF16))
        lane = lax.broadcasted_iota(jnp.int32, (2 * SB_ROWS, LANES), 1)

        def scores(j):
            return lax.dot_general(qst, k_ref[_key_rows(j), :], (((1,), (1,)), ((), ())), preferred_element_type=F32)

        def weights(z, carry, mask):
            take, keep = _sb_log_sigmoids(z)
            if mask is not None:
                keep = jnp.where(mask, keep, 0.0)
            tail, total = _running_sums(keep, carry, scan, True)
            w = jnp.exp(take + tail)
            if mask is not None:
                w = jnp.where(mask, w, 0.0)
            return w.astype(BF16), total

        w_ref[...], carry_ref[...] = weights(scores(i), None, _sb_mask())
        z_ref[...] = scores(jnp.maximum(i - 1, 0))
        acc_ref[...] = jnp.zeros_like(acc_ref)
        c_ref[0, 0] = jnp.zeros((2 * SB_ROWS, LANES), F32)

        @pl.loop(0, i)
        def _(jj):
            j = i - 1 - jj
            z = z_ref[...]
            z_ref[...] = scores(jnp.maximum(j - 1, 0))
            acc_ref[...] += jnp.dot(w_ref[...], v_ref[_key_rows(j + 1), :], preferred_element_type=F32)
            carry = carry_ref[...]
            c_ref[0, 0] = jnp.where(lane == j, carry, c_ref[0, 0])
            w_ref[...], carry_ref[...] = weights(z, carry, None)

        acc = acc_ref[...] + jnp.dot(w_ref[...], v_ref[_key_rows(0), :], preferred_element_type=F32)
        o_ref[...] = _unstack_heads(acc).astype(BF16)

    return pl.pallas_call(
        body, name="attn_b_fwd", grid=(pairs, nq),
        in_specs=[blk(3 * pairs), full(4 * pairs), full(5 * pairs)],
        out_specs=[pl.BlockSpec((SB_ROWS, 2 * HEAD_DIM), lambda p, i: (i, p)),
                   pl.BlockSpec((1, 1, 2 * SB_ROWS, LANES), lambda p, i: (p, i, 0, 0))],
        out_shape=[jax.ShapeDtypeStruct((T, WIDTH), BF16), jax.ShapeDtypeStruct((pairs, nq, 2 * SB_ROWS, LANES), F32)],
        scratch_shapes=[pltpu.VMEM((2 * SB_ROWS, 2 * HEAD_DIM), F32), pltpu.VMEM((2 * SB_ROWS, 1), F32),
                        pltpu.VMEM((2 * SB_ROWS, SB_KEYS), F32), pltpu.VMEM((2 * SB_ROWS, SB_KEYS), BF16)],
        compiler_params=_params(("parallel", "arbitrary")),
    )(qkv, qkv, qkv)


def _attn_b_bwd(qkv, carries, do):
    T = qkv.shape[0]
    pairs = N_HEADS // 2
    nq, blk, full = _sb_specs(T)
    scale = 1.0 / math.sqrt(HEAD_DIM)
    oblk = pl.BlockSpec((SB_ROWS, 2 * HEAD_DIM), lambda p, i: (i, p))
    ofull = pl.BlockSpec((2 * HEAD_DIM, T), lambda p, i: (p, 0))

    def body(q_ref, k_ref, v_ref, c_ref, do_ref, dq_ref, dk_ref, dv_ref, dqacc_ref, before_ref,
             z_ref, dw_ref, dz_ref, w_ref):
        i = pl.program_id(1)

        @pl.when(i == 0)
        def _():
            dk_ref[...] = jnp.zeros_like(dk_ref)
            dv_ref[...] = jnp.zeros_like(dv_ref)

        scan_later = _scan_matrix(True)
        scan_earlier = _scan_matrix(False)
        qst = _stack_heads((q_ref[...].astype(F32) * scale).astype(BF16))
        dost = _stack_heads(do_ref[...].astype(BF16))
        lane = lax.broadcasted_iota(jnp.int32, (2 * SB_ROWS, LANES), 1)
        nt = (((1,), (1,)), ((), ()))

        def products(j):
            return (lax.dot_general(qst, k_ref[_key_rows(j), :], nt, preferred_element_type=F32),
                    lax.dot_general(dost, v_ref[_key_rows(j), :], nt, preferred_element_type=F32))

        def score_grads(z, dw, later, mask):
            take, keep = _sb_log_sigmoids(z)
            sig = jnp.exp(take)
            if mask is not None:
                keep = jnp.where(mask, keep, 0.0)
            tail, _ = _running_sums(keep, later, scan_later, True)
            w = jnp.exp(take + tail)
            if mask is not None:
                w = jnp.where(mask, w, 0.0)
            g = w * dw
            before, before_ref[...] = _running_sums(g, before_ref[...], scan_earlier, False)
            dz = g - sig * (g + before)
            if mask is not None:
                dz = jnp.where(mask, dz, 0.0)
            return dz.astype(BF16), w.astype(BF16)

        qst_t = qst.T
        dost_t = dost.T

        def accumulate(j, dzb, wb):
            cols = pl.ds(pl.multiple_of(j * SB_KEYS, SB_KEYS), SB_KEYS)
            dqacc_ref[...] += jnp.dot(dzb, k_ref[_key_rows(j), :], preferred_element_type=F32)
            dk_ref[:, cols] += jnp.dot(qst_t, dzb, preferred_element_type=F32)
            dv_ref[:, cols] += jnp.dot(dost_t, wb, preferred_element_type=F32)

        dqacc_ref[...] = jnp.zeros_like(dqacc_ref)
        before_ref[...] = jnp.zeros_like(before_ref)
        dz_ref[...] = jnp.zeros_like(dz_ref)
        w_ref[...] = jnp.zeros_like(w_ref)
        z_ref[...], dw_ref[...] = products(0)

        @pl.loop(0, i)
        def _(j):
            z, dw = z_ref[...], dw_ref[...]
            accumulate(jnp.maximum(j - 1, 0), dz_ref[...], w_ref[...])
            later = jnp.sum(jnp.where(lane == j, c_ref[0, 0], 0.0), axis=-1, keepdims=True)
            dz_ref[...], w_ref[...] = score_grads(z, dw, later, None)
            z_ref[...], dw_ref[...] = products(j + 1)

        accumulate(jnp.maximum(i - 1, 0), dz_ref[...], w_ref[...])
        accumulate(i, *score_grads(z_ref[...], dw_ref[...], None, _sb_mask()))
        dq_ref[...] = (_unstack_heads(dqacc_ref[...]) * scale).astype(BF16)

    wide = jax.ShapeDtypeStruct((WIDTH, T), F32)
    return pl.pallas_call(
        body, name="attn_b_bwd", grid=(pairs, nq),
        in_specs=[blk(3 * pairs), full(4 * pairs), full(5 * pairs),
                  pl.BlockSpec((1, 1, 2 * SB_ROWS, LANES), lambda p, i: (p, i, 0, 0)), oblk],
        out_specs=[oblk, ofull, ofull],
        out_shape=[jax.ShapeDtypeStruct((T, WIDTH), BF16), wide, wide],
        scratch_shapes=[pltpu.VMEM((2 * SB_ROWS, 2 * HEAD_DIM), F32), pltpu.VMEM((2 * SB_ROWS, 1), F32),
                        pltpu.VMEM((2 * SB_ROWS, SB_KEYS), F32), pltpu.VMEM((2 * SB_ROWS, SB_KEYS), F32),
                        pltpu.VMEM((2 * SB_ROWS, SB_KEYS), BF16), pltpu.VMEM((2 * SB_ROWS, SB_KEYS), BF16)],
        compiler_params=_params(("parallel", "arbitrary")),
    )(qkv, qkv, qkv, carries, do)


def _window_sums(ext, forward):
    n = ext.shape[0]
    out = []
    s = ext
    for step in (1, 2, 4, 8):
        s = s + pltpu.roll(s, (n - step) if forward else step, 0)
        out.append(s)
    return out


def _pool_counts(base, rows, win):
    t = base + lax.broadcasted_iota(jnp.int32, (rows, 1), 0)
    return jnp.minimum(t + 1, win).astype(F32)


def _pooled(u_ref, up_ref, i, tm):
    prev = jnp.where(i > 0, up_ref[...], 0.0)
    ext = jnp.concatenate([prev, u_ref[...]], axis=0)
    sums = _window_sums(ext, False)
    parts = []
    for g, win in enumerate(POOL_WINDOWS):
        cols = slice(g * POOL_DIM, (g + 1) * POOL_DIM)
        cnt = _pool_counts(i * tm, tm, win)
        parts.append(sums[g][HALO:, cols] / cnt - ext[HALO:, cols])
    return parts


def _pool_fwd(ucg, w_pool, scale, tm=512):
    T = ucg.shape[0]
    C = WIDTH

    def body(u_ref, up_ref, w_ref, s_ref, o_ref):
        i = pl.program_id(0)
        parts = _pooled(u_ref, up_ref, i, tm)
        for g in range(len(POOL_WINDOWS)):
            mixed = jnp.dot(parts[g].astype(BF16), w_ref[g], preferred_element_type=F32)
            o_ref[:, g * POOL_DIM:(g + 1) * POOL_DIM] = (mixed * s_ref[:, g * POOL_DIM:(g + 1) * POOL_DIM]).astype(BF16)

    return pl.pallas_call(
        body, name="pool_fwd", grid=(T // tm,),
        in_specs=[pl.BlockSpec((tm, C), lambda i: (i, 0)),
                  pl.BlockSpec((HALO, C), lambda i: (jnp.maximum(i * (tm // HALO) - 1, 0), 0)),
                  pl.BlockSpec((len(POOL_WINDOWS), POOL_DIM, POOL_DIM), lambda i: (0, 0, 0)),
                  pl.BlockSpec((1, C), lambda i: (0, 0))],
        out_specs=pl.BlockSpec((tm, C), lambda i: (i, 0)),
        out_shape=jax.ShapeDtypeStruct((T, C), BF16),
        compiler_params=_params(("parallel",)),
    )(ucg, ucg, w_pool.astype(BF16), scale.reshape(1, C))


def _pool_bwd(ucg, do_c, w_pool, scale, tm=512):
    T = ucg.shape[0]
    C = WIDTH
    nt = T // tm
    G = len(POOL_WINDOWS)

    def body(u_ref, up_ref, do_ref, don_ref, w_ref, s_ref, du_ref, dw_ref, ds_ref):
        i = pl.program_id(0)

        @pl.when(i == 0)
        def _():
            dw_ref[...] = jnp.zeros_like(dw_ref)
            ds_ref[...] = jnp.zeros_like(ds_ref)

        parts = _pooled(u_ref, up_ref, i, tm)
        nxt = jnp.where(i < nt - 1, don_ref[...].astype(F32), 0.0)
        do_ext = jnp.concatenate([do_ref[...].astype(F32), nxt], axis=0) * s_ref[...]
        for g, win in enumerate(POOL_WINDOWS):
            cols = slice(g * POOL_DIM, (g + 1) * POOL_DIM)
            pooled_b = parts[g].astype(BF16)
            dmix = do_ext[:, cols].astype(BF16)
            mixed = jnp.dot(pooled_b, w_ref[g], preferred_element_type=F32)
            ds_ref[:, cols] += jnp.sum(do_ref[:, cols].astype(F32) * mixed, axis=0, keepdims=True)
            dw_ref[g] += lax.dot_general(pooled_b, dmix[:tm], (((0,), (0,)), ((), ())), preferred_element_type=F32)
            dpool = lax.dot_general(dmix, w_ref[g], (((1,), (1,)), ((), ())), preferred_element_type=F32)
            scaled = dpool / _pool_counts(i * tm, tm + HALO, win)
            fwd = _window_sums(scaled, True)[g]
            du_ref[:, cols] = (fwd[:tm] - dpool[:tm]).astype(BF16)

    return pl.pallas_call(
        body, name="pool_bwd", grid=(nt,),
        in_specs=[pl.BlockSpec((tm, C), lambda i: (i, 0)),
                  pl.BlockSpec((HALO, C), lambda i: (jnp.maximum(i * (tm // HALO) - 1, 0), 0)),
                  pl.BlockSpec((tm, C), lambda i: (i, 0)),
                  pl.BlockSpec((HALO, C), lambda i: (jnp.minimum((i + 1) * (tm // HALO), T // HALO - 1), 0)),
                  pl.BlockSpec((G, POOL_DIM, POOL_DIM), lambda i: (0, 0, 0)),
                  pl.BlockSpec((1, C), lambda i: (0, 0))],
        out_specs=[pl.BlockSpec((tm, C), lambda i: (i, 0)),
                   pl.BlockSpec((G, POOL_DIM, POOL_DIM), lambda i: (0, 0, 0)),
                   pl.BlockSpec((1, C), lambda i: (0, 0))],
        out_shape=[jax.ShapeDtypeStruct((T, C), BF16), jax.ShapeDtypeStruct((G, POOL_DIM, POOL_DIM), F32),
                   jax.ShapeDtypeStruct((1, C), F32)],
        compiler_params=_params(("arbitrary",)),
    )(ucg, ucg, do_c, do_c, w_pool.astype(BF16), scale.reshape(1, C))


def _merge_fwd(oa, ob, oc, glog, b_gate, wa, wb, wc, tm=256):
    T = oa.shape[0]
    Dm = D_MODEL
    row = lambda c: pl.BlockSpec((tm, c), lambda i: (i, 0))
    wspec = pl.BlockSpec((WIDTH, Dm), lambda i: (0, 0))

    def body(oa_ref, ob_ref, oc_ref, g_ref, b_ref, wa_ref, wb_ref, wc_ref, m_ref, ya_ref, yb_ref, yc_ref):
        merged = jnp.zeros((tm, Dm), F32)
        for kk, (o_ref, w_ref, y_ref) in enumerate(((oa_ref, wa_ref, ya_ref), (ob_ref, wb_ref, yb_ref),
                                                    (oc_ref, wc_ref, yc_ref))):
            y = jnp.dot(o_ref[...].astype(BF16), w_ref[...], preferred_element_type=F32)
            gate = jax.nn.sigmoid(g_ref[:, kk * Dm:(kk + 1) * Dm] + b_ref[:, kk * Dm:(kk + 1) * Dm])
            merged = merged + gate * y
            y_ref[...] = y.astype(BF16)
        m_ref[...] = merged.astype(BF16)

    out = jax.ShapeDtypeStruct((T, Dm), BF16)
    return pl.pallas_call(
        body, name="merge_fwd", grid=(T // tm,),
        in_specs=[row(WIDTH), row(WIDTH), row(WIDTH), row(3 * Dm), pl.BlockSpec((1, 3 * Dm), lambda i: (0, 0)),
                  wspec, wspec, wspec],
        out_specs=[row(Dm)] * 4,
        out_shape=[out] * 4,
        compiler_params=_params(("parallel",)),
    )(oa, ob, oc, glog, b_gate.reshape(1, 3 * Dm), wa, wb, wc)


def _merge_bwd(dmerged, glog, b_gate, ya, yb, yc, tm=256):
    T = dmerged.shape[0]
    Dm = D_MODEL
    row = lambda c: pl.BlockSpec((tm, c), lambda i: (i, 0))

    def body(dm_ref, g_ref, b_ref, ya_ref, yb_ref, yc_ref, dya_ref, dyb_ref, dyc_ref, dg_ref, db_ref):
        @pl.when(pl.program_id(0) == 0)
        def _():
            db_ref[...] = jnp.zeros_like(db_ref)

        dm = dm_ref[...]
        for kk, (y_ref, dy_ref) in enumerate(((ya_ref, dya_ref), (yb_ref, dyb_ref), (yc_ref, dyc_ref))):
            cols = slice(kk * Dm, (kk + 1) * Dm)
            gate = jax.nn.sigmoid(g_ref[:, cols] + b_ref[:, cols])
            dy_ref[...] = (dm * gate).astype(BF16)
            dlog = dm * y_ref[...].astype(F32) * gate * (1.0 - gate)
            dg_ref[:, cols] = dlog.astype(BF16)
            db_ref[:, cols] += jnp.sum(dlog, axis=0, keepdims=True)

    out = jax.ShapeDtypeStruct((T, Dm), BF16)
    return pl.pallas_call(
        body, name="merge_bwd", grid=(T // tm,),
        in_specs=[row(Dm), row(3 * Dm), pl.BlockSpec((1, 3 * Dm), lambda i: (0, 0)), row(Dm), row(Dm), row(Dm)],
        out_specs=[row(Dm), row(Dm), row(Dm), row(3 * Dm), pl.BlockSpec((1, 3 * Dm), lambda i: (0, 0))],
        out_shape=[out, out, out, jax.ShapeDtypeStruct((T, 3 * Dm), BF16), jax.ShapeDtypeStruct((1, 3 * Dm), F32)],
        compiler_params=_params(("arbitrary",)),
    )(dmerged, glog, b_gate.reshape(1, 3 * Dm), ya, yb, yc)


def _residual_add(x, y, name, tm=512):
    T, C = x.shape

    def body(x_ref, y_ref, o_ref):
        o_ref[...] = x_ref[...] + y_ref[...]

    spec = pl.BlockSpec((tm, C), lambda i: (i, 0))
    return pl.pallas_call(body, name=name, grid=(T // tm,), in_specs=[spec, spec], out_specs=spec,
                          out_shape=jax.ShapeDtypeStruct((T, C), F32), compiler_params=_params(("parallel",)))(x, y)


FF_TILE = 256
FF_TILES = D_FF // FF_TILE
CONV_HALO = 16
FF_ROWS = 32


def _ff_pair_order(w):
    lead = w.shape[:-1]
    n = len(lead)
    w = w.reshape(*lead, 2, FF_TILES, FF_TILE)
    return jnp.swapaxes(w, n, n + 1).reshape(*lead, 2 * D_FF)


def _ff_natural_order(w):
    lead = w.shape[:-1]
    n = len(lead)
    w = w.reshape(*lead, FF_TILES, 2, FF_TILE)
    return jnp.swapaxes(w, n, n + 1).reshape(*lead, 2 * D_FF)


def _conv(ext, w_ref, b_ref):
    c = b_ref[...] + w_ref[2:3, :] * ext
    c = c + w_ref[1:2, :] * pltpu.roll(ext, 1, 0)
    c = c + w_ref[0:1, :] * pltpu.roll(ext, 2, 0)
    return c[CONV_HALO:]


def _ff_specs(T, tm):
    pair = pl.BlockSpec((tm, 2 * FF_TILE), lambda i, j: (i, j))
    prev = pl.BlockSpec((CONV_HALO, 2 * FF_TILE), lambda i, j: (jnp.maximum(i * (tm // CONV_HALO) - 1, 0), j))
    nxt = pl.BlockSpec((CONV_HALO, 2 * FF_TILE),
                       lambda i, j: (jnp.minimum((i + 1) * (tm // CONV_HALO), T // CONV_HALO - 1), j))
    half = pl.BlockSpec((tm, FF_TILE), lambda i, j: (i, j))
    small = lambda r: pl.BlockSpec((r, 2 * FF_TILE), lambda i, j: (0, j))
    return pair, prev, nxt, half, small


def _ff_row_chunks(u_ref, halo, tm, chunk):
    chunk(0, jnp.concatenate([halo, u_ref[0:FF_ROWS, :]], axis=0).astype(F32))

    @pl.loop(1, tm // FF_ROWS)
    def _(c):
        r0 = pl.multiple_of(c * FF_ROWS, FF_ROWS)
        start = pl.multiple_of(r0 - CONV_HALO, CONV_HALO)
        chunk(r0, u_ref[pl.ds(start, FF_ROWS + CONV_HALO), :].astype(F32))


def _swap_grid(spec):
    return pl.BlockSpec(spec.block_shape, lambda j, i, f=spec.index_map: f(i, j))


def _ff_act_fwd(u, conv_w, conv_b, tm=1024):
    T = u.shape[0]
    pair, prev, _, half, small = _ff_specs(T, tm)

    def body(u_ref, p_ref, w_ref, b_ref, a_ref):
        i = pl.program_id(0)
        ext = jnp.concatenate([jnp.where(i > 0, p_ref[...], 0.0), u_ref[...]], axis=0).astype(F32)
        c = _conv(ext, w_ref, b_ref)
        cg, cv = c[:, :FF_TILE], c[:, FF_TILE:]
        a_ref[...] = (cg * jax.nn.sigmoid(cg) * cv).astype(BF16)

    return pl.pallas_call(
        body, name="ff_act_fwd", grid=(T // tm, FF_TILES),
        in_specs=[pair, prev, small(3), small(1)],
        out_specs=half,
        out_shape=jax.ShapeDtypeStruct((T, D_FF), BF16),
        compiler_params=_params(("parallel", "parallel")),
    )(u, u, conv_w, conv_b.reshape(1, -1))


def _ff_act_bwd(u, da, conv_w, conv_b, tm=1024):
    T = u.shape[0]
    pair, prev, _, half, small = _ff_specs(T, tm)

    def body(u_ref, p_ref, da_ref, w_ref, b_ref, dc_ref, dw_ref, db_ref, sums_ref):
        i = pl.program_id(1)

        @pl.when(i == 0)
        def _():
            dw_ref[...] = jnp.zeros_like(dw_ref)
            db_ref[...] = jnp.zeros_like(db_ref)

        sums_ref[...] = jnp.zeros_like(sums_ref)

        def fold(x):
            return jnp.sum(x.reshape(FF_ROWS // 8, 8, x.shape[-1]), axis=0)

        def chunk(r0, ext):
            c = _conv(ext, w_ref, b_ref)
            cg, cv = c[:, :FF_TILE], c[:, FF_TILE:]
            da = da_ref[pl.ds(r0, FF_ROWS), :].astype(F32)
            sg = jax.nn.sigmoid(cg)
            dc = jnp.concatenate([da * cv * sg * (1.0 + cg * (1.0 - sg)), da * cg * sg], axis=1)
            dc_ref[pl.ds(r0, FF_ROWS), :] = dc.astype(BF16)
            sums_ref[3] += fold(dc)
            sums_ref[2] += fold(dc * ext[CONV_HALO:])
            sums_ref[1] += fold(dc * pltpu.roll(ext, 1, 0)[CONV_HALO:])
            sums_ref[0] += fold(dc * pltpu.roll(ext, 2, 0)[CONV_HALO:])

        _ff_row_chunks(u_ref, jnp.where(i > 0, p_ref[...], 0.0), tm, chunk)
        for tap in range(3):
            dw_ref[tap:tap + 1, :] += jnp.sum(sums_ref[tap], axis=0, keepdims=True)
        db_ref[...] += jnp.sum(sums_ref[3], axis=0, keepdims=True)

    return pl.pallas_call(
        body, name="ff_act_bwd", grid=(FF_TILES, T // tm),
        in_specs=[_swap_grid(pair), _swap_grid(prev), _swap_grid(half), _swap_grid(small(3)), _swap_grid(small(1))],
        out_specs=[_swap_grid(pair), _swap_grid(small(3)), _swap_grid(small(1))],
        out_shape=[jax.ShapeDtypeStruct((T, 2 * D_FF), BF16), jax.ShapeDtypeStruct((3, 2 * D_FF), F32),
                   jax.ShapeDtypeStruct((1, 2 * D_FF), F32)],
        scratch_shapes=[pltpu.VMEM((4, 8, 2 * FF_TILE), F32)],
        compiler_params=_params(("parallel", "arbitrary")),
    )(u, u, da, conv_w, conv_b.reshape(1, -1))


def _ff_conv_bwd(dc, conv_w, tm=1024):
    T = dc.shape[0]
    nt = T // tm
    pair, _, nxt, _, small = _ff_specs(T, tm)
    halo = CONV_HALO

    def body(dc_ref, n_ref, w_ref, du_ref):
        i = pl.program_id(0)
        following = jnp.where(i < nt - 1, n_ref[...].astype(F32), 0.0)
        ext = jnp.concatenate([dc_ref[...].astype(F32), following], axis=0)
        n = tm + halo
        du = w_ref[2:3, :] * ext + w_ref[1:2, :] * pltpu.roll(ext, n - 1, 0) + w_ref[0:1, :] * pltpu.roll(ext, n - 2, 0)
        du_ref[...] = du[:tm].astype(BF16)

    return pl.pallas_call(
        body, name="ff_conv_bwd", grid=(nt, FF_TILES),
        in_specs=[pair, nxt, small(3)],
        out_specs=pair,
        out_shape=jax.ShapeDtypeStruct((T, 2 * D_FF), BF16),
        compiler_params=_params(("parallel", "parallel")),
    )(dc, dc, conv_w)


def _loss_head(y, target, tm=512):
    T, C = y.shape
    nt = T // tm

    def body(y_ref, t_ref, dy_ref, dyb_ref, l_ref):
        err = y_ref[...] - t_ref[...]
        dy = err * (1.0 / C)
        dy_ref[...] = dy
        dyb_ref[...] = dy.astype(BF16)
        part = jnp.sum(err * err, axis=0, keepdims=True) * (0.5 / C)
        l_ref[0] = jnp.broadcast_to(part, (8, C))

    spec = pl.BlockSpec((tm, C), lambda i: (i, 0))
    dy, dyb, parts = pl.pallas_call(
        body, name="loss_head", grid=(nt,),
        in_specs=[spec, spec],
        out_specs=[spec, spec, pl.BlockSpec((1, 8, C), lambda i: (i, 0, 0))],
        out_shape=[jax.ShapeDtypeStruct((T, C), F32), jax.ShapeDtypeStruct((T, C), BF16),
                   jax.ShapeDtypeStruct((nt, 8, C), F32)],
        compiler_params=_params(("parallel",)),
    )(y, target)
    return dy, dyb, jnp.sum(parts[:, 0, :])


def _adamw_math(w, g, m, v):
    m = ADAM_B1 * m + (1.0 - ADAM_B1) * g
    v = ADAM_B2 * v + (1.0 - ADAM_B2) * (g * g)
    m_hat = m / (1.0 - ADAM_B1 ** ADAM_STEP)
    v_hat = v / (1.0 - ADAM_B2 ** ADAM_STEP)
    delta = -ADAM_LR * (m_hat / (jnp.sqrt(v_hat) + ADAM_EPS) + ADAM_WD * w)
    return delta, m, v


def _adamw(parts, w, m, v, name, tm=256):
    R, C = w.shape
    tm = _pick_rows(R, tm)

    def body(p_ref, w_ref, m_ref, v_ref, g_ref, d_ref, nm_ref, nv_ref):
        g = p_ref[0].astype(F32)
        for s in range(1, N_DEV):
            g = g + p_ref[s].astype(F32)
        delta, nm, nv = _adamw_math(w_ref[...], g, m_ref[...], v_ref[...])
        g_ref[...] = g
        d_ref[...] = delta
        nm_ref[...] = nm
        nv_ref[...] = nv

    spec = pl.BlockSpec((tm, C), lambda i: (i, 0))
    out = jax.ShapeDtypeStruct((R, C), F32)
    return pl.pallas_call(
        body, name=name, grid=(R // tm,),
        in_specs=[pl.BlockSpec((N_DEV, tm, C), lambda i: (0, i, 0)), spec, spec, spec],
        out_specs=[spec] * 4,
        out_shape=[out] * 4,
        compiler_params=_params(("parallel",)),
    )(parts, w, m, v)


def _pick_rows(n, cap):
    if n < 16:
        return n
    best = None
    for t in range(16, min(n, cap) + 1, 16):
        if n % t == 0:
            best = t
    assert best is not None, (n, cap)
    return best


def _exchange(srcs, scatter, name):
    n = len(srcs)

    def body(*refs):
        src_refs, out_refs = refs[:n], refs[n:2 * n]
        send_sems, recv_sems, local_sems = refs[2 * n:]
        x, y, c = lax.axis_index("x"), lax.axis_index("y"), lax.axis_index("c")
        me = 4 * x + 2 * y + c

        def piece(a, d):
            return src_refs[a].at[d] if scatter else src_refs[a]

        local = [pltpu.make_async_copy(piece(a, me), out_refs[a].at[me], local_sems.at[a]) for a in range(n)]
        for cp in local:
            cp.start()
        copies = []
        for k in range(1, N_DEV):
            where, peer = _peer(k)
            for a in range(n):
                cp = pltpu.make_async_remote_copy(
                    src_ref=piece(a, peer), dst_ref=out_refs[a].at[me],
                    send_sem=send_sems.at[a * N_DEV + k], recv_sem=recv_sems.at[a * N_DEV + k],
                    device_id=where, device_id_type=MESH)
                cp.start()
                copies.append((cp, a, k, peer))
        for cp, a, k, peer in copies:
            cp.wait_send()
            pltpu.make_async_remote_copy(
                src_ref=piece(a, peer), dst_ref=out_refs[a].at[peer],
                send_sem=send_sems.at[a * N_DEV + k], recv_sem=recv_sems.at[a * N_DEV + k],
                device_id=(x, y, c), device_id_type=MESH).wait_recv()
        for cp in local:
            cp.wait()

    slab = lambda s: tuple(s.shape[1:] if scatter else s.shape)
    return pl.pallas_call(
        body, name=name,
        in_specs=[pl.BlockSpec(memory_space=pl.ANY)] * n,
        out_specs=[pl.BlockSpec(memory_space=pl.ANY)] * n,
        out_shape=[jax.ShapeDtypeStruct((N_DEV,) + slab(s), s.dtype) for s in srcs],
        scratch_shapes=[pltpu.SemaphoreType.DMA((n * N_DEV,)), pltpu.SemaphoreType.DMA((n * N_DEV,)),
                        pltpu.SemaphoreType.DMA((n,))],
    )(*srcs)


def _peer(k):
    x, y, c = lax.axis_index("x"), lax.axis_index("y"), lax.axis_index("c")
    px = 1 - x if k & 4 else x
    py = 1 - y if k & 2 else y
    pc = 1 - c if k & 1 else c
    return (px, py, pc), 4 * px + 2 * py + pc


def _split_copies(src_refs, land_refs, send_sems, recv_sems, scatter):
    x, y, c = lax.axis_index("x"), lax.axis_index("y"), lax.axis_index("c")
    me = 4 * x + 2 * y + c
    sends, arrivals = [], []
    for k in range(1, N_DEV):
        where, peer = _peer(k)
        for a, (src, land) in enumerate(zip(src_refs, land_refs)):
            piece = src.at[peer] if scatter else src
            sends.append(pltpu.make_async_remote_copy(
                src_ref=piece, dst_ref=land.at[me], send_sem=send_sems.at[a * N_DEV + k], recv_sem=recv_sems.at[a * N_DEV + k],
                device_id=where, device_id_type=MESH))
            arrivals.append(pltpu.make_async_remote_copy(
                src_ref=piece, dst_ref=land.at[peer], send_sem=send_sems.at[a * N_DEV + k], recv_sem=recv_sems.at[a * N_DEV + k],
                device_id=(x, y, c), device_id_type=MESH))
    return sends, arrivals


def _exchange_start(srcs, scatter, name):
    n = len(srcs)
    slab = lambda s: tuple(s.shape[1:] if scatter else s.shape)
    lands = [lax.empty((N_DEV,) + slab(s), s.dtype) for s in srcs]

    def body(*refs):
        src_refs, land_refs = refs[:n], refs[n:2 * n]
        send_sems, recv_sems = refs[2 * n], refs[2 * n + 1]
        token = refs[-1]
        sends, _ = _split_copies(src_refs, land_refs, send_sems, recv_sems, scatter)
        for cp in sends:
            cp.start()
        token[...] = jnp.zeros_like(token)

    hbm = pl.BlockSpec(memory_space=pltpu.HBM)
    sem = pl.BlockSpec(memory_space=pltpu.SEMAPHORE)
    out = pl.pallas_call(
        body, name=name,
        in_specs=[hbm] * (2 * n),
        out_specs=[sem, sem] + [hbm] * (2 * n) + [pl.BlockSpec(memory_space=pltpu.VMEM)],
        out_shape=[pltpu.SemaphoreType.DMA((n * N_DEV,)), pltpu.SemaphoreType.DMA((n * N_DEV,))]
        + [pltpu.HBM(s.shape, s.dtype) for s in srcs] + [pltpu.HBM(l.shape, l.dtype) for l in lands]
        + [jax.ShapeDtypeStruct((8, LANES), F32)],
        input_output_aliases={j: 2 + j for j in range(2 * n)},
        compiler_params=pltpu.CompilerParams(has_side_effects=pltpu.SideEffectType.DATAFLOW_SIDE_EFFECTING),
    )(*[pltpu.with_memory_space_constraint(s, pltpu.HBM) for s in srcs],
      *[pltpu.with_memory_space_constraint(l, pltpu.HBM) for l in lands])
    return (out[0], out[1], out[2:2 + n], out[2 + n:2 + 2 * n]), out[-1]


def _exchange_finish(state, scatter, after, name):
    send_sems, recv_sems, srcs, lands = state
    n = len(srcs)

    def body(*refs):
        src_refs, land_refs = refs[:n], refs[n:2 * n]
        sends, arrivals = _split_copies(src_refs, land_refs, refs[2 * n], refs[2 * n + 1], scatter)
        for cp in sends:
            cp.wait_send()
        for cp in arrivals:
            cp.wait_recv()

    hbm = pl.BlockSpec(memory_space=pltpu.HBM)
    sem = pl.BlockSpec(memory_space=pltpu.SEMAPHORE)
    out = pl.pallas_call(
        body, name=name,
        in_specs=[hbm] * (2 * n) + [sem, sem, pl.BlockSpec(memory_space=pl.ANY)],
        out_specs=[hbm] * (2 * n),
        out_shape=[pltpu.HBM(s.shape, s.dtype) for s in srcs] + [pltpu.HBM(l.shape, l.dtype) for l in lands],
        input_output_aliases={j: j for j in range(2 * n)},
        compiler_params=pltpu.CompilerParams(has_side_effects=pltpu.SideEffectType.DATAFLOW_SIDE_EFFECTING),
    )(*srcs, *lands, send_sems, recv_sems, after)
    return list(out[n:])


def _own_slab(landed, src, scatter):
    me = 4 * lax.axis_index("x") + 2 * lax.axis_index("y") + lax.axis_index("c")
    own = lax.dynamic_index_in_dim(src, me, axis=0, keepdims=True) if scatter else src[None]
    return lax.dynamic_update_slice_in_dim(landed, own, me, axis=0)


SHARDED = ("w_in", "w_branch_a", "w_branch_b", "w_branch_c", "w_out", "w_up", "w_down")
REPLICATED = ("norm_mix", "b_gate", "q_norm_a", "k_norm_a", "rel_bias_a", "w_pool", "pool_scale", "norm_ffn", "conv_b")
WEIGHTS = ("norm_mix", "w_in", "b_gate", "q_norm_a", "k_norm_a", "rel_bias_a", "w_pool", "pool_scale",
           "w_branch_a", "w_branch_b", "w_branch_c", "w_out", "norm_ffn", "w_up", "conv_w", "conv_b", "w_down")
SMALL_COLS = 128
QKV_COLS = 6 * WIDTH


def _rel_index():
    q_off = jnp.arange(CHUNK)[:, None] + N_LEFT * CHUNK
    k_off = jnp.arange(BAND)[None, :]
    return jnp.clip(q_off - k_off, -(CHUNK - 1), MAX_REL) + (CHUNK - 1)


def _rel_onehot():
    rel = _rel_index().reshape(1, CHUNK * BAND)
    return (rel == jnp.arange(REL_TABLE)[:, None]).astype(BF16)


def _select_mm(x, onehot, mode, name):
    hi = x.astype(BF16)
    r1 = x - hi.astype(F32)
    mid = r1.astype(BF16)
    lo = (r1 - mid.astype(F32)).astype(BF16)
    y = _mm(jnp.concatenate([hi, mid, lo, jnp.zeros_like(hi)], axis=0), onehot, mode, F32, name)
    n = x.shape[0]
    return y[:n] + y[n:2 * n] + y[2 * n:3 * n]


def _pack_rows(arrays, cols, row_multiple):
    flat = jnp.concatenate([a.reshape(-1) for a in arrays])
    rows = -(-flat.shape[0] // cols)
    rows = -(-rows // row_multiple) * row_multiple
    return jnp.pad(flat, (0, rows * cols - flat.shape[0])).reshape(rows, cols)


def _unpack_rows(packed, like):
    flat = packed.reshape(-1)
    out, off = [], 0
    for a in like:
        out.append(flat[off:off + a.size].reshape(a.shape))
        off += a.size
    return out


def kernel(x, norm_mix, w_in, b_gate, q_norm_a, k_norm_a, rel_bias_a, w_pool, pool_scale, w_branch_a, w_branch_b, w_branch_c, w_out, norm_ffn, w_up, conv_w, conv_b, w_down, loss_target, m_norm_mix, m_w_in, m_b_gate, m_q_norm_a, m_k_norm_a, m_rel_bias_a, m_w_pool, m_pool_scale, m_w_branch_a, m_w_branch_b, m_w_branch_c, m_w_out, m_norm_ffn, m_w_up, m_conv_w, m_conv_b, m_w_down, v_norm_mix, v_w_in, v_b_gate, v_q_norm_a, v_k_norm_a, v_rel_bias_a, v_w_pool, v_pool_scale, v_w_branch_a, v_w_branch_b, v_w_branch_c, v_w_out, v_norm_ffn, v_w_up, v_conv_w, v_conv_b, v_w_down):
    args = dict(locals())
    w = {n: args[n] for n in WEIGHTS}
    m = {n: args["m_" + n] for n in WEIGHTS}
    v = {n: args["v_" + n] for n in WEIGHTS}
    L = w_in.shape[0]
    T = x.shape[1]
    xs = x.reshape(T, D_MODEL)
    target = loss_target.reshape(T, D_MODEL)

    exchanged = SHARDED + ("conv_w",)
    row_sharded = ("w_out", "w_down")
    shard = {(n, l): (w[n][l] if n == "conv_w" else w[n][l].astype(BF16)) for n in exchanged for l in range(L)}
    late = [key for key in shard if key != ("w_in", 0)]
    gathered = {("w_in", 0): _exchange([shard["w_in", 0]], False, "gather_first")[0]}
    gather_state, gather_token = _exchange_start([shard[key] for key in late], False, "gather_rest_start")

    def full_weight(key):
        g = gathered[key]
        return g.reshape(-1, g.shape[-1]) if key[0] in row_sharded else g.transpose(1, 0, 2).reshape(g.shape[1], -1)

    w_in_f = {0: full_weight(("w_in", 0))}
    conv_b_f = _ff_pair_order(conv_b)
    onehot = _rel_onehot()

    saved = []
    cur = xs
    full = {}
    for l in range(L):
        w_qkv, w_uc, w_g = w_in_f[l][:, :QKV_COLS], w_in_f[l][:, QKV_COLS:QKV_COLS + WIDTH], w_in_f[l][:, QKV_COLS + WIDTH:]
        gain = norm_mix[l] + gather_token[0, 0] if l == 0 else norm_mix[l]
        h = _rmsnorm_fwd(cur, gain, "norm_mix_fwd")
        qkv = _mm(h, w_qkv, "nn", BF16, "proj_qkv")
        uc = _mm(h, w_uc, "nn", F32, "proj_pool")
        glog = _mm(h, w_g, "nn", F32, "proj_gate")
        table = _band_table(_select_mm(rel_bias_a[l], onehot, "nn", "rel_bias_table").reshape(N_HEADS, CHUNK, BAND))
        oa = _attn_a_fwd(qkv, table, q_norm_a[l], k_norm_a[l])
        ob, carries = _attn_b_fwd(qkv)
        oc = _pool_fwd(uc, w_pool[l], pool_scale[l])
        if l == 0:
            landed = _exchange_finish(gather_state, False, ob, "gather_rest_finish")
            gathered.update({key: _own_slab(g, shard[key], False) for key, g in zip(late, landed)})
            full = {key: full_weight(key) for key in gathered}
            w_in_f.update({k: full["w_in", k] for k in range(1, L)})
        w_a, w_b, w_c = (full["w_branch_" + tag, l] for tag in "abc")
        w_out_f, w_down_f = full["w_out", l], full["w_down", l]
        w_up_f, conv_w_f = _ff_pair_order(full["w_up", l]), _ff_pair_order(full["conv_w", l])
        merged, ya, yb, yc = _merge_fwd(oa, ob, oc, glog, b_gate[l], w_a, w_b, w_c)
        x1 = _mm(merged, w_out_f, "nn", F32, "out_proj", res=cur)
        h2 = _rmsnorm_fwd(x1, norm_ffn[l], "norm_ffn_fwd")
        u = _mm(h2, w_up_f, "nn", BF16, "ff_up")
        act = _ff_act_fwd(u, conv_w_f, conv_b_f[l])
        x2 = _mm(act, w_down_f, "nn", F32, "ff_down", res=x1)
        saved.append(dict(x=cur, h=h, qkv=qkv, carries=carries, uc=uc, glog=glog, table=table, oa=oa, ob=ob, oc=oc,
                          ya=ya, yb=yb, yc=yc, merged=merged, x1=x1, h2=h2, u=u, act=act, w_qkv=w_qkv, w_uc=w_uc,
                          w_g=w_g, w_a=w_a, w_b=w_b, w_c=w_c, w_out=w_out_f, w_up=w_up_f, w_down=w_down_f,
                          conv_w=conv_w_f))
        cur = x2

    dcur, dcur_b, loss_local = _loss_head(cur, target)
    loss = lax.psum(loss_local, ("x", "y", "c"))

    def pieces_of(n, g):
        if n in row_sharded:
            return g.reshape(N_DEV, -1, g.shape[-1])
        return g.reshape(g.shape[0], N_DEV, -1).transpose(1, 0, 2)

    gw = {n: [None] * L for n in WEIGHTS}
    for l in reversed(range(L)):
        s = saved[l]
        da = _mm(dcur_b, s["w_down"], "nt", BF16, "ff_down_dx", tn_cap=1408)
        gw["w_down"][l] = _mm(s["act"], dcur_b, "tn", BF16, "ff_down_dw")
        dc, dconv_w, dconv_b = _ff_act_bwd(s["u"], da, s["conv_w"], conv_b_f[l])
        du = _ff_conv_bwd(dc, s["conv_w"])
        dh2 = _mm(du, s["w_up"], "nt", F32, "ff_up_dx")
        gw["w_up"][l] = _ff_natural_order(_mm(s["h2"], du, "tn", BF16, "ff_up_dw"))
        gw["conv_w"][l] = _ff_natural_order(dconv_w)
        gw["conv_b"][l] = _ff_natural_order(dconv_b)[0]
        dx1, dx1_b, dg = _rmsnorm_bwd(s["x1"], norm_ffn[l], dh2, dcur, "norm_ffn_bwd")
        gw["norm_ffn"][l] = dg[0]

        dmerged = _mm(dx1_b, s["w_out"], "nt", F32, "out_proj_dx")
        gw["w_out"][l] = _mm(s["merged"], dx1_b, "tn", BF16, "out_proj_dw")
        dya, dyb, dyc, dglog, db_gate = _merge_bwd(dmerged, s["glog"], b_gate[l], s["ya"], s["yb"], s["yc"])
        gw["b_gate"][l] = db_gate[0]
        do = {}
        for tag, dy, ok in (("a", dya, s["oa"]), ("b", dyb, s["ob"]), ("c", dyc, s["oc"])):
            do[tag] = _mm(dy, s["w_" + tag], "nt", BF16, "branch_dx_" + tag)
            gw["w_branch_" + tag][l] = _mm(ok, dy, "tn", BF16, "branch_dw_" + tag)
        duc, dw_pool, dscale = _pool_bwd(s["uc"], do["c"], w_pool[l], pool_scale[l])
        gw["w_pool"][l] = dw_pool
        gw["pool_scale"][l] = dscale[0]
        gain_q = q_norm_a[l]
        if l == 0:
            early = [(n, k) for n in exchanged for k in range(L) if (n, k) != ("w_in", 0)]
            early_pieces = [pieces_of(n, gw[n][k]) for n, k in early]
            grads_state, grads_token = _exchange_start(early_pieces, True, "exchange_early_start")
            gain_q = gain_q + grads_token[0, 0]
        dqa, dkc, dkp, dvc, dvp, dtable, dgq = _attn_a_bwd(s["qkv"], do["a"], s["table"], gain_q, k_norm_a[l])
        dka, dva, dgk = _attn_a_bwd_keys(s["qkv"], dkc, dkp, dvc, dvp, k_norm_a[l])
        gw["q_norm_a"][l] = jnp.sum(dgq.reshape(N_HEADS, HEAD_DIM), axis=0)
        gw["k_norm_a"][l] = jnp.sum(dgk.reshape(N_HEADS, HEAD_DIM), axis=0)
        gw["rel_bias_a"][l] = _select_mm(_band_table_bwd(dtable).reshape(N_HEADS, CHUNK * BAND), onehot, "nt",
                                         "rel_bias_table_dw")
        dqb, dkb, dvb = _attn_b_bwd(s["qkv"], s["carries"], do["b"])
        dqkv = jnp.concatenate([dqa, dka, dva, dqb, dkb.T.astype(BF16), dvb.T.astype(BF16)], axis=1)
        dh = _mm(dqkv, s["w_qkv"], "nt", F32, "proj_qkv_dx")
        dh = _mm(duc, s["w_uc"], "nt", F32, "proj_pool_dx", res=dh)
        dh = _mm(dglog, s["w_g"], "nt", F32, "proj_gate_dx", res=dh)
        gw["w_in"][l] = jnp.concatenate([_mm(s["h"], dqkv, "tn", BF16, "proj_qkv_dw"),
                                         _mm(s["h"], duc, "tn", BF16, "proj_pool_dw"),
                                         _mm(s["h"], dglog, "tn", BF16, "proj_gate_dw")], axis=1)
        dcur, dcur_b, dg = _rmsnorm_bwd(s["x"], norm_mix[l], dh, dx1, "norm_mix_bwd")
        gw["norm_mix"][l] = dg[0]

    landed = _exchange_finish(grads_state, True, dcur, "exchange_early_finish")
    parts = {key: _own_slab(g, src, True) for key, g, src in zip(early, landed, early_pieces)}
    parts["w_in", 0] = _exchange([pieces_of("w_in", gw["w_in"][0])], True, "exchange_last")[0]
    small = _pack_rows([jnp.stack(gw[n]) for n in REPLICATED], SMALL_COLS, 16)
    small_parts = _exchange([small], False, "gather_small_grads")[0]

    out = {}
    for n in exchanged:
        res = [_adamw(parts[n, l], w[n][l], m[n][l], v[n][l], "adamw_" + n) for l in range(L)]
        out[n] = tuple(jnp.stack(r) for r in zip(*res))
    rep_like = [w[n] for n in REPLICATED]
    res = _adamw(small_parts, *[_pack_rows([d[n] for n in REPLICATED], SMALL_COLS, 16) for d in (w, m, v)],
                 "adamw_replicated")
    out.update({n: r for n, r in zip(REPLICATED, zip(*[_unpack_rows(r, rep_like) for r in res]))})

    grads, deltas, new_m, new_v = ([out[n][i] for n in WEIGHTS] for i in range(4))
    return (loss, dcur.reshape(x.shape), *grads, *deltas, *new_m, *new_v)
```
